```python
import jax, jax.numpy as jnp
from jax import lax
import numpy as np

D_MODEL = 1024
BATCH = 32
SEQ = 2048
DEPTH = 2

CHUNK = 64
D_MIX = D_MODEL
D_CONV = D_MIX // 4
CONV_WIDTH = 3
D_FOX = D_MIX // 2
FOX_HEAD_DIM = 64
N_FOX_HEADS = D_FOX // FOX_HEAD_DIM
FOX_BLOCK = 128
D_SGU = D_MIX - D_CONV - D_FOX
N_SGU_GROUPS = 4
SGU_GROUP_DIM = D_SGU // N_SGU_GROUPS
SGU_CHUNK = 128
D_FF = 2816
ALPHA = (2 * DEPTH) ** 0.25
BETA = (8 * DEPTH) ** -0.25
LN_EPS = 1e-5
SPLIT_SIZES = (D_CONV, D_CONV, D_CONV, D_FOX, D_FOX, D_FOX, N_FOX_HEADS, D_SGU, D_SGU)
D_IN = 3 * D_CONV + 3 * D_FOX + N_FOX_HEADS + 2 * D_SGU

kernel_name = "hybrid_conv_fox_sgu_deepnorm_macaron"


def layer_norm(x, g, b):
    xf = x.astype(jnp.float32)
    mu = jnp.mean(xf, axis=-1, keepdims=True)
    xc = xf - mu
    var = jnp.mean(xc * xc, axis=-1, keepdims=True)
    y = xc * lax.rsqrt(var + LN_EPS) * g.astype(jnp.float32) + b.astype(jnp.float32)
    return y.astype(x.dtype)


def swiglu(x, w_up, w_down):
    gate, up = jnp.split(x @ w_up, 2, axis=-1)
    return (jax.nn.silu(gate) * up) @ w_down


def short_conv(gate_b, gate_c, h, w_conv):
    z = gate_c * h
    y = lax.conv_general_dilated(
        z, w_conv[:, None, :].astype(z.dtype),
        window_strides=(1,), padding=[(CONV_WIDTH - 1, 0)],
        dimension_numbers=("NWC", "WIO", "NWC"),
        feature_group_count=D_CONV)
    return gate_b * y


def forgetting_attention(q, k, v, f_logit):
    seq = q.shape[1]
    scale = FOX_HEAD_DIM ** -0.5
    log_f = jax.nn.log_sigmoid(f_logit.astype(jnp.float32))
    cum = jnp.transpose(jnp.cumsum(log_f, axis=1), (0, 2, 1))
    qf = q.astype(jnp.float32) * scale
    kf = k.astype(jnp.float32)
    vf = v.astype(jnp.float32)
    outs = []
    for i in range(seq // FOX_BLOCK):
        lo, hi = i * FOX_BLOCK, (i + 1) * FOX_BLOCK
        s = jnp.einsum('bqhd,bkhd->bhqk', qf[:, lo:hi], kf[:, :hi])
        s = s + cum[:, :, lo:hi, None] - cum[:, :, None, :hi]
        mask = jnp.arange(hi)[None, :] <= jnp.arange(lo, hi)[:, None]
        p = jax.nn.softmax(jnp.where(mask, s, -jnp.inf), axis=-1)
        outs.append(jnp.einsum('bhqk,bkhd->bqhd', p, vf[:, :hi]))
    return jnp.concatenate(outs, axis=1).astype(v.dtype)


def spatial_gating(u, v, ln_g, ln_b, w_s, b_s):
    bsz, seq, _ = v.shape
    u = jax.nn.gelu(u)
    v = layer_norm(jax.nn.gelu(v), ln_g, ln_b)
    vg = v.reshape(bsz, seq // SGU_CHUNK, SGU_CHUNK, N_SGU_GROUPS, SGU_GROUP_DIM)
    causal = jnp.tril(jnp.ones((SGU_CHUNK, SGU_CHUNK), dtype=w_s.dtype))
    mixed = jnp.einsum('gts,bnsgc->bntgc', w_s * causal, vg) + jnp.transpose(b_s)[:, :, None]
    return u * mixed.reshape(bsz, seq, D_SGU)


def hybrid_mixer(x, w_in, b_f, w_conv, sgu_ln_g, sgu_ln_b, w_s, b_s, w_out):
    bsz, seq, _ = x.shape
    offsets = []
    acc = 0
    for n in SPLIT_SIZES[:-1]:
        acc += n
        offsets.append(acc)
    proj = x @ w_in
    cb, cc, ch, q, k, v, f_logit, su, sv = jnp.split(proj, offsets, axis=-1)
    y_a = short_conv(cb, cc, ch, w_conv)
    heads = (bsz, seq, N_FOX_HEADS, FOX_HEAD_DIM)
    y_b = forgetting_attention(q.reshape(heads), k.reshape(heads), v.reshape(heads),
                               f_logit + b_f).reshape(bsz, seq, D_FOX)
    y_c = spatial_gating(su, sv, sgu_ln_g, sgu_ln_b, w_s, b_s)
    return jnp.concatenate([y_a, y_b, y_c], axis=-1) @ w_out


def _fwd_setup_inputs(seed: int = 0) -> dict:
    key = jax.random.key(seed)
    ks = jax.random.split(key, 20)
    nrm = lambda k, shape, s: jax.random.normal(k, shape, jnp.float32) * s
    L, D = DEPTH, D_MODEL
    return {
        "x": nrm(ks[0], (BATCH, SEQ, D), 1.0),
        "ln1_g": 1.0 + nrm(ks[1], (L, D), 0.05),
        "ln1_b": nrm(ks[2], (L, D), 0.02),
        "ffn1_w_up": nrm(ks[3], (L, D, 2 * D_FF), D ** -0.5),
        "ffn1_w_down": nrm(ks[4], (L, D_FF, D), BETA * D_FF ** -0.5),
        "mix_w_in": nrm(ks[5], (L, D, D_IN), D ** -0.5),
        "fox_b_f": 4.0 + nrm(ks[6], (L, N_FOX_HEADS), 0.5),
        "conv_w": nrm(ks[7], (L, CONV_WIDTH, D_CONV), CONV_WIDTH ** -0.5),
        "sgu_ln_g": 1.0 + nrm(ks[8], (L, D_SGU), 0.05),
        "sgu_ln_b": nrm(ks[9], (L, D_SGU), 0.02),
        "sgu_w_s": nrm(ks[10], (L, N_SGU_GROUPS, SGU_CHUNK, SGU_CHUNK), SGU_CHUNK ** -0.5),
        "sgu_b_s": 1.0 + nrm(ks[11], (L, N_SGU_GROUPS, SGU_CHUNK), 0.02),
        "mix_w_out": nrm(ks[12], (L, D_MIX, D), BETA * D_MIX ** -0.5),
        "ln2_g": 1.0 + nrm(ks[13], (L, D), 0.05),
        "ln2_b": nrm(ks[14], (L, D), 0.02),
        "ffn2_w_up": nrm(ks[15], (L, D, 2 * D_FF), D ** -0.5),
        "ffn2_w_down": nrm(ks[16], (L, D_FF, D), BETA * D_FF ** -0.5),
        "ln3_g": 1.0 + nrm(ks[17], (L, D), 0.05),
        "ln3_b": nrm(ks[18], (L, D), 0.02),
    }


def _fwd_reference(x, ln1_g, ln1_b, ffn1_w_up, ffn1_w_down, mix_w_in, fox_b_f, conv_w,
              sgu_ln_g, sgu_ln_b, sgu_w_s, sgu_b_s, mix_w_out, ln2_g, ln2_b,
              ffn2_w_up, ffn2_w_down, ln3_g, ln3_b):
    for l in range(DEPTH):
        x = layer_norm(ALPHA * x + 0.5 * swiglu(x, ffn1_w_up[l], ffn1_w_down[l]), ln1_g[l], ln1_b[l])
        mix = hybrid_mixer(x, mix_w_in[l], fox_b_f[l], conv_w[l], sgu_ln_g[l], sgu_ln_b[l],
                           sgu_w_s[l], sgu_b_s[l], mix_w_out[l])
        x = layer_norm(ALPHA * x + mix, ln2_g[l], ln2_b[l])
        x = layer_norm(ALPHA * x + 0.5 * swiglu(x, ffn2_w_up[l], ffn2_w_down[l]), ln3_g[l], ln3_b[l])
    return x


import jax as _jax
import jax.numpy as _jnp

TWIN_FORMAT = 'train_step'
FWD_PARAMS = ['x', 'ln1_g', 'ln1_b', 'ffn1_w_up', 'ffn1_w_down', 'mix_w_in', 'fox_b_f', 'conv_w', 'sgu_ln_g', 'sgu_ln_b', 'sgu_w_s', 'sgu_b_s', 'mix_w_out', 'ln2_g', 'ln2_b', 'ffn2_w_up', 'ffn2_w_down', 'ln3_g', 'ln3_b']
TWIN_WEIGHTS = ['ln1_g', 'ln1_b', 'ffn1_w_up', 'ffn1_w_down', 'mix_w_in', 'fox_b_f', 'conv_w', 'sgu_ln_g', 'sgu_ln_b', 'sgu_w_s', 'sgu_b_s', 'mix_w_out', 'ln2_g', 'ln2_b', 'ffn2_w_up', 'ffn2_w_down', 'ln3_g', 'ln3_b']
TWIN_DIFF_INPUT = 'x'
TWIN_INPUTS = ['x', 'ln1_g', 'ln1_b', 'ffn1_w_up', 'ffn1_w_down', 'mix_w_in', 'fox_b_f', 'conv_w', 'sgu_ln_g', 'sgu_ln_b', 'sgu_w_s', 'sgu_b_s', 'mix_w_out', 'ln2_g', 'ln2_b', 'ffn2_w_up', 'ffn2_w_down', 'ln3_g', 'ln3_b', 'loss_target', 'm_ln1_g', 'm_ln1_b', 'm_ffn1_w_up', 'm_ffn1_w_down', 'm_mix_w_in', 'm_fox_b_f', 'm_conv_w', 'm_sgu_ln_g', 'm_sgu_ln_b', 'm_sgu_w_s', 'm_sgu_b_s', 'm_mix_w_out', 'm_ln2_g', 'm_ln2_b', 'm_ffn2_w_up', 'm_ffn2_w_down', 'm_ln3_g', 'm_ln3_b', 'v_ln1_g', 'v_ln1_b', 'v_ffn1_w_up', 'v_ffn1_w_down', 'v_mix_w_in', 'v_fox_b_f', 'v_conv_w', 'v_sgu_ln_g', 'v_sgu_ln_b', 'v_sgu_w_s', 'v_sgu_b_s', 'v_mix_w_out', 'v_ln2_g', 'v_ln2_b', 'v_ffn2_w_up', 'v_ffn2_w_down', 'v_ln3_g', 'v_ln3_b']
TWIN_OUTPUTS = ['loss', 'grad_x', 'grad_ln1_g', 'grad_ln1_b', 'grad_ffn1_w_up', 'grad_ffn1_w_down', 'grad_mix_w_in', 'grad_fox_b_f', 'grad_conv_w', 'grad_sgu_ln_g', 'grad_sgu_ln_b', 'grad_sgu_w_s', 'grad_sgu_b_s', 'grad_mix_w_out', 'grad_ln2_g', 'grad_ln2_b', 'grad_ffn2_w_up', 'grad_ffn2_w_down', 'grad_ln3_g', 'grad_ln3_b', 'delta_ln1_g', 'delta_ln1_b', 'delta_ffn1_w_up', 'delta_ffn1_w_down', 'delta_mix_w_in', 'delta_fox_b_f', 'delta_conv_w', 'delta_sgu_ln_g', 'delta_sgu_ln_b', 'delta_sgu_w_s', 'delta_sgu_b_s', 'delta_mix_w_out', 'delta_ln2_g', 'delta_ln2_b', 'delta_ffn2_w_up', 'delta_ffn2_w_down', 'delta_ln3_g', 'delta_ln3_b', 'new_m_ln1_g', 'new_m_ln1_b', 'new_m_ffn1_w_up', 'new_m_ffn1_w_down', 'new_m_mix_w_in', 'new_m_fox_b_f', 'new_m_conv_w', 'new_m_sgu_ln_g', 'new_m_sgu_ln_b', 'new_m_sgu_w_s', 'new_m_sgu_b_s', 'new_m_mix_w_out', 'new_m_ln2_g', 'new_m_ln2_b', 'new_m_ffn2_w_up', 'new_m_ffn2_w_down', 'new_m_ln3_g', 'new_m_ln3_b', 'new_v_ln1_g', 'new_v_ln1_b', 'new_v_ffn1_w_up', 'new_v_ffn1_w_down', 'new_v_mix_w_in', 'new_v_fox_b_f', 'new_v_conv_w', 'new_v_sgu_ln_g', 'new_v_sgu_ln_b', 'new_v_sgu_w_s', 'new_v_sgu_b_s', 'new_v_mix_w_out', 'new_v_ln2_g', 'new_v_ln2_b', 'new_v_ffn2_w_up', 'new_v_ffn2_w_down', 'new_v_ln3_g', 'new_v_ln3_b']
TWIN_LEAF_KINDS = {'loss': 'loss', 'grad_x': 'grad_x', 'grad_ln1_g': 'grad_w', 'grad_ln1_b': 'grad_w', 'grad_ffn1_w_up': 'grad_w', 'grad_ffn1_w_down': 'grad_w', 'grad_mix_w_in': 'grad_w', 'grad_fox_b_f': 'grad_w', 'grad_conv_w': 'grad_w', 'grad_sgu_ln_g': 'grad_w', 'grad_sgu_ln_b': 'grad_w', 'grad_sgu_w_s': 'grad_w', 'grad_sgu_b_s': 'grad_w', 'grad_mix_w_out': 'grad_w', 'grad_ln2_g': 'grad_w', 'grad_ln2_b': 'grad_w', 'grad_ffn2_w_up': 'grad_w', 'grad_ffn2_w_down': 'grad_w', 'grad_ln3_g': 'grad_w', 'grad_ln3_b': 'grad_w', 'delta_ln1_g': 'delta_w', 'delta_ln1_b': 'delta_w', 'delta_ffn1_w_up': 'delta_w', 'delta_ffn1_w_down': 'delta_w', 'delta_mix_w_in': 'delta_w', 'delta_fox_b_f': 'delta_w', 'delta_conv_w': 'delta_w', 'delta_sgu_ln_g': 'delta_w', 'delta_sgu_ln_b': 'delta_w', 'delta_sgu_w_s': 'delta_w', 'delta_sgu_b_s': 'delta_w', 'delta_mix_w_out': 'delta_w', 'delta_ln2_g': 'delta_w', 'delta_ln2_b': 'delta_w', 'delta_ffn2_w_up': 'delta_w', 'delta_ffn2_w_down': 'delta_w', 'delta_ln3_g': 'delta_w', 'delta_ln3_b': 'delta_w', 'new_m_ln1_g': 'new_m', 'new_m_ln1_b': 'new_m', 'new_m_ffn1_w_up': 'new_m', 'new_m_ffn1_w_down': 'new_m', 'new_m_mix_w_in': 'new_m', 'new_m_fox_b_f': 'new_m', 'new_m_conv_w': 'new_m', 'new_m_sgu_ln_g': 'new_m', 'new_m_sgu_ln_b': 'new_m', 'new_m_sgu_w_s': 'new_m', 'new_m_sgu_b_s': 'new_m', 'new_m_mix_w_out': 'new_m', 'new_m_ln2_g': 'new_m', 'new_m_ln2_b': 'new_m', 'new_m_ffn2_w_up': 'new_m', 'new_m_ffn2_w_down': 'new_m', 'new_m_ln3_g': 'new_m', 'new_m_ln3_b': 'new_m', 'new_v_ln1_g': 'new_v', 'new_v_ln1_b': 'new_v', 'new_v_ffn1_w_up': 'new_v', 'new_v_ffn1_w_down': 'new_v', 'new_v_mix_w_in': 'new_v', 'new_v_fox_b_f': 'new_v', 'new_v_conv_w': 'new_v', 'new_v_sgu_ln_g': 'new_v', 'new_v_sgu_ln_b': 'new_v', 'new_v_sgu_w_s': 'new_v', 'new_v_sgu_b_s': 'new_v', 'new_v_mix_w_out': 'new_v', 'new_v_ln2_g': 'new_v', 'new_v_ln2_b': 'new_v', 'new_v_ffn2_w_up': 'new_v', 'new_v_ffn2_w_down': 'new_v', 'new_v_ln3_g': 'new_v', 'new_v_ln3_b': 'new_v'}


def _forward(args):
    return _fwd_reference(*[args[k] for k in FWD_PARAMS])


def _output_shape():
    out = _jax.eval_shape(lambda: _forward(_fwd_setup_inputs(0)))
    return out.shape, out.dtype

N_MICROBATCH = 1
ADAM_LR = 0.001
ADAM_B1 = 0.9
ADAM_B2 = 0.999
ADAM_EPS = 1e-08
ADAM_WD = 0.01
ADAM_STEP = 10
PER_EXAMPLE_BATCH_AXIS = {'x': 0, 'loss_target': 0}
SHARED_INPUTS = []
_WEIGHT_DTYPES = {'ln1_g': _jnp.float32, 'ln1_b': _jnp.float32, 'ffn1_w_up': _jnp.float32, 'ffn1_w_down': _jnp.float32, 'mix_w_in': _jnp.float32, 'fox_b_f': _jnp.float32, 'conv_w': _jnp.float32, 'sgu_ln_g': _jnp.float32, 'sgu_ln_b': _jnp.float32, 'sgu_w_s': _jnp.float32, 'sgu_b_s': _jnp.float32, 'mix_w_out': _jnp.float32, 'ln2_g': _jnp.float32, 'ln2_b': _jnp.float32, 'ffn2_w_up': _jnp.float32, 'ffn2_w_down': _jnp.float32, 'ln3_g': _jnp.float32, 'ln3_b': _jnp.float32}
MOMENT_SCALE = {'ln1_g': 5.541732e+00, 'ln1_b': 8.531070e-01, 'ffn1_w_up': 1.730347e-02, 'ffn1_w_down': 5.647102e-02, 'mix_w_in': 5.518039e-02, 'fox_b_f': 2.802314e-01, 'conv_w': 9.919512e-02, 'sgu_ln_g': 4.331034e-02, 'sgu_ln_b': 4.996692e-02, 'sgu_w_s': 2.922382e-02, 'sgu_b_s': 4.348274e-02, 'mix_w_out': 1.285755e-01, 'ln2_g': 5.913801e+00, 'ln2_b': 8.479988e-01, 'ffn2_w_up': 1.671224e-02, 'ffn2_w_down': 5.461322e-02, 'ln3_g': 4.637063e+01, 'ln3_b': 3.339920e+00}


def _to_microbatches(a, axis):
    t = _jnp.moveaxis(a, axis, 0)
    t = t.reshape((N_MICROBATCH, t.shape[0] // N_MICROBATCH) + t.shape[1:])
    return _jnp.moveaxis(t, 1, axis + 1)


def setup_inputs(seed: int = 0) -> dict:
    inp = _fwd_setup_inputs(seed)
    key = _jax.random.fold_in(_jax.random.key(seed), 7919)
    shape, _ = _output_shape()
    out = dict(inp)
    out["loss_target"] = _jax.random.normal(_jax.random.fold_in(key, 0), shape, _jnp.float32)
    for i, name in enumerate(TWIN_WEIGHTS):
        w = inp[name].astype(_jnp.float32)
        if MOMENT_SCALE is None:
            s = _jnp.sqrt(_jnp.mean(_jnp.square(w)) + 1e-30)
        else:
            s = MOMENT_SCALE[name]
        km, kv = _jax.random.split(_jax.random.fold_in(key, i + 1))
        out[name] = w
        out["m_" + name] = s * _jax.random.normal(km, w.shape, _jnp.float32)
        out["v_" + name] = (s * s) * _jax.random.uniform(kv, w.shape, _jnp.float32, 0.5, 1.5)
    if N_MICROBATCH > 1:
        for name, axis in PER_EXAMPLE_BATCH_AXIS.items():
            out[name] = _to_microbatches(out[name], axis)
    return {'x': out['x'], 'ln1_g': out['ln1_g'], 'ln1_b': out['ln1_b'], 'ffn1_w_up': out['ffn1_w_up'], 'ffn1_w_down': out['ffn1_w_down'], 'mix_w_in': out['mix_w_in'], 'fox_b_f': out['fox_b_f'], 'conv_w': out['conv_w'], 'sgu_ln_g': out['sgu_ln_g'], 'sgu_ln_b': out['sgu_ln_b'], 'sgu_w_s': out['sgu_w_s'], 'sgu_b_s': out['sgu_b_s'], 'mix_w_out': out['mix_w_out'], 'ln2_g': out['ln2_g'], 'ln2_b': out['ln2_b'], 'ffn2_w_up': out['ffn2_w_up'], 'ffn2_w_down': out['ffn2_w_down'], 'ln3_g': out['ln3_g'], 'ln3_b': out['ln3_b'], 'loss_target': out['loss_target'], 'm_ln1_g': out['m_ln1_g'], 'm_ln1_b': out['m_ln1_b'], 'm_ffn1_w_up': out['m_ffn1_w_up'], 'm_ffn1_w_down': out['m_ffn1_w_down'], 'm_mix_w_in': out['m_mix_w_in'], 'm_fox_b_f': out['m_fox_b_f'], 'm_conv_w': out['m_conv_w'], 'm_sgu_ln_g': out['m_sgu_ln_g'], 'm_sgu_ln_b': out['m_sgu_ln_b'], 'm_sgu_w_s': out['m_sgu_w_s'], 'm_sgu_b_s': out['m_sgu_b_s'], 'm_mix_w_out': out['m_mix_w_out'], 'm_ln2_g': out['m_ln2_g'], 'm_ln2_b': out['m_ln2_b'], 'm_ffn2_w_up': out['m_ffn2_w_up'], 'm_ffn2_w_down': out['m_ffn2_w_down'], 'm_ln3_g': out['m_ln3_g'], 'm_ln3_b': out['m_ln3_b'], 'v_ln1_g': out['v_ln1_g'], 'v_ln1_b': out['v_ln1_b'], 'v_ffn1_w_up': out['v_ffn1_w_up'], 'v_ffn1_w_down': out['v_ffn1_w_down'], 'v_mix_w_in': out['v_mix_w_in'], 'v_fox_b_f': out['v_fox_b_f'], 'v_conv_w': out['v_conv_w'], 'v_sgu_ln_g': out['v_sgu_ln_g'], 'v_sgu_ln_b': out['v_sgu_ln_b'], 'v_sgu_w_s': out['v_sgu_w_s'], 'v_sgu_b_s': out['v_sgu_b_s'], 'v_mix_w_out': out['v_mix_w_out'], 'v_ln2_g': out['v_ln2_g'], 'v_ln2_b': out['v_ln2_b'], 'v_ffn2_w_up': out['v_ffn2_w_up'], 'v_ffn2_w_down': out['v_ffn2_w_down'], 'v_ln3_g': out['v_ln3_g'], 'v_ln3_b': out['v_ln3_b']}


def _loss(weights, diff, rest, loss_target):
    with _jax.named_scope("forward"):
        args = {**rest, TWIN_DIFF_INPUT: diff, **{k: w.astype(_WEIGHT_DTYPES[k]) for k, w in weights.items()}}
        y = _forward(args)
    with _jax.named_scope("loss_head"):
        err = _jnp.square(y.astype(_jnp.float32) - loss_target)
        return 0.5 * _jnp.sum(_jnp.mean(err, axis=-1)) if err.ndim else 0.5 * err


def _adamw(w, g, m, v):
    m = ADAM_B1 * m + (1.0 - ADAM_B1) * g
    v = ADAM_B2 * v + (1.0 - ADAM_B2) * _jnp.square(g)
    m_hat = m / (1.0 - ADAM_B1 ** ADAM_STEP)
    v_hat = v / (1.0 - ADAM_B2 ** ADAM_STEP)
    delta = -ADAM_LR * (m_hat / (_jnp.sqrt(v_hat) + ADAM_EPS) + ADAM_WD * w)
    return delta, m, v


def reference(x, ln1_g, ln1_b, ffn1_w_up, ffn1_w_down, mix_w_in, fox_b_f, conv_w, sgu_ln_g, sgu_ln_b, sgu_w_s, sgu_b_s, mix_w_out, ln2_g, ln2_b, ffn2_w_up, ffn2_w_down, ln3_g, ln3_b, loss_target, m_ln1_g, m_ln1_b, m_ffn1_w_up, m_ffn1_w_down, m_mix_w_in, m_fox_b_f, m_conv_w, m_sgu_ln_g, m_sgu_ln_b, m_sgu_w_s, m_sgu_b_s, m_mix_w_out, m_ln2_g, m_ln2_b, m_ffn2_w_up, m_ffn2_w_down, m_ln3_g, m_ln3_b, v_ln1_g, v_ln1_b, v_ffn1_w_up, v_ffn1_w_down, v_mix_w_in, v_fox_b_f, v_conv_w, v_sgu_ln_g, v_sgu_ln_b, v_sgu_w_s, v_sgu_b_s, v_mix_w_out, v_ln2_g, v_ln2_b, v_ffn2_w_up, v_ffn2_w_down, v_ln3_g, v_ln3_b):
    given = dict(x=x, ln1_g=ln1_g, ln1_b=ln1_b, ffn1_w_up=ffn1_w_up, ffn1_w_down=ffn1_w_down, mix_w_in=mix_w_in, fox_b_f=fox_b_f, conv_w=conv_w, sgu_ln_g=sgu_ln_g, sgu_ln_b=sgu_ln_b, sgu_w_s=sgu_w_s, sgu_b_s=sgu_b_s, mix_w_out=mix_w_out, ln2_g=ln2_g, ln2_b=ln2_b, ffn2_w_up=ffn2_w_up, ffn2_w_down=ffn2_w_down, ln3_g=ln3_g, ln3_b=ln3_b, loss_target=loss_target, m_ln1_g=m_ln1_g, m_ln1_b=m_ln1_b, m_ffn1_w_up=m_ffn1_w_up, m_ffn1_w_down=m_ffn1_w_down, m_mix_w_in=m_mix_w_in, m_fox_b_f=m_fox_b_f, m_conv_w=m_conv_w, m_sgu_ln_g=m_sgu_ln_g, m_sgu_ln_b=m_sgu_ln_b, m_sgu_w_s=m_sgu_w_s, m_sgu_b_s=m_sgu_b_s, m_mix_w_out=m_mix_w_out, m_ln2_g=m_ln2_g, m_ln2_b=m_ln2_b, m_ffn2_w_up=m_ffn2_w_up, m_ffn2_w_down=m_ffn2_w_down, m_ln3_g=m_ln3_g, m_ln3_b=m_ln3_b, v_ln1_g=v_ln1_g, v_ln1_b=v_ln1_b, v_ffn1_w_up=v_ffn1_w_up, v_ffn1_w_down=v_ffn1_w_down, v_mix_w_in=v_mix_w_in, v_fox_b_f=v_fox_b_f, v_conv_w=v_conv_w, v_sgu_ln_g=v_sgu_ln_g, v_sgu_ln_b=v_sgu_ln_b, v_sgu_w_s=v_sgu_w_s, v_sgu_b_s=v_sgu_b_s, v_mix_w_out=v_mix_w_out, v_ln2_g=v_ln2_g, v_ln2_b=v_ln2_b, v_ffn2_w_up=v_ffn2_w_up, v_ffn2_w_down=v_ffn2_w_down, v_ln3_g=v_ln3_g, v_ln3_b=v_ln3_b)
    weights = {n: given[n] for n in TWIN_WEIGHTS}
    shared = {n: given[n] for n in SHARED_INPUTS}
    per_example = {n: given[n] for n in ['x']}
    grad_fn = _jax.value_and_grad(_loss, argnums=(0, 1))

    def one_microbatch(ex, loss_target):
        ex = dict(ex)
        diff = ex.pop(TWIN_DIFF_INPUT)
        return grad_fn(weights, diff, {**shared, **ex}, loss_target)

    if N_MICROBATCH == 1:
        loss, (grad_w, grad_x) = one_microbatch(per_example, given["loss_target"])
    else:
        def body(carry, xs):
            loss_sum, grad_sum = carry
            l_k, (gw_k, gx_k) = one_microbatch(xs[0], xs[1])
            with _jax.named_scope("update"):
                return (loss_sum + l_k, _jax.tree.map(_jnp.add, grad_sum, gw_k)), gx_k

        init = (_jnp.zeros((), _jnp.float32), _jax.tree.map(_jnp.zeros_like, weights))
        (loss, grad_w), grad_x = _jax.lax.scan(body, init, (per_example, given["loss_target"]))
    with _jax.named_scope("update"):
        delta_w, new_m, new_v = {}, {}, {}
        for n in TWIN_WEIGHTS:
            delta_w[n], new_m[n], new_v[n] = _adamw(weights[n], grad_w[n], given["m_" + n], given["v_" + n])
    return (loss, grad_x, *[grad_w[n] for n in TWIN_WEIGHTS], *[delta_w[n] for n in TWIN_WEIGHTS],
            *[new_m[n] for n in TWIN_WEIGHTS], *[new_v[n] for n in TWIN_WEIGHTS])
```

```python
import functools

import jax
import jax.numpy as jnp
from jax import lax
from jax.experimental import pallas as pl
from jax.experimental.pallas import tpu as pltpu

F32 = jnp.float32
BF = jnp.bfloat16
SDS = jax.ShapeDtypeStruct
MESH = pl.DeviceIdType.MESH

LN_EPS = 1e-5
FOX_HEAD_DIM = 64
FOX_Q_BLOCK = 256
HEAD_ROWS = 128
GELU_K = 0.7978845608028654
GELU_C = 0.044715
NEG_BIG = -1e30

ADAM_LR = 0.001
ADAM_B1 = 0.9
ADAM_B2 = 0.999
ADAM_EPS = 1e-08
ADAM_WD = 0.01
ADAM_STEP = 10

VMEM_LIMIT_BYTES = 56 * 1024 * 1024
NT = (((1,), (1,)), ((), ()))
TN = (((0,), (0,)), ((), ()))


def _cparams(*sem):
    return pltpu.CompilerParams(dimension_semantics=sem, vmem_limit_bytes=VMEM_LIMIT_BYTES)


def _tile(n, pref, mult=8):
    t = min(n, pref)
    while n % t or t % mult:
        t -= mult
    return t


def _dot(a, b):
    return jnp.dot(a, b, preferred_element_type=F32)


def _dotg(a, b, dims):
    return lax.dot_general(a, b, dims, preferred_element_type=F32)


def _sigmoid(x):
    return 1.0 / (1.0 + jnp.exp(-x))


def _gelu(x):
    return 0.5 * x * (1.0 + jnp.tanh(GELU_K * (x + GELU_C * x * x * x)))


def _gelu_grad(x):
    t = jnp.tanh(GELU_K * (x + GELU_C * x * x * x))
    return 0.5 * (1.0 + t) + 0.5 * x * (1.0 - t * t) * GELU_K * (1.0 + 3.0 * GELU_C * x * x)


def _ln_stats(z):
    mu = jnp.mean(z, axis=-1, keepdims=True)
    zc = z - mu
    var = jnp.mean(zc * zc, axis=-1, keepdims=True)
    rstd = lax.rsqrt(var + LN_EPS)
    return zc * rstd, rstd


def _ln_bwd(dy, z, g):
    xhat, rstd = _ln_stats(z)
    gdy = dy * g
    m1 = jnp.mean(gdy, axis=-1, keepdims=True)
    m2 = jnp.mean(gdy * xhat, axis=-1, keepdims=True)
    dz = rstd * (gdy - m1 - xhat * m2)
    return dz, jnp.sum(dy * xhat, axis=0, keepdims=True), jnp.sum(dy, axis=0, keepdims=True)


def _ffn_up_fwd(x, wup, name):
    t, d = x.shape
    w = wup.shape[2]
    tm = _tile(t, 512)

    def body(x_ref, wg_ref, wu_ref, h_ref, a_ref):
        xb = x_ref[...].astype(BF)
        g = _dot(xb, wg_ref[0])
        u = _dot(xb, wu_ref[0])
        h_ref[0] = g.astype(BF)
        h_ref[1] = u.astype(BF)
        a_ref[...] = (g * _sigmoid(g) * u).astype(BF)

    return pl.pallas_call(
        body, name=name, grid=(2, t // tm),
        in_specs=[pl.BlockSpec((tm, d), lambda j, i: (i, 0)),
                  pl.BlockSpec((1, d, w), lambda j, i: (j, 0, 0)),
                  pl.BlockSpec((1, d, w), lambda j, i: (j + 2, 0, 0))],
        out_specs=[pl.BlockSpec((2, tm, w), lambda j, i: (0, i, j)),
                   pl.BlockSpec((tm, w), lambda j, i: (i, j))],
        out_shape=[SDS((2, t, 2 * w), BF), SDS((t, 2 * w), BF)],
        compiler_params=_cparams("parallel", "parallel"),
    )(x, wup, wup)


def _res_ln_fwd(parts, w, x, gamma, beta, alpha, res_scale, name):
    t, d = x.shape
    n = len(parts)
    offs = [0]
    for p in parts:
        offs.append(offs[-1] + p.shape[1])
    tm = _tile(t, 512)

    def body(*refs):
        p_refs = refs[:n]
        w_ref, x_ref, g_ref, b_ref, z_ref, y_ref = refs[n:]
        f = _dot(p_refs[0][...], w_ref[offs[0]:offs[1], :])
        for k in range(1, n):
            f = f + _dot(p_refs[k][...], w_ref[offs[k]:offs[k + 1], :])
        z = alpha * x_ref[...] + res_scale * f
        z_ref[...] = z
        xhat, _ = _ln_stats(z)
        y_ref[...] = xhat * g_ref[...] + b_ref[...]

    row = lambda i: (i, 0)
    fixed = lambda i: (0, 0)
    return pl.pallas_call(
        body, name=name, grid=(t // tm,),
        in_specs=[pl.BlockSpec((tm, p.shape[1]), row) for p in parts]
        + [pl.BlockSpec(w.shape, fixed), pl.BlockSpec((tm, d), row),
           pl.BlockSpec((1, d), fixed), pl.BlockSpec((1, d), fixed)],
        out_specs=[pl.BlockSpec((tm, d), row), pl.BlockSpec((tm, d), row)],
        out_shape=[SDS((t, d), F32), SDS((t, d), F32)],
        compiler_params=_cparams("parallel"),
    )(*parts, w, x, gamma, beta)


def _mix_proj_fwd(x, w_main, widths, name):
    t, d = x.shape
    dc, dq, ds = widths
    tm = _tile(t, 512)

    def body(x_ref, w_ref, pc_ref, pq_ref, ps_ref):
        xb = x_ref[...].astype(BF)
        pc_ref[...] = _dot(xb, w_ref[:, 0:dc])
        pq_ref[...] = _dot(xb, w_ref[:, dc:dc + dq]).astype(BF)
        ps_ref[...] = _dot(xb, w_ref[:, dc + dq:dc + dq + ds])

    row = lambda i: (i, 0)
    return pl.pallas_call(
        body, name=name, grid=(t // tm,),
        in_specs=[pl.BlockSpec((tm, d), row), pl.BlockSpec(w_main.shape, lambda i: (0, 0))],
        out_specs=[pl.BlockSpec((tm, dc), row), pl.BlockSpec((tm, dq), row), pl.BlockSpec((tm, ds), row)],
        out_shape=[SDS((t, dc), F32), SDS((t, dq), BF), SDS((t, ds), F32)],
        compiler_params=_cparams("parallel"),
    )(x, w_main)


def _prefix_sum_lanes(v, reverse):
    n = v.shape[-1]
    lane = lax.broadcasted_iota(jnp.int32, v.shape, v.ndim - 1)
    sh = 1
    while sh < n:
        if reverse:
            v = v + jnp.where(lane < n - sh, pltpu.roll(v, n - sh, axis=v.ndim - 1), 0.0)
        else:
            v = v + jnp.where(lane >= sh, pltpu.roll(v, sh, axis=v.ndim - 1), 0.0)
        sh *= 2
    return v


def _cum_fwd(x3, wft, bf, name):
    b, s, d = x3.shape
    h = bf.shape[0]

    def body(x_ref, w_ref, b_ref, fl_ref, cum_ref):
        fl = _dotg(w_ref[...], x_ref[0].astype(BF), NT)[0:h] + b_ref[...]
        fl_ref[0] = fl
        lf = jnp.minimum(fl, 0.0) - jnp.log(1.0 + jnp.exp(-jnp.abs(fl)))
        cum_ref[0] = _prefix_sum_lanes(lf, reverse=False)

    return pl.pallas_call(
        body, name=name, grid=(b,),
        in_specs=[pl.BlockSpec((1, s, d), lambda i: (i, 0, 0)),
                  pl.BlockSpec(wft.shape, lambda i: (0, 0)), pl.BlockSpec((h, 1), lambda i: (0, 0))],
        out_specs=[pl.BlockSpec((1, h, s), lambda i: (i, 0, 0)), pl.BlockSpec((1, h, s), lambda i: (i, 0, 0))],
        out_shape=[SDS((b, h, s), F32), SDS((b, h, s), F32)],
        compiler_params=_cparams("parallel"),
    )(x3, wft, bf)


def _shift_rows(z, k, down):
    n = z.shape[0]
    row = lax.broadcasted_iota(jnp.int32, z.shape, 0)
    if down:
        return jnp.where(row >= k, pltpu.roll(z, k, axis=0), 0.0)
    return jnp.where(row < n - k, pltpu.roll(z, n - k, axis=0), 0.0)


def _conv_fwd(pc3, cw, name):
    b, s, c3 = pc3.shape
    c = c3 // 3

    def body(p_ref, w_ref, y_ref):
        z = p_ref[0, :, c:2 * c] * p_ref[0, :, 2 * c:3 * c]
        conv = w_ref[0:1, :] * _shift_rows(z, 2, True) + w_ref[1:2, :] * _shift_rows(z, 1, True) + w_ref[2:3, :] * z
        y_ref[0] = (p_ref[0, :, 0:c] * conv).astype(BF)

    return pl.pallas_call(
        body, name=name, grid=(b,),
        in_specs=[pl.BlockSpec((1, s, c3), lambda i: (i, 0, 0)), pl.BlockSpec((3, c), lambda i: (0, 0))],
        out_specs=pl.BlockSpec((1, s, c), lambda i: (i, 0, 0)),
        out_shape=SDS((b, s, c), BF),
        compiler_params=_cparams("parallel"),
    )(pc3, cw)


def _head_masks(width):
    lane = lax.broadcasted_iota(jnp.int32, (1, width), 1)
    return [lane < FOX_HEAD_DIM, lane >= FOX_HEAD_DIM]


def _fox_probs(qm, k, cum_row, lo):
    tq, hi = qm.shape[0], k.shape[0]
    s = _dotg(qm, k, NT) * (FOX_HEAD_DIM ** -0.5) - cum_row
    col = lax.broadcasted_iota(jnp.int32, (tq, hi), 1)
    rowi = lax.broadcasted_iota(jnp.int32, (tq, hi), 0) + lo
    s = jnp.where(col <= rowi, s, NEG_BIG)
    p = jnp.exp(s - jnp.max(s, axis=-1, keepdims=True))
    return p * (1.0 / jnp.sum(p, axis=-1, keepdims=True))


def _fox_fwd(pq3, cum4, name):
    b, s, d3 = pq3.shape
    df = d3 // 3
    hp = df // 128
    tq = _tile(s, FOX_Q_BLOCK)

    def body(q_ref, k_ref, v_ref, c_ref, o_ref):
        masks = _head_masks(128)
        for i in range(s // tq):
            lo, hi = i * tq, (i + 1) * tq
            q = q_ref[0, lo:hi, :]
            k = k_ref[0, 0:hi, :]
            v = v_ref[0, 0:hi, :]
            o = jnp.zeros((tq, 128), F32)
            for e in range(2):
                p = _fox_probs(jnp.where(masks[e], q, 0), k, c_ref[0, 0, e:e + 1, 0:hi], lo)
                o = jnp.where(masks[e], _dot(p.astype(BF), v), o)
            o_ref[0, lo:hi, :] = o.astype(BF)

    blk = lambda off: pl.BlockSpec((1, s, 128), lambda i, j: (i, 0, off + j))
    return pl.pallas_call(
        body, name=name, grid=(b, hp),
        in_specs=[blk(0), blk(hp), blk(2 * hp), pl.BlockSpec((1, 1, 2, s), lambda i, j: (i, j, 0, 0))],
        out_specs=blk(0),
        out_shape=SDS((b, s, df), BF),
        compiler_params=_cparams("parallel", "parallel"),
    )(pq3, pq3, pq3, cum4)


def _sgu_mix(wm, vnb, bias, gmasks):
    out = bias
    for g in range(len(wm)):
        out = out + jnp.where(gmasks[g], _dot(wm[g], vnb), 0.0)
    return out


def _sgu_consts(ws_ref, bs_ref, ds):
    ng, c, _ = ws_ref.shape
    gd = ds // ng
    tri = lax.broadcasted_iota(jnp.int32, (c, c), 0) >= lax.broadcasted_iota(jnp.int32, (c, c), 1)
    lane = lax.broadcasted_iota(jnp.int32, (1, ds), 1)
    gmasks = [(lane >= g * gd) & (lane < (g + 1) * gd) for g in range(ng)]
    wm = [jnp.where(tri, ws_ref[g], 0.0).astype(BF) for g in range(ng)]
    bias = jnp.zeros((c, ds), F32)
    for g in range(ng):
        bias = jnp.where(gmasks[g], bs_ref[g], bias)
    return tri, gmasks, wm, bias


def _sgu_fwd(ps, lng, lnb, ws, bs, name):
    t, ds2 = ps.shape
    ds = ds2 // 2
    c = ws.shape[1]
    tm = _tile(t, 512, c)

    def body(p_ref, g_ref, b_ref, ws_ref, bs_ref, y_ref):
        _, gmasks, wm, bias = _sgu_consts(ws_ref, bs_ref, ds)
        up = _gelu(p_ref[:, 0:ds])
        xhat, _ = _ln_stats(_gelu(p_ref[:, ds:ds2]))
        vnb = (xhat * g_ref[...] + b_ref[...]).astype(BF)
        for n in range(tm // c):
            r0, r1 = n * c, (n + 1) * c
            y_ref[r0:r1, :] = (up[r0:r1] * _sgu_mix(wm, vnb[r0:r1], bias, gmasks)).astype(BF)

    fixed2 = lambda i: (0, 0)
    fixed3 = lambda i: (0, 0, 0)
    return pl.pallas_call(
        body, name=name, grid=(t // tm,),
        in_specs=[pl.BlockSpec((tm, ds2), lambda i: (i, 0)), pl.BlockSpec((1, ds), fixed2),
                  pl.BlockSpec((1, ds), fixed2), pl.BlockSpec(ws.shape, fixed3), pl.BlockSpec(bs.shape, fixed3)],
        out_specs=pl.BlockSpec((tm, ds), lambda i: (i, 0)),
        out_shape=SDS((t, ds), BF),
        compiler_params=_cparams("parallel"),
    )(ps, lng, lnb, ws, bs)


def _loss_fwd(y, target, name):
    t, d = y.shape
    tm = _tile(t, 512)

    def body(y_ref, t_ref, dy_ref, l_ref):
        @pl.when(pl.program_id(0) == 0)
        def _():
            l_ref[...] = jnp.zeros_like(l_ref)

        err = y_ref[...] - t_ref[...]
        dy_ref[...] = err * (1.0 / d)
        l_ref[...] += 0.5 * jnp.sum(jnp.sum(err * err, axis=-1, keepdims=True) * (1.0 / d), axis=0, keepdims=True)

    row = lambda i: (i, 0)
    return pl.pallas_call(
        body, name=name, grid=(t // tm,),
        in_specs=[pl.BlockSpec((tm, d), row), pl.BlockSpec((tm, d), row)],
        out_specs=[pl.BlockSpec((tm, d), row), pl.BlockSpec((8, 128), lambda i: (0, 0))],
        out_shape=[SDS((t, d), F32), SDS((8, 128), F32)],
        compiler_params=_cparams("arbitrary"),
    )(y, target)


def _acc_rows(ref, val, first):
    @pl.when(first)
    def _():
        ref[...] = val

    @pl.when(jnp.logical_not(first))
    def _():
        ref[...] += val


def _ffn_bwd_mid(dy, z, gamma, wd, h, name):
    t, d = dy.shape
    dff = wd.shape[0]
    half = dff // 2
    tm = _tile(t, 256)

    def body(dy_ref, z_ref, g_ref, wd_ref, h_ref, dz_ref, df_ref, dh_ref, dg_ref, db_ref):
        dz, dgam, dbet = _ln_bwd(dy_ref[...], z_ref[...], g_ref[...])
        first = pl.program_id(0) == 0
        _acc_rows(dg_ref, dgam, first)
        _acc_rows(db_ref, dbet, first)
        dz_ref[...] = dz
        dfb = (0.5 * dz).astype(BF)
        df_ref[...] = dfb
        for j in range(2):
            c0, c1 = j * half, (j + 1) * half
            da = _dotg(dfb, wd_ref[c0:c1, :], NT)
            g = h_ref[0, :, c0:c1].astype(F32)
            u = h_ref[1, :, c0:c1].astype(F32)
            sg = _sigmoid(g)
            dh_ref[0, :, c0:c1] = (da * u * sg * (1.0 + g * (1.0 - sg))).astype(BF)
            dh_ref[1, :, c0:c1] = (da * g * sg).astype(BF)

    row = lambda i: (i, 0)
    fixed = lambda i: (0, 0)
    return pl.pallas_call(
        body, name=name, grid=(t // tm,),
        in_specs=[pl.BlockSpec((tm, d), row), pl.BlockSpec((tm, d), row), pl.BlockSpec((1, d), fixed),
                  pl.BlockSpec(wd.shape, fixed), pl.BlockSpec((2, tm, dff), lambda i: (0, i, 0))],
        out_specs=[pl.BlockSpec((tm, d), row), pl.BlockSpec((tm, d), row),
                   pl.BlockSpec((2, tm, dff), lambda i: (0, i, 0)),
                   pl.BlockSpec((1, d), fixed), pl.BlockSpec((1, d), fixed)],
        out_shape=[SDS((t, d), F32), SDS((t, d), BF), SDS((2, t, dff), BF), SDS((1, d), F32), SDS((1, d), F32)],
        compiler_params=_cparams("arbitrary"),
    )(dy, z, gamma, wd, h)


def _ffn_bwd_dx(dh, wup, dz, alpha, name):
    _, t, dff = dh.shape
    nq, d, w = wup.shape
    per = dff // w
    tm = _tile(t, 256)

    def body(dh_ref, w_ref, dz_ref, dx_ref):
        acc = alpha * dz_ref[...]
        for q in range(nq):
            c0 = (q % per) * w
            acc = acc + _dotg(dh_ref[q // per, :, c0:c0 + w], w_ref[q], NT)
        dx_ref[...] = acc

    row = lambda i: (i, 0)
    return pl.pallas_call(
        body, name=name, grid=(t // tm,),
        in_specs=[pl.BlockSpec((2, tm, dff), lambda i: (0, i, 0)), pl.BlockSpec(wup.shape, lambda i: (0, 0, 0)),
                  pl.BlockSpec((tm, d), row)],
        out_specs=pl.BlockSpec((tm, d), row),
        out_shape=SDS((t, d), F32),
        compiler_params=_cparams("parallel"),
    )(dh, wup, dz)


def _dw(a, b3, ka, nb, name):
    t, ka_tot = a.shape
    gb, _, nb_tot = b3.shape
    na, ncb = ka_tot // ka, nb_tot // nb
    tm = _tile(t, 512)

    def body(a_ref, b_ref, o_ref):
        part = _dotg(a_ref[...].astype(BF), b_ref[0], TN)

        @pl.when(pl.program_id(2) == 0)
        def _():
            o_ref[0, 0] = part

        @pl.when(pl.program_id(2) != 0)
        def _():
            o_ref[0, 0] += part

    return pl.pallas_call(
        body, name=name, grid=(na, gb * ncb, t // tm),
        in_specs=[pl.BlockSpec((tm, ka), lambda ja, jb, i: (i, ja)),
                  pl.BlockSpec((1, tm, nb), lambda ja, jb, i: (jb // ncb, i, jb % ncb))],
        out_specs=pl.BlockSpec((1, 1, ka, nb), lambda ja, jb, i: (ja, jb, 0, 0)),
        out_shape=SDS((na, gb * ncb, ka, nb), F32),
        compiler_params=_cparams("parallel", "parallel", "arbitrary"),
    )(a, b3)


def _out_bwd(dy, z, gamma, wout, widths, name):
    t, d = dy.shape
    wa, wb, wc = widths
    tm = _tile(t, 512)

    def body(dy_ref, z_ref, g_ref, w_ref, dz_ref, dzb_ref, da_ref, dbb_ref, dc_ref, dg_ref, db_ref):
        dz, dgam, dbet = _ln_bwd(dy_ref[...], z_ref[...], g_ref[...])
        first = pl.program_id(0) == 0
        _acc_rows(dg_ref, dgam, first)
        _acc_rows(db_ref, dbet, first)
        dz_ref[...] = dz
        dzb = dz.astype(BF)
        dzb_ref[...] = dzb
        da_ref[...] = _dotg(dzb, w_ref[0:wa, :], NT).astype(BF)
        dbb_ref[...] = _dotg(dzb, w_ref[wa:wa + wb, :], NT).astype(BF)
        dc_ref[...] = _dotg(dzb, w_ref[wa + wb:wa + wb + wc, :], NT).astype(BF)

    row = lambda i: (i, 0)
    fixed = lambda i: (0, 0)
    return pl.pallas_call(
        body, name=name, grid=(t // tm,),
        in_specs=[pl.BlockSpec((tm, d), row), pl.BlockSpec((tm, d), row), pl.BlockSpec((1, d), fixed),
                  pl.BlockSpec(wout.shape, fixed)],
        out_specs=[pl.BlockSpec((tm, d), row), pl.BlockSpec((tm, d), row), pl.BlockSpec((tm, wa), row),
                   pl.BlockSpec((tm, wb), row), pl.BlockSpec((tm, wc), row),
                   pl.BlockSpec((1, d), fixed), pl.BlockSpec((1, d), fixed)],
        out_shape=[SDS((t, d), F32), SDS((t, d), BF), SDS((t, wa), BF), SDS((t, wb), BF), SDS((t, wc), BF),
                   SDS((1, d), F32), SDS((1, d), F32)],
        compiler_params=_cparams("arbitrary"),
    )(dy, z, gamma, wout)


def _conv_bwd(pc3, dya3, cw, name):
    b, s, c3 = pc3.shape
    c = c3 // 3

    def body(p_ref, dy_ref, w_ref, dp_ref, dw_ref):
        cb = p_ref[0, :, 0:c]
        cc = p_ref[0, :, c:2 * c]
        ch = p_ref[0, :, 2 * c:3 * c]
        z = cc * ch
        z1 = _shift_rows(z, 1, True)
        z2 = _shift_rows(z, 2, True)
        w0, w1, w2 = w_ref[0:1, :], w_ref[1:2, :], w_ref[2:3, :]
        dy = dy_ref[0].astype(F32)
        dconv = dy * cb
        dz = w2 * dconv + w1 * _shift_rows(dconv, 1, False) + w0 * _shift_rows(dconv, 2, False)
        dp_ref[0, :, 0:c] = (dy * (w0 * z2 + w1 * z1 + w2 * z)).astype(BF)
        dp_ref[0, :, c:2 * c] = (dz * ch).astype(BF)
        dp_ref[0, :, 2 * c:3 * c] = (dz * cc).astype(BF)
        first = pl.program_id(0) == 0
        for r, zs in enumerate((z2, z1, z)):
            _acc_rows(dw_ref.at[r:r + 1], jnp.sum(dconv * zs, axis=0, keepdims=True), first)

    blk = lambda i: (i, 0, 0)
    return pl.pallas_call(
        body, name=name, grid=(b,),
        in_specs=[pl.BlockSpec((1, s, c3), blk), pl.BlockSpec((1, s, c), blk), pl.BlockSpec((3, c), lambda i: (0, 0))],
        out_specs=[pl.BlockSpec((1, s, c3), blk), pl.BlockSpec((3, c), lambda i: (0, 0))],
        out_shape=[SDS((b, s, c3), BF), SDS((3, c), F32)],
        compiler_params=_cparams("arbitrary"),
    )(pc3, dya3, cw)


def _fox_bwd(pq3, cum4, dyb3, name):
    b, s, d3 = pq3.shape
    df = d3 // 3
    hp = df // 128
    tq = _tile(s, FOX_Q_BLOCK)
    scale = FOX_HEAD_DIM ** -0.5

    def body(q_ref, k_ref, v_ref, c_ref, do_ref, dq_ref, dk_ref, dv_ref, dc_ref, dk_acc, dv_acc):
        masks = _head_masks(128)
        dk_acc[...] = jnp.zeros_like(dk_acc)
        dv_acc[...] = jnp.zeros_like(dv_acc)
        dc_ref[...] = jnp.zeros_like(dc_ref)
        for i in range(s // tq):
            lo, hi = i * tq, (i + 1) * tq
            q = q_ref[0, lo:hi, :]
            do = do_ref[0, lo:hi, :]
            k = k_ref[0, 0:hi, :]
            v = v_ref[0, 0:hi, :]
            dq = jnp.zeros((tq, 128), F32)
            for e in range(2):
                qm = jnp.where(masks[e], q, 0)
                dom = jnp.where(masks[e], do, 0)
                p = _fox_probs(qm, k, c_ref[0, 0, e:e + 1, 0:hi], lo)
                dp = _dotg(dom, v, NT)
                ds = p * (dp - jnp.sum(p * dp, axis=-1, keepdims=True))
                dsb = ds.astype(BF)
                dq = jnp.where(masks[e], _dot(dsb, k) * scale, dq)
                dk_acc[0:hi, :] += _dotg(dsb, qm, TN) * scale
                dv_acc[0:hi, :] += _dotg(p.astype(BF), dom, TN)
                dc_ref[0, 0, e:e + 1, 0:hi] -= jnp.sum(ds, axis=0, keepdims=True)
            dq_ref[0, lo:hi, :] = dq.astype(BF)
        dk_ref[0] = dk_acc[...].astype(BF)
        dv_ref[0] = dv_acc[...].astype(BF)

    blk = lambda off: pl.BlockSpec((1, s, 128), lambda i, j: (i, 0, off + j))
    cblk = pl.BlockSpec((1, 1, 2, s), lambda i, j: (i, j, 0, 0))
    return pl.pallas_call(
        body, name=name, grid=(b, hp),
        in_specs=[blk(0), blk(hp), blk(2 * hp), cblk, blk(0)],
        out_specs=[blk(0), blk(0), blk(0), cblk],
        out_shape=[SDS((b, s, df), BF), SDS((b, s, df), BF), SDS((b, s, df), BF), SDS(cum4.shape, F32)],
        scratch_shapes=[pltpu.VMEM((s, 128), F32), pltpu.VMEM((s, 128), F32)],
        compiler_params=_cparams("parallel", "parallel"),
    )(pq3, pq3, pq3, cum4, dyb3)


def _cum_bwd(dcum, flog, x3, name):
    b, h, s = dcum.shape
    d = x3.shape[2]

    def body(dc_ref, fl_ref, x_ref, dfl_ref, dbf_ref, dwf_ref):
        dfl = _prefix_sum_lanes(dc_ref[0], reverse=True) * _sigmoid(-fl_ref[0])
        dfl_ref[0] = dfl
        first = pl.program_id(0) == 0
        _acc_rows(dbf_ref, jnp.broadcast_to(jnp.sum(dfl, axis=-1, keepdims=True), (h, 128)), first)
        dflp = jnp.concatenate([dfl, jnp.zeros((HEAD_ROWS - h, s), F32)], axis=0).astype(BF)
        _acc_rows(dwf_ref, _dot(dflp, x_ref[0].astype(BF))[0:h], first)

    blk = lambda i: (i, 0, 0)
    return pl.pallas_call(
        body, name=name, grid=(b,),
        in_specs=[pl.BlockSpec((1, h, s), blk), pl.BlockSpec((1, h, s), blk), pl.BlockSpec((1, s, d), blk)],
        out_specs=[pl.BlockSpec((1, h, s), blk), pl.BlockSpec((h, 128), lambda i: (0, 0)),
                   pl.BlockSpec((h, d), lambda i: (0, 0))],
        out_shape=[SDS((b, h, s), F32), SDS((h, 128), F32), SDS((h, d), F32)],
        compiler_params=_cparams("arbitrary"),
    )(dcum, flog, x3)


def _sgu_bwd(ps, dyc, lng, lnb, ws, bs, name):
    t, ds2 = ps.shape
    ds = ds2 // 2
    ng, c, _ = ws.shape
    tm = _tile(t, 512, c)

    def body(p_ref, dy_ref, g_ref, b_ref, ws_ref, bs_ref, dp_ref, dws_ref, dbs_ref, dg_ref, db_ref, dvn_acc):
        tri, gmasks, wm, bias = _sgu_consts(ws_ref, bs_ref, ds)
        su = p_ref[:, 0:ds]
        sv = p_ref[:, ds:ds2]
        up = _gelu(su)
        gv = _gelu(sv)
        xhat, rstd = _ln_stats(gv)
        vnb = (xhat * g_ref[...] + b_ref[...]).astype(BF)
        dy = dy_ref[...].astype(F32)
        dws = [jnp.zeros((c, c), F32) for _ in range(ng)]
        dbs = [jnp.zeros((c, 1), F32) for _ in range(ng)]
        for n in range(tm // c):
            r0, r1 = n * c, (n + 1) * c
            mixed = _sgu_mix(wm, vnb[r0:r1], bias, gmasks)
            dp_ref[r0:r1, 0:ds] = (dy[r0:r1] * mixed * _gelu_grad(su[r0:r1])).astype(BF)
            dmix = dy[r0:r1] * up[r0:r1]
            dvn = jnp.zeros((c, ds), F32)
            for g in range(ng):
                dmg = jnp.where(gmasks[g], dmix, 0.0)
                dmb = dmg.astype(BF)
                dws[g] = dws[g] + _dotg(dmb, vnb[r0:r1], NT)
                dbs[g] = dbs[g] + jnp.sum(dmg, axis=-1, keepdims=True)
                dvn = dvn + _dotg(wm[g], dmb, TN)
            dvn_acc[r0:r1, :] = dvn
        dvn_all = dvn_acc[...]
        gdv = dvn_all * g_ref[...]
        m1 = jnp.mean(gdv, axis=-1, keepdims=True)
        m2 = jnp.mean(gdv * xhat, axis=-1, keepdims=True)
        dgv = rstd * (gdv - m1 - xhat * m2)
        dp_ref[:, ds:ds2] = (dgv * _gelu_grad(sv)).astype(BF)
        first = pl.program_id(0) == 0
        _acc_rows(dg_ref, jnp.sum(dvn_all * xhat, axis=0, keepdims=True), first)
        _acc_rows(db_ref, jnp.sum(dvn_all, axis=0, keepdims=True), first)
        for g in range(ng):
            _acc_rows(dws_ref.at[g], jnp.where(tri, dws[g], 0.0), first)
            _acc_rows(dbs_ref.at[g], dbs[g], first)

    row = lambda i: (i, 0)
    fixed2 = lambda i: (0, 0)
    fixed3 = lambda i: (0, 0, 0)
    return pl.pallas_call(
        body, name=name, grid=(t // tm,),
        in_specs=[pl.BlockSpec((tm, ds2), row), pl.BlockSpec((tm, ds), row), pl.BlockSpec((1, ds), fixed2),
                  pl.BlockSpec((1, ds), fixed2), pl.BlockSpec(ws.shape, fixed3), pl.BlockSpec(bs.shape, fixed3)],
        out_specs=[pl.BlockSpec((tm, ds2), row), pl.BlockSpec(ws.shape, fixed3), pl.BlockSpec(bs.shape, fixed3),
                   pl.BlockSpec((1, ds), fixed2), pl.BlockSpec((1, ds), fixed2)],
        out_shape=[SDS((t, ds2), BF), SDS(ws.shape, F32), SDS(bs.shape, F32), SDS((1, ds), F32), SDS((1, ds), F32)],
        scratch_shapes=[pltpu.VMEM((tm, ds), F32)],
        compiler_params=_cparams("arbitrary"),
    )(ps, dyc, lng, lnb, ws, bs)


def _mix_bwd_dx(dz, dconv, dq, dk, dv, dsgu, dflog, w_main, wft, seq, alpha, name):
    t, d = dz.shape
    groups = [dconv, dq, dk, dv, dsgu]
    offs = [0]
    for g in groups:
        offs.append(offs[-1] + g.shape[1])
    h = dflog.shape[1]
    tm = _tile(seq, 512)
    per_seq = seq // tm

    def body(dz_ref, a0, a1, a2, a3, a4, dfl_ref, w_ref, wf_ref, dx_ref):
        dflp = jnp.concatenate([dfl_ref[0], jnp.zeros((HEAD_ROWS - h, tm), F32)], axis=0).astype(BF)
        acc = alpha * dz_ref[...] + _dotg(dflp, wf_ref[...], TN)
        for k, a_ref in enumerate((a0, a1, a2, a3, a4)):
            acc = acc + _dotg(a_ref[...], w_ref[:, offs[k]:offs[k + 1]], NT)
        dx_ref[...] = acc

    row = lambda i: (i, 0)
    return pl.pallas_call(
        body, name=name, grid=(t // tm,),
        in_specs=[pl.BlockSpec((tm, d), row)] + [pl.BlockSpec((tm, g.shape[1]), row) for g in groups]
        + [pl.BlockSpec((1, h, tm), lambda i: (i // per_seq, 0, i % per_seq)),
           pl.BlockSpec(w_main.shape, lambda i: (0, 0)), pl.BlockSpec(wft.shape, lambda i: (0, 0))],
        out_specs=pl.BlockSpec((tm, d), row),
        out_shape=SDS((t, d), F32),
        compiler_params=_cparams("parallel"),
    )(dz, *groups, dflog, w_main, wft)


def _adamw(w, g, m, v, name):
    r, c = w.shape
    tr = _tile(r, 512)
    c1 = 1.0 / (1.0 - ADAM_B1 ** ADAM_STEP)
    c2 = 1.0 / (1.0 - ADAM_B2 ** ADAM_STEP)

    def body(w_ref, g_ref, m_ref, v_ref, d_ref, nm_ref, nv_ref):
        g_ = g_ref[...]
        nm = ADAM_B1 * m_ref[...] + (1.0 - ADAM_B1) * g_
        nv = ADAM_B2 * v_ref[...] + (1.0 - ADAM_B2) * (g_ * g_)
        nm_ref[...] = nm
        nv_ref[...] = nv
        d_ref[...] = -ADAM_LR * ((nm * c1) / (jnp.sqrt(nv * c2) + ADAM_EPS) + ADAM_WD * w_ref[...])

    blk = pl.BlockSpec((tr, c), lambda i: (i, 0))
    return pl.pallas_call(
        body, name=name, grid=(r // tr,),
        in_specs=[blk, blk, blk, blk], out_specs=[blk, blk, blk],
        out_shape=[SDS((r, c), F32)] * 3,
        compiler_params=_cparams("parallel"),
    )(w, g, m, v)


HBM = pl.BlockSpec(memory_space=pl.ANY)


def _mesh_pos():
    x, y, c = lax.axis_index("x"), lax.axis_index("y"), lax.axis_index("c")
    chips = [(1 - x, y), (x, 1 - y), (1 - x, 1 - y)]
    return x, y, c, chips


def _all_gather_weights(pieces, name):
    n = len(pieces)

    def body(*refs):
        ins, outs = refs[:n], refs[n:2 * n]
        send_sems, recv_sems, loc_sems = refs[2 * n:]
        x, y, c, chips = _mesh_pos()
        q = 2 * x + y
        sib = (x, y, 1 - c)

        def rcopy(p, k, src, dst, to):
            return pltpu.make_async_remote_copy(src_ref=src, dst_ref=dst, send_sem=send_sems.at[p, k],
                                                recv_sem=recv_sems.at[p, k], device_id=to, device_id_type=MESH)

        local = []
        for p in range(n):
            for l in range(2):
                local.append(pltpu.make_async_copy(ins[p].at[l], outs[p].at[l, q], loc_sems.at[p, l]))
        for cp in local:
            cp.start()
        sends = []
        for j, (cx, cy) in enumerate(chips):
            for p in range(n):
                sends.append(rcopy(p, j, ins[p].at[c], outs[p].at[c, q], (cx, cy, c)))
        for cp in sends:
            cp.start()
        for j, (cx, cy) in enumerate(chips):
            qj = 2 * cx + cy
            for p in range(n):
                blk = outs[p].at[c, qj]
                rcopy(p, j, blk, blk, (cx, cy, c)).wait_recv()
                fwd = rcopy(p, 3 + j, blk, blk, sib)
                fwd.start()
                sends.append(fwd)
        for j, (cx, cy) in enumerate(chips):
            qj = 2 * cx + cy
            for p in range(n):
                blk = outs[p].at[1 - c, qj]
                rcopy(p, 3 + j, blk, blk, sib).wait_recv()
        for cp in sends:
            cp.wait_send()
        for cp in local:
            cp.wait()

    return pl.pallas_call(
        body, name=name,
        in_specs=[HBM] * n, out_specs=[HBM] * n,
        out_shape=[SDS((2, 4) + p.shape[1:], p.dtype) for p in pieces],
        scratch_shapes=[pltpu.SemaphoreType.DMA((n, 6)), pltpu.SemaphoreType.DMA((n, 6)),
                        pltpu.SemaphoreType.DMA((n, 2))],
    )(*pieces)


def _pair_send_other_layer(gbs, name):
    n = len(gbs)

    def body(*refs):
        ins, outs = refs[:n], refs[n:2 * n]
        send_sems, recv_sems = refs[2 * n:]
        x, y, c, _ = _mesh_pos()
        cps = [pltpu.make_async_remote_copy(src_ref=ins[p].at[1 - c], dst_ref=outs[p], send_sem=send_sems.at[p],
                                            recv_sem=recv_sems.at[p], device_id=(x, y, 1 - c), device_id_type=MESH)
               for p in range(n)]
        for cp in cps:
            cp.start()
        for cp in cps:
            cp.wait()

    return pl.pallas_call(
        body, name=name, in_specs=[HBM] * n, out_specs=[HBM] * n,
        out_shape=[SDS(g.shape[1:], g.dtype) for g in gbs],
        scratch_shapes=[pltpu.SemaphoreType.DMA((n,)), pltpu.SemaphoreType.DMA((n,))],
    )(*gbs)


def _scatter_to_chips(ps, name):
    n = len(ps)

    def body(*refs):
        ins, outs = refs[:n], refs[n:2 * n]
        send_sems, recv_sems, loc_sems = refs[2 * n:]
        x, y, c, chips = _mesh_pos()
        q = 2 * x + y
        local = [pltpu.make_async_copy(ins[p].at[q], outs[p].at[q], loc_sems.at[p]) for p in range(n)]
        for cp in local:
            cp.start()
        sends = []
        for j, (cx, cy) in enumerate(chips):
            for p in range(n):
                sends.append(pltpu.make_async_remote_copy(
                    src_ref=ins[p].at[2 * cx + cy], dst_ref=outs[p].at[q], send_sem=send_sems.at[p, j],
                    recv_sem=recv_sems.at[p, j], device_id=(cx, cy, c), device_id_type=MESH))
        for cp in sends:
            cp.start()
        for j, (cx, cy) in enumerate(chips):
            for p in range(n):
                blk = outs[p].at[2 * cx + cy]
                pltpu.make_async_remote_copy(src_ref=blk, dst_ref=blk, send_sem=send_sems.at[p, j],
                                             recv_sem=recv_sems.at[p, j], device_id=(cx, cy, c),
                                             device_id_type=MESH).wait_recv()
        for cp in sends:
            cp.wait_send()
        for cp in local:
            cp.wait()

    return pl.pallas_call(
        body, name=name, in_specs=[HBM] * n, out_specs=[HBM] * n,
        out_shape=[SDS(p.shape, p.dtype) for p in ps],
        scratch_shapes=[pltpu.SemaphoreType.DMA((n, 3)), pltpu.SemaphoreType.DMA((n, 3)),
                        pltpu.SemaphoreType.DMA((n,))],
    )(*ps)


def _pair_share_totals(tots, name):
    n = len(tots)

    def body(*refs):
        ins, outs = refs[:n], refs[n:2 * n]
        send_sems, recv_sems, loc_sems = refs[2 * n:]
        x, y, c, _ = _mesh_pos()
        local = [pltpu.make_async_copy(ins[p], outs[p].at[c], loc_sems.at[p]) for p in range(n)]
        cps = [pltpu.make_async_remote_copy(src_ref=ins[p], dst_ref=outs[p].at[c], send_sem=send_sems.at[p],
                                            recv_sem=recv_sems.at[p], device_id=(x, y, 1 - c), device_id_type=MESH)
               for p in range(n)]
        for cp in local + cps:
            cp.start()
        for p in range(n):
            blk = outs[p].at[1 - c]
            pltpu.make_async_remote_copy(src_ref=blk, dst_ref=blk, send_sem=send_sems.at[p], recv_sem=recv_sems.at[p],
                                         device_id=(x, y, 1 - c), device_id_type=MESH).wait_recv()
        for cp in cps:
            cp.wait_send()
        for cp in local:
            cp.wait()

    return pl.pallas_call(
        body, name=name, in_specs=[HBM] * n, out_specs=[HBM] * n,
        out_shape=[SDS((2,) + t_.shape, t_.dtype) for t_ in tots],
        scratch_shapes=[pltpu.SemaphoreType.DMA((n,)), pltpu.SemaphoreType.DMA((n,)), pltpu.SemaphoreType.DMA((n,))],
    )(*tots)


def _all_reduce_small(v, name):
    r = v.shape[0]

    def body(v_ref, o_ref, buf, send_sems, recv_sems):
        x, y, c, _ = _mesh_pos()
        me = 4 * x + 2 * y + c
        buf[me] = v_ref[...]
        sends = []
        for k in range(1, 8):
            dx, dy, dc = (k >> 2) & 1, (k >> 1) & 1, k & 1
            to = (x ^ dx, y ^ dy, c ^ dc)
            sends.append(pltpu.make_async_remote_copy(src_ref=v_ref, dst_ref=buf.at[me], send_sem=send_sems.at[k - 1],
                                                      recv_sem=recv_sems.at[k - 1], device_id=to, device_id_type=MESH))
        for cp in sends:
            cp.start()
        for k in range(1, 8):
            dx, dy, dc = (k >> 2) & 1, (k >> 1) & 1, k & 1
            src = 4 * (x ^ dx) + 2 * (y ^ dy) + (c ^ dc)
            blk = buf.at[src]
            pltpu.make_async_remote_copy(src_ref=blk, dst_ref=blk, send_sem=send_sems.at[k - 1],
                                         recv_sem=recv_sems.at[k - 1], device_id=(x ^ dx, y ^ dy, c ^ dc),
                                         device_id_type=MESH).wait_recv()
        for cp in sends:
            cp.wait_send()
        acc = buf[0]
        for k in range(1, 8):
            acc = acc + buf[k]
        o_ref[...] = acc

    vm = pl.BlockSpec(memory_space=pltpu.VMEM)
    return pl.pallas_call(
        body, name=name, in_specs=[vm], out_specs=vm, out_shape=SDS(v.shape, F32),
        scratch_shapes=[pltpu.VMEM((8, r, 128), F32), pltpu.SemaphoreType.DMA((7,)), pltpu.SemaphoreType.DMA((7,))],
        compiler_params=pltpu.CompilerParams(vmem_limit_bytes=VMEM_LIMIT_BYTES),
    )(v)


def _pair_sum(g, r1, cidx, name):
    _, nq, xr, yc = g.shape
    tr = _tile(xr, 512, 16)

    def body(c_ref, g_ref, r_ref, o_ref):
        o_ref[0] = (g_ref[0, 0] + r_ref[0].astype(F32)).astype(BF)

    return pl.pallas_call(
        body, name=name,
        grid_spec=pltpu.PrefetchScalarGridSpec(
            num_scalar_prefetch=1, grid=(nq, xr // tr),
            in_specs=[pl.BlockSpec((1, 1, tr, yc), lambda j, i, c_ref: (c_ref[0], j, i, 0)),
                      pl.BlockSpec((1, tr, yc), lambda j, i, c_ref: (j, i, 0))],
            out_specs=pl.BlockSpec((1, tr, yc), lambda j, i, c_ref: (j, i, 0))),
        out_shape=SDS((nq, xr, yc), BF),
        compiler_params=_cparams("parallel", "parallel"),
    )(cidx, g, r1)


def _chip_sum(r2, name):
    nq, xr, yc = r2.shape
    tr = _tile(xr, 512, 16)

    def body(r_ref, o_ref):
        acc = r_ref[0].astype(F32)
        for j in range(1, nq):
            acc = acc + r_ref[j].astype(F32)
        o_ref[...] = acc

    return pl.pallas_call(
        body, name=name, grid=(xr // tr,),
        in_specs=[pl.BlockSpec((nq, tr, yc), lambda i: (0, i, 0))],
        out_specs=pl.BlockSpec((tr, yc), lambda i: (i, 0)),
        out_shape=SDS((xr, yc), F32),
        compiler_params=_cparams("parallel"),
    )(r2)


def _row(v):
    return v.reshape(1, -1)


def _forward_layer(x, p, dims, alpha, l):
    b, s = dims["b"], dims["s"]
    t, d = x.shape
    tag = "l%d_" % l
    h1, a1 = _ffn_up_fwd(x, p["wup1"], tag + "ffn1_up")
    z1, x1 = _res_ln_fwd([a1], p["wd1"], x, p["ln1_g"], p["ln1_b"], alpha, 0.5, tag + "ffn1_down")
    pc, pq, ps = _mix_proj_fwd(x1, p["win"], dims["proj_widths"], tag + "mix_proj")
    x1_3 = x1.reshape(b, s, d)
    flog, cum = _cum_fwd(x1_3, p["wft"], p["bf"], tag + "fox_gate")
    nh = flog.shape[1]
    cum4 = cum.reshape(b, nh // 2, 2, s)
    pc3 = pc.reshape(b, s, -1)
    pq3 = pq.reshape(b, s, -1)
    ya = _conv_fwd(pc3, p["cw"], tag + "conv").reshape(t, -1)
    yb = _fox_fwd(pq3, cum4, tag + "fox").reshape(t, -1)
    yc = _sgu_fwd(ps, p["sgu_g"], p["sgu_b"], p["ws"], p["bs"], tag + "sgu")
    z2, x2 = _res_ln_fwd([ya, yb, yc], p["wout"], x1, p["ln2_g"], p["ln2_b"], alpha, 1.0, tag + "mix_out")
    h2, a2 = _ffn_up_fwd(x2, p["wup2"], tag + "ffn2_up")
    z3, x3 = _res_ln_fwd([a2], p["wd2"], x2, p["ln3_g"], p["ln3_b"], alpha, 0.5, tag + "ffn2_down")
    saved = dict(x=x, h1=h1, a1=a1, z1=z1, x1=x1, pc3=pc3, pq3=pq3, ps=ps, flog=flog, cum4=cum4,
                 ya=ya, yb=yb, yc=yc, z2=z2, x2=x2, h2=h2, a2=a2, z3=z3)
    return x3, saved


def _ffn_backward(dy, z, gamma, wd, wup, h, a, x_in, alpha, tag):
    dz, df, dh, dgam, dbet = _ffn_bwd_mid(dy, z, gamma, wd, h, tag + "_bwd_mid")
    dx = _ffn_bwd_dx(dh, wup, dz, alpha, tag + "_bwd_dx")
    nq, d, w = wup.shape
    dwup = _dw(x_in, dh, d, w, tag + "_dw_up")[0]
    half = wd.shape[0] // 2
    dwd = _dw(a, df[None], half, d, tag + "_dw_down").reshape(nq, wd.shape[0] // nq, d)
    return dx, dwup, dwd, dgam, dbet


def _backward_layer(dy, sv, p, dims, alpha, l):
    b, s = dims["b"], dims["s"]
    tag = "l%d_" % l
    t, d = dy.shape
    g = {}
    dx2, g["ffn2_w_up"], g["ffn2_w_down"], g["ln3_g"], g["ln3_b"] = _ffn_backward(
        dy, sv["z3"], p["ln3_g"], p["wd2"], p["wup2"], sv["h2"], sv["a2"], sv["x2"], alpha, tag + "ffn2")
    wa, wb, wc = sv["ya"].shape[1], sv["yb"].shape[1], sv["yc"].shape[1]
    dz2, dz2b, dya, dyb, dyc, g["ln2_g"], g["ln2_b"] = _out_bwd(dx2, sv["z2"], p["ln2_g"], p["wout"], (wa, wb, wc),
                                                               tag + "mix_out_bwd")
    dz2b3 = dz2b[None]
    g["mix_w_out"] = jnp.concatenate(
        [_dw(sv[k], dz2b3, sv[k].shape[1], d, tag + "dw_out_" + k)[0, 0] for k in ("ya", "yb", "yc")], axis=0)
    dpc3, g["conv_w"] = _conv_bwd(sv["pc3"], dya.reshape(b, s, -1), p["cw"], tag + "conv_bwd")
    dq3, dk3, dv3, dcum4 = _fox_bwd(sv["pq3"], sv["cum4"], dyb.reshape(b, s, -1), tag + "fox_bwd")
    nh = sv["flog"].shape[1]
    dflog, dbf, dwft = _cum_bwd(dcum4.reshape(b, nh, s), sv["flog"], sv["x1"].reshape(b, s, d), tag + "fox_gate_bwd")
    g["fox_b_f"] = dbf[:, 0]
    dps, g["sgu_w_s"], dbs, g["sgu_ln_g"], g["sgu_ln_b"] = _sgu_bwd(
        sv["ps"], dyc, p["sgu_g"], p["sgu_b"], p["ws"], p["bs"], tag + "sgu_bwd")
    g["sgu_b_s"] = dbs[:, :, 0]
    dpc = dpc3.reshape(t, -1)
    dq, dk, dv = dq3.reshape(t, -1), dk3.reshape(t, -1), dv3.reshape(t, -1)
    dx1 = _mix_bwd_dx(dz2, dpc, dq, dk, dv, dps, dflog, p["win"], p["wft"], s, alpha, tag + "mix_bwd_dx")
    x1 = sv["x1"]
    cols = [_dw(x1, m[None], d, m.shape[1], tag + "dw_in_" + k)[0, 0]
            for k, m in (("conv", dpc), ("q", dq), ("k", dk), ("v", dv), ("sgu", dps))]
    g["mix_w_in"] = jnp.concatenate(cols[:4] + [dwft.T, cols[4]], axis=1)
    dx0, g["ffn1_w_up"], g["ffn1_w_down"], g["ln1_g"], g["ln1_b"] = _ffn_backward(
        dx1, sv["z1"], p["ln1_g"], p["wd1"], p["wup1"], sv["h1"], sv["a1"], sv["x"], alpha, tag + "ffn1")
    return dx0, g


BIG = ("ffn1_w_up", "ffn1_w_down", "mix_w_in", "mix_w_out", "ffn2_w_up", "ffn2_w_down")
SMALL = ("ln1_g", "ln1_b", "fox_b_f", "sgu_ln_g", "sgu_ln_b", "sgu_w_s", "sgu_b_s", "ln2_g", "ln2_b", "ln3_g", "ln3_b")
ORDER = ("ln1_g", "ln1_b", "ffn1_w_up", "ffn1_w_down", "mix_w_in", "fox_b_f", "conv_w", "sgu_ln_g", "sgu_ln_b",
         "sgu_w_s", "sgu_b_s", "mix_w_out", "ln2_g", "ln2_b", "ffn2_w_up", "ffn2_w_down", "ln3_g", "ln3_b")


def _pack_rows(flat_list):
    v = jnp.concatenate(flat_list)
    n = v.shape[0]
    pad = (-n) % 1024
    return jnp.pad(v, (0, pad)).reshape(-1, 128)


def kernel(x, ln1_g, ln1_b, ffn1_w_up, ffn1_w_down, mix_w_in, fox_b_f, conv_w, sgu_ln_g, sgu_ln_b, sgu_w_s, sgu_b_s, mix_w_out, ln2_g, ln2_b, ffn2_w_up, ffn2_w_down, ln3_g, ln3_b, loss_target, m_ln1_g, m_ln1_b, m_ffn1_w_up, m_ffn1_w_down, m_mix_w_in, m_fox_b_f, m_conv_w, m_sgu_ln_g, m_sgu_ln_b, m_sgu_w_s, m_sgu_b_s, m_mix_w_out, m_ln2_g, m_ln2_b, m_ffn2_w_up, m_ffn2_w_down, m_ln3_g, m_ln3_b, v_ln1_g, v_ln1_b, v_ffn1_w_up, v_ffn1_w_down, v_mix_w_in, v_fox_b_f, v_conv_w, v_sgu_ln_g, v_sgu_ln_b, v_sgu_w_s, v_sgu_b_s, v_mix_w_out, v_ln2_g, v_ln2_b, v_ffn2_w_up, v_ffn2_w_down, v_ln3_g, v_ln3_b):
    wts = dict(ln1_g=ln1_g, ln1_b=ln1_b, ffn1_w_up=ffn1_w_up, ffn1_w_down=ffn1_w_down, mix_w_in=mix_w_in,
               fox_b_f=fox_b_f, conv_w=conv_w, sgu_ln_g=sgu_ln_g, sgu_ln_b=sgu_ln_b, sgu_w_s=sgu_w_s,
               sgu_b_s=sgu_b_s, mix_w_out=mix_w_out, ln2_g=ln2_g, ln2_b=ln2_b, ffn2_w_up=ffn2_w_up,
               ffn2_w_down=ffn2_w_down, ln3_g=ln3_g, ln3_b=ln3_b)
    mom = dict(ln1_g=m_ln1_g, ln1_b=m_ln1_b, ffn1_w_up=m_ffn1_w_up, ffn1_w_down=m_ffn1_w_down, mix_w_in=m_mix_w_in,
               fox_b_f=m_fox_b_f, conv_w=m_conv_w, sgu_ln_g=m_sgu_ln_g, sgu_ln_b=m_sgu_ln_b, sgu_w_s=m_sgu_w_s,
               sgu_b_s=m_sgu_b_s, mix_w_out=m_mix_w_out, ln2_g=m_ln2_g, ln2_b=m_ln2_b, ffn2_w_up=m_ffn2_w_up,
               ffn2_w_down=m_ffn2_w_down, ln3_g=m_ln3_g, ln3_b=m_ln3_b)
    var = dict(ln1_g=v_ln1_g, ln1_b=v_ln1_b, ffn1_w_up=v_ffn1_w_up, ffn1_w_down=v_ffn1_w_down, mix_w_in=v_mix_w_in,
               fox_b_f=v_fox_b_f, conv_w=v_conv_w, sgu_ln_g=v_sgu_ln_g, sgu_ln_b=v_sgu_ln_b, sgu_w_s=v_sgu_w_s,
               sgu_b_s=v_sgu_b_s, mix_w_out=v_mix_w_out, ln2_g=v_ln2_g, ln2_b=v_ln2_b, ffn2_w_up=v_ffn2_w_up,
               ffn2_w_down=v_ffn2_w_down, ln3_g=v_ln3_g, ln3_b=v_ln3_b)

    nl = ln1_g.shape[0]
    assert nl == 2, "core c of a chip owns layer c in the weight exchanges"
    b, s, d = x.shape
    t = b * s
    nsh = 4
    alpha = (2 * nl) ** 0.25
    d_conv = conv_w.shape[2] * nsh
    d_sgu = sgu_ln_g.shape[1]
    nh = fox_b_f.shape[1]
    d_fox = nh * FOX_HEAD_DIM
    n_main = 3 * d_conv + 3 * d_fox
    dims = dict(b=b, s=s, proj_widths=(3 * d_conv, 3 * d_fox, 2 * d_sgu))
    qx = 2 * lax.axis_index("x") + lax.axis_index("y")
    cidx = lax.axis_index("c").astype(jnp.int32).reshape(1)

    cw_sh = conv_w.shape[2]
    conv_tile = jnp.pad(conv_w, ((0, 0), (0, 8 - conv_w.shape[1]), (0, 128 - cw_sh)))
    gathered = _all_gather_weights([wts[k].astype(BF) for k in BIG] + [conv_tile], "gather_weights")
    gw = dict(zip(BIG, gathered[:-1]))
    conv_full = jnp.moveaxis(gathered[-1][:, :, :3, :cw_sh], 1, 2).reshape(nl, 3, d_conv)

    params = []
    for l in range(nl):
        w_in = jnp.moveaxis(gw["mix_w_in"][l], 0, 1).reshape(d, -1)
        params.append(dict(
            wup1=gw["ffn1_w_up"][l], wd1=gw["ffn1_w_down"][l].reshape(-1, d),
            wup2=gw["ffn2_w_up"][l], wd2=gw["ffn2_w_down"][l].reshape(-1, d),
            wout=gw["mix_w_out"][l].reshape(-1, d),
            win=jnp.concatenate([w_in[:, :n_main], w_in[:, n_main + nh:]], axis=1),
            wft=jnp.pad(w_in[:, n_main:n_main + nh].T, ((0, HEAD_ROWS - nh), (0, 0))),
            bf=fox_b_f[l].reshape(nh, 1), cw=conv_full[l],
            sgu_g=_row(sgu_ln_g[l]), sgu_b=_row(sgu_ln_b[l]), ws=sgu_w_s[l], bs=sgu_b_s[l][:, :, None],
            ln1_g=_row(ln1_g[l]), ln1_b=_row(ln1_b[l]), ln2_g=_row(ln2_g[l]), ln2_b=_row(ln2_b[l]),
            ln3_g=_row(ln3_g[l]), ln3_b=_row(ln3_b[l])))

    act = x.reshape(t, d)
    saved = []
    for l in range(nl):
        act, sv = _forward_layer(act, params[l], dims, alpha, l)
        saved.append(sv)
    dy, loss_blk = _loss_fwd(act, loss_target.reshape(t, d), "loss")
    grads = [None] * nl
    for l in reversed(range(nl)):
        dy, grads[l] = _backward_layer(dy, saved[l], params[l], dims, alpha, l)
    grad_x = dy.reshape(b, s, d)

    def shard_major(k, g):
        if k == "mix_w_in":
            return jnp.moveaxis(g.reshape(d, nsh, -1), 1, 0)
        if k == "mix_w_out":
            return g.reshape(nsh, -1, d)
        return g
    full = [jnp.stack([shard_major(k, grads[l][k]) for l in range(nl)]) for k in BIG]
    recv1 = _pair_send_other_layer([g.astype(BF) for g in full], "rs_pair_send")
    part = [_pair_sum(g, r, cidx, "rs_pair_sum_" + k) for k, g, r in zip(BIG, full, recv1)]
    recv2 = _scatter_to_chips(part, "rs_scatter")
    tot = [_chip_sum(r, "rs_chip_sum_" + k) for k, r in zip(BIG, recv2)]
    gfin = dict(zip(BIG, _pair_share_totals(tot, "rs_pair_share")))

    flat = []
    for l in range(nl):
        flat += [grads[l][k].reshape(-1) for k in SMALL] + [grads[l]["conv_w"].reshape(-1)]
    flat.append(loss_blk[0, 0:1])
    vec = _pack_rows(flat)
    red = _all_reduce_small(vec, "all_reduce_small").reshape(-1)
    gsm = {k: [] for k in SMALL + ("conv_w",)}
    off = 0
    for l in range(nl):
        for k in SMALL:
            n = wts[k][l].size
            gsm[k].append(red[off:off + n].reshape(wts[k][l].shape))
            off += n
        n = 3 * d_conv
        gsm["conv_w"].append(lax.dynamic_slice_in_dim(red[off:off + n].reshape(3, d_conv), qx * cw_sh, cw_sh, axis=1))
        off += n
    loss = red[off]
    for k in gsm:
        gfin[k] = jnp.stack(gsm[k])

    delta, new_m, new_v = {}, {}, {}
    for k in BIG:
        shp = wts[k].shape
        two = lambda a: a.reshape(-1, shp[-1])
        dl, nm, nv = _adamw(two(wts[k]), two(gfin[k]), two(mom[k]), two(var[k]), "adamw_" + k)
        delta[k], new_m[k], new_v[k] = dl.reshape(shp), nm.reshape(shp), nv.reshape(shp)
        gfin[k] = gfin[k].reshape(shp)
    small_keys = SMALL + ("conv_w",)
    sizes = [wts[k].size for k in small_keys]
    pk = lambda src: _pack_rows([src[k].reshape(-1) for k in small_keys])
    dl, nm, nv = _adamw(pk(wts), pk(gfin), pk(mom), pk(var), "adamw_small")
    off = 0
    for k, n in zip(small_keys, sizes):
        shp = wts[k].shape
        delta[k] = dl.reshape(-1)[off:off + n].reshape(shp)
        new_m[k] = nm.reshape(-1)[off:off + n].reshape(shp)
        new_v[k] = nv.reshape(-1)[off:off + n].reshape(shp)
        off += n

    return (loss, grad_x, *[gfin[k] for k in ORDER], *[delta[k] for k in ORDER],
            *[new_m[k] for k in ORDER], *[new_v[k] for k in ORDER])
```

```python
import jax
import jax.numpy as jnp
from jax import lax
from jax.experimental import pallas as pl
from jax.experimental.pallas import tpu as pltpu

F32 = jnp.float32
BF = jnp.bfloat16
SDS = jax.ShapeDtypeStruct
MESH = pl.DeviceIdType.MESH

LN_EPS = 1e-5
FOX_HEAD_DIM = 64
FOX_Q_BLOCK = 256
HEAD_ROWS = 128
GELU_K = 0.7978845608028654
GELU_C = 0.044715
NEG_BIG = -1e30
N_SHARDS = 4

ADAM_LR = 0.001
ADAM_B1 = 0.9
ADAM_B2 = 0.999
ADAM_EPS = 1e-08
ADAM_WD = 0.01
ADAM_STEP = 10

VMEM_LIMIT_BYTES = 56 * 1024 * 1024
NT = (((1,), (1,)), ((), ()))
TN = (((0,), (0,)), ((), ()))
HBM = pl.BlockSpec(memory_space=pl.ANY)


def _tile(n, pref, mult=8):
    t = min(n, pref)
    while n % t or t % mult:
        t -= mult
    return t


def _dot(a, b):
    return jnp.dot(a, b, preferred_element_type=F32)


def _dotg(a, b, dims):
    return lax.dot_general(a, b, dims, preferred_element_type=F32)


def _sigmoid(x):
    return 1.0 / (1.0 + jnp.exp(-x))


def _gelu(x):
    return 0.5 * x * (1.0 + jnp.tanh(GELU_K * (x + GELU_C * x * x * x)))


def _gelu_grad(x):
    t = jnp.tanh(GELU_K * (x + GELU_C * x * x * x))
    return 0.5 * (1.0 + t) + 0.5 * x * (1.0 - t * t) * GELU_K * (1.0 + 3.0 * GELU_C * x * x)


def _ln_stats(z):
    mu = jnp.mean(z, axis=-1, keepdims=True)
    zc = z - mu
    var = jnp.mean(zc * zc, axis=-1, keepdims=True)
    rstd = lax.rsqrt(var + LN_EPS)
    return zc * rstd, rstd


def _ln_bwd(dy, z, g):
    xhat, rstd = _ln_stats(z)
    gdy = dy * g
    m1 = jnp.mean(gdy, axis=-1, keepdims=True)
    m2 = jnp.mean(gdy * xhat, axis=-1, keepdims=True)
    dz = rstd * (gdy - m1 - xhat * m2)
    return dz, jnp.sum(dy * xhat, axis=0, keepdims=True), jnp.sum(dy, axis=0, keepdims=True)


class _Side:
    def __init__(self, ins, out_shapes, sems, start, finish):
        self.ins, self.out_shapes, self.sems = list(ins), list(out_shapes), list(sems)
        self.start, self.finish = start, finish
        self.results = None


def _join(sides):
    sides = [s for s in sides if s is not None]
    if not sides:
        return None
    ins = [a for s in sides for a in s.ins]
    outs = [a for s in sides for a in s.out_shapes]
    sems = [a for s in sides for a in s.sems]

    def parts(seq, field):
        out, o = [], 0
        for s in sides:
            n = len(getattr(s, field))
            out.append(seq[o:o + n])
            o += n
        return out

    def run(which):
        def fn(i, o, m):
            for s, a, b, c in zip(sides, parts(i, "ins"), parts(o, "out_shapes"), parts(m, "sems")):
                getattr(s, which)(a, b, c)
        return fn

    joined = _Side(ins, outs, sems, run("start"), run("finish"))
    joined.members = sides
    return joined


def _deliver(side, results):
    members = getattr(side, "members", None)
    side.results = list(results)
    if members:
        o = 0
        for s in members:
            n = len(s.out_shapes)
            _deliver(s, results[o:o + n])
            o += n


def _pcall(body, operands, *, name, grid, in_specs, out_specs, out_shape, sem, scratch_shapes=(),
           prefetch=(), aliases=None, side=None):
    single = not isinstance(out_shape, (list, tuple))
    out_shape = [out_shape] if single else list(out_shape)
    out_specs = [out_specs] if single else list(out_specs)
    in_specs, scratch_shapes = list(in_specs), list(scratch_shapes)
    n_pre, n_in, n_out, n_sc = len(prefetch), len(in_specs), len(out_shape), len(scratch_shapes)
    fn = body
    extra = []
    if side is not None:
        s_in, s_out = len(side.ins), len(side.out_shapes)

        def fn(*refs):
            pre, rest = refs[:n_pre], refs[n_pre:]
            m_in, c_in = rest[:n_in], rest[n_in:n_in + s_in]
            rest = rest[n_in + s_in:]
            m_out, c_out = rest[:n_out], rest[n_out:n_out + s_out]
            rest = rest[n_out + s_out:]
            m_sc, c_sc = rest[:n_sc], rest[n_sc:]
            first = pl.program_id(0) == 0
            last = pl.program_id(0) == grid[0] - 1
            for a in range(1, len(grid)):
                first = jnp.logical_and(first, pl.program_id(a) == 0)
                last = jnp.logical_and(last, pl.program_id(a) == grid[a] - 1)

            @pl.when(first)
            def _():
                side.start(c_in, c_out, c_sc)

            body(*pre, *m_in, *m_out, *m_sc)

            @pl.when(last)
            def _():
                side.finish(c_in, c_out, c_sc)

        in_specs = in_specs + [HBM] * s_in
        out_specs = out_specs + [HBM] * s_out
        out_shape = out_shape + side.out_shapes
        scratch_shapes = scratch_shapes + side.sems
        extra = side.ins
        sem = ("arbitrary",) * len(grid)
    params = pltpu.CompilerParams(dimension_semantics=tuple(sem), vmem_limit_bytes=VMEM_LIMIT_BYTES)
    kw = dict(input_output_aliases=aliases) if aliases else {}
    if n_pre:
        spec = pltpu.PrefetchScalarGridSpec(num_scalar_prefetch=n_pre, grid=grid, in_specs=in_specs,
                                            out_specs=out_specs, scratch_shapes=scratch_shapes)
        call = pl.pallas_call(fn, name=name, grid_spec=spec, out_shape=out_shape, compiler_params=params, **kw)
    else:
        call = pl.pallas_call(fn, name=name, grid=grid, in_specs=in_specs, out_specs=out_specs,
                              out_shape=out_shape, scratch_shapes=scratch_shapes, compiler_params=params, **kw)
    res = call(*prefetch, *operands, *extra)
    if side is not None:
        _deliver(side, res[n_out:])
        res = res[:n_out]
    return res[0] if single else res


def _run_side(side, name):
    def body(*refs):
        n_in, n_out = len(side.ins), len(side.out_shapes)
        i, o, m = refs[:n_in], refs[n_in:n_in + n_out], refs[n_in + n_out:]
        side.start(i, o, m)
        side.finish(i, o, m)

    res = pl.pallas_call(body, name=name, in_specs=[HBM] * len(side.ins), out_specs=[HBM] * len(side.out_shapes),
                         out_shape=side.out_shapes, scratch_shapes=side.sems)(*side.ins)
    _deliver(side, res)


def _mesh_pos():
    x, y, c = lax.axis_index("x"), lax.axis_index("y"), lax.axis_index("c")
    chips = [(1 - x, y), (x, 1 - y), (1 - x, 1 - y)]
    return x, y, c, chips


def _rows(ref, lead, half, n_rows):
    return ref.at[tuple(lead) + (pl.ds(half * n_rows, n_rows),)]


def _side_gather(shards, split):
    n = len(shards)
    hs = [w.shape[0] // 2 for w in shards]

    def plan(ins, outs, sems):
        ssem, rsem = sems
        x, y, c, chips = _mesh_pos()
        q = 2 * x + y
        sib = (x, y, 1 - c)

        def rc(p, k, src, dst, to):
            return pltpu.make_async_remote_copy(src_ref=src, dst_ref=dst, send_sem=ssem.at[p, k],
                                                recv_sem=rsem.at[p, k], device_id=to, device_id_type=MESH)

        def blk(ref, p, qi, half):
            return _rows(ref, (qi,), half, hs[p]) if split[p] else ref.at[qi]

        return x, y, c, chips, q, sib, rc, blk

    def first_sends(ins, outs, sems):
        x, y, c, chips, q, sib, rc, blk = plan(ins, outs, sems)
        cps = [rc(p, 0, ins[p], outs[p].at[q], sib) for p in range(n)]
        for j, (cx, cy) in enumerate(chips):
            for p in range(n):
                src = _rows(ins[p], (), c, hs[p]) if split[p] else ins[p]
                cps.append(rc(p, 1 + j, src, blk(outs[p], p, q, c), (cx, cy, c)))
        return cps

    def start(ins, outs, sems):
        for cp in first_sends(ins, outs, sems):
            cp.start()

    def finish(ins, outs, sems):
        x, y, c, chips, q, sib, rc, blk = plan(ins, outs, sems)
        sent = first_sends(ins, outs, sems)
        for j, (cx, cy) in enumerate(chips):
            qj = 2 * cx + cy
            for p in range(n):
                got = blk(outs[p], p, qj, c)
                rc(p, 1 + j, got, got, (cx, cy, c)).wait_recv()
                if split[p]:
                    fwd = rc(p, 4 + j, got, got, sib)
                    fwd.start()
                    sent.append(fwd)
        for j, (cx, cy) in enumerate(chips):
            qj = 2 * cx + cy
            for p in range(n):
                if split[p]:
                    got = blk(outs[p], p, qj, 1 - c)
                    rc(p, 4 + j, got, got, sib).wait_recv()
        for p in range(n):
            rc(p, 0, outs[p].at[q], outs[p].at[q], sib).wait_recv()
        for cp in sent:
            cp.wait_send()

    return _Side(shards, [SDS((N_SHARDS,) + w.shape, w.dtype) for w in shards],
                 [pltpu.SemaphoreType.DMA((n, 7)), pltpu.SemaphoreType.DMA((n, 7))], start, finish)


def _side_pair_send(gs):
    n = len(gs)

    def copies(ins, outs, sems):
        x, y, c, _ = _mesh_pos()
        return [pltpu.make_async_remote_copy(
            src_ref=ins[p].at[:, pl.ds((1 - c) * (gs[p].shape[1] // 2), gs[p].shape[1] // 2)], dst_ref=outs[p],
            send_sem=sems[0].at[p], recv_sem=sems[1].at[p], device_id=(x, y, 1 - c), device_id_type=MESH)
            for p in range(n)]

    def start(ins, outs, sems):
        for cp in copies(ins, outs, sems):
            cp.start()

    def finish(ins, outs, sems):
        for cp in copies(ins, outs, sems):
            cp.wait()

    return _Side(gs, [SDS((g.shape[0], g.shape[1] // 2, g.shape[2]), g.dtype) for g in gs],
                 [pltpu.SemaphoreType.DMA((n,)), pltpu.SemaphoreType.DMA((n,))], start, finish)


def _side_scatter(ps):
    n = len(ps)

    def sends(ins, outs, sems):
        x, y, c, chips = _mesh_pos()
        q = 2 * x + y
        return [pltpu.make_async_remote_copy(src_ref=ins[p].at[2 * cx + cy], dst_ref=outs[p].at[q],
                                             send_sem=sems[0].at[p, j], recv_sem=sems[1].at[p, j],
                                             device_id=(cx, cy, c), device_id_type=MESH)
                for j, (cx, cy) in enumerate(chips) for p in range(n)]

    def start(ins, outs, sems):
        for cp in sends(ins, outs, sems):
            cp.start()

    def finish(ins, outs, sems):
        x, y, c, chips = _mesh_pos()
        for j, (cx, cy) in enumerate(chips):
            for p in range(n):
                got = outs[p].at[2 * cx + cy]
                pltpu.make_async_remote_copy(src_ref=got, dst_ref=got, send_sem=sems[0].at[p, j],
                                             recv_sem=sems[1].at[p, j], device_id=(cx, cy, c),
                                             device_id_type=MESH).wait_recv()
        for cp in sends(ins, outs, sems):
            cp.wait_send()

    return _Side(ps, [SDS(p.shape, p.dtype) for p in ps],
                 [pltpu.SemaphoreType.DMA((n, 3)), pltpu.SemaphoreType.DMA((n, 3))], start, finish)


def _side_pair_share(tots):
    n = len(tots)

    def copies(ins, outs, sems):
        x, y, c, _ = _mesh_pos()
        return [pltpu.make_async_remote_copy(src_ref=ins[p], dst_ref=outs[p], send_sem=sems[0].at[p],
                                             recv_sem=sems[1].at[p], device_id=(x, y, 1 - c), device_id_type=MESH)
                for p in range(n)]

    def start(ins, outs, sems):
        for cp in copies(ins, outs, sems):
            cp.start()

    def finish(ins, outs, sems):
        for cp in copies(ins, outs, sems):
            cp.wait()

    return _Side(tots, [SDS(t_.shape, t_.dtype) for t_ in tots],
                 [pltpu.SemaphoreType.DMA((n,)), pltpu.SemaphoreType.DMA((n,))], start, finish)


def _all_reduce_small(v, name):
    r = v.shape[0]

    def body(v_ref, o_ref, buf, send_sems, recv_sems):
        x, y, c, _ = _mesh_pos()
        me = 4 * x + 2 * y + c
        buf[me] = v_ref[...]
        sends = []
        for k in range(1, 8):
            dx, dy, dc = (k >> 2) & 1, (k >> 1) & 1, k & 1
            to = (x ^ dx, y ^ dy, c ^ dc)
            sends.append(pltpu.make_async_remote_copy(src_ref=v_ref, dst_ref=buf.at[me], send_sem=send_sems.at[k - 1],
                                                      recv_sem=recv_sems.at[k - 1], device_id=to, device_id_type=MESH))
        for cp in sends:
            cp.start()
        for k in range(1, 8):
            dx, dy, dc = (k >> 2) & 1, (k >> 1) & 1, k & 1
            src = 4 * (x ^ dx) + 2 * (y ^ dy) + (c ^ dc)
            blk = buf.at[src]
            pltpu.make_async_remote_copy(src_ref=blk, dst_ref=blk, send_sem=send_sems.at[k - 1],
                                         recv_sem=recv_sems.at[k - 1], device_id=(x ^ dx, y ^ dy, c ^ dc),
                                         device_id_type=MESH).wait_recv()
        for cp in sends:
            cp.wait_send()
        acc = buf[0]
        for k in range(1, 8):
            acc = acc + buf[k]
        o_ref[...] = acc

    vm = pl.BlockSpec(memory_space=pltpu.VMEM)
    return pl.pallas_call(
        body, name=name, in_specs=[vm], out_specs=vm, out_shape=SDS(v.shape, F32),
        scratch_shapes=[pltpu.VMEM((8, r, 128), F32), pltpu.SemaphoreType.DMA((7,)), pltpu.SemaphoreType.DMA((7,))],
        compiler_params=pltpu.CompilerParams(vmem_limit_bytes=VMEM_LIMIT_BYTES),
    )(v)


def _pair_sum(g, r1, cq, name, side=None):
    nq, xr, yc = g.shape
    h = xr // 2
    tr = _tile(h, 512, 16)
    nt = h // tr

    def body(cq_ref, g_ref, r_ref, o_ref):
        o_ref[...] = (g_ref[...] + r_ref[...]).astype(BF)

    return _pcall(
        body, (g, r1), name=name, grid=(nq, nt), prefetch=(cq,),
        in_specs=[pl.BlockSpec((1, tr, yc), lambda j, i, cq_ref: (j, cq_ref[0] * nt + i, 0)),
                  pl.BlockSpec((1, tr, yc), lambda j, i, cq_ref: (j, i, 0))],
        out_specs=pl.BlockSpec((1, tr, yc), lambda j, i, cq_ref: (j, i, 0)),
        out_shape=SDS((nq, h, yc), BF), sem=("parallel", "parallel"), side=side)


def _chip_sum(p, r2, cq, name, side=None):
    nq, h, yc = r2.shape
    tr = _tile(h, 512, 16)

    def body(cq_ref, p_ref, r_ref, o_ref):
        j = pl.program_id(1)
        term = jnp.where(j == cq_ref[1], p_ref[0], r_ref[0]).astype(F32)

        @pl.when(j == 0)
        def _():
            o_ref[...] = term

        @pl.when(j != 0)
        def _():
            o_ref[...] += term

    other = lambda j, q: jnp.where(j == q, (q + 1) % nq, j)
    return _pcall(
        body, (p, r2), name=name, grid=(h // tr, nq), prefetch=(cq,),
        in_specs=[pl.BlockSpec((1, tr, yc), lambda i, j, cq_ref: (cq_ref[1], i, 0)),
                  pl.BlockSpec((1, tr, yc), lambda i, j, cq_ref: (other(j, cq_ref[1]), i, 0))],
        out_specs=pl.BlockSpec((tr, yc), lambda i, j, cq_ref: (i, 0)),
        out_shape=SDS((h, yc), F32), sem=("parallel", "arbitrary"), side=side)


def _ffn_up_fwd(x, wup, name, side=None):
    t, d = x.shape
    w = wup.shape[2]
    tm = _tile(t, 512)

    def body(x_ref, wg_ref, wu_ref, h_ref, a_ref):
        xb = x_ref[...].astype(BF)
        g = _dot(xb, wg_ref[0])
        u = _dot(xb, wu_ref[0])
        h_ref[0] = g.astype(BF)
        h_ref[1] = u.astype(BF)
        a_ref[...] = (g * _sigmoid(g) * u).astype(BF)

    return _pcall(
        body, (x, wup, wup), name=name, grid=(2, t // tm),
        in_specs=[pl.BlockSpec((tm, d), lambda j, i: (i, 0)),
                  pl.BlockSpec((1, d, w), lambda j, i: (j, 0, 0)),
                  pl.BlockSpec((1, d, w), lambda j, i: (j + 2, 0, 0))],
        out_specs=[pl.BlockSpec((2, tm, w), lambda j, i: (0, i, j)),
                   pl.BlockSpec((tm, w), lambda j, i: (i, j))],
        out_shape=[SDS((2, t, 2 * w), BF), SDS((t, 2 * w), BF)],
        sem=("parallel", "parallel"), side=side)


def _res_ln_fwd(parts, w, x, gamma, beta, alpha, res_scale, name, side=None):
    t, d = x.shape
    n = len(parts)
    offs = [0]
    for p in parts:
        offs.append(offs[-1] + p.shape[1])
    tm = _tile(t, 512)

    def body(*refs):
        p_refs = refs[:n]
        w_ref, x_ref, g_ref, b_ref, z_ref, y_ref = refs[n:]
        f = _dot(p_refs[0][...], w_ref[offs[0]:offs[1], :])
        for k in range(1, n):
            f = f + _dot(p_refs[k][...], w_ref[offs[k]:offs[k + 1], :])
        z = alpha * x_ref[...] + res_scale * f
        z_ref[...] = z
        xhat, _ = _ln_stats(z)
        y_ref[...] = xhat * g_ref[...] + b_ref[...]

    row = lambda i: (i, 0)
    fixed = lambda i: (0, 0)
    return _pcall(
        body, (*parts, w, x, gamma, beta), name=name, grid=(t // tm,),
        in_specs=[pl.BlockSpec((tm, p.shape[1]), row) for p in parts]
        + [pl.BlockSpec(w.shape, fixed), pl.BlockSpec((tm, d), row),
           pl.BlockSpec((1, d), fixed), pl.BlockSpec((1, d), fixed)],
        out_specs=[pl.BlockSpec((tm, d), row), pl.BlockSpec((tm, d), row)],
        out_shape=[SDS((t, d), F32), SDS((t, d), F32)],
        sem=("parallel",), side=side)


def _mix_proj_fwd(x, w_main, widths, name, side=None):
    t, d = x.shape
    dc, dq, ds = widths
    tm = _tile(t, 512)

    def body(x_ref, w_ref, pc_ref, pq_ref, ps_ref):
        xb = x_ref[...].astype(BF)
        pc_ref[...] = _dot(xb, w_ref[:, 0:dc])
        pq_ref[...] = _dot(xb, w_ref[:, dc:dc + dq]).astype(BF)
        ps_ref[...] = _dot(xb, w_ref[:, dc + dq:dc + dq + ds])

    row = lambda i: (i, 0)
    return _pcall(
        body, (x, w_main), name=name, grid=(t // tm,),
        in_specs=[pl.BlockSpec((tm, d), row), pl.BlockSpec(w_main.shape, lambda i: (0, 0))],
        out_specs=[pl.BlockSpec((tm, dc), row), pl.BlockSpec((tm, dq), row), pl.BlockSpec((tm, ds), row)],
        out_shape=[SDS((t, dc), F32), SDS((t, dq), BF), SDS((t, ds), F32)],
        sem=("parallel",), side=side)


def _prefix_sum_lanes(v, reverse):
    n = v.shape[-1]
    lane = lax.broadcasted_iota(jnp.int32, v.shape, v.ndim - 1)
    sh = 1
    while sh < n:
        if reverse:
            v = v + jnp.where(lane < n - sh, pltpu.roll(v, n - sh, axis=v.ndim - 1), 0.0)
        else:
            v = v + jnp.where(lane >= sh, pltpu.roll(v, sh, axis=v.ndim - 1), 0.0)
        sh *= 2
    return v


def _cum_fwd(x3, wft, bf, name, side=None):
    b, s, d = x3.shape
    h = bf.shape[0]

    def body(x_ref, w_ref, b_ref, fl_ref, cum_ref):
        fl = _dotg(w_ref[...], x_ref[0].astype(BF), NT)[0:h] + b_ref[...]
        fl_ref[0] = fl
        lf = jnp.minimum(fl, 0.0) - jnp.log(1.0 + jnp.exp(-jnp.abs(fl)))
        cum_ref[0] = _prefix_sum_lanes(lf, reverse=False)

    return _pcall(
        body, (x3, wft, bf), name=name, grid=(b,),
        in_specs=[pl.BlockSpec((1, s, d), lambda i: (i, 0, 0)),
                  pl.BlockSpec(wft.shape, lambda i: (0, 0)), pl.BlockSpec((h, 1), lambda i: (0, 0))],
        out_specs=[pl.BlockSpec((1, h, s), lambda i: (i, 0, 0)), pl.BlockSpec((1, h, s), lambda i: (i, 0, 0))],
        out_shape=[SDS((b, h, s), F32), SDS((b, h, s), F32)],
        sem=("parallel",), side=side)


def _shift_rows(z, k, down):
    n = z.shape[0]
    row = lax.broadcasted_iota(jnp.int32, z.shape, 0)
    if down:
        return jnp.where(row >= k, pltpu.roll(z, k, axis=0), 0.0)
    return jnp.where(row < n - k, pltpu.roll(z, n - k, axis=0), 0.0)


def _conv_fwd(pc3, cw, name, side=None):
    b, s, c3 = pc3.shape
    c = c3 // 3

    def body(p_ref, w_ref, y_ref):
        z = p_ref[0, :, c:2 * c] * p_ref[0, :, 2 * c:3 * c]
        conv = w_ref[0:1, :] * _shift_rows(z, 2, True) + w_ref[1:2, :] * _shift_rows(z, 1, True) + w_ref[2:3, :] * z
        y_ref[0] = (p_ref[0, :, 0:c] * conv).astype(BF)

    return _pcall(
        body, (pc3, cw), name=name, grid=(b,),
        in_specs=[pl.BlockSpec((1, s, c3), lambda i: (i, 0, 0)), pl.BlockSpec((3, c), lambda i: (0, 0))],
        out_specs=pl.BlockSpec((1, s, c), lambda i: (i, 0, 0)),
        out_shape=SDS((b, s, c), BF), sem=("parallel",), side=side)


def _head_masks(width):
    lane = lax.broadcasted_iota(jnp.int32, (1, width), 1)
    return [lane < FOX_HEAD_DIM, lane >= FOX_HEAD_DIM]


def _fox_probs(qm, k, cum_row, lo):
    tq, hi = qm.shape[0], k.shape[0]
    s = _dotg(qm, k, NT) * (FOX_HEAD_DIM ** -0.5) - cum_row
    col = lax.broadcasted_iota(jnp.int32, (tq, hi), 1)
    rowi = lax.broadcasted_iota(jnp.int32, (tq, hi), 0) + lo
    s = jnp.where(col <= rowi, s, NEG_BIG)
    p = jnp.exp(s - jnp.max(s, axis=-1, keepdims=True))
    return p * (1.0 / jnp.sum(p, axis=-1, keepdims=True))


def _fox_fwd(pq3, cum4, name, side=None):
    b, s, d3 = pq3.shape
    df = d3 // 3
    hp = df // 128
    tq = _tile(s, FOX_Q_BLOCK)

    def body(q_ref, k_ref, v_ref, c_ref, o_ref):
        masks = _head_masks(128)
        for i in range(s // tq):
            lo, hi = i * tq, (i + 1) * tq
            q = q_ref[0, lo:hi, :]
            k = k_ref[0, 0:hi, :]
            v = v_ref[0, 0:hi, :]
            o = jnp.zeros((tq, 128), F32)
            for e in range(2):
                p = _fox_probs(jnp.where(masks[e], q, 0), k, c_ref[0, 0, e:e + 1, 0:hi], lo)
                o = jnp.where(masks[e], _dot(p.astype(BF), v), o)
            o_ref[0, lo:hi, :] = o.astype(BF)

    blk = lambda off: pl.BlockSpec((1, s, 128), lambda i, j: (i, 0, off + j))
    return _pcall(
        body, (pq3, pq3, pq3, cum4), name=name, grid=(b, hp),
        in_specs=[blk(0), blk(hp), blk(2 * hp), pl.BlockSpec((1, 1, 2, s), lambda i, j: (i, j, 0, 0))],
        out_specs=blk(0), out_shape=SDS((b, s, df), BF),
        sem=("parallel", "parallel"), side=side)


def _sgu_mix(wm, vnb, bias, gmasks):
    out = bias
    for g in range(len(wm)):
        out = out + jnp.where(gmasks[g], _dot(wm[g], vnb), 0.0)
    return out


def _sgu_consts(ws_ref, bs_ref, ds):
    ng, c, _ = ws_ref.shape
    gd = ds // ng
    tri = lax.broadcasted_iota(jnp.int32, (c, c), 0) >= lax.broadcasted_iota(jnp.int32, (c, c), 1)
    lane = lax.broadcasted_iota(jnp.int32, (1, ds), 1)
    gmasks = [(lane >= g * gd) & (lane < (g + 1) * gd) for g in range(ng)]
    wm = [jnp.where(tri, ws_ref[g], 0.0).astype(BF) for g in range(ng)]
    bias = jnp.zeros((c, ds), F32)
    for g in range(ng):
        bias = jnp.where(gmasks[g], bs_ref[g], bias)
    return tri, gmasks, wm, bias


def _sgu_fwd(ps, lng, lnb, ws, bs, name, side=None):
    t, ds2 = ps.shape
    ds = ds2 // 2
    c = ws.shape[1]
    tm = _tile(t, 512, c)

    def body(p_ref, g_ref, b_ref, ws_ref, bs_ref, y_ref):
        _, gmasks, wm, bias = _sgu_consts(ws_ref, bs_ref, ds)
        up = _gelu(p_ref[:, 0:ds])
        xhat, _ = _ln_stats(_gelu(p_ref[:, ds:ds2]))
        vnb = (xhat * g_ref[...] + b_ref[...]).astype(BF)
        for n in range(tm // c):
            r0, r1 = n * c, (n + 1) * c
            y_ref[r0:r1, :] = (up[r0:r1] * _sgu_mix(wm, vnb[r0:r1], bias, gmasks)).astype(BF)

    fixed2 = lambda i: (0, 0)
    fixed3 = lambda i: (0, 0, 0)
    return _pcall(
        body, (ps, lng, lnb, ws, bs), name=name, grid=(t // tm,),
        in_specs=[pl.BlockSpec((tm, ds2), lambda i: (i, 0)), pl.BlockSpec((1, ds), fixed2),
                  pl.BlockSpec((1, ds), fixed2), pl.BlockSpec(ws.shape, fixed3), pl.BlockSpec(bs.shape, fixed3)],
        out_specs=pl.BlockSpec((tm, ds), lambda i: (i, 0)),
        out_shape=SDS((t, ds), BF), sem=("parallel",), side=side)


def _loss_fwd(y, target, name, side=None):
    t, d = y.shape
    tm = _tile(t, 512)

    def body(y_ref, t_ref, dy_ref, l_ref):
        @pl.when(pl.program_id(0) == 0)
        def _():
            l_ref[...] = jnp.zeros_like(l_ref)

        err = y_ref[...] - t_ref[...]
        dy_ref[...] = err * (1.0 / d)
        l_ref[...] += 0.5 * jnp.sum(jnp.sum(err * err, axis=-1, keepdims=True) * (1.0 / d), axis=0, keepdims=True)

    row = lambda i: (i, 0)
    return _pcall(
        body, (y, target), name=name, grid=(t // tm,),
        in_specs=[pl.BlockSpec((tm, d), row), pl.BlockSpec((tm, d), row)],
        out_specs=[pl.BlockSpec((tm, d), row), pl.BlockSpec((8, 128), lambda i: (0, 0))],
        out_shape=[SDS((t, d), F32), SDS((8, 128), F32)],
        sem=("arbitrary",), side=side)


def _acc_rows(ref, val, first):
    @pl.when(first)
    def _():
        ref[...] = val

    @pl.when(jnp.logical_not(first))
    def _():
        ref[...] += val


def _ffn_bwd_mid(dy, z, gamma, wd, h, name, side=None):
    t, d = dy.shape
    dff = wd.shape[0]
    half = dff // 2
    tm = _tile(t, 256)

    def body(dy_ref, z_ref, g_ref, wd_ref, h_ref, dz_ref, df_ref, dh_ref, dg_ref, db_ref):
        dz, dgam, dbet = _ln_bwd(dy_ref[...], z_ref[...], g_ref[...])
        first = pl.program_id(0) == 0
        _acc_rows(dg_ref, dgam, first)
        _acc_rows(db_ref, dbet, first)
        dz_ref[...] = dz
        dfb = (0.5 * dz).astype(BF)
        df_ref[...] = dfb
        for j in range(2):
            c0, c1 = j * half, (j + 1) * half
            da = _dotg(dfb, wd_ref[c0:c1, :], NT)
            g = h_ref[0, :, c0:c1].astype(F32)
            u = h_ref[1, :, c0:c1].astype(F32)
            sg = _sigmoid(g)
            dh_ref[0, :, c0:c1] = (da * u * sg * (1.0 + g * (1.0 - sg))).astype(BF)
            dh_ref[1, :, c0:c1] = (da * g * sg).astype(BF)

    row = lambda i: (i, 0)
    fixed = lambda i: (0, 0)
    return _pcall(
        body, (dy, z, gamma, wd, h), name=name, grid=(t // tm,),
        in_specs=[pl.BlockSpec((tm, d), row), pl.BlockSpec((tm, d), row), pl.BlockSpec((1, d), fixed),
                  pl.BlockSpec(wd.shape, fixed), pl.BlockSpec((2, tm, dff), lambda i: (0, i, 0))],
        out_specs=[pl.BlockSpec((tm, d), row), pl.BlockSpec((tm, d), row),
                   pl.BlockSpec((2, tm, dff), lambda i: (0, i, 0)),
                   pl.BlockSpec((1, d), fixed), pl.BlockSpec((1, d), fixed)],
        out_shape=[SDS((t, d), F32), SDS((t, d), BF), SDS((2, t, dff), BF), SDS((1, d), F32), SDS((1, d), F32)],
        sem=("arbitrary",), side=side)


def _ffn_bwd_dx(dh, wup, dz, alpha, name, side=None):
    _, t, dff = dh.shape
    nq, d, w = wup.shape
    per = dff // w
    tm = _tile(t, 256)

    def body(dh_ref, w_ref, dz_ref, dx_ref):
        acc = alpha * dz_ref[...]
        for q in range(nq):
            c0 = (q % per) * w
            acc = acc + _dotg(dh_ref[q // per, :, c0:c0 + w], w_ref[q], NT)
        dx_ref[...] = acc

    row = lambda i: (i, 0)
    return _pcall(
        body, (dh, wup, dz), name=name, grid=(t // tm,),
        in_specs=[pl.BlockSpec((2, tm, dff), lambda i: (0, i, 0)), pl.BlockSpec(wup.shape, lambda i: (0, 0, 0)),
                  pl.BlockSpec((tm, d), row)],
        out_specs=pl.BlockSpec((tm, d), row), out_shape=SDS((t, d), F32),
        sem=("parallel",), side=side)


def _dw(a, b3, ka, nb, name, side=None):
    t, ka_tot = a.shape
    gb, _, nb_tot = b3.shape
    na, ncb = ka_tot // ka, nb_tot // nb
    tm = _tile(t, 512)

    def body(a_ref, b_ref, o_ref):
        part = _dotg(a_ref[...].astype(BF), b_ref[0], TN)

        @pl.when(pl.program_id(2) == 0)
        def _():
            o_ref[0, 0] = part

        @pl.when(pl.program_id(2) != 0)
        def _():
            o_ref[0, 0] += part

    return _pcall(
        body, (a, b3), name=name, grid=(na, gb * ncb, t // tm),
        in_specs=[pl.BlockSpec((tm, ka), lambda ja, jb, i: (i, ja)),
                  pl.BlockSpec((1, tm, nb), lambda ja, jb, i: (jb // ncb, i, jb % ncb))],
        out_specs=pl.BlockSpec((1, 1, ka, nb), lambda ja, jb, i: (ja, jb, 0, 0)),
        out_shape=SDS((na, gb * ncb, ka, nb), F32),
        sem=("parallel", "parallel", "arbitrary"), side=side)


def _out_bwd(dy, z, gamma, wout, widths, name, side=None):
    t, d = dy.shape
    wa, wb, wc = widths
    tm = _tile(t, 512)

    def body(dy_ref, z_ref, g_ref, w_ref, dz_ref, dzb_ref, da_ref, dbb_ref, dc_ref, dg_ref, db_ref):
        dz, dgam, dbet = _ln_bwd(dy_ref[...], z_ref[...], g_ref[...])
        first = pl.program_id(0) == 0
        _acc_rows(dg_ref, dgam, first)
        _acc_rows(db_ref, dbet, first)
        dz_ref[...] = dz
        dzb = dz.astype(BF)
        dzb_ref[...] = dzb
        da_ref[...] = _dotg(dzb, w_ref[0:wa, :], NT).astype(BF)
        dbb_ref[...] = _dotg(dzb, w_ref[wa:wa + wb, :], NT).astype(BF)
        dc_ref[...] = _dotg(dzb, w_ref[wa + wb:wa + wb + wc, :], NT).astype(BF)

    row = lambda i: (i, 0)
    fixed = lambda i: (0, 0)
    return _pcall(
        body, (dy, z, gamma, wout), name=name, grid=(t // tm,),
        in_specs=[pl.BlockSpec((tm, d), row), pl.BlockSpec((tm, d), row), pl.BlockSpec((1, d), fixed),
                  pl.BlockSpec(wout.shape, fixed)],
        out_specs=[pl.BlockSpec((tm, d), row), pl.BlockSpec((tm, d), row), pl.BlockSpec((tm, wa), row),
                   pl.BlockSpec((tm, wb), row), pl.BlockSpec((tm, wc), row),
                   pl.BlockSpec((1, d), fixed), pl.BlockSpec((1, d), fixed)],
        out_shape=[SDS((t, d), F32), SDS((t, d), BF), SDS((t, wa), BF), SDS((t, wb), BF), SDS((t, wc), BF),
                   SDS((1, d), F32), SDS((1, d), F32)],
        sem=("arbitrary",), side=side)


def _conv_bwd(pc3, dya3, cw, name, side=None):
    b, s, c3 = pc3.shape
    c = c3 // 3

    def body(p_ref, dy_ref, w_ref, dp_ref, dw_ref):
        cb = p_ref[0, :, 0:c]
        cc = p_ref[0, :, c:2 * c]
        ch = p_ref[0, :, 2 * c:3 * c]
        z = cc * ch
        z1 = _shift_rows(z, 1, True)
        z2 = _shift_rows(z, 2, True)
        w0, w1, w2 = w_ref[0:1, :], w_ref[1:2, :], w_ref[2:3, :]
        dy = dy_ref[0].astype(F32)
        dconv = dy * cb
        dz = w2 * dconv + w1 * _shift_rows(dconv, 1, False) + w0 * _shift_rows(dconv, 2, False)
        dp_ref[0, :, 0:c] = (dy * (w0 * z2 + w1 * z1 + w2 * z)).astype(BF)
        dp_ref[0, :, c:2 * c] = (dz * ch).astype(BF)
        dp_ref[0, :, 2 * c:3 * c] = (dz * cc).astype(BF)
        first = pl.program_id(0) == 0
        for r, zs in enumerate((z2, z1, z)):
            _acc_rows(dw_ref.at[r:r + 1], jnp.sum(dconv * zs, axis=0, keepdims=True), first)

    blk = lambda i: (i, 0, 0)
    return _pcall(
        body, (pc3, dya3, cw), name=name, grid=(b,),
        in_specs=[pl.BlockSpec((1, s, c3), blk), pl.BlockSpec((1, s, c), blk), pl.BlockSpec((3, c), lambda i: (0, 0))],
        out_specs=[pl.BlockSpec((1, s, c3), blk), pl.BlockSpec((3, c), lambda i: (0, 0))],
        out_shape=[SDS((b, s, c3), BF), SDS((3, c), F32)],
        sem=("arbitrary",), side=side)


def _fox_bwd(pq3, cum4, dyb3, name, side=None):
    b, s, d3 = pq3.shape
    df = d3 // 3
    hp = df // 128
    tq = _tile(s, FOX_Q_BLOCK)
    scale = FOX_HEAD_DIM ** -0.5

    def body(q_ref, k_ref, v_ref, c_ref, do_ref, dq_ref, dk_ref, dv_ref, dc_ref, dk_acc, dv_acc):
        masks = _head_masks(128)
        dk_acc[...] = jnp.zeros_like(dk_acc)
        dv_acc[...] = jnp.zeros_like(dv_acc)
        dc_ref[...] = jnp.zeros_like(dc_ref)
        for i in range(s // tq):
            lo, hi = i * tq, (i + 1) * tq
            q = q_ref[0, lo:hi, :]
            do = do_ref[0, lo:hi, :]
            k = k_ref[0, 0:hi, :]
            v = v_ref[0, 0:hi, :]
            dq = jnp.zeros((tq, 128), F32)
            for e in range(2):
                qm = jnp.where(masks[e], q, 0)
                dom = jnp.where(masks[e], do, 0)
                p = _fox_probs(qm, k, c_ref[0, 0, e:e + 1, 0:hi], lo)
                dp = _dotg(dom, v, NT)
                ds = p * (dp - jnp.sum(p * dp, axis=-1, keepdims=True))
                dsb = ds.astype(BF)
                dq = jnp.where(masks[e], _dot(dsb, k) * scale, dq)
                dk_acc[0:hi, :] += _dotg(dsb, qm, TN) * scale
                dv_acc[0:hi, :] += _dotg(p.astype(BF), dom, TN)
                dc_ref[0, 0, e:e + 1, 0:hi] -= jnp.sum(ds, axis=0, keepdims=True)
            dq_ref[0, lo:hi, :] = dq.astype(BF)
        dk_ref[0] = dk_acc[...].astype(BF)
        dv_ref[0] = dv_acc[...].astype(BF)

    blk = lambda off: pl.BlockSpec((1, s, 128), lambda i, j: (i, 0, off + j))
    cblk = pl.BlockSpec((1, 1, 2, s), lambda i, j: (i, j, 0, 0))
    return _pcall(
        body, (pq3, pq3, pq3, cum4, dyb3), name=name, grid=(b, hp),
        in_specs=[blk(0), blk(hp), blk(2 * hp), cblk, blk(0)],
        out_specs=[blk(0), blk(0), blk(0), cblk],
        out_shape=[SDS((b, s, df), BF), SDS((b, s, df), BF), SDS((b, s, df), BF), SDS(cum4.shape, F32)],
        scratch_shapes=[pltpu.VMEM((s, 128), F32), pltpu.VMEM((s, 128), F32)],
        sem=("parallel", "parallel"), side=side)


def _cum_bwd(dcum, flog, x3, name, side=None):
    b, h, s = dcum.shape
    d = x3.shape[2]

    def body(dc_ref, fl_ref, x_ref, dfl_ref, dbf_ref, dwf_ref):
        dfl = _prefix_sum_lanes(dc_ref[0], reverse=True) * _sigmoid(-fl_ref[0])
        dfl_ref[0] = dfl
        first = pl.program_id(0) == 0
        _acc_rows(dbf_ref, jnp.broadcast_to(jnp.sum(dfl, axis=-1, keepdims=True), (h, 128)), first)
        dflp = jnp.concatenate([dfl, jnp.zeros((HEAD_ROWS - h, s), F32)], axis=0).astype(BF)
        _acc_rows(dwf_ref, _dot(dflp, x_ref[0].astype(BF))[0:h], first)

    blk = lambda i: (i, 0, 0)
    return _pcall(
        body, (dcum, flog, x3), name=name, grid=(b,),
        in_specs=[pl.BlockSpec((1, h, s), blk), pl.BlockSpec((1, h, s), blk), pl.BlockSpec((1, s, d), blk)],
        out_specs=[pl.BlockSpec((1, h, s), blk), pl.BlockSpec((h, 128), lambda i: (0, 0)),
                   pl.BlockSpec((h, d), lambda i: (0, 0))],
        out_shape=[SDS((b, h, s), F32), SDS((h, 128), F32), SDS((h, d), F32)],
        sem=("arbitrary",), side=side)


def _sgu_bwd(ps, dyc, lng, lnb, ws, bs, name, side=None):
    t, ds2 = ps.shape
    ds = ds2 // 2
    ng, c, _ = ws.shape
    tm = _tile(t, 512, c)

    def body(p_ref, dy_ref, g_ref, b_ref, ws_ref, bs_ref, dp_ref, dws_ref, dbs_ref, dg_ref, db_ref, dvn_acc):
        tri, gmasks, wm, bias = _sgu_consts(ws_ref, bs_ref, ds)
        su = p_ref[:, 0:ds]
        sv = p_ref[:, ds:ds2]
        up = _gelu(su)
        gv = _gelu(sv)
        xhat, rstd = _ln_stats(gv)
        vnb = (xhat * g_ref[...] + b_ref[...]).astype(BF)
        dy = dy_ref[...].astype(F32)
        dws = [jnp.zeros((c, c), F32) for _ in range(ng)]
        dbs = [jnp.zeros((c, 1), F32) for _ in range(ng)]
        for n in range(tm // c):
            r0, r1 = n * c, (n + 1) * c
            mixed = _sgu_mix(wm, vnb[r0:r1], bias, gmasks)
            dp_ref[r0:r1, 0:ds] = (dy[r0:r1] * mixed * _gelu_grad(su[r0:r1])).astype(BF)
            dmix = dy[r0:r1] * up[r0:r1]
            dvn = jnp.zeros((c, ds), F32)
            for g in range(ng):
                dmg = jnp.where(gmasks[g], dmix, 0.0)
                dmb = dmg.astype(BF)
                dws[g] = dws[g] + _dotg(dmb, vnb[r0:r1], NT)
                dbs[g] = dbs[g] + jnp.sum(dmg, axis=-1, keepdims=True)
                dvn = dvn + _dotg(wm[g], dmb, TN)
            dvn_acc[r0:r1, :] = dvn
        dvn_all = dvn_acc[...]
        gdv = dvn_all * g_ref[...]
        m1 = jnp.mean(gdv, axis=-1, keepdims=True)
        m2 = jnp.mean(gdv * xhat, axis=-1, keepdims=True)
        dgv = rstd * (gdv - m1 - xhat * m2)
        dp_ref[:, ds:ds2] = (dgv * _gelu_grad(sv)).astype(BF)
        first = pl.program_id(0) == 0
        _acc_rows(dg_ref, jnp.sum(dvn_all * xhat, axis=0, keepdims=True), first)
        _acc_rows(db_ref, jnp.sum(dvn_all, axis=0, keepdims=True), first)
        for g in range(ng):
            _acc_rows(dws_ref.at[g], jnp.where(tri, dws[g], 0.0), first)
            _acc_rows(dbs_ref.at[g], dbs[g], first)

    row = lambda i: (i, 0)
    fixed2 = lambda i: (0, 0)
    fixed3 = lambda i: (0, 0, 0)
    return _pcall(
        body, (ps, dyc, lng, lnb, ws, bs), name=name, grid=(t // tm,),
        in_specs=[pl.BlockSpec((tm, ds2), row), pl.BlockSpec((tm, ds), row), pl.BlockSpec((1, ds), fixed2),
                  pl.BlockSpec((1, ds), fixed2), pl.BlockSpec(ws.shape, fixed3), pl.BlockSpec(bs.shape, fixed3)],
        out_specs=[pl.BlockSpec((tm, ds2), row), pl.BlockSpec(ws.shape, fixed3), pl.BlockSpec(bs.shape, fixed3),
                   pl.BlockSpec((1, ds), fixed2), pl.BlockSpec((1, ds), fixed2)],
        out_shape=[SDS((t, ds2), BF), SDS(ws.shape, F32), SDS(bs.shape, F32), SDS((1, ds), F32), SDS((1, ds), F32)],
        scratch_shapes=[pltpu.VMEM((tm, ds), F32)],
        sem=("arbitrary",), side=side)


def _mix_bwd_dx(dz, dconv, dq, dk, dv, dsgu, dflog, w_main, wft, seq, alpha, name, side=None):
    t, d = dz.shape
    groups = [dconv, dq, dk, dv, dsgu]
    offs = [0]
    for g in groups:
        offs.append(offs[-1] + g.shape[1])
    h = dflog.shape[1]
    tm = _tile(seq, 512)
    per_seq = seq // tm

    def body(dz_ref, a0, a1, a2, a3, a4, dfl_ref, w_ref, wf_ref, dx_ref):
        dflp = jnp.concatenate([dfl_ref[0], jnp.zeros((HEAD_ROWS - h, tm), F32)], axis=0).astype(BF)
        acc = alpha * dz_ref[...] + _dotg(dflp, wf_ref[...], TN)
        for k, a_ref in enumerate((a0, a1, a2, a3, a4)):
            acc = acc + _dotg(a_ref[...], w_ref[:, offs[k]:offs[k + 1]], NT)
        dx_ref[...] = acc

    row = lambda i: (i, 0)
    return _pcall(
        body, (dz, *groups, dflog, w_main, wft), name=name, grid=(t // tm,),
        in_specs=[pl.BlockSpec((tm, d), row)] + [pl.BlockSpec((tm, g.shape[1]), row) for g in groups]
        + [pl.BlockSpec((1, h, tm), lambda i: (i // per_seq, 0, i % per_seq)),
           pl.BlockSpec(w_main.shape, lambda i: (0, 0)), pl.BlockSpec(wft.shape, lambda i: (0, 0))],
        out_specs=pl.BlockSpec((tm, d), row), out_shape=SDS((t, d), F32),
        sem=("parallel",), side=side)


def _adam_math(w, g, m, v):
    c1 = 1.0 / (1.0 - ADAM_B1 ** ADAM_STEP)
    c2 = 1.0 / (1.0 - ADAM_B2 ** ADAM_STEP)
    nm = ADAM_B1 * m + (1.0 - ADAM_B1) * g
    nv = ADAM_B2 * v + (1.0 - ADAM_B2) * (g * g)
    delta = -ADAM_LR * ((nm * c1) / (jnp.sqrt(nv * c2) + ADAM_EPS) + ADAM_WD * w)
    return delta, nm, nv


def _adamw_small(w, g, m, v, name):
    r, c = w.shape
    tr = _tile(r, 512)

    def body(w_ref, g_ref, m_ref, v_ref, d_ref, nm_ref, nv_ref):
        d_ref[...], nm_ref[...], nv_ref[...] = _adam_math(w_ref[...], g_ref[...], m_ref[...], v_ref[...])

    blk = pl.BlockSpec((tr, c), lambda i: (i, 0))
    return _pcall(body, (w, g, m, v), name=name, grid=(r // tr,), in_specs=[blk] * 4, out_specs=[blk] * 3,
                  out_shape=[SDS((r, c), F32)] * 3, sem=("parallel",))


def _adamw_shard(w, m, v, tot, recv, cq, layer, prev, name, side=None):
    nl, xr, yc = w.shape
    h = xr // 2
    tr = _tile(h, 128)
    nt = h // tr

    def body(cq_ref, w_ref, m_ref, v_ref, t_ref, r_ref, *rest):
        g_ref, d_ref, nm_ref, nv_ref = rest[-4:]
        g = jnp.where(pl.program_id(0) == cq_ref[0], t_ref[...], r_ref[...])
        g_ref[0] = g
        d_ref[0], nm_ref[0], nv_ref[0] = _adam_math(w_ref[0], g, m_ref[0], v_ref[0])

    slab = pl.BlockSpec((1, tr, yc), lambda hf, i, cq_ref: (layer, hf * nt + i, 0))
    mine = pl.BlockSpec((tr, yc), lambda hf, i, cq_ref: (jnp.where(hf == cq_ref[0], i, 0), 0))
    theirs = pl.BlockSpec((tr, yc), lambda hf, i, cq_ref: (jnp.where(hf == cq_ref[0], 0, i), 0))
    operands = [w, m, v, tot, recv]
    in_specs = [slab, slab, slab, mine, theirs]
    aliases = None
    if prev is not None:
        operands += list(prev)
        in_specs += [HBM] * 4
        aliases = {6 + k: k for k in range(4)}
    return _pcall(body, operands, name=name, grid=(2, nt), prefetch=(cq,), in_specs=in_specs,
                  out_specs=[slab] * 4, out_shape=[SDS(w.shape, F32)] * 4, aliases=aliases,
                  sem=("parallel", "parallel"), side=side)


BIG = ("ffn1_w_up", "ffn1_w_down", "mix_w_in", "mix_w_out", "ffn2_w_up", "ffn2_w_down")
SMALL = ("ln1_g", "ln1_b", "fox_b_f", "sgu_ln_g", "sgu_ln_b", "sgu_w_s", "sgu_b_s", "ln2_g", "ln2_b", "ln3_g", "ln3_b")
ORDER = ("ln1_g", "ln1_b", "ffn1_w_up", "ffn1_w_down", "mix_w_in", "fox_b_f", "conv_w", "sgu_ln_g", "sgu_ln_b",
         "sgu_w_s", "sgu_b_s", "mix_w_out", "ln2_g", "ln2_b", "ffn2_w_up", "ffn2_w_down", "ln3_g", "ln3_b")


def _row(v):
    return v.reshape(1, -1)


def _forward_layer(x, p, dims, alpha, l):
    b, s = dims["b"], dims["s"]
    t, d = x.shape
    tag = "l%d_" % l
    h1, a1 = _ffn_up_fwd(x, p["wup1"], tag + "ffn1_up")
    z1, x1 = _res_ln_fwd([a1], p["wd1"], x, p["ln1_g"], p["ln1_b"], alpha, 0.5, tag + "ffn1_down")
    pc, pq, ps = _mix_proj_fwd(x1, p["win"], dims["proj_widths"], tag + "mix_proj")
    x1_3 = x1.reshape(b, s, d)
    flog, cum = _cum_fwd(x1_3, p["wft"], p["bf"], tag + "fox_gate")
    nh = flog.shape[1]
    cum4 = cum.reshape(b, nh // 2, 2, s)
    pc3 = pc.reshape(b, s, -1)
    pq3 = pq.reshape(b, s, -1)
    ya = _conv_fwd(pc3, p["cw"], tag + "conv").reshape(t, -1)
    yb = _fox_fwd(pq3, cum4, tag + "fox").reshape(t, -1)
    yc = _sgu_fwd(ps, p["sgu_g"], p["sgu_b"], p["ws"], p["bs"], tag + "sgu")
    z2, x2 = _res_ln_fwd([ya, yb, yc], p["wout"], x1, p["ln2_g"], p["ln2_b"], alpha, 1.0, tag + "mix_out")
    h2, a2 = _ffn_up_fwd(x2, p["wup2"], tag + "ffn2_up")
    z3, x3 = _res_ln_fwd([a2], p["wd2"], x2, p["ln3_g"], p["ln3_b"], alpha, 0.5, tag + "ffn2_down")
    saved = dict(x=x, h1=h1, a1=a1, z1=z1, x1=x1, pc3=pc3, pq3=pq3, ps=ps, flog=flog, cum4=cum4,
                 ya=ya, yb=yb, yc=yc, z2=z2, x2=x2, h2=h2, a2=a2, z3=z3)
    return x3, saved


def _ffn_backward(dy, z, gamma, wd, wup, h, a, x_in, alpha, tag):
    dz, df, dh, dgam, dbet = _ffn_bwd_mid(dy, z, gamma, wd, h, tag + "_bwd_mid")
    dx = _ffn_bwd_dx(dh, wup, dz, alpha, tag + "_bwd_dx")
    nq, d, w = wup.shape
    dwup = _dw(x_in, dh, d, w, tag + "_dw_up")[0]
    half = wd.shape[0] // 2
    dwd = _dw(a, df[None], half, d, tag + "_dw_down").reshape(nq, wd.shape[0] // nq, d)
    return dx, dwup, dwd, dgam, dbet


def _backward_layer(dy, sv, p, dims, alpha, l):
    b, s = dims["b"], dims["s"]
    tag = "l%d_" % l
    t, d = dy.shape
    g = {}
    dx2, g["ffn2_w_up"], g["ffn2_w_down"], g["ln3_g"], g["ln3_b"] = _ffn_backward(
        dy, sv["z3"], p["ln3_g"], p["wd2"], p["wup2"], sv["h2"], sv["a2"], sv["x2"], alpha, tag + "ffn2")
    wa, wb, wc = sv["ya"].shape[1], sv["yb"].shape[1], sv["yc"].shape[1]
    dz2, dz2b, dya, dyb, dyc, g["ln2_g"], g["ln2_b"] = _out_bwd(dx2, sv["z2"], p["ln2_g"], p["wout"], (wa, wb, wc),
                                                               tag + "mix_out_bwd")
    dz2b3 = dz2b[None]
    g["mix_w_out"] = jnp.concatenate(
        [_dw(sv[k], dz2b3, sv[k].shape[1], d, tag + "dw_out_" + k)[0, 0] for k in ("ya", "yb", "yc")],
        axis=0).reshape(N_SHARDS, -1, d)
    dpc3, g["conv_w"] = _conv_bwd(sv["pc3"], dya.reshape(b, s, -1), p["cw"], tag + "conv_bwd")
    dq3, dk3, dv3, dcum4 = _fox_bwd(sv["pq3"], sv["cum4"], dyb.reshape(b, s, -1), tag + "fox_bwd")
    nh = sv["flog"].shape[1]
    dflog, dbf, dwft = _cum_bwd(dcum4.reshape(b, nh, s), sv["flog"], sv["x1"].reshape(b, s, d), tag + "fox_gate_bwd")
    g["fox_b_f"] = dbf[:, 0]
    dps, g["sgu_w_s"], dbs, g["sgu_ln_g"], g["sgu_ln_b"] = _sgu_bwd(
        sv["ps"], dyc, p["sgu_g"], p["sgu_b"], p["ws"], p["bs"], tag + "sgu_bwd")
    g["sgu_b_s"] = dbs[:, :, 0]
    dpc = dpc3.reshape(t, -1)
    dq, dk, dv = dq3.reshape(t, -1), dk3.reshape(t, -1), dv3.reshape(t, -1)
    dx1 = _mix_bwd_dx(dz2, dpc, dq, dk, dv, dps, dflog, p["win"], p["wft"], s, alpha, tag + "mix_bwd_dx")
    x1 = sv["x1"]
    cols = [_dw(x1, m[None], d, m.shape[1], tag + "dw_in_" + k)[0, 0]
            for k, m in (("conv", dpc), ("q", dq), ("k", dk), ("v", dv), ("sgu", dps))]
    w_in_grad = jnp.concatenate(cols[:4] + [dwft.T, cols[4]], axis=1)
    g["mix_w_in"] = jnp.moveaxis(w_in_grad.reshape(d, N_SHARDS, -1), 1, 0)
    dx0, g["ffn1_w_up"], g["ffn1_w_down"], g["ln1_g"], g["ln1_b"] = _ffn_backward(
        dx1, sv["z1"], p["ln1_g"], p["wd1"], p["wup1"], sv["h1"], sv["a1"], sv["x"], alpha, tag + "ffn1")
    return dx0, g


def _pack_rows(flat_list):
    v = jnp.concatenate(flat_list)
    n = v.shape[0]
    pad = (-n) % 1024
    return jnp.pad(v, (0, pad)).reshape(-1, 128)


def kernel(x, ln1_g, ln1_b, ffn1_w_up, ffn1_w_down, mix_w_in, fox_b_f, conv_w, sgu_ln_g, sgu_ln_b, sgu_w_s, sgu_b_s, mix_w_out, ln2_g, ln2_b, ffn2_w_up, ffn2_w_down, ln3_g, ln3_b, loss_target, m_ln1_g, m_ln1_b, m_ffn1_w_up, m_ffn1_w_down, m_mix_w_in, m_fox_b_f, m_conv_w, m_sgu_ln_g, m_sgu_ln_b, m_sgu_w_s, m_sgu_b_s, m_mix_w_out, m_ln2_g, m_ln2_b, m_ffn2_w_up, m_ffn2_w_down, m_ln3_g, m_ln3_b, v_ln1_g, v_ln1_b, v_ffn1_w_up, v_ffn1_w_down, v_mix_w_in, v_fox_b_f, v_conv_w, v_sgu_ln_g, v_sgu_ln_b, v_sgu_w_s, v_sgu_b_s, v_mix_w_out, v_ln2_g, v_ln2_b, v_ffn2_w_up, v_ffn2_w_down, v_ln3_g, v_ln3_b):
    wts = dict(ln1_g=ln1_g, ln1_b=ln1_b, ffn1_w_up=ffn1_w_up, ffn1_w_down=ffn1_w_down, mix_w_in=mix_w_in,
               fox_b_f=fox_b_f, conv_w=conv_w, sgu_ln_g=sgu_ln_g, sgu_ln_b=sgu_ln_b, sgu_w_s=sgu_w_s,
               sgu_b_s=sgu_b_s, mix_w_out=mix_w_out, ln2_g=ln2_g, ln2_b=ln2_b, ffn2_w_up=ffn2_w_up,
               ffn2_w_down=ffn2_w_down, ln3_g=ln3_g, ln3_b=ln3_b)
    mom = dict(ln1_g=m_ln1_g, ln1_b=m_ln1_b, ffn1_w_up=m_ffn1_w_up, ffn1_w_down=m_ffn1_w_down, mix_w_in=m_mix_w_in,
               fox_b_f=m_fox_b_f, conv_w=m_conv_w, sgu_ln_g=m_sgu_ln_g, sgu_ln_b=m_sgu_ln_b, sgu_w_s=m_sgu_w_s,
               sgu_b_s=m_sgu_b_s, mix_w_out=m_mix_w_out, ln2_g=m_ln2_g, ln2_b=m_ln2_b, ffn2_w_up=m_ffn2_w_up,
               ffn2_w_down=m_ffn2_w_down, ln3_g=m_ln3_g, ln3_b=m_ln3_b)
    var = dict(ln1_g=v_ln1_g, ln1_b=v_ln1_b, ffn1_w_up=v_ffn1_w_up, ffn1_w_down=v_ffn1_w_down, mix_w_in=v_mix_w_in,
               fox_b_f=v_fox_b_f, conv_w=v_conv_w, sgu_ln_g=v_sgu_ln_g, sgu_ln_b=v_sgu_ln_b, sgu_w_s=v_sgu_w_s,
               sgu_b_s=v_sgu_b_s, mix_w_out=v_mix_w_out, ln2_g=v_ln2_g, ln2_b=v_ln2_b, ffn2_w_up=v_ffn2_w_up,
               ffn2_w_down=v_ffn2_w_down, ln3_g=v_ln3_g, ln3_b=v_ln3_b)

    nl = ln1_g.shape[0]
    b, s, d = x.shape
    t = b * s
    alpha = (2 * nl) ** 0.25
    cw_sh = conv_w.shape[2]
    d_conv = cw_sh * N_SHARDS
    d_sgu = sgu_ln_g.shape[1]
    nh = fox_b_f.shape[1]
    d_fox = nh * FOX_HEAD_DIM
    n_main = 3 * d_conv + 3 * d_fox
    dims = dict(b=b, s=s, proj_widths=(3 * d_conv, 3 * d_fox, 2 * d_sgu))
    cpos = lax.axis_index("c").astype(jnp.int32)
    qpos = (2 * lax.axis_index("x") + lax.axis_index("y")).astype(jnp.int32)
    cq = jnp.stack([cpos, qpos])

    conv_tile = jnp.pad(conv_w, ((0, 0), (0, 8 - conv_w.shape[1]), (0, 128 - cw_sh)))
    params = []
    for l in range(nl):
        job = _side_gather([wts[k][l].astype(BF) for k in BIG] + [conv_tile[l]], [True] * len(BIG) + [False])
        _run_side(job, "gather_weights_l%d" % l)
        gw = dict(zip(BIG, job.results[:-1]))
        w_in = jnp.moveaxis(gw["mix_w_in"], 0, 1).reshape(d, -1)
        params.append(dict(
            wup1=gw["ffn1_w_up"], wd1=gw["ffn1_w_down"].reshape(-1, d),
            wup2=gw["ffn2_w_up"], wd2=gw["ffn2_w_down"].reshape(-1, d),
            wout=gw["mix_w_out"].reshape(-1, d),
            win=jnp.concatenate([w_in[:, :n_main], w_in[:, n_main + nh:]], axis=1),
            wft=jnp.pad(w_in[:, n_main:n_main + nh].T, ((0, HEAD_ROWS - nh), (0, 0))),
            bf=fox_b_f[l].reshape(nh, 1),
            cw=jnp.moveaxis(job.results[-1][:, :3, :cw_sh], 0, 1).reshape(3, d_conv),
            sgu_g=_row(sgu_ln_g[l]), sgu_b=_row(sgu_ln_b[l]), ws=sgu_w_s[l], bs=sgu_b_s[l][:, :, None],
            ln1_g=_row(ln1_g[l]), ln1_b=_row(ln1_b[l]), ln2_g=_row(ln2_g[l]), ln2_b=_row(ln2_b[l]),
            ln3_g=_row(ln3_g[l]), ln3_b=_row(ln3_b[l])))

    act = x.reshape(t, d)
    saved = []
    for l in range(nl):
        act, sv = _forward_layer(act, params[l], dims, alpha, l)
        saved.append(sv)
    dy, loss_blk = _loss_fwd(act, loss_target.reshape(t, d), "loss")
    grads = [None] * nl
    for l in reversed(range(nl)):
        dy, grads[l] = _backward_layer(dy, saved[l], params[l], dims, alpha, l)
    grad_x = dy.reshape(b, s, d)

    keys = [(l, k) for l in reversed(range(nl)) for k in BIG]
    send = _side_pair_send([grads[l][k] for l, k in keys])
    _run_side(send, "rs_pair_send")
    part = [_pair_sum(grads[l][k], r, cq, "rs_pair_sum_l%d_%s" % (l, k)) for (l, k), r in zip(keys, send.results)]
    scat = _side_scatter(part)
    _run_side(scat, "rs_scatter")
    tot = [_chip_sum(p_, r, cq, "rs_chip_sum_l%d_%s" % lk) for lk, p_, r in zip(keys, part, scat.results)]
    share = _side_pair_share(tot)
    _run_side(share, "rs_pair_share")
    gfin, delta, new_m, new_v = {}, {}, {}, {}
    prev = {k: None for k in BIG}
    for (l, k), tt, rr in zip(keys, tot, share.results):
        prev[k] = _adamw_shard(wts[k], mom[k], var[k], tt, rr, cq, l, prev[k], "adamw_l%d_%s" % (l, k))
    for k in BIG:
        gfin[k], delta[k], new_m[k], new_v[k] = prev[k]

    flat = []
    for l in range(nl):
        flat += [grads[l][k].reshape(-1) for k in SMALL] + [grads[l]["conv_w"].reshape(-1)]
    flat.append(loss_blk[0, 0:1])
    red = _all_reduce_small(_pack_rows(flat), "all_reduce_small").reshape(-1)
    gsm = {k: [] for k in SMALL + ("conv_w",)}
    off = 0
    for l in range(nl):
        for k in SMALL:
            n = wts[k][l].size
            gsm[k].append(red[off:off + n].reshape(wts[k][l].shape))
            off += n
        n = 3 * d_conv
        gsm["conv_w"].append(lax.dynamic_slice_in_dim(red[off:off + n].reshape(3, d_conv), qpos * cw_sh, cw_sh, axis=1))
        off += n
    loss = red[off]
    for k in gsm:
        gfin[k] = jnp.stack(gsm[k])
    small_keys = SMALL + ("conv_w",)
    sizes = [wts[k].size for k in small_keys]
    pk = lambda src: _pack_rows([src[k].reshape(-1) for k in small_keys])
    dl, nm, nv = _adamw_small(pk(wts), pk(gfin), pk(mom), pk(var), "adamw_small")
    off = 0
    for k, n in zip(small_keys, sizes):
        shp = wts[k].shape
        delta[k] = dl.reshape(-1)[off:off + n].reshape(shp)
        new_m[k] = nm.reshape(-1)[off:off + n].reshape(shp)
        new_v[k] = nv.reshape(-1)[off:off + n].reshape(shp)
        off += n

    return (loss, grad_x, *[gfin[k] for k in ORDER], *[delta[k] for k in ORDER],
            *[new_m[k] for k in ORDER], *[new_v[k] for k in ORDER])
```

```python
import jax
import jax.numpy as jnp
from jax import lax
from jax.experimental import pallas as pl
from jax.experimental.pallas import tpu as pltpu

F32 = jnp.float32
BF = jnp.bfloat16
SDS = jax.ShapeDtypeStruct
MESH = pl.DeviceIdType.MESH

LN_EPS = 1e-5
FOX_HEAD_DIM = 64
FOX_Q_BLOCK = 256
HEAD_ROWS = 128
GELU_K = 0.7978845608028654
GELU_C = 0.044715
NEG_BIG = -1e30
N_SHARDS = 4

ADAM_LR = 0.001
ADAM_B1 = 0.9
ADAM_B2 = 0.999
ADAM_EPS = 1e-08
ADAM_WD = 0.01
ADAM_STEP = 10

VMEM_LIMIT_BYTES = 56 * 1024 * 1024
NT = (((1,), (1,)), ((), ()))
TN = (((0,), (0,)), ((), ()))
HBM = pl.BlockSpec(memory_space=pl.ANY)


def _tile(n, pref, mult=8):
    t = min(n, pref)
    while n % t or t % mult:
        t -= mult
    return t


def _dot(a, b):
    return jnp.dot(a, b, preferred_element_type=F32)


def _dotg(a, b, dims):
    return lax.dot_general(a, b, dims, preferred_element_type=F32)


def _sigmoid(x):
    return 1.0 / (1.0 + jnp.exp(-x))


def _gelu(x):
    return 0.5 * x * (1.0 + jnp.tanh(GELU_K * (x + GELU_C * x * x * x)))


def _gelu_grad(x):
    t = jnp.tanh(GELU_K * (x + GELU_C * x * x * x))
    return 0.5 * (1.0 + t) + 0.5 * x * (1.0 - t * t) * GELU_K * (1.0 + 3.0 * GELU_C * x * x)


def _ln_stats(z):
    mu = jnp.mean(z, axis=-1, keepdims=True)
    zc = z - mu
    var = jnp.mean(zc * zc, axis=-1, keepdims=True)
    rstd = lax.rsqrt(var + LN_EPS)
    return zc * rstd, rstd


def _ln_bwd(dy, z, g):
    xhat, rstd = _ln_stats(z)
    gdy = dy * g
    m1 = jnp.mean(gdy, axis=-1, keepdims=True)
    m2 = jnp.mean(gdy * xhat, axis=-1, keepdims=True)
    dz = rstd * (gdy - m1 - xhat * m2)
    return dz, jnp.sum(dy * xhat, axis=0, keepdims=True), jnp.sum(dy, axis=0, keepdims=True)


class _Side:
    def __init__(self, ins, out_shapes, sems, start, finish):
        self.ins, self.out_shapes, self.sems = list(ins), list(out_shapes), list(sems)
        self.start, self.finish = start, finish
        self.results = None


def _join(sides):
    sides = [s for s in sides if s is not None]
    if not sides:
        return None
    ins = [a for s in sides for a in s.ins]
    outs = [a for s in sides for a in s.out_shapes]
    sems = [a for s in sides for a in s.sems]

    def parts(seq, field):
        out, o = [], 0
        for s in sides:
            n = len(getattr(s, field))
            out.append(seq[o:o + n])
            o += n
        return out

    def run(which):
        def fn(i, o, m):
            for s, a, b, c in zip(sides, parts(i, "ins"), parts(o, "out_shapes"), parts(m, "sems")):
                getattr(s, which)(a, b, c)
        return fn

    joined = _Side(ins, outs, sems, run("start"), run("finish"))
    joined.members = sides
    return joined


def _deliver(side, results):
    members = getattr(side, "members", None)
    side.results = list(results)
    if members:
        o = 0
        for s in members:
            n = len(s.out_shapes)
            _deliver(s, results[o:o + n])
            o += n


def _pcall(body, operands, *, name, grid, in_specs, out_specs, out_shape, sem, scratch_shapes=(),
           prefetch=(), aliases=None, side=None):
    single = not isinstance(out_shape, (list, tuple))
    out_shape = [out_shape] if single else list(out_shape)
    out_specs = [out_specs] if single else list(out_specs)
    in_specs, scratch_shapes = list(in_specs), list(scratch_shapes)
    n_pre, n_in, n_out, n_sc = len(prefetch), len(in_specs), len(out_shape), len(scratch_shapes)
    fn = body
    extra = []
    if side is not None:
        s_in, s_out = len(side.ins), len(side.out_shapes)

        def fn(*refs):
            pre, rest = refs[:n_pre], refs[n_pre:]
            m_in, c_in = rest[:n_in], rest[n_in:n_in + s_in]
            rest = rest[n_in + s_in:]
            m_out, c_out = rest[:n_out], rest[n_out:n_out + s_out]
            rest = rest[n_out + s_out:]
            m_sc, c_sc = rest[:n_sc], rest[n_sc:]
            first = pl.program_id(0) == 0
            last = pl.program_id(0) == grid[0] - 1
            for a in range(1, len(grid)):
                first = jnp.logical_and(first, pl.program_id(a) == 0)
                last = jnp.logical_and(last, pl.program_id(a) == grid[a] - 1)

            @pl.when(first)
            def _():
                side.start(c_in, c_out, c_sc)

            body(*pre, *m_in, *m_out, *m_sc)

            @pl.when(last)
            def _():
                side.finish(c_in, c_out, c_sc)

        in_specs = in_specs + [HBM] * s_in
        out_specs = out_specs + [HBM] * s_out
        out_shape = out_shape + side.out_shapes
        scratch_shapes = scratch_shapes + side.sems
        extra = side.ins
        sem = ("arbitrary",) * len(grid)
    params = pltpu.CompilerParams(dimension_semantics=tuple(sem), vmem_limit_bytes=VMEM_LIMIT_BYTES)
    kw = dict(input_output_aliases=aliases) if aliases else {}
    if n_pre:
        spec = pltpu.PrefetchScalarGridSpec(num_scalar_prefetch=n_pre, grid=grid, in_specs=in_specs,
                                            out_specs=out_specs, scratch_shapes=scratch_shapes)
        call = pl.pallas_call(fn, name=name, grid_spec=spec, out_shape=out_shape, compiler_params=params, **kw)
    else:
        call = pl.pallas_call(fn, name=name, grid=grid, in_specs=in_specs, out_specs=out_specs,
                              out_shape=out_shape, scratch_shapes=scratch_shapes, compiler_params=params, **kw)
    res = call(*prefetch, *operands, *extra)
    if side is not None:
        _deliver(side, res[n_out:])
        res = res[:n_out]
    return res[0] if single else res


def _run_side(side, name):
    def body(*refs):
        n_in, n_out = len(side.ins), len(side.out_shapes)
        i, o, m = refs[:n_in], refs[n_in:n_in + n_out], refs[n_in + n_out:]
        side.start(i, o, m)
        side.finish(i, o, m)

    res = pl.pallas_call(body, name=name, in_specs=[HBM] * len(side.ins), out_specs=[HBM] * len(side.out_shapes),
                         out_shape=side.out_shapes, scratch_shapes=side.sems)(*side.ins)
    _deliver(side, res)


def _mesh_pos():
    x, y, c = lax.axis_index("x"), lax.axis_index("y"), lax.axis_index("c")
    chips = [(1 - x, y), (x, 1 - y), (1 - x, 1 - y)]
    return x, y, c, chips


def _rows(ref, lead, half, n_rows):
    return ref.at[tuple(lead) + (pl.ds(half * n_rows, n_rows),)]


def _side_gather(shards, split):
    n = len(shards)
    hs = [w.shape[0] // 2 for w in shards]

    def plan(ins, outs, sems):
        ssem, rsem = sems
        x, y, c, chips = _mesh_pos()
        q = 2 * x + y
        sib = (x, y, 1 - c)

        def rc(p, k, src, dst, to):
            return pltpu.make_async_remote_copy(src_ref=src, dst_ref=dst, send_sem=ssem.at[p, k],
                                                recv_sem=rsem.at[p, k], device_id=to, device_id_type=MESH)

        def blk(ref, p, qi, half):
            return _rows(ref, (qi,), half, hs[p]) if split[p] else ref.at[qi]

        return x, y, c, chips, q, sib, rc, blk

    def first_sends(ins, outs, sems):
        x, y, c, chips, q, sib, rc, blk = plan(ins, outs, sems)
        cps = [rc(p, 0, ins[p], outs[p].at[q], sib) for p in range(n)]
        for j, (cx, cy) in enumerate(chips):
            for p in range(n):
                src = _rows(ins[p], (), c, hs[p]) if split[p] else ins[p]
                cps.append(rc(p, 1 + j, src, blk(outs[p], p, q, c), (cx, cy, c)))
        return cps

    def start(ins, outs, sems):
        for cp in first_sends(ins, outs, sems):
            cp.start()

    def finish(ins, outs, sems):
        x, y, c, chips, q, sib, rc, blk = plan(ins, outs, sems)
        sent = first_sends(ins, outs, sems)
        for j, (cx, cy) in enumerate(chips):
            qj = 2 * cx + cy
            for p in range(n):
                got = blk(outs[p], p, qj, c)
                rc(p, 1 + j, got, got, (cx, cy, c)).wait_recv()
                if split[p]:
                    fwd = rc(p, 4 + j, got, got, sib)
                    fwd.start()
                    sent.append(fwd)
        for j, (cx, cy) in enumerate(chips):
            qj = 2 * cx + cy
            for p in range(n):
                if split[p]:
                    got = blk(outs[p], p, qj, 1 - c)
                    rc(p, 4 + j, got, got, sib).wait_recv()
        for p in range(n):
            rc(p, 0, outs[p].at[q], outs[p].at[q], sib).wait_recv()
        for cp in sent:
            cp.wait_send()

    return _Side(shards, [SDS((N_SHARDS,) + w.shape, w.dtype) for w in shards],
                 [pltpu.SemaphoreType.DMA((n, 7)), pltpu.SemaphoreType.DMA((n, 7))], start, finish)


def _side_pair_send(gs):
    n = len(gs)

    def copies(ins, outs, sems):
        x, y, c, _ = _mesh_pos()
        return [pltpu.make_async_remote_copy(
            src_ref=ins[p].at[:, pl.ds((1 - c) * (gs[p].shape[1] // 2), gs[p].shape[1] // 2)], dst_ref=outs[p],
            send_sem=sems[0].at[p], recv_sem=sems[1].at[p], device_id=(x, y, 1 - c), device_id_type=MESH)
            for p in range(n)]

    def start(ins, outs, sems):
        for cp in copies(ins, outs, sems):
            cp.start()

    def finish(ins, outs, sems):
        for cp in copies(ins, outs, sems):
            cp.wait()

    return _Side(gs, [SDS((g.shape[0], g.shape[1] // 2, g.shape[2]), g.dtype) for g in gs],
                 [pltpu.SemaphoreType.DMA((n,)), pltpu.SemaphoreType.DMA((n,))], start, finish)


def _side_scatter(ps):
    n = len(ps)

    def sends(ins, outs, sems):
        x, y, c, chips = _mesh_pos()
        q = 2 * x + y
        return [pltpu.make_async_remote_copy(src_ref=ins[p].at[2 * cx + cy], dst_ref=outs[p].at[q],
                                             send_sem=sems[0].at[p, j], recv_sem=sems[1].at[p, j],
                                             device_id=(cx, cy, c), device_id_type=MESH)
                for j, (cx, cy) in enumerate(chips) for p in range(n)]

    def start(ins, outs, sems):
        for cp in sends(ins, outs, sems):
            cp.start()

    def finish(ins, outs, sems):
        x, y, c, chips = _mesh_pos()
        for j, (cx, cy) in enumerate(chips):
            for p in range(n):
                got = outs[p].at[2 * cx + cy]
                pltpu.make_async_remote_copy(src_ref=got, dst_ref=got, send_sem=sems[0].at[p, j],
                                             recv_sem=sems[1].at[p, j], device_id=(cx, cy, c),
                                             device_id_type=MESH).wait_recv()
        for cp in sends(ins, outs, sems):
            cp.wait_send()

    return _Side(ps, [SDS(p.shape, p.dtype) for p in ps],
                 [pltpu.SemaphoreType.DMA((n, 3)), pltpu.SemaphoreType.DMA((n, 3))], start, finish)


def _side_pair_share(tots):
    n = len(tots)

    def copies(ins, outs, sems):
        x, y, c, _ = _mesh_pos()
        return [pltpu.make_async_remote_copy(src_ref=ins[p], dst_ref=outs[p], send_sem=sems[0].at[p],
                                             recv_sem=sems[1].at[p], device_id=(x, y, 1 - c), device_id_type=MESH)
                for p in range(n)]

    def start(ins, outs, sems):
        for cp in copies(ins, outs, sems):
            cp.start()

    def finish(ins, outs, sems):
        for cp in copies(ins, outs, sems):
            cp.wait()

    return _Side(tots, [SDS(t_.shape, t_.dtype) for t_ in tots],
                 [pltpu.SemaphoreType.DMA((n,)), pltpu.SemaphoreType.DMA((n,))], start, finish)


N_DEVICES = 8


def _side_bcast(v):
    def peers():
        x, y, c, _ = _mesh_pos()
        out = []
        for k in range(1, N_DEVICES):
            px, py, pc = x ^ ((k >> 2) & 1), y ^ ((k >> 1) & 1), c ^ (k & 1)
            out.append((k - 1, (px, py, pc), 4 * px + 2 * py + pc))
        return 4 * x + 2 * y + c, out

    def sends(ins, outs, sems):
        me, ps = peers()
        return [pltpu.make_async_remote_copy(src_ref=ins[0], dst_ref=outs[0].at[me], send_sem=sems[0].at[k],
                                             recv_sem=sems[1].at[k], device_id=to, device_id_type=MESH)
                for k, to, _ in ps]

    def start(ins, outs, sems):
        for cp in sends(ins, outs, sems):
            cp.start()

    def finish(ins, outs, sems):
        _, ps = peers()
        for k, to, slot in ps:
            got = outs[0].at[slot]
            pltpu.make_async_remote_copy(src_ref=got, dst_ref=got, send_sem=sems[0].at[k], recv_sem=sems[1].at[k],
                                         device_id=to, device_id_type=MESH).wait_recv()
        for cp in sends(ins, outs, sems):
            cp.wait_send()

    return _Side([v], [SDS((N_DEVICES,) + v.shape, v.dtype)],
                 [pltpu.SemaphoreType.DMA((N_DEVICES - 1,)), pltpu.SemaphoreType.DMA((N_DEVICES - 1,))], start, finish)


def _sum_slots(v, r, me, name, side=None):
    n, rows, lanes = r.shape
    tr = _tile(rows, 512)

    def body(me_ref, v_ref, r_ref, o_ref):
        j = pl.program_id(1)
        term = jnp.where(j == me_ref[0], v_ref[...], r_ref[0])

        @pl.when(j == 0)
        def _():
            o_ref[...] = term

        @pl.when(j != 0)
        def _():
            o_ref[...] += term

    other = lambda j, k: jnp.where(j == k, (k + 1) % n, j)
    return _pcall(
        body, (v, r), name=name, grid=(rows // tr, n), prefetch=(me,),
        in_specs=[pl.BlockSpec((tr, lanes), lambda i, j, me_ref: (i, 0)),
                  pl.BlockSpec((1, tr, lanes), lambda i, j, me_ref: (other(j, me_ref[0]), i, 0))],
        out_specs=pl.BlockSpec((tr, lanes), lambda i, j, me_ref: (i, 0)),
        out_shape=SDS((rows, lanes), F32), sem=("parallel", "arbitrary"), side=side)


def _pair_sum(g, r1, cq, name, side=None):
    nq, xr, yc = g.shape
    h = xr // 2
    tr = _tile(h, 512, 16)
    nt = h // tr

    def body(cq_ref, g_ref, r_ref, o_ref):
        o_ref[...] = (g_ref[...] + r_ref[...]).astype(BF)

    return _pcall(
        body, (g, r1), name=name, grid=(nq, nt), prefetch=(cq,),
        in_specs=[pl.BlockSpec((1, tr, yc), lambda j, i, cq_ref: (j, cq_ref[0] * nt + i, 0)),
                  pl.BlockSpec((1, tr, yc), lambda j, i, cq_ref: (j, i, 0))],
        out_specs=pl.BlockSpec((1, tr, yc), lambda j, i, cq_ref: (j, i, 0)),
        out_shape=SDS((nq, h, yc), BF), sem=("parallel", "parallel"), side=side)


def _chip_sum(p, r2, cq, name, side=None):
    nq, h, yc = r2.shape
    tr = _tile(h, 512, 16)

    def body(cq_ref, p_ref, r_ref, o_ref):
        j = pl.program_id(1)
        term = jnp.where(j == cq_ref[1], p_ref[0], r_ref[0]).astype(F32)

        @pl.when(j == 0)
        def _():
            o_ref[...] = term

        @pl.when(j != 0)
        def _():
            o_ref[...] += term

    other = lambda j, q: jnp.where(j == q, (q + 1) % nq, j)
    return _pcall(
        body, (p, r2), name=name, grid=(h // tr, nq), prefetch=(cq,),
        in_specs=[pl.BlockSpec((1, tr, yc), lambda i, j, cq_ref: (cq_ref[1], i, 0)),
                  pl.BlockSpec((1, tr, yc), lambda i, j, cq_ref: (other(j, cq_ref[1]), i, 0))],
        out_specs=pl.BlockSpec((tr, yc), lambda i, j, cq_ref: (i, 0)),
        out_shape=SDS((h, yc), F32), sem=("parallel", "arbitrary"), side=side)


def _ffn_up_fwd(x, wup, name, side=None):
    t, d = x.shape
    w = wup.shape[2]
    tm = _tile(t, 512)

    def body(x_ref, wg_ref, wu_ref, h_ref, a_ref):
        xb = x_ref[...].astype(BF)
        g = _dot(xb, wg_ref[0])
        u = _dot(xb, wu_ref[0])
        h_ref[0] = g.astype(BF)
        h_ref[1] = u.astype(BF)
        a_ref[...] = (g * _sigmoid(g) * u).astype(BF)

    return _pcall(
        body, (x, wup, wup), name=name, grid=(2, t // tm),
        in_specs=[pl.BlockSpec((tm, d), lambda j, i: (i, 0)),
                  pl.BlockSpec((1, d, w), lambda j, i: (j, 0, 0)),
                  pl.BlockSpec((1, d, w), lambda j, i: (j + 2, 0, 0))],
        out_specs=[pl.BlockSpec((2, tm, w), lambda j, i: (0, i, j)),
                   pl.BlockSpec((tm, w), lambda j, i: (i, j))],
        out_shape=[SDS((2, t, 2 * w), BF), SDS((t, 2 * w), BF)],
        sem=("parallel", "parallel"), side=side)


def _res_ln_fwd(parts, w, x, gamma, beta, alpha, res_scale, name, side=None):
    t, d = x.shape
    n = len(parts)
    offs = [0]
    for p in parts:
        offs.append(offs[-1] + p.shape[1])
    tm = _tile(t, 512)

    def body(*refs):
        p_refs = refs[:n]
        w_ref, x_ref, g_ref, b_ref, z_ref, y_ref = refs[n:]
        f = _dot(p_refs[0][...], w_ref[offs[0]:offs[1], :])
        for k in range(1, n):
            f = f + _dot(p_refs[k][...], w_ref[offs[k]:offs[k + 1], :])
        z = alpha * x_ref[...] + res_scale * f
        z_ref[...] = z
        xhat, _ = _ln_stats(z)
        y_ref[...] = xhat * g_ref[...] + b_ref[...]

    row = lambda i: (i, 0)
    fixed = lambda i: (0, 0)
    return _pcall(
        body, (*parts, w, x, gamma, beta), name=name, grid=(t // tm,),
        in_specs=[pl.BlockSpec((tm, p.shape[1]), row) for p in parts]
        + [pl.BlockSpec(w.shape, fixed), pl.BlockSpec((tm, d), row),
           pl.BlockSpec((1, d), fixed), pl.BlockSpec((1, d), fixed)],
        out_specs=[pl.BlockSpec((tm, d), row), pl.BlockSpec((tm, d), row)],
        out_shape=[SDS((t, d), F32), SDS((t, d), F32)],
        sem=("parallel",), side=side)


def _mix_proj_fwd(x, w_main, widths, name, side=None):
    t, d = x.shape
    dc, dq, ds = widths
    tm = _tile(t, 512)

    def body(x_ref, w_ref, pc_ref, pq_ref, ps_ref):
        xb = x_ref[...].astype(BF)
        pc_ref[...] = _dot(xb, w_ref[:, 0:dc])
        pq_ref[...] = _dot(xb, w_ref[:, dc:dc + dq]).astype(BF)
        ps_ref[...] = _dot(xb, w_ref[:, dc + dq:dc + dq + ds])

    row = lambda i: (i, 0)
    return _pcall(
        body, (x, w_main), name=name, grid=(t // tm,),
        in_specs=[pl.BlockSpec((tm, d), row), pl.BlockSpec(w_main.shape, lambda i: (0, 0))],
        out_specs=[pl.BlockSpec((tm, dc), row), pl.BlockSpec((tm, dq), row), pl.BlockSpec((tm, ds), row)],
        out_shape=[SDS((t, dc), F32), SDS((t, dq), BF), SDS((t, ds), F32)],
        sem=("parallel",), side=side)


def _prefix_sum_lanes(v, reverse):
    n = v.shape[-1]
    lane = lax.broadcasted_iota(jnp.int32, v.shape, v.ndim - 1)
    sh = 1
    while sh < n:
        if reverse:
            v = v + jnp.where(lane < n - sh, pltpu.roll(v, n - sh, axis=v.ndim - 1), 0.0)
        else:
            v = v + jnp.where(lane >= sh, pltpu.roll(v, sh, axis=v.ndim - 1), 0.0)
        sh *= 2
    return v


def _cum_fwd(x3, wft, bf, name, side=None):
    b, s, d = x3.shape
    h = bf.shape[0]

    def body(x_ref, w_ref, b_ref, fl_ref, cum_ref):
        fl = _dotg(w_ref[...], x_ref[0].astype(BF), NT)[0:h] + b_ref[...]
        fl_ref[0] = fl
        lf = jnp.minimum(fl, 0.0) - jnp.log(1.0 + jnp.exp(-jnp.abs(fl)))
        cum_ref[0] = _prefix_sum_lanes(lf, reverse=False)

    return _pcall(
        body, (x3, wft, bf), name=name, grid=(b,),
        in_specs=[pl.BlockSpec((1, s, d), lambda i: (i, 0, 0)),
                  pl.BlockSpec(wft.shape, lambda i: (0, 0)), pl.BlockSpec((h, 1), lambda i: (0, 0))],
        out_specs=[pl.BlockSpec((1, h, s), lambda i: (i, 0, 0)), pl.BlockSpec((1, h, s), lambda i: (i, 0, 0))],
        out_shape=[SDS((b, h, s), F32), SDS((b, h, s), F32)],
        sem=("parallel",), side=side)


def _shift_rows(z, k, down):
    n = z.shape[0]
    row = lax.broadcasted_iota(jnp.int32, z.shape, 0)
    if down:
        return jnp.where(row >= k, pltpu.roll(z, k, axis=0), 0.0)
    return jnp.where(row < n - k, pltpu.roll(z, n - k, axis=0), 0.0)


def _conv_fwd(pc3, cw, name, side=None):
    b, s, c3 = pc3.shape
    c = c3 // 3

    def body(p_ref, w_ref, y_ref):
        z = p_ref[0, :, c:2 * c] * p_ref[0, :, 2 * c:3 * c]
        conv = w_ref[0:1, :] * _shift_rows(z, 2, True) + w_ref[1:2, :] * _shift_rows(z, 1, True) + w_ref[2:3, :] * z
        y_ref[0] = (p_ref[0, :, 0:c] * conv).astype(BF)

    return _pcall(
        body, (pc3, cw), name=name, grid=(b,),
        in_specs=[pl.BlockSpec((1, s, c3), lambda i: (i, 0, 0)), pl.BlockSpec((3, c), lambda i: (0, 0))],
        out_specs=pl.BlockSpec((1, s, c), lambda i: (i, 0, 0)),
        out_shape=SDS((b, s, c), BF), sem=("parallel",), side=side)


def _head_masks(width):
    lane = lax.broadcasted_iota(jnp.int32, (1, width), 1)
    return [lane < FOX_HEAD_DIM, lane >= FOX_HEAD_DIM]


def _fox_probs(qm, k, cum_row, lo):
    tq, hi = qm.shape[0], k.shape[0]
    s = _dotg(qm, k, NT) * (FOX_HEAD_DIM ** -0.5) - cum_row
    col = lax.broadcasted_iota(jnp.int32, (tq, hi), 1)
    rowi = lax.broadcasted_iota(jnp.int32, (tq, hi), 0) + lo
    s = jnp.where(col <= rowi, s, NEG_BIG)
    p = jnp.exp(s - jnp.max(s, axis=-1, keepdims=True))
    return p * (1.0 / jnp.sum(p, axis=-1, keepdims=True))


def _fox_fwd(pq3, cum4, name, side=None):
    b, s, d3 = pq3.shape
    df = d3 // 3
    hp = df // 128
    tq = _tile(s, FOX_Q_BLOCK)

    def body(q_ref, k_ref, v_ref, c_ref, o_ref):
        masks = _head_masks(128)
        for i in range(s // tq):
            lo, hi = i * tq, (i + 1) * tq
            q = q_ref[0, lo:hi, :]
            k = k_ref[0, 0:hi, :]
            v = v_ref[0, 0:hi, :]
            o = jnp.zeros((tq, 128), F32)
            for e in range(2):
                p = _fox_probs(jnp.where(masks[e], q, 0), k, c_ref[0, 0, e:e + 1, 0:hi], lo)
                o = jnp.where(masks[e], _dot(p.astype(BF), v), o)
            o_ref[0, lo:hi, :] = o.astype(BF)

    blk = lambda off: pl.BlockSpec((1, s, 128), lambda i, j: (i, 0, off + j))
    return _pcall(
        body, (pq3, pq3, pq3, cum4), name=name, grid=(b, hp),
        in_specs=[blk(0), blk(hp), blk(2 * hp), pl.BlockSpec((1, 1, 2, s), lambda i, j: (i, j, 0, 0))],
        out_specs=blk(0), out_shape=SDS((b, s, df), BF),
        sem=("parallel", "parallel"), side=side)


def _sgu_mix(wm, vnb, bias, gmasks):
    out = bias
    for g in range(len(wm)):
        out = out + jnp.where(gmasks[g], _dot(wm[g], vnb), 0.0)
    return out


def _sgu_consts(ws_ref, bs_ref, ds):
    ng, c, _ = ws_ref.shape
    gd = ds // ng
    tri = lax.broadcasted_iota(jnp.int32, (c, c), 0) >= lax.broadcasted_iota(jnp.int32, (c, c), 1)
    lane = lax.broadcasted_iota(jnp.int32, (1, ds), 1)
    gmasks = [(lane >= g * gd) & (lane < (g + 1) * gd) for g in range(ng)]
    wm = [jnp.where(tri, ws_ref[g], 0.0).astype(BF) for g in range(ng)]
    bias = jnp.zeros((c, ds), F32)
    for g in range(ng):
        bias = jnp.where(gmasks[g], bs_ref[g], bias)
    return tri, gmasks, wm, bias


def _sgu_fwd(ps, lng, lnb, ws, bs, name, side=None):
    t, ds2 = ps.shape
    ds = ds2 // 2
    c = ws.shape[1]
    tm = _tile(t, 512, c)

    def body(p_ref, g_ref, b_ref, ws_ref, bs_ref, y_ref):
        _, gmasks, wm, bias = _sgu_consts(ws_ref, bs_ref, ds)
        up = _gelu(p_ref[:, 0:ds])
        xhat, _ = _ln_stats(_gelu(p_ref[:, ds:ds2]))
        vnb = (xhat * g_ref[...] + b_ref[...]).astype(BF)
        for n in range(tm // c):
            r0, r1 = n * c, (n + 1) * c
            y_ref[r0:r1, :] = (up[r0:r1] * _sgu_mix(wm, vnb[r0:r1], bias, gmasks)).astype(BF)

    fixed2 = lambda i: (0, 0)
    fixed3 = lambda i: (0, 0, 0)
    return _pcall(
        body, (ps, lng, lnb, ws, bs), name=name, grid=(t // tm,),
        in_specs=[pl.BlockSpec((tm, ds2), lambda i: (i, 0)), pl.BlockSpec((1, ds), fixed2),
                  pl.BlockSpec((1, ds), fixed2), pl.BlockSpec(ws.shape, fixed3), pl.BlockSpec(bs.shape, fixed3)],
        out_specs=pl.BlockSpec((tm, ds), lambda i: (i, 0)),
        out_shape=SDS((t, ds), BF), sem=("parallel",), side=side)


def _loss_fwd(y, target, name, side=None):
    t, d = y.shape
    tm = _tile(t, 512)

    def body(y_ref, t_ref, dy_ref, l_ref):
        @pl.when(pl.program_id(0) == 0)
        def _():
            l_ref[...] = jnp.zeros_like(l_ref)

        err = y_ref[...] - t_ref[...]
        dy_ref[...] = err * (1.0 / d)
        l_ref[...] += 0.5 * jnp.sum(jnp.sum(err * err, axis=-1, keepdims=True) * (1.0 / d), axis=0, keepdims=True)

    row = lambda i: (i, 0)
    return _pcall(
        body, (y, target), name=name, grid=(t // tm,),
        in_specs=[pl.BlockSpec((tm, d), row), pl.BlockSpec((tm, d), row)],
        out_specs=[pl.BlockSpec((tm, d), row), pl.BlockSpec((8, 128), lambda i: (0, 0))],
        out_shape=[SDS((t, d), F32), SDS((8, 128), F32)],
        sem=("arbitrary",), side=side)


def _acc_rows(ref, val, first):
    @pl.when(first)
    def _():
        ref[...] = val

    @pl.when(jnp.logical_not(first))
    def _():
        ref[...] += val


def _ffn_bwd_mid(dy, z, gamma, wd, h, name, side=None):
    t, d = dy.shape
    dff = wd.shape[0]
    half = dff // 2
    tm = _tile(t, 256)

    def body(dy_ref, z_ref, g_ref, wd_ref, h_ref, dz_ref, df_ref, dh_ref, dg_ref, db_ref):
        dz, dgam, dbet = _ln_bwd(dy_ref[...], z_ref[...], g_ref[...])
        first = pl.program_id(0) == 0
        _acc_rows(dg_ref, dgam, first)
        _acc_rows(db_ref, dbet, first)
        dz_ref[...] = dz
        dfb = (0.5 * dz).astype(BF)
        df_ref[...] = dfb
        for j in range(2):
            c0, c1 = j * half, (j + 1) * half
            da = _dotg(dfb, wd_ref[c0:c1, :], NT)
            g = h_ref[0, :, c0:c1].astype(F32)
            u = h_ref[1, :, c0:c1].astype(F32)
            sg = _sigmoid(g)
            dh_ref[0, :, c0:c1] = (da * u * sg * (1.0 + g * (1.0 - sg))).astype(BF)
            dh_ref[1, :, c0:c1] = (da * g * sg).astype(BF)

    row = lambda i: (i, 0)
    fixed = lambda i: (0, 0)
    return _pcall(
        body, (dy, z, gamma, wd, h), name=name, grid=(t // tm,),
        in_specs=[pl.BlockSpec((tm, d), row), pl.BlockSpec((tm, d), row), pl.BlockSpec((1, d), fixed),
                  pl.BlockSpec(wd.shape, fixed), pl.BlockSpec((2, tm, dff), lambda i: (0, i, 0))],
        out_specs=[pl.BlockSpec((tm, d), row), pl.BlockSpec((tm, d), row),
                   pl.BlockSpec((2, tm, dff), lambda i: (0, i, 0)),
                   pl.BlockSpec((1, d), fixed), pl.BlockSpec((1, d), fixed)],
        out_shape=[SDS((t, d), F32), SDS((t, d), BF), SDS((2, t, dff), BF), SDS((1, d), F32), SDS((1, d), F32)],
        sem=("arbitrary",), side=side)


def _ffn_bwd_dx(dh, wup, dz, alpha, name, side=None):
    _, t, dff = dh.shape
    nq, d, w = wup.shape
    per = dff // w
    tm = _tile(t, 256)

    def body(dh_ref, w_ref, dz_ref, dx_ref):
        acc = alpha * dz_ref[...]
        for q in range(nq):
            c0 = (q % per) * w
            acc = acc + _dotg(dh_ref[q // per, :, c0:c0 + w], w_ref[q], NT)
        dx_ref[...] = acc

    row = lambda i: (i, 0)
    return _pcall(
        body, (dh, wup, dz), name=name, grid=(t // tm,),
        in_specs=[pl.BlockSpec((2, tm, dff), lambda i: (0, i, 0)), pl.BlockSpec(wup.shape, lambda i: (0, 0, 0)),
                  pl.BlockSpec((tm, d), row)],
        out_specs=pl.BlockSpec((tm, d), row), out_shape=SDS((t, d), F32),
        sem=("parallel",), side=side)


def _dw(a, b3, ka, nb, name, side=None):
    t, ka_tot = a.shape
    gb, _, nb_tot = b3.shape
    na, ncb = ka_tot // ka, nb_tot // nb
    tm = _tile(t, 512)

    def body(a_ref, b_ref, o_ref):
        part = _dotg(a_ref[...].astype(BF), b_ref[0], TN)

        @pl.when(pl.program_id(2) == 0)
        def _():
            o_ref[0, 0] = part

        @pl.when(pl.program_id(2) != 0)
        def _():
            o_ref[0, 0] += part

    return _pcall(
        body, (a, b3), name=name, grid=(na, gb * ncb, t // tm),
        in_specs=[pl.BlockSpec((tm, ka), lambda ja, jb, i: (i, ja)),
                  pl.BlockSpec((1, tm, nb), lambda ja, jb, i: (jb // ncb, i, jb % ncb))],
        out_specs=pl.BlockSpec((1, 1, ka, nb), lambda ja, jb, i: (ja, jb, 0, 0)),
        out_shape=SDS((na, gb * ncb, ka, nb), F32),
        sem=("parallel", "parallel", "arbitrary"), side=side)


def _out_bwd(dy, z, gamma, wout, widths, name, side=None):
    t, d = dy.shape
    wa, wb, wc = widths
    tm = _tile(t, 512)

    def body(dy_ref, z_ref, g_ref, w_ref, dz_ref, dzb_ref, da_ref, dbb_ref, dc_ref, dg_ref, db_ref):
        dz, dgam, dbet = _ln_bwd(dy_ref[...], z_ref[...], g_ref[...])
        first = pl.program_id(0) == 0
        _acc_rows(dg_ref, dgam, first)
        _acc_rows(db_ref, dbet, first)
        dz_ref[...] = dz
        dzb = dz.astype(BF)
        dzb_ref[...] = dzb
        da_ref[...] = _dotg(dzb, w_ref[0:wa, :], NT).astype(BF)
        dbb_ref[...] = _dotg(dzb, w_ref[wa:wa + wb, :], NT).astype(BF)
        dc_ref[...] = _dotg(dzb, w_ref[wa + wb:wa + wb + wc, :], NT).astype(BF)

    row = lambda i: (i, 0)
    fixed = lambda i: (0, 0)
    return _pcall(
        body, (dy, z, gamma, wout), name=name, grid=(t // tm,),
        in_specs=[pl.BlockSpec((tm, d), row), pl.BlockSpec((tm, d), row), pl.BlockSpec((1, d), fixed),
                  pl.BlockSpec(wout.shape, fixed)],
        out_specs=[pl.BlockSpec((tm, d), row), pl.BlockSpec((tm, d), row), pl.BlockSpec((tm, wa), row),
                   pl.BlockSpec((tm, wb), row), pl.BlockSpec((tm, wc), row),
                   pl.BlockSpec((1, d), fixed), pl.BlockSpec((1, d), fixed)],
        out_shape=[SDS((t, d), F32), SDS((t, d), BF), SDS((t, wa), BF), SDS((t, wb), BF), SDS((t, wc), BF),
                   SDS((1, d), F32), SDS((1, d), F32)],
        sem=("arbitrary",), side=side)


def _conv_bwd(pc3, dya3, cw, name, side=None):
    b, s, c3 = pc3.shape
    c = c3 // 3

    def body(p_ref, dy_ref, w_ref, dp_ref, dw_ref):
        cb = p_ref[0, :, 0:c]
        cc = p_ref[0, :, c:2 * c]
        ch = p_ref[0, :, 2 * c:3 * c]
        z = cc * ch
        z1 = _shift_rows(z, 1, True)
        z2 = _shift_rows(z, 2, True)
        w0, w1, w2 = w_ref[0:1, :], w_ref[1:2, :], w_ref[2:3, :]
        dy = dy_ref[0].astype(F32)
        dconv = dy * cb
        dz = w2 * dconv + w1 * _shift_rows(dconv, 1, False) + w0 * _shift_rows(dconv, 2, False)
        dp_ref[0, :, 0:c] = (dy * (w0 * z2 + w1 * z1 + w2 * z)).astype(BF)
        dp_ref[0, :, c:2 * c] = (dz * ch).astype(BF)
        dp_ref[0, :, 2 * c:3 * c] = (dz * cc).astype(BF)
        first = pl.program_id(0) == 0
        for r, zs in enumerate((z2, z1, z)):
            _acc_rows(dw_ref.at[r:r + 1], jnp.sum(dconv * zs, axis=0, keepdims=True), first)

    blk = lambda i: (i, 0, 0)
    return _pcall(
        body, (pc3, dya3, cw), name=name, grid=(b,),
        in_specs=[pl.BlockSpec((1, s, c3), blk), pl.BlockSpec((1, s, c), blk), pl.BlockSpec((3, c), lambda i: (0, 0))],
        out_specs=[pl.BlockSpec((1, s, c3), blk), pl.BlockSpec((3, c), lambda i: (0, 0))],
        out_shape=[SDS((b, s, c3), BF), SDS((3, c), F32)],
        sem=("arbitrary",), side=side)


def _fox_bwd(pq3, cum4, dyb3, name, side=None):
    b, s, d3 = pq3.shape
    df = d3 // 3
    hp = df // 128
    tq = _tile(s, FOX_Q_BLOCK)
    scale = FOX_HEAD_DIM ** -0.5

    def body(q_ref, k_ref, v_ref, c_ref, do_ref, dq_ref, dk_ref, dv_ref, dc_ref, dk_acc, dv_acc):
        masks = _head_masks(128)
        dk_acc[...] = jnp.zeros_like(dk_acc)
        dv_acc[...] = jnp.zeros_like(dv_acc)
        dc_ref[...] = jnp.zeros_like(dc_ref)
        for i in range(s // tq):
            lo, hi = i * tq, (i + 1) * tq
            q = q_ref[0, lo:hi, :]
            do = do_ref[0, lo:hi, :]
            k = k_ref[0, 0:hi, :]
            v = v_ref[0, 0:hi, :]
            dq = jnp.zeros((tq, 128), F32)
            for e in range(2):
                qm = jnp.where(masks[e], q, 0)
                dom = jnp.where(masks[e], do, 0)
                p = _fox_probs(qm, k, c_ref[0, 0, e:e + 1, 0:hi], lo)
                dp = _dotg(dom, v, NT)
                ds = p * (dp - jnp.sum(p * dp, axis=-1, keepdims=True))
                dsb = ds.astype(BF)
                dq = jnp.where(masks[e], _dot(dsb, k) * scale, dq)
                dk_acc[0:hi, :] += _dotg(dsb, qm, TN) * scale
                dv_acc[0:hi, :] += _dotg(p.astype(BF), dom, TN)
                dc_ref[0, 0, e:e + 1, 0:hi] -= jnp.sum(ds, axis=0, keepdims=True)
            dq_ref[0, lo:hi, :] = dq.astype(BF)
        dk_ref[0] = dk_acc[...].astype(BF)
        dv_ref[0] = dv_acc[...].astype(BF)

    blk = lambda off: pl.BlockSpec((1, s, 128), lambda i, j: (i, 0, off + j))
    cblk = pl.BlockSpec((1, 1, 2, s), lambda i, j: (i, j, 0, 0))
    return _pcall(
        body, (pq3, pq3, pq3, cum4, dyb3), name=name, grid=(b, hp),
        in_specs=[blk(0), blk(hp), blk(2 * hp), cblk, blk(0)],
        out_specs=[blk(0), blk(0), blk(0), cblk],
        out_shape=[SDS((b, s, df), BF), SDS((b, s, df), BF), SDS((b, s, df), BF), SDS(cum4.shape, F32)],
        scratch_shapes=[pltpu.VMEM((s, 128), F32), pltpu.VMEM((s, 128), F32)],
        sem=("parallel", "parallel"), side=side)


def _cum_bwd(dcum, flog, x3, name, side=None):
    b, h, s = dcum.shape
    d = x3.shape[2]

    def body(dc_ref, fl_ref, x_ref, dfl_ref, dbf_ref, dwf_ref):
        dfl = _prefix_sum_lanes(dc_ref[0], reverse=True) * _sigmoid(-fl_ref[0])
        dfl_ref[0] = dfl
        first = pl.program_id(0) == 0
        _acc_rows(dbf_ref, jnp.broadcast_to(jnp.sum(dfl, axis=-1, keepdims=True), (h, 128)), first)
        dflp = jnp.concatenate([dfl, jnp.zeros((HEAD_ROWS - h, s), F32)], axis=0).astype(BF)
        _acc_rows(dwf_ref, _dot(dflp, x_ref[0].astype(BF))[0:h], first)

    blk = lambda i: (i, 0, 0)
    return _pcall(
        body, (dcum, flog, x3), name=name, grid=(b,),
        in_specs=[pl.BlockSpec((1, h, s), blk), pl.BlockSpec((1, h, s), blk), pl.BlockSpec((1, s, d), blk)],
        out_specs=[pl.BlockSpec((1, h, s), blk), pl.BlockSpec((h, 128), lambda i: (0, 0)),
                   pl.BlockSpec((h, d), lambda i: (0, 0))],
        out_shape=[SDS((b, h, s), F32), SDS((h, 128), F32), SDS((h, d), F32)],
        sem=("arbitrary",), side=side)


def _sgu_bwd(ps, dyc, lng, lnb, ws, bs, name, side=None):
    t, ds2 = ps.shape
    ds = ds2 // 2
    ng, c, _ = ws.shape
    tm = _tile(t, 512, c)

    def body(p_ref, dy_ref, g_ref, b_ref, ws_ref, bs_ref, dp_ref, dws_ref, dbs_ref, dg_ref, db_ref, dvn_acc):
        tri, gmasks, wm, bias = _sgu_consts(ws_ref, bs_ref, ds)
        su = p_ref[:, 0:ds]
        sv = p_ref[:, ds:ds2]
        up = _gelu(su)
        gv = _gelu(sv)
        xhat, rstd = _ln_stats(gv)
        vnb = (xhat * g_ref[...] + b_ref[...]).astype(BF)
        dy = dy_ref[...].astype(F32)
        dws = [jnp.zeros((c, c), F32) for _ in range(ng)]
        dbs = [jnp.zeros((c, 1), F32) for _ in range(ng)]
        for n in range(tm // c):
            r0, r1 = n * c, (n + 1) * c
            mixed = _sgu_mix(wm, vnb[r0:r1], bias, gmasks)
            dp_ref[r0:r1, 0:ds] = (dy[r0:r1] * mixed * _gelu_grad(su[r0:r1])).astype(BF)
            dmix = dy[r0:r1] * up[r0:r1]
            dvn = jnp.zeros((c, ds), F32)
            for g in range(ng):
                dmg = jnp.where(gmasks[g], dmix, 0.0)
                dmb = dmg.astype(BF)
                dws[g] = dws[g] + _dotg(dmb, vnb[r0:r1], NT)
                dbs[g] = dbs[g] + jnp.sum(dmg, axis=-1, keepdims=True)
                dvn = dvn + _dotg(wm[g], dmb, TN)
            dvn_acc[r0:r1, :] = dvn
        dvn_all = dvn_acc[...]
        gdv = dvn_all * g_ref[...]
        m1 = jnp.mean(gdv, axis=-1, keepdims=True)
        m2 = jnp.mean(gdv * xhat, axis=-1, keepdims=True)
        dgv = rstd * (gdv - m1 - xhat * m2)
        dp_ref[:, ds:ds2] = (dgv * _gelu_grad(sv)).astype(BF)
        first = pl.program_id(0) == 0
        _acc_rows(dg_ref, jnp.sum(dvn_all * xhat, axis=0, keepdims=True), first)
        _acc_rows(db_ref, jnp.sum(dvn_all, axis=0, keepdims=True), first)
        for g in range(ng):
            _acc_rows(dws_ref.at[g], jnp.where(tri, dws[g], 0.0), first)
            _acc_rows(dbs_ref.at[g], dbs[g], first)

    row = lambda i: (i, 0)
    fixed2 = lambda i: (0, 0)
    fixed3 = lambda i: (0, 0, 0)
    return _pcall(
        body, (ps, dyc, lng, lnb, ws, bs), name=name, grid=(t // tm,),
        in_specs=[pl.BlockSpec((tm, ds2), row), pl.BlockSpec((tm, ds), row), pl.BlockSpec((1, ds), fixed2),
                  pl.BlockSpec((1, ds), fixed2), pl.BlockSpec(ws.shape, fixed3), pl.BlockSpec(bs.shape, fixed3)],
        out_specs=[pl.BlockSpec((tm, ds2), row), pl.BlockSpec(ws.shape, fixed3), pl.BlockSpec(bs.shape, fixed3),
                   pl.BlockSpec((1, ds), fixed2), pl.BlockSpec((1, ds), fixed2)],
        out_shape=[SDS((t, ds2), BF), SDS(ws.shape, F32), SDS(bs.shape, F32), SDS((1, ds), F32), SDS((1, ds), F32)],
        scratch_shapes=[pltpu.VMEM((tm, ds), F32)],
        sem=("arbitrary",), side=side)


def _mix_bwd_dx(dz, dconv, dq, dk, dv, dsgu, dflog, w_main, wft, seq, alpha, name, side=None):
    t, d = dz.shape
    groups = [dconv, dq, dk, dv, dsgu]
    offs = [0]
    for g in groups:
        offs.append(offs[-1] + g.shape[1])
    h = dflog.shape[1]
    tm = _tile(seq, 512)
    per_seq = seq // tm

    def body(dz_ref, a0, a1, a2, a3, a4, dfl_ref, w_ref, wf_ref, dx_ref):
        dflp = jnp.concatenate([dfl_ref[0], jnp.zeros((HEAD_ROWS - h, tm), F32)], axis=0).astype(BF)
        acc = alpha * dz_ref[...] + _dotg(dflp, wf_ref[...], TN)
        for k, a_ref in enumerate((a0, a1, a2, a3, a4)):
            acc = acc + _dotg(a_ref[...], w_ref[:, offs[k]:offs[k + 1]], NT)
        dx_ref[...] = acc

    row = lambda i: (i, 0)
    return _pcall(
        body, (dz, *groups, dflog, w_main, wft), name=name, grid=(t // tm,),
        in_specs=[pl.BlockSpec((tm, d), row)] + [pl.BlockSpec((tm, g.shape[1]), row) for g in groups]
        + [pl.BlockSpec((1, h, tm), lambda i: (i // per_seq, 0, i % per_seq)),
           pl.BlockSpec(w_main.shape, lambda i: (0, 0)), pl.BlockSpec(wft.shape, lambda i: (0, 0))],
        out_specs=pl.BlockSpec((tm, d), row), out_shape=SDS((t, d), F32),
        sem=("parallel",), side=side)


def _adam_math(w, g, m, v):
    c1 = 1.0 / (1.0 - ADAM_B1 ** ADAM_STEP)
    c2 = 1.0 / (1.0 - ADAM_B2 ** ADAM_STEP)
    nm = ADAM_B1 * m + (1.0 - ADAM_B1) * g
    nv = ADAM_B2 * v + (1.0 - ADAM_B2) * (g * g)
    delta = -ADAM_LR * ((nm * c1) / (jnp.sqrt(nv * c2) + ADAM_EPS) + ADAM_WD * w)
    return delta, nm, nv


def _adamw_small(w, g, m, v, name):
    r, c = w.shape
    tr = _tile(r, 512)

    def body(w_ref, g_ref, m_ref, v_ref, d_ref, nm_ref, nv_ref):
        d_ref[...], nm_ref[...], nv_ref[...] = _adam_math(w_ref[...], g_ref[...], m_ref[...], v_ref[...])

    blk = pl.BlockSpec((tr, c), lambda i: (i, 0))
    return _pcall(body, (w, g, m, v), name=name, grid=(r // tr,), in_specs=[blk] * 4, out_specs=[blk] * 3,
                  out_shape=[SDS((r, c), F32)] * 3, sem=("parallel",))


def _adamw_shard(w, m, v, tot, recv, cq, layer, prev, name, side=None):
    nl, xr, yc = w.shape
    h = xr // 2
    tr = _tile(h, 128)
    nt = h // tr

    def body(cq_ref, w_ref, m_ref, v_ref, t_ref, r_ref, *rest):
        g_ref, d_ref, nm_ref, nv_ref = rest[-4:]
        g = jnp.where(pl.program_id(0) == cq_ref[0], t_ref[...], r_ref[...])
        g_ref[0] = g
        d_ref[0], nm_ref[0], nv_ref[0] = _adam_math(w_ref[0], g, m_ref[0], v_ref[0])

    slab = pl.BlockSpec((1, tr, yc), lambda hf, i, cq_ref: (layer, hf * nt + i, 0))
    mine = pl.BlockSpec((tr, yc), lambda hf, i, cq_ref: (jnp.where(hf == cq_ref[0], i, 0), 0))
    theirs = pl.BlockSpec((tr, yc), lambda hf, i, cq_ref: (jnp.where(hf == cq_ref[0], 0, i), 0))
    operands = [w, m, v, tot, recv]
    in_specs = [slab, slab, slab, mine, theirs]
    aliases = None
    if prev is not None:
        operands += list(prev)
        in_specs += [HBM] * 4
        aliases = {6 + k: k for k in range(4)}
    return _pcall(body, operands, name=name, grid=(2, nt), prefetch=(cq,), in_specs=in_specs,
                  out_specs=[slab] * 4, out_shape=[SDS(w.shape, F32)] * 4, aliases=aliases,
                  sem=("parallel", "parallel"), side=side)


BIG = ("ffn1_w_up", "ffn1_w_down", "mix_w_in", "mix_w_out", "ffn2_w_up", "ffn2_w_down")
SMALL = ("ln1_g", "ln1_b", "fox_b_f", "sgu_ln_g", "sgu_ln_b", "sgu_w_s", "sgu_b_s", "ln2_g", "ln2_b", "ln3_g", "ln3_b")
ORDER = ("ln1_g", "ln1_b", "ffn1_w_up", "ffn1_w_down", "mix_w_in", "fox_b_f", "conv_w", "sgu_ln_g", "sgu_ln_b",
         "sgu_w_s", "sgu_b_s", "mix_w_out", "ln2_g", "ln2_b", "ffn2_w_up", "ffn2_w_down", "ln3_g", "ln3_b")


def _row(v):
    return v.reshape(1, -1)


class _Pipe:
    def __init__(self, stages):
        self.stages = list(stages)
        self.pos = 0
        self.last = None

    def kind(self):
        return self.stages[self.pos][0] if self.pos < len(self.stages) else None


class _Sched:
    def __init__(self):
        self.pipes = []
        self.n_alone = 0

    def add(self, stages):
        self.pipes.append(_Pipe(stages))

    def _take_comms(self, skip=None):
        jobs = []
        for p in self.pipes:
            if p is not skip and p.kind() == "comm":
                jobs.append((p, p.stages[p.pos][1]()))
        return jobs

    @staticmethod
    def _landed(jobs):
        for p, side in jobs:
            p.last = side.results
            p.pos += 1

    def carry(self, builder, *args, **kw):
        jobs = self._take_comms()
        res = builder(*args, side=_join([s for _, s in jobs]), **kw)
        self._landed(jobs)
        self._computes(ride=False)
        return res

    def _computes(self, ride):
        again = True
        while again:
            again = False
            for p in self.pipes:
                if p.kind() == "compute":
                    jobs = self._take_comms(skip=p) if ride else []
                    p.stages[p.pos][1](p.last, _join([s for _, s in jobs]))
                    p.pos += 1
                    self._landed(jobs)
                    again = True

    def drain(self):
        while any(p.kind() is not None for p in self.pipes):
            self._computes(ride=True)
            jobs = self._take_comms()
            if jobs:
                _run_side(_join([s for _, s in jobs]), "exchange_tail_%d" % self.n_alone)
                self.n_alone += 1
                self._landed(jobs)


def _forward_layer(x, p, dims, alpha, l, ride):
    b, s = dims["b"], dims["s"]
    t, d = x.shape
    tag = "l%d_" % l

    def run(stage, builder, *args):
        side, on_done = ride.get(stage, (None, None))
        res = builder(*args, tag + stage, side=side)
        if on_done is not None:
            on_done()
        return res

    h1, a1 = run("ffn1_up", _ffn_up_fwd, x, p["wup1"])
    z1, x1 = run("ffn1_down", _res_ln_fwd, [a1], p["wd1"], x, p["ln1_g"], p["ln1_b"], alpha, 0.5)
    pc, pq, ps = run("mix_proj", _mix_proj_fwd, x1, p["win"], dims["proj_widths"])
    x1_3 = x1.reshape(b, s, d)
    flog, cum = run("fox_gate", _cum_fwd, x1_3, p["wft"], p["bf"])
    nh = flog.shape[1]
    cum4 = cum.reshape(b, nh // 2, 2, s)
    pc3 = pc.reshape(b, s, -1)
    pq3 = pq.reshape(b, s, -1)
    ya = run("conv", _conv_fwd, pc3, p["cw"]).reshape(t, -1)
    yb = run("fox", _fox_fwd, pq3, cum4).reshape(t, -1)
    yc = run("sgu", _sgu_fwd, ps, p["sgu_g"], p["sgu_b"], p["ws"], p["bs"])
    z2, x2 = run("mix_out", _res_ln_fwd, [ya, yb, yc], p["wout"], x1, p["ln2_g"], p["ln2_b"], alpha, 1.0)
    h2, a2 = run("ffn2_up", _ffn_up_fwd, x2, p["wup2"])
    z3, x3 = run("ffn2_down", _res_ln_fwd, [a2], p["wd2"], x2, p["ln3_g"], p["ln3_b"], alpha, 0.5)
    saved = dict(x=x, h1=h1, a1=a1, z1=z1, x1=x1, pc3=pc3, pq3=pq3, ps=ps, flog=flog, cum4=cum4,
                 ya=ya, yb=yb, yc=yc, z2=z2, x2=x2, h2=h2, a2=a2, z3=z3)
    return x3, saved


def _ffn_backward(sched, emit, which, dy, z, gamma, wd, wup, h, a, x_in, alpha, tag, after_mid=None):
    dz, df, dh, dgam, dbet = sched.carry(_ffn_bwd_mid, dy, z, gamma, wd, h, tag + "_bwd_mid")
    if after_mid is not None:
        after_mid(dgam, dbet)
    nq, d, w = wup.shape
    emit(which + "_w_up", sched.carry(_dw, x_in, dh, d, w, tag + "_dw_up")[0])
    half = wd.shape[0] // 2
    emit(which + "_w_down", sched.carry(_dw, a, df[None], half, d, tag + "_dw_down").reshape(nq, -1, d))
    dx = sched.carry(_ffn_bwd_dx, dh, wup, dz, alpha, tag + "_bwd_dx")
    return dx, dgam, dbet


def _backward_layer(sched, emit, emit_small, dy, sv, p, dims, alpha, l):
    b, s = dims["b"], dims["s"]
    tag = "l%d_" % l
    t, d = dy.shape
    g = {}
    dx2, g["ln3_g"], g["ln3_b"] = _ffn_backward(sched, emit, "ffn2", dy, sv["z3"], p["ln3_g"], p["wd2"], p["wup2"],
                                                sv["h2"], sv["a2"], sv["x2"], alpha, tag + "ffn2")
    wa, wb, wc = sv["ya"].shape[1], sv["yb"].shape[1], sv["yc"].shape[1]
    dz2, dz2b, dya, dyb, dyc, g["ln2_g"], g["ln2_b"] = sched.carry(
        _out_bwd, dx2, sv["z2"], p["ln2_g"], p["wout"], (wa, wb, wc), tag + "mix_out_bwd")
    dz2b3 = dz2b[None]
    emit("mix_w_out", jnp.concatenate(
        [sched.carry(_dw, sv[k], dz2b3, sv[k].shape[1], d, tag + "dw_out_" + k)[0, 0] for k in ("ya", "yb", "yc")],
        axis=0).reshape(N_SHARDS, -1, d))
    dpc3, g["conv_w"] = sched.carry(_conv_bwd, sv["pc3"], dya.reshape(b, s, -1), p["cw"], tag + "conv_bwd")
    dq3, dk3, dv3, dcum4 = sched.carry(_fox_bwd, sv["pq3"], sv["cum4"], dyb.reshape(b, s, -1), tag + "fox_bwd")
    nh = sv["flog"].shape[1]
    dflog, dbf, dwft = sched.carry(_cum_bwd, dcum4.reshape(b, nh, s), sv["flog"], sv["x1"].reshape(b, s, d),
                                   tag + "fox_gate_bwd")
    g["fox_b_f"] = dbf[:, 0]
    dps, g["sgu_w_s"], dbs, g["sgu_ln_g"], g["sgu_ln_b"] = sched.carry(
        _sgu_bwd, sv["ps"], dyc, p["sgu_g"], p["sgu_b"], p["ws"], p["bs"], tag + "sgu_bwd")
    g["sgu_b_s"] = dbs[:, :, 0]
    dpc = dpc3.reshape(t, -1)
    dq, dk, dv = dq3.reshape(t, -1), dk3.reshape(t, -1), dv3.reshape(t, -1)
    x1 = sv["x1"]
    cols = [sched.carry(_dw, x1, m[None], d, m.shape[1], tag + "dw_in_" + k)[0, 0]
            for k, m in (("conv", dpc), ("q", dq), ("k", dk), ("v", dv), ("sgu", dps))]
    w_in_grad = jnp.concatenate(cols[:4] + [dwft.T, cols[4]], axis=1)
    emit("mix_w_in", jnp.moveaxis(w_in_grad.reshape(d, N_SHARDS, -1), 1, 0))
    dx1 = sched.carry(_mix_bwd_dx, dz2, dpc, dq, dk, dv, dps, dflog, p["win"], p["wft"], s, alpha, tag + "mix_bwd_dx")

    def small_ready(dgam, dbet):
        g["ln1_g"], g["ln1_b"] = dgam, dbet
        emit_small(g)

    dx0, _, _ = _ffn_backward(sched, emit, "ffn1", dx1, sv["z1"], p["ln1_g"], p["wd1"], p["wup1"],
                              sv["h1"], sv["a1"], sv["x"], alpha, tag + "ffn1", after_mid=small_ready)
    return dx0


def _pack_rows(flat_list):
    v = jnp.concatenate(flat_list)
    n = v.shape[0]
    pad = (-n) % 1024
    return jnp.pad(v, (0, pad)).reshape(-1, 128)


def kernel(x, ln1_g, ln1_b, ffn1_w_up, ffn1_w_down, mix_w_in, fox_b_f, conv_w, sgu_ln_g, sgu_ln_b, sgu_w_s, sgu_b_s, mix_w_out, ln2_g, ln2_b, ffn2_w_up, ffn2_w_down, ln3_g, ln3_b, loss_target, m_ln1_g, m_ln1_b, m_ffn1_w_up, m_ffn1_w_down, m_mix_w_in, m_fox_b_f, m_conv_w, m_sgu_ln_g, m_sgu_ln_b, m_sgu_w_s, m_sgu_b_s, m_mix_w_out, m_ln2_g, m_ln2_b, m_ffn2_w_up, m_ffn2_w_down, m_ln3_g, m_ln3_b, v_ln1_g, v_ln1_b, v_ffn1_w_up, v_ffn1_w_down, v_mix_w_in, v_fox_b_f, v_conv_w, v_sgu_ln_g, v_sgu_ln_b, v_sgu_w_s, v_sgu_b_s, v_mix_w_out, v_ln2_g, v_ln2_b, v_ffn2_w_up, v_ffn2_w_down, v_ln3_g, v_ln3_b):
    wts = dict(ln1_g=ln1_g, ln1_b=ln1_b, ffn1_w_up=ffn1_w_up, ffn1_w_down=ffn1_w_down, mix_w_in=mix_w_in,
               fox_b_f=fox_b_f, conv_w=conv_w, sgu_ln_g=sgu_ln_g, sgu_ln_b=sgu_ln_b, sgu_w_s=sgu_w_s,
               sgu_b_s=sgu_b_s, mix_w_out=mix_w_out, ln2_g=ln2_g, ln2_b=ln2_b, ffn2_w_up=ffn2_w_up,
               ffn2_w_down=ffn2_w_down, ln3_g=ln3_g, ln3_b=ln3_b)
    mom = dict(ln1_g=m_ln1_g, ln1_b=m_ln1_b, ffn1_w_up=m_ffn1_w_up, ffn1_w_down=m_ffn1_w_down, mix_w_in=m_mix_w_in,
               fox_b_f=m_fox_b_f, conv_w=m_conv_w, sgu_ln_g=m_sgu_ln_g, sgu_ln_b=m_sgu_ln_b, sgu_w_s=m_sgu_w_s,
               sgu_b_s=m_sgu_b_s, mix_w_out=m_mix_w_out, ln2_g=m_ln2_g, ln2_b=m_ln2_b, ffn2_w_up=m_ffn2_w_up,
               ffn2_w_down=m_ffn2_w_down, ln3_g=m_ln3_g, ln3_b=m_ln3_b)
    var = dict(ln1_g=v_ln1_g, ln1_b=v_ln1_b, ffn1_w_up=v_ffn1_w_up, ffn1_w_down=v_ffn1_w_down, mix_w_in=v_mix_w_in,
               fox_b_f=v_fox_b_f, conv_w=v_conv_w, sgu_ln_g=v_sgu_ln_g, sgu_ln_b=v_sgu_ln_b, sgu_w_s=v_sgu_w_s,
               sgu_b_s=v_sgu_b_s, mix_w_out=v_mix_w_out, ln2_g=v_ln2_g, ln2_b=v_ln2_b, ffn2_w_up=v_ffn2_w_up,
               ffn2_w_down=v_ffn2_w_down, ln3_g=v_ln3_g, ln3_b=v_ln3_b)

    nl = ln1_g.shape[0]
    b, s, d = x.shape
    t = b * s
    alpha = (2 * nl) ** 0.25
    cw_sh = conv_w.shape[2]
    d_conv = cw_sh * N_SHARDS
    d_sgu = sgu_ln_g.shape[1]
    nh = fox_b_f.shape[1]
    d_fox = nh * FOX_HEAD_DIM
    n_main = 3 * d_conv + 3 * d_fox
    dims = dict(b=b, s=s, proj_widths=(3 * d_conv, 3 * d_fox, 2 * d_sgu))
    cpos = lax.axis_index("c").astype(jnp.int32)
    qpos = (2 * lax.axis_index("x") + lax.axis_index("y")).astype(jnp.int32)
    cq = jnp.stack([cpos, qpos])

    me = (2 * qpos + cpos).reshape(1)
    assert nl == 2, "the gather schedule below names the carriers of a two-layer step"

    conv_tile = jnp.pad(conv_w, ((0, 0), (0, 8 - conv_w.shape[1]), (0, 128 - cw_sh)))
    params = [dict(bf=fox_b_f[l].reshape(nh, 1), sgu_g=_row(sgu_ln_g[l]), sgu_b=_row(sgu_ln_b[l]), ws=sgu_w_s[l],
                   bs=sgu_b_s[l][:, :, None], ln1_g=_row(ln1_g[l]), ln1_b=_row(ln1_b[l]), ln2_g=_row(ln2_g[l]),
                   ln2_b=_row(ln2_b[l]), ln3_g=_row(ln3_g[l]), ln3_b=_row(ln3_b[l])) for l in range(nl)]

    def operands_of(k, arr):
        if k == "mix_w_in":
            w_in = jnp.moveaxis(arr, 0, 1).reshape(d, -1)
            return dict(win=jnp.concatenate([w_in[:, :n_main], w_in[:, n_main + nh:]], axis=1),
                        wft=jnp.pad(w_in[:, n_main:n_main + nh].T, ((0, HEAD_ROWS - nh), (0, 0))))
        if k == "conv_w":
            return dict(cw=jnp.moveaxis(arr[:, :3, :cw_sh], 0, 1).reshape(3, d_conv))
        if k in ("ffn1_w_up", "ffn2_w_up"):
            return {"wup" + k[3]: arr}
        return {dict(ffn1_w_down="wd1", ffn2_w_down="wd2", mix_w_out="wout")[k]: arr.reshape(-1, d)}

    def gather(l, keys):
        side = _side_gather([conv_tile[l] if k == "conv_w" else wts[k][l].astype(BF) for k in keys],
                            [k != "conv_w" for k in keys])

        def install():
            for k, arr in zip(keys, side.results):
                params[l].update(operands_of(k, arr))
        return side, install

    first, install_first = gather(0, ["ffn1_w_up"])
    _run_side(first, "gather_first")
    install_first()
    rides = [{"ffn1_up": gather(0, ["ffn1_w_down", "mix_w_in", "mix_w_out", "conv_w"]),
              "ffn1_down": gather(0, ["ffn2_w_up"]),
              "mix_proj": gather(0, ["ffn2_w_down"]),
              "fox": gather(1, ["ffn1_w_up", "ffn1_w_down", "mix_w_in", "mix_w_out", "conv_w"]),
              "ffn2_up": gather(1, ["ffn2_w_up", "ffn2_w_down"])}, {}]

    act = x.reshape(t, d)
    saved = []
    for l in range(nl):
        act, sv = _forward_layer(act, params[l], dims, alpha, l, rides[l])
        saved.append(sv)
    dy, loss_blk = _loss_fwd(act, loss_target.reshape(t, d), "loss")

    sched = _Sched()
    prev = {k: None for k in BIG}
    red = {}

    def emit_for(l):
        def emit(key, g):
            st = {}
            name = "l%d_%s" % (l, key)

            def pair_sum(res, side):
                st["p"] = _pair_sum(g, res[0], cq, "rs_pair_sum_" + name, side=side)

            def chip_sum(res, side):
                st["t"] = _chip_sum(st["p"], res[0], cq, "rs_chip_sum_" + name, side=side)

            def adamw(res, side):
                prev[key] = _adamw_shard(wts[key], mom[key], var[key], st["t"], res[0], cq, l, prev[key],
                                         "adamw_" + name, side=side)

            sched.add([("comm", lambda: _side_pair_send([g])), ("compute", pair_sum),
                       ("comm", lambda: _side_scatter([st["p"]])), ("compute", chip_sum),
                       ("comm", lambda: _side_pair_share([st["t"]])), ("compute", adamw)])
        return emit

    def emit_small_for(l):
        def emit_small(g):
            flat = [g[k].reshape(-1) for k in SMALL] + [g["conv_w"].reshape(-1)]
            if l == nl - 1:
                flat.append(loss_blk[0, 0:1])
            vec = _pack_rows(flat)

            def slot_sum(res, side):
                red[l] = _sum_slots(vec, res[0], me, "small_sum_l%d" % l, side=side)

            sched.add([("comm", lambda: _side_bcast(vec)), ("compute", slot_sum)])
        return emit_small

    for l in reversed(range(nl)):
        dy = _backward_layer(sched, emit_for(l), emit_small_for(l), dy, saved[l], params[l], dims, alpha, l)
    sched.drain()
    grad_x = dy.reshape(b, s, d)
    gfin, delta, new_m, new_v = {}, {}, {}, {}
    for k in BIG:
        gfin[k], delta[k], new_m[k], new_v[k] = prev[k]

    gsm = {k: [] for k in SMALL + ("conv_w",)}
    for l in range(nl):
        flat_l = red[l].reshape(-1)
        off = 0
        for k in SMALL:
            n = wts[k][l].size
            gsm[k].append(flat_l[off:off + n].reshape(wts[k][l].shape))
            off += n
        n = 3 * d_conv
        gsm["conv_w"].append(lax.dynamic_slice_in_dim(flat_l[off:off + n].reshape(3, d_conv), qpos * cw_sh, cw_sh,
                                                      axis=1))
        off += n
        if l == nl - 1:
            loss = flat_l[off]
    for k in gsm:
        gfin[k] = jnp.stack(gsm[k])
    small_keys = SMALL + ("conv_w",)
    sizes = [wts[k].size for k in small_keys]
    pk = lambda src: _pack_rows([src[k].reshape(-1) for k in small_keys])
    dl, nm, nv = _adamw_small(pk(wts), pk(gfin), pk(mom), pk(var), "adamw_small")
    off = 0
    for k, n in zip(small_keys, sizes):
        shp = wts[k].shape
        delta[k] = dl.reshape(-1)[off:off + n].reshape(shp)
        new_m[k] = nm.reshape(-1)[off:off + n].reshape(shp)
        new_v[k] = nv.reshape(-1)[off:off + n].reshape(shp)
        off += n

    return (loss, grad_x, *[gfin[k] for k in ORDER], *[delta[k] for k in ORDER],
            *[new_m[k] for k in ORDER], *[new_v[k] for k in ORDER])
```

```python
import jax
import jax.numpy as jnp
from jax import lax
from jax.experimental import pallas as pl
from jax.experimental.pallas import tpu as pltpu

F32 = jnp.float32
BF = jnp.bfloat16
SDS = jax.ShapeDtypeStruct
MESH = pl.DeviceIdType.MESH

LN_EPS = 1e-5
FOX_HEAD_DIM = 64
FOX_Q_BLOCK = 256
DW_TOKENS = 2048
HEAD_ROWS = 128
GELU_K = 0.7978845608028654
GELU_C = 0.044715
NEG_BIG = -1e30
N_SHARDS = 4

ADAM_LR = 0.001
ADAM_B1 = 0.9
ADAM_B2 = 0.999
ADAM_EPS = 1e-08
ADAM_WD = 0.01
ADAM_STEP = 10

VMEM_LIMIT_BYTES = 56 * 1024 * 1024
NT = (((1,), (1,)), ((), ()))
TN = (((0,), (0,)), ((), ()))
HBM = pl.BlockSpec(memory_space=pl.ANY)


def _tile(n, pref, mult=8):
    t = min(n, pref)
    while n % t or t % mult:
        t -= mult
    return t


def _dot(a, b):
    return jnp.dot(a, b, preferred_element_type=F32)


def _dotg(a, b, dims):
    return lax.dot_general(a, b, dims, preferred_element_type=F32)


def _sigmoid(x):
    return 1.0 / (1.0 + jnp.exp(-x))


def _gelu(x):
    return 0.5 * x * (1.0 + jnp.tanh(GELU_K * (x + GELU_C * x * x * x)))


def _gelu_grad(x):
    t = jnp.tanh(GELU_K * (x + GELU_C * x * x * x))
    return 0.5 * (1.0 + t) + 0.5 * x * (1.0 - t * t) * GELU_K * (1.0 + 3.0 * GELU_C * x * x)


def _ln_stats(z):
    mu = jnp.mean(z, axis=-1, keepdims=True)
    zc = z - mu
    var = jnp.mean(zc * zc, axis=-1, keepdims=True)
    rstd = lax.rsqrt(var + LN_EPS)
    return zc * rstd, rstd


def _ln_bwd(dy, z, g):
    xhat, rstd = _ln_stats(z)
    gdy = dy * g
    m1 = jnp.mean(gdy, axis=-1, keepdims=True)
    m2 = jnp.mean(gdy * xhat, axis=-1, keepdims=True)
    dz = rstd * (gdy - m1 - xhat * m2)
    return dz, jnp.sum(dy * xhat, axis=0, keepdims=True), jnp.sum(dy, axis=0, keepdims=True)


class _Side:
    def __init__(self, ins, out_shapes, sems, start, finish):
        self.ins, self.out_shapes, self.sems = list(ins), list(out_shapes), list(sems)
        self.start, self.finish = start, finish
        self.results = None


def _join(sides):
    sides = [s for s in sides if s is not None]
    if not sides:
        return None
    ins = [a for s in sides for a in s.ins]
    outs = [a for s in sides for a in s.out_shapes]
    sems = [a for s in sides for a in s.sems]

    def parts(seq, field):
        out, o = [], 0
        for s in sides:
            n = len(getattr(s, field))
            out.append(seq[o:o + n])
            o += n
        return out

    def run(which):
        def fn(i, o, m):
            for s, a, b, c in zip(sides, parts(i, "ins"), parts(o, "out_shapes"), parts(m, "sems")):
                getattr(s, which)(a, b, c)
        return fn

    joined = _Side(ins, outs, sems, run("start"), run("finish"))
    joined.members = sides
    return joined


def _deliver(side, results):
    members = getattr(side, "members", None)
    side.results = list(results)
    if members:
        o = 0
        for s in members:
            n = len(s.out_shapes)
            _deliver(s, results[o:o + n])
            o += n


def _pcall(body, operands, *, name, grid, in_specs, out_specs, out_shape, sem, scratch_shapes=(),
           prefetch=(), aliases=None, side=None):
    single = not isinstance(out_shape, (list, tuple))
    out_shape = [out_shape] if single else list(out_shape)
    out_specs = [out_specs] if single else list(out_specs)
    in_specs, scratch_shapes = list(in_specs), list(scratch_shapes)
    n_pre, n_in, n_out, n_sc = len(prefetch), len(in_specs), len(out_shape), len(scratch_shapes)
    fn = body
    extra = []
    if side is not None:
        s_in, s_out = len(side.ins), len(side.out_shapes)

        def fn(*refs):
            pre, rest = refs[:n_pre], refs[n_pre:]
            m_in, c_in = rest[:n_in], rest[n_in:n_in + s_in]
            rest = rest[n_in + s_in:]
            m_out, c_out = rest[:n_out], rest[n_out:n_out + s_out]
            rest = rest[n_out + s_out:]
            m_sc, c_sc = rest[:n_sc], rest[n_sc:]
            first = pl.program_id(0) == 0
            last = pl.program_id(0) == grid[0] - 1
            for a in range(1, len(grid)):
                first = jnp.logical_and(first, pl.program_id(a) == 0)
                last = jnp.logical_and(last, pl.program_id(a) == grid[a] - 1)

            @pl.when(first)
            def _():
                side.start(c_in, c_out, c_sc)

            body(*pre, *m_in, *m_out, *m_sc)

            @pl.when(last)
            def _():
                side.finish(c_in, c_out, c_sc)

        in_specs = in_specs + [HBM] * s_in
        out_specs = out_specs + [HBM] * s_out
        out_shape = out_shape + side.out_shapes
        scratch_shapes = scratch_shapes + side.sems
        extra = side.ins
        sem = ("arbitrary",) * len(grid)
    params = pltpu.CompilerParams(dimension_semantics=tuple(sem), vmem_limit_bytes=VMEM_LIMIT_BYTES)
    kw = dict(input_output_aliases=aliases) if aliases else {}
    if n_pre:
        spec = pltpu.PrefetchScalarGridSpec(num_scalar_prefetch=n_pre, grid=grid, in_specs=in_specs,
                                            out_specs=out_specs, scratch_shapes=scratch_shapes)
        call = pl.pallas_call(fn, name=name, grid_spec=spec, out_shape=out_shape, compiler_params=params, **kw)
    else:
        call = pl.pallas_call(fn, name=name, grid=grid, in_specs=in_specs, out_specs=out_specs,
                              out_shape=out_shape, scratch_shapes=scratch_shapes, compiler_params=params, **kw)
    res = call(*prefetch, *operands, *extra)
    if side is not None:
        _deliver(side, res[n_out:])
        res = res[:n_out]
    return res[0] if single else res


def _run_side(side, name):
    def body(*refs):
        n_in, n_out = len(side.ins), len(side.out_shapes)
        i, o, m = refs[:n_in], refs[n_in:n_in + n_out], refs[n_in + n_out:]
        side.start(i, o, m)
        side.finish(i, o, m)

    res = pl.pallas_call(body, name=name, in_specs=[HBM] * len(side.ins), out_specs=[HBM] * len(side.out_shapes),
                         out_shape=side.out_shapes, scratch_shapes=side.sems)(*side.ins)
    _deliver(side, res)


def _mesh_pos():
    x, y, c = lax.axis_index("x"), lax.axis_index("y"), lax.axis_index("c")
    chips = [(1 - x, y), (x, 1 - y), (1 - x, 1 - y)]
    return x, y, c, chips


def _rows(ref, lead, half, n_rows):
    return ref.at[tuple(lead) + (pl.ds(half * n_rows, n_rows),)]


def _side_gather(shards, split):
    n = len(shards)
    hs = [w.shape[0] // 2 for w in shards]

    def plan(ins, outs, sems):
        ssem, rsem = sems
        x, y, c, chips = _mesh_pos()
        q = 2 * x + y
        sib = (x, y, 1 - c)

        def rc(p, k, src, dst, to):
            return pltpu.make_async_remote_copy(src_ref=src, dst_ref=dst, send_sem=ssem.at[p, k],
                                                recv_sem=rsem.at[p, k], device_id=to, device_id_type=MESH)

        def blk(ref, p, qi, half):
            return _rows(ref, (qi,), half, hs[p]) if split[p] else ref.at[qi]

        return x, y, c, chips, q, sib, rc, blk

    def first_sends(ins, outs, sems):
        x, y, c, chips, q, sib, rc, blk = plan(ins, outs, sems)
        cps = [rc(p, 0, ins[p], outs[p].at[q], sib) for p in range(n)]
        for j, (cx, cy) in enumerate(chips):
            for p in range(n):
                src = _rows(ins[p], (), c, hs[p]) if split[p] else ins[p]
                cps.append(rc(p, 1 + j, src, blk(outs[p], p, q, c), (cx, cy, c)))
        return cps

    def start(ins, outs, sems):
        for cp in first_sends(ins, outs, sems):
            cp.start()

    def finish(ins, outs, sems):
        x, y, c, chips, q, sib, rc, blk = plan(ins, outs, sems)
        sent = first_sends(ins, outs, sems)
        for j, (cx, cy) in enumerate(chips):
            qj = 2 * cx + cy
            for p in range(n):
                got = blk(outs[p], p, qj, c)
                rc(p, 1 + j, got, got, (cx, cy, c)).wait_recv()
                if split[p]:
                    fwd = rc(p, 4 + j, got, got, sib)
                    fwd.start()
                    sent.append(fwd)
        for j, (cx, cy) in enumerate(chips):
            qj = 2 * cx + cy
            for p in range(n):
                if split[p]:
                    got = blk(outs[p], p, qj, 1 - c)
                    rc(p, 4 + j, got, got, sib).wait_recv()
        for p in range(n):
            rc(p, 0, outs[p].at[q], outs[p].at[q], sib).wait_recv()
        for cp in sent:
            cp.wait_send()

    return _Side(shards, [SDS((N_SHARDS,) + w.shape, w.dtype) for w in shards],
                 [pltpu.SemaphoreType.DMA((n, 7)), pltpu.SemaphoreType.DMA((n, 7))], start, finish)


def _side_pair_send(gs):
    n = len(gs)

    def copies(ins, outs, sems):
        x, y, c, _ = _mesh_pos()
        return [pltpu.make_async_remote_copy(
            src_ref=ins[p].at[:, pl.ds((1 - c) * (gs[p].shape[1] // 2), gs[p].shape[1] // 2)], dst_ref=outs[p],
            send_sem=sems[0].at[p], recv_sem=sems[1].at[p], device_id=(x, y, 1 - c), device_id_type=MESH)
            for p in range(n)]

    def start(ins, outs, sems):
        for cp in copies(ins, outs, sems):
            cp.start()

    def finish(ins, outs, sems):
        for cp in copies(ins, outs, sems):
            cp.wait()

    return _Side(gs, [SDS((g.shape[0], g.shape[1] // 2, g.shape[2]), g.dtype) for g in gs],
                 [pltpu.SemaphoreType.DMA((n,)), pltpu.SemaphoreType.DMA((n,))], start, finish)


def _side_scatter(ps):
    n = len(ps)

    def sends(ins, outs, sems):
        x, y, c, chips = _mesh_pos()
        q = 2 * x + y
        return [pltpu.make_async_remote_copy(src_ref=ins[p].at[2 * cx + cy], dst_ref=outs[p].at[q],
                                             send_sem=sems[0].at[p, j], recv_sem=sems[1].at[p, j],
                                             device_id=(cx, cy, c), device_id_type=MESH)
                for j, (cx, cy) in enumerate(chips) for p in range(n)]

    def start(ins, outs, sems):
        for cp in sends(ins, outs, sems):
            cp.start()

    def finish(ins, outs, sems):
        x, y, c, chips = _mesh_pos()
        for j, (cx, cy) in enumerate(chips):
            for p in range(n):
                got = outs[p].at[2 * cx + cy]
                pltpu.make_async_remote_copy(src_ref=got, dst_ref=got, send_sem=sems[0].at[p, j],
                                             recv_sem=sems[1].at[p, j], device_id=(cx, cy, c),
                                             device_id_type=MESH).wait_recv()
        for cp in sends(ins, outs, sems):
            cp.wait_send()

    return _Side(ps, [SDS(p.shape, p.dtype) for p in ps],
                 [pltpu.SemaphoreType.DMA((n, 3)), pltpu.SemaphoreType.DMA((n, 3))], start, finish)


def _side_pair_share(tots):
    n = len(tots)

    def copies(ins, outs, sems):
        x, y, c, _ = _mesh_pos()
        return [pltpu.make_async_remote_copy(src_ref=ins[p], dst_ref=outs[p], send_sem=sems[0].at[p],
                                             recv_sem=sems[1].at[p], device_id=(x, y, 1 - c), device_id_type=MESH)
                for p in range(n)]

    def start(ins, outs, sems):
        for cp in copies(ins, outs, sems):
            cp.start()

    def finish(ins, outs, sems):
        for cp in copies(ins, outs, sems):
            cp.wait()

    return _Side(tots, [SDS(t_.shape, t_.dtype) for t_ in tots],
                 [pltpu.SemaphoreType.DMA((n,)), pltpu.SemaphoreType.DMA((n,))], start, finish)


N_DEVICES = 8


def _side_bcast(v):
    def peers():
        x, y, c, _ = _mesh_pos()
        out = []
        for k in range(1, N_DEVICES):
            px, py, pc = x ^ ((k >> 2) & 1), y ^ ((k >> 1) & 1), c ^ (k & 1)
            out.append((k - 1, (px, py, pc), 4 * px + 2 * py + pc))
        return 4 * x + 2 * y + c, out

    def sends(ins, outs, sems):
        me, ps = peers()
        return [pltpu.make_async_remote_copy(src_ref=ins[0], dst_ref=outs[0].at[me], send_sem=sems[0].at[k],
                                             recv_sem=sems[1].at[k], device_id=to, device_id_type=MESH)
                for k, to, _ in ps]

    def start(ins, outs, sems):
        for cp in sends(ins, outs, sems):
            cp.start()

    def finish(ins, outs, sems):
        _, ps = peers()
        for k, to, slot in ps:
            got = outs[0].at[slot]
            pltpu.make_async_remote_copy(src_ref=got, dst_ref=got, send_sem=sems[0].at[k], recv_sem=sems[1].at[k],
                                         device_id=to, device_id_type=MESH).wait_recv()
        for cp in sends(ins, outs, sems):
            cp.wait_send()

    return _Side([v], [SDS((N_DEVICES,) + v.shape, v.dtype)],
                 [pltpu.SemaphoreType.DMA((N_DEVICES - 1,)), pltpu.SemaphoreType.DMA((N_DEVICES - 1,))], start, finish)


def _sum_slots(v, r, me, name, side=None):
    n, rows, lanes = r.shape
    tr = _tile(rows, 512)

    def body(me_ref, v_ref, r_ref, o_ref):
        j = pl.program_id(1)
        term = jnp.where(j == me_ref[0], v_ref[...], r_ref[0])

        @pl.when(j == 0)
        def _():
            o_ref[...] = term

        @pl.when(j != 0)
        def _():
            o_ref[...] += term

    other = lambda j, k: jnp.where(j == k, (k + 1) % n, j)
    return _pcall(
        body, (v, r), name=name, grid=(rows // tr, n), prefetch=(me,),
        in_specs=[pl.BlockSpec((tr, lanes), lambda i, j, me_ref: (i, 0)),
                  pl.BlockSpec((1, tr, lanes), lambda i, j, me_ref: (other(j, me_ref[0]), i, 0))],
        out_specs=pl.BlockSpec((tr, lanes), lambda i, j, me_ref: (i, 0)),
        out_shape=SDS((rows, lanes), F32), sem=("parallel", "arbitrary"), side=side)


def _pair_sum(g, r1, cq, name, side=None):
    nq, xr, yc = g.shape
    h = xr // 2
    tr = _tile(h, 512, 16)
    nt = h // tr

    def body(cq_ref, g_ref, r_ref, o_ref):
        o_ref[...] = (g_ref[...] + r_ref[...]).astype(BF)

    return _pcall(
        body, (g, r1), name=name, grid=(nq, nt), prefetch=(cq,),
        in_specs=[pl.BlockSpec((1, tr, yc), lambda j, i, cq_ref: (j, cq_ref[0] * nt + i, 0)),
                  pl.BlockSpec((1, tr, yc), lambda j, i, cq_ref: (j, i, 0))],
        out_specs=pl.BlockSpec((1, tr, yc), lambda j, i, cq_ref: (j, i, 0)),
        out_shape=SDS((nq, h, yc), BF), sem=("parallel", "parallel"), side=side)


def _chip_sum(p, r2, cq, name, side=None):
    nq, h, yc = r2.shape
    tr = _tile(h, 512, 16)

    def body(cq_ref, p_ref, r_ref, o_ref):
        j = pl.program_id(1)
        term = jnp.where(j == cq_ref[1], p_ref[0], r_ref[0]).astype(F32)

        @pl.when(j == 0)
        def _():
            o_ref[...] = term

        @pl.when(j != 0)
        def _():
            o_ref[...] += term

    other = lambda j, q: jnp.where(j == q, (q + 1) % nq, j)
    return _pcall(
        body, (p, r2), name=name, grid=(h // tr, nq), prefetch=(cq,),
        in_specs=[pl.BlockSpec((1, tr, yc), lambda i, j, cq_ref: (cq_ref[1], i, 0)),
                  pl.BlockSpec((1, tr, yc), lambda i, j, cq_ref: (other(j, cq_ref[1]), i, 0))],
        out_specs=pl.BlockSpec((tr, yc), lambda i, j, cq_ref: (i, 0)),
        out_shape=SDS((h, yc), F32), sem=("parallel", "arbitrary"), side=side)


def _ffn_up_fwd(x, wup, name, side=None, emit_xt=False):
    t, d = x.shape
    w = wup.shape[2]
    tm = _tile(t, 512)

    def body(x_ref, wg_ref, wu_ref, h_ref, a_ref, at_ref, *xt_ref):
        xb = x_ref[...].astype(BF)
        g = _dot(xb, wg_ref[0])
        u = _dot(xb, wu_ref[0])
        h_ref[0] = g.astype(BF)
        h_ref[1] = u.astype(BF)
        ab = (g * _sigmoid(g) * u).astype(BF)
        a_ref[...] = ab
        at_ref[...] = ab.T
        if emit_xt:
            @pl.when(pl.program_id(0) == 0)
            def _():
                xt_ref[0][...] = xb.T

    nt = t // tm
    out_specs = [pl.BlockSpec((2, tm, w), lambda j, i: (0, i, j)), pl.BlockSpec((tm, w), lambda j, i: (i, j)),
                 pl.BlockSpec((w, tm), lambda j, i: (j, i))]
    out_shape = [SDS((2, t, 2 * w), BF), SDS((t, 2 * w), BF), SDS((2 * w, t), BF)]
    if emit_xt:
        out_specs.append(pl.BlockSpec((d, tm), lambda j, i: (0, jnp.where(j == 0, i, nt - 1))))
        out_shape.append(SDS((d, t), BF))
    return _pcall(
        body, (x, wup, wup), name=name, grid=(2, nt),
        in_specs=[pl.BlockSpec((tm, d), lambda j, i: (i, 0)),
                  pl.BlockSpec((1, d, w), lambda j, i: (j, 0, 0)),
                  pl.BlockSpec((1, d, w), lambda j, i: (j + 2, 0, 0))],
        out_specs=out_specs, out_shape=out_shape,
        sem=("arbitrary", "arbitrary") if emit_xt else ("parallel", "parallel"), side=side)


def _res_ln_fwd(parts, w, x, gamma, beta, alpha, res_scale, name, side=None, parts_t=False):
    t, d = x.shape
    n = len(parts)
    offs = [0]
    for p in parts:
        offs.append(offs[-1] + p.shape[1])
    tm = _tile(t, 512)

    def body(*refs):
        p_refs = refs[:n]
        w_ref, x_ref, g_ref, b_ref, z_ref, y_ref, yt_ref = refs[n:n + 7]
        pt_refs = refs[n + 7:]
        f = _dot(p_refs[0][...], w_ref[offs[0]:offs[1], :])
        for k in range(1, n):
            f = f + _dot(p_refs[k][...], w_ref[offs[k]:offs[k + 1], :])
        z = alpha * x_ref[...] + res_scale * f
        z_ref[...] = z
        xhat, _ = _ln_stats(z)
        y = xhat * g_ref[...] + b_ref[...]
        y_ref[...] = y
        yt_ref[...] = y.astype(BF).T
        for k in range(len(pt_refs)):
            pt_refs[k][...] = p_refs[k][...].T

    row = lambda i: (i, 0)
    col = lambda i: (0, i)
    fixed = lambda i: (0, 0)
    out_specs = [pl.BlockSpec((tm, d), row), pl.BlockSpec((tm, d), row), pl.BlockSpec((d, tm), col)]
    out_shape = [SDS((t, d), F32), SDS((t, d), F32), SDS((d, t), BF)]
    if parts_t:
        out_specs += [pl.BlockSpec((p.shape[1], tm), col) for p in parts]
        out_shape += [SDS((p.shape[1], t), BF) for p in parts]
    return _pcall(
        body, (*parts, w, x, gamma, beta), name=name, grid=(t // tm,),
        in_specs=[pl.BlockSpec((tm, p.shape[1]), row) for p in parts]
        + [pl.BlockSpec(w.shape, fixed), pl.BlockSpec((tm, d), row),
           pl.BlockSpec((1, d), fixed), pl.BlockSpec((1, d), fixed)],
        out_specs=out_specs, out_shape=out_shape,
        sem=("parallel",), side=side)


def _mix_proj_fwd(x, w_main, widths, name, side=None):
    t, d = x.shape
    dc, dq, ds = widths
    tm = _tile(t, 512)

    def body(x_ref, w_ref, pc_ref, pq_ref, ps_ref):
        xb = x_ref[...].astype(BF)
        pc_ref[...] = _dot(xb, w_ref[:, 0:dc])
        pq_ref[...] = _dot(xb, w_ref[:, dc:dc + dq]).astype(BF)
        ps_ref[...] = _dot(xb, w_ref[:, dc + dq:dc + dq + ds])

    row = lambda i: (i, 0)
    return _pcall(
        body, (x, w_main), name=name, grid=(t // tm,),
        in_specs=[pl.BlockSpec((tm, d), row), pl.BlockSpec(w_main.shape, lambda i: (0, 0))],
        out_specs=[pl.BlockSpec((tm, dc), row), pl.BlockSpec((tm, dq), row), pl.BlockSpec((tm, ds), row)],
        out_shape=[SDS((t, dc), F32), SDS((t, dq), BF), SDS((t, ds), F32)],
        sem=("parallel",), side=side)


def _prefix_sum_lanes(v, reverse):
    n = v.shape[-1]
    lane = lax.broadcasted_iota(jnp.int32, v.shape, v.ndim - 1)
    sh = 1
    while sh < n:
        if reverse:
            v = v + jnp.where(lane < n - sh, pltpu.roll(v, n - sh, axis=v.ndim - 1), 0.0)
        else:
            v = v + jnp.where(lane >= sh, pltpu.roll(v, sh, axis=v.ndim - 1), 0.0)
        sh *= 2
    return v


def _cum_fwd(x3, wft, bf, name, side=None):
    b, s, d = x3.shape
    h = bf.shape[0]

    def body(x_ref, w_ref, b_ref, fl_ref, cum_ref):
        fl = _dotg(w_ref[...], x_ref[0].astype(BF), NT)[0:h] + b_ref[...]
        fl_ref[0] = fl
        lf = jnp.minimum(fl, 0.0) - jnp.log(1.0 + jnp.exp(-jnp.abs(fl)))
        cum_ref[0] = _prefix_sum_lanes(lf, reverse=False)

    return _pcall(
        body, (x3, wft, bf), name=name, grid=(b,),
        in_specs=[pl.BlockSpec((1, s, d), lambda i: (i, 0, 0)),
                  pl.BlockSpec(wft.shape, lambda i: (0, 0)), pl.BlockSpec((h, 1), lambda i: (0, 0))],
        out_specs=[pl.BlockSpec((1, h, s), lambda i: (i, 0, 0)), pl.BlockSpec((1, h, s), lambda i: (i, 0, 0))],
        out_shape=[SDS((b, h, s), F32), SDS((b, h, s), F32)],
        sem=("parallel",), side=side)


def _shift_rows(z, k, down):
    n = z.shape[0]
    row = lax.broadcasted_iota(jnp.int32, z.shape, 0)
    if down:
        return jnp.where(row >= k, pltpu.roll(z, k, axis=0), 0.0)
    return jnp.where(row < n - k, pltpu.roll(z, n - k, axis=0), 0.0)


def _conv_fwd(pc3, cw, name, side=None):
    b, s, c3 = pc3.shape
    c = c3 // 3

    def body(p_ref, w_ref, y_ref):
        z = p_ref[0, :, c:2 * c] * p_ref[0, :, 2 * c:3 * c]
        conv = w_ref[0:1, :] * _shift_rows(z, 2, True) + w_ref[1:2, :] * _shift_rows(z, 1, True) + w_ref[2:3, :] * z
        y_ref[0] = (p_ref[0, :, 0:c] * conv).astype(BF)

    return _pcall(
        body, (pc3, cw), name=name, grid=(b,),
        in_specs=[pl.BlockSpec((1, s, c3), lambda i: (i, 0, 0)), pl.BlockSpec((3, c), lambda i: (0, 0))],
        out_specs=pl.BlockSpec((1, s, c), lambda i: (i, 0, 0)),
        out_shape=SDS((b, s, c), BF), sem=("parallel",), side=side)


def _head_masks(width):
    lane = lax.broadcasted_iota(jnp.int32, (1, width), 1)
    return [lane < FOX_HEAD_DIM, lane >= FOX_HEAD_DIM]


def _fox_probs(q, k, cum_row, lo, head_mask):
    tq = q.shape[0]
    qm = jnp.where(head_mask, q * (FOX_HEAD_DIM ** -0.5), 0)
    s = _dotg(qm, k, NT) - cum_row
    tri = lax.broadcasted_iota(jnp.int32, (tq, tq), 1) <= lax.broadcasted_iota(jnp.int32, (tq, tq), 0)
    diag = jnp.where(tri, s[:, lo:], NEG_BIG)
    s = diag if lo == 0 else jnp.concatenate([s[:, :lo], diag], axis=1)
    p = jnp.exp(s - jnp.max(s, axis=-1, keepdims=True))
    return p, 1.0 / jnp.sum(p, axis=-1, keepdims=True), qm


def _fox_fwd(pq3, cum4, name, side=None):
    b, s, d3 = pq3.shape
    df = d3 // 3
    hp = df // 128
    tq = _tile(s, FOX_Q_BLOCK)

    def body(q_ref, k_ref, v_ref, c_ref, o_ref):
        masks = _head_masks(128)
        for i in range(s // tq):
            lo, hi = i * tq, (i + 1) * tq
            q = q_ref[0, lo:hi, :]
            k = k_ref[0, 0:hi, :]
            v = v_ref[0, 0:hi, :]
            o = jnp.zeros((tq, 128), F32)
            for e in range(2):
                p, inv, _ = _fox_probs(q, k, c_ref[0, 0, e:e + 1, 0:hi], lo, masks[e])
                o = jnp.where(masks[e], _dot(p.astype(BF), v) * inv, o)
            o_ref[0, lo:hi, :] = o.astype(BF)

    blk = lambda off: pl.BlockSpec((1, s, 128), lambda i, j: (i, 0, off + j))
    return _pcall(
        body, (pq3, pq3, pq3, cum4), name=name, grid=(b, hp),
        in_specs=[blk(0), blk(hp), blk(2 * hp), pl.BlockSpec((1, 1, 2, s), lambda i, j: (i, j, 0, 0))],
        out_specs=blk(0), out_shape=SDS((b, s, df), BF),
        sem=("parallel", "parallel"), side=side)


def _sgu_mix(wm, vnb, bias, gmasks):
    out = bias
    for g in range(len(wm)):
        out = out + jnp.where(gmasks[g], _dot(wm[g], vnb), 0.0)
    return out


def _sgu_consts(ws_ref, bs_ref, ds):
    ng, c, _ = ws_ref.shape
    gd = ds // ng
    tri = lax.broadcasted_iota(jnp.int32, (c, c), 0) >= lax.broadcasted_iota(jnp.int32, (c, c), 1)
    lane = lax.broadcasted_iota(jnp.int32, (1, ds), 1)
    gmasks = [(lane >= g * gd) & (lane < (g + 1) * gd) for g in range(ng)]
    wm = [jnp.where(tri, ws_ref[g], 0.0).astype(BF) for g in range(ng)]
    bias = jnp.zeros((c, ds), F32)
    for g in range(ng):
        bias = jnp.where(gmasks[g], bs_ref[g], bias)
    return tri, gmasks, wm, bias


def _sgu_fwd(ps, lng, lnb, ws, bs, name, side=None):
    t, ds2 = ps.shape
    ds = ds2 // 2
    c = ws.shape[1]
    tm = _tile(t, 512, c)

    def body(p_ref, g_ref, b_ref, ws_ref, bs_ref, y_ref):
        _, gmasks, wm, bias = _sgu_consts(ws_ref, bs_ref, ds)
        up = _gelu(p_ref[:, 0:ds])
        xhat, _ = _ln_stats(_gelu(p_ref[:, ds:ds2]))
        vnb = (xhat * g_ref[...] + b_ref[...]).astype(BF)
        for n in range(tm // c):
            r0, r1 = n * c, (n + 1) * c
            y_ref[r0:r1, :] = (up[r0:r1] * _sgu_mix(wm, vnb[r0:r1], bias, gmasks)).astype(BF)

    fixed2 = lambda i: (0, 0)
    fixed3 = lambda i: (0, 0, 0)
    return _pcall(
        body, (ps, lng, lnb, ws, bs), name=name, grid=(t // tm,),
        in_specs=[pl.BlockSpec((tm, ds2), lambda i: (i, 0)), pl.BlockSpec((1, ds), fixed2),
                  pl.BlockSpec((1, ds), fixed2), pl.BlockSpec(ws.shape, fixed3), pl.BlockSpec(bs.shape, fixed3)],
        out_specs=pl.BlockSpec((tm, ds), lambda i: (i, 0)),
        out_shape=SDS((t, ds), BF), sem=("parallel",), side=side)


def _loss_fwd(y, target, name, side=None):
    t, d = y.shape
    tm = _tile(t, 512)

    def body(y_ref, t_ref, dy_ref, l_ref):
        @pl.when(pl.program_id(0) == 0)
        def _():
            l_ref[...] = jnp.zeros_like(l_ref)

        err = y_ref[...] - t_ref[...]
        dy_ref[...] = err * (1.0 / d)
        l_ref[...] += 0.5 * jnp.sum(jnp.sum(err * err, axis=-1, keepdims=True) * (1.0 / d), axis=0, keepdims=True)

    row = lambda i: (i, 0)
    return _pcall(
        body, (y, target), name=name, grid=(t // tm,),
        in_specs=[pl.BlockSpec((tm, d), row), pl.BlockSpec((tm, d), row)],
        out_specs=[pl.BlockSpec((tm, d), row), pl.BlockSpec((8, 128), lambda i: (0, 0))],
        out_shape=[SDS((t, d), F32), SDS((8, 128), F32)],
        sem=("arbitrary",), side=side)


def _acc_rows(ref, val, first):
    @pl.when(first)
    def _():
        ref[...] = val

    @pl.when(jnp.logical_not(first))
    def _():
        ref[...] += val


def _ffn_bwd_mid(dy, z, gamma, wd, h, name, side=None):
    t, d = dy.shape
    dff = wd.shape[0]
    half = dff // 2
    tm = _tile(t, 256)

    def body(dy_ref, z_ref, g_ref, wd_ref, h_ref, dz_ref, df_ref, dh_ref, dg_ref, db_ref):
        dz, dgam, dbet = _ln_bwd(dy_ref[...], z_ref[...], g_ref[...])
        first = pl.program_id(0) == 0
        _acc_rows(dg_ref, dgam, first)
        _acc_rows(db_ref, dbet, first)
        dz_ref[...] = dz
        dfb = (0.5 * dz).astype(BF)
        df_ref[...] = dfb
        for j in range(2):
            c0, c1 = j * half, (j + 1) * half
            da = _dotg(dfb, wd_ref[c0:c1, :], NT)
            g = h_ref[0, :, c0:c1].astype(F32)
            u = h_ref[1, :, c0:c1].astype(F32)
            sg = _sigmoid(g)
            dh_ref[0, :, c0:c1] = (da * u * sg * (1.0 + g * (1.0 - sg))).astype(BF)
            dh_ref[1, :, c0:c1] = (da * g * sg).astype(BF)

    row = lambda i: (i, 0)
    fixed = lambda i: (0, 0)
    return _pcall(
        body, (dy, z, gamma, wd, h), name=name, grid=(t // tm,),
        in_specs=[pl.BlockSpec((tm, d), row), pl.BlockSpec((tm, d), row), pl.BlockSpec((1, d), fixed),
                  pl.BlockSpec(wd.shape, fixed), pl.BlockSpec((2, tm, dff), lambda i: (0, i, 0))],
        out_specs=[pl.BlockSpec((tm, d), row), pl.BlockSpec((tm, d), row),
                   pl.BlockSpec((2, tm, dff), lambda i: (0, i, 0)),
                   pl.BlockSpec((1, d), fixed), pl.BlockSpec((1, d), fixed)],
        out_shape=[SDS((t, d), F32), SDS((t, d), BF), SDS((2, t, dff), BF), SDS((1, d), F32), SDS((1, d), F32)],
        sem=("arbitrary",), side=side)


def _ffn_bwd_dx(dh, wup, dz, alpha, name, side=None):
    _, t, dff = dh.shape
    nq, d, w = wup.shape
    per = dff // w
    tm = _tile(t, 256)

    def body(dh_ref, w_ref, dz_ref, dx_ref):
        acc = alpha * dz_ref[...]
        for q in range(nq):
            c0 = (q % per) * w
            acc = acc + _dotg(dh_ref[q // per, :, c0:c0 + w], w_ref[q], NT)
        dx_ref[...] = acc

    row = lambda i: (i, 0)
    return _pcall(
        body, (dh, wup, dz), name=name, grid=(t // tm,),
        in_specs=[pl.BlockSpec((2, tm, dff), lambda i: (0, i, 0)), pl.BlockSpec(wup.shape, lambda i: (0, 0, 0)),
                  pl.BlockSpec((tm, d), row)],
        out_specs=pl.BlockSpec((tm, d), row), out_shape=SDS((t, d), F32),
        sem=("parallel",), side=side)


def _dw(at, b3, ka, nb, name, side=None):
    ka_tot, t = at.shape
    gb, _, nb_tot = b3.shape
    na, ncb = ka_tot // ka, nb_tot // nb
    tm = _tile(t, DW_TOKENS, 128)

    def body(a_ref, b_ref, o_ref):
        part = _dot(a_ref[...], b_ref[0])

        @pl.when(pl.program_id(2) == 0)
        def _():
            o_ref[0, 0] = part

        @pl.when(pl.program_id(2) != 0)
        def _():
            o_ref[0, 0] += part

    return _pcall(
        body, (at, b3), name=name, grid=(na, gb * ncb, t // tm),
        in_specs=[pl.BlockSpec((ka, tm), lambda ja, jb, i: (ja, i)),
                  pl.BlockSpec((1, tm, nb), lambda ja, jb, i: (jb // ncb, i, jb % ncb))],
        out_specs=pl.BlockSpec((1, 1, ka, nb), lambda ja, jb, i: (ja, jb, 0, 0)),
        out_shape=SDS((na, gb * ncb, ka, nb), F32),
        sem=("parallel", "parallel", "arbitrary"), side=side)


def _out_bwd(dy, z, gamma, wout, widths, name, side=None):
    t, d = dy.shape
    wa, wb, wc = widths
    tm = _tile(t, 512)

    def body(dy_ref, z_ref, g_ref, w_ref, dz_ref, dzb_ref, da_ref, dbb_ref, dc_ref, dg_ref, db_ref):
        dz, dgam, dbet = _ln_bwd(dy_ref[...], z_ref[...], g_ref[...])
        first = pl.program_id(0) == 0
        _acc_rows(dg_ref, dgam, first)
        _acc_rows(db_ref, dbet, first)
        dz_ref[...] = dz
        dzb = dz.astype(BF)
        dzb_ref[...] = dzb
        da_ref[...] = _dotg(dzb, w_ref[0:wa, :], NT).astype(BF)
        dbb_ref[...] = _dotg(dzb, w_ref[wa:wa + wb, :], NT).astype(BF)
        dc_ref[...] = _dotg(dzb, w_ref[wa + wb:wa + wb + wc, :], NT).astype(BF)

    row = lambda i: (i, 0)
    fixed = lambda i: (0, 0)
    return _pcall(
        body, (dy, z, gamma, wout), name=name, grid=(t // tm,),
        in_specs=[pl.BlockSpec((tm, d), row), pl.BlockSpec((tm, d), row), pl.BlockSpec((1, d), fixed),
                  pl.BlockSpec(wout.shape, fixed)],
        out_specs=[pl.BlockSpec((tm, d), row), pl.BlockSpec((tm, d), row), pl.BlockSpec((tm, wa), row),
                   pl.BlockSpec((tm, wb), row), pl.BlockSpec((tm, wc), row),
                   pl.BlockSpec((1, d), fixed), pl.BlockSpec((1, d), fixed)],
        out_shape=[SDS((t, d), F32), SDS((t, d), BF), SDS((t, wa), BF), SDS((t, wb), BF), SDS((t, wc), BF),
                   SDS((1, d), F32), SDS((1, d), F32)],
        sem=("arbitrary",), side=side)


def _conv_bwd(pc3, dya3, cw, name, side=None):
    b, s, c3 = pc3.shape
    c = c3 // 3

    def body(p_ref, dy_ref, w_ref, dp_ref, dw_ref):
        cb = p_ref[0, :, 0:c]
        cc = p_ref[0, :, c:2 * c]
        ch = p_ref[0, :, 2 * c:3 * c]
        z = cc * ch
        z1 = _shift_rows(z, 1, True)
        z2 = _shift_rows(z, 2, True)
        w0, w1, w2 = w_ref[0:1, :], w_ref[1:2, :], w_ref[2:3, :]
        dy = dy_ref[0].astype(F32)
        dconv = dy * cb
        dz = w2 * dconv + w1 * _shift_rows(dconv, 1, False) + w0 * _shift_rows(dconv, 2, False)
        dp_ref[0, :, 0:c] = (dy * (w0 * z2 + w1 * z1 + w2 * z)).astype(BF)
        dp_ref[0, :, c:2 * c] = (dz * ch).astype(BF)
        dp_ref[0, :, 2 * c:3 * c] = (dz * cc).astype(BF)
        first = pl.program_id(0) == 0
        for r, zs in enumerate((z2, z1, z)):
            _acc_rows(dw_ref.at[r:r + 1], jnp.sum(dconv * zs, axis=0, keepdims=True), first)

    blk = lambda i: (i, 0, 0)
    return _pcall(
        body, (pc3, dya3, cw), name=name, grid=(b,),
        in_specs=[pl.BlockSpec((1, s, c3), blk), pl.BlockSpec((1, s, c), blk), pl.BlockSpec((3, c), lambda i: (0, 0))],
        out_specs=[pl.BlockSpec((1, s, c3), blk), pl.BlockSpec((3, c), lambda i: (0, 0))],
        out_shape=[SDS((b, s, c3), BF), SDS((3, c), F32)],
        sem=("arbitrary",), side=side)


def _fox_bwd(pq3, cum4, dyb3, name, side=None):
    b, s, d3 = pq3.shape
    df = d3 // 3
    hp = df // 128
    tq = _tile(s, FOX_Q_BLOCK)
    scale = FOX_HEAD_DIM ** -0.5

    def body(q_ref, k_ref, v_ref, c_ref, do_ref, dq_ref, dk_ref, dv_ref, dc_ref, dk_acc, dv_acc):
        masks = _head_masks(128)
        dk_acc[...] = jnp.zeros_like(dk_acc)
        dv_acc[...] = jnp.zeros_like(dv_acc)
        dc_ref[...] = jnp.zeros_like(dc_ref)
        for i in range(s // tq):
            lo, hi = i * tq, (i + 1) * tq
            q = q_ref[0, lo:hi, :]
            do = do_ref[0, lo:hi, :]
            k = k_ref[0, 0:hi, :]
            v = v_ref[0, 0:hi, :]
            dq = jnp.zeros((tq, 128), F32)
            for e in range(2):
                dom = jnp.where(masks[e], do, 0)
                p, inv, qm = _fox_probs(q, k, c_ref[0, 0, e:e + 1, 0:hi], lo, masks[e])
                p = p * inv
                dp = _dotg(dom, v, NT)
                ds = p * (dp - jnp.sum(p * dp, axis=-1, keepdims=True))
                dsb = ds.astype(BF)
                dq = jnp.where(masks[e], _dot(dsb, k) * scale, dq)
                dk_acc[0:hi, :] += _dotg(dsb, qm, TN)
                dv_acc[0:hi, :] += _dotg(p.astype(BF), dom, TN)
                dc_ref[0, 0, e:e + 1, 0:hi] -= jnp.sum(ds, axis=0, keepdims=True)
            dq_ref[0, lo:hi, :] = dq.astype(BF)
        dk_ref[0] = dk_acc[...].astype(BF)
        dv_ref[0] = dv_acc[...].astype(BF)

    blk = lambda off: pl.BlockSpec((1, s, 128), lambda i, j: (i, 0, off + j))
    cblk = pl.BlockSpec((1, 1, 2, s), lambda i, j: (i, j, 0, 0))
    return _pcall(
        body, (pq3, pq3, pq3, cum4, dyb3), name=name, grid=(b, hp),
        in_specs=[blk(0), blk(hp), blk(2 * hp), cblk, blk(0)],
        out_specs=[blk(0), blk(0), blk(0), cblk],
        out_shape=[SDS((b, s, df), BF), SDS((b, s, df), BF), SDS((b, s, df), BF), SDS(cum4.shape, F32)],
        scratch_shapes=[pltpu.VMEM((s, 128), F32), pltpu.VMEM((s, 128), F32)],
        sem=("parallel", "parallel"), side=side)


def _cum_bwd(dcum, flog, x3, name, side=None):
    b, h, s = dcum.shape
    d = x3.shape[2]

    def body(dc_ref, fl_ref, x_ref, dfl_ref, dbf_ref, dwf_ref):
        dfl = _prefix_sum_lanes(dc_ref[0], reverse=True) * _sigmoid(-fl_ref[0])
        dfl_ref[0] = dfl
        first = pl.program_id(0) == 0
        _acc_rows(dbf_ref, jnp.broadcast_to(jnp.sum(dfl, axis=-1, keepdims=True), (h, 128)), first)
        dflp = jnp.concatenate([dfl, jnp.zeros((HEAD_ROWS - h, s), F32)], axis=0).astype(BF)
        _acc_rows(dwf_ref, _dot(dflp, x_ref[0].astype(BF))[0:h], first)

    blk = lambda i: (i, 0, 0)
    return _pcall(
        body, (dcum, flog, x3), name=name, grid=(b,),
        in_specs=[pl.BlockSpec((1, h, s), blk), pl.BlockSpec((1, h, s), blk), pl.BlockSpec((1, s, d), blk)],
        out_specs=[pl.BlockSpec((1, h, s), blk), pl.BlockSpec((h, 128), lambda i: (0, 0)),
                   pl.BlockSpec((h, d), lambda i: (0, 0))],
        out_shape=[SDS((b, h, s), F32), SDS((h, 128), F32), SDS((h, d), F32)],
        sem=("arbitrary",), side=side)


def _sgu_bwd(ps, dyc, lng, lnb, ws, bs, name, side=None):
    t, ds2 = ps.shape
    ds = ds2 // 2
    ng, c, _ = ws.shape
    tm = _tile(t, 512, c)

    def body(p_ref, dy_ref, g_ref, b_ref, ws_ref, bs_ref, dp_ref, dws_ref, dbs_ref, dg_ref, db_ref, dvn_acc):
        tri, gmasks, wm, bias = _sgu_consts(ws_ref, bs_ref, ds)
        su = p_ref[:, 0:ds]
        sv = p_ref[:, ds:ds2]
        up = _gelu(su)
        gv = _gelu(sv)
        xhat, rstd = _ln_stats(gv)
        vnb = (xhat * g_ref[...] + b_ref[...]).astype(BF)
        dy = dy_ref[...].astype(F32)
        dws = [jnp.zeros((c, c), F32) for _ in range(ng)]
        dbs = [jnp.zeros((c, 1), F32) for _ in range(ng)]
        for n in range(tm // c):
            r0, r1 = n * c, (n + 1) * c
            mixed = _sgu_mix(wm, vnb[r0:r1], bias, gmasks)
            dp_ref[r0:r1, 0:ds] = (dy[r0:r1] * mixed * _gelu_grad(su[r0:r1])).astype(BF)
            dmix = dy[r0:r1] * up[r0:r1]
            dvn = jnp.zeros((c, ds), F32)
            for g in range(ng):
                dmg = jnp.where(gmasks[g], dmix, 0.0)
                dmb = dmg.astype(BF)
                dws[g] = dws[g] + _dotg(dmb, vnb[r0:r1], NT)
                dbs[g] = dbs[g] + jnp.sum(dmg, axis=-1, keepdims=True)
                dvn = dvn + _dotg(wm[g], dmb, TN)
            dvn_acc[r0:r1, :] = dvn
        dvn_all = dvn_acc[...]
        gdv = dvn_all * g_ref[...]
        m1 = jnp.mean(gdv, axis=-1, keepdims=True)
        m2 = jnp.mean(gdv * xhat, axis=-1, keepdims=True)
        dgv = rstd * (gdv - m1 - xhat * m2)
        dp_ref[:, ds:ds2] = (dgv * _gelu_grad(sv)).astype(BF)
        first = pl.program_id(0) == 0
        _acc_rows(dg_ref, jnp.sum(dvn_all * xhat, axis=0, keepdims=True), first)
        _acc_rows(db_ref, jnp.sum(dvn_all, axis=0, keepdims=True), first)
        for g in range(ng):
            _acc_rows(dws_ref.at[g], jnp.where(tri, dws[g], 0.0), first)
            _acc_rows(dbs_ref.at[g], dbs[g], first)

    row = lambda i: (i, 0)
    fixed2 = lambda i: (0, 0)
    fixed3 = lambda i: (0, 0, 0)
    return _pcall(
        body, (ps, dyc, lng, lnb, ws, bs), name=name, grid=(t // tm,),
        in_specs=[pl.BlockSpec((tm, ds2), row), pl.BlockSpec((tm, ds), row), pl.BlockSpec((1, ds), fixed2),
                  pl.BlockSpec((1, ds), fixed2), pl.BlockSpec(ws.shape, fixed3), pl.BlockSpec(bs.shape, fixed3)],
        out_specs=[pl.BlockSpec((tm, ds2), row), pl.BlockSpec(ws.shape, fixed3), pl.BlockSpec(bs.shape, fixed3),
                   pl.BlockSpec((1, ds), fixed2), pl.BlockSpec((1, ds), fixed2)],
        out_shape=[SDS((t, ds2), BF), SDS(ws.shape, F32), SDS(bs.shape, F32), SDS((1, ds), F32), SDS((1, ds), F32)],
        scratch_shapes=[pltpu.VMEM((tm, ds), F32)],
        sem=("arbitrary",), side=side)


def _mix_bwd_dx(dz, dconv, dq, dk, dv, dsgu, dflog, w_main, wft, seq, alpha, name, side=None):
    t, d = dz.shape
    groups = [dconv, dq, dk, dv, dsgu]
    offs = [0]
    for g in groups:
        offs.append(offs[-1] + g.shape[1])
    h = dflog.shape[1]
    tm = _tile(seq, 512)
    per_seq = seq // tm

    def body(dz_ref, a0, a1, a2, a3, a4, dfl_ref, w_ref, wf_ref, dx_ref):
        dflp = jnp.concatenate([dfl_ref[0], jnp.zeros((HEAD_ROWS - h, tm), F32)], axis=0).astype(BF)
        acc = alpha * dz_ref[...] + _dotg(dflp, wf_ref[...], TN)
        for k, a_ref in enumerate((a0, a1, a2, a3, a4)):
            acc = acc + _dotg(a_ref[...], w_ref[:, offs[k]:offs[k + 1]], NT)
        dx_ref[...] = acc

    row = lambda i: (i, 0)
    return _pcall(
        body, (dz, *groups, dflog, w_main, wft), name=name, grid=(t // tm,),
        in_specs=[pl.BlockSpec((tm, d), row)] + [pl.BlockSpec((tm, g.shape[1]), row) for g in groups]
        + [pl.BlockSpec((1, h, tm), lambda i: (i // per_seq, 0, i % per_seq)),
           pl.BlockSpec(w_main.shape, lambda i: (0, 0)), pl.BlockSpec(wft.shape, lambda i: (0, 0))],
        out_specs=pl.BlockSpec((tm, d), row), out_shape=SDS((t, d), F32),
        sem=("parallel",), side=side)


def _adam_math(w, g, m, v):
    c1 = 1.0 / (1.0 - ADAM_B1 ** ADAM_STEP)
    c2 = 1.0 / (1.0 - ADAM_B2 ** ADAM_STEP)
    nm = ADAM_B1 * m + (1.0 - ADAM_B1) * g
    nv = ADAM_B2 * v + (1.0 - ADAM_B2) * (g * g)
    delta = -ADAM_LR * ((nm * c1) / (jnp.sqrt(nv * c2) + ADAM_EPS) + ADAM_WD * w)
    return delta, nm, nv


def _adamw_small(w, g, m, v, name):
    r, c = w.shape
    tr = _tile(r, 512)

    def body(w_ref, g_ref, m_ref, v_ref, d_ref, nm_ref, nv_ref):
        d_ref[...], nm_ref[...], nv_ref[...] = _adam_math(w_ref[...], g_ref[...], m_ref[...], v_ref[...])

    blk = pl.BlockSpec((tr, c), lambda i: (i, 0))
    return _pcall(body, (w, g, m, v), name=name, grid=(r // tr,), in_specs=[blk] * 4, out_specs=[blk] * 3,
                  out_shape=[SDS((r, c), F32)] * 3, sem=("parallel",))


def _adamw_shard(w, m, v, tot, recv, cq, layer, prev, name, side=None):
    nl, xr, yc = w.shape
    h = xr // 2
    tr = _tile(h, 128)
    nt = h // tr

    def body(cq_ref, w_ref, m_ref, v_ref, t_ref, r_ref, *rest):
        g_ref, d_ref, nm_ref, nv_ref = rest[-4:]
        g = jnp.where(pl.program_id(0) == cq_ref[0], t_ref[...], r_ref[...])
        g_ref[0] = g
        d_ref[0], nm_ref[0], nv_ref[0] = _adam_math(w_ref[0], g, m_ref[0], v_ref[0])

    slab = pl.BlockSpec((1, tr, yc), lambda hf, i, cq_ref: (layer, hf * nt + i, 0))
    mine = pl.BlockSpec((tr, yc), lambda hf, i, cq_ref: (jnp.where(hf == cq_ref[0], i, 0), 0))
    theirs = pl.BlockSpec((tr, yc), lambda hf, i, cq_ref: (jnp.where(hf == cq_ref[0], 0, i), 0))
    operands = [w, m, v, tot, recv]
    in_specs = [slab, slab, slab, mine, theirs]
    aliases = None
    if prev is not None:
        operands += list(prev)
        in_specs += [HBM] * 4
        aliases = {6 + k: k for k in range(4)}
    return _pcall(body, operands, name=name, grid=(2, nt), prefetch=(cq,), in_specs=in_specs,
                  out_specs=[slab] * 4, out_shape=[SDS(w.shape, F32)] * 4, aliases=aliases,
                  sem=("parallel", "parallel"), side=side)


BIG = ("ffn1_w_up", "ffn1_w_down", "mix_w_in", "mix_w_out", "ffn2_w_up", "ffn2_w_down")
SMALL = ("ln1_g", "ln1_b", "fox_b_f", "sgu_ln_g", "sgu_ln_b", "sgu_w_s", "sgu_b_s", "ln2_g", "ln2_b", "ln3_g", "ln3_b")
ORDER = ("ln1_g", "ln1_b", "ffn1_w_up", "ffn1_w_down", "mix_w_in", "fox_b_f", "conv_w", "sgu_ln_g", "sgu_ln_b",
         "sgu_w_s", "sgu_b_s", "mix_w_out", "ln2_g", "ln2_b", "ffn2_w_up", "ffn2_w_down", "ln3_g", "ln3_b")


def _row(v):
    return v.reshape(1, -1)


class _Pipe:
    def __init__(self, stages):
        self.stages = list(stages)
        self.pos = 0
        self.last = None

    def kind(self):
        return self.stages[self.pos][0] if self.pos < len(self.stages) else None


class _Sched:
    def __init__(self):
        self.pipes = []
        self.n_alone = 0

    def add(self, stages):
        self.pipes.append(_Pipe(stages))

    def _take_comms(self, skip=None):
        jobs = []
        for p in self.pipes:
            if p is not skip and p.kind() == "comm":
                jobs.append((p, p.stages[p.pos][1]()))
        return jobs

    @staticmethod
    def _landed(jobs):
        for p, side in jobs:
            p.last = side.results
            p.pos += 1

    def carry(self, builder, *args, **kw):
        jobs = self._take_comms()
        res = builder(*args, side=_join([s for _, s in jobs]), **kw)
        self._landed(jobs)
        self._computes(ride=False)
        return res

    def _computes(self, ride):
        again = True
        while again:
            again = False
            for p in self.pipes:
                if p.kind() == "compute":
                    jobs = self._take_comms(skip=p) if ride else []
                    p.stages[p.pos][1](p.last, _join([s for _, s in jobs]))
                    p.pos += 1
                    self._landed(jobs)
                    again = True

    def drain(self):
        while any(p.kind() is not None for p in self.pipes):
            self._computes(ride=True)
            jobs = self._take_comms()
            if jobs:
                _run_side(_join([s for _, s in jobs]), "exchange_tail_%d" % self.n_alone)
                self.n_alone += 1
                self._landed(jobs)


def _forward_layer(x, xt, p, dims, alpha, l, ride):
    b, s = dims["b"], dims["s"]
    t, d = x.shape
    tag = "l%d_" % l

    def run(stage, builder, *args, **kw):
        side, on_done = ride.get(stage, (None, None))
        res = builder(*args, tag + stage, side=side, **kw)
        if on_done is not None:
            on_done()
        return res

    if xt is None:
        h1, a1, a1t, xt = run("ffn1_up", _ffn_up_fwd, x, p["wup1"], emit_xt=True)
    else:
        h1, a1, a1t = run("ffn1_up", _ffn_up_fwd, x, p["wup1"])
    z1, x1, x1t = run("ffn1_down", _res_ln_fwd, [a1], p["wd1"], x, p["ln1_g"], p["ln1_b"], alpha, 0.5)
    pc, pq, ps = run("mix_proj", _mix_proj_fwd, x1, p["win"], dims["proj_widths"])
    x1_3 = x1.reshape(b, s, d)
    flog, cum = run("fox_gate", _cum_fwd, x1_3, p["wft"], p["bf"])
    nh = flog.shape[1]
    cum4 = cum.reshape(b, nh // 2, 2, s)
    pc3 = pc.reshape(b, s, -1)
    pq3 = pq.reshape(b, s, -1)
    ya = run("conv", _conv_fwd, pc3, p["cw"]).reshape(t, -1)
    yb = run("fox", _fox_fwd, pq3, cum4).reshape(t, -1)
    yc = run("sgu", _sgu_fwd, ps, p["sgu_g"], p["sgu_b"], p["ws"], p["bs"])
    z2, x2, x2t, yat, ybt, yct = run("mix_out", _res_ln_fwd, [ya, yb, yc], p["wout"], x1, p["ln2_g"], p["ln2_b"],
                                      alpha, 1.0, parts_t=True)
    h2, a2, a2t = run("ffn2_up", _ffn_up_fwd, x2, p["wup2"])
    z3, x3, x3t = run("ffn2_down", _res_ln_fwd, [a2], p["wd2"], x2, p["ln3_g"], p["ln3_b"], alpha, 0.5)
    saved = dict(xt=xt, h1=h1, a1t=a1t, z1=z1, x1=x1, x1t=x1t, pc3=pc3, pq3=pq3, ps=ps, flog=flog, cum4=cum4,
                 yat=yat, ybt=ybt, yct=yct, z2=z2, x2t=x2t, h2=h2, a2t=a2t, z3=z3)
    return x3, x3t, saved


def _ffn_backward(sched, emit, which, dy, z, gamma, wd, wup, h, a_t, x_in_t, alpha, tag, after_mid=None):
    dz, df, dh, dgam, dbet = sched.carry(_ffn_bwd_mid, dy, z, gamma, wd, h, tag + "_bwd_mid")
    if after_mid is not None:
        after_mid(dgam, dbet)
    nq, d, w = wup.shape
    emit(which + "_w_up", sched.carry(_dw, x_in_t, dh, d, w, tag + "_dw_up")[0])
    half = wd.shape[0] // 2
    emit(which + "_w_down", sched.carry(_dw, a_t, df[None], half, d, tag + "_dw_down").reshape(nq, -1, d))
    dx = sched.carry(_ffn_bwd_dx, dh, wup, dz, alpha, tag + "_bwd_dx")
    return dx, dgam, dbet


def _backward_layer(sched, emit, emit_small, dy, sv, p, dims, alpha, l):
    b, s = dims["b"], dims["s"]
    tag = "l%d_" % l
    t, d = dy.shape
    g = {}
    dx2, g["ln3_g"], g["ln3_b"] = _ffn_backward(sched, emit, "ffn2", dy, sv["z3"], p["ln3_g"], p["wd2"], p["wup2"],
                                                sv["h2"], sv["a2t"], sv["x2t"], alpha, tag + "ffn2")
    wa, wb, wc = sv["yat"].shape[0], sv["ybt"].shape[0], sv["yct"].shape[0]
    dz2, dz2b, dya, dyb, dyc, g["ln2_g"], g["ln2_b"] = sched.carry(
        _out_bwd, dx2, sv["z2"], p["ln2_g"], p["wout"], (wa, wb, wc), tag + "mix_out_bwd")
    dz2b3 = dz2b[None]
    emit("mix_w_out", jnp.concatenate(
        [sched.carry(_dw, sv[k], dz2b3, sv[k].shape[0], d, tag + "dw_out_" + k)[0, 0] for k in ("yat", "ybt", "yct")],
        axis=0).reshape(N_SHARDS, -1, d))
    dpc3, g["conv_w"] = sched.carry(_conv_bwd, sv["pc3"], dya.reshape(b, s, -1), p["cw"], tag + "conv_bwd")
    dq3, dk3, dv3, dcum4 = sched.carry(_fox_bwd, sv["pq3"], sv["cum4"], dyb.reshape(b, s, -1), tag + "fox_bwd")
    nh = sv["flog"].shape[1]
    dflog, dbf, dwft = sched.carry(_cum_bwd, dcum4.reshape(b, nh, s), sv["flog"], sv["x1"].reshape(b, s, d),
                                   tag + "fox_gate_bwd")
    g["fox_b_f"] = dbf[:, 0]
    dps, g["sgu_w_s"], dbs, g["sgu_ln_g"], g["sgu_ln_b"] = sched.carry(
        _sgu_bwd, sv["ps"], dyc, p["sgu_g"], p["sgu_b"], p["ws"], p["bs"], tag + "sgu_bwd")
    g["sgu_b_s"] = dbs[:, :, 0]
    dpc = dpc3.reshape(t, -1)
    dq, dk, dv = dq3.reshape(t, -1), dk3.reshape(t, -1), dv3.reshape(t, -1)
    cols = [sched.carry(_dw, sv["x1t"], m[None], d, m.shape[1], tag + "dw_in_" + k)[0, 0]
            for k, m in (("conv", dpc), ("q", dq), ("k", dk), ("v", dv), ("sgu", dps))]
    w_in_grad = jnp.concatenate(cols[:4] + [dwft.T, cols[4]], axis=1)
    emit("mix_w_in", jnp.moveaxis(w_in_grad.reshape(d, N_SHARDS, -1), 1, 0))
    dx1 = sched.carry(_mix_bwd_dx, dz2, dpc, dq, dk, dv, dps, dflog, p["win"], p["wft"], s, alpha, tag + "mix_bwd_dx")

    def small_ready(dgam, dbet):
        g["ln1_g"], g["ln1_b"] = dgam, dbet
        emit_small(g)

    dx0, _, _ = _ffn_backward(sched, emit, "ffn1", dx1, sv["z1"], p["ln1_g"], p["wd1"], p["wup1"],
                              sv["h1"], sv["a1t"], sv["xt"], alpha, tag + "ffn1", after_mid=small_ready)
    return dx0


def _pack_rows(flat_list):
    v = jnp.concatenate(flat_list)
    n = v.shape[0]
    pad = (-n) % 1024
    return jnp.pad(v, (0, pad)).reshape(-1, 128)


def kernel(x, ln1_g, ln1_b, ffn1_w_up, ffn1_w_down, mix_w_in, fox_b_f, conv_w, sgu_ln_g, sgu_ln_b, sgu_w_s, sgu_b_s, mix_w_out, ln2_g, ln2_b, ffn2_w_up, ffn2_w_down, ln3_g, ln3_b, loss_target, m_ln1_g, m_ln1_b, m_ffn1_w_up, m_ffn1_w_down, m_mix_w_in, m_fox_b_f, m_conv_w, m_sgu_ln_g, m_sgu_ln_b, m_sgu_w_s, m_sgu_b_s, m_mix_w_out, m_ln2_g, m_ln2_b, m_ffn2_w_up, m_ffn2_w_down, m_ln3_g, m_ln3_b, v_ln1_g, v_ln1_b, v_ffn1_w_up, v_ffn1_w_down, v_mix_w_in, v_fox_b_f, v_conv_w, v_sgu_ln_g, v_sgu_ln_b, v_sgu_w_s, v_sgu_b_s, v_mix_w_out, v_ln2_g, v_ln2_b, v_ffn2_w_up, v_ffn2_w_down, v_ln3_g, v_ln3_b):
    wts = dict(ln1_g=ln1_g, ln1_b=ln1_b, ffn1_w_up=ffn1_w_up, ffn1_w_down=ffn1_w_down, mix_w_in=mix_w_in,
               fox_b_f=fox_b_f, conv_w=conv_w, sgu_ln_g=sgu_ln_g, sgu_ln_b=sgu_ln_b, sgu_w_s=sgu_w_s,
               sgu_b_s=sgu_b_s, mix_w_out=mix_w_out, ln2_g=ln2_g, ln2_b=ln2_b, ffn2_w_up=ffn2_w_up,
               ffn2_w_down=ffn2_w_down, ln3_g=ln3_g, ln3_b=ln3_b)
    mom = dict(ln1_g=m_ln1_g, ln1_b=m_ln1_b, ffn1_w_up=m_ffn1_w_up, ffn1_w_down=m_ffn1_w_down, mix_w_in=m_mix_w_in,
               fox_b_f=m_fox_b_f, conv_w=m_conv_w, sgu_ln_g=m_sgu_ln_g, sgu_ln_b=m_sgu_ln_b, sgu_w_s=m_sgu_w_s,
               sgu_b_s=m_sgu_b_s, mix_w_out=m_mix_w_out, ln2_g=m_ln2_g, ln2_b=m_ln2_b, ffn2_w_up=m_ffn2_w_up,
               ffn2_w_down=m_ffn2_w_down, ln3_g=m_ln3_g, ln3_b=m_ln3_b)
    var = dict(ln1_g=v_ln1_g, ln1_b=v_ln1_b, ffn1_w_up=v_ffn1_w_up, ffn1_w_down=v_ffn1_w_down, mix_w_in=v_mix_w_in,
               fox_b_f=v_fox_b_f, conv_w=v_conv_w, sgu_ln_g=v_sgu_ln_g, sgu_ln_b=v_sgu_ln_b, sgu_w_s=v_sgu_w_s,
               sgu_b_s=v_sgu_b_s, mix_w_out=v_mix_w_out, ln2_g=v_ln2_g, ln2_b=v_ln2_b, ffn2_w_up=v_ffn2_w_up,
               ffn2_w_down=v_ffn2_w_down, ln3_g=v_ln3_g, ln3_b=v_ln3_b)

    nl = ln1_g.shape[0]
    b, s, d = x.shape
    t = b * s
    alpha = (2 * nl) ** 0.25
    cw_sh = conv_w.shape[2]
    d_conv = cw_sh * N_SHARDS
    d_sgu = sgu_ln_g.shape[1]
    nh = fox_b_f.shape[1]
    d_fox = nh * FOX_HEAD_DIM
    n_main = 3 * d_conv + 3 * d_fox
    dims = dict(b=b, s=s, proj_widths=(3 * d_conv, 3 * d_fox, 2 * d_sgu))
    cpos = lax.axis_index("c").astype(jnp.int32)
    qpos = (2 * lax.axis_index("x") + lax.axis_index("y")).astype(jnp.int32)
    cq = jnp.stack([cpos, qpos])

    me = (2 * qpos + cpos).reshape(1)
    assert nl == 2, "the gather schedule below names the carriers of a two-layer step"

    conv_tile = jnp.pad(conv_w, ((0, 0), (0, 8 - conv_w.shape[1]), (0, 128 - cw_sh)))
    params = [dict(bf=fox_b_f[l].reshape(nh, 1), sgu_g=_row(sgu_ln_g[l]), sgu_b=_row(sgu_ln_b[l]), ws=sgu_w_s[l],
                   bs=sgu_b_s[l][:, :, None], ln1_g=_row(ln1_g[l]), ln1_b=_row(ln1_b[l]), ln2_g=_row(ln2_g[l]),
                   ln2_b=_row(ln2_b[l]), ln3_g=_row(ln3_g[l]), ln3_b=_row(ln3_b[l])) for l in range(nl)]

    def operands_of(k, arr):
        if k == "mix_w_in":
            w_in = jnp.moveaxis(arr, 0, 1).reshape(d, -1)
            return dict(win=jnp.concatenate([w_in[:, :n_main], w_in[:, n_main + nh:]], axis=1),
                        wft=jnp.pad(w_in[:, n_main:n_main + nh].T, ((0, HEAD_ROWS - nh), (0, 0))))
        if k == "conv_w":
            return dict(cw=jnp.moveaxis(arr[:, :3, :cw_sh], 0, 1).reshape(3, d_conv))
        if k in ("ffn1_w_up", "ffn2_w_up"):
            return {"wup" + k[3]: arr}
        return {dict(ffn1_w_down="wd1", ffn2_w_down="wd2", mix_w_out="wout")[k]: arr.reshape(-1, d)}

    def gather(l, keys):
        side = _side_gather([conv_tile[l] if k == "conv_w" else wts[k][l].astype(BF) for k in keys],
                            [k != "conv_w" for k in keys])

        def install():
            for k, arr in zip(keys, side.results):
                params[l].update(operands_of(k, arr))
        return side, install

    first, install_first = gather(0, ["ffn1_w_up"])
    _run_side(first, "gather_first")
    install_first()
    rides = [{"ffn1_up": gather(0, ["ffn1_w_down", "mix_w_in", "mix_w_out", "conv_w"]),
              "ffn1_down": gather(0, ["ffn2_w_up"]),
              "mix_proj": gather(0, ["ffn2_w_down"]),
              "fox": gather(1, ["ffn1_w_up", "ffn1_w_down", "mix_w_in", "mix_w_out", "conv_w"]),
              "ffn2_up": gather(1, ["ffn2_w_up", "ffn2_w_down"])}, {}]

    act, act_t = x.reshape(t, d), None
    saved = []
    for l in range(nl):
        act, act_t, sv = _forward_layer(act, act_t, params[l], dims, alpha, l, rides[l])
        saved.append(sv)
    dy, loss_blk = _loss_fwd(act, loss_target.reshape(t, d), "loss")

    sched = _Sched()
    prev = {k: None for k in BIG}
    red = {}

    def emit_for(l):
        def emit(key, g):
            st = {}
            name = "l%d_%s" % (l, key)

            def pair_sum(res, side):
                st["p"] = _pair_sum(g, res[0], cq, "rs_pair_sum_" + name, side=side)

            def chip_sum(res, side):
                st["t"] = _chip_sum(st["p"], res[0], cq, "rs_chip_sum_" + name, side=side)

            def adamw(res, side):
                prev[key] = _adamw_shard(wts[key], mom[key], var[key], st["t"], res[0], cq, l, prev[key],
                                         "adamw_" + name, side=side)

            sched.add([("comm", lambda: _side_pair_send([g])), ("compute", pair_sum),
                       ("comm", lambda: _side_scatter([st["p"]])), ("compute", chip_sum),
                       ("comm", lambda: _side_pair_share([st["t"]])), ("compute", adamw)])
        return emit

    def emit_small_for(l):
        def emit_small(g):
            flat = [g[k].reshape(-1) for k in SMALL] + [g["conv_w"].reshape(-1)]
            if l == nl - 1:
                flat.append(loss_blk[0, 0:1])
            vec = _pack_rows(flat)

            def slot_sum(res, side):
                red[l] = _sum_slots(vec, res[0], me, "small_sum_l%d" % l, side=side)

            sched.add([("comm", lambda: _side_bcast(vec)), ("compute", slot_sum)])
        return emit_small

    for l in reversed(range(nl)):
        dy = _backward_layer(sched, emit_for(l), emit_small_for(l), dy, saved[l], params[l], dims, alpha, l)
    sched.drain()
    grad_x = dy.reshape(b, s, d)
    gfin, delta, new_m, new_v = {}, {}, {}, {}
    for k in BIG:
        gfin[k], delta[k], new_m[k], new_v[k] = prev[k]

    gsm = {k: [] for k in SMALL + ("conv_w",)}
    for l in range(nl):
        flat_l = red[l].reshape(-1)
        off = 0
        for k in SMALL:
            n = wts[k][l].size
            gsm[k].append(flat_l[off:off + n].reshape(wts[k][l].shape))
            off += n
        n = 3 * d_conv
        gsm["conv_w"].append(lax.dynamic_slice_in_dim(flat_l[off:off + n].reshape(3, d_conv), qpos * cw_sh, cw_sh,
                                                      axis=1))
        off += n
        if l == nl - 1:
            loss = flat_l[off]
    for k in gsm:
        gfin[k] = jnp.stack(gsm[k])
    small_keys = SMALL + ("conv_w",)
    sizes = [wts[k].size for k in small_keys]
    pk = lambda src: _pack_rows([src[k].reshape(-1) for k in small_keys])
    dl, nm, nv = _adamw_small(pk(wts), pk(gfin), pk(mom), pk(var), "adamw_small")
    off = 0
    for k, n in zip(small_keys, sizes):
        shp = wts[k].shape
        delta[k] = dl.reshape(-1)[off:off + n].reshape(shp)
        new_m[k] = nm.reshape(-1)[off:off + n].reshape(shp)
        new_v[k] = nv.reshape(-1)[off:off + n].reshape(shp)
        off += n

    return (loss, grad_x, *[gfin[k] for k in ORDER], *[delta[k] for k in ORDER],
            *[new_m[k] for k in ORDER], *[new_v[k] for k in ORDER])
```

```python
import jax
import jax.numpy as jnp
from jax import lax
from jax.experimental import pallas as pl
from jax.experimental.pallas import tpu as pltpu

F32 = jnp.float32
BF = jnp.bfloat16
SDS = jax.ShapeDtypeStruct
MESH = pl.DeviceIdType.MESH

LN_EPS = 1e-5
FOX_HEAD_DIM = 64
FOX_Q_BLOCK = 256
DW_TOKENS = 2048
HEAD_ROWS = 128
GELU_K = 0.7978845608028654
GELU_C = 0.044715
NEG_BIG = -1e30
N_SHARDS = 4

ADAM_LR = 0.001
ADAM_B1 = 0.9
ADAM_B2 = 0.999
ADAM_EPS = 1e-08
ADAM_WD = 0.01
ADAM_STEP = 10

VMEM_LIMIT_BYTES = 56 * 1024 * 1024
NT = (((1,), (1,)), ((), ()))
TN = (((0,), (0,)), ((), ()))
HBM = pl.BlockSpec(memory_space=pl.ANY)


def _tile(n, pref, mult=8):
    t = min(n, pref)
    while n % t or t % mult:
        t -= mult
    return t


def _dot(a, b):
    return jnp.dot(a, b, preferred_element_type=F32)


def _dotg(a, b, dims):
    return lax.dot_general(a, b, dims, preferred_element_type=F32)


def _sigmoid(x):
    return 1.0 / (1.0 + jnp.exp(-x))


def _gelu(x):
    return 0.5 * x * (1.0 + jnp.tanh(GELU_K * (x + GELU_C * x * x * x)))


def _gelu_grad(x):
    t = jnp.tanh(GELU_K * (x + GELU_C * x * x * x))
    return 0.5 * (1.0 + t) + 0.5 * x * (1.0 - t * t) * GELU_K * (1.0 + 3.0 * GELU_C * x * x)


def _ln_stats(z):
    mu = jnp.mean(z, axis=-1, keepdims=True)
    zc = z - mu
    var = jnp.mean(zc * zc, axis=-1, keepdims=True)
    rstd = lax.rsqrt(var + LN_EPS)
    return zc * rstd, rstd


def _ln_bwd(dy, z, g):
    xhat, rstd = _ln_stats(z)
    gdy = dy * g
    m1 = jnp.mean(gdy, axis=-1, keepdims=True)
    m2 = jnp.mean(gdy * xhat, axis=-1, keepdims=True)
    dz = rstd * (gdy - m1 - xhat * m2)
    return dz, jnp.sum(dy * xhat, axis=0, keepdims=True), jnp.sum(dy, axis=0, keepdims=True)


class _Side:
    def __init__(self, ins, out_shapes, sems, start, finish):
        self.ins, self.out_shapes, self.sems = list(ins), list(out_shapes), list(sems)
        self.start, self.finish = start, finish
        self.results = None


def _join(sides):
    sides = [s for s in sides if s is not None]
    if not sides:
        return None
    ins = [a for s in sides for a in s.ins]
    outs = [a for s in sides for a in s.out_shapes]
    sems = [a for s in sides for a in s.sems]

    def parts(seq, field):
        out, o = [], 0
        for s in sides:
            n = len(getattr(s, field))
            out.append(seq[o:o + n])
            o += n
        return out

    def run(which):
        def fn(i, o, m):
            for s, a, b, c in zip(sides, parts(i, "ins"), parts(o, "out_shapes"), parts(m, "sems")):
                getattr(s, which)(a, b, c)
        return fn

    joined = _Side(ins, outs, sems, run("start"), run("finish"))
    joined.members = sides
    return joined


def _deliver(side, results):
    members = getattr(side, "members", None)
    side.results = list(results)
    if members:
        o = 0
        for s in members:
            n = len(s.out_shapes)
            _deliver(s, results[o:o + n])
            o += n


def _pcall(body, operands, *, name, grid, in_specs, out_specs, out_shape, sem, scratch_shapes=(),
           prefetch=(), aliases=None, side=None):
    single = not isinstance(out_shape, (list, tuple))
    out_shape = [out_shape] if single else list(out_shape)
    out_specs = [out_specs] if single else list(out_specs)
    in_specs, scratch_shapes = list(in_specs), list(scratch_shapes)
    n_pre, n_in, n_out, n_sc = len(prefetch), len(in_specs), len(out_shape), len(scratch_shapes)
    fn = body
    extra = []
    if side is not None:
        s_in, s_out = len(side.ins), len(side.out_shapes)

        def fn(*refs):
            pre, rest = refs[:n_pre], refs[n_pre:]
            m_in, c_in = rest[:n_in], rest[n_in:n_in + s_in]
            rest = rest[n_in + s_in:]
            m_out, c_out = rest[:n_out], rest[n_out:n_out + s_out]
            rest = rest[n_out + s_out:]
            m_sc, c_sc = rest[:n_sc], rest[n_sc:]
            first = pl.program_id(0) == 0
            last = pl.program_id(0) == grid[0] - 1
            for a in range(1, len(grid)):
                first = jnp.logical_and(first, pl.program_id(a) == 0)
                last = jnp.logical_and(last, pl.program_id(a) == grid[a] - 1)

            @pl.when(first)
            def _():
                side.start(c_in, c_out, c_sc)

            body(*pre, *m_in, *m_out, *m_sc)

            @pl.when(last)
            def _():
                side.finish(c_in, c_out, c_sc)

        in_specs = in_specs + [HBM] * s_in
        out_specs = out_specs + [HBM] * s_out
        out_shape = out_shape + side.out_shapes
        scratch_shapes = scratch_shapes + side.sems
        extra = side.ins
        sem = ("arbitrary",) * len(grid)
    params = pltpu.CompilerParams(dimension_semantics=tuple(sem), vmem_limit_bytes=VMEM_LIMIT_BYTES)
    kw = dict(input_output_aliases=aliases) if aliases else {}
    if n_pre:
        spec = pltpu.PrefetchScalarGridSpec(num_scalar_prefetch=n_pre, grid=grid, in_specs=in_specs,
                                            out_specs=out_specs, scratch_shapes=scratch_shapes)
        call = pl.pallas_call(fn, name=name, grid_spec=spec, out_shape=out_shape, compiler_params=params, **kw)
    else:
        call = pl.pallas_call(fn, name=name, grid=grid, in_specs=in_specs, out_specs=out_specs,
                              out_shape=out_shape, scratch_shapes=scratch_shapes, compiler_params=params, **kw)
    res = call(*prefetch, *operands, *extra)
    if side is not None:
        _deliver(side, res[n_out:])
        res = res[:n_out]
    return res[0] if single else res


def _run_side(side, name):
    def body(*refs):
        n_in, n_out = len(side.ins), len(side.out_shapes)
        i, o, m = refs[:n_in], refs[n_in:n_in + n_out], refs[n_in + n_out:]
        side.start(i, o, m)
        side.finish(i, o, m)

    res = pl.pallas_call(body, name=name, in_specs=[HBM] * len(side.ins), out_specs=[HBM] * len(side.out_shapes),
                         out_shape=side.out_shapes, scratch_shapes=side.sems)(*side.ins)
    _deliver(side, res)


def _mesh_pos():
    x, y, c = lax.axis_index("x"), lax.axis_index("y"), lax.axis_index("c")
    chips = [(1 - x, y), (x, 1 - y), (1 - x, 1 - y)]
    return x, y, c, chips


def _rows(ref, lead, half, n_rows):
    return ref.at[tuple(lead) + (pl.ds(half * n_rows, n_rows),)]


def _side_gather(shards, split):
    n = len(shards)
    hs = [w.shape[0] // 2 for w in shards]

    def plan(ins, outs, sems):
        ssem, rsem = sems
        x, y, c, chips = _mesh_pos()
        q = 2 * x + y
        sib = (x, y, 1 - c)

        def rc(p, k, src, dst, to):
            return pltpu.make_async_remote_copy(src_ref=src, dst_ref=dst, send_sem=ssem.at[p, k],
                                                recv_sem=rsem.at[p, k], device_id=to, device_id_type=MESH)

        def blk(ref, p, qi, half):
            return _rows(ref, (qi,), half, hs[p]) if split[p] else ref.at[qi]

        return x, y, c, chips, q, sib, rc, blk

    def first_sends(ins, outs, sems):
        x, y, c, chips, q, sib, rc, blk = plan(ins, outs, sems)
        cps = [rc(p, 0, ins[p], outs[p].at[q], sib) for p in range(n)]
        for j, (cx, cy) in enumerate(chips):
            for p in range(n):
                src = _rows(ins[p], (), c, hs[p]) if split[p] else ins[p]
                cps.append(rc(p, 1 + j, src, blk(outs[p], p, q, c), (cx, cy, c)))
        return cps

    def start(ins, outs, sems):
        for cp in first_sends(ins, outs, sems):
            cp.start()

    def finish(ins, outs, sems):
        x, y, c, chips, q, sib, rc, blk = plan(ins, outs, sems)
        sent = first_sends(ins, outs, sems)
        for j, (cx, cy) in enumerate(chips):
            qj = 2 * cx + cy
            for p in range(n):
                got = blk(outs[p], p, qj, c)
                rc(p, 1 + j, got, got, (cx, cy, c)).wait_recv()
                if split[p]:
                    fwd = rc(p, 4 + j, got, got, sib)
                    fwd.start()
                    sent.append(fwd)
        for j, (cx, cy) in enumerate(chips):
            qj = 2 * cx + cy
            for p in range(n):
                if split[p]:
                    got = blk(outs[p], p, qj, 1 - c)
                    rc(p, 4 + j, got, got, sib).wait_recv()
        for p in range(n):
            rc(p, 0, outs[p].at[q], outs[p].at[q], sib).wait_recv()
        for cp in sent:
            cp.wait_send()

    return _Side(shards, [SDS((N_SHARDS,) + w.shape, w.dtype) for w in shards],
                 [pltpu.SemaphoreType.DMA((n, 7)), pltpu.SemaphoreType.DMA((n, 7))], start, finish)


def _side_pair_send(gs):
    n = len(gs)

    def copies(ins, outs, sems):
        x, y, c, _ = _mesh_pos()
        return [pltpu.make_async_remote_copy(
            src_ref=ins[p].at[:, pl.ds((1 - c) * (gs[p].shape[1] // 2), gs[p].shape[1] // 2)], dst_ref=outs[p],
            send_sem=sems[0].at[p], recv_sem=sems[1].at[p], device_id=(x, y, 1 - c), device_id_type=MESH)
            for p in range(n)]

    def start(ins, outs, sems):
        for cp in copies(ins, outs, sems):
            cp.start()

    def finish(ins, outs, sems):
        for cp in copies(ins, outs, sems):
            cp.wait()

    return _Side(gs, [SDS((g.shape[0], g.shape[1] // 2, g.shape[2]), g.dtype) for g in gs],
                 [pltpu.SemaphoreType.DMA((n,)), pltpu.SemaphoreType.DMA((n,))], start, finish)


def _side_scatter(ps):
    n = len(ps)

    def sends(ins, outs, sems):
        x, y, c, chips = _mesh_pos()
        q = 2 * x + y
        return [pltpu.make_async_remote_copy(src_ref=ins[p].at[2 * cx + cy], dst_ref=outs[p].at[q],
                                             send_sem=sems[0].at[p, j], recv_sem=sems[1].at[p, j],
                                             device_id=(cx, cy, c), device_id_type=MESH)
                for j, (cx, cy) in enumerate(chips) for p in range(n)]

    def start(ins, outs, sems):
        for cp in sends(ins, outs, sems):
            cp.start()

    def finish(ins, outs, sems):
        x, y, c, chips = _mesh_pos()
        for j, (cx, cy) in enumerate(chips):
            for p in range(n):
                got = outs[p].at[2 * cx + cy]
                pltpu.make_async_remote_copy(src_ref=got, dst_ref=got, send_sem=sems[0].at[p, j],
                                             recv_sem=sems[1].at[p, j], device_id=(cx, cy, c),
                                             device_id_type=MESH).wait_recv()
        for cp in sends(ins, outs, sems):
            cp.wait_send()

    return _Side(ps, [SDS(p.shape, p.dtype) for p in ps],
                 [pltpu.SemaphoreType.DMA((n, 3)), pltpu.SemaphoreType.DMA((n, 3))], start, finish)


def _side_pair_share(tots):
    n = len(tots)

    def copies(ins, outs, sems):
        x, y, c, _ = _mesh_pos()
        return [pltpu.make_async_remote_copy(src_ref=ins[p], dst_ref=outs[p], send_sem=sems[0].at[p],
                                             recv_sem=sems[1].at[p], device_id=(x, y, 1 - c), device_id_type=MESH)
                for p in range(n)]

    def start(ins, outs, sems):
        for cp in copies(ins, outs, sems):
            cp.start()

    def finish(ins, outs, sems):
        for cp in copies(ins, outs, sems):
            cp.wait()

    return _Side(tots, [SDS(t_.shape, t_.dtype) for t_ in tots],
                 [pltpu.SemaphoreType.DMA((n,)), pltpu.SemaphoreType.DMA((n,))], start, finish)


N_DEVICES = 8


def _side_bcast(v):
    def peers():
        x, y, c, _ = _mesh_pos()
        out = []
        for k in range(1, N_DEVICES):
            px, py, pc = x ^ ((k >> 2) & 1), y ^ ((k >> 1) & 1), c ^ (k & 1)
            out.append((k - 1, (px, py, pc), 4 * px + 2 * py + pc))
        return 4 * x + 2 * y + c, out

    def sends(ins, outs, sems):
        me, ps = peers()
        return [pltpu.make_async_remote_copy(src_ref=ins[0], dst_ref=outs[0].at[me], send_sem=sems[0].at[k],
                                             recv_sem=sems[1].at[k], device_id=to, device_id_type=MESH)
                for k, to, _ in ps]

    def start(ins, outs, sems):
        for cp in sends(ins, outs, sems):
            cp.start()

    def finish(ins, outs, sems):
        _, ps = peers()
        for k, to, slot in ps:
            got = outs[0].at[slot]
            pltpu.make_async_remote_copy(src_ref=got, dst_ref=got, send_sem=sems[0].at[k], recv_sem=sems[1].at[k],
                                         device_id=to, device_id_type=MESH).wait_recv()
        for cp in sends(ins, outs, sems):
            cp.wait_send()

    return _Side([v], [SDS((N_DEVICES,) + v.shape, v.dtype)],
                 [pltpu.SemaphoreType.DMA((N_DEVICES - 1,)), pltpu.SemaphoreType.DMA((N_DEVICES - 1,))], start, finish)


def _sum_slots(v, r, me, name, side=None):
    n, rows, lanes = r.shape
    tr = _tile(rows, 512)

    def body(me_ref, v_ref, r_ref, o_ref):
        j = pl.program_id(1)
        term = jnp.where(j == me_ref[0], v_ref[...], r_ref[0])

        @pl.when(j == 0)
        def _():
            o_ref[...] = term

        @pl.when(j != 0)
        def _():
            o_ref[...] += term

    other = lambda j, k: jnp.where(j == k, (k + 1) % n, j)
    return _pcall(
        body, (v, r), name=name, grid=(rows // tr, n), prefetch=(me,),
        in_specs=[pl.BlockSpec((tr, lanes), lambda i, j, me_ref: (i, 0)),
                  pl.BlockSpec((1, tr, lanes), lambda i, j, me_ref: (other(j, me_ref[0]), i, 0))],
        out_specs=pl.BlockSpec((tr, lanes), lambda i, j, me_ref: (i, 0)),
        out_shape=SDS((rows, lanes), F32), sem=("parallel", "arbitrary"), side=side)


def _pair_sum(g, r1, cq, name, side=None):
    nq, xr, yc = g.shape
    h = xr // 2
    tr = _tile(h, 512, 16)
    nt = h // tr

    def body(cq_ref, g_ref, r_ref, o_ref):
        o_ref[...] = (g_ref[...] + r_ref[...]).astype(BF)

    return _pcall(
        body, (g, r1), name=name, grid=(nq, nt), prefetch=(cq,),
        in_specs=[pl.BlockSpec((1, tr, yc), lambda j, i, cq_ref: (j, cq_ref[0] * nt + i, 0)),
                  pl.BlockSpec((1, tr, yc), lambda j, i, cq_ref: (j, i, 0))],
        out_specs=pl.BlockSpec((1, tr, yc), lambda j, i, cq_ref: (j, i, 0)),
        out_shape=SDS((nq, h, yc), BF), sem=("parallel", "parallel"), side=side)


def _chip_sum(p, r2, cq, name, side=None):
    nq, h, yc = r2.shape
    tr = _tile(h, 512, 16)

    def body(cq_ref, p_ref, r_ref, o_ref):
        j = pl.program_id(1)
        term = jnp.where(j == cq_ref[1], p_ref[0], r_ref[0]).astype(F32)

        @pl.when(j == 0)
        def _():
            o_ref[...] = term

        @pl.when(j != 0)
        def _():
            o_ref[...] += term

    other = lambda j, q: jnp.where(j == q, (q + 1) % nq, j)
    return _pcall(
        body, (p, r2), name=name, grid=(h // tr, nq), prefetch=(cq,),
        in_specs=[pl.BlockSpec((1, tr, yc), lambda i, j, cq_ref: (cq_ref[1], i, 0)),
                  pl.BlockSpec((1, tr, yc), lambda i, j, cq_ref: (other(j, cq_ref[1]), i, 0))],
        out_specs=pl.BlockSpec((tr, yc), lambda i, j, cq_ref: (i, 0)),
        out_shape=SDS((h, yc), F32), sem=("parallel", "arbitrary"), side=side)


def _ffn_up_fwd(x, wup, name, side=None, emit_xt=False):
    t, d = x.shape
    w = wup.shape[2]
    tm = _tile(t, 512)

    def body(x_ref, wg_ref, wu_ref, h_ref, a_ref, at_ref, *xt_ref):
        xb = x_ref[...].astype(BF)
        g = _dot(xb, wg_ref[0])
        u = _dot(xb, wu_ref[0])
        h_ref[0] = g.astype(BF)
        h_ref[1] = u.astype(BF)
        ab = (g * _sigmoid(g) * u).astype(BF)
        a_ref[...] = ab
        at_ref[...] = ab.T
        if emit_xt:
            @pl.when(pl.program_id(0) == 0)
            def _():
                xt_ref[0][...] = xb.T

    nt = t // tm
    out_specs = [pl.BlockSpec((2, tm, w), lambda j, i: (0, i, j)), pl.BlockSpec((tm, w), lambda j, i: (i, j)),
                 pl.BlockSpec((w, tm), lambda j, i: (j, i))]
    out_shape = [SDS((2, t, 2 * w), BF), SDS((t, 2 * w), BF), SDS((2 * w, t), BF)]
    if emit_xt:
        out_specs.append(pl.BlockSpec((d, tm), lambda j, i: (0, jnp.where(j == 0, i, nt - 1))))
        out_shape.append(SDS((d, t), BF))
    return _pcall(
        body, (x, wup, wup), name=name, grid=(2, nt),
        in_specs=[pl.BlockSpec((tm, d), lambda j, i: (i, 0)),
                  pl.BlockSpec((1, d, w), lambda j, i: (j, 0, 0)),
                  pl.BlockSpec((1, d, w), lambda j, i: (j + 2, 0, 0))],
        out_specs=out_specs, out_shape=out_shape,
        sem=("arbitrary", "arbitrary") if emit_xt else ("parallel", "parallel"), side=side)


def _res_ln_fwd(parts, w, x, gamma, beta, alpha, res_scale, name, side=None, parts_t=False):
    t, d = x.shape
    n = len(parts)
    offs = [0]
    for p in parts:
        offs.append(offs[-1] + p.shape[1])
    tm = _tile(t, 512)

    def body(*refs):
        p_refs = refs[:n]
        w_ref, x_ref, g_ref, b_ref, z_ref, y_ref, yt_ref = refs[n:n + 7]
        pt_refs = refs[n + 7:]
        f = _dot(p_refs[0][...], w_ref[offs[0]:offs[1], :])
        for k in range(1, n):
            f = f + _dot(p_refs[k][...], w_ref[offs[k]:offs[k + 1], :])
        z = alpha * x_ref[...] + res_scale * f
        z_ref[...] = z
        xhat, _ = _ln_stats(z)
        y = xhat * g_ref[...] + b_ref[...]
        y_ref[...] = y
        yt_ref[...] = y.astype(BF).T
        for k in range(len(pt_refs)):
            pt_refs[k][...] = p_refs[k][...].T

    row = lambda i: (i, 0)
    col = lambda i: (0, i)
    fixed = lambda i: (0, 0)
    out_specs = [pl.BlockSpec((tm, d), row), pl.BlockSpec((tm, d), row), pl.BlockSpec((d, tm), col)]
    out_shape = [SDS((t, d), F32), SDS((t, d), F32), SDS((d, t), BF)]
    if parts_t:
        out_specs += [pl.BlockSpec((p.shape[1], tm), col) for p in parts]
        out_shape += [SDS((p.shape[1], t), BF) for p in parts]
    return _pcall(
        body, (*parts, w, x, gamma, beta), name=name, grid=(t // tm,),
        in_specs=[pl.BlockSpec((tm, p.shape[1]), row) for p in parts]
        + [pl.BlockSpec(w.shape, fixed), pl.BlockSpec((tm, d), row),
           pl.BlockSpec((1, d), fixed), pl.BlockSpec((1, d), fixed)],
        out_specs=out_specs, out_shape=out_shape,
        sem=("parallel",), side=side)


def _mix_proj_fwd(x, w_main, widths, name, side=None):
    t, d = x.shape
    dc, dq, ds = widths
    tm = _tile(t, 512)

    def body(x_ref, w_ref, pc_ref, pq_ref, ps_ref):
        xb = x_ref[...].astype(BF)
        pc_ref[...] = _dot(xb, w_ref[:, 0:dc])
        pq_ref[...] = _dot(xb, w_ref[:, dc:dc + dq]).astype(BF)
        ps_ref[...] = _dot(xb, w_ref[:, dc + dq:dc + dq + ds])

    row = lambda i: (i, 0)
    return _pcall(
        body, (x, w_main), name=name, grid=(t // tm,),
        in_specs=[pl.BlockSpec((tm, d), row), pl.BlockSpec(w_main.shape, lambda i: (0, 0))],
        out_specs=[pl.BlockSpec((tm, dc), row), pl.BlockSpec((tm, dq), row), pl.BlockSpec((tm, ds), row)],
        out_shape=[SDS((t, dc), F32), SDS((t, dq), BF), SDS((t, ds), F32)],
        sem=("parallel",), side=side)


def _prefix_sum_lanes(v, reverse):
    n = v.shape[-1]
    lane = lax.broadcasted_iota(jnp.int32, v.shape, v.ndim - 1)
    sh = 1
    while sh < n:
        if reverse:
            v = v + jnp.where(lane < n - sh, pltpu.roll(v, n - sh, axis=v.ndim - 1), 0.0)
        else:
            v = v + jnp.where(lane >= sh, pltpu.roll(v, sh, axis=v.ndim - 1), 0.0)
        sh *= 2
    return v


def _cum_fwd(x3, wft, bf, name, side=None):
    b, s, d = x3.shape
    h = bf.shape[0]

    def body(x_ref, w_ref, b_ref, fl_ref, cum_ref):
        fl = _dotg(w_ref[...], x_ref[0].astype(BF), NT)[0:h] + b_ref[...]
        fl_ref[0] = fl
        lf = jnp.minimum(fl, 0.0) - jnp.log(1.0 + jnp.exp(-jnp.abs(fl)))
        cum_ref[0] = _prefix_sum_lanes(lf, reverse=False)

    return _pcall(
        body, (x3, wft, bf), name=name, grid=(b,),
        in_specs=[pl.BlockSpec((1, s, d), lambda i: (i, 0, 0)),
                  pl.BlockSpec(wft.shape, lambda i: (0, 0)), pl.BlockSpec((h, 1), lambda i: (0, 0))],
        out_specs=[pl.BlockSpec((1, h, s), lambda i: (i, 0, 0)), pl.BlockSpec((1, h, s), lambda i: (i, 0, 0))],
        out_shape=[SDS((b, h, s), F32), SDS((b, h, s), F32)],
        sem=("parallel",), side=side)


def _shift_rows(z, k, down):
    n = z.shape[0]
    row = lax.broadcasted_iota(jnp.int32, z.shape, 0)
    if down:
        return jnp.where(row >= k, pltpu.roll(z, k, axis=0), 0.0)
    return jnp.where(row < n - k, pltpu.roll(z, n - k, axis=0), 0.0)


def _conv_fwd(pc3, cw, name, side=None):
    b, s, c3 = pc3.shape
    c = c3 // 3

    def body(p_ref, w_ref, y_ref):
        z = p_ref[0, :, c:2 * c] * p_ref[0, :, 2 * c:3 * c]
        conv = w_ref[0:1, :] * _shift_rows(z, 2, True) + w_ref[1:2, :] * _shift_rows(z, 1, True) + w_ref[2:3, :] * z
        y_ref[0] = (p_ref[0, :, 0:c] * conv).astype(BF)

    return _pcall(
        body, (pc3, cw), name=name, grid=(b,),
        in_specs=[pl.BlockSpec((1, s, c3), lambda i: (i, 0, 0)), pl.BlockSpec((3, c), lambda i: (0, 0))],
        out_specs=pl.BlockSpec((1, s, c), lambda i: (i, 0, 0)),
        out_shape=SDS((b, s, c), BF), sem=("parallel",), side=side)


def _head_masks(width):
    lane = lax.broadcasted_iota(jnp.int32, (1, width), 1)
    return [lane < FOX_HEAD_DIM, lane >= FOX_HEAD_DIM]


def _fox_scores(q, k, cum_row, lo, head_mask):
    tq = q.shape[0]
    qm = jnp.where(head_mask, q * (FOX_HEAD_DIM ** -0.5), 0)
    s = _dotg(qm, k, NT) - cum_row
    tri = lax.broadcasted_iota(jnp.int32, (tq, tq), 1) <= lax.broadcasted_iota(jnp.int32, (tq, tq), 0)
    diag = jnp.where(tri, s[:, lo:], NEG_BIG)
    return (diag if lo == 0 else jnp.concatenate([s[:, :lo], diag], axis=1)), qm


def _fox_fwd(pq3, cum4, name, side=None):
    b, s, d3 = pq3.shape
    df = d3 // 3
    hp = df // 128
    tq = _tile(s, FOX_Q_BLOCK)

    def body(q_ref, k_ref, v_ref, c_ref, o_ref, lse_ref):
        masks = _head_masks(128)
        for i in range(s // tq):
            lo, hi = i * tq, (i + 1) * tq
            q = q_ref[0, lo:hi, :]
            k = k_ref[0, 0:hi, :]
            v = v_ref[0, 0:hi, :]
            o = jnp.zeros((tq, 128), F32)
            lse = jnp.zeros((tq, 128), F32)
            for e in range(2):
                sc, _ = _fox_scores(q, k, c_ref[0, 0, e:e + 1, 0:hi], lo, masks[e])
                m = jnp.max(sc, axis=-1, keepdims=True)
                p = jnp.exp(sc - m)
                l = jnp.sum(p, axis=-1, keepdims=True)
                o = jnp.where(masks[e], _dot(p.astype(BF), v) * (1.0 / l), o)
                lse = jnp.where(masks[e], m + jnp.log(l), lse)
            o_ref[0, lo:hi, :] = o.astype(BF)
            lse_ref[0, 0, lo:hi, :] = lse

    blk = lambda off: pl.BlockSpec((1, s, 128), lambda i, j: (i, 0, off + j))
    return _pcall(
        body, (pq3, pq3, pq3, cum4), name=name, grid=(b, hp),
        in_specs=[blk(0), blk(hp), blk(2 * hp), pl.BlockSpec((1, 1, 2, s), lambda i, j: (i, j, 0, 0))],
        out_specs=[blk(0), pl.BlockSpec((1, 1, s, 128), lambda i, j: (i, j, 0, 0))],
        out_shape=[SDS((b, s, df), BF), SDS((b, hp, s, 128), F32)],
        sem=("parallel", "parallel"), side=side)


def _sgu_mix(wm, vnb, bias, gmasks):
    out = bias
    for g in range(len(wm)):
        out = out + jnp.where(gmasks[g], _dot(wm[g], vnb), 0.0)
    return out


def _sgu_consts(ws_ref, bs_ref, ds):
    ng, c, _ = ws_ref.shape
    gd = ds // ng
    tri = lax.broadcasted_iota(jnp.int32, (c, c), 0) >= lax.broadcasted_iota(jnp.int32, (c, c), 1)
    lane = lax.broadcasted_iota(jnp.int32, (1, ds), 1)
    gmasks = [(lane >= g * gd) & (lane < (g + 1) * gd) for g in range(ng)]
    wm = [jnp.where(tri, ws_ref[g], 0.0).astype(BF) for g in range(ng)]
    bias = jnp.zeros((c, ds), F32)
    for g in range(ng):
        bias = jnp.where(gmasks[g], bs_ref[g], bias)
    return tri, gmasks, wm, bias


def _sgu_fwd(ps, lng, lnb, ws, bs, name, side=None):
    t, ds2 = ps.shape
    ds = ds2 // 2
    c = ws.shape[1]
    tm = _tile(t, 512, c)

    def body(p_ref, g_ref, b_ref, ws_ref, bs_ref, y_ref):
        _, gmasks, wm, bias = _sgu_consts(ws_ref, bs_ref, ds)
        up = _gelu(p_ref[:, 0:ds])
        xhat, _ = _ln_stats(_gelu(p_ref[:, ds:ds2]))
        vnb = (xhat * g_ref[...] + b_ref[...]).astype(BF)
        for n in range(tm // c):
            r0, r1 = n * c, (n + 1) * c
            y_ref[r0:r1, :] = (up[r0:r1] * _sgu_mix(wm, vnb[r0:r1], bias, gmasks)).astype(BF)

    fixed2 = lambda i: (0, 0)
    fixed3 = lambda i: (0, 0, 0)
    return _pcall(
        body, (ps, lng, lnb, ws, bs), name=name, grid=(t // tm,),
        in_specs=[pl.BlockSpec((tm, ds2), lambda i: (i, 0)), pl.BlockSpec((1, ds), fixed2),
                  pl.BlockSpec((1, ds), fixed2), pl.BlockSpec(ws.shape, fixed3), pl.BlockSpec(bs.shape, fixed3)],
        out_specs=pl.BlockSpec((tm, ds), lambda i: (i, 0)),
        out_shape=SDS((t, ds), BF), sem=("parallel",), side=side)


def _loss_fwd(y, target, name, side=None):
    t, d = y.shape
    tm = _tile(t, 512)

    def body(y_ref, t_ref, dy_ref, l_ref):
        @pl.when(pl.program_id(0) == 0)
        def _():
            l_ref[...] = jnp.zeros_like(l_ref)

        err = y_ref[...] - t_ref[...]
        dy_ref[...] = err * (1.0 / d)
        l_ref[...] += 0.5 * jnp.sum(jnp.sum(err * err, axis=-1, keepdims=True) * (1.0 / d), axis=0, keepdims=True)

    row = lambda i: (i, 0)
    return _pcall(
        body, (y, target), name=name, grid=(t // tm,),
        in_specs=[pl.BlockSpec((tm, d), row), pl.BlockSpec((tm, d), row)],
        out_specs=[pl.BlockSpec((tm, d), row), pl.BlockSpec((8, 128), lambda i: (0, 0))],
        out_shape=[SDS((t, d), F32), SDS((8, 128), F32)],
        sem=("arbitrary",), side=side)


def _acc_rows(ref, val, first):
    @pl.when(first)
    def _():
        ref[...] = val

    @pl.when(jnp.logical_not(first))
    def _():
        ref[...] += val


def _ffn_bwd_mid(dy, z, gamma, wd, h, name, side=None):
    t, d = dy.shape
    dff = wd.shape[0]
    half = dff // 2
    tm = _tile(t, 256)

    def body(dy_ref, z_ref, g_ref, wd_ref, h_ref, dz_ref, df_ref, dh_ref, dg_ref, db_ref):
        dz, dgam, dbet = _ln_bwd(dy_ref[...], z_ref[...], g_ref[...])
        first = pl.program_id(0) == 0
        _acc_rows(dg_ref, dgam, first)
        _acc_rows(db_ref, dbet, first)
        dz_ref[...] = dz
        dfb = (0.5 * dz).astype(BF)
        df_ref[...] = dfb
        for j in range(2):
            c0, c1 = j * half, (j + 1) * half
            da = _dotg(dfb, wd_ref[c0:c1, :], NT).astype(BF)
            g = h_ref[0, :, c0:c1]
            u = h_ref[1, :, c0:c1]
            sg = _sigmoid(g)
            dh_ref[0, :, c0:c1] = da * u * sg * (1.0 + g * (1.0 - sg))
            dh_ref[1, :, c0:c1] = da * g * sg

    row = lambda i: (i, 0)
    fixed = lambda i: (0, 0)
    return _pcall(
        body, (dy, z, gamma, wd, h), name=name, grid=(t // tm,),
        in_specs=[pl.BlockSpec((tm, d), row), pl.BlockSpec((tm, d), row), pl.BlockSpec((1, d), fixed),
                  pl.BlockSpec(wd.shape, fixed), pl.BlockSpec((2, tm, dff), lambda i: (0, i, 0))],
        out_specs=[pl.BlockSpec((tm, d), row), pl.BlockSpec((tm, d), row),
                   pl.BlockSpec((2, tm, dff), lambda i: (0, i, 0)),
                   pl.BlockSpec((1, d), fixed), pl.BlockSpec((1, d), fixed)],
        out_shape=[SDS((t, d), F32), SDS((t, d), BF), SDS((2, t, dff), BF), SDS((1, d), F32), SDS((1, d), F32)],
        sem=("arbitrary",), side=side)


def _ffn_bwd_dx(dh, wup, dz, alpha, name, side=None):
    _, t, dff = dh.shape
    nq, d, w = wup.shape
    per = dff // w
    tm = _tile(t, 256)

    def body(dh_ref, w_ref, dz_ref, dx_ref):
        acc = alpha * dz_ref[...]
        for q in range(nq):
            c0 = (q % per) * w
            acc = acc + _dotg(dh_ref[q // per, :, c0:c0 + w], w_ref[q], NT)
        dx_ref[...] = acc

    row = lambda i: (i, 0)
    return _pcall(
        body, (dh, wup, dz), name=name, grid=(t // tm,),
        in_specs=[pl.BlockSpec((2, tm, dff), lambda i: (0, i, 0)), pl.BlockSpec(wup.shape, lambda i: (0, 0, 0)),
                  pl.BlockSpec((tm, d), row)],
        out_specs=pl.BlockSpec((tm, d), row), out_shape=SDS((t, d), F32),
        sem=("parallel",), side=side)


def _dw(at, b3, ka, nb, name, side=None):
    ka_tot, t = at.shape
    gb, _, nb_tot = b3.shape
    na, ncb = ka_tot // ka, nb_tot // nb
    tm = _tile(t, DW_TOKENS, 128)

    def body(a_ref, b_ref, o_ref):
        part = _dot(a_ref[...], b_ref[0])

        @pl.when(pl.program_id(2) == 0)
        def _():
            o_ref[0, 0] = part

        @pl.when(pl.program_id(2) != 0)
        def _():
            o_ref[0, 0] += part

    return _pcall(
        body, (at, b3), name=name, grid=(na, gb * ncb, t // tm),
        in_specs=[pl.BlockSpec((ka, tm), lambda ja, jb, i: (ja, i)),
                  pl.BlockSpec((1, tm, nb), lambda ja, jb, i: (jb // ncb, i, jb % ncb))],
        out_specs=pl.BlockSpec((1, 1, ka, nb), lambda ja, jb, i: (ja, jb, 0, 0)),
        out_shape=SDS((na, gb * ncb, ka, nb), F32),
        sem=("parallel", "parallel", "arbitrary"), side=side)


def _out_bwd(dy, z, gamma, wout, widths, name, side=None):
    t, d = dy.shape
    wa, wb, wc = widths
    tm = _tile(t, 512)

    def body(dy_ref, z_ref, g_ref, w_ref, dz_ref, dzb_ref, da_ref, dbb_ref, dc_ref, dg_ref, db_ref):
        dz, dgam, dbet = _ln_bwd(dy_ref[...], z_ref[...], g_ref[...])
        first = pl.program_id(0) == 0
        _acc_rows(dg_ref, dgam, first)
        _acc_rows(db_ref, dbet, first)
        dz_ref[...] = dz
        dzb = dz.astype(BF)
        dzb_ref[...] = dzb
        da_ref[...] = _dotg(dzb, w_ref[0:wa, :], NT).astype(BF)
        dbb_ref[...] = _dotg(dzb, w_ref[wa:wa + wb, :], NT).astype(BF)
        dc_ref[...] = _dotg(dzb, w_ref[wa + wb:wa + wb + wc, :], NT).astype(BF)

    row = lambda i: (i, 0)
    fixed = lambda i: (0, 0)
    return _pcall(
        body, (dy, z, gamma, wout), name=name, grid=(t // tm,),
        in_specs=[pl.BlockSpec((tm, d), row), pl.BlockSpec((tm, d), row), pl.BlockSpec((1, d), fixed),
                  pl.BlockSpec(wout.shape, fixed)],
        out_specs=[pl.BlockSpec((tm, d), row), pl.BlockSpec((tm, d), row), pl.BlockSpec((tm, wa), row),
                   pl.BlockSpec((tm, wb), row), pl.BlockSpec((tm, wc), row),
                   pl.BlockSpec((1, d), fixed), pl.BlockSpec((1, d), fixed)],
        out_shape=[SDS((t, d), F32), SDS((t, d), BF), SDS((t, wa), BF), SDS((t, wb), BF), SDS((t, wc), BF),
                   SDS((1, d), F32), SDS((1, d), F32)],
        sem=("arbitrary",), side=side)


def _conv_bwd(pc3, dya3, cw, name, side=None):
    b, s, c3 = pc3.shape
    c = c3 // 3

    def body(p_ref, dy_ref, w_ref, dp_ref, dw_ref):
        cb = p_ref[0, :, 0:c]
        cc = p_ref[0, :, c:2 * c]
        ch = p_ref[0, :, 2 * c:3 * c]
        z = cc * ch
        z1 = _shift_rows(z, 1, True)
        z2 = _shift_rows(z, 2, True)
        w0, w1, w2 = w_ref[0:1, :], w_ref[1:2, :], w_ref[2:3, :]
        dy = dy_ref[0].astype(F32)
        dconv = dy * cb
        dz = w2 * dconv + w1 * _shift_rows(dconv, 1, False) + w0 * _shift_rows(dconv, 2, False)
        dp_ref[0, :, 0:c] = (dy * (w0 * z2 + w1 * z1 + w2 * z)).astype(BF)
        dp_ref[0, :, c:2 * c] = (dz * ch).astype(BF)
        dp_ref[0, :, 2 * c:3 * c] = (dz * cc).astype(BF)
        first = pl.program_id(0) == 0
        for r, zs in enumerate((z2, z1, z)):
            _acc_rows(dw_ref.at[r:r + 1], jnp.sum(dconv * zs, axis=0, keepdims=True), first)

    blk = lambda i: (i, 0, 0)
    return _pcall(
        body, (pc3, dya3, cw), name=name, grid=(b,),
        in_specs=[pl.BlockSpec((1, s, c3), blk), pl.BlockSpec((1, s, c), blk), pl.BlockSpec((3, c), lambda i: (0, 0))],
        out_specs=[pl.BlockSpec((1, s, c3), blk), pl.BlockSpec((3, c), lambda i: (0, 0))],
        out_shape=[SDS((b, s, c3), BF), SDS((3, c), F32)],
        sem=("arbitrary",), side=side)


def _fox_bwd(pq3, cum4, lse4, dyb3, name, side=None):
    b, s, d3 = pq3.shape
    df = d3 // 3
    hp = df // 128
    tq = _tile(s, FOX_Q_BLOCK)
    scale = FOX_HEAD_DIM ** -0.5

    def body(q_ref, k_ref, v_ref, c_ref, lse_ref, do_ref, dq_ref, dk_ref, dv_ref, dc_ref, dk_acc, dv_acc):
        masks = _head_masks(128)
        dk_acc[...] = jnp.zeros_like(dk_acc)
        dv_acc[...] = jnp.zeros_like(dv_acc)
        dc_ref[...] = jnp.zeros_like(dc_ref)
        for i in range(s // tq):
            lo, hi = i * tq, (i + 1) * tq
            q = q_ref[0, lo:hi, :]
            do = do_ref[0, lo:hi, :]
            k = k_ref[0, 0:hi, :]
            v = v_ref[0, 0:hi, :]
            lse = lse_ref[0, 0, lo:hi, :]
            dq = jnp.zeros((tq, 128), F32)
            for e in range(2):
                dom = jnp.where(masks[e], do, 0)
                sc, qm = _fox_scores(q, k, c_ref[0, 0, e:e + 1, 0:hi], lo, masks[e])
                p = jnp.exp(sc - lse[:, FOX_HEAD_DIM * e:FOX_HEAD_DIM * e + 1])
                dp = _dotg(dom, v, NT)
                ds = p * (dp - jnp.sum(p * dp, axis=-1, keepdims=True))
                dsb = ds.astype(BF)
                dq = jnp.where(masks[e], _dot(dsb, k) * scale, dq)
                dk_acc[0:hi, :] += _dotg(dsb, qm, TN)
                dv_acc[0:hi, :] += _dotg(p.astype(BF), dom, TN)
                dc_ref[0, 0, e:e + 1, 0:hi] -= jnp.sum(ds, axis=0, keepdims=True)
            dq_ref[0, lo:hi, :] = dq.astype(BF)
        dk_ref[0] = dk_acc[...].astype(BF)
        dv_ref[0] = dv_acc[...].astype(BF)

    blk = lambda off: pl.BlockSpec((1, s, 128), lambda i, j: (i, 0, off + j))
    cblk = pl.BlockSpec((1, 1, 2, s), lambda i, j: (i, j, 0, 0))
    return _pcall(
        body, (pq3, pq3, pq3, cum4, lse4, dyb3), name=name, grid=(b, hp),
        in_specs=[blk(0), blk(hp), blk(2 * hp), cblk,
                  pl.BlockSpec((1, 1, s, 128), lambda i, j: (i, j, 0, 0)), blk(0)],
        out_specs=[blk(0), blk(0), blk(0), cblk],
        out_shape=[SDS((b, s, df), BF), SDS((b, s, df), BF), SDS((b, s, df), BF), SDS(cum4.shape, F32)],
        scratch_shapes=[pltpu.VMEM((s, 128), F32), pltpu.VMEM((s, 128), F32)],
        sem=("parallel", "parallel"), side=side)


def _cum_bwd(dcum, flog, x3, name, side=None):
    b, h, s = dcum.shape
    d = x3.shape[2]

    def body(dc_ref, fl_ref, x_ref, dfl_ref, dbf_ref, dwf_ref):
        dfl = _prefix_sum_lanes(dc_ref[0], reverse=True) * _sigmoid(-fl_ref[0])
        dfl_ref[0] = dfl
        first = pl.program_id(0) == 0
        _acc_rows(dbf_ref, jnp.broadcast_to(jnp.sum(dfl, axis=-1, keepdims=True), (h, 128)), first)
        dflp = jnp.concatenate([dfl, jnp.zeros((HEAD_ROWS - h, s), F32)], axis=0).astype(BF)
        _acc_rows(dwf_ref, _dot(dflp, x_ref[0].astype(BF))[0:h], first)

    blk = lambda i: (i, 0, 0)
    return _pcall(
        body, (dcum, flog, x3), name=name, grid=(b,),
        in_specs=[pl.BlockSpec((1, h, s), blk), pl.BlockSpec((1, h, s), blk), pl.BlockSpec((1, s, d), blk)],
        out_specs=[pl.BlockSpec((1, h, s), blk), pl.BlockSpec((h, 128), lambda i: (0, 0)),
                   pl.BlockSpec((h, d), lambda i: (0, 0))],
        out_shape=[SDS((b, h, s), F32), SDS((h, 128), F32), SDS((h, d), F32)],
        sem=("arbitrary",), side=side)


def _sgu_bwd(ps, dyc, lng, lnb, ws, bs, name, side=None):
    t, ds2 = ps.shape
    ds = ds2 // 2
    ng, c, _ = ws.shape
    tm = _tile(t, 512, c)

    def body(p_ref, dy_ref, g_ref, b_ref, ws_ref, bs_ref, dp_ref, dws_ref, dbs_ref, dg_ref, db_ref, dvn_acc):
        tri, gmasks, wm, bias = _sgu_consts(ws_ref, bs_ref, ds)
        su = p_ref[:, 0:ds]
        sv = p_ref[:, ds:ds2]
        up = _gelu(su)
        gv = _gelu(sv)
        xhat, rstd = _ln_stats(gv)
        vnb = (xhat * g_ref[...] + b_ref[...]).astype(BF)
        dy = dy_ref[...].astype(F32)
        dws = [jnp.zeros((c, c), F32) for _ in range(ng)]
        dbs = [jnp.zeros((c, 1), F32) for _ in range(ng)]
        for n in range(tm // c):
            r0, r1 = n * c, (n + 1) * c
            mixed = _sgu_mix(wm, vnb[r0:r1], bias, gmasks)
            dp_ref[r0:r1, 0:ds] = (dy[r0:r1] * mixed * _gelu_grad(su[r0:r1])).astype(BF)
            dmix = dy[r0:r1] * up[r0:r1]
            dvn = jnp.zeros((c, ds), F32)
            for g in range(ng):
                dmg = jnp.where(gmasks[g], dmix, 0.0)
                dmb = dmg.astype(BF)
                dws[g] = dws[g] + _dotg(dmb, vnb[r0:r1], NT)
                dbs[g] = dbs[g] + jnp.sum(dmg, axis=-1, keepdims=True)
                dvn = dvn + _dotg(wm[g], dmb, TN)
            dvn_acc[r0:r1, :] = dvn
        dvn_all = dvn_acc[...]
        gdv = dvn_all * g_ref[...]
        m1 = jnp.mean(gdv, axis=-1, keepdims=True)
        m2 = jnp.mean(gdv * xhat, axis=-1, keepdims=True)
        dgv = rstd * (gdv - m1 - xhat * m2)
        dp_ref[:, ds:ds2] = (dgv * _gelu_grad(sv)).astype(BF)
        first = pl.program_id(0) == 0
        _acc_rows(dg_ref, jnp.sum(dvn_all * xhat, axis=0, keepdims=True), first)
        _acc_rows(db_ref, jnp.sum(dvn_all, axis=0, keepdims=True), first)
        for g in range(ng):
            _acc_rows(dws_ref.at[g], jnp.where(tri, dws[g], 0.0), first)
            _acc_rows(dbs_ref.at[g], dbs[g], first)

    row = lambda i: (i, 0)
    fixed2 = lambda i: (0, 0)
    fixed3 = lambda i: (0, 0, 0)
    return _pcall(
        body, (ps, dyc, lng, lnb, ws, bs), name=name, grid=(t // tm,),
        in_specs=[pl.BlockSpec((tm, ds2), row), pl.BlockSpec((tm, ds), row), pl.BlockSpec((1, ds), fixed2),
                  pl.BlockSpec((1, ds), fixed2), pl.BlockSpec(ws.shape, fixed3), pl.BlockSpec(bs.shape, fixed3)],
        out_specs=[pl.BlockSpec((tm, ds2), row), pl.BlockSpec(ws.shape, fixed3), pl.BlockSpec(bs.shape, fixed3),
                   pl.BlockSpec((1, ds), fixed2), pl.BlockSpec((1, ds), fixed2)],
        out_shape=[SDS((t, ds2), BF), SDS(ws.shape, F32), SDS(bs.shape, F32), SDS((1, ds), F32), SDS((1, ds), F32)],
        scratch_shapes=[pltpu.VMEM((tm, ds), F32)],
        sem=("arbitrary",), side=side)


def _mix_bwd_dx(dz, dconv, dq, dk, dv, dsgu, dflog, w_main, wft, seq, alpha, name, side=None):
    t, d = dz.shape
    groups = [dconv, dq, dk, dv, dsgu]
    offs = [0]
    for g in groups:
        offs.append(offs[-1] + g.shape[1])
    h = dflog.shape[1]
    tm = _tile(seq, 512)
    per_seq = seq // tm

    def body(dz_ref, a0, a1, a2, a3, a4, dfl_ref, w_ref, wf_ref, dx_ref):
        dflp = jnp.concatenate([dfl_ref[0], jnp.zeros((HEAD_ROWS - h, tm), F32)], axis=0).astype(BF)
        acc = alpha * dz_ref[...] + _dotg(dflp, wf_ref[...], TN)
        for k, a_ref in enumerate((a0, a1, a2, a3, a4)):
            acc = acc + _dotg(a_ref[...], w_ref[:, offs[k]:offs[k + 1]], NT)
        dx_ref[...] = acc

    row = lambda i: (i, 0)
    return _pcall(
        body, (dz, *groups, dflog, w_main, wft), name=name, grid=(t // tm,),
        in_specs=[pl.BlockSpec((tm, d), row)] + [pl.BlockSpec((tm, g.shape[1]), row) for g in groups]
        + [pl.BlockSpec((1, h, tm), lambda i: (i // per_seq, 0, i % per_seq)),
           pl.BlockSpec(w_main.shape, lambda i: (0, 0)), pl.BlockSpec(wft.shape, lambda i: (0, 0))],
        out_specs=pl.BlockSpec((tm, d), row), out_shape=SDS((t, d), F32),
        sem=("parallel",), side=side)


def _adam_math(w, g, m, v):
    c1 = 1.0 / (1.0 - ADAM_B1 ** ADAM_STEP)
    c2 = 1.0 / (1.0 - ADAM_B2 ** ADAM_STEP)
    nm = ADAM_B1 * m + (1.0 - ADAM_B1) * g
    nv = ADAM_B2 * v + (1.0 - ADAM_B2) * (g * g)
    delta = -ADAM_LR * ((nm * c1) / (jnp.sqrt(nv * c2) + ADAM_EPS) + ADAM_WD * w)
    return delta, nm, nv


def _adamw_small(w, g, m, v, name):
    r, c = w.shape
    tr = _tile(r, 512)

    def body(w_ref, g_ref, m_ref, v_ref, d_ref, nm_ref, nv_ref):
        d_ref[...], nm_ref[...], nv_ref[...] = _adam_math(w_ref[...], g_ref[...], m_ref[...], v_ref[...])

    blk = pl.BlockSpec((tr, c), lambda i: (i, 0))
    return _pcall(body, (w, g, m, v), name=name, grid=(r // tr,), in_specs=[blk] * 4, out_specs=[blk] * 3,
                  out_shape=[SDS((r, c), F32)] * 3, sem=("parallel",))


def _adamw_shard(w, m, v, tot, recv, cq, layer, prev, name, side=None):
    nl, xr, yc = w.shape
    h = xr // 2
    tr = _tile(h, 128)
    nt = h // tr

    def body(cq_ref, w_ref, m_ref, v_ref, t_ref, r_ref, *rest):
        g_ref, d_ref, nm_ref, nv_ref = rest[-4:]
        g = jnp.where(pl.program_id(0) == cq_ref[0], t_ref[...], r_ref[...])
        g_ref[0] = g
        d_ref[0], nm_ref[0], nv_ref[0] = _adam_math(w_ref[0], g, m_ref[0], v_ref[0])

    slab = pl.BlockSpec((1, tr, yc), lambda hf, i, cq_ref: (layer, hf * nt + i, 0))
    mine = pl.BlockSpec((tr, yc), lambda hf, i, cq_ref: (jnp.where(hf == cq_ref[0], i, 0), 0))
    theirs = pl.BlockSpec((tr, yc), lambda hf, i, cq_ref: (jnp.where(hf == cq_ref[0], 0, i), 0))
    operands = [w, m, v, tot, recv]
    in_specs = [slab, slab, slab, mine, theirs]
    aliases = None
    if prev is not None:
        operands += list(prev)
        in_specs += [HBM] * 4
        aliases = {6 + k: k for k in range(4)}
    return _pcall(body, operands, name=name, grid=(2, nt), prefetch=(cq,), in_specs=in_specs,
                  out_specs=[slab] * 4, out_shape=[SDS(w.shape, F32)] * 4, aliases=aliases,
                  sem=("parallel", "parallel"), side=side)


BIG = ("ffn1_w_up", "ffn1_w_down", "mix_w_in", "mix_w_out", "ffn2_w_up", "ffn2_w_down")
SMALL = ("ln1_g", "ln1_b", "fox_b_f", "sgu_ln_g", "sgu_ln_b", "sgu_w_s", "sgu_b_s", "ln2_g", "ln2_b", "ln3_g", "ln3_b")
ORDER = ("ln1_g", "ln1_b", "ffn1_w_up", "ffn1_w_down", "mix_w_in", "fox_b_f", "conv_w", "sgu_ln_g", "sgu_ln_b",
         "sgu_w_s", "sgu_b_s", "mix_w_out", "ln2_g", "ln2_b", "ffn2_w_up", "ffn2_w_down", "ln3_g", "ln3_b")


def _row(v):
    return v.reshape(1, -1)


class _Pipe:
    def __init__(self, stages):
        self.stages = list(stages)
        self.pos = 0
        self.last = None

    def kind(self):
        return self.stages[self.pos][0] if self.pos < len(self.stages) else None


class _Sched:
    def __init__(self):
        self.pipes = []
        self.n_alone = 0

    def add(self, stages):
        self.pipes.append(_Pipe(stages))

    def _take_comms(self, skip=None):
        jobs = []
        for p in self.pipes:
            if p is not skip and p.kind() == "comm":
                jobs.append((p, p.stages[p.pos][1]()))
        return jobs

    @staticmethod
    def _landed(jobs):
        for p, side in jobs:
            p.last = side.results
            p.pos += 1

    def carry(self, builder, *args, **kw):
        jobs = self._take_comms()
        res = builder(*args, side=_join([s for _, s in jobs]), **kw)
        self._landed(jobs)
        self._computes(ride=False)
        return res

    def _computes(self, ride):
        again = True
        while again:
            again = False
            for p in self.pipes:
                if p.kind() == "compute":
                    jobs = self._take_comms(skip=p) if ride else []
                    p.stages[p.pos][1](p.last, _join([s for _, s in jobs]))
                    p.pos += 1
                    self._landed(jobs)
                    again = True

    def drain(self):
        while any(p.kind() is not None for p in self.pipes):
            self._computes(ride=True)
            jobs = self._take_comms()
            if jobs:
                _run_side(_join([s for _, s in jobs]), "exchange_tail_%d" % self.n_alone)
                self.n_alone += 1
                self._landed(jobs)


def _forward_layer(x, xt, p, dims, alpha, l, ride):
    b, s = dims["b"], dims["s"]
    t, d = x.shape
    tag = "l%d_" % l

    def run(stage, builder, *args, **kw):
        side, on_done = ride.get(stage, (None, None))
        res = builder(*args, tag + stage, side=side, **kw)
        if on_done is not None:
            on_done()
        return res

    if xt is None:
        h1, a1, a1t, xt = run("ffn1_up", _ffn_up_fwd, x, p["wup1"], emit_xt=True)
    else:
        h1, a1, a1t = run("ffn1_up", _ffn_up_fwd, x, p["wup1"])
    z1, x1, x1t = run("ffn1_down", _res_ln_fwd, [a1], p["wd1"], x, p["ln1_g"], p["ln1_b"], alpha, 0.5)
    pc, pq, ps = run("mix_proj", _mix_proj_fwd, x1, p["win"], dims["proj_widths"])
    x1_3 = x1.reshape(b, s, d)
    flog, cum = run("fox_gate", _cum_fwd, x1_3, p["wft"], p["bf"])
    nh = flog.shape[1]
    cum4 = cum.reshape(b, nh // 2, 2, s)
    pc3 = pc.reshape(b, s, -1)
    pq3 = pq.reshape(b, s, -1)
    ya = run("conv", _conv_fwd, pc3, p["cw"]).reshape(t, -1)
    yb3, lse4 = run("fox", _fox_fwd, pq3, cum4)
    yb = yb3.reshape(t, -1)
    yc = run("sgu", _sgu_fwd, ps, p["sgu_g"], p["sgu_b"], p["ws"], p["bs"])
    z2, x2, x2t, yat, ybt, yct = run("mix_out", _res_ln_fwd, [ya, yb, yc], p["wout"], x1, p["ln2_g"], p["ln2_b"],
                                      alpha, 1.0, parts_t=True)
    h2, a2, a2t = run("ffn2_up", _ffn_up_fwd, x2, p["wup2"])
    z3, x3, x3t = run("ffn2_down", _res_ln_fwd, [a2], p["wd2"], x2, p["ln3_g"], p["ln3_b"], alpha, 0.5)
    saved = dict(xt=xt, h1=h1, a1t=a1t, z1=z1, x1=x1, x1t=x1t, pc3=pc3, pq3=pq3, ps=ps, flog=flog, cum4=cum4,
                 lse4=lse4, yat=yat, ybt=ybt, yct=yct, z2=z2, x2t=x2t, h2=h2, a2t=a2t, z3=z3)
    return x3, x3t, saved


def _ffn_backward(sched, emit, which, dy, z, gamma, wd, wup, h, a_t, x_in_t, alpha, tag, after_mid=None):
    dz, df, dh, dgam, dbet = sched.carry(_ffn_bwd_mid, dy, z, gamma, wd, h, tag + "_bwd_mid")
    if after_mid is not None:
        after_mid(dgam, dbet)
    nq, d, w = wup.shape
    emit(which + "_w_up", sched.carry(_dw, x_in_t, dh, d, w, tag + "_dw_up")[0])
    half = wd.shape[0] // 2
    emit(which + "_w_down", sched.carry(_dw, a_t, df[None], half, d, tag + "_dw_down").reshape(nq, -1, d))
    dx = sched.carry(_ffn_bwd_dx, dh, wup, dz, alpha, tag + "_bwd_dx")
    return dx, dgam, dbet


def _backward_layer(sched, emit, emit_small, dy, sv, p, dims, alpha, l):
    b, s = dims["b"], dims["s"]
    tag = "l%d_" % l
    t, d = dy.shape
    g = {}
    dx2, g["ln3_g"], g["ln3_b"] = _ffn_backward(sched, emit, "ffn2", dy, sv["z3"], p["ln3_g"], p["wd2"], p["wup2"],
                                                sv["h2"], sv["a2t"], sv["x2t"], alpha, tag + "ffn2")
    wa, wb, wc = sv["yat"].shape[0], sv["ybt"].shape[0], sv["yct"].shape[0]
    dz2, dz2b, dya, dyb, dyc, g["ln2_g"], g["ln2_b"] = sched.carry(
        _out_bwd, dx2, sv["z2"], p["ln2_g"], p["wout"], (wa, wb, wc), tag + "mix_out_bwd")
    dz2b3 = dz2b[None]
    emit("mix_w_out", jnp.concatenate(
        [sched.carry(_dw, sv[k], dz2b3, sv[k].shape[0], d, tag + "dw_out_" + k)[0, 0] for k in ("yat", "ybt", "yct")],
        axis=0).reshape(N_SHARDS, -1, d))
    dpc3, g["conv_w"] = sched.carry(_conv_bwd, sv["pc3"], dya.reshape(b, s, -1), p["cw"], tag + "conv_bwd")
    dq3, dk3, dv3, dcum4 = sched.carry(_fox_bwd, sv["pq3"], sv["cum4"], sv["lse4"], dyb.reshape(b, s, -1),
                                       tag + "fox_bwd")
    nh = sv["flog"].shape[1]
    dflog, dbf, dwft = sched.carry(_cum_bwd, dcum4.reshape(b, nh, s), sv["flog"], sv["x1"].reshape(b, s, d),
                                   tag + "fox_gate_bwd")
    g["fox_b_f"] = dbf[:, 0]
    dps, g["sgu_w_s"], dbs, g["sgu_ln_g"], g["sgu_ln_b"] = sched.carry(
        _sgu_bwd, sv["ps"], dyc, p["sgu_g"], p["sgu_b"], p["ws"], p["bs"], tag + "sgu_bwd")
    g["sgu_b_s"] = dbs[:, :, 0]
    dpc = dpc3.reshape(t, -1)
    dq, dk, dv = dq3.reshape(t, -1), dk3.reshape(t, -1), dv3.reshape(t, -1)
    cols = [sched.carry(_dw, sv["x1t"], m[None], d, m.shape[1], tag + "dw_in_" + k)[0, 0]
            for k, m in (("conv", dpc), ("q", dq), ("k", dk), ("v", dv), ("sgu", dps))]
    w_in_grad = jnp.concatenate(cols[:4] + [dwft.T, cols[4]], axis=1)
    emit("mix_w_in", jnp.moveaxis(w_in_grad.reshape(d, N_SHARDS, -1), 1, 0))
    dx1 = sched.carry(_mix_bwd_dx, dz2, dpc, dq, dk, dv, dps, dflog, p["win"], p["wft"], s, alpha, tag + "mix_bwd_dx")

    def small_ready(dgam, dbet):
        g["ln1_g"], g["ln1_b"] = dgam, dbet
        emit_small(g)

    dx0, _, _ = _ffn_backward(sched, emit, "ffn1", dx1, sv["z1"], p["ln1_g"], p["wd1"], p["wup1"],
                              sv["h1"], sv["a1t"], sv["xt"], alpha, tag + "ffn1", after_mid=small_ready)
    return dx0


def _pack_rows(flat_list):
    v = jnp.concatenate(flat_list)
    n = v.shape[0]
    pad = (-n) % 1024
    return jnp.pad(v, (0, pad)).reshape(-1, 128)


def kernel(x, ln1_g, ln1_b, ffn1_w_up, ffn1_w_down, mix_w_in, fox_b_f, conv_w, sgu_ln_g, sgu_ln_b, sgu_w_s, sgu_b_s, mix_w_out, ln2_g, ln2_b, ffn2_w_up, ffn2_w_down, ln3_g, ln3_b, loss_target, m_ln1_g, m_ln1_b, m_ffn1_w_up, m_ffn1_w_down, m_mix_w_in, m_fox_b_f, m_conv_w, m_sgu_ln_g, m_sgu_ln_b, m_sgu_w_s, m_sgu_b_s, m_mix_w_out, m_ln2_g, m_ln2_b, m_ffn2_w_up, m_ffn2_w_down, m_ln3_g, m_ln3_b, v_ln1_g, v_ln1_b, v_ffn1_w_up, v_ffn1_w_down, v_mix_w_in, v_fox_b_f, v_conv_w, v_sgu_ln_g, v_sgu_ln_b, v_sgu_w_s, v_sgu_b_s, v_mix_w_out, v_ln2_g, v_ln2_b, v_ffn2_w_up, v_ffn2_w_down, v_ln3_g, v_ln3_b):
    wts = dict(ln1_g=ln1_g, ln1_b=ln1_b, ffn1_w_up=ffn1_w_up, ffn1_w_down=ffn1_w_down, mix_w_in=mix_w_in,
               fox_b_f=fox_b_f, conv_w=conv_w, sgu_ln_g=sgu_ln_g, sgu_ln_b=sgu_ln_b, sgu_w_s=sgu_w_s,
               sgu_b_s=sgu_b_s, mix_w_out=mix_w_out, ln2_g=ln2_g, ln2_b=ln2_b, ffn2_w_up=ffn2_w_up,
               ffn2_w_down=ffn2_w_down, ln3_g=ln3_g, ln3_b=ln3_b)
    mom = dict(ln1_g=m_ln1_g, ln1_b=m_ln1_b, ffn1_w_up=m_ffn1_w_up, ffn1_w_down=m_ffn1_w_down, mix_w_in=m_mix_w_in,
               fox_b_f=m_fox_b_f, conv_w=m_conv_w, sgu_ln_g=m_sgu_ln_g, sgu_ln_b=m_sgu_ln_b, sgu_w_s=m_sgu_w_s,
               sgu_b_s=m_sgu_b_s, mix_w_out=m_mix_w_out, ln2_g=m_ln2_g, ln2_b=m_ln2_b, ffn2_w_up=m_ffn2_w_up,
               ffn2_w_down=m_ffn2_w_down, ln3_g=m_ln3_g, ln3_b=m_ln3_b)
    var = dict(ln1_g=v_ln1_g, ln1_b=v_ln1_b, ffn1_w_up=v_ffn1_w_up, ffn1_w_down=v_ffn1_w_down, mix_w_in=v_mix_w_in,
               fox_b_f=v_fox_b_f, conv_w=v_conv_w, sgu_ln_g=v_sgu_ln_g, sgu_ln_b=v_sgu_ln_b, sgu_w_s=v_sgu_w_s,
               sgu_b_s=v_sgu_b_s, mix_w_out=v_mix_w_out, ln2_g=v_ln2_g, ln2_b=v_ln2_b, ffn2_w_up=v_ffn2_w_up,
               ffn2_w_down=v_ffn2_w_down, ln3_g=v_ln3_g, ln3_b=v_ln3_b)

    nl = ln1_g.shape[0]
    b, s, d = x.shape
    t = b * s
    alpha = (2 * nl) ** 0.25
    cw_sh = conv_w.shape[2]
    d_conv = cw_sh * N_SHARDS
    d_sgu = sgu_ln_g.shape[1]
    nh = fox_b_f.shape[1]
    d_fox = nh * FOX_HEAD_DIM
    n_main = 3 * d_conv + 3 * d_fox
    dims = dict(b=b, s=s, proj_widths=(3 * d_conv, 3 * d_fox, 2 * d_sgu))
    cpos = lax.axis_index("c").astype(jnp.int32)
    qpos = (2 * lax.axis_index("x") + lax.axis_index("y")).astype(jnp.int32)
    cq = jnp.stack([cpos, qpos])

    me = (2 * qpos + cpos).reshape(1)
    assert nl == 2, "the gather schedule below names the carriers of a two-layer step"

    conv_tile = jnp.pad(conv_w, ((0, 0), (0, 8 - conv_w.shape[1]), (0, 128 - cw_sh)))
    params = [dict(bf=fox_b_f[l].reshape(nh, 1), sgu_g=_row(sgu_ln_g[l]), sgu_b=_row(sgu_ln_b[l]), ws=sgu_w_s[l],
                   bs=sgu_b_s[l][:, :, None], ln1_g=_row(ln1_g[l]), ln1_b=_row(ln1_b[l]), ln2_g=_row(ln2_g[l]),
                   ln2_b=_row(ln2_b[l]), ln3_g=_row(ln3_g[l]), ln3_b=_row(ln3_b[l])) for l in range(nl)]

    def operands_of(k, arr):
        if k == "mix_w_in":
            w_in = jnp.moveaxis(arr, 0, 1).reshape(d, -1)
            return dict(win=jnp.concatenate([w_in[:, :n_main], w_in[:, n_main + nh:]], axis=1),
                        wft=jnp.pad(w_in[:, n_main:n_main + nh].T, ((0, HEAD_ROWS - nh), (0, 0))))
        if k == "conv_w":
            return dict(cw=jnp.moveaxis(arr[:, :3, :cw_sh], 0, 1).reshape(3, d_conv))
        if k in ("ffn1_w_up", "ffn2_w_up"):
            return {"wup" + k[3]: arr}
        return {dict(ffn1_w_down="wd1", ffn2_w_down="wd2", mix_w_out="wout")[k]: arr.reshape(-1, d)}

    def gather(l, keys):
        side = _side_gather([conv_tile[l] if k == "conv_w" else wts[k][l].astype(BF) for k in keys],
                            [k != "conv_w" for k in keys])

        def install():
            for k, arr in zip(keys, side.results):
                params[l].update(operands_of(k, arr))
        return side, install

    first, install_first = gather(0, ["ffn1_w_up"])
    _run_side(first, "gather_first")
    install_first()
    rides = [{"ffn1_up": gather(0, ["ffn1_w_down", "mix_w_in", "mix_w_out", "conv_w"]),
              "ffn1_down": gather(0, ["ffn2_w_up"]),
              "mix_proj": gather(0, ["ffn2_w_down"]),
              "fox": gather(1, ["ffn1_w_up", "ffn1_w_down", "mix_w_in", "mix_w_out", "conv_w"]),
              "ffn2_up": gather(1, ["ffn2_w_up", "ffn2_w_down"])}, {}]

    act, act_t = x.reshape(t, d), None
    saved = []
    for l in range(nl):
        act, act_t, sv = _forward_layer(act, act_t, params[l], dims, alpha, l, rides[l])
        saved.append(sv)
    dy, loss_blk = _loss_fwd(act, loss_target.reshape(t, d), "loss")

    sched = _Sched()
    prev = {k: None for k in BIG}
    red = {}

    def emit_for(l):
        def emit(key, g):
            st = {}
            name = "l%d_%s" % (l, key)

            def pair_sum(res, side):
                st["p"] = _pair_sum(g, res[0], cq, "rs_pair_sum_" + name, side=side)

            def chip_sum(res, side):
                st["t"] = _chip_sum(st["p"], res[0], cq, "rs_chip_sum_" + name, side=side)

            def adamw(res, side):
                prev[key] = _adamw_shard(wts[key], mom[key], var[key], st["t"], res[0], cq, l, prev[key],
                                         "adamw_" + name, side=side)

            sched.add([("comm", lambda: _side_pair_send([g])), ("compute", pair_sum),
                       ("comm", lambda: _side_scatter([st["p"]])), ("compute", chip_sum),
                       ("comm", lambda: _side_pair_share([st["t"]])), ("compute", adamw)])
        return emit

    def emit_small_for(l):
        def emit_small(g):
            flat = [g[k].reshape(-1) for k in SMALL] + [g["conv_w"].reshape(-1)]
            if l == nl - 1:
                flat.append(loss_blk[0, 0:1])
            vec = _pack_rows(flat)

            def slot_sum(res, side):
                red[l] = _sum_slots(vec, res[0], me, "small_sum_l%d" % l, side=side)

            sched.add([("comm", lambda: _side_bcast(vec)), ("compute", slot_sum)])
        return emit_small

    for l in reversed(range(nl)):
        dy = _backward_layer(sched, emit_for(l), emit_small_for(l), dy, saved[l], params[l], dims, alpha, l)
    sched.drain()
    grad_x = dy.reshape(b, s, d)
    gfin, delta, new_m, new_v = {}, {}, {}, {}
    for k in BIG:
        gfin[k], delta[k], new_m[k], new_v[k] = prev[k]

    gsm = {k: [] for k in SMALL + ("conv_w",)}
    for l in range(nl):
        flat_l = red[l].reshape(-1)
        off = 0
        for k in SMALL:
            n = wts[k][l].size
            gsm[k].append(flat_l[off:off + n].reshape(wts[k][l].shape))
            off += n
        n = 3 * d_conv
        gsm["conv_w"].append(lax.dynamic_slice_in_dim(flat_l[off:off + n].reshape(3, d_conv), qpos * cw_sh, cw_sh,
                                                      axis=1))
        off += n
        if l == nl - 1:
            loss = flat_l[off]
    for k in gsm:
        gfin[k] = jnp.stack(gsm[k])
    small_keys = SMALL + ("conv_w",)
    sizes = [wts[k].size for k in small_keys]
    pk = lambda src: _pack_rows([src[k].reshape(-1) for k in small_keys])
    dl, nm, nv = _adamw_small(pk(wts), pk(gfin), pk(mom), pk(var), "adamw_small")
    off = 0
    for k, n in zip(small_keys, sizes):
        shp = wts[k].shape
        delta[k] = dl.reshape(-1)[off:off + n].reshape(shp)
        new_m[k] = nm.reshape(-1)[off:off + n].reshape(shp)
        new_v[k] = nv.reshape(-1)[off:off + n].reshape(shp)
        off += n

    return (loss, grad_x, *[gfin[k] for k in ORDER], *[delta[k] for k in ORDER],
            *[new_m[k] for k in ORDER], *[new_v[k] for k in ORDER])
```

```python
import jax
import jax.numpy as jnp
from jax import lax
from jax.experimental import pallas as pl
from jax.experimental.pallas import tpu as pltpu

F32 = jnp.float32
BF = jnp.bfloat16
SDS = jax.ShapeDtypeStruct
MESH = pl.DeviceIdType.MESH

LN_EPS = 1e-5
FOX_HEAD_DIM = 64
FOX_Q_BLOCK = 512
DW_TOKENS = 2048
HEAD_ROWS = 128
GELU_K = 0.7978845608028654
GELU_C = 0.044715
NEG_BIG = -1e30
N_SHARDS = 4

ADAM_LR = 0.001
ADAM_B1 = 0.9
ADAM_B2 = 0.999
ADAM_EPS = 1e-08
ADAM_WD = 0.01
ADAM_STEP = 10

VMEM_LIMIT_BYTES = 56 * 1024 * 1024
NT = (((1,), (1,)), ((), ()))
TN = (((0,), (0,)), ((), ()))
HBM = pl.BlockSpec(memory_space=pl.ANY)


def _tile(n, pref, mult=8):
    t = min(n, pref)
    while n % t or t % mult:
        t -= mult
    return t


def _dot(a, b):
    return jnp.dot(a, b, preferred_element_type=F32)


def _dotg(a, b, dims):
    return lax.dot_general(a, b, dims, preferred_element_type=F32)


def _sigmoid(x):
    return 1.0 / (1.0 + jnp.exp(-x))


def _gelu(x):
    return 0.5 * x * (1.0 + jnp.tanh(GELU_K * (x + GELU_C * x * x * x)))


def _gelu_grad(x):
    t = jnp.tanh(GELU_K * (x + GELU_C * x * x * x))
    return 0.5 * (1.0 + t) + 0.5 * x * (1.0 - t * t) * GELU_K * (1.0 + 3.0 * GELU_C * x * x)


def _ln_stats(z):
    mu = jnp.mean(z, axis=-1, keepdims=True)
    zc = z - mu
    var = jnp.mean(zc * zc, axis=-1, keepdims=True)
    rstd = lax.rsqrt(var + LN_EPS)
    return zc * rstd, rstd


def _ln_bwd(dy, z, g):
    xhat, rstd = _ln_stats(z)
    gdy = dy * g
    m1 = jnp.mean(gdy, axis=-1, keepdims=True)
    m2 = jnp.mean(gdy * xhat, axis=-1, keepdims=True)
    dz = rstd * (gdy - m1 - xhat * m2)
    return dz, jnp.sum(dy * xhat, axis=0, keepdims=True), jnp.sum(dy, axis=0, keepdims=True)


class _Side:
    def __init__(self, ins, out_shapes, sems, start, finish):
        self.ins, self.out_shapes, self.sems = list(ins), list(out_shapes), list(sems)
        self.start, self.finish = start, finish
        self.results = None


def _join(sides):
    sides = [s for s in sides if s is not None]
    if not sides:
        return None
    ins = [a for s in sides for a in s.ins]
    outs = [a for s in sides for a in s.out_shapes]
    sems = [a for s in sides for a in s.sems]

    def parts(seq, field):
        out, o = [], 0
        for s in sides:
            n = len(getattr(s, field))
            out.append(seq[o:o + n])
            o += n
        return out

    def run(which):
        def fn(i, o, m):
            for s, a, b, c in zip(sides, parts(i, "ins"), parts(o, "out_shapes"), parts(m, "sems")):
                getattr(s, which)(a, b, c)
        return fn

    joined = _Side(ins, outs, sems, run("start"), run("finish"))
    joined.members = sides
    return joined


def _deliver(side, results):
    members = getattr(side, "members", None)
    side.results = list(results)
    if members:
        o = 0
        for s in members:
            n = len(s.out_shapes)
            _deliver(s, results[o:o + n])
            o += n


def _pcall(body, operands, *, name, grid, in_specs, out_specs, out_shape, sem, scratch_shapes=(),
           prefetch=(), aliases=None, side=None):
    single = not isinstance(out_shape, (list, tuple))
    out_shape = [out_shape] if single else list(out_shape)
    out_specs = [out_specs] if single else list(out_specs)
    in_specs, scratch_shapes = list(in_specs), list(scratch_shapes)
    n_pre, n_in, n_out, n_sc = len(prefetch), len(in_specs), len(out_shape), len(scratch_shapes)
    fn = body
    extra = []
    if side is not None:
        s_in, s_out = len(side.ins), len(side.out_shapes)

        def fn(*refs):
            pre, rest = refs[:n_pre], refs[n_pre:]
            m_in, c_in = rest[:n_in], rest[n_in:n_in + s_in]
            rest = rest[n_in + s_in:]
            m_out, c_out = rest[:n_out], rest[n_out:n_out + s_out]
            rest = rest[n_out + s_out:]
            m_sc, c_sc = rest[:n_sc], rest[n_sc:]
            first = pl.program_id(0) == 0
            last = pl.program_id(0) == grid[0] - 1
            for a in range(1, len(grid)):
                first = jnp.logical_and(first, pl.program_id(a) == 0)
                last = jnp.logical_and(last, pl.program_id(a) == grid[a] - 1)

            @pl.when(first)
            def _():
                side.start(c_in, c_out, c_sc)

            body(*pre, *m_in, *m_out, *m_sc)

            @pl.when(last)
            def _():
                side.finish(c_in, c_out, c_sc)

        in_specs = in_specs + [HBM] * s_in
        out_specs = out_specs + [HBM] * s_out
        out_shape = out_shape + side.out_shapes
        scratch_shapes = scratch_shapes + side.sems
        extra = side.ins
        sem = ("arbitrary",) * len(grid)
    params = pltpu.CompilerParams(dimension_semantics=tuple(sem), vmem_limit_bytes=VMEM_LIMIT_BYTES)
    kw = dict(input_output_aliases=aliases) if aliases else {}
    if n_pre:
        spec = pltpu.PrefetchScalarGridSpec(num_scalar_prefetch=n_pre, grid=grid, in_specs=in_specs,
                                            out_specs=out_specs, scratch_shapes=scratch_shapes)
        call = pl.pallas_call(fn, name=name, grid_spec=spec, out_shape=out_shape, compiler_params=params, **kw)
    else:
        call = pl.pallas_call(fn, name=name, grid=grid, in_specs=in_specs, out_specs=out_specs,
                              out_shape=out_shape, scratch_shapes=scratch_shapes, compiler_params=params, **kw)
    res = call(*prefetch, *operands, *extra)
    if side is not None:
        _deliver(side, res[n_out:])
        res = res[:n_out]
    return res[0] if single else res


def _run_side(side, name):
    def body(*refs):
        n_in, n_out = len(side.ins), len(side.out_shapes)
        i, o, m = refs[:n_in], refs[n_in:n_in + n_out], refs[n_in + n_out:]
        side.start(i, o, m)
        side.finish(i, o, m)

    res = pl.pallas_call(body, name=name, in_specs=[HBM] * len(side.ins), out_specs=[HBM] * len(side.out_shapes),
                         out_shape=side.out_shapes, scratch_shapes=side.sems)(*side.ins)
    _deliver(side, res)


def _mesh_pos():
    x, y, c = lax.axis_index("x"), lax.axis_index("y"), lax.axis_index("c")
    chips = [(1 - x, y), (x, 1 - y), (1 - x, 1 - y)]
    return x, y, c, chips


def _rows(ref, lead, half, n_rows):
    return ref.at[tuple(lead) + (pl.ds(half * n_rows, n_rows),)]


def _side_gather(shards, split):
    n = len(shards)
    hs = [w.shape[0] // 2 for w in shards]

    def plan(ins, outs, sems):
        ssem, rsem = sems
        x, y, c, chips = _mesh_pos()
        q = 2 * x + y
        sib = (x, y, 1 - c)

        def rc(p, k, src, dst, to):
            return pltpu.make_async_remote_copy(src_ref=src, dst_ref=dst, send_sem=ssem.at[p, k],
                                                recv_sem=rsem.at[p, k], device_id=to, device_id_type=MESH)

        def blk(ref, p, qi, half):
            return _rows(ref, (qi,), half, hs[p]) if split[p] else ref.at[qi]

        return x, y, c, chips, q, sib, rc, blk

    def first_sends(ins, outs, sems):
        x, y, c, chips, q, sib, rc, blk = plan(ins, outs, sems)
        cps = [rc(p, 0, ins[p], outs[p].at[q], sib) for p in range(n)]
        for j, (cx, cy) in enumerate(chips):
            for p in range(n):
                src = _rows(ins[p], (), c, hs[p]) if split[p] else ins[p]
                cps.append(rc(p, 1 + j, src, blk(outs[p], p, q, c), (cx, cy, c)))
        return cps

    def start(ins, outs, sems):
        for cp in first_sends(ins, outs, sems):
            cp.start()

    def finish(ins, outs, sems):
        x, y, c, chips, q, sib, rc, blk = plan(ins, outs, sems)
        sent = first_sends(ins, outs, sems)
        for j, (cx, cy) in enumerate(chips):
            qj = 2 * cx + cy
            for p in range(n):
                got = blk(outs[p], p, qj, c)
                rc(p, 1 + j, got, got, (cx, cy, c)).wait_recv()
                if split[p]:
                    fwd = rc(p, 4 + j, got, got, sib)
                    fwd.start()
                    sent.append(fwd)
        for j, (cx, cy) in enumerate(chips):
            qj = 2 * cx + cy
            for p in range(n):
                if split[p]:
                    got = blk(outs[p], p, qj, 1 - c)
                    rc(p, 4 + j, got, got, sib).wait_recv()
        for p in range(n):
            rc(p, 0, outs[p].at[q], outs[p].at[q], sib).wait_recv()
        for cp in sent:
            cp.wait_send()

    return _Side(shards, [SDS((N_SHARDS,) + w.shape, w.dtype) for w in shards],
                 [pltpu.SemaphoreType.DMA((n, 7)), pltpu.SemaphoreType.DMA((n, 7))], start, finish)


def _side_pair_send(gs):
    n = len(gs)

    def copies(ins, outs, sems):
        x, y, c, _ = _mesh_pos()
        return [pltpu.make_async_remote_copy(
            src_ref=ins[p].at[:, pl.ds((1 - c) * (gs[p].shape[1] // 2), gs[p].shape[1] // 2)], dst_ref=outs[p],
            send_sem=sems[0].at[p], recv_sem=sems[1].at[p], device_id=(x, y, 1 - c), device_id_type=MESH)
            for p in range(n)]

    def start(ins, outs, sems):
        for cp in copies(ins, outs, sems):
            cp.start()

    def finish(ins, outs, sems):
        for cp in copies(ins, outs, sems):
            cp.wait()

    return _Side(gs, [SDS((g.shape[0], g.shape[1] // 2, g.shape[2]), g.dtype) for g in gs],
                 [pltpu.SemaphoreType.DMA((n,)), pltpu.SemaphoreType.DMA((n,))], start, finish)


def _side_scatter(ps):
    n = len(ps)

    def sends(ins, outs, sems):
        x, y, c, chips = _mesh_pos()
        q = 2 * x + y
        return [pltpu.make_async_remote_copy(src_ref=ins[p].at[2 * cx + cy], dst_ref=outs[p].at[q],
                                             send_sem=sems[0].at[p, j], recv_sem=sems[1].at[p, j],
                                             device_id=(cx, cy, c), device_id_type=MESH)
                for j, (cx, cy) in enumerate(chips) for p in range(n)]

    def start(ins, outs, sems):
        for cp in sends(ins, outs, sems):
            cp.start()

    def finish(ins, outs, sems):
        x, y, c, chips = _mesh_pos()
        for j, (cx, cy) in enumerate(chips):
            for p in range(n):
                got = outs[p].at[2 * cx + cy]
                pltpu.make_async_remote_copy(src_ref=got, dst_ref=got, send_sem=sems[0].at[p, j],
                                             recv_sem=sems[1].at[p, j], device_id=(cx, cy, c),
                                             device_id_type=MESH).wait_recv()
        for cp in sends(ins, outs, sems):
            cp.wait_send()

    return _Side(ps, [SDS(p.shape, p.dtype) for p in ps],
                 [pltpu.SemaphoreType.DMA((n, 3)), pltpu.SemaphoreType.DMA((n, 3))], start, finish)


def _side_pair_share(tots):
    n = len(tots)

    def copies(ins, outs, sems):
        x, y, c, _ = _mesh_pos()
        return [pltpu.make_async_remote_copy(src_ref=ins[p], dst_ref=outs[p], send_sem=sems[0].at[p],
                                             recv_sem=sems[1].at[p], device_id=(x, y, 1 - c), device_id_type=MESH)
                for p in range(n)]

    def start(ins, outs, sems):
        for cp in copies(ins, outs, sems):
            cp.start()

    def finish(ins, outs, sems):
        for cp in copies(ins, outs, sems):
            cp.wait()

    return _Side(tots, [SDS(t_.shape, t_.dtype) for t_ in tots],
                 [pltpu.SemaphoreType.DMA((n,)), pltpu.SemaphoreType.DMA((n,))], start, finish)


N_DEVICES = 8


def _side_bcast(v):
    def peers():
        x, y, c, _ = _mesh_pos()
        out = []
        for k in range(1, N_DEVICES):
            px, py, pc = x ^ ((k >> 2) & 1), y ^ ((k >> 1) & 1), c ^ (k & 1)
            out.append((k - 1, (px, py, pc), 4 * px + 2 * py + pc))
        return 4 * x + 2 * y + c, out

    def sends(ins, outs, sems):
        me, ps = peers()
        return [pltpu.make_async_remote_copy(src_ref=ins[0], dst_ref=outs[0].at[me], send_sem=sems[0].at[k],
                                             recv_sem=sems[1].at[k], device_id=to, device_id_type=MESH)
                for k, to, _ in ps]

    def start(ins, outs, sems):
        for cp in sends(ins, outs, sems):
            cp.start()

    def finish(ins, outs, sems):
        _, ps = peers()
        for k, to, slot in ps:
            got = outs[0].at[slot]
            pltpu.make_async_remote_copy(src_ref=got, dst_ref=got, send_sem=sems[0].at[k], recv_sem=sems[1].at[k],
                                         device_id=to, device_id_type=MESH).wait_recv()
        for cp in sends(ins, outs, sems):
            cp.wait_send()

    return _Side([v], [SDS((N_DEVICES,) + v.shape, v.dtype)],
                 [pltpu.SemaphoreType.DMA((N_DEVICES - 1,)), pltpu.SemaphoreType.DMA((N_DEVICES - 1,))], start, finish)


def _sum_slots(v, r, me, name, side=None):
    n, rows, lanes = r.shape
    tr = _tile(rows, 512)

    def body(me_ref, v_ref, r_ref, o_ref):
        j = pl.program_id(1)
        term = jnp.where(j == me_ref[0], v_ref[...], r_ref[0])

        @pl.when(j == 0)
        def _():
            o_ref[...] = term

        @pl.when(j != 0)
        def _():
            o_ref[...] += term

    other = lambda j, k: jnp.where(j == k, (k + 1) % n, j)
    return _pcall(
        body, (v, r), name=name, grid=(rows // tr, n), prefetch=(me,),
        in_specs=[pl.BlockSpec((tr, lanes), lambda i, j, me_ref: (i, 0)),
                  pl.BlockSpec((1, tr, lanes), lambda i, j, me_ref: (other(j, me_ref[0]), i, 0))],
        out_specs=pl.BlockSpec((tr, lanes), lambda i, j, me_ref: (i, 0)),
        out_shape=SDS((rows, lanes), F32), sem=("parallel", "arbitrary"), side=side)


def _pair_sum(g, r1, cq, name, side=None):
    nq, xr, yc = g.shape
    h = xr // 2
    tr = _tile(h, 512, 16)
    nt = h // tr

    def body(cq_ref, g_ref, r_ref, o_ref):
        o_ref[...] = (g_ref[...] + r_ref[...]).astype(BF)

    return _pcall(
        body, (g, r1), name=name, grid=(nq, nt), prefetch=(cq,),
        in_specs=[pl.BlockSpec((1, tr, yc), lambda j, i, cq_ref: (j, cq_ref[0] * nt + i, 0)),
                  pl.BlockSpec((1, tr, yc), lambda j, i, cq_ref: (j, i, 0))],
        out_specs=pl.BlockSpec((1, tr, yc), lambda j, i, cq_ref: (j, i, 0)),
        out_shape=SDS((nq, h, yc), BF), sem=("parallel", "parallel"), side=side)


def _chip_sum(p, r2, cq, name, side=None):
    nq, h, yc = r2.shape
    tr = _tile(h, 512, 16)

    def body(cq_ref, p_ref, r_ref, o_ref):
        j = pl.program_id(1)
        term = jnp.where(j == cq_ref[1], p_ref[0], r_ref[0]).astype(F32)

        @pl.when(j == 0)
        def _():
            o_ref[...] = term

        @pl.when(j != 0)
        def _():
            o_ref[...] += term

    other = lambda j, q: jnp.where(j == q, (q + 1) % nq, j)
    return _pcall(
        body, (p, r2), name=name, grid=(h // tr, nq), prefetch=(cq,),
        in_specs=[pl.BlockSpec((1, tr, yc), lambda i, j, cq_ref: (cq_ref[1], i, 0)),
                  pl.BlockSpec((1, tr, yc), lambda i, j, cq_ref: (other(j, cq_ref[1]), i, 0))],
        out_specs=pl.BlockSpec((tr, yc), lambda i, j, cq_ref: (i, 0)),
        out_shape=SDS((h, yc), F32), sem=("parallel", "arbitrary"), side=side)


def _ffn_up_fwd(x, wup, name, side=None, emit_xt=False):
    t, d = x.shape
    w = wup.shape[2]
    tm = _tile(t, 512)

    def body(x_ref, wg_ref, wu_ref, h_ref, a_ref, at_ref, *xt_ref):
        xb = x_ref[...].astype(BF)
        g = _dot(xb, wg_ref[0])
        u = _dot(xb, wu_ref[0])
        h_ref[0] = g.astype(BF)
        h_ref[1] = u.astype(BF)
        ab = (g * _sigmoid(g) * u).astype(BF)
        a_ref[...] = ab
        at_ref[...] = ab.T
        if emit_xt:
            @pl.when(pl.program_id(0) == 0)
            def _():
                xt_ref[0][...] = xb.T

    nt = t // tm
    out_specs = [pl.BlockSpec((2, tm, w), lambda j, i: (0, i, j)), pl.BlockSpec((tm, w), lambda j, i: (i, j)),
                 pl.BlockSpec((w, tm), lambda j, i: (j, i))]
    out_shape = [SDS((2, t, 2 * w), BF), SDS((t, 2 * w), BF), SDS((2 * w, t), BF)]
    if emit_xt:
        out_specs.append(pl.BlockSpec((d, tm), lambda j, i: (0, jnp.where(j == 0, i, nt - 1))))
        out_shape.append(SDS((d, t), BF))
    return _pcall(
        body, (x, wup, wup), name=name, grid=(2, nt),
        in_specs=[pl.BlockSpec((tm, d), lambda j, i: (i, 0)),
                  pl.BlockSpec((1, d, w), lambda j, i: (j, 0, 0)),
                  pl.BlockSpec((1, d, w), lambda j, i: (j + 2, 0, 0))],
        out_specs=out_specs, out_shape=out_shape,
        sem=("arbitrary", "arbitrary") if emit_xt else ("parallel", "parallel"), side=side)


def _res_ln_fwd(parts, w, x, gamma, beta, alpha, res_scale, name, side=None, parts_t=False):
    t, d = x.shape
    n = len(parts)
    offs = [0]
    for p in parts:
        offs.append(offs[-1] + p.shape[1])
    tm = _tile(t, 512)

    def body(*refs):
        p_refs = refs[:n]
        w_ref, x_ref, g_ref, b_ref, z_ref, y_ref, yt_ref = refs[n:n + 7]
        pt_refs = refs[n + 7:]
        f = _dot(p_refs[0][...], w_ref[offs[0]:offs[1], :])
        for k in range(1, n):
            f = f + _dot(p_refs[k][...], w_ref[offs[k]:offs[k + 1], :])
        z = alpha * x_ref[...] + res_scale * f
        z_ref[...] = z
        xhat, _ = _ln_stats(z)
        y = xhat * g_ref[...] + b_ref[...]
        y_ref[...] = y
        yt_ref[...] = y.astype(BF).T
        for k in range(len(pt_refs)):
            pt_refs[k][...] = p_refs[k][...].T

    row = lambda i: (i, 0)
    col = lambda i: (0, i)
    fixed = lambda i: (0, 0)
    out_specs = [pl.BlockSpec((tm, d), row), pl.BlockSpec((tm, d), row), pl.BlockSpec((d, tm), col)]
    out_shape = [SDS((t, d), F32), SDS((t, d), F32), SDS((d, t), BF)]
    if parts_t:
        out_specs += [pl.BlockSpec((p.shape[1], tm), col) for p in parts]
        out_shape += [SDS((p.shape[1], t), BF) for p in parts]
    return _pcall(
        body, (*parts, w, x, gamma, beta), name=name, grid=(t // tm,),
        in_specs=[pl.BlockSpec((tm, p.shape[1]), row) for p in parts]
        + [pl.BlockSpec(w.shape, fixed), pl.BlockSpec((tm, d), row),
           pl.BlockSpec((1, d), fixed), pl.BlockSpec((1, d), fixed)],
        out_specs=out_specs, out_shape=out_shape,
        sem=("parallel",), side=side)


def _mix_proj_fwd(x, w_main, widths, name, side=None):
    t, d = x.shape
    dc, dq, ds = widths
    tm = _tile(t, 512)

    def body(x_ref, w_ref, pc_ref, pq_ref, ps_ref):
        xb = x_ref[...].astype(BF)
        pc_ref[...] = _dot(xb, w_ref[:, 0:dc])
        pq_ref[...] = _dot(xb, w_ref[:, dc:dc + dq]).astype(BF)
        ps_ref[...] = _dot(xb, w_ref[:, dc + dq:dc + dq + ds])

    row = lambda i: (i, 0)
    return _pcall(
        body, (x, w_main), name=name, grid=(t // tm,),
        in_specs=[pl.BlockSpec((tm, d), row), pl.BlockSpec(w_main.shape, lambda i: (0, 0))],
        out_specs=[pl.BlockSpec((tm, dc), row), pl.BlockSpec((tm, dq), row), pl.BlockSpec((tm, ds), row)],
        out_shape=[SDS((t, dc), F32), SDS((t, dq), BF), SDS((t, ds), F32)],
        sem=("parallel",), side=side)


def _prefix_sum_lanes(v, reverse):
    n = v.shape[-1]
    lane = lax.broadcasted_iota(jnp.int32, v.shape, v.ndim - 1)
    sh = 1
    while sh < n:
        if reverse:
            v = v + jnp.where(lane < n - sh, pltpu.roll(v, n - sh, axis=v.ndim - 1), 0.0)
        else:
            v = v + jnp.where(lane >= sh, pltpu.roll(v, sh, axis=v.ndim - 1), 0.0)
        sh *= 2
    return v


def _cum_fwd(x3, wft, bf, name, side=None):
    b, s, d = x3.shape
    h = bf.shape[0]

    def body(x_ref, w_ref, b_ref, fl_ref, cum_ref):
        fl = _dotg(w_ref[...], x_ref[0].astype(BF), NT)[0:h] + b_ref[...]
        fl_ref[0] = fl
        lf = jnp.minimum(fl, 0.0) - jnp.log(1.0 + jnp.exp(-jnp.abs(fl)))
        cum_ref[0] = _prefix_sum_lanes(lf, reverse=False)

    return _pcall(
        body, (x3, wft, bf), name=name, grid=(b,),
        in_specs=[pl.BlockSpec((1, s, d), lambda i: (i, 0, 0)),
                  pl.BlockSpec(wft.shape, lambda i: (0, 0)), pl.BlockSpec((h, 1), lambda i: (0, 0))],
        out_specs=[pl.BlockSpec((1, h, s), lambda i: (i, 0, 0)), pl.BlockSpec((1, h, s), lambda i: (i, 0, 0))],
        out_shape=[SDS((b, h, s), F32), SDS((b, h, s), F32)],
        sem=("parallel",), side=side)


def _shift_rows(z, k, down):
    n = z.shape[0]
    row = lax.broadcasted_iota(jnp.int32, z.shape, 0)
    if down:
        return jnp.where(row >= k, pltpu.roll(z, k, axis=0), 0.0)
    return jnp.where(row < n - k, pltpu.roll(z, n - k, axis=0), 0.0)


def _conv_fwd(pc3, cw, name, side=None):
    b, s, c3 = pc3.shape
    c = c3 // 3

    def body(p_ref, w_ref, y_ref):
        z = p_ref[0, :, c:2 * c] * p_ref[0, :, 2 * c:3 * c]
        conv = w_ref[0:1, :] * _shift_rows(z, 2, True) + w_ref[1:2, :] * _shift_rows(z, 1, True) + w_ref[2:3, :] * z
        y_ref[0] = (p_ref[0, :, 0:c] * conv).astype(BF)

    return _pcall(
        body, (pc3, cw), name=name, grid=(b,),
        in_specs=[pl.BlockSpec((1, s, c3), lambda i: (i, 0, 0)), pl.BlockSpec((3, c), lambda i: (0, 0))],
        out_specs=pl.BlockSpec((1, s, c), lambda i: (i, 0, 0)),
        out_shape=SDS((b, s, c), BF), sem=("parallel",), side=side)


def _head_masks(width):
    lane = lax.broadcasted_iota(jnp.int32, (1, width), 1)
    return [lane < FOX_HEAD_DIM, lane >= FOX_HEAD_DIM]


def _fox_scores(q, k, cum_row, lo, head_mask):
    tq = q.shape[0]
    qm = jnp.where(head_mask, q * (FOX_HEAD_DIM ** -0.5), 0)
    s = _dotg(qm, k, NT) - cum_row
    tri = lax.broadcasted_iota(jnp.int32, (tq, tq), 1) <= lax.broadcasted_iota(jnp.int32, (tq, tq), 0)
    diag = jnp.where(tri, s[:, lo:], NEG_BIG)
    return (diag if lo == 0 else jnp.concatenate([s[:, :lo], diag], axis=1)), qm


def _fox_fwd(pq3, cum4, name, side=None):
    b, s, d3 = pq3.shape
    df = d3 // 3
    hp = df // 128
    tq = _tile(s, FOX_Q_BLOCK)

    def body(q_ref, k_ref, v_ref, c_ref, o_ref, lse_ref):
        masks = _head_masks(128)
        for i in range(s // tq):
            lo, hi = i * tq, (i + 1) * tq
            q = q_ref[0, lo:hi, :]
            k = k_ref[0, 0:hi, :]
            v = v_ref[0, 0:hi, :]
            o = jnp.zeros((tq, 128), F32)
            lse = jnp.zeros((tq, 128), F32)
            for e in range(2):
                sc, _ = _fox_scores(q, k, c_ref[0, 0, e:e + 1, 0:hi], lo, masks[e])
                m = jnp.max(sc, axis=-1, keepdims=True)
                p = jnp.exp(sc - m)
                l = jnp.sum(p, axis=-1, keepdims=True)
                o = jnp.where(masks[e], _dot(p.astype(BF), v) * (1.0 / l), o)
                lse = jnp.where(masks[e], m + jnp.log(l), lse)
            o_ref[0, lo:hi, :] = o.astype(BF)
            lse_ref[0, 0, lo:hi, :] = lse

    blk = lambda off: pl.BlockSpec((1, s, 128), lambda i, j: (i, 0, off + j))
    return _pcall(
        body, (pq3, pq3, pq3, cum4), name=name, grid=(b, hp),
        in_specs=[blk(0), blk(hp), blk(2 * hp), pl.BlockSpec((1, 1, 2, s), lambda i, j: (i, j, 0, 0))],
        out_specs=[blk(0), pl.BlockSpec((1, 1, s, 128), lambda i, j: (i, j, 0, 0))],
        out_shape=[SDS((b, s, df), BF), SDS((b, hp, s, 128), F32)],
        sem=("parallel", "parallel"), side=side)


def _sgu_mix(wm, vnb, bias, gmasks):
    out = bias
    for g in range(len(wm)):
        out = out + jnp.where(gmasks[g], _dot(wm[g], vnb), 0.0)
    return out


def _sgu_consts(ws_ref, bs_ref, ds):
    ng, c, _ = ws_ref.shape
    gd = ds // ng
    tri = lax.broadcasted_iota(jnp.int32, (c, c), 0) >= lax.broadcasted_iota(jnp.int32, (c, c), 1)
    lane = lax.broadcasted_iota(jnp.int32, (1, ds), 1)
    gmasks = [(lane >= g * gd) & (lane < (g + 1) * gd) for g in range(ng)]
    wm = [jnp.where(tri, ws_ref[g], 0.0).astype(BF) for g in range(ng)]
    bias = jnp.zeros((c, ds), F32)
    for g in range(ng):
        bias = jnp.where(gmasks[g], bs_ref[g], bias)
    return tri, gmasks, wm, bias


def _sgu_fwd(ps, lng, lnb, ws, bs, name, side=None):
    t, ds2 = ps.shape
    ds = ds2 // 2
    c = ws.shape[1]
    tm = _tile(t, 512, c)

    def body(p_ref, g_ref, b_ref, ws_ref, bs_ref, y_ref):
        _, gmasks, wm, bias = _sgu_consts(ws_ref, bs_ref, ds)
        up = _gelu(p_ref[:, 0:ds])
        xhat, _ = _ln_stats(_gelu(p_ref[:, ds:ds2]))
        vnb = (xhat * g_ref[...] + b_ref[...]).astype(BF)
        for n in range(tm // c):
            r0, r1 = n * c, (n + 1) * c
            y_ref[r0:r1, :] = (up[r0:r1] * _sgu_mix(wm, vnb[r0:r1], bias, gmasks)).astype(BF)

    fixed2 = lambda i: (0, 0)
    fixed3 = lambda i: (0, 0, 0)
    return _pcall(
        body, (ps, lng, lnb, ws, bs), name=name, grid=(t // tm,),
        in_specs=[pl.BlockSpec((tm, ds2), lambda i: (i, 0)), pl.BlockSpec((1, ds), fixed2),
                  pl.BlockSpec((1, ds), fixed2), pl.BlockSpec(ws.shape, fixed3), pl.BlockSpec(bs.shape, fixed3)],
        out_specs=pl.BlockSpec((tm, ds), lambda i: (i, 0)),
        out_shape=SDS((t, ds), BF), sem=("parallel",), side=side)


def _loss_fwd(y, target, name, side=None):
    t, d = y.shape
    tm = _tile(t, 512)

    def body(y_ref, t_ref, dy_ref, l_ref):
        @pl.when(pl.program_id(0) == 0)
        def _():
            l_ref[...] = jnp.zeros_like(l_ref)

        err = y_ref[...] - t_ref[...]
        dy_ref[...] = err * (1.0 / d)
        l_ref[...] += 0.5 * jnp.sum(jnp.sum(err * err, axis=-1, keepdims=True) * (1.0 / d), axis=0, keepdims=True)

    row = lambda i: (i, 0)
    return _pcall(
        body, (y, target), name=name, grid=(t // tm,),
        in_specs=[pl.BlockSpec((tm, d), row), pl.BlockSpec((tm, d), row)],
        out_specs=[pl.BlockSpec((tm, d), row), pl.BlockSpec((8, 128), lambda i: (0, 0))],
        out_shape=[SDS((t, d), F32), SDS((8, 128), F32)],
        sem=("arbitrary",), side=side)


def _acc_rows(ref, val, first):
    @pl.when(first)
    def _():
        ref[...] = val

    @pl.when(jnp.logical_not(first))
    def _():
        ref[...] += val


def _ffn_bwd_mid(dy, z, gamma, wd, h, name, side=None):
    t, d = dy.shape
    dff = wd.shape[0]
    half = dff // 2
    tm = _tile(t, 512)

    def body(dy_ref, z_ref, g_ref, wd_ref, h_ref, dz_ref, df_ref, dh_ref, dg_ref, db_ref):
        dz, dgam, dbet = _ln_bwd(dy_ref[...], z_ref[...], g_ref[...])
        first = pl.program_id(0) == 0
        _acc_rows(dg_ref, dgam, first)
        _acc_rows(db_ref, dbet, first)
        dz_ref[...] = dz
        dfb = (0.5 * dz).astype(BF)
        df_ref[...] = dfb
        for j in range(2):
            c0, c1 = j * half, (j + 1) * half
            da = _dotg(dfb, wd_ref[c0:c1, :], NT).astype(BF)
            g = h_ref[0, :, c0:c1]
            u = h_ref[1, :, c0:c1]
            sg = _sigmoid(g)
            dh_ref[0, :, c0:c1] = da * u * sg * (1.0 + g * (1.0 - sg))
            dh_ref[1, :, c0:c1] = da * g * sg

    row = lambda i: (i, 0)
    fixed = lambda i: (0, 0)
    return _pcall(
        body, (dy, z, gamma, wd, h), name=name, grid=(t // tm,),
        in_specs=[pl.BlockSpec((tm, d), row), pl.BlockSpec((tm, d), row), pl.BlockSpec((1, d), fixed),
                  pl.BlockSpec(wd.shape, fixed, pipeline_mode=pl.Buffered(1)),
                  pl.BlockSpec((2, tm, dff), lambda i: (0, i, 0))],
        out_specs=[pl.BlockSpec((tm, d), row), pl.BlockSpec((tm, d), row),
                   pl.BlockSpec((2, tm, dff), lambda i: (0, i, 0)),
                   pl.BlockSpec((1, d), fixed), pl.BlockSpec((1, d), fixed)],
        out_shape=[SDS((t, d), F32), SDS((t, d), BF), SDS((2, t, dff), BF), SDS((1, d), F32), SDS((1, d), F32)],
        sem=("arbitrary",), side=side)


def _ffn_bwd_dx(dh, wup, dz, alpha, name, side=None):
    _, t, dff = dh.shape
    nq, d, w = wup.shape
    per = dff // w
    tm = _tile(t, 512)

    def body(dh_ref, w_ref, dz_ref, dx_ref):
        acc = alpha * dz_ref[...]
        for q in range(nq):
            c0 = (q % per) * w
            acc = acc + _dotg(dh_ref[q // per, :, c0:c0 + w], w_ref[q], NT)
        dx_ref[...] = acc

    row = lambda i: (i, 0)
    return _pcall(
        body, (dh, wup, dz), name=name, grid=(t // tm,),
        in_specs=[pl.BlockSpec((2, tm, dff), lambda i: (0, i, 0)),
                  pl.BlockSpec(wup.shape, lambda i: (0, 0, 0), pipeline_mode=pl.Buffered(1)),
                  pl.BlockSpec((tm, d), row)],
        out_specs=pl.BlockSpec((tm, d), row), out_shape=SDS((t, d), F32),
        sem=("parallel",), side=side)


def _dw(at, b3, ka, nb, name, side=None):
    ka_tot, t = at.shape
    gb, _, nb_tot = b3.shape
    na, ncb = ka_tot // ka, nb_tot // nb
    tm = _tile(t, DW_TOKENS, 128)

    def body(a_ref, b_ref, o_ref):
        part = _dot(a_ref[...], b_ref[0])

        @pl.when(pl.program_id(2) == 0)
        def _():
            o_ref[0, 0] = part

        @pl.when(pl.program_id(2) != 0)
        def _():
            o_ref[0, 0] += part

    return _pcall(
        body, (at, b3), name=name, grid=(na, gb * ncb, t // tm),
        in_specs=[pl.BlockSpec((ka, tm), lambda ja, jb, i: (ja, i)),
                  pl.BlockSpec((1, tm, nb), lambda ja, jb, i: (jb // ncb, i, jb % ncb))],
        out_specs=pl.BlockSpec((1, 1, ka, nb), lambda ja, jb, i: (ja, jb, 0, 0)),
        out_shape=SDS((na, gb * ncb, ka, nb), F32),
        sem=("parallel", "parallel", "arbitrary"), side=side)


def _out_bwd(dy, z, gamma, wout, widths, name, side=None):
    t, d = dy.shape
    wa, wb, wc = widths
    tm = _tile(t, 512)

    def body(dy_ref, z_ref, g_ref, w_ref, dz_ref, dzb_ref, da_ref, dbb_ref, dc_ref, dg_ref, db_ref):
        dz, dgam, dbet = _ln_bwd(dy_ref[...], z_ref[...], g_ref[...])
        first = pl.program_id(0) == 0
        _acc_rows(dg_ref, dgam, first)
        _acc_rows(db_ref, dbet, first)
        dz_ref[...] = dz
        dzb = dz.astype(BF)
        dzb_ref[...] = dzb
        da_ref[...] = _dotg(dzb, w_ref[0:wa, :], NT).astype(BF)
        dbb_ref[...] = _dotg(dzb, w_ref[wa:wa + wb, :], NT).astype(BF)
        dc_ref[...] = _dotg(dzb, w_ref[wa + wb:wa + wb + wc, :], NT).astype(BF)

    row = lambda i: (i, 0)
    fixed = lambda i: (0, 0)
    return _pcall(
        body, (dy, z, gamma, wout), name=name, grid=(t // tm,),
        in_specs=[pl.BlockSpec((tm, d), row), pl.BlockSpec((tm, d), row), pl.BlockSpec((1, d), fixed),
                  pl.BlockSpec(wout.shape, fixed)],
        out_specs=[pl.BlockSpec((tm, d), row), pl.BlockSpec((tm, d), row), pl.BlockSpec((tm, wa), row),
                   pl.BlockSpec((tm, wb), row), pl.BlockSpec((tm, wc), row),
                   pl.BlockSpec((1, d), fixed), pl.BlockSpec((1, d), fixed)],
        out_shape=[SDS((t, d), F32), SDS((t, d), BF), SDS((t, wa), BF), SDS((t, wb), BF), SDS((t, wc), BF),
                   SDS((1, d), F32), SDS((1, d), F32)],
        sem=("arbitrary",), side=side)


def _conv_bwd(pc3, dya3, cw, name, side=None):
    b, s, c3 = pc3.shape
    c = c3 // 3

    def body(p_ref, dy_ref, w_ref, dp_ref, dw_ref):
        cb = p_ref[0, :, 0:c]
        cc = p_ref[0, :, c:2 * c]
        ch = p_ref[0, :, 2 * c:3 * c]
        z = cc * ch
        z1 = _shift_rows(z, 1, True)
        z2 = _shift_rows(z, 2, True)
        w0, w1, w2 = w_ref[0:1, :], w_ref[1:2, :], w_ref[2:3, :]
        dy = dy_ref[0].astype(F32)
        dconv = dy * cb
        dz = w2 * dconv + w1 * _shift_rows(dconv, 1, False) + w0 * _shift_rows(dconv, 2, False)
        dp_ref[0, :, 0:c] = (dy * (w0 * z2 + w1 * z1 + w2 * z)).astype(BF)
        dp_ref[0, :, c:2 * c] = (dz * ch).astype(BF)
        dp_ref[0, :, 2 * c:3 * c] = (dz * cc).astype(BF)
        first = pl.program_id(0) == 0
        for r, zs in enumerate((z2, z1, z)):
            _acc_rows(dw_ref.at[r:r + 1], jnp.sum(dconv * zs, axis=0, keepdims=True), first)

    blk = lambda i: (i, 0, 0)
    return _pcall(
        body, (pc3, dya3, cw), name=name, grid=(b,),
        in_specs=[pl.BlockSpec((1, s, c3), blk), pl.BlockSpec((1, s, c), blk), pl.BlockSpec((3, c), lambda i: (0, 0))],
        out_specs=[pl.BlockSpec((1, s, c3), blk), pl.BlockSpec((3, c), lambda i: (0, 0))],
        out_shape=[SDS((b, s, c3), BF), SDS((3, c), F32)],
        sem=("arbitrary",), side=side)


def _fox_bwd(pq3, cum4, lse4, dyb3, name, side=None):
    b, s, d3 = pq3.shape
    df = d3 // 3
    hp = df // 128
    tq = _tile(s, FOX_Q_BLOCK)
    scale = FOX_HEAD_DIM ** -0.5

    def body(q_ref, k_ref, v_ref, c_ref, lse_ref, do_ref, dq_ref, dk_ref, dv_ref, dc_ref, dk_acc, dv_acc):
        masks = _head_masks(128)
        dk_acc[...] = jnp.zeros_like(dk_acc)
        dv_acc[...] = jnp.zeros_like(dv_acc)
        dc_ref[...] = jnp.zeros_like(dc_ref)
        for i in range(s // tq):
            lo, hi = i * tq, (i + 1) * tq
            q = q_ref[0, lo:hi, :]
            do = do_ref[0, lo:hi, :]
            k = k_ref[0, 0:hi, :]
            v = v_ref[0, 0:hi, :]
            lse = lse_ref[0, 0, lo:hi, :]
            dq = jnp.zeros((tq, 128), F32)
            for e in range(2):
                dom = jnp.where(masks[e], do, 0)
                sc, qm = _fox_scores(q, k, c_ref[0, 0, e:e + 1, 0:hi], lo, masks[e])
                p = jnp.exp(sc - lse[:, FOX_HEAD_DIM * e:FOX_HEAD_DIM * e + 1])
                dp = _dotg(dom, v, NT)
                ds = p * (dp - jnp.sum(p * dp, axis=-1, keepdims=True))
                dsb = ds.astype(BF)
                dq = jnp.where(masks[e], _dot(dsb, k) * scale, dq)
                dk_acc[0:hi, :] += _dotg(dsb, qm, TN)
                dv_acc[0:hi, :] += _dotg(p.astype(BF), dom, TN)
                dc_ref[0, 0, e:e + 1, 0:hi] -= jnp.sum(ds, axis=0, keepdims=True)
            dq_ref[0, lo:hi, :] = dq.astype(BF)
        dk_ref[0] = dk_acc[...].astype(BF)
        dv_ref[0] = dv_acc[...].astype(BF)

    blk = lambda off: pl.BlockSpec((1, s, 128), lambda i, j: (i, 0, off + j))
    cblk = pl.BlockSpec((1, 1, 2, s), lambda i, j: (i, j, 0, 0))
    return _pcall(
        body, (pq3, pq3, pq3, cum4, lse4, dyb3), name=name, grid=(b, hp),
        in_specs=[blk(0), blk(hp), blk(2 * hp), cblk,
                  pl.BlockSpec((1, 1, s, 128), lambda i, j: (i, j, 0, 0)), blk(0)],
        out_specs=[blk(0), blk(0), blk(0), cblk],
        out_shape=[SDS((b, s, df), BF), SDS((b, s, df), BF), SDS((b, s, df), BF), SDS(cum4.shape, F32)],
        scratch_shapes=[pltpu.VMEM((s, 128), F32), pltpu.VMEM((s, 128), F32)],
        sem=("parallel", "parallel"), side=side)


def _cum_bwd(dcum, flog, x3, name, side=None):
    b, h, s = dcum.shape
    d = x3.shape[2]

    def body(dc_ref, fl_ref, x_ref, dfl_ref, dbf_ref, dwf_ref):
        dfl = _prefix_sum_lanes(dc_ref[0], reverse=True) * _sigmoid(-fl_ref[0])
        dfl_ref[0] = dfl
        first = pl.program_id(0) == 0
        _acc_rows(dbf_ref, jnp.broadcast_to(jnp.sum(dfl, axis=-1, keepdims=True), (h, 128)), first)
        dflp = jnp.concatenate([dfl, jnp.zeros((HEAD_ROWS - h, s), F32)], axis=0).astype(BF)
        _acc_rows(dwf_ref, _dot(dflp, x_ref[0].astype(BF))[0:h], first)

    blk = lambda i: (i, 0, 0)
    return _pcall(
        body, (dcum, flog, x3), name=name, grid=(b,),
        in_specs=[pl.BlockSpec((1, h, s), blk), pl.BlockSpec((1, h, s), blk), pl.BlockSpec((1, s, d), blk)],
        out_specs=[pl.BlockSpec((1, h, s), blk), pl.BlockSpec((h, 128), lambda i: (0, 0)),
                   pl.BlockSpec((h, d), lambda i: (0, 0))],
        out_shape=[SDS((b, h, s), F32), SDS((h, 128), F32), SDS((h, d), F32)],
        sem=("arbitrary",), side=side)


def _sgu_bwd(ps, dyc, lng, lnb, ws, bs, name, side=None):
    t, ds2 = ps.shape
    ds = ds2 // 2
    ng, c, _ = ws.shape
    tm = _tile(t, 512, c)

    def body(p_ref, dy_ref, g_ref, b_ref, ws_ref, bs_ref, dp_ref, dws_ref, dbs_ref, dg_ref, db_ref, dvn_acc):
        tri, gmasks, wm, bias = _sgu_consts(ws_ref, bs_ref, ds)
        su = p_ref[:, 0:ds]
        sv = p_ref[:, ds:ds2]
        up = _gelu(su)
        gv = _gelu(sv)
        xhat, rstd = _ln_stats(gv)
        vnb = (xhat * g_ref[...] + b_ref[...]).astype(BF)
        dy = dy_ref[...].astype(F32)
        dws = [jnp.zeros((c, c), F32) for _ in range(ng)]
        dbs = [jnp.zeros((c, 1), F32) for _ in range(ng)]
        for n in range(tm // c):
            r0, r1 = n * c, (n + 1) * c
            mixed = _sgu_mix(wm, vnb[r0:r1], bias, gmasks)
            dp_ref[r0:r1, 0:ds] = (dy[r0:r1] * mixed * _gelu_grad(su[r0:r1])).astype(BF)
            dmix = dy[r0:r1] * up[r0:r1]
            dvn = jnp.zeros((c, ds), F32)
            for g in range(ng):
                dmg = jnp.where(gmasks[g], dmix, 0.0)
                dmb = dmg.astype(BF)
                dws[g] = dws[g] + _dotg(dmb, vnb[r0:r1], NT)
                dbs[g] = dbs[g] + jnp.sum(dmg, axis=-1, keepdims=True)
                dvn = dvn + _dotg(wm[g], dmb, TN)
            dvn_acc[r0:r1, :] = dvn
        dvn_all = dvn_acc[...]
        gdv = dvn_all * g_ref[...]
        m1 = jnp.mean(gdv, axis=-1, keepdims=True)
        m2 = jnp.mean(gdv * xhat, axis=-1, keepdims=True)
        dgv = rstd * (gdv - m1 - xhat * m2)
        dp_ref[:, ds:ds2] = (dgv * _gelu_grad(sv)).astype(BF)
        first = pl.program_id(0) == 0
        _acc_rows(dg_ref, jnp.sum(dvn_all * xhat, axis=0, keepdims=True), first)
        _acc_rows(db_ref, jnp.sum(dvn_all, axis=0, keepdims=True), first)
        for g in range(ng):
            _acc_rows(dws_ref.at[g], jnp.where(tri, dws[g], 0.0), first)
            _acc_rows(dbs_ref.at[g], dbs[g], first)

    row = lambda i: (i, 0)
    fixed2 = lambda i: (0, 0)
    fixed3 = lambda i: (0, 0, 0)
    return _pcall(
        body, (ps, dyc, lng, lnb, ws, bs), name=name, grid=(t // tm,),
        in_specs=[pl.BlockSpec((tm, ds2), row), pl.BlockSpec((tm, ds), row), pl.BlockSpec((1, ds), fixed2),
                  pl.BlockSpec((1, ds), fixed2), pl.BlockSpec(ws.shape, fixed3), pl.BlockSpec(bs.shape, fixed3)],
        out_specs=[pl.BlockSpec((tm, ds2), row), pl.BlockSpec(ws.shape, fixed3), pl.BlockSpec(bs.shape, fixed3),
                   pl.BlockSpec((1, ds), fixed2), pl.BlockSpec((1, ds), fixed2)],
        out_shape=[SDS((t, ds2), BF), SDS(ws.shape, F32), SDS(bs.shape, F32), SDS((1, ds), F32), SDS((1, ds), F32)],
        scratch_shapes=[pltpu.VMEM((tm, ds), F32)],
        sem=("arbitrary",), side=side)


def _mix_bwd_dx(dz, dconv, dq, dk, dv, dsgu, dflog, w_main, wft, seq, alpha, name, side=None):
    t, d = dz.shape
    groups = [dconv, dq, dk, dv, dsgu]
    offs = [0]
    for g in groups:
        offs.append(offs[-1] + g.shape[1])
    h = dflog.shape[1]
    tm = _tile(seq, 512)
    per_seq = seq // tm

    def body(dz_ref, a0, a1, a2, a3, a4, dfl_ref, w_ref, wf_ref, dx_ref):
        dflp = jnp.concatenate([dfl_ref[0], jnp.zeros((HEAD_ROWS - h, tm), F32)], axis=0).astype(BF)
        acc = alpha * dz_ref[...] + _dotg(dflp, wf_ref[...], TN)
        for k, a_ref in enumerate((a0, a1, a2, a3, a4)):
            acc = acc + _dotg(a_ref[...], w_ref[:, offs[k]:offs[k + 1]], NT)
        dx_ref[...] = acc

    row = lambda i: (i, 0)
    return _pcall(
        body, (dz, *groups, dflog, w_main, wft), name=name, grid=(t // tm,),
        in_specs=[pl.BlockSpec((tm, d), row)] + [pl.BlockSpec((tm, g.shape[1]), row) for g in groups]
        + [pl.BlockSpec((1, h, tm), lambda i: (i // per_seq, 0, i % per_seq)),
           pl.BlockSpec(w_main.shape, lambda i: (0, 0)), pl.BlockSpec(wft.shape, lambda i: (0, 0))],
        out_specs=pl.BlockSpec((tm, d), row), out_shape=SDS((t, d), F32),
        sem=("parallel",), side=side)


def _adam_math(w, g, m, v):
    c1 = 1.0 / (1.0 - ADAM_B1 ** ADAM_STEP)
    c2 = 1.0 / (1.0 - ADAM_B2 ** ADAM_STEP)
    nm = ADAM_B1 * m + (1.0 - ADAM_B1) * g
    nv = ADAM_B2 * v + (1.0 - ADAM_B2) * (g * g)
    delta = -ADAM_LR * ((nm * c1) / (jnp.sqrt(nv * c2) + ADAM_EPS) + ADAM_WD * w)
    return delta, nm, nv


def _adamw_small(w, g, m, v, name):
    r, c = w.shape
    tr = _tile(r, 512)

    def body(w_ref, g_ref, m_ref, v_ref, d_ref, nm_ref, nv_ref):
        d_ref[...], nm_ref[...], nv_ref[...] = _adam_math(w_ref[...], g_ref[...], m_ref[...], v_ref[...])

    blk = pl.BlockSpec((tr, c), lambda i: (i, 0))
    return _pcall(body, (w, g, m, v), name=name, grid=(r // tr,), in_specs=[blk] * 4, out_specs=[blk] * 3,
                  out_shape=[SDS((r, c), F32)] * 3, sem=("parallel",))


def _adamw_shard(w, m, v, tot, recv, cq, layer, prev, name, side=None):
    nl, xr, yc = w.shape
    h = xr // 2
    tr = _tile(h, 256)
    nt = h // tr

    def body(cq_ref, w_ref, m_ref, v_ref, t_ref, r_ref, *rest):
        g_ref, d_ref, nm_ref, nv_ref = rest[-4:]
        g = jnp.where(pl.program_id(0) == cq_ref[0], t_ref[...], r_ref[...])
        g_ref[0] = g
        d_ref[0], nm_ref[0], nv_ref[0] = _adam_math(w_ref[0], g, m_ref[0], v_ref[0])

    slab = pl.BlockSpec((1, tr, yc), lambda hf, i, cq_ref: (layer, hf * nt + i, 0))
    mine = pl.BlockSpec((tr, yc), lambda hf, i, cq_ref: (jnp.where(hf == cq_ref[0], i, 0), 0))
    theirs = pl.BlockSpec((tr, yc), lambda hf, i, cq_ref: (jnp.where(hf == cq_ref[0], 0, i), 0))
    operands = [w, m, v, tot, recv]
    in_specs = [slab, slab, slab, mine, theirs]
    aliases = None
    if prev is not None:
        operands += list(prev)
        in_specs += [HBM] * 4
        aliases = {6 + k: k for k in range(4)}
    return _pcall(body, operands, name=name, grid=(2, nt), prefetch=(cq,), in_specs=in_specs,
                  out_specs=[slab] * 4, out_shape=[SDS(w.shape, F32)] * 4, aliases=aliases,
                  sem=("parallel", "parallel"), side=side)


BIG = ("ffn1_w_up", "ffn1_w_down", "mix_w_in", "mix_w_out", "ffn2_w_up", "ffn2_w_down")
SMALL = ("ln1_g", "ln1_b", "fox_b_f", "sgu_ln_g", "sgu_ln_b", "sgu_w_s", "sgu_b_s", "ln2_g", "ln2_b", "ln3_g", "ln3_b")
ORDER = ("ln1_g", "ln1_b", "ffn1_w_up", "ffn1_w_down", "mix_w_in", "fox_b_f", "conv_w", "sgu_ln_g", "sgu_ln_b",
         "sgu_w_s", "sgu_b_s", "mix_w_out", "ln2_g", "ln2_b", "ffn2_w_up", "ffn2_w_down", "ln3_g", "ln3_b")


def _row(v):
    return v.reshape(1, -1)


class _Pipe:
    def __init__(self, stages):
        self.stages = list(stages)
        self.pos = 0
        self.last = None

    def kind(self):
        return self.stages[self.pos][0] if self.pos < len(self.stages) else None


class _Sched:
    def __init__(self):
        self.pipes = []
        self.n_alone = 0

    def add(self, stages):
        self.pipes.append(_Pipe(stages))

    def _take_comms(self, skip=None):
        jobs = []
        for p in self.pipes:
            if p is not skip and p.kind() == "comm":
                jobs.append((p, p.stages[p.pos][1]()))
        return jobs

    @staticmethod
    def _landed(jobs):
        for p, side in jobs:
            p.last = side.results
            p.pos += 1

    def carry(self, builder, *args, **kw):
        jobs = self._take_comms()
        res = builder(*args, side=_join([s for _, s in jobs]), **kw)
        self._landed(jobs)
        self._computes(ride=False)
        return res

    def _computes(self, ride):
        again = True
        while again:
            again = False
            for p in self.pipes:
                if p.kind() == "compute":
                    jobs = self._take_comms(skip=p) if ride else []
                    p.stages[p.pos][1](p.last, _join([s for _, s in jobs]))
                    p.pos += 1
                    self._landed(jobs)
                    again = True

    def drain(self):
        while any(p.kind() is not None for p in self.pipes):
            self._computes(ride=True)
            jobs = self._take_comms()
            if jobs:
                _run_side(_join([s for _, s in jobs]), "exchange_tail_%d" % self.n_alone)
                self.n_alone += 1
                self._landed(jobs)


def _forward_layer(x, xt, p, dims, alpha, l, ride):
    b, s = dims["b"], dims["s"]
    t, d = x.shape
    tag = "l%d_" % l

    def run(stage, builder, *args, **kw):
        side, on_done = ride.get(stage, (None, None))
        res = builder(*args, tag + stage, side=side, **kw)
        if on_done is not None:
            on_done()
        return res

    if xt is None:
        h1, a1, a1t, xt = run("ffn1_up", _ffn_up_fwd, x, p["wup1"], emit_xt=True)
    else:
        h1, a1, a1t = run("ffn1_up", _ffn_up_fwd, x, p["wup1"])
    z1, x1, x1t = run("ffn1_down", _res_ln_fwd, [a1], p["wd1"], x, p["ln1_g"], p["ln1_b"], alpha, 0.5)
    pc, pq, ps = run("mix_proj", _mix_proj_fwd, x1, p["win"], dims["proj_widths"])
    x1_3 = x1.reshape(b, s, d)
    flog, cum = run("fox_gate", _cum_fwd, x1_3, p["wft"], p["bf"])
    nh = flog.shape[1]
    cum4 = cum.reshape(b, nh // 2, 2, s)
    pc3 = pc.reshape(b, s, -1)
    pq3 = pq.reshape(b, s, -1)
    ya = run("conv", _conv_fwd, pc3, p["cw"]).reshape(t, -1)
    yb3, lse4 = run("fox", _fox_fwd, pq3, cum4)
    yb = yb3.reshape(t, -1)
    yc = run("sgu", _sgu_fwd, ps, p["sgu_g"], p["sgu_b"], p["ws"], p["bs"])
    z2, x2, x2t, yat, ybt, yct = run("mix_out", _res_ln_fwd, [ya, yb, yc], p["wout"], x1, p["ln2_g"], p["ln2_b"],
                                      alpha, 1.0, parts_t=True)
    h2, a2, a2t = run("ffn2_up", _ffn_up_fwd, x2, p["wup2"])
    z3, x3, x3t = run("ffn2_down", _res_ln_fwd, [a2], p["wd2"], x2, p["ln3_g"], p["ln3_b"], alpha, 0.5)
    saved = dict(xt=xt, h1=h1, a1t=a1t, z1=z1, x1=x1, x1t=x1t, pc3=pc3, pq3=pq3, ps=ps, flog=flog, cum4=cum4,
                 lse4=lse4, yat=yat, ybt=ybt, yct=yct, z2=z2, x2t=x2t, h2=h2, a2t=a2t, z3=z3)
    return x3, x3t, saved


def _ffn_backward(sched, emit, which, dy, z, gamma, wd, wup, h, a_t, x_in_t, alpha, tag, after_mid=None):
    dz, df, dh, dgam, dbet = sched.carry(_ffn_bwd_mid, dy, z, gamma, wd, h, tag + "_bwd_mid")
    if after_mid is not None:
        after_mid(dgam, dbet)
    nq, d, w = wup.shape
    emit(which + "_w_up", sched.carry(_dw, x_in_t, dh, d, w, tag + "_dw_up")[0])
    half = wd.shape[0] // 2
    emit(which + "_w_down", sched.carry(_dw, a_t, df[None], half, d, tag + "_dw_down").reshape(nq, -1, d))
    dx = sched.carry(_ffn_bwd_dx, dh, wup, dz, alpha, tag + "_bwd_dx")
    return dx, dgam, dbet


def _backward_layer(sched, emit, emit_small, dy, sv, p, dims, alpha, l):
    b, s = dims["b"], dims["s"]
    tag = "l%d_" % l
    t, d = dy.shape
    g = {}
    dx2, g["ln3_g"], g["ln3_b"] = _ffn_backward(sched, emit, "ffn2", dy, sv["z3"], p["ln3_g"], p["wd2"], p["wup2"],
                                                sv["h2"], sv["a2t"], sv["x2t"], alpha, tag + "ffn2")
    wa, wb, wc = sv["yat"].shape[0], sv["ybt"].shape[0], sv["yct"].shape[0]
    dz2, dz2b, dya, dyb, dyc, g["ln2_g"], g["ln2_b"] = sched.carry(
        _out_bwd, dx2, sv["z2"], p["ln2_g"], p["wout"], (wa, wb, wc), tag + "mix_out_bwd")
    dz2b3 = dz2b[None]
    emit("mix_w_out", jnp.concatenate(
        [sched.carry(_dw, sv[k], dz2b3, sv[k].shape[0], d, tag + "dw_out_" + k)[0, 0] for k in ("yat", "ybt", "yct")],
        axis=0).reshape(N_SHARDS, -1, d))
    dpc3, g["conv_w"] = sched.carry(_conv_bwd, sv["pc3"], dya.reshape(b, s, -1), p["cw"], tag + "conv_bwd")
    dq3, dk3, dv3, dcum4 = sched.carry(_fox_bwd, sv["pq3"], sv["cum4"], sv["lse4"], dyb.reshape(b, s, -1),
                                       tag + "fox_bwd")
    nh = sv["flog"].shape[1]
    dflog, dbf, dwft = sched.carry(_cum_bwd, dcum4.reshape(b, nh, s), sv["flog"], sv["x1"].reshape(b, s, d),
                                   tag + "fox_gate_bwd")
    g["fox_b_f"] = dbf[:, 0]
    dps, g["sgu_w_s"], dbs, g["sgu_ln_g"], g["sgu_ln_b"] = sched.carry(
        _sgu_bwd, sv["ps"], dyc, p["sgu_g"], p["sgu_b"], p["ws"], p["bs"], tag + "sgu_bwd")
    g["sgu_b_s"] = dbs[:, :, 0]
    dpc = dpc3.reshape(t, -1)
    dq, dk, dv = dq3.reshape(t, -1), dk3.reshape(t, -1), dv3.reshape(t, -1)
    cols = [sched.carry(_dw, sv["x1t"], m[None], d, m.shape[1], tag + "dw_in_" + k)[0, 0]
            for k, m in (("conv", dpc), ("q", dq), ("k", dk), ("v", dv), ("sgu", dps))]
    w_in_grad = jnp.concatenate(cols[:4] + [dwft.T, cols[4]], axis=1)
    emit("mix_w_in", jnp.moveaxis(w_in_grad.reshape(d, N_SHARDS, -1), 1, 0))
    dx1 = sched.carry(_mix_bwd_dx, dz2, dpc, dq, dk, dv, dps, dflog, p["win"], p["wft"], s, alpha, tag + "mix_bwd_dx")

    def small_ready(dgam, dbet):
        g["ln1_g"], g["ln1_b"] = dgam, dbet
        emit_small(g)

    dx0, _, _ = _ffn_backward(sched, emit, "ffn1", dx1, sv["z1"], p["ln1_g"], p["wd1"], p["wup1"],
                              sv["h1"], sv["a1t"], sv["xt"], alpha, tag + "ffn1", after_mid=small_ready)
    return dx0


def _pack_rows(flat_list):
    v = jnp.concatenate(flat_list)
    n = v.shape[0]
    pad = (-n) % 1024
    return jnp.pad(v, (0, pad)).reshape(-1, 128)


def kernel(x, ln1_g, ln1_b, ffn1_w_up, ffn1_w_down, mix_w_in, fox_b_f, conv_w, sgu_ln_g, sgu_ln_b, sgu_w_s, sgu_b_s, mix_w_out, ln2_g, ln2_b, ffn2_w_up, ffn2_w_down, ln3_g, ln3_b, loss_target, m_ln1_g, m_ln1_b, m_ffn1_w_up, m_ffn1_w_down, m_mix_w_in, m_fox_b_f, m_conv_w, m_sgu_ln_g, m_sgu_ln_b, m_sgu_w_s, m_sgu_b_s, m_mix_w_out, m_ln2_g, m_ln2_b, m_ffn2_w_up, m_ffn2_w_down, m_ln3_g, m_ln3_b, v_ln1_g, v_ln1_b, v_ffn1_w_up, v_ffn1_w_down, v_mix_w_in, v_fox_b_f, v_conv_w, v_sgu_ln_g, v_sgu_ln_b, v_sgu_w_s, v_sgu_b_s, v_mix_w_out, v_ln2_g, v_ln2_b, v_ffn2_w_up, v_ffn2_w_down, v_ln3_g, v_ln3_b):
    wts = dict(ln1_g=ln1_g, ln1_b=ln1_b, ffn1_w_up=ffn1_w_up, ffn1_w_down=ffn1_w_down, mix_w_in=mix_w_in,
               fox_b_f=fox_b_f, conv_w=conv_w, sgu_ln_g=sgu_ln_g, sgu_ln_b=sgu_ln_b, sgu_w_s=sgu_w_s,
               sgu_b_s=sgu_b_s, mix_w_out=mix_w_out, ln2_g=ln2_g, ln2_b=ln2_b, ffn2_w_up=ffn2_w_up,
               ffn2_w_down=ffn2_w_down, ln3_g=ln3_g, ln3_b=ln3_b)
    mom = dict(ln1_g=m_ln1_g, ln1_b=m_ln1_b, ffn1_w_up=m_ffn1_w_up, ffn1_w_down=m_ffn1_w_down, mix_w_in=m_mix_w_in,
               fox_b_f=m_fox_b_f, conv_w=m_conv_w, sgu_ln_g=m_sgu_ln_g, sgu_ln_b=m_sgu_ln_b, sgu_w_s=m_sgu_w_s,
               sgu_b_s=m_sgu_b_s, mix_w_out=m_mix_w_out, ln2_g=m_ln2_g, ln2_b=m_ln2_b, ffn2_w_up=m_ffn2_w_up,
               ffn2_w_down=m_ffn2_w_down, ln3_g=m_ln3_g, ln3_b=m_ln3_b)
    var = dict(ln1_g=v_ln1_g, ln1_b=v_ln1_b, ffn1_w_up=v_ffn1_w_up, ffn1_w_down=v_ffn1_w_down, mix_w_in=v_mix_w_in,
               fox_b_f=v_fox_b_f, conv_w=v_conv_w, sgu_ln_g=v_sgu_ln_g, sgu_ln_b=v_sgu_ln_b, sgu_w_s=v_sgu_w_s,
               sgu_b_s=v_sgu_b_s, mix_w_out=v_mix_w_out, ln2_g=v_ln2_g, ln2_b=v_ln2_b, ffn2_w_up=v_ffn2_w_up,
               ffn2_w_down=v_ffn2_w_down, ln3_g=v_ln3_g, ln3_b=v_ln3_b)

    nl = ln1_g.shape[0]
    b, s, d = x.shape
    t = b * s
    alpha = (2 * nl) ** 0.25
    cw_sh = conv_w.shape[2]
    d_conv = cw_sh * N_SHARDS
    d_sgu = sgu_ln_g.shape[1]
    nh = fox_b_f.shape[1]
    d_fox = nh * FOX_HEAD_DIM
    n_main = 3 * d_conv + 3 * d_fox
    dims = dict(b=b, s=s, proj_widths=(3 * d_conv, 3 * d_fox, 2 * d_sgu))
    cpos = lax.axis_index("c").astype(jnp.int32)
    qpos = (2 * lax.axis_index("x") + lax.axis_index("y")).astype(jnp.int32)
    cq = jnp.stack([cpos, qpos])

    me = (2 * qpos + cpos).reshape(1)
    assert nl == 2, "the gather schedule below names the carriers of a two-layer step"

    conv_tile = jnp.pad(conv_w, ((0, 0), (0, 8 - conv_w.shape[1]), (0, 128 - cw_sh)))
    params = [dict(bf=fox_b_f[l].reshape(nh, 1), sgu_g=_row(sgu_ln_g[l]), sgu_b=_row(sgu_ln_b[l]), ws=sgu_w_s[l],
                   bs=sgu_b_s[l][:, :, None], ln1_g=_row(ln1_g[l]), ln1_b=_row(ln1_b[l]), ln2_g=_row(ln2_g[l]),
                   ln2_b=_row(ln2_b[l]), ln3_g=_row(ln3_g[l]), ln3_b=_row(ln3_b[l])) for l in range(nl)]

    def operands_of(k, arr):
        if k == "mix_w_in":
            w_in = jnp.moveaxis(arr, 0, 1).reshape(d, -1)
            return dict(win=jnp.concatenate([w_in[:, :n_main], w_in[:, n_main + nh:]], axis=1),
                        wft=jnp.pad(w_in[:, n_main:n_main + nh].T, ((0, HEAD_ROWS - nh), (0, 0))))
        if k == "conv_w":
            return dict(cw=jnp.moveaxis(arr[:, :3, :cw_sh], 0, 1).reshape(3, d_conv))
        if k in ("ffn1_w_up", "ffn2_w_up"):
            return {"wup" + k[3]: arr}
        return {dict(ffn1_w_down="wd1", ffn2_w_down="wd2", mix_w_out="wout")[k]: arr.reshape(-1, d)}

    def gather(l, keys):
        side = _side_gather([conv_tile[l] if k == "conv_w" else wts[k][l].astype(BF) for k in keys],
                            [k != "conv_w" for k in keys])

        def install():
            for k, arr in zip(keys, side.results):
                params[l].update(operands_of(k, arr))
        return side, install

    first, install_first = gather(0, ["ffn1_w_up"])
    _run_side(first, "gather_first")
    install_first()
    rides = [{"ffn1_up": gather(0, ["ffn1_w_down", "mix_w_in", "mix_w_out", "conv_w"]),
              "ffn1_down": gather(0, ["ffn2_w_up"]),
              "mix_proj": gather(0, ["ffn2_w_down"]),
              "fox": gather(1, ["ffn1_w_up", "ffn1_w_down", "mix_w_in", "mix_w_out", "conv_w"]),
              "ffn2_up": gather(1, ["ffn2_w_up", "ffn2_w_down"])}, {}]

    act, act_t = x.reshape(t, d), None
    saved = []
    for l in range(nl):
        act, act_t, sv = _forward_layer(act, act_t, params[l], dims, alpha, l, rides[l])
        saved.append(sv)
    dy, loss_blk = _loss_fwd(act, loss_target.reshape(t, d), "loss")

    sched = _Sched()
    prev = {k: None for k in BIG}
    red = {}

    def emit_for(l):
        def emit(key, g):
            st = {}
            name = "l%d_%s" % (l, key)

            def pair_sum(res, side):
                st["p"] = _pair_sum(g, res[0], cq, "rs_pair_sum_" + name, side=side)

            def chip_sum(res, side):
                st["t"] = _chip_sum(st["p"], res[0], cq, "rs_chip_sum_" + name, side=side)

            def adamw(res, side):
                prev[key] = _adamw_shard(wts[key], mom[key], var[key], st["t"], res[0], cq, l, prev[key],
                                         "adamw_" + name, side=side)

            sched.add([("comm", lambda: _side_pair_send([g])), ("compute", pair_sum),
                       ("comm", lambda: _side_scatter([st["p"]])), ("compute", chip_sum),
                       ("comm", lambda: _side_pair_share([st["t"]])), ("compute", adamw)])
        return emit

    def emit_small_for(l):
        def emit_small(g):
            flat = [g[k].reshape(-1) for k in SMALL] + [g["conv_w"].reshape(-1)]
            if l == nl - 1:
                flat.append(loss_blk[0, 0:1])
            vec = _pack_rows(flat)

            def slot_sum(res, side):
                red[l] = _sum_slots(vec, res[0], me, "small_sum_l%d" % l, side=side)

            sched.add([("comm", lambda: _side_bcast(vec)), ("compute", slot_sum)])
        return emit_small

    for l in reversed(range(nl)):
        dy = _backward_layer(sched, emit_for(l), emit_small_for(l), dy, saved[l], params[l], dims, alpha, l)
    sched.drain()
    grad_x = dy.reshape(b, s, d)
    gfin, delta, new_m, new_v = {}, {}, {}, {}
    for k in BIG:
        gfin[k], delta[k], new_m[k], new_v[k] = prev[k]

    gsm = {k: [] for k in SMALL + ("conv_w",)}
    for l in range(nl):
        flat_l = red[l].reshape(-1)
        off = 0
        for k in SMALL:
            n = wts[k][l].size
            gsm[k].append(flat_l[off:off + n].reshape(wts[k][l].shape))
            off += n
        n = 3 * d_conv
        gsm["conv_w"].append(lax.dynamic_slice_in_dim(flat_l[off:off + n].reshape(3, d_conv), qpos * cw_sh, cw_sh,
                                                      axis=1))
        off += n
        if l == nl - 1:
            loss = flat_l[off]
    for k in gsm:
        gfin[k] = jnp.stack(gsm[k])
    small_keys = SMALL + ("conv_w",)
    sizes = [wts[k].size for k in small_keys]
    pk = lambda src: _pack_rows([src[k].reshape(-1) for k in small_keys])
    dl, nm, nv = _adamw_small(pk(wts), pk(gfin), pk(mom), pk(var), "adamw_small")
    off = 0
    for k, n in zip(small_keys, sizes):
        shp = wts[k].shape
        delta[k] = dl.reshape(-1)[off:off + n].reshape(shp)
        new_m[k] = nm.reshape(-1)[off:off + n].reshape(shp)
        new_v[k] = nv.reshape(-1)[off:off + n].reshape(shp)
        off += n

    return (loss, grad_x, *[gfin[k] for k in ORDER], *[delta[k] for k in ORDER],
            *[new_m[k] for k in ORDER], *[new_v[k] for k in ORDER])
```

```python
import jax
import jax.numpy as jnp
from jax import lax
from jax.experimental import pallas as pl
from jax.experimental.pallas import tpu as pltpu

F32 = jnp.float32
BF = jnp.bfloat16
SDS = jax.ShapeDtypeStruct
MESH = pl.DeviceIdType.MESH

LN_EPS = 1e-5
FOX_HEAD_DIM = 64
FOX_Q_BLOCK = 512
DW_TOKENS = 2048
HEAD_ROWS = 128
GELU_K = 0.7978845608028654
GELU_C = 0.044715
NEG_BIG = -1e30
N_SHARDS = 4

ADAM_LR = 0.001
ADAM_B1 = 0.9
ADAM_B2 = 0.999
ADAM_EPS = 1e-08
ADAM_WD = 0.01
ADAM_STEP = 10

VMEM_LIMIT_BYTES = 56 * 1024 * 1024
NT = (((1,), (1,)), ((), ()))
TN = (((0,), (0,)), ((), ()))
HBM = pl.BlockSpec(memory_space=pl.ANY)


def _tile(n, pref, mult=8):
    t = min(n, pref)
    while n % t or t % mult:
        t -= mult
    return t


def _dot(a, b):
    return jnp.dot(a, b, preferred_element_type=F32)


def _dotg(a, b, dims):
    return lax.dot_general(a, b, dims, preferred_element_type=F32)


def _sigmoid(x):
    return 1.0 / (1.0 + jnp.exp(-x))


def _gelu(x):
    return 0.5 * x * (1.0 + jnp.tanh(GELU_K * (x + GELU_C * x * x * x)))


def _gelu_grad(x):
    t = jnp.tanh(GELU_K * (x + GELU_C * x * x * x))
    return 0.5 * (1.0 + t) + 0.5 * x * (1.0 - t * t) * GELU_K * (1.0 + 3.0 * GELU_C * x * x)


def _ln_stats(z):
    mu = jnp.mean(z, axis=-1, keepdims=True)
    zc = z - mu
    var = jnp.mean(zc * zc, axis=-1, keepdims=True)
    rstd = lax.rsqrt(var + LN_EPS)
    return zc * rstd, rstd


def _ln_bwd(dy, z, g):
    xhat, rstd = _ln_stats(z)
    gdy = dy * g
    m1 = jnp.mean(gdy, axis=-1, keepdims=True)
    m2 = jnp.mean(gdy * xhat, axis=-1, keepdims=True)
    dz = rstd * (gdy - m1 - xhat * m2)
    return dz, jnp.sum(dy * xhat, axis=0, keepdims=True), jnp.sum(dy, axis=0, keepdims=True)


class _Side:
    def __init__(self, ins, out_shapes, sems, start, finish):
        self.ins, self.out_shapes, self.sems = list(ins), list(out_shapes), list(sems)
        self.start, self.finish = start, finish
        self.results = None


def _join(sides):
    sides = [s for s in sides if s is not None]
    if not sides:
        return None
    ins = [a for s in sides for a in s.ins]
    outs = [a for s in sides for a in s.out_shapes]
    sems = [a for s in sides for a in s.sems]

    def parts(seq, field):
        out, o = [], 0
        for s in sides:
            n = len(getattr(s, field))
            out.append(seq[o:o + n])
            o += n
        return out

    def run(which):
        def fn(i, o, m):
            for s, a, b, c in zip(sides, parts(i, "ins"), parts(o, "out_shapes"), parts(m, "sems")):
                getattr(s, which)(a, b, c)
        return fn

    joined = _Side(ins, outs, sems, run("start"), run("finish"))
    joined.members = sides
    return joined


def _deliver(side, results):
    members = getattr(side, "members", None)
    side.results = list(results)
    if members:
        o = 0
        for s in members:
            n = len(s.out_shapes)
            _deliver(s, results[o:o + n])
            o += n


def _pcall(body, operands, *, name, grid, in_specs, out_specs, out_shape, sem, scratch_shapes=(),
           prefetch=(), aliases=None, side=None):
    single = not isinstance(out_shape, (list, tuple))
    out_shape = [out_shape] if single else list(out_shape)
    out_specs = [out_specs] if single else list(out_specs)
    in_specs, scratch_shapes = list(in_specs), list(scratch_shapes)
    n_pre, n_in, n_out, n_sc = len(prefetch), len(in_specs), len(out_shape), len(scratch_shapes)
    fn = body
    extra = []
    if side is not None:
        s_in, s_out = len(side.ins), len(side.out_shapes)

        def fn(*refs):
            pre, rest = refs[:n_pre], refs[n_pre:]
            m_in, c_in = rest[:n_in], rest[n_in:n_in + s_in]
            rest = rest[n_in + s_in:]
            m_out, c_out = rest[:n_out], rest[n_out:n_out + s_out]
            rest = rest[n_out + s_out:]
            m_sc, c_sc = rest[:n_sc], rest[n_sc:]
            first = pl.program_id(0) == 0
            last = pl.program_id(0) == grid[0] - 1
            for a in range(1, len(grid)):
                first = jnp.logical_and(first, pl.program_id(a) == 0)
                last = jnp.logical_and(last, pl.program_id(a) == grid[a] - 1)

            @pl.when(first)
            def _():
                side.start(c_in, c_out, c_sc)

            body(*pre, *m_in, *m_out, *m_sc)

            @pl.when(last)
            def _():
                side.finish(c_in, c_out, c_sc)

        in_specs = in_specs + [HBM] * s_in
        out_specs = out_specs + [HBM] * s_out
        out_shape = out_shape + side.out_shapes
        scratch_shapes = scratch_shapes + side.sems
        extra = side.ins
        sem = ("arbitrary",) * len(grid)
    params = pltpu.CompilerParams(dimension_semantics=tuple(sem), vmem_limit_bytes=VMEM_LIMIT_BYTES)
    kw = dict(input_output_aliases=aliases) if aliases else {}
    if n_pre:
        spec = pltpu.PrefetchScalarGridSpec(num_scalar_prefetch=n_pre, grid=grid, in_specs=in_specs,
                                            out_specs=out_specs, scratch_shapes=scratch_shapes)
        call = pl.pallas_call(fn, name=name, grid_spec=spec, out_shape=out_shape, compiler_params=params, **kw)
    else:
        call = pl.pallas_call(fn, name=name, grid=grid, in_specs=in_specs, out_specs=out_specs,
                              out_shape=out_shape, scratch_shapes=scratch_shapes, compiler_params=params, **kw)
    res = call(*prefetch, *operands, *extra)
    if side is not None:
        _deliver(side, res[n_out:])
        res = res[:n_out]
    return res[0] if single else res


def _run_side(side, name):
    def body(*refs):
        n_in, n_out = len(side.ins), len(side.out_shapes)
        i, o, m = refs[:n_in], refs[n_in:n_in + n_out], refs[n_in + n_out:]
        side.start(i, o, m)
        side.finish(i, o, m)

    res = pl.pallas_call(body, name=name, in_specs=[HBM] * len(side.ins), out_specs=[HBM] * len(side.out_shapes),
                         out_shape=side.out_shapes, scratch_shapes=side.sems)(*side.ins)
    _deliver(side, res)


def _mesh_pos():
    x, y, c = lax.axis_index("x"), lax.axis_index("y"), lax.axis_index("c")
    chips = [(1 - x, y), (x, 1 - y), (1 - x, 1 - y)]
    return x, y, c, chips


def _rows(ref, lead, half, n_rows):
    return ref.at[tuple(lead) + (pl.ds(half * n_rows, n_rows),)]


def _side_gather(shards, split):
    n = len(shards)
    hs = [w.shape[0] // 2 for w in shards]

    def plan(ins, outs, sems):
        ssem, rsem = sems
        x, y, c, chips = _mesh_pos()
        q = 2 * x + y
        sib = (x, y, 1 - c)

        def rc(p, k, src, dst, to):
            return pltpu.make_async_remote_copy(src_ref=src, dst_ref=dst, send_sem=ssem.at[p, k],
                                                recv_sem=rsem.at[p, k], device_id=to, device_id_type=MESH)

        def blk(ref, p, qi, half):
            return _rows(ref, (qi,), half, hs[p]) if split[p] else ref.at[qi]

        return x, y, c, chips, q, sib, rc, blk

    def first_sends(ins, outs, sems):
        x, y, c, chips, q, sib, rc, blk = plan(ins, outs, sems)
        cps = [rc(p, 0, ins[p], outs[p].at[q], sib) for p in range(n)]
        for j, (cx, cy) in enumerate(chips):
            for p in range(n):
                src = _rows(ins[p], (), c, hs[p]) if split[p] else ins[p]
                cps.append(rc(p, 1 + j, src, blk(outs[p], p, q, c), (cx, cy, c)))
        return cps

    def start(ins, outs, sems):
        for cp in first_sends(ins, outs, sems):
            cp.start()

    def finish(ins, outs, sems):
        x, y, c, chips, q, sib, rc, blk = plan(ins, outs, sems)
        sent = first_sends(ins, outs, sems)
        for j, (cx, cy) in enumerate(chips):
            qj = 2 * cx + cy
            for p in range(n):
                got = blk(outs[p], p, qj, c)
                rc(p, 1 + j, got, got, (cx, cy, c)).wait_recv()
                if split[p]:
                    fwd = rc(p, 4 + j, got, got, sib)
                    fwd.start()
                    sent.append(fwd)
        for j, (cx, cy) in enumerate(chips):
            qj = 2 * cx + cy
            for p in range(n):
                if split[p]:
                    got = blk(outs[p], p, qj, 1 - c)
                    rc(p, 4 + j, got, got, sib).wait_recv()
        for p in range(n):
            rc(p, 0, outs[p].at[q], outs[p].at[q], sib).wait_recv()
        for cp in sent:
            cp.wait_send()

    return _Side(shards, [SDS((N_SHARDS,) + w.shape, w.dtype) for w in shards],
                 [pltpu.SemaphoreType.DMA((n, 7)), pltpu.SemaphoreType.DMA((n, 7))], start, finish)


def _side_pair_send(gs):
    n = len(gs)

    def copies(ins, outs, sems):
        x, y, c, _ = _mesh_pos()
        return [pltpu.make_async_remote_copy(
            src_ref=ins[p].at[:, pl.ds((1 - c) * (gs[p].shape[1] // 2), gs[p].shape[1] // 2)], dst_ref=outs[p],
            send_sem=sems[0].at[p], recv_sem=sems[1].at[p], device_id=(x, y, 1 - c), device_id_type=MESH)
            for p in range(n)]

    def start(ins, outs, sems):
        for cp in copies(ins, outs, sems):
            cp.start()

    def finish(ins, outs, sems):
        for cp in copies(ins, outs, sems):
            cp.wait()

    return _Side(gs, [SDS((g.shape[0], g.shape[1] // 2, g.shape[2]), g.dtype) for g in gs],
                 [pltpu.SemaphoreType.DMA((n,)), pltpu.SemaphoreType.DMA((n,))], start, finish)


def _side_scatter(ps):
    n = len(ps)

    def sends(ins, outs, sems):
        x, y, c, chips = _mesh_pos()
        q = 2 * x + y
        return [pltpu.make_async_remote_copy(src_ref=ins[p].at[2 * cx + cy], dst_ref=outs[p].at[q],
                                             send_sem=sems[0].at[p, j], recv_sem=sems[1].at[p, j],
                                             device_id=(cx, cy, c), device_id_type=MESH)
                for j, (cx, cy) in enumerate(chips) for p in range(n)]

    def start(ins, outs, sems):
        for cp in sends(ins, outs, sems):
            cp.start()

    def finish(ins, outs, sems):
        x, y, c, chips = _mesh_pos()
        for j, (cx, cy) in enumerate(chips):
            for p in range(n):
                got = outs[p].at[2 * cx + cy]
                pltpu.make_async_remote_copy(src_ref=got, dst_ref=got, send_sem=sems[0].at[p, j],
                                             recv_sem=sems[1].at[p, j], device_id=(cx, cy, c),
                                             device_id_type=MESH).wait_recv()
        for cp in sends(ins, outs, sems):
            cp.wait_send()

    return _Side(ps, [SDS(p.shape, p.dtype) for p in ps],
                 [pltpu.SemaphoreType.DMA((n, 3)), pltpu.SemaphoreType.DMA((n, 3))], start, finish)


def _side_pair_share(tots):
    n = len(tots)

    def copies(ins, outs, sems):
        x, y, c, _ = _mesh_pos()
        return [pltpu.make_async_remote_copy(src_ref=ins[p], dst_ref=outs[p], send_sem=sems[0].at[p],
                                             recv_sem=sems[1].at[p], device_id=(x, y, 1 - c), device_id_type=MESH)
                for p in range(n)]

    def start(ins, outs, sems):
        for cp in copies(ins, outs, sems):
            cp.start()

    def finish(ins, outs, sems):
        for cp in copies(ins, outs, sems):
            cp.wait()

    return _Side(tots, [SDS(t_.shape, t_.dtype) for t_ in tots],
                 [pltpu.SemaphoreType.DMA((n,)), pltpu.SemaphoreType.DMA((n,))], start, finish)


N_DEVICES = 8


def _side_bcast(v):
    def peers():
        x, y, c, _ = _mesh_pos()
        out = []
        for k in range(1, N_DEVICES):
            px, py, pc = x ^ ((k >> 2) & 1), y ^ ((k >> 1) & 1), c ^ (k & 1)
            out.append((k - 1, (px, py, pc), 4 * px + 2 * py + pc))
        return 4 * x + 2 * y + c, out

    def sends(ins, outs, sems):
        me, ps = peers()
        return [pltpu.make_async_remote_copy(src_ref=ins[0], dst_ref=outs[0].at[me], send_sem=sems[0].at[k],
                                             recv_sem=sems[1].at[k], device_id=to, device_id_type=MESH)
                for k, to, _ in ps]

    def start(ins, outs, sems):
        for cp in sends(ins, outs, sems):
            cp.start()

    def finish(ins, outs, sems):
        _, ps = peers()
        for k, to, slot in ps:
            got = outs[0].at[slot]
            pltpu.make_async_remote_copy(src_ref=got, dst_ref=got, send_sem=sems[0].at[k], recv_sem=sems[1].at[k],
                                         device_id=to, device_id_type=MESH).wait_recv()
        for cp in sends(ins, outs, sems):
            cp.wait_send()

    return _Side([v], [SDS((N_DEVICES,) + v.shape, v.dtype)],
                 [pltpu.SemaphoreType.DMA((N_DEVICES - 1,)), pltpu.SemaphoreType.DMA((N_DEVICES - 1,))], start, finish)


def _sum_slots(v, r, me, name, side=None):
    n, rows, lanes = r.shape
    tr = _tile(rows, 512)

    def body(me_ref, v_ref, r_ref, o_ref):
        j = pl.program_id(1)
        term = jnp.where(j == me_ref[0], v_ref[...], r_ref[0])

        @pl.when(j == 0)
        def _():
            o_ref[...] = term

        @pl.when(j != 0)
        def _():
            o_ref[...] += term

    other = lambda j, k: jnp.where(j == k, (k + 1) % n, j)
    return _pcall(
        body, (v, r), name=name, grid=(rows // tr, n), prefetch=(me,),
        in_specs=[pl.BlockSpec((tr, lanes), lambda i, j, me_ref: (i, 0)),
                  pl.BlockSpec((1, tr, lanes), lambda i, j, me_ref: (other(j, me_ref[0]), i, 0))],
        out_specs=pl.BlockSpec((tr, lanes), lambda i, j, me_ref: (i, 0)),
        out_shape=SDS((rows, lanes), F32), sem=("parallel", "arbitrary"), side=side)


def _pair_sum(g, r1, cq, name, side=None):
    nq, xr, yc = g.shape
    h = xr // 2
    tr = _tile(h, 512, 16)
    nt = h // tr

    def body(cq_ref, g_ref, r_ref, o_ref):
        o_ref[...] = (g_ref[...] + r_ref[...]).astype(BF)

    return _pcall(
        body, (g, r1), name=name, grid=(nq, nt), prefetch=(cq,),
        in_specs=[pl.BlockSpec((1, tr, yc), lambda j, i, cq_ref: (j, cq_ref[0] * nt + i, 0)),
                  pl.BlockSpec((1, tr, yc), lambda j, i, cq_ref: (j, i, 0))],
        out_specs=pl.BlockSpec((1, tr, yc), lambda j, i, cq_ref: (j, i, 0)),
        out_shape=SDS((nq, h, yc), BF), sem=("parallel", "parallel"), side=side)


def _chip_sum(p, r2, cq, name, side=None):
    nq, h, yc = r2.shape
    tr = _tile(h, 512, 16)

    def body(cq_ref, p_ref, r_ref, o_ref):
        j = pl.program_id(1)
        term = jnp.where(j == cq_ref[1], p_ref[0], r_ref[0]).astype(F32)

        @pl.when(j == 0)
        def _():
            o_ref[...] = term

        @pl.when(j != 0)
        def _():
            o_ref[...] += term

    other = lambda j, q: jnp.where(j == q, (q + 1) % nq, j)
    return _pcall(
        body, (p, r2), name=name, grid=(h // tr, nq), prefetch=(cq,),
        in_specs=[pl.BlockSpec((1, tr, yc), lambda i, j, cq_ref: (cq_ref[1], i, 0)),
                  pl.BlockSpec((1, tr, yc), lambda i, j, cq_ref: (other(j, cq_ref[1]), i, 0))],
        out_specs=pl.BlockSpec((tr, yc), lambda i, j, cq_ref: (i, 0)),
        out_shape=SDS((h, yc), F32), sem=("parallel", "arbitrary"), side=side)


def _ffn_up_fwd(x, wup, name, side=None, emit_xt=False):
    t, d = x.shape
    w = wup.shape[2]
    tm = _tile(t, 512)

    def body(x_ref, wg_ref, wu_ref, h_ref, a_ref, at_ref, *xt_ref):
        xb = x_ref[...].astype(BF)
        g = _dot(xb, wg_ref[0])
        u = _dot(xb, wu_ref[0])
        h_ref[0] = g.astype(BF)
        h_ref[1] = u.astype(BF)
        ab = (g * _sigmoid(g) * u).astype(BF)
        a_ref[...] = ab
        at_ref[...] = ab.T
        if emit_xt:
            @pl.when(pl.program_id(0) == 0)
            def _():
                xt_ref[0][...] = xb.T

    nt = t // tm
    out_specs = [pl.BlockSpec((2, tm, w), lambda j, i: (0, i, j)), pl.BlockSpec((tm, w), lambda j, i: (i, j)),
                 pl.BlockSpec((w, tm), lambda j, i: (j, i))]
    out_shape = [SDS((2, t, 2 * w), BF), SDS((t, 2 * w), BF), SDS((2 * w, t), BF)]
    if emit_xt:
        out_specs.append(pl.BlockSpec((d, tm), lambda j, i: (0, jnp.where(j == 0, i, nt - 1))))
        out_shape.append(SDS((d, t), BF))
    return _pcall(
        body, (x, wup, wup), name=name, grid=(2, nt),
        in_specs=[pl.BlockSpec((tm, d), lambda j, i: (i, 0)),
                  pl.BlockSpec((1, d, w), lambda j, i: (j, 0, 0)),
                  pl.BlockSpec((1, d, w), lambda j, i: (j + 2, 0, 0))],
        out_specs=out_specs, out_shape=out_shape,
        sem=("arbitrary", "arbitrary") if emit_xt else ("parallel", "parallel"), side=side)


def _res_ln_fwd(parts, w, x, gamma, beta, alpha, res_scale, name, side=None, parts_t=False, target=None):
    t, d = x.shape
    n = len(parts)
    offs = [0]
    for p in parts:
        offs.append(offs[-1] + p.shape[1])
    tm = _tile(t, 512)
    n_in = n + 4 + (target is not None)

    def body(*refs):
        p_refs = refs[:n]
        w_ref, x_ref, g_ref, b_ref = refs[n:n + 4]
        out = refs[n_in:]
        f = _dot(p_refs[0][...], w_ref[offs[0]:offs[1], :])
        for k in range(1, n):
            f = f + _dot(p_refs[k][...], w_ref[offs[k]:offs[k + 1], :])
        z = alpha * x_ref[...] + res_scale * f
        out[0][...] = z
        xhat, _ = _ln_stats(z)
        y = xhat * g_ref[...] + b_ref[...]
        if target is not None:
            err = y - refs[n + 4][...]
            out[1][...] = err * (1.0 / d)
            part = 0.5 * jnp.sum(jnp.sum(err * err, axis=-1, keepdims=True) * (1.0 / d), axis=0, keepdims=True)
            _acc_rows(out[2], jnp.broadcast_to(part, (8, 128)), pl.program_id(0) == 0)
            return
        out[1][...] = y
        out[2][...] = y.astype(BF).T
        for k in range(len(out) - 3):
            out[3 + k][...] = p_refs[k][...].T

    row = lambda i: (i, 0)
    col = lambda i: (0, i)
    fixed = lambda i: (0, 0)
    operands = [*parts, w, x, gamma, beta]
    in_specs = [pl.BlockSpec((tm, p.shape[1]), row) for p in parts] + [
        pl.BlockSpec(w.shape, fixed), pl.BlockSpec((tm, d), row), pl.BlockSpec((1, d), fixed), pl.BlockSpec((1, d), fixed)]
    if target is not None:
        operands.append(target)
        in_specs.append(pl.BlockSpec((tm, d), row))
        out_specs = [pl.BlockSpec((tm, d), row), pl.BlockSpec((tm, d), row), pl.BlockSpec((8, 128), fixed)]
        out_shape = [SDS((t, d), F32), SDS((t, d), F32), SDS((8, 128), F32)]
    else:
        out_specs = [pl.BlockSpec((tm, d), row), pl.BlockSpec((tm, d), row), pl.BlockSpec((d, tm), col)]
        out_shape = [SDS((t, d), F32), SDS((t, d), F32), SDS((d, t), BF)]
        if parts_t:
            out_specs += [pl.BlockSpec((p.shape[1], tm), col) for p in parts]
            out_shape += [SDS((p.shape[1], t), BF) for p in parts]
    return _pcall(
        body, operands, name=name, grid=(t // tm,), in_specs=in_specs, out_specs=out_specs, out_shape=out_shape,
        sem=("arbitrary",) if target is not None else ("parallel",), side=side)


def _mix_proj_fwd(x, w_main, widths, name, side=None):
    t, d = x.shape
    dc, dq, ds = widths
    tm = _tile(t, 512)

    def body(x_ref, w_ref, pc_ref, pq_ref, ps_ref):
        xb = x_ref[...].astype(BF)
        pc_ref[...] = _dot(xb, w_ref[:, 0:dc])
        pq_ref[...] = _dot(xb, w_ref[:, dc:dc + dq]).astype(BF)
        ps_ref[...] = _dot(xb, w_ref[:, dc + dq:dc + dq + ds])

    row = lambda i: (i, 0)
    return _pcall(
        body, (x, w_main), name=name, grid=(t // tm,),
        in_specs=[pl.BlockSpec((tm, d), row), pl.BlockSpec(w_main.shape, lambda i: (0, 0))],
        out_specs=[pl.BlockSpec((tm, dc), row), pl.BlockSpec((tm, dq), row), pl.BlockSpec((tm, ds), row)],
        out_shape=[SDS((t, dc), F32), SDS((t, dq), BF), SDS((t, ds), F32)],
        sem=("parallel",), side=side)


def _prefix_sum_lanes(v, reverse):
    n = v.shape[-1]
    lane = lax.broadcasted_iota(jnp.int32, v.shape, v.ndim - 1)
    sh = 1
    while sh < n:
        if reverse:
            v = v + jnp.where(lane < n - sh, pltpu.roll(v, n - sh, axis=v.ndim - 1), 0.0)
        else:
            v = v + jnp.where(lane >= sh, pltpu.roll(v, sh, axis=v.ndim - 1), 0.0)
        sh *= 2
    return v


def _cum_fwd(x3, wft, bf, name, side=None):
    b, s, d = x3.shape
    h = bf.shape[0]

    def body(x_ref, w_ref, b_ref, fl_ref, cum_ref):
        fl = _dotg(w_ref[...], x_ref[0].astype(BF), NT)[0:h] + b_ref[...]
        fl_ref[0] = fl
        lf = jnp.minimum(fl, 0.0) - jnp.log(1.0 + jnp.exp(-jnp.abs(fl)))
        cum_ref[0] = _prefix_sum_lanes(lf, reverse=False)

    return _pcall(
        body, (x3, wft, bf), name=name, grid=(b,),
        in_specs=[pl.BlockSpec((1, s, d), lambda i: (i, 0, 0)),
                  pl.BlockSpec(wft.shape, lambda i: (0, 0)), pl.BlockSpec((h, 1), lambda i: (0, 0))],
        out_specs=[pl.BlockSpec((1, h, s), lambda i: (i, 0, 0)), pl.BlockSpec((1, h, s), lambda i: (i, 0, 0))],
        out_shape=[SDS((b, h, s), F32), SDS((b, h, s), F32)],
        sem=("parallel",), side=side)


def _shift_rows(z, k, down):
    n = z.shape[0]
    row = lax.broadcasted_iota(jnp.int32, z.shape, 0)
    if down:
        return jnp.where(row >= k, pltpu.roll(z, k, axis=0), 0.0)
    return jnp.where(row < n - k, pltpu.roll(z, n - k, axis=0), 0.0)


def _conv_fwd(pc3, cw, name, side=None):
    b, s, c3 = pc3.shape
    c = c3 // 3

    def body(p_ref, w_ref, y_ref):
        z = p_ref[0, :, c:2 * c] * p_ref[0, :, 2 * c:3 * c]
        conv = w_ref[0:1, :] * _shift_rows(z, 2, True) + w_ref[1:2, :] * _shift_rows(z, 1, True) + w_ref[2:3, :] * z
        y_ref[0] = (p_ref[0, :, 0:c] * conv).astype(BF)

    return _pcall(
        body, (pc3, cw), name=name, grid=(b,),
        in_specs=[pl.BlockSpec((1, s, c3), lambda i: (i, 0, 0)), pl.BlockSpec((3, c), lambda i: (0, 0))],
        out_specs=pl.BlockSpec((1, s, c), lambda i: (i, 0, 0)),
        out_shape=SDS((b, s, c), BF), sem=("parallel",), side=side)


def _head_masks(width):
    lane = lax.broadcasted_iota(jnp.int32, (1, width), 1)
    return [lane < FOX_HEAD_DIM, lane >= FOX_HEAD_DIM]


def _fox_scores(q, k, cum_row, lo, head_mask):
    tq = q.shape[0]
    qm = jnp.where(head_mask, q * (FOX_HEAD_DIM ** -0.5), 0)
    s = _dotg(qm, k, NT) - cum_row
    tri = lax.broadcasted_iota(jnp.int32, (tq, tq), 1) <= lax.broadcasted_iota(jnp.int32, (tq, tq), 0)
    diag = jnp.where(tri, s[:, lo:], NEG_BIG)
    return (diag if lo == 0 else jnp.concatenate([s[:, :lo], diag], axis=1)), qm


def _fox_fwd(pq3, cum4, name, side=None):
    b, s, d3 = pq3.shape
    df = d3 // 3
    hp = df // 128
    tq = _tile(s, FOX_Q_BLOCK)

    def body(q_ref, k_ref, v_ref, c_ref, o_ref, lse_ref):
        masks = _head_masks(128)
        for i in range(s // tq):
            lo, hi = i * tq, (i + 1) * tq
            q = q_ref[0, lo:hi, :]
            k = k_ref[0, 0:hi, :]
            v = v_ref[0, 0:hi, :]
            o = jnp.zeros((tq, 128), F32)
            lse = jnp.zeros((tq, 128), F32)
            for e in range(2):
                sc, _ = _fox_scores(q, k, c_ref[0, 0, e:e + 1, 0:hi], lo, masks[e])
                m = jnp.max(sc, axis=-1, keepdims=True)
                p = jnp.exp(sc - m)
                l = jnp.sum(p, axis=-1, keepdims=True)
                o = jnp.where(masks[e], _dot(p.astype(BF), v) * (1.0 / l), o)
                lse = jnp.where(masks[e], m + jnp.log(l), lse)
            o_ref[0, lo:hi, :] = o.astype(BF)
            lse_ref[0, 0, lo:hi, :] = lse

    blk = lambda off: pl.BlockSpec((1, s, 128), lambda i, j: (i, 0, off + j))
    return _pcall(
        body, (pq3, pq3, pq3, cum4), name=name, grid=(b, hp),
        in_specs=[blk(0), blk(hp), blk(2 * hp), pl.BlockSpec((1, 1, 2, s), lambda i, j: (i, j, 0, 0))],
        out_specs=[blk(0), pl.BlockSpec((1, 1, s, 128), lambda i, j: (i, j, 0, 0))],
        out_shape=[SDS((b, s, df), BF), SDS((b, hp, s, 128), F32)],
        sem=("parallel", "parallel"), side=side)


def _sgu_mix(wm, vnb, bias, gmasks):
    out = bias
    for g in range(len(wm)):
        out = out + jnp.where(gmasks[g], _dot(wm[g], vnb), 0.0)
    return out


def _sgu_consts(ws_ref, bs_ref, ds):
    ng, c, _ = ws_ref.shape
    gd = ds // ng
    tri = lax.broadcasted_iota(jnp.int32, (c, c), 0) >= lax.broadcasted_iota(jnp.int32, (c, c), 1)
    lane = lax.broadcasted_iota(jnp.int32, (1, ds), 1)
    gmasks = [(lane >= g * gd) & (lane < (g + 1) * gd) for g in range(ng)]
    wm = [jnp.where(tri, ws_ref[g], 0.0).astype(BF) for g in range(ng)]
    bias = jnp.zeros((c, ds), F32)
    for g in range(ng):
        bias = jnp.where(gmasks[g], bs_ref[g], bias)
    return tri, gmasks, wm, bias


def _sgu_fwd(ps, lng, lnb, ws, bs, name, side=None):
    t, ds2 = ps.shape
    ds = ds2 // 2
    c = ws.shape[1]
    tm = _tile(t, 512, c)

    def body(p_ref, g_ref, b_ref, ws_ref, bs_ref, y_ref):
        _, gmasks, wm, bias = _sgu_consts(ws_ref, bs_ref, ds)
        up = _gelu(p_ref[:, 0:ds])
        xhat, _ = _ln_stats(_gelu(p_ref[:, ds:ds2]))
        vnb = (xhat * g_ref[...] + b_ref[...]).astype(BF)
        for n in range(tm // c):
            r0, r1 = n * c, (n + 1) * c
            y_ref[r0:r1, :] = (up[r0:r1] * _sgu_mix(wm, vnb[r0:r1], bias, gmasks)).astype(BF)

    fixed2 = lambda i: (0, 0)
    fixed3 = lambda i: (0, 0, 0)
    return _pcall(
        body, (ps, lng, lnb, ws, bs), name=name, grid=(t // tm,),
        in_specs=[pl.BlockSpec((tm, ds2), lambda i: (i, 0)), pl.BlockSpec((1, ds), fixed2),
                  pl.BlockSpec((1, ds), fixed2), pl.BlockSpec(ws.shape, fixed3), pl.BlockSpec(bs.shape, fixed3)],
        out_specs=pl.BlockSpec((tm, ds), lambda i: (i, 0)),
        out_shape=SDS((t, ds), BF), sem=("parallel",), side=side)


def _acc_rows(ref, val, first):
    @pl.when(first)
    def _():
        ref[...] = val

    @pl.when(jnp.logical_not(first))
    def _():
        ref[...] += val


def _ffn_bwd_mid(dy, z, gamma, wd, h, name, side=None):
    t, d = dy.shape
    dff = wd.shape[0]
    half = dff // 2
    tm = _tile(t, 512)

    def body(dy_ref, z_ref, g_ref, wd_ref, h_ref, dz_ref, df_ref, dh_ref, dg_ref, db_ref):
        dz, dgam, dbet = _ln_bwd(dy_ref[...], z_ref[...], g_ref[...])
        first = pl.program_id(0) == 0
        _acc_rows(dg_ref, dgam, first)
        _acc_rows(db_ref, dbet, first)
        dz_ref[...] = dz
        dfb = (0.5 * dz).astype(BF)
        df_ref[...] = dfb
        for j in range(2):
            c0, c1 = j * half, (j + 1) * half
            da = _dotg(dfb, wd_ref[c0:c1, :], NT).astype(BF)
            g = h_ref[0, :, c0:c1]
            u = h_ref[1, :, c0:c1]
            sg = _sigmoid(g)
            dh_ref[0, :, c0:c1] = da * u * sg * (1.0 + g * (1.0 - sg))
            dh_ref[1, :, c0:c1] = da * g * sg

    row = lambda i: (i, 0)
    fixed = lambda i: (0, 0)
    return _pcall(
        body, (dy, z, gamma, wd, h), name=name, grid=(t // tm,),
        in_specs=[pl.BlockSpec((tm, d), row), pl.BlockSpec((tm, d), row), pl.BlockSpec((1, d), fixed),
                  pl.BlockSpec(wd.shape, fixed, pipeline_mode=pl.Buffered(1)),
                  pl.BlockSpec((2, tm, dff), lambda i: (0, i, 0))],
        out_specs=[pl.BlockSpec((tm, d), row), pl.BlockSpec((tm, d), row),
                   pl.BlockSpec((2, tm, dff), lambda i: (0, i, 0)),
                   pl.BlockSpec((1, d), fixed), pl.BlockSpec((1, d), fixed)],
        out_shape=[SDS((t, d), F32), SDS((t, d), BF), SDS((2, t, dff), BF), SDS((1, d), F32), SDS((1, d), F32)],
        sem=("arbitrary",), side=side)


def _ffn_bwd_dx(dh, wup, dz, alpha, name, side=None):
    _, t, dff = dh.shape
    nq, d, w = wup.shape
    per = dff // w
    tm = _tile(t, 512)

    def body(dh_ref, w_ref, dz_ref, dx_ref):
        acc = alpha * dz_ref[...]
        for q in range(nq):
            c0 = (q % per) * w
            acc = acc + _dotg(dh_ref[q // per, :, c0:c0 + w], w_ref[q], NT)
        dx_ref[...] = acc

    row = lambda i: (i, 0)
    return _pcall(
        body, (dh, wup, dz), name=name, grid=(t // tm,),
        in_specs=[pl.BlockSpec((2, tm, dff), lambda i: (0, i, 0)),
                  pl.BlockSpec(wup.shape, lambda i: (0, 0, 0), pipeline_mode=pl.Buffered(1)),
                  pl.BlockSpec((tm, d), row)],
        out_specs=pl.BlockSpec((tm, d), row), out_shape=SDS((t, d), F32),
        sem=("parallel",), side=side)


def _dw(at, b3, ka, nb, name, side=None):
    ka_tot, t = at.shape
    gb, _, nb_tot = b3.shape
    na, ncb = ka_tot // ka, nb_tot // nb
    tm = _tile(t, DW_TOKENS, 128)

    def body(a_ref, b_ref, o_ref):
        part = _dot(a_ref[...], b_ref[0])

        @pl.when(pl.program_id(2) == 0)
        def _():
            o_ref[0, 0] = part

        @pl.when(pl.program_id(2) != 0)
        def _():
            o_ref[0, 0] += part

    return _pcall(
        body, (at, b3), name=name, grid=(na, gb * ncb, t // tm),
        in_specs=[pl.BlockSpec((ka, tm), lambda ja, jb, i: (ja, i)),
                  pl.BlockSpec((1, tm, nb), lambda ja, jb, i: (jb // ncb, i, jb % ncb))],
        out_specs=pl.BlockSpec((1, 1, ka, nb), lambda ja, jb, i: (ja, jb, 0, 0)),
        out_shape=SDS((na, gb * ncb, ka, nb), F32),
        sem=("parallel", "parallel", "arbitrary"), side=side)


def _dw_groups(ats, bs, tokens, name, side=None):
    t = bs[0].shape[0]
    na, nb = len(ats), len(bs)
    roff, coff = [0], [0]
    for a in ats:
        roff.append(roff[-1] + a.shape[0])
    for b in bs:
        coff.append(coff[-1] + b.shape[1])
    tm = _tile(t, tokens, 128)

    def body(*refs):
        a_refs, b_refs, o_ref = refs[:na], refs[na:na + nb], refs[na + nb]
        first = pl.program_id(0) == 0
        for i in range(na):
            for j in range(nb):
                _acc_rows(o_ref.at[roff[i]:roff[i + 1], coff[j]:coff[j + 1]], _dot(a_refs[i][...], b_refs[j][...]), first)

    return _pcall(
        body, (*ats, *bs), name=name, grid=(t // tm,),
        in_specs=[pl.BlockSpec((a.shape[0], tm), lambda i: (0, i)) for a in ats]
        + [pl.BlockSpec((tm, b.shape[1]), lambda i: (i, 0)) for b in bs],
        out_specs=pl.BlockSpec((roff[-1], coff[-1]), lambda i: (0, 0)),
        out_shape=SDS((roff[-1], coff[-1]), F32), sem=("arbitrary",), side=side)


def _out_bwd(dy, z, gamma, wout, widths, name, side=None):
    t, d = dy.shape
    wa, wb, wc = widths
    tm = _tile(t, 512)

    def body(dy_ref, z_ref, g_ref, w_ref, dz_ref, dzb_ref, da_ref, dbb_ref, dc_ref, dg_ref, db_ref):
        dz, dgam, dbet = _ln_bwd(dy_ref[...], z_ref[...], g_ref[...])
        first = pl.program_id(0) == 0
        _acc_rows(dg_ref, dgam, first)
        _acc_rows(db_ref, dbet, first)
        dz_ref[...] = dz
        dzb = dz.astype(BF)
        dzb_ref[...] = dzb
        da_ref[...] = _dotg(dzb, w_ref[0:wa, :], NT).astype(BF)
        dbb_ref[...] = _dotg(dzb, w_ref[wa:wa + wb, :], NT).astype(BF)
        dc_ref[...] = _dotg(dzb, w_ref[wa + wb:wa + wb + wc, :], NT).astype(BF)

    row = lambda i: (i, 0)
    fixed = lambda i: (0, 0)
    return _pcall(
        body, (dy, z, gamma, wout), name=name, grid=(t // tm,),
        in_specs=[pl.BlockSpec((tm, d), row), pl.BlockSpec((tm, d), row), pl.BlockSpec((1, d), fixed),
                  pl.BlockSpec(wout.shape, fixed)],
        out_specs=[pl.BlockSpec((tm, d), row), pl.BlockSpec((tm, d), row), pl.BlockSpec((tm, wa), row),
                   pl.BlockSpec((tm, wb), row), pl.BlockSpec((tm, wc), row),
                   pl.BlockSpec((1, d), fixed), pl.BlockSpec((1, d), fixed)],
        out_shape=[SDS((t, d), F32), SDS((t, d), BF), SDS((t, wa), BF), SDS((t, wb), BF), SDS((t, wc), BF),
                   SDS((1, d), F32), SDS((1, d), F32)],
        sem=("arbitrary",), side=side)


def _conv_bwd(pc3, dya3, cw, name, side=None):
    b, s, c3 = pc3.shape
    c = c3 // 3

    def body(p_ref, dy_ref, w_ref, dp_ref, dw_ref):
        cb = p_ref[0, :, 0:c]
        cc = p_ref[0, :, c:2 * c]
        ch = p_ref[0, :, 2 * c:3 * c]
        z = cc * ch
        z1 = _shift_rows(z, 1, True)
        z2 = _shift_rows(z, 2, True)
        w0, w1, w2 = w_ref[0:1, :], w_ref[1:2, :], w_ref[2:3, :]
        dy = dy_ref[0].astype(F32)
        dconv = dy * cb
        dz = w2 * dconv + w1 * _shift_rows(dconv, 1, False) + w0 * _shift_rows(dconv, 2, False)
        dp_ref[0, :, 0:c] = (dy * (w0 * z2 + w1 * z1 + w2 * z)).astype(BF)
        dp_ref[0, :, c:2 * c] = (dz * ch).astype(BF)
        dp_ref[0, :, 2 * c:3 * c] = (dz * cc).astype(BF)
        first = pl.program_id(0) == 0
        for r, zs in enumerate((z2, z1, z)):
            _acc_rows(dw_ref.at[r:r + 1], jnp.sum(dconv * zs, axis=0, keepdims=True), first)

    blk = lambda i: (i, 0, 0)
    return _pcall(
        body, (pc3, dya3, cw), name=name, grid=(b,),
        in_specs=[pl.BlockSpec((1, s, c3), blk), pl.BlockSpec((1, s, c), blk), pl.BlockSpec((3, c), lambda i: (0, 0))],
        out_specs=[pl.BlockSpec((1, s, c3), blk), pl.BlockSpec((3, c), lambda i: (0, 0))],
        out_shape=[SDS((b, s, c3), BF), SDS((3, c), F32)],
        sem=("arbitrary",), side=side)


def _fox_bwd(pq3, cum4, lse4, dyb3, name, side=None):
    b, s, d3 = pq3.shape
    df = d3 // 3
    hp = df // 128
    tq = _tile(s, FOX_Q_BLOCK)
    scale = FOX_HEAD_DIM ** -0.5

    def body(q_ref, k_ref, v_ref, c_ref, lse_ref, do_ref, dq_ref, dk_ref, dv_ref, dc_ref, dk_acc, dv_acc):
        masks = _head_masks(128)
        dk_acc[...] = jnp.zeros_like(dk_acc)
        dv_acc[...] = jnp.zeros_like(dv_acc)
        dc_ref[...] = jnp.zeros_like(dc_ref)
        for i in range(s // tq):
            lo, hi = i * tq, (i + 1) * tq
            q = q_ref[0, lo:hi, :]
            do = do_ref[0, lo:hi, :]
            k = k_ref[0, 0:hi, :]
            v = v_ref[0, 0:hi, :]
            lse = lse_ref[0, 0, lo:hi, :]
            dq = jnp.zeros((tq, 128), F32)
            for e in range(2):
                dom = jnp.where(masks[e], do, 0)
                sc, qm = _fox_scores(q, k, c_ref[0, 0, e:e + 1, 0:hi], lo, masks[e])
                p = jnp.exp(sc - lse[:, FOX_HEAD_DIM * e:FOX_HEAD_DIM * e + 1])
                dp = _dotg(dom, v, NT)
                ds = p * (dp - jnp.sum(p * dp, axis=-1, keepdims=True))
                dsb = ds.astype(BF)
                dq = jnp.where(masks[e], _dot(dsb, k) * scale, dq)
                dk_acc[0:hi, :] += _dotg(dsb, qm, TN)
                dv_acc[0:hi, :] += _dotg(p.astype(BF), dom, TN)
                dc_ref[0, 0, e:e + 1, 0:hi] -= jnp.sum(ds, axis=0, keepdims=True)
            dq_ref[0, lo:hi, :] = dq.astype(BF)
        dk_ref[0] = dk_acc[...].astype(BF)
        dv_ref[0] = dv_acc[...].astype(BF)

    blk = lambda off: pl.BlockSpec((1, s, 128), lambda i, j: (i, 0, off + j))
    cblk = pl.BlockSpec((1, 1, 2, s), lambda i, j: (i, j, 0, 0))
    return _pcall(
        body, (pq3, pq3, pq3, cum4, lse4, dyb3), name=name, grid=(b, hp),
        in_specs=[blk(0), blk(hp), blk(2 * hp), cblk,
                  pl.BlockSpec((1, 1, s, 128), lambda i, j: (i, j, 0, 0)), blk(0)],
        out_specs=[blk(0), blk(0), blk(0), cblk],
        out_shape=[SDS((b, s, df), BF), SDS((b, s, df), BF), SDS((b, s, df), BF), SDS(cum4.shape, F32)],
        scratch_shapes=[pltpu.VMEM((s, 128), F32), pltpu.VMEM((s, 128), F32)],
        sem=("parallel", "parallel"), side=side)


def _cum_bwd(dcum, flog, x3, name, side=None):
    b, h, s = dcum.shape
    d = x3.shape[2]

    def body(dc_ref, fl_ref, x_ref, dfl_ref, dbf_ref, dwf_ref):
        dfl = _prefix_sum_lanes(dc_ref[0], reverse=True) * _sigmoid(-fl_ref[0])
        dfl_ref[0] = dfl
        first = pl.program_id(0) == 0
        _acc_rows(dbf_ref, jnp.broadcast_to(jnp.sum(dfl, axis=-1, keepdims=True), (h, 128)), first)
        dflp = jnp.concatenate([dfl, jnp.zeros((HEAD_ROWS - h, s), F32)], axis=0).astype(BF)
        _acc_rows(dwf_ref, _dot(dflp, x_ref[0].astype(BF))[0:h], first)

    blk = lambda i: (i, 0, 0)
    return _pcall(
        body, (dcum, flog, x3), name=name, grid=(b,),
        in_specs=[pl.BlockSpec((1, h, s), blk), pl.BlockSpec((1, h, s), blk), pl.BlockSpec((1, s, d), blk)],
        out_specs=[pl.BlockSpec((1, h, s), blk), pl.BlockSpec((h, 128), lambda i: (0, 0)),
                   pl.BlockSpec((h, d), lambda i: (0, 0))],
        out_shape=[SDS((b, h, s), F32), SDS((h, 128), F32), SDS((h, d), F32)],
        sem=("arbitrary",), side=side)


def _sgu_bwd(ps, dyc, lng, lnb, ws, bs, name, side=None):
    t, ds2 = ps.shape
    ds = ds2 // 2
    ng, c, _ = ws.shape
    tm = _tile(t, 512, c)

    def body(p_ref, dy_ref, g_ref, b_ref, ws_ref, bs_ref, dp_ref, dws_ref, dbs_ref, dg_ref, db_ref, dvn_acc):
        tri, gmasks, wm, bias = _sgu_consts(ws_ref, bs_ref, ds)
        su = p_ref[:, 0:ds]
        sv = p_ref[:, ds:ds2]
        up = _gelu(su)
        gv = _gelu(sv)
        xhat, rstd = _ln_stats(gv)
        vnb = (xhat * g_ref[...] + b_ref[...]).astype(BF)
        dy = dy_ref[...].astype(F32)
        dws = [jnp.zeros((c, c), F32) for _ in range(ng)]
        dbs = [jnp.zeros((c, 1), F32) for _ in range(ng)]
        for n in range(tm // c):
            r0, r1 = n * c, (n + 1) * c
            mixed = _sgu_mix(wm, vnb[r0:r1], bias, gmasks)
            dp_ref[r0:r1, 0:ds] = (dy[r0:r1] * mixed * _gelu_grad(su[r0:r1])).astype(BF)
            dmix = dy[r0:r1] * up[r0:r1]
            dvn = jnp.zeros((c, ds), F32)
            for g in range(ng):
                dmg = jnp.where(gmasks[g], dmix, 0.0)
                dmb = dmg.astype(BF)
                dws[g] = dws[g] + _dotg(dmb, vnb[r0:r1], NT)
                dbs[g] = dbs[g] + jnp.sum(dmg, axis=-1, keepdims=True)
                dvn = dvn + _dotg(wm[g], dmb, TN)
            dvn_acc[r0:r1, :] = dvn
        dvn_all = dvn_acc[...]
        gdv = dvn_all * g_ref[...]
        m1 = jnp.mean(gdv, axis=-1, keepdims=True)
        m2 = jnp.mean(gdv * xhat, axis=-1, keepdims=True)
        dgv = rstd * (gdv - m1 - xhat * m2)
        dp_ref[:, ds:ds2] = (dgv * _gelu_grad(sv)).astype(BF)
        first = pl.program_id(0) == 0
        _acc_rows(dg_ref, jnp.sum(dvn_all * xhat, axis=0, keepdims=True), first)
        _acc_rows(db_ref, jnp.sum(dvn_all, axis=0, keepdims=True), first)
        for g in range(ng):
            _acc_rows(dws_ref.at[g], jnp.where(tri, dws[g], 0.0), first)
            _acc_rows(dbs_ref.at[g], dbs[g], first)

    row = lambda i: (i, 0)
    fixed2 = lambda i: (0, 0)
    fixed3 = lambda i: (0, 0, 0)
    return _pcall(
        body, (ps, dyc, lng, lnb, ws, bs), name=name, grid=(t // tm,),
        in_specs=[pl.BlockSpec((tm, ds2), row), pl.BlockSpec((tm, ds), row), pl.BlockSpec((1, ds), fixed2),
                  pl.BlockSpec((1, ds), fixed2), pl.BlockSpec(ws.shape, fixed3), pl.BlockSpec(bs.shape, fixed3)],
        out_specs=[pl.BlockSpec((tm, ds2), row), pl.BlockSpec(ws.shape, fixed3), pl.BlockSpec(bs.shape, fixed3),
                   pl.BlockSpec((1, ds), fixed2), pl.BlockSpec((1, ds), fixed2)],
        out_shape=[SDS((t, ds2), BF), SDS(ws.shape, F32), SDS(bs.shape, F32), SDS((1, ds), F32), SDS((1, ds), F32)],
        scratch_shapes=[pltpu.VMEM((tm, ds), F32)],
        sem=("arbitrary",), side=side)


def _mix_bwd_dx(dz, dconv, dq, dk, dv, dsgu, dflog, w_main, wft, seq, alpha, name, side=None):
    t, d = dz.shape
    groups = [dconv, dq, dk, dv, dsgu]
    offs = [0]
    for g in groups:
        offs.append(offs[-1] + g.shape[1])
    h = dflog.shape[1]
    tm = _tile(seq, 512)
    per_seq = seq // tm

    def body(dz_ref, a0, a1, a2, a3, a4, dfl_ref, w_ref, wf_ref, dx_ref):
        dflp = jnp.concatenate([dfl_ref[0], jnp.zeros((HEAD_ROWS - h, tm), F32)], axis=0).astype(BF)
        acc = alpha * dz_ref[...] + _dotg(dflp, wf_ref[...], TN)
        for k, a_ref in enumerate((a0, a1, a2, a3, a4)):
            acc = acc + _dotg(a_ref[...], w_ref[:, offs[k]:offs[k + 1]], NT)
        dx_ref[...] = acc

    row = lambda i: (i, 0)
    return _pcall(
        body, (dz, *groups, dflog, w_main, wft), name=name, grid=(t // tm,),
        in_specs=[pl.BlockSpec((tm, d), row)] + [pl.BlockSpec((tm, g.shape[1]), row) for g in groups]
        + [pl.BlockSpec((1, h, tm), lambda i: (i // per_seq, 0, i % per_seq)),
           pl.BlockSpec(w_main.shape, lambda i: (0, 0)), pl.BlockSpec(wft.shape, lambda i: (0, 0))],
        out_specs=pl.BlockSpec((tm, d), row), out_shape=SDS((t, d), F32),
        sem=("parallel",), side=side)


def _adam_math(w, g, m, v):
    c1 = 1.0 / (1.0 - ADAM_B1 ** ADAM_STEP)
    c2 = 1.0 / (1.0 - ADAM_B2 ** ADAM_STEP)
    nm = ADAM_B1 * m + (1.0 - ADAM_B1) * g
    nv = ADAM_B2 * v + (1.0 - ADAM_B2) * (g * g)
    delta = -ADAM_LR * ((nm * c1) / (jnp.sqrt(nv * c2) + ADAM_EPS) + ADAM_WD * w)
    return delta, nm, nv


def _adamw_small(w, g, m, v, name):
    r, c = w.shape
    tr = _tile(r, 512)

    def body(w_ref, g_ref, m_ref, v_ref, d_ref, nm_ref, nv_ref):
        d_ref[...], nm_ref[...], nv_ref[...] = _adam_math(w_ref[...], g_ref[...], m_ref[...], v_ref[...])

    blk = pl.BlockSpec((tr, c), lambda i: (i, 0))
    return _pcall(body, (w, g, m, v), name=name, grid=(r // tr,), in_specs=[blk] * 4, out_specs=[blk] * 3,
                  out_shape=[SDS((r, c), F32)] * 3, sem=("parallel",))


def _adamw_shard(w, m, v, tot, recv, cq, layer, prev, name, side=None):
    nl, xr, yc = w.shape
    h = xr // 2
    tr = _tile(h, 256)
    nt = h // tr

    def body(cq_ref, w_ref, m_ref, v_ref, t_ref, r_ref, *rest):
        g_ref, d_ref, nm_ref, nv_ref = rest[-4:]
        g = jnp.where(pl.program_id(0) == cq_ref[0], t_ref[...], r_ref[...])
        g_ref[0] = g
        d_ref[0], nm_ref[0], nv_ref[0] = _adam_math(w_ref[0], g, m_ref[0], v_ref[0])

    slab = pl.BlockSpec((1, tr, yc), lambda hf, i, cq_ref: (layer, hf * nt + i, 0))
    mine = pl.BlockSpec((tr, yc), lambda hf, i, cq_ref: (jnp.where(hf == cq_ref[0], i, 0), 0))
    theirs = pl.BlockSpec((tr, yc), lambda hf, i, cq_ref: (jnp.where(hf == cq_ref[0], 0, i), 0))
    operands = [w, m, v, tot, recv]
    in_specs = [slab, slab, slab, mine, theirs]
    aliases = None
    if prev is not None:
        operands += list(prev)
        in_specs += [HBM] * 4
        aliases = {6 + k: k for k in range(4)}
    return _pcall(body, operands, name=name, grid=(2, nt), prefetch=(cq,), in_specs=in_specs,
                  out_specs=[slab] * 4, out_shape=[SDS(w.shape, F32)] * 4, aliases=aliases,
                  sem=("parallel", "parallel"), side=side)


BIG = ("ffn1_w_up", "ffn1_w_down", "mix_w_in", "mix_w_out", "ffn2_w_up", "ffn2_w_down")
SMALL = ("ln1_g", "ln1_b", "fox_b_f", "sgu_ln_g", "sgu_ln_b", "sgu_w_s", "sgu_b_s", "ln2_g", "ln2_b", "ln3_g", "ln3_b")
ORDER = ("ln1_g", "ln1_b", "ffn1_w_up", "ffn1_w_down", "mix_w_in", "fox_b_f", "conv_w", "sgu_ln_g", "sgu_ln_b",
         "sgu_w_s", "sgu_b_s", "mix_w_out", "ln2_g", "ln2_b", "ffn2_w_up", "ffn2_w_down", "ln3_g", "ln3_b")


def _row(v):
    return v.reshape(1, -1)


class _Pipe:
    def __init__(self, stages):
        self.stages = list(stages)
        self.pos = 0
        self.last = None

    def kind(self):
        return self.stages[self.pos][0] if self.pos < len(self.stages) else None


class _Sched:
    def __init__(self):
        self.pipes = []
        self.n_alone = 0

    def add(self, stages):
        self.pipes.append(_Pipe(stages))

    def _take_comms(self, skip=None):
        jobs = []
        for p in self.pipes:
            if p is not skip and p.kind() == "comm":
                jobs.append((p, p.stages[p.pos][1]()))
        return jobs

    @staticmethod
    def _landed(jobs):
        for p, side in jobs:
            p.last = side.results
            p.pos += 1

    def carry(self, builder, *args, **kw):
        jobs = self._take_comms()
        res = builder(*args, side=_join([s for _, s in jobs]), **kw)
        self._landed(jobs)
        self._computes(ride=False)
        return res

    def _computes(self, ride):
        again = True
        while again:
            again = False
            for p in self.pipes:
                if p.kind() == "compute":
                    jobs = self._take_comms(skip=p) if ride else []
                    p.stages[p.pos][1](p.last, _join([s for _, s in jobs]))
                    p.pos += 1
                    self._landed(jobs)
                    again = True

    def drain(self):
        while any(p.kind() is not None for p in self.pipes):
            self._computes(ride=True)
            jobs = self._take_comms()
            if jobs:
                _run_side(_join([s for _, s in jobs]), "exchange_tail_%d" % self.n_alone)
                self.n_alone += 1
                self._landed(jobs)


def _forward_layer(x, xt, p, dims, alpha, l, ride, target=None):
    b, s = dims["b"], dims["s"]
    t, d = x.shape
    tag = "l%d_" % l

    def run(stage, builder, *args, **kw):
        side, on_done = ride.get(stage, (None, None))
        res = builder(*args, tag + stage, side=side, **kw)
        if on_done is not None:
            on_done()
        return res

    if xt is None:
        h1, a1, a1t, xt = run("ffn1_up", _ffn_up_fwd, x, p["wup1"], emit_xt=True)
    else:
        h1, a1, a1t = run("ffn1_up", _ffn_up_fwd, x, p["wup1"])
    z1, x1, x1t = run("ffn1_down", _res_ln_fwd, [a1], p["wd1"], x, p["ln1_g"], p["ln1_b"], alpha, 0.5)
    pc, pq, ps = run("mix_proj", _mix_proj_fwd, x1, p["win"], dims["proj_widths"])
    x1_3 = x1.reshape(b, s, d)
    flog, cum = run("fox_gate", _cum_fwd, x1_3, p["wft"], p["bf"])
    nh = flog.shape[1]
    cum4 = cum.reshape(b, nh // 2, 2, s)
    pc3 = pc.reshape(b, s, -1)
    pq3 = pq.reshape(b, s, -1)
    ya = run("conv", _conv_fwd, pc3, p["cw"]).reshape(t, -1)
    yb3, lse4 = run("fox", _fox_fwd, pq3, cum4)
    yb = yb3.reshape(t, -1)
    yc = run("sgu", _sgu_fwd, ps, p["sgu_g"], p["sgu_b"], p["ws"], p["bs"])
    z2, x2, x2t, yat, ybt, yct = run("mix_out", _res_ln_fwd, [ya, yb, yc], p["wout"], x1, p["ln2_g"], p["ln2_b"],
                                      alpha, 1.0, parts_t=True)
    h2, a2, a2t = run("ffn2_up", _ffn_up_fwd, x2, p["wup2"])
    z3, x3, x3t = run("ffn2_down", _res_ln_fwd, [a2], p["wd2"], x2, p["ln3_g"], p["ln3_b"], alpha, 0.5, target=target)
    saved = dict(xt=xt, h1=h1, a1t=a1t, z1=z1, x1=x1, x1t=x1t, pc3=pc3, pq3=pq3, ps=ps, flog=flog, cum4=cum4,
                 lse4=lse4, yat=yat, ybt=ybt, yct=yct, z2=z2, x2t=x2t, h2=h2, a2t=a2t, z3=z3)
    return x3, x3t, saved


def _ffn_backward(sched, emit, which, dy, z, gamma, wd, wup, h, a_t, x_in_t, alpha, tag, after_mid=None):
    dz, df, dh, dgam, dbet = sched.carry(_ffn_bwd_mid, dy, z, gamma, wd, h, tag + "_bwd_mid")
    if after_mid is not None:
        after_mid(dgam, dbet)
    nq, d, w = wup.shape
    emit(which + "_w_up", sched.carry(_dw, x_in_t, dh, d, w, tag + "_dw_up")[0])
    half = wd.shape[0] // 2
    emit(which + "_w_down", sched.carry(_dw, a_t, df[None], half, d, tag + "_dw_down").reshape(nq, -1, d))
    dx = sched.carry(_ffn_bwd_dx, dh, wup, dz, alpha, tag + "_bwd_dx")
    return dx, dgam, dbet


def _backward_layer(sched, emit, emit_small, dy, sv, p, dims, alpha, l):
    b, s = dims["b"], dims["s"]
    tag = "l%d_" % l
    t, d = dy.shape
    g = {}
    dx2, g["ln3_g"], g["ln3_b"] = _ffn_backward(sched, emit, "ffn2", dy, sv["z3"], p["ln3_g"], p["wd2"], p["wup2"],
                                                sv["h2"], sv["a2t"], sv["x2t"], alpha, tag + "ffn2")
    wa, wb, wc = sv["yat"].shape[0], sv["ybt"].shape[0], sv["yct"].shape[0]
    dz2, dz2b, dya, dyb, dyc, g["ln2_g"], g["ln2_b"] = sched.carry(
        _out_bwd, dx2, sv["z2"], p["ln2_g"], p["wout"], (wa, wb, wc), tag + "mix_out_bwd")
    emit("mix_w_out", sched.carry(_dw_groups, [sv["yat"], sv["ybt"], sv["yct"]], [dz2b], DW_TOKENS,
                                  tag + "dw_out").reshape(N_SHARDS, -1, d))
    dpc3, g["conv_w"] = sched.carry(_conv_bwd, sv["pc3"], dya.reshape(b, s, -1), p["cw"], tag + "conv_bwd")
    dq3, dk3, dv3, dcum4 = sched.carry(_fox_bwd, sv["pq3"], sv["cum4"], sv["lse4"], dyb.reshape(b, s, -1),
                                       tag + "fox_bwd")
    nh = sv["flog"].shape[1]
    dflog, dbf, dwft = sched.carry(_cum_bwd, dcum4.reshape(b, nh, s), sv["flog"], sv["x1"].reshape(b, s, d),
                                   tag + "fox_gate_bwd")
    g["fox_b_f"] = dbf[:, 0]
    dps, g["sgu_w_s"], dbs, g["sgu_ln_g"], g["sgu_ln_b"] = sched.carry(
        _sgu_bwd, sv["ps"], dyc, p["sgu_g"], p["sgu_b"], p["ws"], p["bs"], tag + "sgu_bwd")
    g["sgu_b_s"] = dbs[:, :, 0]
    dpc = dpc3.reshape(t, -1)
    dq, dk, dv = dq3.reshape(t, -1), dk3.reshape(t, -1), dv3.reshape(t, -1)
    main = sched.carry(_dw_groups, [sv["x1t"]], [dpc, dq, dk, dv, dps], DW_TOKENS // 2, tag + "dw_in")
    n_main = main.shape[1] - dps.shape[1]
    w_in_grad = jnp.concatenate([main[:, :n_main], dwft.T, main[:, n_main:]], axis=1)
    emit("mix_w_in", jnp.moveaxis(w_in_grad.reshape(d, N_SHARDS, -1), 1, 0))
    dx1 = sched.carry(_mix_bwd_dx, dz2, dpc, dq, dk, dv, dps, dflog, p["win"], p["wft"], s, alpha, tag + "mix_bwd_dx")

    def small_ready(dgam, dbet):
        g["ln1_g"], g["ln1_b"] = dgam, dbet
        emit_small(g)

    dx0, _, _ = _ffn_backward(sched, emit, "ffn1", dx1, sv["z1"], p["ln1_g"], p["wd1"], p["wup1"],
                              sv["h1"], sv["a1t"], sv["xt"], alpha, tag + "ffn1", after_mid=small_ready)
    return dx0


def _pack_rows(flat_list):
    v = jnp.concatenate(flat_list)
    n = v.shape[0]
    pad = (-n) % 1024
    return jnp.pad(v, (0, pad)).reshape(-1, 128)


def kernel(x, ln1_g, ln1_b, ffn1_w_up, ffn1_w_down, mix_w_in, fox_b_f, conv_w, sgu_ln_g, sgu_ln_b, sgu_w_s, sgu_b_s, mix_w_out, ln2_g, ln2_b, ffn2_w_up, ffn2_w_down, ln3_g, ln3_b, loss_target, m_ln1_g, m_ln1_b, m_ffn1_w_up, m_ffn1_w_down, m_mix_w_in, m_fox_b_f, m_conv_w, m_sgu_ln_g, m_sgu_ln_b, m_sgu_w_s, m_sgu_b_s, m_mix_w_out, m_ln2_g, m_ln2_b, m_ffn2_w_up, m_ffn2_w_down, m_ln3_g, m_ln3_b, v_ln1_g, v_ln1_b, v_ffn1_w_up, v_ffn1_w_down, v_mix_w_in, v_fox_b_f, v_conv_w, v_sgu_ln_g, v_sgu_ln_b, v_sgu_w_s, v_sgu_b_s, v_mix_w_out, v_ln2_g, v_ln2_b, v_ffn2_w_up, v_ffn2_w_down, v_ln3_g, v_ln3_b):
    wts = dict(ln1_g=ln1_g, ln1_b=ln1_b, ffn1_w_up=ffn1_w_up, ffn1_w_down=ffn1_w_down, mix_w_in=mix_w_in,
               fox_b_f=fox_b_f, conv_w=conv_w, sgu_ln_g=sgu_ln_g, sgu_ln_b=sgu_ln_b, sgu_w_s=sgu_w_s,
               sgu_b_s=sgu_b_s, mix_w_out=mix_w_out, ln2_g=ln2_g, ln2_b=ln2_b, ffn2_w_up=ffn2_w_up,
               ffn2_w_down=ffn2_w_down, ln3_g=ln3_g, ln3_b=ln3_b)
    mom = dict(ln1_g=m_ln1_g, ln1_b=m_ln1_b, ffn1_w_up=m_ffn1_w_up, ffn1_w_down=m_ffn1_w_down, mix_w_in=m_mix_w_in,
               fox_b_f=m_fox_b_f, conv_w=m_conv_w, sgu_ln_g=m_sgu_ln_g, sgu_ln_b=m_sgu_ln_b, sgu_w_s=m_sgu_w_s,
               sgu_b_s=m_sgu_b_s, mix_w_out=m_mix_w_out, ln2_g=m_ln2_g, ln2_b=m_ln2_b, ffn2_w_up=m_ffn2_w_up,
               ffn2_w_down=m_ffn2_w_down, ln3_g=m_ln3_g, ln3_b=m_ln3_b)
    var = dict(ln1_g=v_ln1_g, ln1_b=v_ln1_b, ffn1_w_up=v_ffn1_w_up, ffn1_w_down=v_ffn1_w_down, mix_w_in=v_mix_w_in,
               fox_b_f=v_fox_b_f, conv_w=v_conv_w, sgu_ln_g=v_sgu_ln_g, sgu_ln_b=v_sgu_ln_b, sgu_w_s=v_sgu_w_s,
               sgu_b_s=v_sgu_b_s, mix_w_out=v_mix_w_out, ln2_g=v_ln2_g, ln2_b=v_ln2_b, ffn2_w_up=v_ffn2_w_up,
               ffn2_w_down=v_ffn2_w_down, ln3_g=v_ln3_g, ln3_b=v_ln3_b)

    nl = ln1_g.shape[0]
    b, s, d = x.shape
    t = b * s
    alpha = (2 * nl) ** 0.25
    cw_sh = conv_w.shape[2]
    d_conv = cw_sh * N_SHARDS
    d_sgu = sgu_ln_g.shape[1]
    nh = fox_b_f.shape[1]
    d_fox = nh * FOX_HEAD_DIM
    n_main = 3 * d_conv + 3 * d_fox
    dims = dict(b=b, s=s, proj_widths=(3 * d_conv, 3 * d_fox, 2 * d_sgu))
    cpos = lax.axis_index("c").astype(jnp.int32)
    qpos = (2 * lax.axis_index("x") + lax.axis_index("y")).astype(jnp.int32)
    cq = jnp.stack([cpos, qpos])

    me = (2 * qpos + cpos).reshape(1)
    assert nl == 2, "the gather schedule below names the carriers of a two-layer step"

    conv_tile = jnp.pad(conv_w, ((0, 0), (0, 8 - conv_w.shape[1]), (0, 128 - cw_sh)))
    params = [dict(bf=fox_b_f[l].reshape(nh, 1), sgu_g=_row(sgu_ln_g[l]), sgu_b=_row(sgu_ln_b[l]), ws=sgu_w_s[l],
                   bs=sgu_b_s[l][:, :, None], ln1_g=_row(ln1_g[l]), ln1_b=_row(ln1_b[l]), ln2_g=_row(ln2_g[l]),
                   ln2_b=_row(ln2_b[l]), ln3_g=_row(ln3_g[l]), ln3_b=_row(ln3_b[l])) for l in range(nl)]

    def operands_of(k, arr):
        if k == "mix_w_in":
            w_in = jnp.moveaxis(arr, 0, 1).reshape(d, -1)
            return dict(win=jnp.concatenate([w_in[:, :n_main], w_in[:, n_main + nh:]], axis=1),
                        wft=jnp.pad(w_in[:, n_main:n_main + nh].T, ((0, HEAD_ROWS - nh), (0, 0))))
        if k == "conv_w":
            return dict(cw=jnp.moveaxis(arr[:, :3, :cw_sh], 0, 1).reshape(3, d_conv))
        if k in ("ffn1_w_up", "ffn2_w_up"):
            return {"wup" + k[3]: arr}
        return {dict(ffn1_w_down="wd1", ffn2_w_down="wd2", mix_w_out="wout")[k]: arr.reshape(-1, d)}

    def gather(l, keys):
        side = _side_gather([conv_tile[l] if k == "conv_w" else wts[k][l].astype(BF) for k in keys],
                            [k != "conv_w" for k in keys])

        def install():
            for k, arr in zip(keys, side.results):
                params[l].update(operands_of(k, arr))
        return side, install

    first, install_first = gather(0, ["ffn1_w_up"])
    _run_side(first, "gather_first")
    install_first()
    rides = [{"ffn1_up": gather(0, ["ffn1_w_down", "mix_w_in", "mix_w_out", "conv_w"]),
              "ffn1_down": gather(0, ["ffn2_w_up"]),
              "mix_proj": gather(0, ["ffn2_w_down"]),
              "fox": gather(1, ["ffn1_w_up", "ffn1_w_down", "mix_w_in", "mix_w_out", "conv_w"]),
              "ffn2_up": gather(1, ["ffn2_w_up", "ffn2_w_down"])}, {}]

    act, act_t = x.reshape(t, d), None
    saved = []
    for l in range(nl):
        act, act_t, sv = _forward_layer(act, act_t, params[l], dims, alpha, l, rides[l],
                                        target=loss_target.reshape(t, d) if l == nl - 1 else None)
        saved.append(sv)
    dy, loss_blk = act, act_t

    sched = _Sched()
    prev = {k: None for k in BIG}
    red = {}

    def emit_for(l):
        def emit(key, g):
            st = {}
            name = "l%d_%s" % (l, key)

            def pair_sum(res, side):
                st["p"] = _pair_sum(g, res[0], cq, "rs_pair_sum_" + name, side=side)

            def chip_sum(res, side):
                st["t"] = _chip_sum(st["p"], res[0], cq, "rs_chip_sum_" + name, side=side)

            def adamw(res, side):
                prev[key] = _adamw_shard(wts[key], mom[key], var[key], st["t"], res[0], cq, l, prev[key],
                                         "adamw_" + name, side=side)

            sched.add([("comm", lambda: _side_pair_send([g])), ("compute", pair_sum),
                       ("comm", lambda: _side_scatter([st["p"]])), ("compute", chip_sum),
                       ("comm", lambda: _side_pair_share([st["t"]])), ("compute", adamw)])
        return emit

    def emit_small_for(l):
        def emit_small(g):
            flat = [g[k].reshape(-1) for k in SMALL] + [g["conv_w"].reshape(-1)]
            if l == nl - 1:
                flat.append(loss_blk[0, 0:1])
            vec = _pack_rows(flat)

            def slot_sum(res, side):
                red[l] = _sum_slots(vec, res[0], me, "small_sum_l%d" % l, side=side)

            sched.add([("comm", lambda: _side_bcast(vec)), ("compute", slot_sum)])
        return emit_small

    for l in reversed(range(nl)):
        dy = _backward_layer(sched, emit_for(l), emit_small_for(l), dy, saved[l], params[l], dims, alpha, l)
    sched.drain()
    grad_x = dy.reshape(b, s, d)
    gfin, delta, new_m, new_v = {}, {}, {}, {}
    for k in BIG:
        gfin[k], delta[k], new_m[k], new_v[k] = prev[k]

    gsm = {k: [] for k in SMALL + ("conv_w",)}
    for l in range(nl):
        flat_l = red[l].reshape(-1)
        off = 0
        for k in SMALL:
            n = wts[k][l].size
            gsm[k].append(flat_l[off:off + n].reshape(wts[k][l].shape))
            off += n
        n = 3 * d_conv
        gsm["conv_w"].append(lax.dynamic_slice_in_dim(flat_l[off:off + n].reshape(3, d_conv), qpos * cw_sh, cw_sh,
                                                      axis=1))
        off += n
        if l == nl - 1:
            loss = flat_l[off]
    for k in gsm:
        gfin[k] = jnp.stack(gsm[k])
    small_keys = SMALL + ("conv_w",)
    sizes = [wts[k].size for k in small_keys]
    pk = lambda src: _pack_rows([src[k].reshape(-1) for k in small_keys])
    dl, nm, nv = _adamw_small(pk(wts), pk(gfin), pk(mom), pk(var), "adamw_small")
    off = 0
    for k, n in zip(small_keys, sizes):
        shp = wts[k].shape
        delta[k] = dl.reshape(-1)[off:off + n].reshape(shp)
        new_m[k] = nm.reshape(-1)[off:off + n].reshape(shp)
        new_v[k] = nv.reshape(-1)[off:off + n].reshape(shp)
        off += n

    return (loss, grad_x, *[gfin[k] for k in ORDER], *[delta[k] for k in ORDER],
            *[new_m[k] for k in ORDER], *[new_v[k] for k in ORDER])
```

```python
import jax
import jax.numpy as jnp
from jax import lax
from jax.experimental import pallas as pl
from jax.experimental.pallas import tpu as pltpu

F32 = jnp.float32
BF = jnp.bfloat16
SDS = jax.ShapeDtypeStruct
MESH = pl.DeviceIdType.MESH

LN_EPS = 1e-5
FOX_HEAD_DIM = 64
FOX_Q_BLOCK = 512
DW_TOKENS = 2048
HEAD_ROWS = 128
GELU_K = 0.7978845608028654
GELU_C = 0.044715
NEG_BIG = -1e30
N_SHARDS = 4

ADAM_LR = 0.001
ADAM_B1 = 0.9
ADAM_B2 = 0.999
ADAM_EPS = 1e-08
ADAM_WD = 0.01
ADAM_STEP = 10

VMEM_LIMIT_BYTES = 56 * 1024 * 1024
NT = (((1,), (1,)), ((), ()))
TN = (((0,), (0,)), ((), ()))
HBM = pl.BlockSpec(memory_space=pl.ANY)


def _tile(n, pref, mult=8):
    t = min(n, pref)
    while n % t or t % mult:
        t -= mult
    return t


def _dot(a, b):
    return jnp.dot(a, b, preferred_element_type=F32)


def _dotg(a, b, dims):
    return lax.dot_general(a, b, dims, preferred_element_type=F32)


def _sigmoid(x):
    return 1.0 / (1.0 + jnp.exp(-x))


def _gelu(x):
    return 0.5 * x * (1.0 + jnp.tanh(GELU_K * (x + GELU_C * x * x * x)))


def _gelu_grad(x):
    t = jnp.tanh(GELU_K * (x + GELU_C * x * x * x))
    return 0.5 * (1.0 + t) + 0.5 * x * (1.0 - t * t) * GELU_K * (1.0 + 3.0 * GELU_C * x * x)


def _ln_stats(z):
    mu = jnp.mean(z, axis=-1, keepdims=True)
    zc = z - mu
    var = jnp.mean(zc * zc, axis=-1, keepdims=True)
    rstd = lax.rsqrt(var + LN_EPS)
    return zc * rstd, rstd


def _ln_bwd(dy, z, g):
    xhat, rstd = _ln_stats(z)
    gdy = dy * g
    m1 = jnp.mean(gdy, axis=-1, keepdims=True)
    m2 = jnp.mean(gdy * xhat, axis=-1, keepdims=True)
    dz = rstd * (gdy - m1 - xhat * m2)
    return dz, jnp.sum(dy * xhat, axis=0, keepdims=True), jnp.sum(dy, axis=0, keepdims=True)


class _Side:
    def __init__(self, ins, out_shapes, sems, start, finish):
        self.ins, self.out_shapes, self.sems = list(ins), list(out_shapes), list(sems)
        self.start, self.finish = start, finish
        self.results = None


def _join(sides):
    sides = [s for s in sides if s is not None]
    if not sides:
        return None
    ins = [a for s in sides for a in s.ins]
    outs = [a for s in sides for a in s.out_shapes]
    sems = [a for s in sides for a in s.sems]

    def parts(seq, field):
        out, o = [], 0
        for s in sides:
            n = len(getattr(s, field))
            out.append(seq[o:o + n])
            o += n
        return out

    def run(which):
        def fn(i, o, m):
            for s, a, b, c in zip(sides, parts(i, "ins"), parts(o, "out_shapes"), parts(m, "sems")):
                getattr(s, which)(a, b, c)
        return fn

    joined = _Side(ins, outs, sems, run("start"), run("finish"))
    joined.members = sides
    return joined


def _deliver(side, results):
    members = getattr(side, "members", None)
    side.results = list(results)
    if members:
        o = 0
        for s in members:
            n = len(s.out_shapes)
            _deliver(s, results[o:o + n])
            o += n


def _pcall(body, operands, *, name, grid, in_specs, out_specs, out_shape, sem, scratch_shapes=(),
           prefetch=(), aliases=None, side=None):
    single = not isinstance(out_shape, (list, tuple))
    out_shape = [out_shape] if single else list(out_shape)
    out_specs = [out_specs] if single else list(out_specs)
    in_specs, scratch_shapes = list(in_specs), list(scratch_shapes)
    n_pre, n_in, n_out, n_sc = len(prefetch), len(in_specs), len(out_shape), len(scratch_shapes)
    fn = body
    extra = []
    if side is not None:
        s_in, s_out = len(side.ins), len(side.out_shapes)

        def fn(*refs):
            pre, rest = refs[:n_pre], refs[n_pre:]
            m_in, c_in = rest[:n_in], rest[n_in:n_in + s_in]
            rest = rest[n_in + s_in:]
            m_out, c_out = rest[:n_out], rest[n_out:n_out + s_out]
            rest = rest[n_out + s_out:]
            m_sc, c_sc = rest[:n_sc], rest[n_sc:]
            first = pl.program_id(0) == 0
            last = pl.program_id(0) == grid[0] - 1
            for a in range(1, len(grid)):
                first = jnp.logical_and(first, pl.program_id(a) == 0)
                last = jnp.logical_and(last, pl.program_id(a) == grid[a] - 1)

            @pl.when(first)
            def _():
                side.start(c_in, c_out, c_sc)

            body(*pre, *m_in, *m_out, *m_sc)

            @pl.when(last)
            def _():
                side.finish(c_in, c_out, c_sc)

        in_specs = in_specs + [HBM] * s_in
        out_specs = out_specs + [HBM] * s_out
        out_shape = out_shape + side.out_shapes
        scratch_shapes = scratch_shapes + side.sems
        extra = side.ins
        sem = ("arbitrary",) * len(grid)
    params = pltpu.CompilerParams(dimension_semantics=tuple(sem), vmem_limit_bytes=VMEM_LIMIT_BYTES)
    kw = dict(input_output_aliases=aliases) if aliases else {}
    if n_pre:
        spec = pltpu.PrefetchScalarGridSpec(num_scalar_prefetch=n_pre, grid=grid, in_specs=in_specs,
                                            out_specs=out_specs, scratch_shapes=scratch_shapes)
        call = pl.pallas_call(fn, name=name, grid_spec=spec, out_shape=out_shape, compiler_params=params, **kw)
    else:
        call = pl.pallas_call(fn, name=name, grid=grid, in_specs=in_specs, out_specs=out_specs,
                              out_shape=out_shape, scratch_shapes=scratch_shapes, compiler_params=params, **kw)
    res = call(*prefetch, *operands, *extra)
    if side is not None:
        _deliver(side, res[n_out:])
        res = res[:n_out]
    return res[0] if single else res


def _run_side(side, name):
    def body(*refs):
        n_in, n_out = len(side.ins), len(side.out_shapes)
        i, o, m = refs[:n_in], refs[n_in:n_in + n_out], refs[n_in + n_out:]
        side.start(i, o, m)
        side.finish(i, o, m)

    res = pl.pallas_call(body, name=name, in_specs=[HBM] * len(side.ins), out_specs=[HBM] * len(side.out_shapes),
                         out_shape=side.out_shapes, scratch_shapes=side.sems)(*side.ins)
    _deliver(side, res)


def _mesh_pos():
    x, y, c = lax.axis_index("x"), lax.axis_index("y"), lax.axis_index("c")
    chips = [(1 - x, y), (x, 1 - y), (1 - x, 1 - y)]
    return x, y, c, chips


def _rows(ref, lead, half, n_rows):
    return ref.at[tuple(lead) + (pl.ds(half * n_rows, n_rows),)]


def _side_gather(shards, split):
    n = len(shards)
    hs = [w.shape[0] // 2 for w in shards]

    def plan(ins, outs, sems):
        ssem, rsem = sems
        x, y, c, chips = _mesh_pos()
        q = 2 * x + y
        sib = (x, y, 1 - c)

        def rc(p, k, src, dst, to):
            return pltpu.make_async_remote_copy(src_ref=src, dst_ref=dst, send_sem=ssem.at[p, k],
                                                recv_sem=rsem.at[p, k], device_id=to, device_id_type=MESH)

        def blk(ref, p, qi, half):
            return _rows(ref, (qi,), half, hs[p]) if split[p] else ref.at[qi]

        return x, y, c, chips, q, sib, rc, blk

    def first_sends(ins, outs, sems):
        x, y, c, chips, q, sib, rc, blk = plan(ins, outs, sems)
        cps = [rc(p, 0, ins[p], outs[p].at[q], sib) for p in range(n)]
        for j, (cx, cy) in enumerate(chips):
            for p in range(n):
                src = _rows(ins[p], (), c, hs[p]) if split[p] else ins[p]
                cps.append(rc(p, 1 + j, src, blk(outs[p], p, q, c), (cx, cy, c)))
        return cps

    def start(ins, outs, sems):
        for cp in first_sends(ins, outs, sems):
            cp.start()

    def finish(ins, outs, sems):
        x, y, c, chips, q, sib, rc, blk = plan(ins, outs, sems)
        sent = first_sends(ins, outs, sems)
        for j, (cx, cy) in enumerate(chips):
            qj = 2 * cx + cy
            for p in range(n):
                got = blk(outs[p], p, qj, c)
                rc(p, 1 + j, got, got, (cx, cy, c)).wait_recv()
                if split[p]:
                    fwd = rc(p, 4 + j, got, got, sib)
                    fwd.start()
                    sent.append(fwd)
        for j, (cx, cy) in enumerate(chips):
            qj = 2 * cx + cy
            for p in range(n):
                if split[p]:
                    got = blk(outs[p], p, qj, 1 - c)
                    rc(p, 4 + j, got, got, sib).wait_recv()
        for p in range(n):
            rc(p, 0, outs[p].at[q], outs[p].at[q], sib).wait_recv()
        for cp in sent:
            cp.wait_send()

    return _Side(shards, [SDS((N_SHARDS,) + w.shape, w.dtype) for w in shards],
                 [pltpu.SemaphoreType.DMA((n, 7)), pltpu.SemaphoreType.DMA((n, 7))], start, finish)


def _side_pair_send(gs):
    n = len(gs)

    def copies(ins, outs, sems):
        x, y, c, _ = _mesh_pos()
        return [pltpu.make_async_remote_copy(
            src_ref=ins[p].at[:, pl.ds((1 - c) * (gs[p].shape[1] // 2), gs[p].shape[1] // 2)], dst_ref=outs[p],
            send_sem=sems[0].at[p], recv_sem=sems[1].at[p], device_id=(x, y, 1 - c), device_id_type=MESH)
            for p in range(n)]

    def start(ins, outs, sems):
        for cp in copies(ins, outs, sems):
            cp.start()

    def finish(ins, outs, sems):
        for cp in copies(ins, outs, sems):
            cp.wait()

    return _Side(gs, [SDS((g.shape[0], g.shape[1] // 2, g.shape[2]), g.dtype) for g in gs],
                 [pltpu.SemaphoreType.DMA((n,)), pltpu.SemaphoreType.DMA((n,))], start, finish)


def _side_scatter(ps):
    n = len(ps)

    def sends(ins, outs, sems):
        x, y, c, chips = _mesh_pos()
        q = 2 * x + y
        return [pltpu.make_async_remote_copy(src_ref=ins[p].at[2 * cx + cy], dst_ref=outs[p].at[q],
                                             send_sem=sems[0].at[p, j], recv_sem=sems[1].at[p, j],
                                             device_id=(cx, cy, c), device_id_type=MESH)
                for j, (cx, cy) in enumerate(chips) for p in range(n)]

    def start(ins, outs, sems):
        for cp in sends(ins, outs, sems):
            cp.start()

    def finish(ins, outs, sems):
        x, y, c, chips = _mesh_pos()
        for j, (cx, cy) in enumerate(chips):
            for p in range(n):
                got = outs[p].at[2 * cx + cy]
                pltpu.make_async_remote_copy(src_ref=got, dst_ref=got, send_sem=sems[0].at[p, j],
                                             recv_sem=sems[1].at[p, j], device_id=(cx, cy, c),
                                             device_id_type=MESH).wait_recv()
        for cp in sends(ins, outs, sems):
            cp.wait_send()

    return _Side(ps, [SDS(p.shape, p.dtype) for p in ps],
                 [pltpu.SemaphoreType.DMA((n, 3)), pltpu.SemaphoreType.DMA((n, 3))], start, finish)


def _side_pair_share(tots):
    n = len(tots)

    def copies(ins, outs, sems):
        x, y, c, _ = _mesh_pos()
        return [pltpu.make_async_remote_copy(src_ref=ins[p], dst_ref=outs[p], send_sem=sems[0].at[p],
                                             recv_sem=sems[1].at[p], device_id=(x, y, 1 - c), device_id_type=MESH)
                for p in range(n)]

    def start(ins, outs, sems):
        for cp in copies(ins, outs, sems):
            cp.start()

    def finish(ins, outs, sems):
        for cp in copies(ins, outs, sems):
            cp.wait()

    return _Side(tots, [SDS(t_.shape, t_.dtype) for t_ in tots],
                 [pltpu.SemaphoreType.DMA((n,)), pltpu.SemaphoreType.DMA((n,))], start, finish)


N_DEVICES = 8


def _side_bcast(v):
    def peers():
        x, y, c, _ = _mesh_pos()
        out = []
        for k in range(1, N_DEVICES):
            px, py, pc = x ^ ((k >> 2) & 1), y ^ ((k >> 1) & 1), c ^ (k & 1)
            out.append((k - 1, (px, py, pc), 4 * px + 2 * py + pc))
        return 4 * x + 2 * y + c, out

    def sends(ins, outs, sems):
        me, ps = peers()
        return [pltpu.make_async_remote_copy(src_ref=ins[0], dst_ref=outs[0].at[me], send_sem=sems[0].at[k],
                                             recv_sem=sems[1].at[k], device_id=to, device_id_type=MESH)
                for k, to, _ in ps]

    def start(ins, outs, sems):
        for cp in sends(ins, outs, sems):
            cp.start()

    def finish(ins, outs, sems):
        _, ps = peers()
        for k, to, slot in ps:
            got = outs[0].at[slot]
            pltpu.make_async_remote_copy(src_ref=got, dst_ref=got, send_sem=sems[0].at[k], recv_sem=sems[1].at[k],
                                         device_id=to, device_id_type=MESH).wait_recv()
        for cp in sends(ins, outs, sems):
            cp.wait_send()

    return _Side([v], [SDS((N_DEVICES,) + v.shape, v.dtype)],
                 [pltpu.SemaphoreType.DMA((N_DEVICES - 1,)), pltpu.SemaphoreType.DMA((N_DEVICES - 1,))], start, finish)


def _sum_slots(v, r, me, name, side=None):
    n, rows, lanes = r.shape
    tr = _tile(rows, 512)

    def body(me_ref, v_ref, r_ref, o_ref):
        j = pl.program_id(1)
        term = jnp.where(j == me_ref[0], v_ref[...], r_ref[0])

        @pl.when(j == 0)
        def _():
            o_ref[...] = term

        @pl.when(j != 0)
        def _():
            o_ref[...] += term

    other = lambda j, k: jnp.where(j == k, (k + 1) % n, j)
    return _pcall(
        body, (v, r), name=name, grid=(rows // tr, n), prefetch=(me,),
        in_specs=[pl.BlockSpec((tr, lanes), lambda i, j, me_ref: (i, 0)),
                  pl.BlockSpec((1, tr, lanes), lambda i, j, me_ref: (other(j, me_ref[0]), i, 0))],
        out_specs=pl.BlockSpec((tr, lanes), lambda i, j, me_ref: (i, 0)),
        out_shape=SDS((rows, lanes), F32), sem=("parallel", "arbitrary"), side=side)


def _pair_sum(g, r1, cq, name, side=None):
    nq, xr, yc = g.shape
    h = xr // 2
    tr = _tile(h, 512, 16)
    nt = h // tr

    def body(cq_ref, g_ref, r_ref, o_ref):
        o_ref[...] = (g_ref[...] + r_ref[...]).astype(BF)

    return _pcall(
        body, (g, r1), name=name, grid=(nq, nt), prefetch=(cq,),
        in_specs=[pl.BlockSpec((1, tr, yc), lambda j, i, cq_ref: (j, cq_ref[0] * nt + i, 0)),
                  pl.BlockSpec((1, tr, yc), lambda j, i, cq_ref: (j, i, 0))],
        out_specs=pl.BlockSpec((1, tr, yc), lambda j, i, cq_ref: (j, i, 0)),
        out_shape=SDS((nq, h, yc), BF), sem=("parallel", "parallel"), side=side)


def _chip_sum(p, r2, cq, name, side=None):
    nq, h, yc = r2.shape
    tr = _tile(h, 512, 16)

    def body(cq_ref, p_ref, r_ref, o_ref):
        j = pl.program_id(1)
        term = jnp.where(j == cq_ref[1], p_ref[0], r_ref[0]).astype(F32)

        @pl.when(j == 0)
        def _():
            o_ref[...] = term

        @pl.when(j != 0)
        def _():
            o_ref[...] += term

    other = lambda j, q: jnp.where(j == q, (q + 1) % nq, j)
    return _pcall(
        body, (p, r2), name=name, grid=(h // tr, nq), prefetch=(cq,),
        in_specs=[pl.BlockSpec((1, tr, yc), lambda i, j, cq_ref: (cq_ref[1], i, 0)),
                  pl.BlockSpec((1, tr, yc), lambda i, j, cq_ref: (other(j, cq_ref[1]), i, 0))],
        out_specs=pl.BlockSpec((tr, yc), lambda i, j, cq_ref: (i, 0)),
        out_shape=SDS((h, yc), F32), sem=("parallel", "arbitrary"), side=side)


def _ffn_up_fwd(x, wup, name, side=None, emit_xt=False):
    t, d = x.shape
    w = wup.shape[2]
    tm = _tile(t, 512)

    def body(x_ref, wg_ref, wu_ref, h_ref, a_ref, at_ref, *xt_ref):
        xb = x_ref[...].astype(BF)
        g = _dot(xb, wg_ref[0])
        u = _dot(xb, wu_ref[0])
        h_ref[0] = g.astype(BF)
        h_ref[1] = u.astype(BF)
        ab = (g * _sigmoid(g) * u).astype(BF)
        a_ref[...] = ab
        at_ref[...] = ab.T
        if emit_xt:
            @pl.when(pl.program_id(0) == 0)
            def _():
                xt_ref[0][...] = xb.T

    nt = t // tm
    out_specs = [pl.BlockSpec((2, tm, w), lambda j, i: (0, i, j)), pl.BlockSpec((tm, w), lambda j, i: (i, j)),
                 pl.BlockSpec((w, tm), lambda j, i: (j, i))]
    out_shape = [SDS((2, t, 2 * w), BF), SDS((t, 2 * w), BF), SDS((2 * w, t), BF)]
    if emit_xt:
        out_specs.append(pl.BlockSpec((d, tm), lambda j, i: (0, jnp.where(j == 0, i, nt - 1))))
        out_shape.append(SDS((d, t), BF))
    return _pcall(
        body, (x, wup, wup), name=name, grid=(2, nt),
        in_specs=[pl.BlockSpec((tm, d), lambda j, i: (i, 0)),
                  pl.BlockSpec((1, d, w), lambda j, i: (j, 0, 0)),
                  pl.BlockSpec((1, d, w), lambda j, i: (j + 2, 0, 0))],
        out_specs=out_specs, out_shape=out_shape,
        sem=("arbitrary", "arbitrary") if emit_xt else ("parallel", "parallel"), side=side)


def _res_ln_fwd(parts, w, x, gamma, beta, alpha, res_scale, name, side=None, parts_t=False, target=None):
    t, d = x.shape
    n = len(parts)
    offs = [0]
    for p in parts:
        offs.append(offs[-1] + p.shape[1])
    tm = _tile(t, 512)
    n_in = n + 4 + (target is not None)

    def body(*refs):
        p_refs = refs[:n]
        w_ref, x_ref, g_ref, b_ref = refs[n:n + 4]
        out = refs[n_in:]
        f = _dot(p_refs[0][...], w_ref[offs[0]:offs[1], :])
        for k in range(1, n):
            f = f + _dot(p_refs[k][...], w_ref[offs[k]:offs[k + 1], :])
        z = alpha * x_ref[...] + res_scale * f
        out[0][...] = z
        xhat, _ = _ln_stats(z)
        y = xhat * g_ref[...] + b_ref[...]
        if target is not None:
            err = y - refs[n + 4][...]
            out[1][...] = err * (1.0 / d)
            part = 0.5 * jnp.sum(jnp.sum(err * err, axis=-1, keepdims=True) * (1.0 / d), axis=0, keepdims=True)
            _acc_rows(out[2], jnp.broadcast_to(part, (8, 128)), pl.program_id(0) == 0)
            return
        out[1][...] = y
        out[2][...] = y.astype(BF).T
        for k in range(len(out) - 3):
            out[3 + k][...] = p_refs[k][...].T

    row = lambda i: (i, 0)
    col = lambda i: (0, i)
    fixed = lambda i: (0, 0)
    operands = [*parts, w, x, gamma, beta]
    in_specs = [pl.BlockSpec((tm, p.shape[1]), row) for p in parts] + [
        pl.BlockSpec(w.shape, fixed), pl.BlockSpec((tm, d), row), pl.BlockSpec((1, d), fixed), pl.BlockSpec((1, d), fixed)]
    if target is not None:
        operands.append(target)
        in_specs.append(pl.BlockSpec((tm, d), row))
        out_specs = [pl.BlockSpec((tm, d), row), pl.BlockSpec((tm, d), row), pl.BlockSpec((8, 128), fixed)]
        out_shape = [SDS((t, d), F32), SDS((t, d), F32), SDS((8, 128), F32)]
    else:
        out_specs = [pl.BlockSpec((tm, d), row), pl.BlockSpec((tm, d), row), pl.BlockSpec((d, tm), col)]
        out_shape = [SDS((t, d), F32), SDS((t, d), F32), SDS((d, t), BF)]
        if parts_t:
            out_specs += [pl.BlockSpec((p.shape[1], tm), col) for p in parts]
            out_shape += [SDS((p.shape[1], t), BF) for p in parts]
    return _pcall(
        body, operands, name=name, grid=(t // tm,), in_specs=in_specs, out_specs=out_specs, out_shape=out_shape,
        sem=("arbitrary",) if target is not None else ("parallel",), side=side)


def _mix_proj_fwd(x, w_main, widths, name, side=None):
    t, d = x.shape
    dc, dq, ds = widths
    tm = _tile(t, 512)

    def body(x_ref, w_ref, pc_ref, pq_ref, ps_ref):
        xb = x_ref[...].astype(BF)
        pc_ref[...] = _dot(xb, w_ref[:, 0:dc])
        pq_ref[...] = _dot(xb, w_ref[:, dc:dc + dq]).astype(BF)
        ps_ref[...] = _dot(xb, w_ref[:, dc + dq:dc + dq + ds])

    row = lambda i: (i, 0)
    return _pcall(
        body, (x, w_main), name=name, grid=(t // tm,),
        in_specs=[pl.BlockSpec((tm, d), row), pl.BlockSpec(w_main.shape, lambda i: (0, 0))],
        out_specs=[pl.BlockSpec((tm, dc), row), pl.BlockSpec((tm, dq), row), pl.BlockSpec((tm, ds), row)],
        out_shape=[SDS((t, dc), F32), SDS((t, dq), BF), SDS((t, ds), F32)],
        sem=("parallel",), side=side)


def _prefix_sum_lanes(v, reverse):
    n = v.shape[-1]
    lane = lax.broadcasted_iota(jnp.int32, v.shape, v.ndim - 1)
    sh = 1
    while sh < n:
        if reverse:
            v = v + jnp.where(lane < n - sh, pltpu.roll(v, n - sh, axis=v.ndim - 1), 0.0)
        else:
            v = v + jnp.where(lane >= sh, pltpu.roll(v, sh, axis=v.ndim - 1), 0.0)
        sh *= 2
    return v


def _cum_fwd(x3, wft, bf, name, side=None):
    b, s, d = x3.shape
    h = bf.shape[0]

    def body(x_ref, w_ref, b_ref, fl_ref, cum_ref):
        fl = _dotg(w_ref[...], x_ref[0].astype(BF), NT)[0:h] + b_ref[...]
        fl_ref[0] = fl
        lf = jnp.minimum(fl, 0.0) - jnp.log(1.0 + jnp.exp(-jnp.abs(fl)))
        cum_ref[0] = _prefix_sum_lanes(lf, reverse=False)

    return _pcall(
        body, (x3, wft, bf), name=name, grid=(b,),
        in_specs=[pl.BlockSpec((1, s, d), lambda i: (i, 0, 0)),
                  pl.BlockSpec(wft.shape, lambda i: (0, 0)), pl.BlockSpec((h, 1), lambda i: (0, 0))],
        out_specs=[pl.BlockSpec((1, h, s), lambda i: (i, 0, 0)), pl.BlockSpec((1, h, s), lambda i: (i, 0, 0))],
        out_shape=[SDS((b, h, s), F32), SDS((b, h, s), F32)],
        sem=("parallel",), side=side)


def _shift_rows(z, k, down):
    n = z.shape[0]
    row = lax.broadcasted_iota(jnp.int32, z.shape, 0)
    if down:
        return jnp.where(row >= k, pltpu.roll(z, k, axis=0), 0.0)
    return jnp.where(row < n - k, pltpu.roll(z, n - k, axis=0), 0.0)


def _conv_fwd(pc3, cw, name, side=None):
    b, s, c3 = pc3.shape
    c = c3 // 3

    def body(p_ref, w_ref, y_ref):
        z = p_ref[0, :, c:2 * c] * p_ref[0, :, 2 * c:3 * c]
        conv = w_ref[0:1, :] * _shift_rows(z, 2, True) + w_ref[1:2, :] * _shift_rows(z, 1, True) + w_ref[2:3, :] * z
        y_ref[0] = (p_ref[0, :, 0:c] * conv).astype(BF)

    return _pcall(
        body, (pc3, cw), name=name, grid=(b,),
        in_specs=[pl.BlockSpec((1, s, c3), lambda i: (i, 0, 0)), pl.BlockSpec((3, c), lambda i: (0, 0))],
        out_specs=pl.BlockSpec((1, s, c), lambda i: (i, 0, 0)),
        out_shape=SDS((b, s, c), BF), sem=("parallel",), side=side)


def _head_masks(width):
    lane = lax.broadcasted_iota(jnp.int32, (1, width), 1)
    return [lane < FOX_HEAD_DIM, lane >= FOX_HEAD_DIM]


def _fox_scores(q, k, cum_row, lo, head_mask):
    tq = q.shape[0]
    qm = jnp.where(head_mask, q * (FOX_HEAD_DIM ** -0.5), 0)
    s = _dotg(qm, k, NT) - cum_row
    tri = lax.broadcasted_iota(jnp.int32, (tq, tq), 1) <= lax.broadcasted_iota(jnp.int32, (tq, tq), 0)
    diag = jnp.where(tri, s[:, lo:], NEG_BIG)
    return (diag if lo == 0 else jnp.concatenate([s[:, :lo], diag], axis=1)), qm


def _fox_fwd(pq3, cum4, name, side=None):
    b, s, d3 = pq3.shape
    df = d3 // 3
    hp = df // 128
    tq = _tile(s, FOX_Q_BLOCK)

    def body(q_ref, k_ref, v_ref, c_ref, o_ref, lse_ref):
        masks = _head_masks(128)
        for i in range(s // tq):
            lo, hi = i * tq, (i + 1) * tq
            q = q_ref[0, lo:hi, :]
            k = k_ref[0, 0:hi, :]
            v = v_ref[0, 0:hi, :]
            o = jnp.zeros((tq, 128), F32)
            lse = jnp.zeros((tq, 128), F32)
            for e in range(2):
                sc, _ = _fox_scores(q, k, c_ref[0, 0, e:e + 1, 0:hi], lo, masks[e])
                m = jnp.max(sc, axis=-1, keepdims=True)
                p = jnp.exp(sc - m)
                l = jnp.sum(p, axis=-1, keepdims=True)
                o = jnp.where(masks[e], _dot(p.astype(BF), v) * (1.0 / l), o)
                lse = jnp.where(masks[e], m + jnp.log(l), lse)
            o_ref[0, lo:hi, :] = o.astype(BF)
            lse_ref[0, 0, lo:hi, :] = lse

    blk = lambda off: pl.BlockSpec((1, s, 128), lambda i, j: (i, 0, off + j))
    return _pcall(
        body, (pq3, pq3, pq3, cum4), name=name, grid=(b, hp),
        in_specs=[blk(0), blk(hp), blk(2 * hp), pl.BlockSpec((1, 1, 2, s), lambda i, j: (i, j, 0, 0))],
        out_specs=[blk(0), pl.BlockSpec((1, 1, s, 128), lambda i, j: (i, j, 0, 0))],
        out_shape=[SDS((b, s, df), BF), SDS((b, hp, s, 128), F32)],
        sem=("parallel", "parallel"), side=side)


def _sgu_mix(wm, vnb, bias, gmasks):
    out = bias
    for g in range(len(wm)):
        out = out + jnp.where(gmasks[g], _dot(wm[g], vnb), 0.0)
    return out


def _sgu_consts(ws_ref, bs_ref, ds):
    ng, c, _ = ws_ref.shape
    gd = ds // ng
    tri = lax.broadcasted_iota(jnp.int32, (c, c), 0) >= lax.broadcasted_iota(jnp.int32, (c, c), 1)
    lane = lax.broadcasted_iota(jnp.int32, (1, ds), 1)
    gmasks = [(lane >= g * gd) & (lane < (g + 1) * gd) for g in range(ng)]
    wm = [jnp.where(tri, ws_ref[g], 0.0).astype(BF) for g in range(ng)]
    bias = jnp.zeros((c, ds), F32)
    for g in range(ng):
        bias = jnp.where(gmasks[g], bs_ref[g], bias)
    return tri, gmasks, wm, bias


def _sgu_fwd(ps, lng, lnb, ws, bs, name, side=None):
    t, ds2 = ps.shape
    ds = ds2 // 2
    c = ws.shape[1]
    tm = _tile(t, 512, c)

    def body(p_ref, g_ref, b_ref, ws_ref, bs_ref, y_ref):
        _, gmasks, wm, bias = _sgu_consts(ws_ref, bs_ref, ds)
        up = _gelu(p_ref[:, 0:ds])
        xhat, _ = _ln_stats(_gelu(p_ref[:, ds:ds2]))
        vnb = (xhat * g_ref[...] + b_ref[...]).astype(BF)
        for n in range(tm // c):
            r0, r1 = n * c, (n + 1) * c
            y_ref[r0:r1, :] = (up[r0:r1] * _sgu_mix(wm, vnb[r0:r1], bias, gmasks)).astype(BF)

    fixed2 = lambda i: (0, 0)
    fixed3 = lambda i: (0, 0, 0)
    return _pcall(
        body, (ps, lng, lnb, ws, bs), name=name, grid=(t // tm,),
        in_specs=[pl.BlockSpec((tm, ds2), lambda i: (i, 0)), pl.BlockSpec((1, ds), fixed2),
                  pl.BlockSpec((1, ds), fixed2), pl.BlockSpec(ws.shape, fixed3), pl.BlockSpec(bs.shape, fixed3)],
        out_specs=pl.BlockSpec((tm, ds), lambda i: (i, 0)),
        out_shape=SDS((t, ds), BF), sem=("parallel",), side=side)


def _acc_rows(ref, val, first):
    @pl.when(first)
    def _():
        ref[...] = val

    @pl.when(jnp.logical_not(first))
    def _():
        ref[...] += val


def _ffn_bwd_mid(dy, z, gamma, wd, h, name, side=None):
    t, d = dy.shape
    dff = wd.shape[0]
    half = dff // 2
    tm = _tile(t, 512)

    def body(dy_ref, z_ref, g_ref, wd_ref, h_ref, dz_ref, df_ref, dh_ref, dg_ref, db_ref):
        dz, dgam, dbet = _ln_bwd(dy_ref[...], z_ref[...], g_ref[...])
        first = pl.program_id(0) == 0
        _acc_rows(dg_ref, dgam, first)
        _acc_rows(db_ref, dbet, first)
        dz_ref[...] = dz
        dfb = (0.5 * dz).astype(BF)
        df_ref[...] = dfb
        for j in range(2):
            c0, c1 = j * half, (j + 1) * half
            da = _dotg(dfb, wd_ref[c0:c1, :], NT).astype(BF)
            g = h_ref[0, :, c0:c1]
            u = h_ref[1, :, c0:c1]
            sg = _sigmoid(g)
            dh_ref[0, :, c0:c1] = da * u * sg * (1.0 + g * (1.0 - sg))
            dh_ref[1, :, c0:c1] = da * g * sg

    row = lambda i: (i, 0)
    fixed = lambda i: (0, 0)
    return _pcall(
        body, (dy, z, gamma, wd, h), name=name, grid=(t // tm,),
        in_specs=[pl.BlockSpec((tm, d), row), pl.BlockSpec((tm, d), row), pl.BlockSpec((1, d), fixed),
                  pl.BlockSpec(wd.shape, fixed, pipeline_mode=pl.Buffered(1)),
                  pl.BlockSpec((2, tm, dff), lambda i: (0, i, 0))],
        out_specs=[pl.BlockSpec((tm, d), row), pl.BlockSpec((tm, d), row),
                   pl.BlockSpec((2, tm, dff), lambda i: (0, i, 0)),
                   pl.BlockSpec((1, d), fixed), pl.BlockSpec((1, d), fixed)],
        out_shape=[SDS((t, d), F32), SDS((t, d), BF), SDS((2, t, dff), BF), SDS((1, d), F32), SDS((1, d), F32)],
        sem=("arbitrary",), side=side)


def _ffn_bwd_dx(dh, wup, dz, alpha, name, side=None):
    _, t, dff = dh.shape
    nq, d, w = wup.shape
    per = dff // w
    tm = _tile(t, 512)

    def body(dh_ref, w_ref, dz_ref, dx_ref):
        acc = alpha * dz_ref[...]
        for q in range(nq):
            c0 = (q % per) * w
            acc = acc + _dotg(dh_ref[q // per, :, c0:c0 + w], w_ref[q], NT)
        dx_ref[...] = acc

    row = lambda i: (i, 0)
    return _pcall(
        body, (dh, wup, dz), name=name, grid=(t // tm,),
        in_specs=[pl.BlockSpec((2, tm, dff), lambda i: (0, i, 0)),
                  pl.BlockSpec(wup.shape, lambda i: (0, 0, 0), pipeline_mode=pl.Buffered(1)),
                  pl.BlockSpec((tm, d), row)],
        out_specs=pl.BlockSpec((tm, d), row), out_shape=SDS((t, d), F32),
        sem=("parallel",), side=side)


def _dw(at, b3, ka, nb, name, side=None):
    ka_tot, t = at.shape
    gb, _, nb_tot = b3.shape
    na, ncb = ka_tot // ka, nb_tot // nb
    tm = _tile(t, DW_TOKENS, 128)

    def body(a_ref, b_ref, o_ref):
        part = _dot(a_ref[...], b_ref[0])

        @pl.when(pl.program_id(2) == 0)
        def _():
            o_ref[0, 0] = part

        @pl.when(pl.program_id(2) != 0)
        def _():
            o_ref[0, 0] += part

    return _pcall(
        body, (at, b3), name=name, grid=(na, gb * ncb, t // tm),
        in_specs=[pl.BlockSpec((ka, tm), lambda ja, jb, i: (ja, i)),
                  pl.BlockSpec((1, tm, nb), lambda ja, jb, i: (jb // ncb, i, jb % ncb))],
        out_specs=pl.BlockSpec((1, 1, ka, nb), lambda ja, jb, i: (ja, jb, 0, 0)),
        out_shape=SDS((na, gb * ncb, ka, nb), F32),
        sem=("parallel", "parallel", "arbitrary"), side=side)


def _dw_groups(ats, bs, tokens, name, side=None):
    t = bs[0].shape[0]
    na, nb = len(ats), len(bs)
    roff, coff = [0], [0]
    for a in ats:
        roff.append(roff[-1] + a.shape[0])
    for b in bs:
        coff.append(coff[-1] + b.shape[1])
    tm = _tile(t, tokens, 128)

    def body(*refs):
        a_refs, b_refs, o_ref = refs[:na], refs[na:na + nb], refs[na + nb]
        first = pl.program_id(0) == 0
        for i in range(na):
            for j in range(nb):
                _acc_rows(o_ref.at[roff[i]:roff[i + 1], coff[j]:coff[j + 1]], _dot(a_refs[i][...], b_refs[j][...]), first)

    return _pcall(
        body, (*ats, *bs), name=name, grid=(t // tm,),
        in_specs=[pl.BlockSpec((a.shape[0], tm), lambda i: (0, i)) for a in ats]
        + [pl.BlockSpec((tm, b.shape[1]), lambda i: (i, 0)) for b in bs],
        out_specs=pl.BlockSpec((roff[-1], coff[-1]), lambda i: (0, 0)),
        out_shape=SDS((roff[-1], coff[-1]), F32), sem=("arbitrary",), side=side)


def _out_bwd(dy, z, gamma, wout, widths, name, side=None):
    t, d = dy.shape
    wa, wb, wc = widths
    tm = _tile(t, 512)

    def body(dy_ref, z_ref, g_ref, w_ref, dz_ref, dzb_ref, da_ref, dbb_ref, dc_ref, dg_ref, db_ref):
        dz, dgam, dbet = _ln_bwd(dy_ref[...], z_ref[...], g_ref[...])
        first = pl.program_id(0) == 0
        _acc_rows(dg_ref, dgam, first)
        _acc_rows(db_ref, dbet, first)
        dz_ref[...] = dz
        dzb = dz.astype(BF)
        dzb_ref[...] = dzb
        da_ref[...] = _dotg(dzb, w_ref[0:wa, :], NT).astype(BF)
        dbb_ref[...] = _dotg(dzb, w_ref[wa:wa + wb, :], NT).astype(BF)
        dc_ref[...] = _dotg(dzb, w_ref[wa + wb:wa + wb + wc, :], NT).astype(BF)

    row = lambda i: (i, 0)
    fixed = lambda i: (0, 0)
    return _pcall(
        body, (dy, z, gamma, wout), name=name, grid=(t // tm,),
        in_specs=[pl.BlockSpec((tm, d), row), pl.BlockSpec((tm, d), row), pl.BlockSpec((1, d), fixed),
                  pl.BlockSpec(wout.shape, fixed)],
        out_specs=[pl.BlockSpec((tm, d), row), pl.BlockSpec((tm, d), row), pl.BlockSpec((tm, wa), row),
                   pl.BlockSpec((tm, wb), row), pl.BlockSpec((tm, wc), row),
                   pl.BlockSpec((1, d), fixed), pl.BlockSpec((1, d), fixed)],
        out_shape=[SDS((t, d), F32), SDS((t, d), BF), SDS((t, wa), BF), SDS((t, wb), BF), SDS((t, wc), BF),
                   SDS((1, d), F32), SDS((1, d), F32)],
        sem=("arbitrary",), side=side)


def _conv_bwd(pc3, dya3, cw, name, side=None):
    b, s, c3 = pc3.shape
    c = c3 // 3

    def body(p_ref, dy_ref, w_ref, dp_ref, dw_ref):
        cb = p_ref[0, :, 0:c]
        cc = p_ref[0, :, c:2 * c]
        ch = p_ref[0, :, 2 * c:3 * c]
        z = cc * ch
        z1 = _shift_rows(z, 1, True)
        z2 = _shift_rows(z, 2, True)
        w0, w1, w2 = w_ref[0:1, :], w_ref[1:2, :], w_ref[2:3, :]
        dy = dy_ref[0].astype(F32)
        dconv = dy * cb
        dz = w2 * dconv + w1 * _shift_rows(dconv, 1, False) + w0 * _shift_rows(dconv, 2, False)
        dp_ref[0, :, 0:c] = (dy * (w0 * z2 + w1 * z1 + w2 * z)).astype(BF)
        dp_ref[0, :, c:2 * c] = (dz * ch).astype(BF)
        dp_ref[0, :, 2 * c:3 * c] = (dz * cc).astype(BF)
        first = pl.program_id(0) == 0
        for r, zs in enumerate((z2, z1, z)):
            _acc_rows(dw_ref.at[r:r + 1], jnp.sum(dconv * zs, axis=0, keepdims=True), first)

    blk = lambda i: (i, 0, 0)
    return _pcall(
        body, (pc3, dya3, cw), name=name, grid=(b,),
        in_specs=[pl.BlockSpec((1, s, c3), blk), pl.BlockSpec((1, s, c), blk), pl.BlockSpec((3, c), lambda i: (0, 0))],
        out_specs=[pl.BlockSpec((1, s, c3), blk), pl.BlockSpec((3, c), lambda i: (0, 0))],
        out_shape=[SDS((b, s, c3), BF), SDS((3, c), F32)],
        sem=("arbitrary",), side=side)


def _fox_bwd(pq3, cum4, lse4, dyb3, name, side=None):
    b, s, d3 = pq3.shape
    df = d3 // 3
    hp = df // 128
    tq = _tile(s, FOX_Q_BLOCK)
    scale = FOX_HEAD_DIM ** -0.5

    def body(q_ref, k_ref, v_ref, c_ref, lse_ref, do_ref, dq_ref, dk_ref, dv_ref, dc_ref, dk_acc, dv_acc):
        masks = _head_masks(128)
        dk_acc[...] = jnp.zeros_like(dk_acc)
        dv_acc[...] = jnp.zeros_like(dv_acc)
        dc_ref[...] = jnp.zeros_like(dc_ref)
        for i in range(s // tq):
            lo, hi = i * tq, (i + 1) * tq
            q = q_ref[0, lo:hi, :]
            do = do_ref[0, lo:hi, :]
            k = k_ref[0, 0:hi, :]
            v = v_ref[0, 0:hi, :]
            lse = lse_ref[0, 0, lo:hi, :]
            dq = jnp.zeros((tq, 128), F32)
            for e in range(2):
                dom = jnp.where(masks[e], do, 0)
                sc, qm = _fox_scores(q, k, c_ref[0, 0, e:e + 1, 0:hi], lo, masks[e])
                p = jnp.exp(sc - lse[:, FOX_HEAD_DIM * e:FOX_HEAD_DIM * e + 1])
                dp = _dotg(dom, v, NT)
                ds = p * (dp - jnp.sum(p * dp, axis=-1, keepdims=True))
                dsb = ds.astype(BF)
                dq = jnp.where(masks[e], _dot(dsb, k) * scale, dq)
                dk_acc[0:hi, :] += _dotg(dsb, qm, TN)
                dv_acc[0:hi, :] += _dotg(p.astype(BF), dom, TN)
                dc_ref[0, 0, e:e + 1, 0:hi] -= jnp.sum(ds, axis=0, keepdims=True)
            dq_ref[0, lo:hi, :] = dq.astype(BF)
        dk_ref[0] = dk_acc[...].astype(BF)
        dv_ref[0] = dv_acc[...].astype(BF)

    blk = lambda off: pl.BlockSpec((1, s, 128), lambda i, j: (i, 0, off + j))
    cblk = pl.BlockSpec((1, 1, 2, s), lambda i, j: (i, j, 0, 0))
    return _pcall(
        body, (pq3, pq3, pq3, cum4, lse4, dyb3), name=name, grid=(b, hp),
        in_specs=[blk(0), blk(hp), blk(2 * hp), cblk,
                  pl.BlockSpec((1, 1, s, 128), lambda i, j: (i, j, 0, 0)), blk(0)],
        out_specs=[blk(0), blk(0), blk(0), cblk],
        out_shape=[SDS((b, s, df), BF), SDS((b, s, df), BF), SDS((b, s, df), BF), SDS(cum4.shape, F32)],
        scratch_shapes=[pltpu.VMEM((s, 128), F32), pltpu.VMEM((s, 128), F32)],
        sem=("parallel", "parallel"), side=side)


def _cum_bwd(dcum, flog, xt, name, side=None):
    b, h, s = dcum.shape
    d = xt.shape[0]

    def body(dc_ref, fl_ref, xt_ref, dfl_ref, dbf_ref, dwf_ref):
        dfl = _prefix_sum_lanes(dc_ref[0], reverse=True) * _sigmoid(-fl_ref[0])
        dfl_ref[0] = dfl
        first = pl.program_id(0) == 0
        _acc_rows(dbf_ref, jnp.broadcast_to(jnp.sum(dfl, axis=-1, keepdims=True), (h, 128)), first)
        dflp = jnp.concatenate([dfl, jnp.zeros((HEAD_ROWS - h, s), F32)], axis=0).astype(BF)
        _acc_rows(dwf_ref, _dotg(dflp, xt_ref[...], NT)[0:h], first)

    blk = lambda i: (i, 0, 0)
    return _pcall(
        body, (dcum, flog, xt), name=name, grid=(b,),
        in_specs=[pl.BlockSpec((1, h, s), blk), pl.BlockSpec((1, h, s), blk), pl.BlockSpec((d, s), lambda i: (0, i))],
        out_specs=[pl.BlockSpec((1, h, s), blk), pl.BlockSpec((h, 128), lambda i: (0, 0)),
                   pl.BlockSpec((h, d), lambda i: (0, 0))],
        out_shape=[SDS((b, h, s), F32), SDS((h, 128), F32), SDS((h, d), F32)],
        sem=("arbitrary",), side=side)


def _sgu_bwd(ps, dyc, lng, lnb, ws, bs, name, side=None):
    t, ds2 = ps.shape
    ds = ds2 // 2
    ng, c, _ = ws.shape
    tm = _tile(t, 512, c)

    def body(p_ref, dy_ref, g_ref, b_ref, ws_ref, bs_ref, dp_ref, dws_ref, dbs_ref, dg_ref, db_ref, dvn_acc):
        tri, gmasks, wm, bias = _sgu_consts(ws_ref, bs_ref, ds)
        su = p_ref[:, 0:ds]
        sv = p_ref[:, ds:ds2]
        up = _gelu(su)
        gv = _gelu(sv)
        xhat, rstd = _ln_stats(gv)
        vnb = (xhat * g_ref[...] + b_ref[...]).astype(BF)
        dy = dy_ref[...].astype(F32)
        dws = [jnp.zeros((c, c), F32) for _ in range(ng)]
        dbs = [jnp.zeros((c, 1), F32) for _ in range(ng)]
        for n in range(tm // c):
            r0, r1 = n * c, (n + 1) * c
            mixed = _sgu_mix(wm, vnb[r0:r1], bias, gmasks)
            dp_ref[r0:r1, 0:ds] = (dy[r0:r1] * mixed * _gelu_grad(su[r0:r1])).astype(BF)
            dmix = dy[r0:r1] * up[r0:r1]
            dvn = jnp.zeros((c, ds), F32)
            for g in range(ng):
                dmg = jnp.where(gmasks[g], dmix, 0.0)
                dmb = dmg.astype(BF)
                dws[g] = dws[g] + _dotg(dmb, vnb[r0:r1], NT)
                dbs[g] = dbs[g] + jnp.sum(dmg, axis=-1, keepdims=True)
                dvn = dvn + _dotg(wm[g], dmb, TN)
            dvn_acc[r0:r1, :] = dvn
        dvn_all = dvn_acc[...]
        gdv = dvn_all * g_ref[...]
        m1 = jnp.mean(gdv, axis=-1, keepdims=True)
        m2 = jnp.mean(gdv * xhat, axis=-1, keepdims=True)
        dgv = rstd * (gdv - m1 - xhat * m2)
        dp_ref[:, ds:ds2] = (dgv * _gelu_grad(sv)).astype(BF)
        first = pl.program_id(0) == 0
        _acc_rows(dg_ref, jnp.sum(dvn_all * xhat, axis=0, keepdims=True), first)
        _acc_rows(db_ref, jnp.sum(dvn_all, axis=0, keepdims=True), first)
        for g in range(ng):
            _acc_rows(dws_ref.at[g], jnp.where(tri, dws[g], 0.0), first)
            _acc_rows(dbs_ref.at[g], dbs[g], first)

    row = lambda i: (i, 0)
    fixed2 = lambda i: (0, 0)
    fixed3 = lambda i: (0, 0, 0)
    return _pcall(
        body, (ps, dyc, lng, lnb, ws, bs), name=name, grid=(t // tm,),
        in_specs=[pl.BlockSpec((tm, ds2), row), pl.BlockSpec((tm, ds), row), pl.BlockSpec((1, ds), fixed2),
                  pl.BlockSpec((1, ds), fixed2), pl.BlockSpec(ws.shape, fixed3), pl.BlockSpec(bs.shape, fixed3)],
        out_specs=[pl.BlockSpec((tm, ds2), row), pl.BlockSpec(ws.shape, fixed3), pl.BlockSpec(bs.shape, fixed3),
                   pl.BlockSpec((1, ds), fixed2), pl.BlockSpec((1, ds), fixed2)],
        out_shape=[SDS((t, ds2), BF), SDS(ws.shape, F32), SDS(bs.shape, F32), SDS((1, ds), F32), SDS((1, ds), F32)],
        scratch_shapes=[pltpu.VMEM((tm, ds), F32)],
        sem=("arbitrary",), side=side)


def _mix_bwd_dx(dz, dconv, dq, dk, dv, dsgu, dflog, w_main, wft, seq, alpha, name, side=None):
    t, d = dz.shape
    groups = [dconv, dq, dk, dv, dsgu]
    offs = [0]
    for g in groups:
        offs.append(offs[-1] + g.shape[1])
    h = dflog.shape[1]
    tm = _tile(seq, 512)
    per_seq = seq // tm

    def body(dz_ref, a0, a1, a2, a3, a4, dfl_ref, w_ref, wf_ref, dx_ref):
        dflp = jnp.concatenate([dfl_ref[0], jnp.zeros((HEAD_ROWS - h, tm), F32)], axis=0).astype(BF)
        acc = alpha * dz_ref[...] + _dotg(dflp, wf_ref[...], TN)
        for k, a_ref in enumerate((a0, a1, a2, a3, a4)):
            acc = acc + _dotg(a_ref[...], w_ref[:, offs[k]:offs[k + 1]], NT)
        dx_ref[...] = acc

    row = lambda i: (i, 0)
    return _pcall(
        body, (dz, *groups, dflog, w_main, wft), name=name, grid=(t // tm,),
        in_specs=[pl.BlockSpec((tm, d), row)] + [pl.BlockSpec((tm, g.shape[1]), row) for g in groups]
        + [pl.BlockSpec((1, h, tm), lambda i: (i // per_seq, 0, i % per_seq)),
           pl.BlockSpec(w_main.shape, lambda i: (0, 0)), pl.BlockSpec(wft.shape, lambda i: (0, 0))],
        out_specs=pl.BlockSpec((tm, d), row), out_shape=SDS((t, d), F32),
        sem=("parallel",), side=side)


def _adam_math(w, g, m, v):
    c1 = 1.0 / (1.0 - ADAM_B1 ** ADAM_STEP)
    c2 = 1.0 / (1.0 - ADAM_B2 ** ADAM_STEP)
    nm = ADAM_B1 * m + (1.0 - ADAM_B1) * g
    nv = ADAM_B2 * v + (1.0 - ADAM_B2) * (g * g)
    delta = -ADAM_LR * ((nm * c1) / (jnp.sqrt(nv * c2) + ADAM_EPS) + ADAM_WD * w)
    return delta, nm, nv


def _adamw_small(w, g, m, v, name):
    r, c = w.shape
    tr = _tile(r, 512)

    def body(w_ref, g_ref, m_ref, v_ref, d_ref, nm_ref, nv_ref):
        d_ref[...], nm_ref[...], nv_ref[...] = _adam_math(w_ref[...], g_ref[...], m_ref[...], v_ref[...])

    blk = pl.BlockSpec((tr, c), lambda i: (i, 0))
    return _pcall(body, (w, g, m, v), name=name, grid=(r // tr,), in_specs=[blk] * 4, out_specs=[blk] * 3,
                  out_shape=[SDS((r, c), F32)] * 3, sem=("parallel",))


def _adamw_shard(w, m, v, tot, recv, cq, layer, prev, name, side=None):
    nl, xr, yc = w.shape
    h = xr // 2
    tr = _tile(h, 256)
    nt = h // tr

    def body(cq_ref, w_ref, m_ref, v_ref, t_ref, r_ref, *rest):
        g_ref, d_ref, nm_ref, nv_ref = rest[-4:]
        g = jnp.where(pl.program_id(0) == cq_ref[0], t_ref[...], r_ref[...])
        g_ref[0] = g
        d_ref[0], nm_ref[0], nv_ref[0] = _adam_math(w_ref[0], g, m_ref[0], v_ref[0])

    slab = pl.BlockSpec((1, tr, yc), lambda hf, i, cq_ref: (layer, hf * nt + i, 0))
    mine = pl.BlockSpec((tr, yc), lambda hf, i, cq_ref: (jnp.where(hf == cq_ref[0], i, 0), 0))
    theirs = pl.BlockSpec((tr, yc), lambda hf, i, cq_ref: (jnp.where(hf == cq_ref[0], 0, i), 0))
    operands = [w, m, v, tot, recv]
    in_specs = [slab, slab, slab, mine, theirs]
    aliases = None
    if prev is not None:
        operands += list(prev)
        in_specs += [HBM] * 4
        aliases = {6 + k: k for k in range(4)}
    return _pcall(body, operands, name=name, grid=(2, nt), prefetch=(cq,), in_specs=in_specs,
                  out_specs=[slab] * 4, out_shape=[SDS(w.shape, F32)] * 4, aliases=aliases,
                  sem=("parallel", "parallel"), side=side)


BIG = ("ffn1_w_up", "ffn1_w_down", "mix_w_in", "mix_w_out", "ffn2_w_up", "ffn2_w_down")
SMALL = ("ln1_g", "ln1_b", "fox_b_f", "sgu_ln_g", "sgu_ln_b", "sgu_w_s", "sgu_b_s", "ln2_g", "ln2_b", "ln3_g", "ln3_b")
ORDER = ("ln1_g", "ln1_b", "ffn1_w_up", "ffn1_w_down", "mix_w_in", "fox_b_f", "conv_w", "sgu_ln_g", "sgu_ln_b",
         "sgu_w_s", "sgu_b_s", "mix_w_out", "ln2_g", "ln2_b", "ffn2_w_up", "ffn2_w_down", "ln3_g", "ln3_b")


def _row(v):
    return v.reshape(1, -1)


class _Pipe:
    def __init__(self, stages):
        self.stages = list(stages)
        self.pos = 0
        self.last = None

    def kind(self):
        return self.stages[self.pos][0] if self.pos < len(self.stages) else None


class _Sched:
    def __init__(self):
        self.pipes = []
        self.n_alone = 0

    def add(self, stages):
        self.pipes.append(_Pipe(stages))

    def _take_comms(self, skip=None):
        jobs = []
        for p in self.pipes:
            if p is not skip and p.kind() == "comm":
                jobs.append((p, p.stages[p.pos][1]()))
        return jobs

    @staticmethod
    def _landed(jobs):
        for p, side in jobs:
            p.last = side.results
            p.pos += 1

    def carry(self, builder, *args, **kw):
        jobs = self._take_comms()
        res = builder(*args, side=_join([s for _, s in jobs]), **kw)
        self._landed(jobs)
        self._computes(ride=False)
        return res

    def _computes(self, ride):
        again = True
        while again:
            again = False
            for p in self.pipes:
                if p.kind() == "compute":
                    jobs = self._take_comms(skip=p) if ride else []
                    p.stages[p.pos][1](p.last, _join([s for _, s in jobs]))
                    p.pos += 1
                    self._landed(jobs)
                    again = True

    def drain(self):
        while any(p.kind() is not None for p in self.pipes):
            self._computes(ride=True)
            jobs = self._take_comms()
            if jobs:
                _run_side(_join([s for _, s in jobs]), "exchange_tail_%d" % self.n_alone)
                self.n_alone += 1
                self._landed(jobs)


def _forward_layer(x, xt, p, dims, alpha, l, ride, target=None):
    b, s = dims["b"], dims["s"]
    t, d = x.shape
    tag = "l%d_" % l

    def run(stage, builder, *args, **kw):
        side, on_done = ride.get(stage, (None, None))
        res = builder(*args, tag + stage, side=side, **kw)
        if on_done is not None:
            on_done()
        return res

    if xt is None:
        h1, a1, a1t, xt = run("ffn1_up", _ffn_up_fwd, x, p["wup1"], emit_xt=True)
    else:
        h1, a1, a1t = run("ffn1_up", _ffn_up_fwd, x, p["wup1"])
    z1, x1, x1t = run("ffn1_down", _res_ln_fwd, [a1], p["wd1"], x, p["ln1_g"], p["ln1_b"], alpha, 0.5)
    pc, pq, ps = run("mix_proj", _mix_proj_fwd, x1, p["win"], dims["proj_widths"])
    x1_3 = x1.reshape(b, s, d)
    flog, cum = run("fox_gate", _cum_fwd, x1_3, p["wft"], p["bf"])
    nh = flog.shape[1]
    cum4 = cum.reshape(b, nh // 2, 2, s)
    pc3 = pc.reshape(b, s, -1)
    pq3 = pq.reshape(b, s, -1)
    ya = run("conv", _conv_fwd, pc3, p["cw"]).reshape(t, -1)
    yb3, lse4 = run("fox", _fox_fwd, pq3, cum4)
    yb = yb3.reshape(t, -1)
    yc = run("sgu", _sgu_fwd, ps, p["sgu_g"], p["sgu_b"], p["ws"], p["bs"])
    z2, x2, x2t, yat, ybt, yct = run("mix_out", _res_ln_fwd, [ya, yb, yc], p["wout"], x1, p["ln2_g"], p["ln2_b"],
                                      alpha, 1.0, parts_t=True)
    h2, a2, a2t = run("ffn2_up", _ffn_up_fwd, x2, p["wup2"])
    z3, x3, x3t = run("ffn2_down", _res_ln_fwd, [a2], p["wd2"], x2, p["ln3_g"], p["ln3_b"], alpha, 0.5, target=target)
    saved = dict(xt=xt, h1=h1, a1t=a1t, z1=z1, x1=x1, x1t=x1t, pc3=pc3, pq3=pq3, ps=ps, flog=flog, cum4=cum4,
                 lse4=lse4, yat=yat, ybt=ybt, yct=yct, z2=z2, x2t=x2t, h2=h2, a2t=a2t, z3=z3)
    return x3, x3t, saved


def _ffn_backward(sched, emit, which, dy, z, gamma, wd, wup, h, a_t, x_in_t, alpha, tag, after_mid=None):
    dz, df, dh, dgam, dbet = sched.carry(_ffn_bwd_mid, dy, z, gamma, wd, h, tag + "_bwd_mid")
    if after_mid is not None:
        after_mid(dgam, dbet)
    nq, d, w = wup.shape
    emit(which + "_w_up", sched.carry(_dw, x_in_t, dh, d, w, tag + "_dw_up")[0])
    half = wd.shape[0] // 2
    emit(which + "_w_down", sched.carry(_dw, a_t, df[None], half, d, tag + "_dw_down").reshape(nq, -1, d))
    dx = sched.carry(_ffn_bwd_dx, dh, wup, dz, alpha, tag + "_bwd_dx")
    return dx, dgam, dbet


def _backward_layer(sched, emit, emit_small, dy, sv, p, dims, alpha, l):
    b, s = dims["b"], dims["s"]
    tag = "l%d_" % l
    t, d = dy.shape
    g = {}
    dx2, g["ln3_g"], g["ln3_b"] = _ffn_backward(sched, emit, "ffn2", dy, sv["z3"], p["ln3_g"], p["wd2"], p["wup2"],
                                                sv["h2"], sv["a2t"], sv["x2t"], alpha, tag + "ffn2")
    wa, wb, wc = sv["yat"].shape[0], sv["ybt"].shape[0], sv["yct"].shape[0]
    dz2, dz2b, dya, dyb, dyc, g["ln2_g"], g["ln2_b"] = sched.carry(
        _out_bwd, dx2, sv["z2"], p["ln2_g"], p["wout"], (wa, wb, wc), tag + "mix_out_bwd")
    emit("mix_w_out", sched.carry(_dw_groups, [sv["yat"], sv["ybt"], sv["yct"]], [dz2b], DW_TOKENS,
                                  tag + "dw_out").reshape(N_SHARDS, -1, d))
    dpc3, g["conv_w"] = sched.carry(_conv_bwd, sv["pc3"], dya.reshape(b, s, -1), p["cw"], tag + "conv_bwd")
    dq3, dk3, dv3, dcum4 = sched.carry(_fox_bwd, sv["pq3"], sv["cum4"], sv["lse4"], dyb.reshape(b, s, -1),
                                       tag + "fox_bwd")
    nh = sv["flog"].shape[1]
    dflog, dbf, dwft = sched.carry(_cum_bwd, dcum4.reshape(b, nh, s), sv["flog"], sv["x1t"], tag + "fox_gate_bwd")
    g["fox_b_f"] = dbf[:, 0]
    dps, g["sgu_w_s"], dbs, g["sgu_ln_g"], g["sgu_ln_b"] = sched.carry(
        _sgu_bwd, sv["ps"], dyc, p["sgu_g"], p["sgu_b"], p["ws"], p["bs"], tag + "sgu_bwd")
    g["sgu_b_s"] = dbs[:, :, 0]
    dpc = dpc3.reshape(t, -1)
    dq, dk, dv = dq3.reshape(t, -1), dk3.reshape(t, -1), dv3.reshape(t, -1)
    main = sched.carry(_dw_groups, [sv["x1t"]], [dpc, dq, dk, dv, dps], DW_TOKENS // 2, tag + "dw_in")
    n_main = main.shape[1] - dps.shape[1]
    w_in_grad = jnp.concatenate([main[:, :n_main], dwft.T, main[:, n_main:]], axis=1)
    emit("mix_w_in", jnp.moveaxis(w_in_grad.reshape(d, N_SHARDS, -1), 1, 0))
    dx1 = sched.carry(_mix_bwd_dx, dz2, dpc, dq, dk, dv, dps, dflog, p["win"], p["wft"], s, alpha, tag + "mix_bwd_dx")

    def small_ready(dgam, dbet):
        g["ln1_g"], g["ln1_b"] = dgam, dbet
        emit_small(g)

    dx0, _, _ = _ffn_backward(sched, emit, "ffn1", dx1, sv["z1"], p["ln1_g"], p["wd1"], p["wup1"],
                              sv["h1"], sv["a1t"], sv["xt"], alpha, tag + "ffn1", after_mid=small_ready)
    return dx0


def _span(n):
    return -(-n // 1024) * 1024


def _pack_rows(flat_list):
    return jnp.concatenate([jnp.pad(v, (0, _span(v.shape[0]) - v.shape[0])) for v in flat_list]).reshape(-1, 128)


def kernel(x, ln1_g, ln1_b, ffn1_w_up, ffn1_w_down, mix_w_in, fox_b_f, conv_w, sgu_ln_g, sgu_ln_b, sgu_w_s, sgu_b_s, mix_w_out, ln2_g, ln2_b, ffn2_w_up, ffn2_w_down, ln3_g, ln3_b, loss_target, m_ln1_g, m_ln1_b, m_ffn1_w_up, m_ffn1_w_down, m_mix_w_in, m_fox_b_f, m_conv_w, m_sgu_ln_g, m_sgu_ln_b, m_sgu_w_s, m_sgu_b_s, m_mix_w_out, m_ln2_g, m_ln2_b, m_ffn2_w_up, m_ffn2_w_down, m_ln3_g, m_ln3_b, v_ln1_g, v_ln1_b, v_ffn1_w_up, v_ffn1_w_down, v_mix_w_in, v_fox_b_f, v_conv_w, v_sgu_ln_g, v_sgu_ln_b, v_sgu_w_s, v_sgu_b_s, v_mix_w_out, v_ln2_g, v_ln2_b, v_ffn2_w_up, v_ffn2_w_down, v_ln3_g, v_ln3_b):
    wts = dict(ln1_g=ln1_g, ln1_b=ln1_b, ffn1_w_up=ffn1_w_up, ffn1_w_down=ffn1_w_down, mix_w_in=mix_w_in,
               fox_b_f=fox_b_f, conv_w=conv_w, sgu_ln_g=sgu_ln_g, sgu_ln_b=sgu_ln_b, sgu_w_s=sgu_w_s,
               sgu_b_s=sgu_b_s, mix_w_out=mix_w_out, ln2_g=ln2_g, ln2_b=ln2_b, ffn2_w_up=ffn2_w_up,
               ffn2_w_down=ffn2_w_down, ln3_g=ln3_g, ln3_b=ln3_b)
    mom = dict(ln1_g=m_ln1_g, ln1_b=m_ln1_b, ffn1_w_up=m_ffn1_w_up, ffn1_w_down=m_ffn1_w_down, mix_w_in=m_mix_w_in,
               fox_b_f=m_fox_b_f, conv_w=m_conv_w, sgu_ln_g=m_sgu_ln_g, sgu_ln_b=m_sgu_ln_b, sgu_w_s=m_sgu_w_s,
               sgu_b_s=m_sgu_b_s, mix_w_out=m_mix_w_out, ln2_g=m_ln2_g, ln2_b=m_ln2_b, ffn2_w_up=m_ffn2_w_up,
               ffn2_w_down=m_ffn2_w_down, ln3_g=m_ln3_g, ln3_b=m_ln3_b)
    var = dict(ln1_g=v_ln1_g, ln1_b=v_ln1_b, ffn1_w_up=v_ffn1_w_up, ffn1_w_down=v_ffn1_w_down, mix_w_in=v_mix_w_in,
               fox_b_f=v_fox_b_f, conv_w=v_conv_w, sgu_ln_g=v_sgu_ln_g, sgu_ln_b=v_sgu_ln_b, sgu_w_s=v_sgu_w_s,
               sgu_b_s=v_sgu_b_s, mix_w_out=v_mix_w_out, ln2_g=v_ln2_g, ln2_b=v_ln2_b, ffn2_w_up=v_ffn2_w_up,
               ffn2_w_down=v_ffn2_w_down, ln3_g=v_ln3_g, ln3_b=v_ln3_b)

    nl = ln1_g.shape[0]
    b, s, d = x.shape
    t = b * s
    alpha = (2 * nl) ** 0.25
    cw_sh = conv_w.shape[2]
    d_conv = cw_sh * N_SHARDS
    d_sgu = sgu_ln_g.shape[1]
    nh = fox_b_f.shape[1]
    d_fox = nh * FOX_HEAD_DIM
    n_main = 3 * d_conv + 3 * d_fox
    dims = dict(b=b, s=s, proj_widths=(3 * d_conv, 3 * d_fox, 2 * d_sgu))
    cpos = lax.axis_index("c").astype(jnp.int32)
    qpos = (2 * lax.axis_index("x") + lax.axis_index("y")).astype(jnp.int32)
    cq = jnp.stack([cpos, qpos])

    me = (2 * qpos + cpos).reshape(1)
    assert nl == 2, "the gather schedule below names the carriers of a two-layer step"

    conv_tile = jnp.pad(conv_w, ((0, 0), (0, 8 - conv_w.shape[1]), (0, 128 - cw_sh)))
    params = [dict(bf=fox_b_f[l].reshape(nh, 1), sgu_g=_row(sgu_ln_g[l]), sgu_b=_row(sgu_ln_b[l]), ws=sgu_w_s[l],
                   bs=sgu_b_s[l][:, :, None], ln1_g=_row(ln1_g[l]), ln1_b=_row(ln1_b[l]), ln2_g=_row(ln2_g[l]),
                   ln2_b=_row(ln2_b[l]), ln3_g=_row(ln3_g[l]), ln3_b=_row(ln3_b[l])) for l in range(nl)]

    def operands_of(k, arr):
        if k == "mix_w_in":
            w_in = jnp.moveaxis(arr, 0, 1).reshape(d, -1)
            return dict(win=jnp.concatenate([w_in[:, :n_main], w_in[:, n_main + nh:]], axis=1),
                        wft=jnp.pad(w_in[:, n_main:n_main + nh].T, ((0, HEAD_ROWS - nh), (0, 0))))
        if k == "conv_w":
            return dict(cw=jnp.moveaxis(arr[:, :3, :cw_sh], 0, 1).reshape(3, d_conv))
        if k in ("ffn1_w_up", "ffn2_w_up"):
            return {"wup" + k[3]: arr}
        return {dict(ffn1_w_down="wd1", ffn2_w_down="wd2", mix_w_out="wout")[k]: arr.reshape(-1, d)}

    def gather(l, keys):
        side = _side_gather([conv_tile[l] if k == "conv_w" else wts[k][l].astype(BF) for k in keys],
                            [k != "conv_w" for k in keys])

        def install():
            for k, arr in zip(keys, side.results):
                params[l].update(operands_of(k, arr))
        return side, install

    first, install_first = gather(0, ["ffn1_w_up"])
    _run_side(first, "gather_first")
    install_first()
    rides = [{"ffn1_up": gather(0, ["ffn1_w_down", "mix_w_in", "mix_w_out", "conv_w"]),
              "ffn1_down": gather(0, ["ffn2_w_up"]),
              "mix_proj": gather(0, ["ffn2_w_down"]),
              "fox": gather(1, ["ffn1_w_up", "ffn1_w_down", "mix_w_in", "mix_w_out", "conv_w"]),
              "ffn2_up": gather(1, ["ffn2_w_up", "ffn2_w_down"])}, {}]

    act, act_t = x.reshape(t, d), None
    saved = []
    for l in range(nl):
        act, act_t, sv = _forward_layer(act, act_t, params[l], dims, alpha, l, rides[l],
                                        target=loss_target.reshape(t, d) if l == nl - 1 else None)
        saved.append(sv)
    dy, loss_blk = act, act_t

    sched = _Sched()
    prev = {k: None for k in BIG}
    red = {}

    def emit_for(l):
        def emit(key, g):
            st = {}
            name = "l%d_%s" % (l, key)

            def pair_sum(res, side):
                st["p"] = _pair_sum(g, res[0], cq, "rs_pair_sum_" + name, side=side)

            def chip_sum(res, side):
                st["t"] = _chip_sum(st["p"], res[0], cq, "rs_chip_sum_" + name, side=side)

            def adamw(res, side):
                prev[key] = _adamw_shard(wts[key], mom[key], var[key], st["t"], res[0], cq, l, prev[key],
                                         "adamw_" + name, side=side)

            sched.add([("comm", lambda: _side_pair_send([g])), ("compute", pair_sum),
                       ("comm", lambda: _side_scatter([st["p"]])), ("compute", chip_sum),
                       ("comm", lambda: _side_pair_share([st["t"]])), ("compute", adamw)])
        return emit

    def emit_small_for(l):
        def emit_small(g):
            flat = [g[k].reshape(-1) for k in SMALL] + [g["conv_w"].reshape(-1)]
            if l == nl - 1:
                flat.append(loss_blk[0, 0:1])
            vec = _pack_rows(flat)

            def slot_sum(res, side):
                red[l] = _sum_slots(vec, res[0], me, "small_sum_l%d" % l, side=side)

            sched.add([("comm", lambda: _side_bcast(vec)), ("compute", slot_sum)])
        return emit_small

    for l in reversed(range(nl)):
        dy = _backward_layer(sched, emit_for(l), emit_small_for(l), dy, saved[l], params[l], dims, alpha, l)
    sched.drain()
    grad_x = dy.reshape(b, s, d)
    gfin, delta, new_m, new_v = {}, {}, {}, {}
    for k in BIG:
        gfin[k], delta[k], new_m[k], new_v[k] = prev[k]

    gsm = {k: [] for k in SMALL + ("conv_w",)}
    for l in range(nl):
        flat_l = red[l].reshape(-1)
        off = 0
        for k in SMALL:
            n = wts[k][l].size
            gsm[k].append(flat_l[off:off + n].reshape(wts[k][l].shape))
            off += _span(n)
        n = 3 * d_conv
        gsm["conv_w"].append(lax.dynamic_slice_in_dim(flat_l[off:off + n].reshape(3, d_conv), qpos * cw_sh, cw_sh,
                                                      axis=1))
        off += _span(n)
        if l == nl - 1:
            loss = flat_l[off]
    for k in gsm:
        gfin[k] = jnp.stack(gsm[k])
    small_keys = SMALL + ("conv_w",)
    sizes = [wts[k].size for k in small_keys]
    pk = lambda src: _pack_rows([src[k].reshape(-1) for k in small_keys])
    dl, nm, nv = _adamw_small(pk(wts), pk(gfin), pk(mom), pk(var), "adamw_small")
    off = 0
    for k, n in zip(small_keys, sizes):
        shp = wts[k].shape
        delta[k] = dl.reshape(-1)[off:off + n].reshape(shp)
        new_m[k] = nm.reshape(-1)[off:off + n].reshape(shp)
        new_v[k] = nv.reshape(-1)[off:off + n].reshape(shp)
        off += _span(n)

    return (loss, grad_x, *[gfin[k] for k in ORDER], *[delta[k] for k in ORDER],
            *[new_m[k] for k in ORDER], *[new_v[k] for k in ORDER])
```

```python
import functools

import jax
import jax.numpy as jnp
from jax import lax
from jax.experimental import pallas as pl
from jax.experimental.pallas import tpu as pltpu

F32 = jnp.float32
BF = jnp.bfloat16
SDS = jax.ShapeDtypeStruct
MESH = pl.DeviceIdType.MESH

LN_EPS = 1e-5
FOX_HEAD_DIM = 64
FOX_Q_BLOCK = 512
DW_TOKENS = 2048
HEAD_ROWS = 128
GELU_K = 0.7978845608028654
GELU_C = 0.044715
NEG_BIG = -1e30
N_SHARDS = 4

ADAM_LR = 0.001
ADAM_B1 = 0.9
ADAM_B2 = 0.999
ADAM_EPS = 1e-08
ADAM_WD = 0.01
ADAM_STEP = 10

VMEM_LIMIT_BYTES = 56 * 1024 * 1024
NT = (((1,), (1,)), ((), ()))
TN = (((0,), (0,)), ((), ()))
HBM = pl.BlockSpec(memory_space=pl.ANY)


def _tile(n, pref, mult=8):
    t = min(n, pref)
    while n % t or t % mult:
        t -= mult
    return t


def _dot(a, b):
    return jnp.dot(a, b, preferred_element_type=F32)


def _dotg(a, b, dims):
    return lax.dot_general(a, b, dims, preferred_element_type=F32)


def _sigmoid(x):
    return 1.0 / (1.0 + jnp.exp(-x))


def _gelu(x):
    return 0.5 * x * (1.0 + jnp.tanh(GELU_K * (x + GELU_C * x * x * x)))


def _gelu_grad(x):
    t = jnp.tanh(GELU_K * (x + GELU_C * x * x * x))
    return 0.5 * (1.0 + t) + 0.5 * x * (1.0 - t * t) * GELU_K * (1.0 + 3.0 * GELU_C * x * x)


def _ln_stats(z):
    mu = jnp.mean(z, axis=-1, keepdims=True)
    zc = z - mu
    var = jnp.mean(zc * zc, axis=-1, keepdims=True)
    rstd = lax.rsqrt(var + LN_EPS)
    return zc * rstd, rstd


def _ln_bwd(dy, z, g):
    xhat, rstd = _ln_stats(z)
    gdy = dy * g
    m1 = jnp.mean(gdy, axis=-1, keepdims=True)
    m2 = jnp.mean(gdy * xhat, axis=-1, keepdims=True)
    dz = rstd * (gdy - m1 - xhat * m2)
    return dz, jnp.sum(dy * xhat, axis=0, keepdims=True), jnp.sum(dy, axis=0, keepdims=True)


class _Side:
    def __init__(self, ins, out_shapes, sems, start, finish):
        self.ins, self.out_shapes, self.sems = list(ins), list(out_shapes), list(sems)
        self.start, self.finish = start, finish
        self.results = None


def _join(sides):
    sides = [s for s in sides if s is not None]
    if not sides:
        return None
    ins = [a for s in sides for a in s.ins]
    outs = [a for s in sides for a in s.out_shapes]
    sems = [a for s in sides for a in s.sems]

    def parts(seq, field):
        out, o = [], 0
        for s in sides:
            n = len(getattr(s, field))
            out.append(seq[o:o + n])
            o += n
        return out

    def run(which):
        def fn(i, o, m):
            for s, a, b, c in zip(sides, parts(i, "ins"), parts(o, "out_shapes"), parts(m, "sems")):
                getattr(s, which)(a, b, c)
        return fn

    joined = _Side(ins, outs, sems, run("start"), run("finish"))
    joined.members = sides
    return joined


def _deliver(side, results):
    members = getattr(side, "members", None)
    side.results = list(results)
    if members:
        o = 0
        for s in members:
            n = len(s.out_shapes)
            _deliver(s, results[o:o + n])
            o += n


def _pcall(body, operands, *, name, grid, in_specs, out_specs, out_shape, sem, scratch_shapes=(),
           prefetch=(), aliases=None, side=None):
    single = not isinstance(out_shape, (list, tuple))
    out_shape = [out_shape] if single else list(out_shape)
    out_specs = [out_specs] if single else list(out_specs)
    in_specs, scratch_shapes = list(in_specs), list(scratch_shapes)
    n_pre, n_in, n_out, n_sc = len(prefetch), len(in_specs), len(out_shape), len(scratch_shapes)
    fn = body
    extra = []
    if side is not None:
        s_in, s_out = len(side.ins), len(side.out_shapes)

        def fn(*refs):
            pre, rest = refs[:n_pre], refs[n_pre:]
            m_in, c_in = rest[:n_in], rest[n_in:n_in + s_in]
            rest = rest[n_in + s_in:]
            m_out, c_out = rest[:n_out], rest[n_out:n_out + s_out]
            rest = rest[n_out + s_out:]
            m_sc, c_sc = rest[:n_sc], rest[n_sc:]
            first = pl.program_id(0) == 0
            last = pl.program_id(0) == grid[0] - 1
            for a in range(1, len(grid)):
                first = jnp.logical_and(first, pl.program_id(a) == 0)
                last = jnp.logical_and(last, pl.program_id(a) == grid[a] - 1)

            @pl.when(first)
            def _():
                side.start(c_in, c_out, c_sc)

            body(*pre, *m_in, *m_out, *m_sc)

            @pl.when(last)
            def _():
                side.finish(c_in, c_out, c_sc)

        in_specs = in_specs + [HBM] * s_in
        out_specs = out_specs + [HBM] * s_out
        out_shape = out_shape + side.out_shapes
        scratch_shapes = scratch_shapes + side.sems
        extra = side.ins
        sem = ("arbitrary",) * len(grid)
    params = pltpu.CompilerParams(dimension_semantics=tuple(sem), vmem_limit_bytes=VMEM_LIMIT_BYTES)
    kw = dict(input_output_aliases=aliases) if aliases else {}
    if n_pre:
        spec = pltpu.PrefetchScalarGridSpec(num_scalar_prefetch=n_pre, grid=grid, in_specs=in_specs,
                                            out_specs=out_specs, scratch_shapes=scratch_shapes)
        call = pl.pallas_call(fn, name=name, grid_spec=spec, out_shape=out_shape, compiler_params=params, **kw)
    else:
        call = pl.pallas_call(fn, name=name, grid=grid, in_specs=in_specs, out_specs=out_specs,
                              out_shape=out_shape, scratch_shapes=scratch_shapes, compiler_params=params, **kw)
    res = call(*prefetch, *operands, *extra)
    if side is not None:
        _deliver(side, res[n_out:])
        res = res[:n_out]
    return res[0] if single else res


def _run_side(side, name):
    def body(*refs):
        n_in, n_out = len(side.ins), len(side.out_shapes)
        i, o, m = refs[:n_in], refs[n_in:n_in + n_out], refs[n_in + n_out:]
        side.start(i, o, m)
        side.finish(i, o, m)

    res = pl.pallas_call(body, name=name, in_specs=[HBM] * len(side.ins), out_specs=[HBM] * len(side.out_shapes),
                         out_shape=side.out_shapes, scratch_shapes=side.sems)(*side.ins)
    _deliver(side, res)


def _mesh_pos():
    x, y, c = lax.axis_index("x"), lax.axis_index("y"), lax.axis_index("c")
    chips = [(1 - x, y), (x, 1 - y), (1 - x, 1 - y)]
    return x, y, c, chips


def _rows(ref, lead, half, n_rows):
    return ref.at[tuple(lead) + (pl.ds(half * n_rows, n_rows),)]


def _side_gather(shards, split):
    n = len(shards)
    hs = [w.shape[0] // 2 for w in shards]

    def plan(ins, outs, sems):
        ssem, rsem = sems
        x, y, c, chips = _mesh_pos()
        q = 2 * x + y
        sib = (x, y, 1 - c)

        def rc(p, k, src, dst, to):
            return pltpu.make_async_remote_copy(src_ref=src, dst_ref=dst, send_sem=ssem.at[p, k],
                                                recv_sem=rsem.at[p, k], device_id=to, device_id_type=MESH)

        def blk(ref, p, qi, half):
            return _rows(ref, (qi,), half, hs[p]) if split[p] else ref.at[qi]

        return x, y, c, chips, q, sib, rc, blk

    def first_sends(ins, outs, sems):
        x, y, c, chips, q, sib, rc, blk = plan(ins, outs, sems)
        cps = [rc(p, 0, ins[p], outs[p].at[q], sib) for p in range(n)]
        for j, (cx, cy) in enumerate(chips):
            for p in range(n):
                src = _rows(ins[p], (), c, hs[p]) if split[p] else ins[p]
                cps.append(rc(p, 1 + j, src, blk(outs[p], p, q, c), (cx, cy, c)))
        return cps

    def start(ins, outs, sems):
        for cp in first_sends(ins, outs, sems):
            cp.start()

    def finish(ins, outs, sems):
        x, y, c, chips, q, sib, rc, blk = plan(ins, outs, sems)
        sent = first_sends(ins, outs, sems)
        for j, (cx, cy) in enumerate(chips):
            qj = 2 * cx + cy
            for p in range(n):
                got = blk(outs[p], p, qj, c)
                rc(p, 1 + j, got, got, (cx, cy, c)).wait_recv()
                if split[p]:
                    fwd = rc(p, 4 + j, got, got, sib)
                    fwd.start()
                    sent.append(fwd)
        for j, (cx, cy) in enumerate(chips):
            qj = 2 * cx + cy
            for p in range(n):
                if split[p]:
                    got = blk(outs[p], p, qj, 1 - c)
                    rc(p, 4 + j, got, got, sib).wait_recv()
        for p in range(n):
            rc(p, 0, outs[p].at[q], outs[p].at[q], sib).wait_recv()
        for cp in sent:
            cp.wait_send()

    return _Side(shards, [SDS((N_SHARDS,) + w.shape, w.dtype) for w in shards],
                 [pltpu.SemaphoreType.DMA((n, 7)), pltpu.SemaphoreType.DMA((n, 7))], start, finish)


def _side_pair_send(gs):
    n = len(gs)

    def copies(ins, outs, sems):
        x, y, c, _ = _mesh_pos()
        return [pltpu.make_async_remote_copy(
            src_ref=ins[p].at[:, pl.ds((1 - c) * (gs[p].shape[1] // 2), gs[p].shape[1] // 2)], dst_ref=outs[p],
            send_sem=sems[0].at[p], recv_sem=sems[1].at[p], device_id=(x, y, 1 - c), device_id_type=MESH)
            for p in range(n)]

    def start(ins, outs, sems):
        for cp in copies(ins, outs, sems):
            cp.start()

    def finish(ins, outs, sems):
        for cp in copies(ins, outs, sems):
            cp.wait()

    return _Side(gs, [SDS((g.shape[0], g.shape[1] // 2, g.shape[2]), g.dtype) for g in gs],
                 [pltpu.SemaphoreType.DMA((n,)), pltpu.SemaphoreType.DMA((n,))], start, finish)


def _side_scatter(ps):
    n = len(ps)

    def sends(ins, outs, sems):
        x, y, c, chips = _mesh_pos()
        q = 2 * x + y
        return [pltpu.make_async_remote_copy(src_ref=ins[p].at[2 * cx + cy], dst_ref=outs[p].at[q],
                                             send_sem=sems[0].at[p, j], recv_sem=sems[1].at[p, j],
                                             device_id=(cx, cy, c), device_id_type=MESH)
                for j, (cx, cy) in enumerate(chips) for p in range(n)]

    def start(ins, outs, sems):
        for cp in sends(ins, outs, sems):
            cp.start()

    def finish(ins, outs, sems):
        x, y, c, chips = _mesh_pos()
        for j, (cx, cy) in enumerate(chips):
            for p in range(n):
                got = outs[p].at[2 * cx + cy]
                pltpu.make_async_remote_copy(src_ref=got, dst_ref=got, send_sem=sems[0].at[p, j],
                                             recv_sem=sems[1].at[p, j], device_id=(cx, cy, c),
                                             device_id_type=MESH).wait_recv()
        for cp in sends(ins, outs, sems):
            cp.wait_send()

    return _Side(ps, [SDS(p.shape, p.dtype) for p in ps],
                 [pltpu.SemaphoreType.DMA((n, 3)), pltpu.SemaphoreType.DMA((n, 3))], start, finish)


def _side_pair_share(tots):
    n = len(tots)

    def copies(ins, outs, sems):
        x, y, c, _ = _mesh_pos()
        return [pltpu.make_async_remote_copy(src_ref=ins[p], dst_ref=outs[p], send_sem=sems[0].at[p],
                                             recv_sem=sems[1].at[p], device_id=(x, y, 1 - c), device_id_type=MESH)
                for p in range(n)]

    def start(ins, outs, sems):
        for cp in copies(ins, outs, sems):
            cp.start()

    def finish(ins, outs, sems):
        for cp in copies(ins, outs, sems):
            cp.wait()

    return _Side(tots, [SDS(t_.shape, t_.dtype) for t_ in tots],
                 [pltpu.SemaphoreType.DMA((n,)), pltpu.SemaphoreType.DMA((n,))], start, finish)


N_DEVICES = 8


def _side_bcast(v):
    def peers():
        x, y, c, _ = _mesh_pos()
        out = []
        for k in range(1, N_DEVICES):
            px, py, pc = x ^ ((k >> 2) & 1), y ^ ((k >> 1) & 1), c ^ (k & 1)
            out.append((k - 1, (px, py, pc), 4 * px + 2 * py + pc))
        return 4 * x + 2 * y + c, out

    def sends(ins, outs, sems):
        me, ps = peers()
        return [pltpu.make_async_remote_copy(src_ref=ins[0], dst_ref=outs[0].at[me], send_sem=sems[0].at[k],
                                             recv_sem=sems[1].at[k], device_id=to, device_id_type=MESH)
                for k, to, _ in ps]

    def start(ins, outs, sems):
        for cp in sends(ins, outs, sems):
            cp.start()

    def finish(ins, outs, sems):
        _, ps = peers()
        for k, to, slot in ps:
            got = outs[0].at[slot]
            pltpu.make_async_remote_copy(src_ref=got, dst_ref=got, send_sem=sems[0].at[k], recv_sem=sems[1].at[k],
                                         device_id=to, device_id_type=MESH).wait_recv()
        for cp in sends(ins, outs, sems):
            cp.wait_send()

    return _Side([v], [SDS((N_DEVICES,) + v.shape, v.dtype)],
                 [pltpu.SemaphoreType.DMA((N_DEVICES - 1,)), pltpu.SemaphoreType.DMA((N_DEVICES - 1,))], start, finish)


def _sum_slots(v, r, me, name, side=None):
    n, rows, lanes = r.shape
    tr = _tile(rows, 512)

    def body(me_ref, v_ref, r_ref, o_ref):
        j = pl.program_id(1)
        term = jnp.where(j == me_ref[0], v_ref[...], r_ref[0])

        @pl.when(j == 0)
        def _():
            o_ref[...] = term

        @pl.when(j != 0)
        def _():
            o_ref[...] += term

    other = lambda j, k: jnp.where(j == k, (k + 1) % n, j)
    return _pcall(
        body, (v, r), name=name, grid=(rows // tr, n), prefetch=(me,),
        in_specs=[pl.BlockSpec((tr, lanes), lambda i, j, me_ref: (i, 0)),
                  pl.BlockSpec((1, tr, lanes), lambda i, j, me_ref: (other(j, me_ref[0]), i, 0))],
        out_specs=pl.BlockSpec((tr, lanes), lambda i, j, me_ref: (i, 0)),
        out_shape=SDS((rows, lanes), F32), sem=("parallel", "arbitrary"), side=side)


def _pair_sum(g, r1, cq, name, side=None):
    nq, xr, yc = g.shape
    h = xr // 2
    tr = _tile(h, 512, 16)
    nt = h // tr

    def body(cq_ref, g_ref, r_ref, o_ref):
        o_ref[...] = (g_ref[...] + r_ref[...]).astype(BF)

    return _pcall(
        body, (g, r1), name=name, grid=(nq, nt), prefetch=(cq,),
        in_specs=[pl.BlockSpec((1, tr, yc), lambda j, i, cq_ref: (j, cq_ref[0] * nt + i, 0)),
                  pl.BlockSpec((1, tr, yc), lambda j, i, cq_ref: (j, i, 0))],
        out_specs=pl.BlockSpec((1, tr, yc), lambda j, i, cq_ref: (j, i, 0)),
        out_shape=SDS((nq, h, yc), BF), sem=("parallel", "parallel"), side=side)


def _chip_sum(p, r2, cq, name, side=None):
    nq, h, yc = r2.shape
    tr = _tile(h, 512, 16)

    def body(cq_ref, p_ref, r_ref, o_ref):
        j = pl.program_id(1)
        term = jnp.where(j == cq_ref[1], p_ref[0], r_ref[0]).astype(F32)

        @pl.when(j == 0)
        def _():
            o_ref[...] = term

        @pl.when(j != 0)
        def _():
            o_ref[...] += term

    other = lambda j, q: jnp.where(j == q, (q + 1) % nq, j)
    return _pcall(
        body, (p, r2), name=name, grid=(h // tr, nq), prefetch=(cq,),
        in_specs=[pl.BlockSpec((1, tr, yc), lambda i, j, cq_ref: (cq_ref[1], i, 0)),
                  pl.BlockSpec((1, tr, yc), lambda i, j, cq_ref: (other(j, cq_ref[1]), i, 0))],
        out_specs=pl.BlockSpec((tr, yc), lambda i, j, cq_ref: (i, 0)),
        out_shape=SDS((h, yc), F32), sem=("parallel", "arbitrary"), side=side)


def _ffn_up_fwd(x, wup, name, side=None, emit_xt=False):
    t, d = x.shape
    w = wup.shape[2]
    tm = _tile(t, 512)

    def body(x_ref, wg_ref, wu_ref, h_ref, a_ref, at_ref, *xt_ref):
        xb = x_ref[...].astype(BF)
        g = _dot(xb, wg_ref[0])
        u = _dot(xb, wu_ref[0])
        h_ref[0] = g.astype(BF)
        h_ref[1] = u.astype(BF)
        ab = (g * _sigmoid(g) * u).astype(BF)
        a_ref[...] = ab
        at_ref[...] = ab.T
        if emit_xt:
            @pl.when(pl.program_id(0) == 0)
            def _():
                xt_ref[0][...] = xb.T

    nt = t // tm
    out_specs = [pl.BlockSpec((2, tm, w), lambda j, i: (0, i, j)), pl.BlockSpec((tm, w), lambda j, i: (i, j)),
                 pl.BlockSpec((w, tm), lambda j, i: (j, i))]
    out_shape = [SDS((2, t, 2 * w), BF), SDS((t, 2 * w), BF), SDS((2 * w, t), BF)]
    if emit_xt:
        out_specs.append(pl.BlockSpec((d, tm), lambda j, i: (0, jnp.where(j == 0, i, nt - 1))))
        out_shape.append(SDS((d, t), BF))
    return _pcall(
        body, (x, wup, wup), name=name, grid=(2, nt),
        in_specs=[pl.BlockSpec((tm, d), lambda j, i: (i, 0)),
                  pl.BlockSpec((1, d, w), lambda j, i: (j, 0, 0)),
                  pl.BlockSpec((1, d, w), lambda j, i: (j + 2, 0, 0))],
        out_specs=out_specs, out_shape=out_shape,
        sem=("arbitrary", "arbitrary") if emit_xt else ("parallel", "parallel"), side=side)


def _res_ln_fwd(parts, w, x, gamma, beta, alpha, res_scale, name, side=None, parts_t=False, target=None):
    t, d = x.shape
    n = len(parts)
    offs = [0]
    for p in parts:
        offs.append(offs[-1] + p.shape[1])
    tm = _tile(t, 512)
    n_in = n + 4 + (target is not None)

    def body(*refs):
        p_refs = refs[:n]
        w_ref, x_ref, g_ref, b_ref = refs[n:n + 4]
        out = refs[n_in:]
        f = _dot(p_refs[0][...], w_ref[offs[0]:offs[1], :])
        for k in range(1, n):
            f = f + _dot(p_refs[k][...], w_ref[offs[k]:offs[k + 1], :])
        z = alpha * x_ref[...] + res_scale * f
        out[0][...] = z
        xhat, _ = _ln_stats(z)
        y = xhat * g_ref[...] + b_ref[...]
        if target is not None:
            err = y - refs[n + 4][...]
            out[1][...] = err * (1.0 / d)
            part = 0.5 * jnp.sum(jnp.sum(err * err, axis=-1, keepdims=True) * (1.0 / d), axis=0, keepdims=True)
            _acc_rows(out[2], jnp.broadcast_to(part, (8, 128)), pl.program_id(0) == 0)
            return
        out[1][...] = y
        out[2][...] = y.astype(BF).T
        for k in range(len(out) - 3):
            out[3 + k][...] = p_refs[k][...].T

    row = lambda i: (i, 0)
    col = lambda i: (0, i)
    fixed = lambda i: (0, 0)
    operands = [*parts, w, x, gamma, beta]
    in_specs = [pl.BlockSpec((tm, p.shape[1]), row) for p in parts] + [
        pl.BlockSpec(w.shape, fixed), pl.BlockSpec((tm, d), row), pl.BlockSpec((1, d), fixed), pl.BlockSpec((1, d), fixed)]
    if target is not None:
        operands.append(target)
        in_specs.append(pl.BlockSpec((tm, d), row))
        out_specs = [pl.BlockSpec((tm, d), row), pl.BlockSpec((tm, d), row), pl.BlockSpec((8, 128), fixed)]
        out_shape = [SDS((t, d), F32), SDS((t, d), F32), SDS((8, 128), F32)]
    else:
        out_specs = [pl.BlockSpec((tm, d), row), pl.BlockSpec((tm, d), row), pl.BlockSpec((d, tm), col)]
        out_shape = [SDS((t, d), F32), SDS((t, d), F32), SDS((d, t), BF)]
        if parts_t:
            out_specs += [pl.BlockSpec((p.shape[1], tm), col) for p in parts]
            out_shape += [SDS((p.shape[1], t), BF) for p in parts]
    return _pcall(
        body, operands, name=name, grid=(t // tm,), in_specs=in_specs, out_specs=out_specs, out_shape=out_shape,
        sem=("arbitrary",) if target is not None else ("parallel",), side=side)


def _mix_proj_fwd(x, w_main, widths, name, side=None):
    t, d = x.shape
    dc, dq, ds = widths
    tm = _tile(t, 512)

    def body(x_ref, w_ref, pc_ref, pq_ref, ps_ref):
        xb = x_ref[...].astype(BF)
        pc_ref[...] = _dot(xb, w_ref[:, 0:dc])
        pq_ref[...] = _dot(xb, w_ref[:, dc:dc + dq]).astype(BF)
        ps_ref[...] = _dot(xb, w_ref[:, dc + dq:dc + dq + ds])

    row = lambda i: (i, 0)
    return _pcall(
        body, (x, w_main), name=name, grid=(t // tm,),
        in_specs=[pl.BlockSpec((tm, d), row), pl.BlockSpec(w_main.shape, lambda i: (0, 0))],
        out_specs=[pl.BlockSpec((tm, dc), row), pl.BlockSpec((tm, dq), row), pl.BlockSpec((tm, ds), row)],
        out_shape=[SDS((t, dc), F32), SDS((t, dq), BF), SDS((t, ds), F32)],
        sem=("parallel",), side=side)


def _prefix_sum_lanes(v, reverse):
    n = v.shape[-1]
    lane = lax.broadcasted_iota(jnp.int32, v.shape, v.ndim - 1)
    sh = 1
    while sh < n:
        if reverse:
            v = v + jnp.where(lane < n - sh, pltpu.roll(v, n - sh, axis=v.ndim - 1), 0.0)
        else:
            v = v + jnp.where(lane >= sh, pltpu.roll(v, sh, axis=v.ndim - 1), 0.0)
        sh *= 2
    return v


def _cum_fwd(x3, wft, bf, name, side=None):
    b, s, d = x3.shape
    h = bf.shape[0]

    def body(x_ref, w_ref, b_ref, fl_ref, cum_ref):
        fl = _dotg(w_ref[...], x_ref[0].astype(BF), NT)[0:h] + b_ref[...]
        fl_ref[0] = fl
        lf = jnp.minimum(fl, 0.0) - jnp.log(1.0 + jnp.exp(-jnp.abs(fl)))
        cum_ref[0] = _prefix_sum_lanes(lf, reverse=False)

    return _pcall(
        body, (x3, wft, bf), name=name, grid=(b,),
        in_specs=[pl.BlockSpec((1, s, d), lambda i: (i, 0, 0)),
                  pl.BlockSpec(wft.shape, lambda i: (0, 0)), pl.BlockSpec((h, 1), lambda i: (0, 0))],
        out_specs=[pl.BlockSpec((1, h, s), lambda i: (i, 0, 0)), pl.BlockSpec((1, h, s), lambda i: (i, 0, 0))],
        out_shape=[SDS((b, h, s), F32), SDS((b, h, s), F32)],
        sem=("parallel",), side=side)


def _shift_rows(z, k, down):
    n = z.shape[0]
    row = lax.broadcasted_iota(jnp.int32, z.shape, 0)
    if down:
        return jnp.where(row >= k, pltpu.roll(z, k, axis=0), 0.0)
    return jnp.where(row < n - k, pltpu.roll(z, n - k, axis=0), 0.0)


def _conv_fwd(pc3, cw, name, side=None):
    b, s, c3 = pc3.shape
    c = c3 // 3

    def body(p_ref, w_ref, y_ref):
        z = p_ref[0, :, c:2 * c] * p_ref[0, :, 2 * c:3 * c]
        conv = w_ref[0:1, :] * _shift_rows(z, 2, True) + w_ref[1:2, :] * _shift_rows(z, 1, True) + w_ref[2:3, :] * z
        y_ref[0] = (p_ref[0, :, 0:c] * conv).astype(BF)

    return _pcall(
        body, (pc3, cw), name=name, grid=(b,),
        in_specs=[pl.BlockSpec((1, s, c3), lambda i: (i, 0, 0)), pl.BlockSpec((3, c), lambda i: (0, 0))],
        out_specs=pl.BlockSpec((1, s, c), lambda i: (i, 0, 0)),
        out_shape=SDS((b, s, c), BF), sem=("parallel",), side=side)


def _head_masks(width):
    lane = lax.broadcasted_iota(jnp.int32, (1, width), 1)
    return [lane < FOX_HEAD_DIM, lane >= FOX_HEAD_DIM]


def _fox_scores(q, k, cum_row, lo, head_mask):
    tq = q.shape[0]
    qm = jnp.where(head_mask, q * (FOX_HEAD_DIM ** -0.5), 0)
    s = _dotg(qm, k, NT) - cum_row
    tri = lax.broadcasted_iota(jnp.int32, (tq, tq), 1) <= lax.broadcasted_iota(jnp.int32, (tq, tq), 0)
    parts = [(jnp.where(tri, s[:, lo:], NEG_BIG), lo, lo + tq)]
    if lo:
        parts.insert(0, (s[:, :lo], 0, lo))
    return parts, qm


def _fox_fwd(pq3, cum4, name, side=None):
    b, s, d3 = pq3.shape
    df = d3 // 3
    hp = df // 128
    tq = _tile(s, FOX_Q_BLOCK)

    def body(q_ref, k_ref, v_ref, c_ref, o_ref, lse_ref):
        masks = _head_masks(128)
        for i in range(s // tq):
            lo, hi = i * tq, (i + 1) * tq
            q = q_ref[0, lo:hi, :]
            k = k_ref[0, 0:hi, :]
            v = v_ref[0, 0:hi, :]
            o = jnp.zeros((tq, 128), F32)
            lse = jnp.zeros((tq, 128), F32)
            for e in range(2):
                parts, _ = _fox_scores(q, k, c_ref[0, 0, e:e + 1, 0:hi], lo, masks[e])
                m = functools.reduce(jnp.maximum, [jnp.max(sc, axis=-1, keepdims=True) for sc, _, _ in parts])
                l, pv = 0.0, 0.0
                for sc, c0, c1 in parts:
                    p = jnp.exp(sc - m)
                    l = l + jnp.sum(p, axis=-1, keepdims=True)
                    pv = pv + _dot(p.astype(BF), v[c0:c1])
                o = jnp.where(masks[e], pv * (1.0 / l), o)
                lse = jnp.where(masks[e], m + jnp.log(l), lse)
            o_ref[0, lo:hi, :] = o.astype(BF)
            lse_ref[0, 0, lo:hi, :] = lse

    blk = lambda off: pl.BlockSpec((1, s, 128), lambda i, j: (i, 0, off + j))
    return _pcall(
        body, (pq3, pq3, pq3, cum4), name=name, grid=(b, hp),
        in_specs=[blk(0), blk(hp), blk(2 * hp), pl.BlockSpec((1, 1, 2, s), lambda i, j: (i, j, 0, 0))],
        out_specs=[blk(0), pl.BlockSpec((1, 1, s, 128), lambda i, j: (i, j, 0, 0))],
        out_shape=[SDS((b, s, df), BF), SDS((b, hp, s, 128), F32)],
        sem=("parallel", "parallel"), side=side)


def _sgu_mix(wm, vnb, bias, gmasks):
    out = bias
    for g in range(len(wm)):
        out = out + jnp.where(gmasks[g], _dot(wm[g], vnb), 0.0)
    return out


def _sgu_consts(ws_ref, bs_ref, ds):
    ng, c, _ = ws_ref.shape
    gd = ds // ng
    tri = lax.broadcasted_iota(jnp.int32, (c, c), 0) >= lax.broadcasted_iota(jnp.int32, (c, c), 1)
    lane = lax.broadcasted_iota(jnp.int32, (1, ds), 1)
    gmasks = [(lane >= g * gd) & (lane < (g + 1) * gd) for g in range(ng)]
    wm = [jnp.where(tri, ws_ref[g], 0.0).astype(BF) for g in range(ng)]
    bias = jnp.zeros((c, ds), F32)
    for g in range(ng):
        bias = jnp.where(gmasks[g], bs_ref[g], bias)
    return tri, gmasks, wm, bias


def _sgu_fwd(ps, lng, lnb, ws, bs, name, side=None):
    t, ds2 = ps.shape
    ds = ds2 // 2
    c = ws.shape[1]
    tm = _tile(t, 512, c)

    def body(p_ref, g_ref, b_ref, ws_ref, bs_ref, y_ref):
        _, gmasks, wm, bias = _sgu_consts(ws_ref, bs_ref, ds)
        up = _gelu(p_ref[:, 0:ds])
        xhat, _ = _ln_stats(_gelu(p_ref[:, ds:ds2]))
        vnb = (xhat * g_ref[...] + b_ref[...]).astype(BF)
        for n in range(tm // c):
            r0, r1 = n * c, (n + 1) * c
            y_ref[r0:r1, :] = (up[r0:r1] * _sgu_mix(wm, vnb[r0:r1], bias, gmasks)).astype(BF)

    fixed2 = lambda i: (0, 0)
    fixed3 = lambda i: (0, 0, 0)
    return _pcall(
        body, (ps, lng, lnb, ws, bs), name=name, grid=(t // tm,),
        in_specs=[pl.BlockSpec((tm, ds2), lambda i: (i, 0)), pl.BlockSpec((1, ds), fixed2),
                  pl.BlockSpec((1, ds), fixed2), pl.BlockSpec(ws.shape, fixed3), pl.BlockSpec(bs.shape, fixed3)],
        out_specs=pl.BlockSpec((tm, ds), lambda i: (i, 0)),
        out_shape=SDS((t, ds), BF), sem=("parallel",), side=side)


def _acc_rows(ref, val, first):
    @pl.when(first)
    def _():
        ref[...] = val

    @pl.when(jnp.logical_not(first))
    def _():
        ref[...] += val


def _ffn_bwd_mid(dy, z, gamma, wd, h, name, side=None):
    t, d = dy.shape
    dff = wd.shape[0]
    half = dff // 2
    tm = _tile(t, 512)

    def body(dy_ref, z_ref, g_ref, wd_ref, h_ref, dz_ref, df_ref, dh_ref, dg_ref, db_ref):
        dz, dgam, dbet = _ln_bwd(dy_ref[...], z_ref[...], g_ref[...])
        first = pl.program_id(0) == 0
        _acc_rows(dg_ref, dgam, first)
        _acc_rows(db_ref, dbet, first)
        dz_ref[...] = dz
        dfb = (0.5 * dz).astype(BF)
        df_ref[...] = dfb
        for j in range(2):
            c0, c1 = j * half, (j + 1) * half
            da = _dotg(dfb, wd_ref[c0:c1, :], NT).astype(BF)
            g = h_ref[0, :, c0:c1]
            u = h_ref[1, :, c0:c1]
            sg = _sigmoid(g)
            dh_ref[0, :, c0:c1] = da * u * sg * (1.0 + g * (1.0 - sg))
            dh_ref[1, :, c0:c1] = da * g * sg

    row = lambda i: (i, 0)
    fixed = lambda i: (0, 0)
    return _pcall(
        body, (dy, z, gamma, wd, h), name=name, grid=(t // tm,),
        in_specs=[pl.BlockSpec((tm, d), row), pl.BlockSpec((tm, d), row), pl.BlockSpec((1, d), fixed),
                  pl.BlockSpec(wd.shape, fixed, pipeline_mode=pl.Buffered(1)),
                  pl.BlockSpec((2, tm, dff), lambda i: (0, i, 0))],
        out_specs=[pl.BlockSpec((tm, d), row), pl.BlockSpec((tm, d), row),
                   pl.BlockSpec((2, tm, dff), lambda i: (0, i, 0)),
                   pl.BlockSpec((1, d), fixed), pl.BlockSpec((1, d), fixed)],
        out_shape=[SDS((t, d), F32), SDS((t, d), BF), SDS((2, t, dff), BF), SDS((1, d), F32), SDS((1, d), F32)],
        sem=("arbitrary",), side=side)


def _ffn_bwd_dx(dh, wup, dz, alpha, name, side=None):
    _, t, dff = dh.shape
    nq, d, w = wup.shape
    per = dff // w
    tm = _tile(t, 512)

    def body(dh_ref, w_ref, dz_ref, dx_ref):
        acc = alpha * dz_ref[...]
        for q in range(nq):
            c0 = (q % per) * w
            acc = acc + _dotg(dh_ref[q // per, :, c0:c0 + w], w_ref[q], NT)
        dx_ref[...] = acc

    row = lambda i: (i, 0)
    return _pcall(
        body, (dh, wup, dz), name=name, grid=(t // tm,),
        in_specs=[pl.BlockSpec((2, tm, dff), lambda i: (0, i, 0)),
                  pl.BlockSpec(wup.shape, lambda i: (0, 0, 0), pipeline_mode=pl.Buffered(1)),
                  pl.BlockSpec((tm, d), row)],
        out_specs=pl.BlockSpec((tm, d), row), out_shape=SDS((t, d), F32),
        sem=("parallel",), side=side)


def _dw(at, b3, ka, nb, name, side=None):
    ka_tot, t = at.shape
    gb, _, nb_tot = b3.shape
    na, ncb = ka_tot // ka, nb_tot // nb
    tm = _tile(t, DW_TOKENS, 128)

    def body(a_ref, b_ref, o_ref):
        part = _dot(a_ref[...], b_ref[0])

        @pl.when(pl.program_id(2) == 0)
        def _():
            o_ref[0, 0] = part

        @pl.when(pl.program_id(2) != 0)
        def _():
            o_ref[0, 0] += part

    return _pcall(
        body, (at, b3), name=name, grid=(na, gb * ncb, t // tm),
        in_specs=[pl.BlockSpec((ka, tm), lambda ja, jb, i: (ja, i)),
                  pl.BlockSpec((1, tm, nb), lambda ja, jb, i: (jb // ncb, i, jb % ncb))],
        out_specs=pl.BlockSpec((1, 1, ka, nb), lambda ja, jb, i: (ja, jb, 0, 0)),
        out_shape=SDS((na, gb * ncb, ka, nb), F32),
        sem=("parallel", "parallel", "arbitrary"), side=side)


def _dw_groups(ats, bs, tokens, name, side=None):
    t = bs[0].shape[0]
    na, nb = len(ats), len(bs)
    roff, coff = [0], [0]
    for a in ats:
        roff.append(roff[-1] + a.shape[0])
    for b in bs:
        coff.append(coff[-1] + b.shape[1])
    tm = _tile(t, tokens, 128)

    def body(*refs):
        a_refs, b_refs, o_ref = refs[:na], refs[na:na + nb], refs[na + nb]
        first = pl.program_id(0) == 0
        for i in range(na):
            for j in range(nb):
                _acc_rows(o_ref.at[roff[i]:roff[i + 1], coff[j]:coff[j + 1]], _dot(a_refs[i][...], b_refs[j][...]), first)

    return _pcall(
        body, (*ats, *bs), name=name, grid=(t // tm,),
        in_specs=[pl.BlockSpec((a.shape[0], tm), lambda i: (0, i)) for a in ats]
        + [pl.BlockSpec((tm, b.shape[1]), lambda i: (i, 0)) for b in bs],
        out_specs=pl.BlockSpec((roff[-1], coff[-1]), lambda i: (0, 0)),
        out_shape=SDS((roff[-1], coff[-1]), F32), sem=("arbitrary",), side=side)


def _out_bwd(dy, z, gamma, wout, widths, name, side=None):
    t, d = dy.shape
    wa, wb, wc = widths
    tm = _tile(t, 512)

    def body(dy_ref, z_ref, g_ref, w_ref, dz_ref, dzb_ref, da_ref, dbb_ref, dc_ref, dg_ref, db_ref):
        dz, dgam, dbet = _ln_bwd(dy_ref[...], z_ref[...], g_ref[...])
        first = pl.program_id(0) == 0
        _acc_rows(dg_ref, dgam, first)
        _acc_rows(db_ref, dbet, first)
        dz_ref[...] = dz
        dzb = dz.astype(BF)
        dzb_ref[...] = dzb
        da_ref[...] = _dotg(dzb, w_ref[0:wa, :], NT).astype(BF)
        dbb_ref[...] = _dotg(dzb, w_ref[wa:wa + wb, :], NT).astype(BF)
        dc_ref[...] = _dotg(dzb, w_ref[wa + wb:wa + wb + wc, :], NT).astype(BF)

    row = lambda i: (i, 0)
    fixed = lambda i: (0, 0)
    return _pcall(
        body, (dy, z, gamma, wout), name=name, grid=(t // tm,),
        in_specs=[pl.BlockSpec((tm, d), row), pl.BlockSpec((tm, d), row), pl.BlockSpec((1, d), fixed),
                  pl.BlockSpec(wout.shape, fixed)],
        out_specs=[pl.BlockSpec((tm, d), row), pl.BlockSpec((tm, d), row), pl.BlockSpec((tm, wa), row),
                   pl.BlockSpec((tm, wb), row), pl.BlockSpec((tm, wc), row),
                   pl.BlockSpec((1, d), fixed), pl.BlockSpec((1, d), fixed)],
        out_shape=[SDS((t, d), F32), SDS((t, d), BF), SDS((t, wa), BF), SDS((t, wb), BF), SDS((t, wc), BF),
                   SDS((1, d), F32), SDS((1, d), F32)],
        sem=("arbitrary",), side=side)


def _conv_bwd(pc3, dya3, cw, name, side=None):
    b, s, c3 = pc3.shape
    c = c3 // 3

    def body(p_ref, dy_ref, w_ref, dp_ref, dw_ref):
        cb = p_ref[0, :, 0:c]
        cc = p_ref[0, :, c:2 * c]
        ch = p_ref[0, :, 2 * c:3 * c]
        z = cc * ch
        z1 = _shift_rows(z, 1, True)
        z2 = _shift_rows(z, 2, True)
        w0, w1, w2 = w_ref[0:1, :], w_ref[1:2, :], w_ref[2:3, :]
        dy = dy_ref[0].astype(F32)
        dconv = dy * cb
        dz = w2 * dconv + w1 * _shift_rows(dconv, 1, False) + w0 * _shift_rows(dconv, 2, False)
        dp_ref[0, :, 0:c] = (dy * (w0 * z2 + w1 * z1 + w2 * z)).astype(BF)
        dp_ref[0, :, c:2 * c] = (dz * ch).astype(BF)
        dp_ref[0, :, 2 * c:3 * c] = (dz * cc).astype(BF)
        first = pl.program_id(0) == 0
        for r, zs in enumerate((z2, z1, z)):
            _acc_rows(dw_ref.at[r:r + 1], jnp.sum(dconv * zs, axis=0, keepdims=True), first)

    blk = lambda i: (i, 0, 0)
    return _pcall(
        body, (pc3, dya3, cw), name=name, grid=(b,),
        in_specs=[pl.BlockSpec((1, s, c3), blk), pl.BlockSpec((1, s, c), blk), pl.BlockSpec((3, c), lambda i: (0, 0))],
        out_specs=[pl.BlockSpec((1, s, c3), blk), pl.BlockSpec((3, c), lambda i: (0, 0))],
        out_shape=[SDS((b, s, c3), BF), SDS((3, c), F32)],
        sem=("arbitrary",), side=side)


def _fox_bwd(pq3, cum4, lse4, dyb3, name, side=None):
    b, s, d3 = pq3.shape
    df = d3 // 3
    hp = df // 128
    tq = _tile(s, FOX_Q_BLOCK)
    scale = FOX_HEAD_DIM ** -0.5

    def body(q_ref, k_ref, v_ref, c_ref, lse_ref, do_ref, dq_ref, dk_ref, dv_ref, dc_ref, dk_acc, dv_acc):
        masks = _head_masks(128)
        dk_acc[...] = jnp.zeros_like(dk_acc)
        dv_acc[...] = jnp.zeros_like(dv_acc)
        dc_ref[...] = jnp.zeros_like(dc_ref)
        for i in range(s // tq):
            lo, hi = i * tq, (i + 1) * tq
            q = q_ref[0, lo:hi, :]
            do = do_ref[0, lo:hi, :]
            k = k_ref[0, 0:hi, :]
            v = v_ref[0, 0:hi, :]
            lse = lse_ref[0, 0, lo:hi, :]
            dq = jnp.zeros((tq, 128), F32)
            for e in range(2):
                dom = jnp.where(masks[e], do, 0)
                parts, qm = _fox_scores(q, k, c_ref[0, 0, e:e + 1, 0:hi], lo, masks[e])
                lse_e = lse[:, FOX_HEAD_DIM * e:FOX_HEAD_DIM * e + 1]
                probs = [jnp.exp(sc - lse_e) for sc, _, _ in parts]
                dps = [_dotg(dom, v[c0:c1], NT) for _, c0, c1 in parts]
                row = functools.reduce(lambda a, c: a + c,
                                       [jnp.sum(p * dp, axis=-1, keepdims=True) for p, dp in zip(probs, dps)])
                dq_e = 0.0
                for p, dp, (_, c0, c1) in zip(probs, dps, parts):
                    ds = p * (dp - row)
                    dsb = ds.astype(BF)
                    dq_e = dq_e + _dot(dsb, k[c0:c1])
                    dk_acc[c0:c1, :] += _dotg(dsb, qm, TN)
                    dv_acc[c0:c1, :] += _dotg(p.astype(BF), dom, TN)
                    dc_ref[0, 0, e:e + 1, c0:c1] -= jnp.sum(ds, axis=0, keepdims=True)
                dq = jnp.where(masks[e], dq_e * scale, dq)
            dq_ref[0, lo:hi, :] = dq.astype(BF)
        dk_ref[0] = dk_acc[...].astype(BF)
        dv_ref[0] = dv_acc[...].astype(BF)

    blk = lambda off: pl.BlockSpec((1, s, 128), lambda i, j: (i, 0, off + j))
    cblk = pl.BlockSpec((1, 1, 2, s), lambda i, j: (i, j, 0, 0))
    return _pcall(
        body, (pq3, pq3, pq3, cum4, lse4, dyb3), name=name, grid=(b, hp),
        in_specs=[blk(0), blk(hp), blk(2 * hp), cblk,
                  pl.BlockSpec((1, 1, s, 128), lambda i, j: (i, j, 0, 0)), blk(0)],
        out_specs=[blk(0), blk(0), blk(0), cblk],
        out_shape=[SDS((b, s, df), BF), SDS((b, s, df), BF), SDS((b, s, df), BF), SDS(cum4.shape, F32)],
        scratch_shapes=[pltpu.VMEM((s, 128), F32), pltpu.VMEM((s, 128), F32)],
        sem=("parallel", "parallel"), side=side)


def _cum_bwd(dcum, flog, xt, name, side=None):
    b, h, s = dcum.shape
    d = xt.shape[0]

    def body(dc_ref, fl_ref, xt_ref, dfl_ref, dbf_ref, dwf_ref):
        dfl = _prefix_sum_lanes(dc_ref[0], reverse=True) * _sigmoid(-fl_ref[0])
        dfl_ref[0] = dfl
        first = pl.program_id(0) == 0
        _acc_rows(dbf_ref, jnp.broadcast_to(jnp.sum(dfl, axis=-1, keepdims=True), (h, 128)), first)
        dflp = jnp.concatenate([dfl, jnp.zeros((HEAD_ROWS - h, s), F32)], axis=0).astype(BF)
        _acc_rows(dwf_ref, _dotg(dflp, xt_ref[...], NT)[0:h], first)

    blk = lambda i: (i, 0, 0)
    return _pcall(
        body, (dcum, flog, xt), name=name, grid=(b,),
        in_specs=[pl.BlockSpec((1, h, s), blk), pl.BlockSpec((1, h, s), blk), pl.BlockSpec((d, s), lambda i: (0, i))],
        out_specs=[pl.BlockSpec((1, h, s), blk), pl.BlockSpec((h, 128), lambda i: (0, 0)),
                   pl.BlockSpec((h, d), lambda i: (0, 0))],
        out_shape=[SDS((b, h, s), F32), SDS((h, 128), F32), SDS((h, d), F32)],
        sem=("arbitrary",), side=side)


def _sgu_bwd(ps, dyc, lng, lnb, ws, bs, name, side=None):
    t, ds2 = ps.shape
    ds = ds2 // 2
    ng, c, _ = ws.shape
    tm = _tile(t, 512, c)

    def body(p_ref, dy_ref, g_ref, b_ref, ws_ref, bs_ref, dp_ref, dws_ref, dbs_ref, dg_ref, db_ref, dvn_acc):
        tri, gmasks, wm, bias = _sgu_consts(ws_ref, bs_ref, ds)
        su = p_ref[:, 0:ds]
        sv = p_ref[:, ds:ds2]
        up = _gelu(su)
        gv = _gelu(sv)
        xhat, rstd = _ln_stats(gv)
        vnb = (xhat * g_ref[...] + b_ref[...]).astype(BF)
        dy = dy_ref[...].astype(F32)
        dws = [jnp.zeros((c, c), F32) for _ in range(ng)]
        dbs = [jnp.zeros((c, 1), F32) for _ in range(ng)]
        for n in range(tm // c):
            r0, r1 = n * c, (n + 1) * c
            mixed = _sgu_mix(wm, vnb[r0:r1], bias, gmasks)
            dp_ref[r0:r1, 0:ds] = (dy[r0:r1] * mixed * _gelu_grad(su[r0:r1])).astype(BF)
            dmix = dy[r0:r1] * up[r0:r1]
            dvn = jnp.zeros((c, ds), F32)
            for g in range(ng):
                dmg = jnp.where(gmasks[g], dmix, 0.0)
                dmb = dmg.astype(BF)
                dws[g] = dws[g] + _dotg(dmb, vnb[r0:r1], NT)
                dbs[g] = dbs[g] + jnp.sum(dmg, axis=-1, keepdims=True)
                dvn = dvn + _dotg(wm[g], dmb, TN)
            dvn_acc[r0:r1, :] = dvn
        dvn_all = dvn_acc[...]
        gdv = dvn_all * g_ref[...]
        m1 = jnp.mean(gdv, axis=-1, keepdims=True)
        m2 = jnp.mean(gdv * xhat, axis=-1, keepdims=True)
        dgv = rstd * (gdv - m1 - xhat * m2)
        dp_ref[:, ds:ds2] = (dgv * _gelu_grad(sv)).astype(BF)
        first = pl.program_id(0) == 0
        _acc_rows(dg_ref, jnp.sum(dvn_all * xhat, axis=0, keepdims=True), first)
        _acc_rows(db_ref, jnp.sum(dvn_all, axis=0, keepdims=True), first)
        for g in range(ng):
            _acc_rows(dws_ref.at[g], jnp.where(tri, dws[g], 0.0), first)
            _acc_rows(dbs_ref.at[g], dbs[g], first)

    row = lambda i: (i, 0)
    fixed2 = lambda i: (0, 0)
    fixed3 = lambda i: (0, 0, 0)
    return _pcall(
        body, (ps, dyc, lng, lnb, ws, bs), name=name, grid=(t // tm,),
        in_specs=[pl.BlockSpec((tm, ds2), row), pl.BlockSpec((tm, ds), row), pl.BlockSpec((1, ds), fixed2),
                  pl.BlockSpec((1, ds), fixed2), pl.BlockSpec(ws.shape, fixed3), pl.BlockSpec(bs.shape, fixed3)],
        out_specs=[pl.BlockSpec((tm, ds2), row), pl.BlockSpec(ws.shape, fixed3), pl.BlockSpec(bs.shape, fixed3),
                   pl.BlockSpec((1, ds), fixed2), pl.BlockSpec((1, ds), fixed2)],
        out_shape=[SDS((t, ds2), BF), SDS(ws.shape, F32), SDS(bs.shape, F32), SDS((1, ds), F32), SDS((1, ds), F32)],
        scratch_shapes=[pltpu.VMEM((tm, ds), F32)],
        sem=("arbitrary",), side=side)


def _mix_bwd_dx(dz, dconv, dq, dk, dv, dsgu, dflog, w_main, wft, seq, alpha, name, side=None):
    t, d = dz.shape
    groups = [dconv, dq, dk, dv, dsgu]
    offs = [0]
    for g in groups:
        offs.append(offs[-1] + g.shape[1])
    h = dflog.shape[1]
    tm = _tile(seq, 512)
    per_seq = seq // tm

    def body(dz_ref, a0, a1, a2, a3, a4, dfl_ref, w_ref, wf_ref, dx_ref):
        dflp = jnp.concatenate([dfl_ref[0], jnp.zeros((HEAD_ROWS - h, tm), F32)], axis=0).astype(BF)
        acc = alpha * dz_ref[...] + _dotg(dflp, wf_ref[...], TN)
        for k, a_ref in enumerate((a0, a1, a2, a3, a4)):
            acc = acc + _dotg(a_ref[...], w_ref[:, offs[k]:offs[k + 1]], NT)
        dx_ref[...] = acc

    row = lambda i: (i, 0)
    return _pcall(
        body, (dz, *groups, dflog, w_main, wft), name=name, grid=(t // tm,),
        in_specs=[pl.BlockSpec((tm, d), row)] + [pl.BlockSpec((tm, g.shape[1]), row) for g in groups]
        + [pl.BlockSpec((1, h, tm), lambda i: (i // per_seq, 0, i % per_seq)),
           pl.BlockSpec(w_main.shape, lambda i: (0, 0)), pl.BlockSpec(wft.shape, lambda i: (0, 0))],
        out_specs=pl.BlockSpec((tm, d), row), out_shape=SDS((t, d), F32),
        sem=("parallel",), side=side)


def _adam_math(w, g, m, v):
    c1 = 1.0 / (1.0 - ADAM_B1 ** ADAM_STEP)
    c2 = 1.0 / (1.0 - ADAM_B2 ** ADAM_STEP)
    nm = ADAM_B1 * m + (1.0 - ADAM_B1) * g
    nv = ADAM_B2 * v + (1.0 - ADAM_B2) * (g * g)
    delta = -ADAM_LR * ((nm * c1) / (jnp.sqrt(nv * c2) + ADAM_EPS) + ADAM_WD * w)
    return delta, nm, nv


def _adamw_small(w, g, m, v, name):
    r, c = w.shape
    tr = _tile(r, 512)

    def body(w_ref, g_ref, m_ref, v_ref, d_ref, nm_ref, nv_ref):
        d_ref[...], nm_ref[...], nv_ref[...] = _adam_math(w_ref[...], g_ref[...], m_ref[...], v_ref[...])

    blk = pl.BlockSpec((tr, c), lambda i: (i, 0))
    return _pcall(body, (w, g, m, v), name=name, grid=(r // tr,), in_specs=[blk] * 4, out_specs=[blk] * 3,
                  out_shape=[SDS((r, c), F32)] * 3, sem=("parallel",))


def _adamw_shard(w, m, v, tot, recv, cq, layer, prev, name, side=None):
    nl, xr, yc = w.shape
    h = xr // 2
    tr = _tile(h, 256)
    nt = h // tr

    def body(cq_ref, w_ref, m_ref, v_ref, t_ref, r_ref, *rest):
        g_ref, d_ref, nm_ref, nv_ref = rest[-4:]
        g = jnp.where(pl.program_id(0) == cq_ref[0], t_ref[...], r_ref[...])
        g_ref[0] = g
        d_ref[0], nm_ref[0], nv_ref[0] = _adam_math(w_ref[0], g, m_ref[0], v_ref[0])

    slab = pl.BlockSpec((1, tr, yc), lambda hf, i, cq_ref: (layer, hf * nt + i, 0))
    mine = pl.BlockSpec((tr, yc), lambda hf, i, cq_ref: (jnp.where(hf == cq_ref[0], i, 0), 0))
    theirs = pl.BlockSpec((tr, yc), lambda hf, i, cq_ref: (jnp.where(hf == cq_ref[0], 0, i), 0))
    operands = [w, m, v, tot, recv]
    in_specs = [slab, slab, slab, mine, theirs]
    aliases = None
    if prev is not None:
        operands += list(prev)
        in_specs += [HBM] * 4
        aliases = {6 + k: k for k in range(4)}
    return _pcall(body, operands, name=name, grid=(2, nt), prefetch=(cq,), in_specs=in_specs,
                  out_specs=[slab] * 4, out_shape=[SDS(w.shape, F32)] * 4, aliases=aliases,
                  sem=("parallel", "parallel"), side=side)


BIG = ("ffn1_w_up", "ffn1_w_down", "mix_w_in", "mix_w_out", "ffn2_w_up", "ffn2_w_down")
SMALL = ("ln1_g", "ln1_b", "fox_b_f", "sgu_ln_g", "sgu_ln_b", "sgu_w_s", "sgu_b_s", "ln2_g", "ln2_b", "ln3_g", "ln3_b")
ORDER = ("ln1_g", "ln1_b", "ffn1_w_up", "ffn1_w_down", "mix_w_in", "fox_b_f", "conv_w", "sgu_ln_g", "sgu_ln_b",
         "sgu_w_s", "sgu_b_s", "mix_w_out", "ln2_g", "ln2_b", "ffn2_w_up", "ffn2_w_down", "ln3_g", "ln3_b")


def _row(v):
    return v.reshape(1, -1)


class _Pipe:
    def __init__(self, stages):
        self.stages = list(stages)
        self.pos = 0
        self.last = None

    def kind(self):
        return self.stages[self.pos][0] if self.pos < len(self.stages) else None


class _Sched:
    def __init__(self):
        self.pipes = []
        self.n_alone = 0

    def add(self, stages):
        self.pipes.append(_Pipe(stages))

    def _take_comms(self, skip=None):
        jobs = []
        for p in self.pipes:
            if p is not skip and p.kind() == "comm":
                jobs.append((p, p.stages[p.pos][1]()))
        return jobs

    @staticmethod
    def _landed(jobs):
        for p, side in jobs:
            p.last = side.results
            p.pos += 1

    def carry(self, builder, *args, **kw):
        jobs = self._take_comms()
        res = builder(*args, side=_join([s for _, s in jobs]), **kw)
        self._landed(jobs)
        self._computes(ride=False)
        return res

    def _computes(self, ride):
        again = True
        while again:
            again = False
            for p in self.pipes:
                if p.kind() == "compute":
                    jobs = self._take_comms(skip=p) if ride else []
                    p.stages[p.pos][1](p.last, _join([s for _, s in jobs]))
                    p.pos += 1
                    self._landed(jobs)
                    again = True

    def drain(self):
        while any(p.kind() is not None for p in self.pipes):
            self._computes(ride=True)
            jobs = self._take_comms()
            if jobs:
                _run_side(_join([s for _, s in jobs]), "exchange_tail_%d" % self.n_alone)
                self.n_alone += 1
                self._landed(jobs)


def _forward_layer(x, xt, p, dims, alpha, l, ride, target=None):
    b, s = dims["b"], dims["s"]
    t, d = x.shape
    tag = "l%d_" % l

    def run(stage, builder, *args, **kw):
        side, on_done = ride.get(stage, (None, None))
        res = builder(*args, tag + stage, side=side, **kw)
        if on_done is not None:
            on_done()
        return res

    if xt is None:
        h1, a1, a1t, xt = run("ffn1_up", _ffn_up_fwd, x, p["wup1"], emit_xt=True)
    else:
        h1, a1, a1t = run("ffn1_up", _ffn_up_fwd, x, p["wup1"])
    z1, x1, x1t = run("ffn1_down", _res_ln_fwd, [a1], p["wd1"], x, p["ln1_g"], p["ln1_b"], alpha, 0.5)
    pc, pq, ps = run("mix_proj", _mix_proj_fwd, x1, p["win"], dims["proj_widths"])
    x1_3 = x1.reshape(b, s, d)
    flog, cum = run("fox_gate", _cum_fwd, x1_3, p["wft"], p["bf"])
    nh = flog.shape[1]
    cum4 = cum.reshape(b, nh // 2, 2, s)
    pc3 = pc.reshape(b, s, -1)
    pq3 = pq.reshape(b, s, -1)
    ya = run("conv", _conv_fwd, pc3, p["cw"]).reshape(t, -1)
    yb3, lse4 = run("fox", _fox_fwd, pq3, cum4)
    yb = yb3.reshape(t, -1)
    yc = run("sgu", _sgu_fwd, ps, p["sgu_g"], p["sgu_b"], p["ws"], p["bs"])
    z2, x2, x2t, yat, ybt, yct = run("mix_out", _res_ln_fwd, [ya, yb, yc], p["wout"], x1, p["ln2_g"], p["ln2_b"],
                                      alpha, 1.0, parts_t=True)
    h2, a2, a2t = run("ffn2_up", _ffn_up_fwd, x2, p["wup2"])
    z3, x3, x3t = run("ffn2_down", _res_ln_fwd, [a2], p["wd2"], x2, p["ln3_g"], p["ln3_b"], alpha, 0.5, target=target)
    saved = dict(xt=xt, h1=h1, a1t=a1t, z1=z1, x1=x1, x1t=x1t, pc3=pc3, pq3=pq3, ps=ps, flog=flog, cum4=cum4,
                 lse4=lse4, yat=yat, ybt=ybt, yct=yct, z2=z2, x2t=x2t, h2=h2, a2t=a2t, z3=z3)
    return x3, x3t, saved


def _ffn_backward(sched, emit, which, dy, z, gamma, wd, wup, h, a_t, x_in_t, alpha, tag, after_mid=None):
    dz, df, dh, dgam, dbet = sched.carry(_ffn_bwd_mid, dy, z, gamma, wd, h, tag + "_bwd_mid")
    if after_mid is not None:
        after_mid(dgam, dbet)
    nq, d, w = wup.shape
    emit(which + "_w_up", sched.carry(_dw, x_in_t, dh, d, w, tag + "_dw_up")[0])
    half = wd.shape[0] // 2
    emit(which + "_w_down", sched.carry(_dw, a_t, df[None], half, d, tag + "_dw_down").reshape(nq, -1, d))
    dx = sched.carry(_ffn_bwd_dx, dh, wup, dz, alpha, tag + "_bwd_dx")
    return dx, dgam, dbet


def _backward_layer(sched, emit, emit_small, dy, sv, p, dims, alpha, l):
    b, s = dims["b"], dims["s"]
    tag = "l%d_" % l
    t, d = dy.shape
    g = {}
    dx2, g["ln3_g"], g["ln3_b"] = _ffn_backward(sched, emit, "ffn2", dy, sv["z3"], p["ln3_g"], p["wd2"], p["wup2"],
                                                sv["h2"], sv["a2t"], sv["x2t"], alpha, tag + "ffn2")
    wa, wb, wc = sv["yat"].shape[0], sv["ybt"].shape[0], sv["yct"].shape[0]
    dz2, dz2b, dya, dyb, dyc, g["ln2_g"], g["ln2_b"] = sched.carry(
        _out_bwd, dx2, sv["z2"], p["ln2_g"], p["wout"], (wa, wb, wc), tag + "mix_out_bwd")
    emit("mix_w_out", sched.carry(_dw_groups, [sv["yat"], sv["ybt"], sv["yct"]], [dz2b], DW_TOKENS,
                                  tag + "dw_out").reshape(N_SHARDS, -1, d))
    dpc3, g["conv_w"] = sched.carry(_conv_bwd, sv["pc3"], dya.reshape(b, s, -1), p["cw"], tag + "conv_bwd")
    dq3, dk3, dv3, dcum4 = sched.carry(_fox_bwd, sv["pq3"], sv["cum4"], sv["lse4"], dyb.reshape(b, s, -1),
                                       tag + "fox_bwd")
    nh = sv["flog"].shape[1]
    dflog, dbf, dwft = sched.carry(_cum_bwd, dcum4.reshape(b, nh, s), sv["flog"], sv["x1t"], tag + "fox_gate_bwd")
    g["fox_b_f"] = dbf[:, 0]
    dps, g["sgu_w_s"], dbs, g["sgu_ln_g"], g["sgu_ln_b"] = sched.carry(
        _sgu_bwd, sv["ps"], dyc, p["sgu_g"], p["sgu_b"], p["ws"], p["bs"], tag + "sgu_bwd")
    g["sgu_b_s"] = dbs[:, :, 0]
    dpc = dpc3.reshape(t, -1)
    dq, dk, dv = dq3.reshape(t, -1), dk3.reshape(t, -1), dv3.reshape(t, -1)
    main = sched.carry(_dw_groups, [sv["x1t"]], [dpc, dq, dk, dv, dps], DW_TOKENS // 2, tag + "dw_in")
    n_main = main.shape[1] - dps.shape[1]
    w_in_grad = jnp.concatenate([main[:, :n_main], dwft.T, main[:, n_main:]], axis=1)
    emit("mix_w_in", jnp.moveaxis(w_in_grad.reshape(d, N_SHARDS, -1), 1, 0))
    dx1 = sched.carry(_mix_bwd_dx, dz2, dpc, dq, dk, dv, dps, dflog, p["win"], p["wft"], s, alpha, tag + "mix_bwd_dx")

    def small_ready(dgam, dbet):
        g["ln1_g"], g["ln1_b"] = dgam, dbet
        emit_small(g)

    dx0, _, _ = _ffn_backward(sched, emit, "ffn1", dx1, sv["z1"], p["ln1_g"], p["wd1"], p["wup1"],
                              sv["h1"], sv["a1t"], sv["xt"], alpha, tag + "ffn1", after_mid=small_ready)
    return dx0


def _span(n):
    return -(-n // 1024) * 1024


def _pack_rows(flat_list):
    return jnp.concatenate([jnp.pad(v, (0, _span(v.shape[0]) - v.shape[0])) for v in flat_list]).reshape(-1, 128)


def kernel(x, ln1_g, ln1_b, ffn1_w_up, ffn1_w_down, mix_w_in, fox_b_f, conv_w, sgu_ln_g, sgu_ln_b, sgu_w_s, sgu_b_s, mix_w_out, ln2_g, ln2_b, ffn2_w_up, ffn2_w_down, ln3_g, ln3_b, loss_target, m_ln1_g, m_ln1_b, m_ffn1_w_up, m_ffn1_w_down, m_mix_w_in, m_fox_b_f, m_conv_w, m_sgu_ln_g, m_sgu_ln_b, m_sgu_w_s, m_sgu_b_s, m_mix_w_out, m_ln2_g, m_ln2_b, m_ffn2_w_up, m_ffn2_w_down, m_ln3_g, m_ln3_b, v_ln1_g, v_ln1_b, v_ffn1_w_up, v_ffn1_w_down, v_mix_w_in, v_fox_b_f, v_conv_w, v_sgu_ln_g, v_sgu_ln_b, v_sgu_w_s, v_sgu_b_s, v_mix_w_out, v_ln2_g, v_ln2_b, v_ffn2_w_up, v_ffn2_w_down, v_ln3_g, v_ln3_b):
    wts = dict(ln1_g=ln1_g, ln1_b=ln1_b, ffn1_w_up=ffn1_w_up, ffn1_w_down=ffn1_w_down, mix_w_in=mix_w_in,
               fox_b_f=fox_b_f, conv_w=conv_w, sgu_ln_g=sgu_ln_g, sgu_ln_b=sgu_ln_b, sgu_w_s=sgu_w_s,
               sgu_b_s=sgu_b_s, mix_w_out=mix_w_out, ln2_g=ln2_g, ln2_b=ln2_b, ffn2_w_up=ffn2_w_up,
               ffn2_w_down=ffn2_w_down, ln3_g=ln3_g, ln3_b=ln3_b)
    mom = dict(ln1_g=m_ln1_g, ln1_b=m_ln1_b, ffn1_w_up=m_ffn1_w_up, ffn1_w_down=m_ffn1_w_down, mix_w_in=m_mix_w_in,
               fox_b_f=m_fox_b_f, conv_w=m_conv_w, sgu_ln_g=m_sgu_ln_g, sgu_ln_b=m_sgu_ln_b, sgu_w_s=m_sgu_w_s,
               sgu_b_s=m_sgu_b_s, mix_w_out=m_mix_w_out, ln2_g=m_ln2_g, ln2_b=m_ln2_b, ffn2_w_up=m_ffn2_w_up,
               ffn2_w_down=m_ffn2_w_down, ln3_g=m_ln3_g, ln3_b=m_ln3_b)
    var = dict(ln1_g=v_ln1_g, ln1_b=v_ln1_b, ffn1_w_up=v_ffn1_w_up, ffn1_w_down=v_ffn1_w_down, mix_w_in=v_mix_w_in,
               fox_b_f=v_fox_b_f, conv_w=v_conv_w, sgu_ln_g=v_sgu_ln_g, sgu_ln_b=v_sgu_ln_b, sgu_w_s=v_sgu_w_s,
               sgu_b_s=v_sgu_b_s, mix_w_out=v_mix_w_out, ln2_g=v_ln2_g, ln2_b=v_ln2_b, ffn2_w_up=v_ffn2_w_up,
               ffn2_w_down=v_ffn2_w_down, ln3_g=v_ln3_g, ln3_b=v_ln3_b)

    nl = ln1_g.shape[0]
    b, s, d = x.shape
    t = b * s
    alpha = (2 * nl) ** 0.25
    cw_sh = conv_w.shape[2]
    d_conv = cw_sh * N_SHARDS
    d_sgu = sgu_ln_g.shape[1]
    nh = fox_b_f.shape[1]
    d_fox = nh * FOX_HEAD_DIM
    n_main = 3 * d_conv + 3 * d_fox
    dims = dict(b=b, s=s, proj_widths=(3 * d_conv, 3 * d_fox, 2 * d_sgu))
    cpos = lax.axis_index("c").astype(jnp.int32)
    qpos = (2 * lax.axis_index("x") + lax.axis_index("y")).astype(jnp.int32)
    cq = jnp.stack([cpos, qpos])

    me = (2 * qpos + cpos).reshape(1)
    assert nl == 2, "the gather schedule below names the carriers of a two-layer step"

    conv_tile = jnp.pad(conv_w, ((0, 0), (0, 8 - conv_w.shape[1]), (0, 128 - cw_sh)))
    params = [dict(bf=fox_b_f[l].reshape(nh, 1), sgu_g=_row(sgu_ln_g[l]), sgu_b=_row(sgu_ln_b[l]), ws=sgu_w_s[l],
                   bs=sgu_b_s[l][:, :, None], ln1_g=_row(ln1_g[l]), ln1_b=_row(ln1_b[l]), ln2_g=_row(ln2_g[l]),
                   ln2_b=_row(ln2_b[l]), ln3_g=_row(ln3_g[l]), ln3_b=_row(ln3_b[l])) for l in range(nl)]

    def operands_of(k, arr):
        if k == "mix_w_in":
            w_in = jnp.moveaxis(arr, 0, 1).reshape(d, -1)
            return dict(win=jnp.concatenate([w_in[:, :n_main], w_in[:, n_main + nh:]], axis=1),
                        wft=jnp.pad(w_in[:, n_main:n_main + nh].T, ((0, HEAD_ROWS - nh), (0, 0))))
        if k == "conv_w":
            return dict(cw=jnp.moveaxis(arr[:, :3, :cw_sh], 0, 1).reshape(3, d_conv))
        if k in ("ffn1_w_up", "ffn2_w_up"):
            return {"wup" + k[3]: arr}
        return {dict(ffn1_w_down="wd1", ffn2_w_down="wd2", mix_w_out="wout")[k]: arr.reshape(-1, d)}

    def gather(l, keys):
        side = _side_gather([conv_tile[l] if k == "conv_w" else wts[k][l].astype(BF) for k in keys],
                            [k != "conv_w" for k in keys])

        def install():
            for k, arr in zip(keys, side.results):
                params[l].update(operands_of(k, arr))
        return side, install

    first, install_first = gather(0, ["ffn1_w_up"])
    _run_side(first, "gather_first")
    install_first()
    rides = [{"ffn1_up": gather(0, ["ffn1_w_down", "mix_w_in", "mix_w_out", "conv_w"]),
              "ffn1_down": gather(0, ["ffn2_w_up"]),
              "mix_proj": gather(0, ["ffn2_w_down"]),
              "fox": gather(1, ["ffn1_w_up", "ffn1_w_down", "mix_w_in", "mix_w_out", "conv_w"]),
              "ffn2_up": gather(1, ["ffn2_w_up", "ffn2_w_down"])}, {}]

    act, act_t = x.reshape(t, d), None
    saved = []
    for l in range(nl):
        act, act_t, sv = _forward_layer(act, act_t, params[l], dims, alpha, l, rides[l],
                                        target=loss_target.reshape(t, d) if l == nl - 1 else None)
        saved.append(sv)
    dy, loss_blk = act, act_t

    sched = _Sched()
    prev = {k: None for k in BIG}
    red = {}

    def emit_for(l):
        def emit(key, g):
            st = {}
            name = "l%d_%s" % (l, key)

            def pair_sum(res, side):
                st["p"] = _pair_sum(g, res[0], cq, "rs_pair_sum_" + name, side=side)

            def chip_sum(res, side):
                st["t"] = _chip_sum(st["p"], res[0], cq, "rs_chip_sum_" + name, side=side)

            def adamw(res, side):
                prev[key] = _adamw_shard(wts[key], mom[key], var[key], st["t"], res[0], cq, l, prev[key],
                                         "adamw_" + name, side=side)

            sched.add([("comm", lambda: _side_pair_send([g])), ("compute", pair_sum),
                       ("comm", lambda: _side_scatter([st["p"]])), ("compute", chip_sum),
                       ("comm", lambda: _side_pair_share([st["t"]])), ("compute", adamw)])
        return emit

    def emit_small_for(l):
        def emit_small(g):
            flat = [g[k].reshape(-1) for k in SMALL] + [g["conv_w"].reshape(-1)]
            if l == nl - 1:
                flat.append(loss_blk[0, 0:1])
            vec = _pack_rows(flat)

            def slot_sum(res, side):
                red[l] = _sum_slots(vec, res[0], me, "small_sum_l%d" % l, side=side)

            sched.add([("comm", lambda: _side_bcast(vec)), ("compute", slot_sum)])
        return emit_small

    for l in reversed(range(nl)):
        dy = _backward_layer(sched, emit_for(l), emit_small_for(l), dy, saved[l], params[l], dims, alpha, l)
    sched.drain()
    grad_x = dy.reshape(b, s, d)
    gfin, delta, new_m, new_v = {}, {}, {}, {}
    for k in BIG:
        gfin[k], delta[k], new_m[k], new_v[k] = prev[k]

    gsm = {k: [] for k in SMALL + ("conv_w",)}
    for l in range(nl):
        flat_l = red[l].reshape(-1)
        off = 0
        for k in SMALL:
            n = wts[k][l].size
            gsm[k].append(flat_l[off:off + n].reshape(wts[k][l].shape))
            off += _span(n)
        n = 3 * d_conv
        gsm["conv_w"].append(lax.dynamic_slice_in_dim(flat_l[off:off + n].reshape(3, d_conv), qpos * cw_sh, cw_sh,
                                                      axis=1))
        off += _span(n)
        if l == nl - 1:
            loss = flat_l[off]
    for k in gsm:
        gfin[k] = jnp.stack(gsm[k])
    small_keys = SMALL + ("conv_w",)
    sizes = [wts[k].size for k in small_keys]
    pk = lambda src: _pack_rows([src[k].reshape(-1) for k in small_keys])
    dl, nm, nv = _adamw_small(pk(wts), pk(gfin), pk(mom), pk(var), "adamw_small")
    off = 0
    for k, n in zip(small_keys, sizes):
        shp = wts[k].shape
        delta[k] = dl.reshape(-1)[off:off + n].reshape(shp)
        new_m[k] = nm.reshape(-1)[off:off + n].reshape(shp)
        new_v[k] = nv.reshape(-1)[off:off + n].reshape(shp)
        off += _span(n)

    return (loss, grad_x, *[gfin[k] for k in ORDER], *[delta[k] for k in ORDER],
            *[new_m[k] for k in ORDER], *[new_v[k] for k in ORDER])
```

```python
import functools

import jax
import jax.numpy as jnp
from jax import lax
from jax.experimental import pallas as pl
from jax.experimental.pallas import tpu as pltpu

F32 = jnp.float32
BF = jnp.bfloat16
SDS = jax.ShapeDtypeStruct
MESH = pl.DeviceIdType.MESH

LN_EPS = 1e-5
FOX_HEAD_DIM = 64
FOX_Q_BLOCK = 512
DW_TOKENS = 2048
HEAD_ROWS = 128
GELU_K = 0.7978845608028654
GELU_C = 0.044715
NEG_BIG = -1e30
N_SHARDS = 4

ADAM_LR = 0.001
ADAM_B1 = 0.9
ADAM_B2 = 0.999
ADAM_EPS = 1e-08
ADAM_WD = 0.01
ADAM_STEP = 10

VMEM_LIMIT_BYTES = 56 * 1024 * 1024
NT = (((1,), (1,)), ((), ()))
TN = (((0,), (0,)), ((), ()))
HBM = pl.BlockSpec(memory_space=pl.ANY)


def _tile(n, pref, mult=8):
    t = min(n, pref)
    while n % t or t % mult:
        t -= mult
    return t


def _dot(a, b):
    return jnp.dot(a, b, preferred_element_type=F32)


def _dotg(a, b, dims):
    return lax.dot_general(a, b, dims, preferred_element_type=F32)


def _sigmoid(x):
    return 1.0 / (1.0 + jnp.exp(-x))


def _gelu(x):
    return 0.5 * x * (1.0 + jnp.tanh(GELU_K * (x + GELU_C * x * x * x)))


def _gelu_grad(x):
    t = jnp.tanh(GELU_K * (x + GELU_C * x * x * x))
    return 0.5 * (1.0 + t) + 0.5 * x * (1.0 - t * t) * GELU_K * (1.0 + 3.0 * GELU_C * x * x)


def _ln_stats(z):
    mu = jnp.mean(z, axis=-1, keepdims=True)
    zc = z - mu
    var = jnp.mean(zc * zc, axis=-1, keepdims=True)
    rstd = lax.rsqrt(var + LN_EPS)
    return zc * rstd, rstd


def _ln_bwd(dy, z, g):
    xhat, rstd = _ln_stats(z)
    gdy = dy * g
    m1 = jnp.mean(gdy, axis=-1, keepdims=True)
    m2 = jnp.mean(gdy * xhat, axis=-1, keepdims=True)
    dz = rstd * (gdy - m1 - xhat * m2)
    return dz, jnp.sum(dy * xhat, axis=0, keepdims=True), jnp.sum(dy, axis=0, keepdims=True)


class _Side:
    def __init__(self, ins, out_shapes, sems, start, finish):
        self.ins, self.out_shapes, self.sems = list(ins), list(out_shapes), list(sems)
        self.start, self.finish = start, finish
        self.results = None


def _join(sides):
    sides = [s for s in sides if s is not None]
    if not sides:
        return None
    ins = [a for s in sides for a in s.ins]
    outs = [a for s in sides for a in s.out_shapes]
    sems = [a for s in sides for a in s.sems]

    def parts(seq, field):
        out, o = [], 0
        for s in sides:
            n = len(getattr(s, field))
            out.append(seq[o:o + n])
            o += n
        return out

    def run(which):
        def fn(i, o, m):
            for s, a, b, c in zip(sides, parts(i, "ins"), parts(o, "out_shapes"), parts(m, "sems")):
                getattr(s, which)(a, b, c)
        return fn

    joined = _Side(ins, outs, sems, run("start"), run("finish"))
    joined.members = sides
    return joined


def _deliver(side, results):
    members = getattr(side, "members", None)
    side.results = list(results)
    if members:
        o = 0
        for s in members:
            n = len(s.out_shapes)
            _deliver(s, results[o:o + n])
            o += n


def _pcall(body, operands, *, name, grid, in_specs, out_specs, out_shape, sem, scratch_shapes=(),
           prefetch=(), aliases=None, side=None):
    single = not isinstance(out_shape, (list, tuple))
    out_shape = [out_shape] if single else list(out_shape)
    out_specs = [out_specs] if single else list(out_specs)
    in_specs, scratch_shapes = list(in_specs), list(scratch_shapes)
    n_pre, n_in, n_out, n_sc = len(prefetch), len(in_specs), len(out_shape), len(scratch_shapes)
    fn = body
    extra = []
    if side is not None:
        s_in, s_out = len(side.ins), len(side.out_shapes)

        def fn(*refs):
            pre, rest = refs[:n_pre], refs[n_pre:]
            m_in, c_in = rest[:n_in], rest[n_in:n_in + s_in]
            rest = rest[n_in + s_in:]
            m_out, c_out = rest[:n_out], rest[n_out:n_out + s_out]
            rest = rest[n_out + s_out:]
            m_sc, c_sc = rest[:n_sc], rest[n_sc:]
            first = pl.program_id(0) == 0
            last = pl.program_id(0) == grid[0] - 1
            for a in range(1, len(grid)):
                first = jnp.logical_and(first, pl.program_id(a) == 0)
                last = jnp.logical_and(last, pl.program_id(a) == grid[a] - 1)

            @pl.when(first)
            def _():
                side.start(c_in, c_out, c_sc)

            body(*pre, *m_in, *m_out, *m_sc)

            @pl.when(last)
            def _():
                side.finish(c_in, c_out, c_sc)

        in_specs = in_specs + [HBM] * s_in
        out_specs = out_specs + [HBM] * s_out
        out_shape = out_shape + side.out_shapes
        scratch_shapes = scratch_shapes + side.sems
        extra = side.ins
        sem = ("arbitrary",) * len(grid)
    params = pltpu.CompilerParams(dimension_semantics=tuple(sem), vmem_limit_bytes=VMEM_LIMIT_BYTES)
    kw = dict(input_output_aliases=aliases) if aliases else {}
    if n_pre:
        spec = pltpu.PrefetchScalarGridSpec(num_scalar_prefetch=n_pre, grid=grid, in_specs=in_specs,
                                            out_specs=out_specs, scratch_shapes=scratch_shapes)
        call = pl.pallas_call(fn, name=name, grid_spec=spec, out_shape=out_shape, compiler_params=params, **kw)
    else:
        call = pl.pallas_call(fn, name=name, grid=grid, in_specs=in_specs, out_specs=out_specs,
                              out_shape=out_shape, scratch_shapes=scratch_shapes, compiler_params=params, **kw)
    res = call(*prefetch, *operands, *extra)
    if side is not None:
        _deliver(side, res[n_out:])
        res = res[:n_out]
    return res[0] if single else res


def _run_side(side, name):
    def body(*refs):
        n_in, n_out = len(side.ins), len(side.out_shapes)
        i, o, m = refs[:n_in], refs[n_in:n_in + n_out], refs[n_in + n_out:]
        side.start(i, o, m)
        side.finish(i, o, m)

    res = pl.pallas_call(body, name=name, in_specs=[HBM] * len(side.ins), out_specs=[HBM] * len(side.out_shapes),
                         out_shape=side.out_shapes, scratch_shapes=side.sems)(*side.ins)
    _deliver(side, res)


def _mesh_pos():
    x, y, c = lax.axis_index("x"), lax.axis_index("y"), lax.axis_index("c")
    chips = [(1 - x, y), (x, 1 - y), (1 - x, 1 - y)]
    return x, y, c, chips


def _rows(ref, lead, half, n_rows):
    return ref.at[tuple(lead) + (pl.ds(half * n_rows, n_rows),)]


def _side_gather(shards, split):
    n = len(shards)
    hs = [w.shape[0] // 2 for w in shards]

    def plan(ins, outs, sems):
        ssem, rsem = sems
        x, y, c, chips = _mesh_pos()
        q = 2 * x + y
        sib = (x, y, 1 - c)

        def rc(p, k, src, dst, to):
            return pltpu.make_async_remote_copy(src_ref=src, dst_ref=dst, send_sem=ssem.at[p, k],
                                                recv_sem=rsem.at[p, k], device_id=to, device_id_type=MESH)

        def blk(ref, p, qi, half):
            return _rows(ref, (qi,), half, hs[p]) if split[p] else ref.at[qi]

        return x, y, c, chips, q, sib, rc, blk

    def first_sends(ins, outs, sems):
        x, y, c, chips, q, sib, rc, blk = plan(ins, outs, sems)
        cps = [rc(p, 0, ins[p], outs[p].at[q], sib) for p in range(n)]
        for j, (cx, cy) in enumerate(chips):
            for p in range(n):
                src = _rows(ins[p], (), c, hs[p]) if split[p] else ins[p]
                cps.append(rc(p, 1 + j, src, blk(outs[p], p, q, c), (cx, cy, c)))
        return cps

    def start(ins, outs, sems):
        for cp in first_sends(ins, outs, sems):
            cp.start()

    def finish(ins, outs, sems):
        x, y, c, chips, q, sib, rc, blk = plan(ins, outs, sems)
        sent = first_sends(ins, outs, sems)
        for j, (cx, cy) in enumerate(chips):
            qj = 2 * cx + cy
            for p in range(n):
                got = blk(outs[p], p, qj, c)
                rc(p, 1 + j, got, got, (cx, cy, c)).wait_recv()
                if split[p]:
                    fwd = rc(p, 4 + j, got, got, sib)
                    fwd.start()
                    sent.append(fwd)
        for j, (cx, cy) in enumerate(chips):
            qj = 2 * cx + cy
            for p in range(n):
                if split[p]:
                    got = blk(outs[p], p, qj, 1 - c)
                    rc(p, 4 + j, got, got, sib).wait_recv()
        for p in range(n):
            rc(p, 0, outs[p].at[q], outs[p].at[q], sib).wait_recv()
        for cp in sent:
            cp.wait_send()

    return _Side(shards, [SDS((N_SHARDS,) + w.shape, w.dtype) for w in shards],
                 [pltpu.SemaphoreType.DMA((n, 7)), pltpu.SemaphoreType.DMA((n, 7))], start, finish)


def _side_pair_send(gs):
    n = len(gs)

    def copies(ins, outs, sems):
        x, y, c, _ = _mesh_pos()
        return [pltpu.make_async_remote_copy(
            src_ref=ins[p].at[:, pl.ds((1 - c) * (gs[p].shape[1] // 2), gs[p].shape[1] // 2)], dst_ref=outs[p],
            send_sem=sems[0].at[p], recv_sem=sems[1].at[p], device_id=(x, y, 1 - c), device_id_type=MESH)
            for p in range(n)]

    def start(ins, outs, sems):
        for cp in copies(ins, outs, sems):
            cp.start()

    def finish(ins, outs, sems):
        for cp in copies(ins, outs, sems):
            cp.wait()

    return _Side(gs, [SDS((g.shape[0], g.shape[1] // 2, g.shape[2]), g.dtype) for g in gs],
                 [pltpu.SemaphoreType.DMA((n,)), pltpu.SemaphoreType.DMA((n,))], start, finish)


def _side_scatter(ps):
    n = len(ps)

    def sends(ins, outs, sems):
        x, y, c, chips = _mesh_pos()
        q = 2 * x + y
        return [pltpu.make_async_remote_copy(src_ref=ins[p].at[2 * cx + cy], dst_ref=outs[p].at[q],
                                             send_sem=sems[0].at[p, j], recv_sem=sems[1].at[p, j],
                                             device_id=(cx, cy, c), device_id_type=MESH)
                for j, (cx, cy) in enumerate(chips) for p in range(n)]

    def start(ins, outs, sems):
        for cp in sends(ins, outs, sems):
            cp.start()

    def finish(ins, outs, sems):
        x, y, c, chips = _mesh_pos()
        for j, (cx, cy) in enumerate(chips):
            for p in range(n):
                got = outs[p].at[2 * cx + cy]
                pltpu.make_async_remote_copy(src_ref=got, dst_ref=got, send_sem=sems[0].at[p, j],
                                             recv_sem=sems[1].at[p, j], device_id=(cx, cy, c),
                                             device_id_type=MESH).wait_recv()
        for cp in sends(ins, outs, sems):
            cp.wait_send()

    return _Side(ps, [SDS(p.shape, p.dtype) for p in ps],
                 [pltpu.SemaphoreType.DMA((n, 3)), pltpu.SemaphoreType.DMA((n, 3))], start, finish)


def _side_pair_share(tots):
    n = len(tots)

    def copies(ins, outs, sems):
        x, y, c, _ = _mesh_pos()
        return [pltpu.make_async_remote_copy(src_ref=ins[p], dst_ref=outs[p], send_sem=sems[0].at[p],
                                             recv_sem=sems[1].at[p], device_id=(x, y, 1 - c), device_id_type=MESH)
                for p in range(n)]

    def start(ins, outs, sems):
        for cp in copies(ins, outs, sems):
            cp.start()

    def finish(ins, outs, sems):
        for cp in copies(ins, outs, sems):
            cp.wait()

    return _Side(tots, [SDS(t_.shape, t_.dtype) for t_ in tots],
                 [pltpu.SemaphoreType.DMA((n,)), pltpu.SemaphoreType.DMA((n,))], start, finish)


N_DEVICES = 8


def _side_bcast(v):
    def peers():
        x, y, c, _ = _mesh_pos()
        out = []
        for k in range(1, N_DEVICES):
            px, py, pc = x ^ ((k >> 2) & 1), y ^ ((k >> 1) & 1), c ^ (k & 1)
            out.append((k - 1, (px, py, pc), 4 * px + 2 * py + pc))
        return 4 * x + 2 * y + c, out

    def sends(ins, outs, sems):
        me, ps = peers()
        return [pltpu.make_async_remote_copy(src_ref=ins[0], dst_ref=outs[0].at[me], send_sem=sems[0].at[k],
                                             recv_sem=sems[1].at[k], device_id=to, device_id_type=MESH)
                for k, to, _ in ps]

    def start(ins, outs, sems):
        for cp in sends(ins, outs, sems):
            cp.start()

    def finish(ins, outs, sems):
        _, ps = peers()
        for k, to, slot in ps:
            got = outs[0].at[slot]
            pltpu.make_async_remote_copy(src_ref=got, dst_ref=got, send_sem=sems[0].at[k], recv_sem=sems[1].at[k],
                                         device_id=to, device_id_type=MESH).wait_recv()
        for cp in sends(ins, outs, sems):
            cp.wait_send()

    return _Side([v], [SDS((N_DEVICES,) + v.shape, v.dtype)],
                 [pltpu.SemaphoreType.DMA((N_DEVICES - 1,)), pltpu.SemaphoreType.DMA((N_DEVICES - 1,))], start, finish)


def _sum_slots(v, r, me, name, side=None):
    n, rows, lanes = r.shape
    tr = _tile(rows, 512)

    def body(me_ref, v_ref, r_ref, o_ref):
        j = pl.program_id(1)
        term = jnp.where(j == me_ref[0], v_ref[...], r_ref[0])

        @pl.when(j == 0)
        def _():
            o_ref[...] = term

        @pl.when(j != 0)
        def _():
            o_ref[...] += term

    other = lambda j, k: jnp.where(j == k, (k + 1) % n, j)
    return _pcall(
        body, (v, r), name=name, grid=(rows // tr, n), prefetch=(me,),
        in_specs=[pl.BlockSpec((tr, lanes), lambda i, j, me_ref: (i, 0)),
                  pl.BlockSpec((1, tr, lanes), lambda i, j, me_ref: (other(j, me_ref[0]), i, 0))],
        out_specs=pl.BlockSpec((tr, lanes), lambda i, j, me_ref: (i, 0)),
        out_shape=SDS((rows, lanes), F32), sem=("parallel", "arbitrary"), side=side)


def _pair_sum(g, r1, cq, name, side=None):
    nq, xr, yc = g.shape
    h = xr // 2
    tr = _tile(h, 512, 16)
    nt = h // tr

    def body(cq_ref, g_ref, r_ref, o_ref):
        o_ref[...] = (g_ref[...] + r_ref[...]).astype(BF)

    return _pcall(
        body, (g, r1), name=name, grid=(nq, nt), prefetch=(cq,),
        in_specs=[pl.BlockSpec((1, tr, yc), lambda j, i, cq_ref: (j, cq_ref[0] * nt + i, 0)),
                  pl.BlockSpec((1, tr, yc), lambda j, i, cq_ref: (j, i, 0))],
        out_specs=pl.BlockSpec((1, tr, yc), lambda j, i, cq_ref: (j, i, 0)),
        out_shape=SDS((nq, h, yc), BF), sem=("parallel", "parallel"), side=side)


def _chip_sum(p, r2, cq, name, side=None):
    nq, h, yc = r2.shape
    tr = _tile(h, 512, 16)

    def body(cq_ref, p_ref, r_ref, o_ref):
        j = pl.program_id(1)
        term = jnp.where(j == cq_ref[1], p_ref[0], r_ref[0]).astype(F32)

        @pl.when(j == 0)
        def _():
            o_ref[...] = term

        @pl.when(j != 0)
        def _():
            o_ref[...] += term

    other = lambda j, q: jnp.where(j == q, (q + 1) % nq, j)
    return _pcall(
        body, (p, r2), name=name, grid=(h // tr, nq), prefetch=(cq,),
        in_specs=[pl.BlockSpec((1, tr, yc), lambda i, j, cq_ref: (cq_ref[1], i, 0)),
                  pl.BlockSpec((1, tr, yc), lambda i, j, cq_ref: (other(j, cq_ref[1]), i, 0))],
        out_specs=pl.BlockSpec((tr, yc), lambda i, j, cq_ref: (i, 0)),
        out_shape=SDS((h, yc), F32), sem=("parallel", "arbitrary"), side=side)


def _ffn_up_fwd(x, wup, name, side=None, emit_xt=False):
    t, d = x.shape
    w = wup.shape[2]
    tm = _tile(t, 512)

    def body(x_ref, wg_ref, wu_ref, h_ref, a_ref, at_ref, *xt_ref):
        xb = x_ref[...].astype(BF)
        g = _dot(xb, wg_ref[0])
        u = _dot(xb, wu_ref[0])
        h_ref[0] = g.astype(BF)
        h_ref[1] = u.astype(BF)
        ab = (g * _sigmoid(g) * u).astype(BF)
        a_ref[...] = ab
        at_ref[...] = ab.T
        if emit_xt:
            @pl.when(pl.program_id(0) == 0)
            def _():
                xt_ref[0][...] = xb.T

    nt = t // tm
    out_specs = [pl.BlockSpec((2, tm, w), lambda j, i: (0, i, j)), pl.BlockSpec((tm, w), lambda j, i: (i, j)),
                 pl.BlockSpec((w, tm), lambda j, i: (j, i))]
    out_shape = [SDS((2, t, 2 * w), BF), SDS((t, 2 * w), BF), SDS((2 * w, t), BF)]
    if emit_xt:
        out_specs.append(pl.BlockSpec((d, tm), lambda j, i: (0, jnp.where(j == 0, i, nt - 1))))
        out_shape.append(SDS((d, t), BF))
    return _pcall(
        body, (x, wup, wup), name=name, grid=(2, nt),
        in_specs=[pl.BlockSpec((tm, d), lambda j, i: (i, 0)),
                  pl.BlockSpec((1, d, w), lambda j, i: (j, 0, 0)),
                  pl.BlockSpec((1, d, w), lambda j, i: (j + 2, 0, 0))],
        out_specs=out_specs, out_shape=out_shape,
        sem=("arbitrary", "arbitrary") if emit_xt else ("parallel", "parallel"), side=side)


def _res_ln_fwd(parts, w, x, gamma, beta, alpha, res_scale, name, side=None, parts_t=False, target=None):
    t, d = x.shape
    n = len(parts)
    offs = [0]
    for p in parts:
        offs.append(offs[-1] + p.shape[1])
    tm = _tile(t, 512)
    n_in = n + 4 + (target is not None)

    def body(*refs):
        p_refs = refs[:n]
        w_ref, x_ref, g_ref, b_ref = refs[n:n + 4]
        out = refs[n_in:]
        f = _dot(p_refs[0][...], w_ref[offs[0]:offs[1], :])
        for k in range(1, n):
            f = f + _dot(p_refs[k][...], w_ref[offs[k]:offs[k + 1], :])
        z = alpha * x_ref[...] + res_scale * f
        out[0][...] = z
        xhat, _ = _ln_stats(z)
        y = xhat * g_ref[...] + b_ref[...]
        if target is not None:
            err = y - refs[n + 4][...]
            out[1][...] = err * (1.0 / d)
            part = 0.5 * jnp.sum(jnp.sum(err * err, axis=-1, keepdims=True) * (1.0 / d), axis=0, keepdims=True)
            _acc_rows(out[2], jnp.broadcast_to(part, (8, 128)), pl.program_id(0) == 0)
            return
        out[1][...] = y
        out[2][...] = y.astype(BF).T
        for k in range(len(out) - 3):
            out[3 + k][...] = p_refs[k][...].T

    row = lambda i: (i, 0)
    col = lambda i: (0, i)
    fixed = lambda i: (0, 0)
    operands = [*parts, w, x, gamma, beta]
    in_specs = [pl.BlockSpec((tm, p.shape[1]), row) for p in parts] + [
        pl.BlockSpec(w.shape, fixed), pl.BlockSpec((tm, d), row), pl.BlockSpec((1, d), fixed), pl.BlockSpec((1, d), fixed)]
    if target is not None:
        operands.append(target)
        in_specs.append(pl.BlockSpec((tm, d), row))
        out_specs = [pl.BlockSpec((tm, d), row), pl.BlockSpec((tm, d), row), pl.BlockSpec((8, 128), fixed)]
        out_shape = [SDS((t, d), F32), SDS((t, d), F32), SDS((8, 128), F32)]
    else:
        out_specs = [pl.BlockSpec((tm, d), row), pl.BlockSpec((tm, d), row), pl.BlockSpec((d, tm), col)]
        out_shape = [SDS((t, d), F32), SDS((t, d), F32), SDS((d, t), BF)]
        if parts_t:
            out_specs += [pl.BlockSpec((p.shape[1], tm), col) for p in parts]
            out_shape += [SDS((p.shape[1], t), BF) for p in parts]
    return _pcall(
        body, operands, name=name, grid=(t // tm,), in_specs=in_specs, out_specs=out_specs, out_shape=out_shape,
        sem=("arbitrary",) if target is not None else ("parallel",), side=side)


def _mix_proj_fwd(x, w_main, widths, name, side=None):
    t, d = x.shape
    dc, dq, ds = widths
    tm = _tile(t, 512)

    def body(x_ref, w_ref, pc_ref, pq_ref, ps_ref):
        xb = x_ref[...].astype(BF)
        pc_ref[...] = _dot(xb, w_ref[:, 0:dc])
        pq_ref[...] = _dot(xb, w_ref[:, dc:dc + dq]).astype(BF)
        ps_ref[...] = _dot(xb, w_ref[:, dc + dq:dc + dq + ds])

    row = lambda i: (i, 0)
    return _pcall(
        body, (x, w_main), name=name, grid=(t // tm,),
        in_specs=[pl.BlockSpec((tm, d), row), pl.BlockSpec(w_main.shape, lambda i: (0, 0))],
        out_specs=[pl.BlockSpec((tm, dc), row), pl.BlockSpec((tm, dq), row), pl.BlockSpec((tm, ds), row)],
        out_shape=[SDS((t, dc), F32), SDS((t, dq), BF), SDS((t, ds), F32)],
        sem=("parallel",), side=side)


def _prefix_sum_lanes(v, reverse):
    n = v.shape[-1]
    lane = lax.broadcasted_iota(jnp.int32, v.shape, v.ndim - 1)
    sh = 1
    while sh < n:
        if reverse:
            v = v + jnp.where(lane < n - sh, pltpu.roll(v, n - sh, axis=v.ndim - 1), 0.0)
        else:
            v = v + jnp.where(lane >= sh, pltpu.roll(v, sh, axis=v.ndim - 1), 0.0)
        sh *= 2
    return v


def _cum_fwd(x3, wft, bf, name, side=None):
    b, s, d = x3.shape
    h = bf.shape[0]

    def body(x_ref, w_ref, b_ref, fl_ref, cum_ref):
        fl = _dotg(w_ref[...], x_ref[0].astype(BF), NT)[0:h] + b_ref[...]
        fl_ref[0] = fl
        lf = jnp.minimum(fl, 0.0) - jnp.log(1.0 + jnp.exp(-jnp.abs(fl)))
        cum_ref[0] = _prefix_sum_lanes(lf, reverse=False)

    return _pcall(
        body, (x3, wft, bf), name=name, grid=(b,),
        in_specs=[pl.BlockSpec((1, s, d), lambda i: (i, 0, 0)),
                  pl.BlockSpec(wft.shape, lambda i: (0, 0)), pl.BlockSpec((h, 1), lambda i: (0, 0))],
        out_specs=[pl.BlockSpec((1, h, s), lambda i: (i, 0, 0)), pl.BlockSpec((1, h, s), lambda i: (i, 0, 0))],
        out_shape=[SDS((b, h, s), F32), SDS((b, h, s), F32)],
        sem=("parallel",), side=side)


def _shift_rows(z, k, down):
    n = z.shape[0]
    row = lax.broadcasted_iota(jnp.int32, z.shape, 0)
    if down:
        return jnp.where(row >= k, pltpu.roll(z, k, axis=0), 0.0)
    return jnp.where(row < n - k, pltpu.roll(z, n - k, axis=0), 0.0)


def _conv_fwd(pc3, cw, name, side=None):
    b, s, c3 = pc3.shape
    c = c3 // 3

    def body(p_ref, w_ref, y_ref):
        z = p_ref[0, :, c:2 * c] * p_ref[0, :, 2 * c:3 * c]
        conv = w_ref[0:1, :] * _shift_rows(z, 2, True) + w_ref[1:2, :] * _shift_rows(z, 1, True) + w_ref[2:3, :] * z
        y_ref[0] = (p_ref[0, :, 0:c] * conv).astype(BF)

    return _pcall(
        body, (pc3, cw), name=name, grid=(b,),
        in_specs=[pl.BlockSpec((1, s, c3), lambda i: (i, 0, 0)), pl.BlockSpec((3, c), lambda i: (0, 0))],
        out_specs=pl.BlockSpec((1, s, c), lambda i: (i, 0, 0)),
        out_shape=SDS((b, s, c), BF), sem=("parallel",), side=side)


def _head_masks(width):
    lane = lax.broadcasted_iota(jnp.int32, (1, width), 1)
    return [lane < FOX_HEAD_DIM, lane >= FOX_HEAD_DIM]


def _fox_scores(q, k, cum_row, lo, head_mask):
    tq = q.shape[0]
    qm = jnp.where(head_mask, q * (FOX_HEAD_DIM ** -0.5), 0)
    s = _dotg(qm, k, NT) - cum_row
    tri = lax.broadcasted_iota(jnp.int32, (tq, tq), 1) <= lax.broadcasted_iota(jnp.int32, (tq, tq), 0)
    parts = [(jnp.where(tri, s[:, lo:], NEG_BIG), lo, lo + tq)]
    if lo:
        parts.insert(0, (s[:, :lo], 0, lo))
    return parts, qm


def _fox_fwd(pq3, cum4, name, side=None):
    b, s, d3 = pq3.shape
    df = d3 // 3
    hp = df // 128
    tq = _tile(s, FOX_Q_BLOCK)

    def body(q_ref, k_ref, v_ref, c_ref, o_ref, lse_ref):
        masks = _head_masks(128)
        for i in range(s // tq):
            lo, hi = i * tq, (i + 1) * tq
            q = q_ref[0, lo:hi, :]
            k = k_ref[0, 0:hi, :]
            v = v_ref[0, 0:hi, :]
            o = jnp.zeros((tq, 128), F32)
            lse = jnp.zeros((tq, 128), F32)
            for e in range(2):
                parts, _ = _fox_scores(q, k, c_ref[0, 0, e:e + 1, 0:hi], lo, masks[e])
                m = functools.reduce(jnp.maximum, [jnp.max(sc, axis=-1, keepdims=True) for sc, _, _ in parts])
                l, pv = 0.0, 0.0
                for sc, c0, c1 in parts:
                    p = jnp.exp(sc - m)
                    l = l + jnp.sum(p, axis=-1, keepdims=True)
                    pv = pv + _dot(p.astype(BF), v[c0:c1])
                o = jnp.where(masks[e], pv * (1.0 / l), o)
                lse = jnp.where(masks[e], m + jnp.log(l), lse)
            o_ref[0, lo:hi, :] = o.astype(BF)
            lse_ref[0, 0, lo:hi, :] = lse

    blk = lambda off: pl.BlockSpec((1, s, 128), lambda i, j: (i, 0, off + j))
    return _pcall(
        body, (pq3, pq3, pq3, cum4), name=name, grid=(b, hp),
        in_specs=[blk(0), blk(hp), blk(2 * hp), pl.BlockSpec((1, 1, 2, s), lambda i, j: (i, j, 0, 0))],
        out_specs=[blk(0), pl.BlockSpec((1, 1, s, 128), lambda i, j: (i, j, 0, 0))],
        out_shape=[SDS((b, s, df), BF), SDS((b, hp, s, 128), F32)],
        sem=("parallel", "parallel"), side=side)


def _sgu_mix(wm, vnb, bias, gmasks):
    out = bias
    for g in range(len(wm)):
        out = out + jnp.where(gmasks[g], _dot(wm[g], vnb), 0.0)
    return out


def _sgu_consts(ws_ref, bs_ref, ds):
    ng, c, _ = ws_ref.shape
    gd = ds // ng
    tri = lax.broadcasted_iota(jnp.int32, (c, c), 0) >= lax.broadcasted_iota(jnp.int32, (c, c), 1)
    lane = lax.broadcasted_iota(jnp.int32, (1, ds), 1)
    gmasks = [(lane >= g * gd) & (lane < (g + 1) * gd) for g in range(ng)]
    wm = [jnp.where(tri, ws_ref[g], 0.0).astype(BF) for g in range(ng)]
    bias = jnp.zeros((c, ds), F32)
    for g in range(ng):
        bias = jnp.where(gmasks[g], bs_ref[g], bias)
    return tri, gmasks, wm, bias


def _sgu_fwd(ps, lng, lnb, ws, bs, name, side=None):
    t, ds2 = ps.shape
    ds = ds2 // 2
    c = ws.shape[1]
    tm = _tile(t, 512, c)

    def body(p_ref, g_ref, b_ref, ws_ref, bs_ref, y_ref):
        _, gmasks, wm, bias = _sgu_consts(ws_ref, bs_ref, ds)
        up = _gelu(p_ref[:, 0:ds])
        xhat, _ = _ln_stats(_gelu(p_ref[:, ds:ds2]))
        vnb = (xhat * g_ref[...] + b_ref[...]).astype(BF)
        for n in range(tm // c):
            r0, r1 = n * c, (n + 1) * c
            y_ref[r0:r1, :] = (up[r0:r1] * _sgu_mix(wm, vnb[r0:r1], bias, gmasks)).astype(BF)

    fixed2 = lambda i: (0, 0)
    fixed3 = lambda i: (0, 0, 0)
    return _pcall(
        body, (ps, lng, lnb, ws, bs), name=name, grid=(t // tm,),
        in_specs=[pl.BlockSpec((tm, ds2), lambda i: (i, 0)), pl.BlockSpec((1, ds), fixed2),
                  pl.BlockSpec((1, ds), fixed2), pl.BlockSpec(ws.shape, fixed3), pl.BlockSpec(bs.shape, fixed3)],
        out_specs=pl.BlockSpec((tm, ds), lambda i: (i, 0)),
        out_shape=SDS((t, ds), BF), sem=("parallel",), side=side)


def _acc_rows(ref, val, first):
    @pl.when(first)
    def _():
        ref[...] = val

    @pl.when(jnp.logical_not(first))
    def _():
        ref[...] += val


def _ffn_bwd_mid(dy, z, gamma, wd, h, name, side=None):
    t, d = dy.shape
    dff = wd.shape[0]
    half = dff // 2
    tm = _tile(t, 512)

    def body(dy_ref, z_ref, g_ref, wd_ref, h_ref, dz_ref, df_ref, dh_ref, dg_ref, db_ref):
        dz, dgam, dbet = _ln_bwd(dy_ref[...], z_ref[...], g_ref[...])
        first = pl.program_id(0) == 0
        _acc_rows(dg_ref, dgam, first)
        _acc_rows(db_ref, dbet, first)
        dz_ref[...] = dz
        dfb = (0.5 * dz).astype(BF)
        df_ref[...] = dfb
        for j in range(2):
            c0, c1 = j * half, (j + 1) * half
            da = _dotg(dfb, wd_ref[c0:c1, :], NT).astype(BF)
            g = h_ref[0, :, c0:c1]
            u = h_ref[1, :, c0:c1]
            sg = _sigmoid(g)
            dh_ref[0, :, c0:c1] = da * u * sg * (1.0 + g * (1.0 - sg))
            dh_ref[1, :, c0:c1] = da * g * sg

    row = lambda i: (i, 0)
    fixed = lambda i: (0, 0)
    return _pcall(
        body, (dy, z, gamma, wd, h), name=name, grid=(t // tm,),
        in_specs=[pl.BlockSpec((tm, d), row), pl.BlockSpec((tm, d), row), pl.BlockSpec((1, d), fixed),
                  pl.BlockSpec(wd.shape, fixed, pipeline_mode=pl.Buffered(1)),
                  pl.BlockSpec((2, tm, dff), lambda i: (0, i, 0))],
        out_specs=[pl.BlockSpec((tm, d), row), pl.BlockSpec((tm, d), row),
                   pl.BlockSpec((2, tm, dff), lambda i: (0, i, 0)),
                   pl.BlockSpec((1, d), fixed), pl.BlockSpec((1, d), fixed)],
        out_shape=[SDS((t, d), F32), SDS((t, d), BF), SDS((2, t, dff), BF), SDS((1, d), F32), SDS((1, d), F32)],
        sem=("arbitrary",), side=side)


def _ffn_bwd_dx(dh, wup, dz, alpha, name, side=None):
    _, t, dff = dh.shape
    nq, d, w = wup.shape
    per = dff // w
    tm = _tile(t, 512)

    def body(dh_ref, w_ref, dz_ref, dx_ref):
        acc = alpha * dz_ref[...]
        for q in range(nq):
            c0 = (q % per) * w
            acc = acc + _dotg(dh_ref[q // per, :, c0:c0 + w], w_ref[q], NT)
        dx_ref[...] = acc

    row = lambda i: (i, 0)
    return _pcall(
        body, (dh, wup, dz), name=name, grid=(t // tm,),
        in_specs=[pl.BlockSpec((2, tm, dff), lambda i: (0, i, 0)),
                  pl.BlockSpec(wup.shape, lambda i: (0, 0, 0), pipeline_mode=pl.Buffered(1)),
                  pl.BlockSpec((tm, d), row)],
        out_specs=pl.BlockSpec((tm, d), row), out_shape=SDS((t, d), F32),
        sem=("parallel",), side=side)


def _dw(at, b3, ka, nb, name, side=None):
    ka_tot, t = at.shape
    gb, _, nb_tot = b3.shape
    na, ncb = ka_tot // ka, nb_tot // nb
    tm = _tile(t, DW_TOKENS, 128)

    def body(a_ref, b_ref, o_ref):
        part = _dot(a_ref[...], b_ref[0])

        @pl.when(pl.program_id(2) == 0)
        def _():
            o_ref[0, 0] = part

        @pl.when(pl.program_id(2) != 0)
        def _():
            o_ref[0, 0] += part

    return _pcall(
        body, (at, b3), name=name, grid=(na, gb * ncb, t // tm),
        in_specs=[pl.BlockSpec((ka, tm), lambda ja, jb, i: (ja, i)),
                  pl.BlockSpec((1, tm, nb), lambda ja, jb, i: (jb // ncb, i, jb % ncb))],
        out_specs=pl.BlockSpec((1, 1, ka, nb), lambda ja, jb, i: (ja, jb, 0, 0)),
        out_shape=SDS((na, gb * ncb, ka, nb), F32),
        sem=("parallel", "parallel", "arbitrary"), side=side)


def _dw_groups(ats, bs, tokens, name, side=None):
    t = bs[0].shape[0]
    na, nb = len(ats), len(bs)
    roff, coff = [0], [0]
    for a in ats:
        roff.append(roff[-1] + a.shape[0])
    for b in bs:
        coff.append(coff[-1] + b.shape[1])
    tm = _tile(t, tokens, 128)

    def body(*refs):
        a_refs, b_refs, o_ref = refs[:na], refs[na:na + nb], refs[na + nb]
        first = pl.program_id(0) == 0
        for i in range(na):
            for j in range(nb):
                _acc_rows(o_ref.at[roff[i]:roff[i + 1], coff[j]:coff[j + 1]], _dot(a_refs[i][...], b_refs[j][...]), first)

    return _pcall(
        body, (*ats, *bs), name=name, grid=(t // tm,),
        in_specs=[pl.BlockSpec((a.shape[0], tm), lambda i: (0, i)) for a in ats]
        + [pl.BlockSpec((tm, b.shape[1]), lambda i: (i, 0)) for b in bs],
        out_specs=pl.BlockSpec((roff[-1], coff[-1]), lambda i: (0, 0)),
        out_shape=SDS((roff[-1], coff[-1]), F32), sem=("arbitrary",), side=side)


def _out_bwd(dy, z, gamma, wout, widths, name, side=None):
    t, d = dy.shape
    wa, wb, wc = widths
    tm = _tile(t, 512)

    def body(dy_ref, z_ref, g_ref, w_ref, dz_ref, dzb_ref, da_ref, dbb_ref, dc_ref, dg_ref, db_ref):
        dz, dgam, dbet = _ln_bwd(dy_ref[...], z_ref[...], g_ref[...])
        first = pl.program_id(0) == 0
        _acc_rows(dg_ref, dgam, first)
        _acc_rows(db_ref, dbet, first)
        dz_ref[...] = dz
        dzb = dz.astype(BF)
        dzb_ref[...] = dzb
        da_ref[...] = _dotg(dzb, w_ref[0:wa, :], NT).astype(BF)
        dbb_ref[...] = _dotg(dzb, w_ref[wa:wa + wb, :], NT).astype(BF)
        dc_ref[...] = _dotg(dzb, w_ref[wa + wb:wa + wb + wc, :], NT).astype(BF)

    row = lambda i: (i, 0)
    fixed = lambda i: (0, 0)
    return _pcall(
        body, (dy, z, gamma, wout), name=name, grid=(t // tm,),
        in_specs=[pl.BlockSpec((tm, d), row), pl.BlockSpec((tm, d), row), pl.BlockSpec((1, d), fixed),
                  pl.BlockSpec(wout.shape, fixed)],
        out_specs=[pl.BlockSpec((tm, d), row), pl.BlockSpec((tm, d), row), pl.BlockSpec((tm, wa), row),
                   pl.BlockSpec((tm, wb), row), pl.BlockSpec((tm, wc), row),
                   pl.BlockSpec((1, d), fixed), pl.BlockSpec((1, d), fixed)],
        out_shape=[SDS((t, d), F32), SDS((t, d), BF), SDS((t, wa), BF), SDS((t, wb), BF), SDS((t, wc), BF),
                   SDS((1, d), F32), SDS((1, d), F32)],
        sem=("arbitrary",), side=side)


def _conv_bwd(pc3, dya3, cw, name, side=None):
    b, s, c3 = pc3.shape
    c = c3 // 3

    def body(p_ref, dy_ref, w_ref, dp_ref, dw_ref):
        cb = p_ref[0, :, 0:c]
        cc = p_ref[0, :, c:2 * c]
        ch = p_ref[0, :, 2 * c:3 * c]
        z = cc * ch
        z1 = _shift_rows(z, 1, True)
        z2 = _shift_rows(z, 2, True)
        w0, w1, w2 = w_ref[0:1, :], w_ref[1:2, :], w_ref[2:3, :]
        dy = dy_ref[0].astype(F32)
        dconv = dy * cb
        dz = w2 * dconv + w1 * _shift_rows(dconv, 1, False) + w0 * _shift_rows(dconv, 2, False)
        dp_ref[0, :, 0:c] = (dy * (w0 * z2 + w1 * z1 + w2 * z)).astype(BF)
        dp_ref[0, :, c:2 * c] = (dz * ch).astype(BF)
        dp_ref[0, :, 2 * c:3 * c] = (dz * cc).astype(BF)
        first = pl.program_id(0) == 0
        for r, zs in enumerate((z2, z1, z)):
            _acc_rows(dw_ref.at[r:r + 1], jnp.sum(dconv * zs, axis=0, keepdims=True), first)

    blk = lambda i: (i, 0, 0)
    return _pcall(
        body, (pc3, dya3, cw), name=name, grid=(b,),
        in_specs=[pl.BlockSpec((1, s, c3), blk), pl.BlockSpec((1, s, c), blk), pl.BlockSpec((3, c), lambda i: (0, 0))],
        out_specs=[pl.BlockSpec((1, s, c3), blk), pl.BlockSpec((3, c), lambda i: (0, 0))],
        out_shape=[SDS((b, s, c3), BF), SDS((3, c), F32)],
        sem=("arbitrary",), side=side)


def _fox_bwd(pq3, cum4, lse4, dyb3, name, side=None):
    b, s, d3 = pq3.shape
    df = d3 // 3
    hp = df // 128
    tq = _tile(s, FOX_Q_BLOCK)
    scale = FOX_HEAD_DIM ** -0.5

    def body(q_ref, k_ref, v_ref, c_ref, lse_ref, do_ref, dq_ref, dk_ref, dv_ref, dc_ref, dk_acc, dv_acc):
        masks = _head_masks(128)
        dk_acc[...] = jnp.zeros_like(dk_acc)
        dv_acc[...] = jnp.zeros_like(dv_acc)
        dc_ref[...] = jnp.zeros_like(dc_ref)
        for i in range(s // tq):
            lo, hi = i * tq, (i + 1) * tq
            q = q_ref[0, lo:hi, :]
            do = do_ref[0, lo:hi, :]
            k = k_ref[0, 0:hi, :]
            v = v_ref[0, 0:hi, :]
            lse = lse_ref[0, 0, lo:hi, :]
            dq = jnp.zeros((tq, 128), F32)
            for e in range(2):
                dom = jnp.where(masks[e], do, 0)
                parts, qm = _fox_scores(q, k, c_ref[0, 0, e:e + 1, 0:hi], lo, masks[e])
                lse_e = lse[:, FOX_HEAD_DIM * e:FOX_HEAD_DIM * e + 1]
                probs = [jnp.exp(sc - lse_e) for sc, _, _ in parts]
                dps = [_dotg(dom, v[c0:c1], NT) for _, c0, c1 in parts]
                row = functools.reduce(lambda a, c: a + c,
                                       [jnp.sum(p * dp, axis=-1, keepdims=True) for p, dp in zip(probs, dps)])
                dq_e = 0.0
                for p, dp, (_, c0, c1) in zip(probs, dps, parts):
                    ds = p * (dp - row)
                    dsb = ds.astype(BF)
                    dq_e = dq_e + _dot(dsb, k[c0:c1])
                    dk_acc[c0:c1, :] += _dotg(dsb, qm, TN)
                    dv_acc[c0:c1, :] += _dotg(p.astype(BF), dom, TN)
                    dc_ref[0, 0, e:e + 1, c0:c1] -= jnp.sum(ds, axis=0, keepdims=True)
                dq = jnp.where(masks[e], dq_e * scale, dq)
            dq_ref[0, lo:hi, :] = dq.astype(BF)
        dk_ref[0] = dk_acc[...].astype(BF)
        dv_ref[0] = dv_acc[...].astype(BF)

    blk = lambda off: pl.BlockSpec((1, s, 128), lambda i, j: (i, 0, off + j))
    cblk = pl.BlockSpec((1, 1, 2, s), lambda i, j: (i, j, 0, 0))
    return _pcall(
        body, (pq3, pq3, pq3, cum4, lse4, dyb3), name=name, grid=(b, hp),
        in_specs=[blk(0), blk(hp), blk(2 * hp), cblk,
                  pl.BlockSpec((1, 1, s, 128), lambda i, j: (i, j, 0, 0)), blk(0)],
        out_specs=[blk(0), blk(0), blk(0), cblk],
        out_shape=[SDS((b, s, df), BF), SDS((b, s, df), BF), SDS((b, s, df), BF), SDS(cum4.shape, F32)],
        scratch_shapes=[pltpu.VMEM((s, 128), F32), pltpu.VMEM((s, 128), F32)],
        sem=("parallel", "parallel"), side=side)


def _cum_bwd(dcum, flog, xt, name, side=None):
    b, h, s = dcum.shape
    d = xt.shape[0]

    def body(dc_ref, fl_ref, xt_ref, dfl_ref, dbf_ref, dwf_ref):
        dfl = _prefix_sum_lanes(dc_ref[0], reverse=True) * _sigmoid(-fl_ref[0])
        dfl_ref[0] = dfl
        first = pl.program_id(0) == 0
        _acc_rows(dbf_ref, jnp.broadcast_to(jnp.sum(dfl, axis=-1, keepdims=True), (h, 128)), first)
        dflp = jnp.concatenate([dfl, jnp.zeros((HEAD_ROWS - h, s), F32)], axis=0).astype(BF)
        _acc_rows(dwf_ref, _dotg(dflp, xt_ref[...], NT)[0:h], first)

    blk = lambda i: (i, 0, 0)
    return _pcall(
        body, (dcum, flog, xt), name=name, grid=(b,),
        in_specs=[pl.BlockSpec((1, h, s), blk), pl.BlockSpec((1, h, s), blk), pl.BlockSpec((d, s), lambda i: (0, i))],
        out_specs=[pl.BlockSpec((1, h, s), blk), pl.BlockSpec((h, 128), lambda i: (0, 0)),
                   pl.BlockSpec((h, d), lambda i: (0, 0))],
        out_shape=[SDS((b, h, s), F32), SDS((h, 128), F32), SDS((h, d), F32)],
        sem=("arbitrary",), side=side)


def _sgu_bwd(ps, dyc, lng, lnb, ws, bs, name, side=None):
    t, ds2 = ps.shape
    ds = ds2 // 2
    ng, c, _ = ws.shape
    tm = _tile(t, 512, c)

    def body(p_ref, dy_ref, g_ref, b_ref, ws_ref, bs_ref, dp_ref, dws_ref, dbs_ref, dg_ref, db_ref, dvn_acc):
        tri, gmasks, wm, bias = _sgu_consts(ws_ref, bs_ref, ds)
        su = p_ref[:, 0:ds]
        sv = p_ref[:, ds:ds2]
        up = _gelu(su)
        gv = _gelu(sv)
        xhat, rstd = _ln_stats(gv)
        vnb = (xhat * g_ref[...] + b_ref[...]).astype(BF)
        dy = dy_ref[...].astype(F32)
        dws = [jnp.zeros((c, c), F32) for _ in range(ng)]
        dbs = [jnp.zeros((c, 1), F32) for _ in range(ng)]
        for n in range(tm // c):
            r0, r1 = n * c, (n + 1) * c
            mixed = _sgu_mix(wm, vnb[r0:r1], bias, gmasks)
            dp_ref[r0:r1, 0:ds] = (dy[r0:r1] * mixed * _gelu_grad(su[r0:r1])).astype(BF)
            dmix = dy[r0:r1] * up[r0:r1]
            dvn = jnp.zeros((c, ds), F32)
            for g in range(ng):
                dmg = jnp.where(gmasks[g], dmix, 0.0)
                dmb = dmg.astype(BF)
                dws[g] = dws[g] + _dotg(dmb, vnb[r0:r1], NT)
                dbs[g] = dbs[g] + jnp.sum(dmg, axis=-1, keepdims=True)
                dvn = dvn + _dotg(wm[g], dmb, TN)
            dvn_acc[r0:r1, :] = dvn
        dvn_all = dvn_acc[...]
        gdv = dvn_all * g_ref[...]
        m1 = jnp.mean(gdv, axis=-1, keepdims=True)
        m2 = jnp.mean(gdv * xhat, axis=-1, keepdims=True)
        dgv = rstd * (gdv - m1 - xhat * m2)
        dp_ref[:, ds:ds2] = (dgv * _gelu_grad(sv)).astype(BF)
        first = pl.program_id(0) == 0
        _acc_rows(dg_ref, jnp.sum(dvn_all * xhat, axis=0, keepdims=True), first)
        _acc_rows(db_ref, jnp.sum(dvn_all, axis=0, keepdims=True), first)
        for g in range(ng):
            _acc_rows(dws_ref.at[g], jnp.where(tri, dws[g], 0.0), first)
            _acc_rows(dbs_ref.at[g], dbs[g], first)

    row = lambda i: (i, 0)
    fixed2 = lambda i: (0, 0)
    fixed3 = lambda i: (0, 0, 0)
    return _pcall(
        body, (ps, dyc, lng, lnb, ws, bs), name=name, grid=(t // tm,),
        in_specs=[pl.BlockSpec((tm, ds2), row), pl.BlockSpec((tm, ds), row), pl.BlockSpec((1, ds), fixed2),
                  pl.BlockSpec((1, ds), fixed2), pl.BlockSpec(ws.shape, fixed3), pl.BlockSpec(bs.shape, fixed3)],
        out_specs=[pl.BlockSpec((tm, ds2), row), pl.BlockSpec(ws.shape, fixed3), pl.BlockSpec(bs.shape, fixed3),
                   pl.BlockSpec((1, ds), fixed2), pl.BlockSpec((1, ds), fixed2)],
        out_shape=[SDS((t, ds2), BF), SDS(ws.shape, F32), SDS(bs.shape, F32), SDS((1, ds), F32), SDS((1, ds), F32)],
        scratch_shapes=[pltpu.VMEM((tm, ds), F32)],
        sem=("arbitrary",), side=side)


def _mix_bwd_dx(dz, dconv, dq, dk, dv, dsgu, dflog, w_main, wft, seq, alpha, name, side=None):
    t, d = dz.shape
    groups = [dconv, dq, dk, dv, dsgu]
    offs = [0]
    for g in groups:
        offs.append(offs[-1] + g.shape[1])
    h = dflog.shape[1]
    tm = _tile(seq, 512)
    per_seq = seq // tm

    def body(dz_ref, a0, a1, a2, a3, a4, dfl_ref, w_ref, wf_ref, dx_ref):
        dflp = jnp.concatenate([dfl_ref[0], jnp.zeros((HEAD_ROWS - h, tm), F32)], axis=0).astype(BF)
        acc = alpha * dz_ref[...] + _dotg(dflp, wf_ref[...], TN)
        for k, a_ref in enumerate((a0, a1, a2, a3, a4)):
            acc = acc + _dotg(a_ref[...], w_ref[:, offs[k]:offs[k + 1]], NT)
        dx_ref[...] = acc

    row = lambda i: (i, 0)
    return _pcall(
        body, (dz, *groups, dflog, w_main, wft), name=name, grid=(t // tm,),
        in_specs=[pl.BlockSpec((tm, d), row)] + [pl.BlockSpec((tm, g.shape[1]), row) for g in groups]
        + [pl.BlockSpec((1, h, tm), lambda i: (i // per_seq, 0, i % per_seq)),
           pl.BlockSpec(w_main.shape, lambda i: (0, 0)), pl.BlockSpec(wft.shape, lambda i: (0, 0))],
        out_specs=pl.BlockSpec((tm, d), row), out_shape=SDS((t, d), F32),
        sem=("parallel",), side=side)


def _adam_math(w, g, m, v):
    c1 = 1.0 / (1.0 - ADAM_B1 ** ADAM_STEP)
    c2 = 1.0 / (1.0 - ADAM_B2 ** ADAM_STEP)
    nm = ADAM_B1 * m + (1.0 - ADAM_B1) * g
    nv = ADAM_B2 * v + (1.0 - ADAM_B2) * (g * g)
    delta = -ADAM_LR * ((nm * c1) / (jnp.sqrt(nv * c2) + ADAM_EPS) + ADAM_WD * w)
    return delta, nm, nv


def _adamw_small(w, g, m, v, name):
    r, c = w.shape
    tr = _tile(r, 512)

    def body(w_ref, g_ref, m_ref, v_ref, d_ref, nm_ref, nv_ref):
        d_ref[...], nm_ref[...], nv_ref[...] = _adam_math(w_ref[...], g_ref[...], m_ref[...], v_ref[...])

    blk = pl.BlockSpec((tr, c), lambda i: (i, 0))
    return _pcall(body, (w, g, m, v), name=name, grid=(r // tr,), in_specs=[blk] * 4, out_specs=[blk] * 3,
                  out_shape=[SDS((r, c), F32)] * 3, sem=("parallel",))


def _adamw_shard(w, m, v, tot, recv, cq, layer, prev, name, side=None):
    nl, xr, yc = w.shape
    h = xr // 2
    tr = _tile(h, 256)
    nt = h // tr

    def body(cq_ref, w_ref, m_ref, v_ref, t_ref, r_ref, *rest):
        g_ref, d_ref, nm_ref, nv_ref = rest[-4:]
        g = jnp.where(pl.program_id(0) == cq_ref[0], t_ref[...], r_ref[...])
        g_ref[0] = g
        d_ref[0], nm_ref[0], nv_ref[0] = _adam_math(w_ref[0], g, m_ref[0], v_ref[0])

    slab = pl.BlockSpec((1, tr, yc), lambda hf, i, cq_ref: (layer, hf * nt + i, 0))
    mine = pl.BlockSpec((tr, yc), lambda hf, i, cq_ref: (jnp.where(hf == cq_ref[0], i, 0), 0))
    theirs = pl.BlockSpec((tr, yc), lambda hf, i, cq_ref: (jnp.where(hf == cq_ref[0], 0, i), 0))
    operands = [w, m, v, tot, recv]
    in_specs = [slab, slab, slab, mine, theirs]
    aliases = None
    if prev is not None:
        operands += list(prev)
        in_specs += [HBM] * 4
        aliases = {6 + k: k for k in range(4)}
    return _pcall(body, operands, name=name, grid=(2, nt), prefetch=(cq,), in_specs=in_specs,
                  out_specs=[slab] * 4, out_shape=[SDS(w.shape, F32)] * 4, aliases=aliases,
                  sem=("parallel", "parallel"), side=side)


BIG = ("ffn1_w_up", "ffn1_w_down", "mix_w_in", "mix_w_out", "ffn2_w_up", "ffn2_w_down")
SMALL = ("ln1_g", "ln1_b", "fox_b_f", "sgu_ln_g", "sgu_ln_b", "sgu_w_s", "sgu_b_s", "ln2_g", "ln2_b", "ln3_g", "ln3_b")
ORDER = ("ln1_g", "ln1_b", "ffn1_w_up", "ffn1_w_down", "mix_w_in", "fox_b_f", "conv_w", "sgu_ln_g", "sgu_ln_b",
         "sgu_w_s", "sgu_b_s", "mix_w_out", "ln2_g", "ln2_b", "ffn2_w_up", "ffn2_w_down", "ln3_g", "ln3_b")


def _row(v):
    return v.reshape(1, -1)


class _Pipe:
    def __init__(self, stages):
        self.stages = list(stages)
        self.pos = 0
        self.last = None

    def kind(self):
        return self.stages[self.pos][0] if self.pos < len(self.stages) else None


class _Sched:
    def __init__(self):
        self.pipes = []
        self.n_alone = 0

    def add(self, stages):
        self.pipes.append(_Pipe(stages))

    def _take_comms(self, skip=None):
        jobs = []
        for p in self.pipes:
            if p is not skip and p.kind() == "comm":
                jobs.append((p, p.stages[p.pos][1]()))
        return jobs

    @staticmethod
    def _landed(jobs):
        for p, side in jobs:
            p.last = side.results
            p.pos += 1

    def carry(self, builder, *args, **kw):
        jobs = self._take_comms()
        res = builder(*args, side=_join([s for _, s in jobs]), **kw)
        self._landed(jobs)
        self._computes(ride=False)
        return res

    def _computes(self, ride):
        again = True
        while again:
            again = False
            for p in self.pipes:
                if p.kind() == "compute":
                    jobs = self._take_comms(skip=p) if ride else []
                    p.stages[p.pos][1](p.last, _join([s for _, s in jobs]))
                    p.pos += 1
                    self._landed(jobs)
                    again = True

    def drain(self):
        while any(p.kind() is not None for p in self.pipes):
            self._computes(ride=True)
            jobs = self._take_comms()
            if jobs:
                _run_side(_join([s for _, s in jobs]), "exchange_tail_%d" % self.n_alone)
                self.n_alone += 1
                self._landed(jobs)


def _forward_layer(x, xt, p, dims, alpha, l, ride, target=None):
    b, s = dims["b"], dims["s"]
    t, d = x.shape
    tag = "l%d_" % l

    def run(stage, builder, *args, **kw):
        side, on_done = ride.get(stage, (None, None))
        res = builder(*args, tag + stage, side=side, **kw)
        if on_done is not None:
            on_done()
        return res

    if xt is None:
        h1, a1, a1t, xt = run("ffn1_up", _ffn_up_fwd, x, p["wup1"], emit_xt=True)
    else:
        h1, a1, a1t = run("ffn1_up", _ffn_up_fwd, x, p["wup1"])
    z1, x1, x1t = run("ffn1_down", _res_ln_fwd, [a1], p["wd1"], x, p["ln1_g"], p["ln1_b"], alpha, 0.5)
    pc, pq, ps = run("mix_proj", _mix_proj_fwd, x1, p["win"], dims["proj_widths"])
    x1_3 = x1.reshape(b, s, d)
    flog, cum = run("fox_gate", _cum_fwd, x1_3, p["wft"], p["bf"])
    nh = flog.shape[1]
    cum4 = cum.reshape(b, nh // 2, 2, s)
    pc3 = pc.reshape(b, s, -1)
    pq3 = pq.reshape(b, s, -1)
    ya = run("conv", _conv_fwd, pc3, p["cw"]).reshape(t, -1)
    yb3, lse4 = run("fox", _fox_fwd, pq3, cum4)
    yb = yb3.reshape(t, -1)
    yc = run("sgu", _sgu_fwd, ps, p["sgu_g"], p["sgu_b"], p["ws"], p["bs"])
    z2, x2, x2t, yat, ybt, yct = run("mix_out", _res_ln_fwd, [ya, yb, yc], p["wout"], x1, p["ln2_g"], p["ln2_b"],
                                      alpha, 1.0, parts_t=True)
    h2, a2, a2t = run("ffn2_up", _ffn_up_fwd, x2, p["wup2"])
    z3, x3, x3t = run("ffn2_down", _res_ln_fwd, [a2], p["wd2"], x2, p["ln3_g"], p["ln3_b"], alpha, 0.5, target=target)
    saved = dict(xt=xt, h1=h1, a1t=a1t, z1=z1, x1=x1, x1t=x1t, pc3=pc3, pq3=pq3, ps=ps, flog=flog, cum4=cum4,
                 lse4=lse4, yat=yat, ybt=ybt, yct=yct, z2=z2, x2t=x2t, h2=h2, a2t=a2t, z3=z3)
    return x3, x3t, saved


def _ffn_backward(sched, emit, which, dy, z, gamma, wd, wup, h, a_t, x_in_t, alpha, tag, after_mid=None):
    dz, df, dh, dgam, dbet = sched.carry(_ffn_bwd_mid, dy, z, gamma, wd, h, tag + "_bwd_mid")
    if after_mid is not None:
        after_mid(dgam, dbet)
    nq, d, w = wup.shape
    emit(which + "_w_up", sched.carry(_dw, x_in_t, dh, d, w, tag + "_dw_up")[0])
    half = wd.shape[0] // 2
    emit(which + "_w_down", sched.carry(_dw, a_t, df[None], half, d, tag + "_dw_down").reshape(nq, -1, d))
    dx = sched.carry(_ffn_bwd_dx, dh, wup, dz, alpha, tag + "_bwd_dx")
    return dx, dgam, dbet


def _backward_layer(sched, emit, emit_small, dy, sv, p, dims, alpha, l):
    b, s = dims["b"], dims["s"]
    tag = "l%d_" % l
    t, d = dy.shape
    g = {}
    dx2, g["ln3_g"], g["ln3_b"] = _ffn_backward(sched, emit, "ffn2", dy, sv["z3"], p["ln3_g"], p["wd2"], p["wup2"],
                                                sv["h2"], sv["a2t"], sv["x2t"], alpha, tag + "ffn2")
    wa, wb, wc = sv["yat"].shape[0], sv["ybt"].shape[0], sv["yct"].shape[0]
    dz2, dz2b, dya, dyb, dyc, g["ln2_g"], g["ln2_b"] = sched.carry(
        _out_bwd, dx2, sv["z2"], p["ln2_g"], p["wout"], (wa, wb, wc), tag + "mix_out_bwd")
    emit("mix_w_out", _dw_groups([sv["yat"], sv["ybt"], sv["yct"]], [dz2b], DW_TOKENS,
                                 tag + "dw_out").reshape(N_SHARDS, -1, d))
    dpc3, g["conv_w"] = _conv_bwd(sv["pc3"], dya.reshape(b, s, -1), p["cw"], tag + "conv_bwd")
    dq3, dk3, dv3, dcum4 = sched.carry(_fox_bwd, sv["pq3"], sv["cum4"], sv["lse4"], dyb.reshape(b, s, -1),
                                       tag + "fox_bwd")
    nh = sv["flog"].shape[1]
    dflog, dbf, dwft = _cum_bwd(dcum4.reshape(b, nh, s), sv["flog"], sv["x1t"], tag + "fox_gate_bwd")
    g["fox_b_f"] = dbf[:, 0]
    dps, g["sgu_w_s"], dbs, g["sgu_ln_g"], g["sgu_ln_b"] = _sgu_bwd(
        sv["ps"], dyc, p["sgu_g"], p["sgu_b"], p["ws"], p["bs"], tag + "sgu_bwd")
    g["sgu_b_s"] = dbs[:, :, 0]
    dpc = dpc3.reshape(t, -1)
    dq, dk, dv = dq3.reshape(t, -1), dk3.reshape(t, -1), dv3.reshape(t, -1)
    main = sched.carry(_dw_groups, [sv["x1t"]], [dpc, dq, dk, dv, dps], DW_TOKENS // 2, tag + "dw_in")
    n_main = main.shape[1] - dps.shape[1]
    w_in_grad = jnp.concatenate([main[:, :n_main], dwft.T, main[:, n_main:]], axis=1)
    emit("mix_w_in", jnp.moveaxis(w_in_grad.reshape(d, N_SHARDS, -1), 1, 0))
    dx1 = sched.carry(_mix_bwd_dx, dz2, dpc, dq, dk, dv, dps, dflog, p["win"], p["wft"], s, alpha, tag + "mix_bwd_dx")

    def small_ready(dgam, dbet):
        g["ln1_g"], g["ln1_b"] = dgam, dbet
        emit_small(g)

    dx0, _, _ = _ffn_backward(sched, emit, "ffn1", dx1, sv["z1"], p["ln1_g"], p["wd1"], p["wup1"],
                              sv["h1"], sv["a1t"], sv["xt"], alpha, tag + "ffn1", after_mid=small_ready)
    return dx0


def _span(n):
    return -(-n // 1024) * 1024


def _pack_rows(flat_list):
    return jnp.concatenate([jnp.pad(v, (0, _span(v.shape[0]) - v.shape[0])) for v in flat_list]).reshape(-1, 128)


def kernel(x, ln1_g, ln1_b, ffn1_w_up, ffn1_w_down, mix_w_in, fox_b_f, conv_w, sgu_ln_g, sgu_ln_b, sgu_w_s, sgu_b_s, mix_w_out, ln2_g, ln2_b, ffn2_w_up, ffn2_w_down, ln3_g, ln3_b, loss_target, m_ln1_g, m_ln1_b, m_ffn1_w_up, m_ffn1_w_down, m_mix_w_in, m_fox_b_f, m_conv_w, m_sgu_ln_g, m_sgu_ln_b, m_sgu_w_s, m_sgu_b_s, m_mix_w_out, m_ln2_g, m_ln2_b, m_ffn2_w_up, m_ffn2_w_down, m_ln3_g, m_ln3_b, v_ln1_g, v_ln1_b, v_ffn1_w_up, v_ffn1_w_down, v_mix_w_in, v_fox_b_f, v_conv_w, v_sgu_ln_g, v_sgu_ln_b, v_sgu_w_s, v_sgu_b_s, v_mix_w_out, v_ln2_g, v_ln2_b, v_ffn2_w_up, v_ffn2_w_down, v_ln3_g, v_ln3_b):
    wts = dict(ln1_g=ln1_g, ln1_b=ln1_b, ffn1_w_up=ffn1_w_up, ffn1_w_down=ffn1_w_down, mix_w_in=mix_w_in,
               fox_b_f=fox_b_f, conv_w=conv_w, sgu_ln_g=sgu_ln_g, sgu_ln_b=sgu_ln_b, sgu_w_s=sgu_w_s,
               sgu_b_s=sgu_b_s, mix_w_out=mix_w_out, ln2_g=ln2_g, ln2_b=ln2_b, ffn2_w_up=ffn2_w_up,
               ffn2_w_down=ffn2_w_down, ln3_g=ln3_g, ln3_b=ln3_b)
    mom = dict(ln1_g=m_ln1_g, ln1_b=m_ln1_b, ffn1_w_up=m_ffn1_w_up, ffn1_w_down=m_ffn1_w_down, mix_w_in=m_mix_w_in,
               fox_b_f=m_fox_b_f, conv_w=m_conv_w, sgu_ln_g=m_sgu_ln_g, sgu_ln_b=m_sgu_ln_b, sgu_w_s=m_sgu_w_s,
               sgu_b_s=m_sgu_b_s, mix_w_out=m_mix_w_out, ln2_g=m_ln2_g, ln2_b=m_ln2_b, ffn2_w_up=m_ffn2_w_up,
               ffn2_w_down=m_ffn2_w_down, ln3_g=m_ln3_g, ln3_b=m_ln3_b)
    var = dict(ln1_g=v_ln1_g, ln1_b=v_ln1_b, ffn1_w_up=v_ffn1_w_up, ffn1_w_down=v_ffn1_w_down, mix_w_in=v_mix_w_in,
               fox_b_f=v_fox_b_f, conv_w=v_conv_w, sgu_ln_g=v_sgu_ln_g, sgu_ln_b=v_sgu_ln_b, sgu_w_s=v_sgu_w_s,
               sgu_b_s=v_sgu_b_s, mix_w_out=v_mix_w_out, ln2_g=v_ln2_g, ln2_b=v_ln2_b, ffn2_w_up=v_ffn2_w_up,
               ffn2_w_down=v_ffn2_w_down, ln3_g=v_ln3_g, ln3_b=v_ln3_b)

    nl = ln1_g.shape[0]
    b, s, d = x.shape
    t = b * s
    alpha = (2 * nl) ** 0.25
    cw_sh = conv_w.shape[2]
    d_conv = cw_sh * N_SHARDS
    d_sgu = sgu_ln_g.shape[1]
    nh = fox_b_f.shape[1]
    d_fox = nh * FOX_HEAD_DIM
    n_main = 3 * d_conv + 3 * d_fox
    dims = dict(b=b, s=s, proj_widths=(3 * d_conv, 3 * d_fox, 2 * d_sgu))
    cpos = lax.axis_index("c").astype(jnp.int32)
    qpos = (2 * lax.axis_index("x") + lax.axis_index("y")).astype(jnp.int32)
    cq = jnp.stack([cpos, qpos])

    me = (2 * qpos + cpos).reshape(1)
    assert nl == 2, "the gather schedule below names the carriers of a two-layer step"

    conv_tile = jnp.pad(conv_w, ((0, 0), (0, 8 - conv_w.shape[1]), (0, 128 - cw_sh)))
    params = [dict(bf=fox_b_f[l].reshape(nh, 1), sgu_g=_row(sgu_ln_g[l]), sgu_b=_row(sgu_ln_b[l]), ws=sgu_w_s[l],
                   bs=sgu_b_s[l][:, :, None], ln1_g=_row(ln1_g[l]), ln1_b=_row(ln1_b[l]), ln2_g=_row(ln2_g[l]),
                   ln2_b=_row(ln2_b[l]), ln3_g=_row(ln3_g[l]), ln3_b=_row(ln3_b[l])) for l in range(nl)]

    def operands_of(k, arr):
        if k == "mix_w_in":
            w_in = jnp.moveaxis(arr, 0, 1).reshape(d, -1)
            return dict(win=jnp.concatenate([w_in[:, :n_main], w_in[:, n_main + nh:]], axis=1),
                        wft=jnp.pad(w_in[:, n_main:n_main + nh].T, ((0, HEAD_ROWS - nh), (0, 0))))
        if k == "conv_w":
            return dict(cw=jnp.moveaxis(arr[:, :3, :cw_sh], 0, 1).reshape(3, d_conv))
        if k in ("ffn1_w_up", "ffn2_w_up"):
            return {"wup" + k[3]: arr}
        return {dict(ffn1_w_down="wd1", ffn2_w_down="wd2", mix_w_out="wout")[k]: arr.reshape(-1, d)}

    def gather(l, keys):
        side = _side_gather([conv_tile[l] if k == "conv_w" else wts[k][l].astype(BF) for k in keys],
                            [k != "conv_w" for k in keys])

        def install():
            for k, arr in zip(keys, side.results):
                params[l].update(operands_of(k, arr))
        return side, install

    first, install_first = gather(0, ["ffn1_w_up"])
    _run_side(first, "gather_first")
    install_first()
    rides = [{"ffn1_up": gather(0, ["ffn1_w_down", "mix_w_in", "mix_w_out", "conv_w"]),
              "ffn1_down": gather(0, ["ffn2_w_up"]),
              "mix_proj": gather(0, ["ffn2_w_down"]),
              "fox": gather(1, ["ffn1_w_up", "ffn1_w_down", "mix_w_in", "mix_w_out", "conv_w"]),
              "ffn2_up": gather(1, ["ffn2_w_up", "ffn2_w_down"])}, {}]

    act, act_t = x.reshape(t, d), None
    saved = []
    for l in range(nl):
        act, act_t, sv = _forward_layer(act, act_t, params[l], dims, alpha, l, rides[l],
                                        target=loss_target.reshape(t, d) if l == nl - 1 else None)
        saved.append(sv)
    dy, loss_blk = act, act_t

    sched = _Sched()
    prev = {k: None for k in BIG}
    red = {}

    def emit_for(l):
        def emit(key, g):
            st = {}
            name = "l%d_%s" % (l, key)

            def pair_sum(res, side):
                st["p"] = _pair_sum(g, res[0], cq, "rs_pair_sum_" + name, side=side)

            def chip_sum(res, side):
                st["t"] = _chip_sum(st["p"], res[0], cq, "rs_chip_sum_" + name, side=side)

            def adamw(res, side):
                prev[key] = _adamw_shard(wts[key], mom[key], var[key], st["t"], res[0], cq, l, prev[key],
                                         "adamw_" + name, side=side)

            sched.add([("comm", lambda: _side_pair_send([g])), ("compute", pair_sum),
                       ("comm", lambda: _side_scatter([st["p"]])), ("compute", chip_sum),
                       ("comm", lambda: _side_pair_share([st["t"]])), ("compute", adamw)])
        return emit

    def emit_small_for(l):
        def emit_small(g):
            flat = [g[k].reshape(-1) for k in SMALL] + [g["conv_w"].reshape(-1)]
            if l == nl - 1:
                flat.append(loss_blk[0, 0:1])
            vec = _pack_rows(flat)

            def slot_sum(res, side):
                red[l] = _sum_slots(vec, res[0], me, "small_sum_l%d" % l, side=side)

            sched.add([("comm", lambda: _side_bcast(vec)), ("compute", slot_sum)])
        return emit_small

    for l in reversed(range(nl)):
        dy = _backward_layer(sched, emit_for(l), emit_small_for(l), dy, saved[l], params[l], dims, alpha, l)
    sched.drain()
    grad_x = dy.reshape(b, s, d)
    gfin, delta, new_m, new_v = {}, {}, {}, {}
    for k in BIG:
        gfin[k], delta[k], new_m[k], new_v[k] = prev[k]

    gsm = {k: [] for k in SMALL + ("conv_w",)}
    for l in range(nl):
        flat_l = red[l].reshape(-1)
        off = 0
        for k in SMALL:
            n = wts[k][l].size
            gsm[k].append(flat_l[off:off + n].reshape(wts[k][l].shape))
            off += _span(n)
        n = 3 * d_conv
        gsm["conv_w"].append(lax.dynamic_slice_in_dim(flat_l[off:off + n].reshape(3, d_conv), qpos * cw_sh, cw_sh,
                                                      axis=1))
        off += _span(n)
        if l == nl - 1:
            loss = flat_l[off]
    for k in gsm:
        gfin[k] = jnp.stack(gsm[k])
    small_keys = SMALL + ("conv_w",)
    sizes = [wts[k].size for k in small_keys]
    pk = lambda src: _pack_rows([src[k].reshape(-1) for k in small_keys])
    dl, nm, nv = _adamw_small(pk(wts), pk(gfin), pk(mom), pk(var), "adamw_small")
    off = 0
    for k, n in zip(small_keys, sizes):
        shp = wts[k].shape
        delta[k] = dl.reshape(-1)[off:off + n].reshape(shp)
        new_m[k] = nm.reshape(-1)[off:off + n].reshape(shp)
        new_v[k] = nv.reshape(-1)[off:off + n].reshape(shp)
        off += _span(n)

    return (loss, grad_x, *[gfin[k] for k in ORDER], *[delta[k] for k in ORDER],
            *[new_m[k] for k in ORDER], *[new_v[k] for k in ORDER])
```

```python
import functools

import jax
import jax.numpy as jnp
from jax import lax
from jax.experimental import pallas as pl
from jax.experimental.pallas import tpu as pltpu

F32 = jnp.float32
BF = jnp.bfloat16
SDS = jax.ShapeDtypeStruct
MESH = pl.DeviceIdType.MESH

LN_EPS = 1e-5
FOX_HEAD_DIM = 64
FOX_Q_BLOCK = 512
DW_TOKENS = 2048
HEAD_ROWS = 128
GELU_K = 0.7978845608028654
GELU_C = 0.044715
NEG_BIG = -1e30
N_SHARDS = 4

ADAM_LR = 0.001
ADAM_B1 = 0.9
ADAM_B2 = 0.999
ADAM_EPS = 1e-08
ADAM_WD = 0.01
ADAM_STEP = 10

VMEM_LIMIT_BYTES = 56 * 1024 * 1024
NT = (((1,), (1,)), ((), ()))
TN = (((0,), (0,)), ((), ()))
HBM = pl.BlockSpec(memory_space=pl.ANY)


def _tile(n, pref, mult=8):
    t = min(n, pref)
    while n % t or t % mult:
        t -= mult
    return t


def _dot(a, b):
    return jnp.dot(a, b, preferred_element_type=F32)


def _dotg(a, b, dims):
    return lax.dot_general(a, b, dims, preferred_element_type=F32)


def _sigmoid(x):
    return 1.0 / (1.0 + jnp.exp(-x))


def _gelu(x):
    return 0.5 * x * (1.0 + jnp.tanh(GELU_K * (x + GELU_C * x * x * x)))


def _gelu_grad(x):
    t = jnp.tanh(GELU_K * (x + GELU_C * x * x * x))
    return 0.5 * (1.0 + t) + 0.5 * x * (1.0 - t * t) * GELU_K * (1.0 + 3.0 * GELU_C * x * x)


def _ln_stats(z):
    mu = jnp.mean(z, axis=-1, keepdims=True)
    zc = z - mu
    var = jnp.mean(zc * zc, axis=-1, keepdims=True)
    rstd = lax.rsqrt(var + LN_EPS)
    return zc * rstd, rstd


def _ln_bwd(dy, z, g):
    xhat, rstd = _ln_stats(z)
    gdy = dy * g
    m1 = jnp.mean(gdy, axis=-1, keepdims=True)
    m2 = jnp.mean(gdy * xhat, axis=-1, keepdims=True)
    dz = rstd * (gdy - m1 - xhat * m2)
    return dz, jnp.sum(dy * xhat, axis=0, keepdims=True), jnp.sum(dy, axis=0, keepdims=True)


class _Side:
    def __init__(self, ins, out_shapes, sems, start, finish, relay=None):
        self.ins, self.out_shapes, self.sems = list(ins), list(out_shapes), list(sems)
        self.start, self.finish = start, finish
        self.relay = relay
        self.results = None


def _join(sides):
    sides = [s for s in sides if s is not None]
    if not sides:
        return None
    ins = [a for s in sides for a in s.ins]
    outs = [a for s in sides for a in s.out_shapes]
    sems = [a for s in sides for a in s.sems]

    def parts(seq, field):
        out, o = [], 0
        for s in sides:
            n = len(getattr(s, field))
            out.append(seq[o:o + n])
            o += n
        return out

    def run(which):
        def fn(i, o, m):
            for s, a, b, c in zip(sides, parts(i, "ins"), parts(o, "out_shapes"), parts(m, "sems")):
                if getattr(s, which) is not None:
                    getattr(s, which)(a, b, c)
        return fn

    joined = _Side(ins, outs, sems, run("start"), run("finish"),
                   run("relay") if any(s.relay is not None for s in sides) else None)
    joined.members = sides
    return joined


def _deliver(side, results):
    members = getattr(side, "members", None)
    side.results = list(results)
    if members:
        o = 0
        for s in members:
            n = len(s.out_shapes)
            _deliver(s, results[o:o + n])
            o += n


def _pcall(body, operands, *, name, grid, in_specs, out_specs, out_shape, sem, scratch_shapes=(),
           prefetch=(), aliases=None, side=None):
    single = not isinstance(out_shape, (list, tuple))
    out_shape = [out_shape] if single else list(out_shape)
    out_specs = [out_specs] if single else list(out_specs)
    in_specs, scratch_shapes = list(in_specs), list(scratch_shapes)
    n_pre, n_in, n_out, n_sc = len(prefetch), len(in_specs), len(out_shape), len(scratch_shapes)
    fn = body
    extra = []
    if side is not None:
        s_in, s_out = len(side.ins), len(side.out_shapes)

        def fn(*refs):
            pre, rest = refs[:n_pre], refs[n_pre:]
            m_in, c_in = rest[:n_in], rest[n_in:n_in + s_in]
            rest = rest[n_in + s_in:]
            m_out, c_out = rest[:n_out], rest[n_out:n_out + s_out]
            rest = rest[n_out + s_out:]
            m_sc, c_sc = rest[:n_sc], rest[n_sc:]
            first = pl.program_id(0) == 0
            last = pl.program_id(0) == grid[0] - 1
            for a in range(1, len(grid)):
                first = jnp.logical_and(first, pl.program_id(a) == 0)
                last = jnp.logical_and(last, pl.program_id(a) == grid[a] - 1)

            @pl.when(first)
            def _():
                side.start(c_in, c_out, c_sc)

            if side.relay is not None:
                step, total = pl.program_id(0), grid[0]
                for a in range(1, len(grid)):
                    step, total = step * grid[a] + pl.program_id(a), total * grid[a]

                @pl.when(step == (3 * total) // 4)
                def _():
                    side.relay(c_in, c_out, c_sc)

            body(*pre, *m_in, *m_out, *m_sc)

            @pl.when(last)
            def _():
                side.finish(c_in, c_out, c_sc)

        in_specs = in_specs + [HBM] * s_in
        out_specs = out_specs + [HBM] * s_out
        out_shape = out_shape + side.out_shapes
        scratch_shapes = scratch_shapes + side.sems
        extra = side.ins
        sem = ("arbitrary",) * len(grid)
    params = pltpu.CompilerParams(dimension_semantics=tuple(sem), vmem_limit_bytes=VMEM_LIMIT_BYTES)
    kw = dict(input_output_aliases=aliases) if aliases else {}
    if n_pre:
        spec = pltpu.PrefetchScalarGridSpec(num_scalar_prefetch=n_pre, grid=grid, in_specs=in_specs,
                                            out_specs=out_specs, scratch_shapes=scratch_shapes)
        call = pl.pallas_call(fn, name=name, grid_spec=spec, out_shape=out_shape, compiler_params=params, **kw)
    else:
        call = pl.pallas_call(fn, name=name, grid=grid, in_specs=in_specs, out_specs=out_specs,
                              out_shape=out_shape, scratch_shapes=scratch_shapes, compiler_params=params, **kw)
    res = call(*prefetch, *operands, *extra)
    if side is not None:
        _deliver(side, res[n_out:])
        res = res[:n_out]
    return res[0] if single else res


def _run_side(side, name):
    def body(*refs):
        n_in, n_out = len(side.ins), len(side.out_shapes)
        i, o, m = refs[:n_in], refs[n_in:n_in + n_out], refs[n_in + n_out:]
        side.start(i, o, m)
        if side.relay is not None:
            side.relay(i, o, m)
        side.finish(i, o, m)

    res = pl.pallas_call(body, name=name, in_specs=[HBM] * len(side.ins), out_specs=[HBM] * len(side.out_shapes),
                         out_shape=side.out_shapes, scratch_shapes=side.sems)(*side.ins)
    _deliver(side, res)


def _mesh_pos():
    x, y, c = lax.axis_index("x"), lax.axis_index("y"), lax.axis_index("c")
    chips = [(1 - x, y), (x, 1 - y), (1 - x, 1 - y)]
    return x, y, c, chips


def _rows(ref, lead, half, n_rows):
    return ref.at[tuple(lead) + (pl.ds(half * n_rows, n_rows),)]


def _side_gather(shards, split):
    n = len(shards)
    hs = [w.shape[0] // 2 for w in shards]

    def plan(ins, outs, sems):
        ssem, rsem = sems
        x, y, c, chips = _mesh_pos()
        q = 2 * x + y
        sib = (x, y, 1 - c)

        def rc(p, k, src, dst, to):
            return pltpu.make_async_remote_copy(src_ref=src, dst_ref=dst, send_sem=ssem.at[p, k],
                                                recv_sem=rsem.at[p, k], device_id=to, device_id_type=MESH)

        def blk(ref, p, qi, half):
            return _rows(ref, (qi,), half, hs[p]) if split[p] else ref.at[qi]

        return x, y, c, chips, q, sib, rc, blk

    def first_sends(ins, outs, sems):
        x, y, c, chips, q, sib, rc, blk = plan(ins, outs, sems)
        cps = [rc(p, 0, ins[p], outs[p].at[q], sib) for p in range(n)]
        for j, (cx, cy) in enumerate(chips):
            for p in range(n):
                src = _rows(ins[p], (), c, hs[p]) if split[p] else ins[p]
                cps.append(rc(p, 1 + j, src, blk(outs[p], p, q, c), (cx, cy, c)))
        return cps

    def start(ins, outs, sems):
        for cp in first_sends(ins, outs, sems):
            cp.start()

    def forwards(ins, outs, sems):
        x, y, c, chips, q, sib, rc, blk = plan(ins, outs, sems)
        return [rc(p, 4 + j, blk(outs[p], p, 2 * cx + cy, c), blk(outs[p], p, 2 * cx + cy, c), sib)
                for j, (cx, cy) in enumerate(chips) for p in range(n) if split[p]]

    def relay(ins, outs, sems):
        x, y, c, chips, q, sib, rc, blk = plan(ins, outs, sems)
        for j, (cx, cy) in enumerate(chips):
            for p in range(n):
                got = blk(outs[p], p, 2 * cx + cy, c)
                rc(p, 1 + j, got, got, (cx, cy, c)).wait_recv()
        for cp in forwards(ins, outs, sems):
            cp.start()

    def finish(ins, outs, sems):
        x, y, c, chips, q, sib, rc, blk = plan(ins, outs, sems)
        sent = first_sends(ins, outs, sems) + forwards(ins, outs, sems)
        for j, (cx, cy) in enumerate(chips):
            qj = 2 * cx + cy
            for p in range(n):
                if split[p]:
                    got = blk(outs[p], p, qj, 1 - c)
                    rc(p, 4 + j, got, got, sib).wait_recv()
        for p in range(n):
            rc(p, 0, outs[p].at[q], outs[p].at[q], sib).wait_recv()
        for cp in sent:
            cp.wait_send()

    return _Side(shards, [SDS((N_SHARDS,) + w.shape, w.dtype) for w in shards],
                 [pltpu.SemaphoreType.DMA((n, 7)), pltpu.SemaphoreType.DMA((n, 7))], start, finish, relay)


def _side_pair_send(gs):
    n = len(gs)

    def copies(ins, outs, sems):
        x, y, c, _ = _mesh_pos()
        return [pltpu.make_async_remote_copy(
            src_ref=ins[p].at[:, pl.ds((1 - c) * (gs[p].shape[1] // 2), gs[p].shape[1] // 2)], dst_ref=outs[p],
            send_sem=sems[0].at[p], recv_sem=sems[1].at[p], device_id=(x, y, 1 - c), device_id_type=MESH)
            for p in range(n)]

    def start(ins, outs, sems):
        for cp in copies(ins, outs, sems):
            cp.start()

    def finish(ins, outs, sems):
        for cp in copies(ins, outs, sems):
            cp.wait()

    return _Side(gs, [SDS((g.shape[0], g.shape[1] // 2, g.shape[2]), g.dtype) for g in gs],
                 [pltpu.SemaphoreType.DMA((n,)), pltpu.SemaphoreType.DMA((n,))], start, finish)


def _side_scatter(ps):
    n = len(ps)

    def sends(ins, outs, sems):
        x, y, c, chips = _mesh_pos()
        q = 2 * x + y
        return [pltpu.make_async_remote_copy(src_ref=ins[p].at[2 * cx + cy], dst_ref=outs[p].at[q],
                                             send_sem=sems[0].at[p, j], recv_sem=sems[1].at[p, j],
                                             device_id=(cx, cy, c), device_id_type=MESH)
                for j, (cx, cy) in enumerate(chips) for p in range(n)]

    def start(ins, outs, sems):
        for cp in sends(ins, outs, sems):
            cp.start()

    def finish(ins, outs, sems):
        x, y, c, chips = _mesh_pos()
        for j, (cx, cy) in enumerate(chips):
            for p in range(n):
                got = outs[p].at[2 * cx + cy]
                pltpu.make_async_remote_copy(src_ref=got, dst_ref=got, send_sem=sems[0].at[p, j],
                                             recv_sem=sems[1].at[p, j], device_id=(cx, cy, c),
                                             device_id_type=MESH).wait_recv()
        for cp in sends(ins, outs, sems):
            cp.wait_send()

    return _Side(ps, [SDS(p.shape, p.dtype) for p in ps],
                 [pltpu.SemaphoreType.DMA((n, 3)), pltpu.SemaphoreType.DMA((n, 3))], start, finish)


def _side_pair_share(tots):
    n = len(tots)

    def copies(ins, outs, sems):
        x, y, c, _ = _mesh_pos()
        return [pltpu.make_async_remote_copy(src_ref=ins[p], dst_ref=outs[p], send_sem=sems[0].at[p],
                                             recv_sem=sems[1].at[p], device_id=(x, y, 1 - c), device_id_type=MESH)
                for p in range(n)]

    def start(ins, outs, sems):
        for cp in copies(ins, outs, sems):
            cp.start()

    def finish(ins, outs, sems):
        for cp in copies(ins, outs, sems):
            cp.wait()

    return _Side(tots, [SDS(t_.shape, t_.dtype) for t_ in tots],
                 [pltpu.SemaphoreType.DMA((n,)), pltpu.SemaphoreType.DMA((n,))], start, finish)


N_DEVICES = 8


def _side_bcast(v):
    def peers():
        x, y, c, _ = _mesh_pos()
        out = []
        for k in range(1, N_DEVICES):
            px, py, pc = x ^ ((k >> 2) & 1), y ^ ((k >> 1) & 1), c ^ (k & 1)
            out.append((k - 1, (px, py, pc), 4 * px + 2 * py + pc))
        return 4 * x + 2 * y + c, out

    def sends(ins, outs, sems):
        me, ps = peers()
        return [pltpu.make_async_remote_copy(src_ref=ins[0], dst_ref=outs[0].at[me], send_sem=sems[0].at[k],
                                             recv_sem=sems[1].at[k], device_id=to, device_id_type=MESH)
                for k, to, _ in ps]

    def start(ins, outs, sems):
        for cp in sends(ins, outs, sems):
            cp.start()

    def finish(ins, outs, sems):
        _, ps = peers()
        for k, to, slot in ps:
            got = outs[0].at[slot]
            pltpu.make_async_remote_copy(src_ref=got, dst_ref=got, send_sem=sems[0].at[k], recv_sem=sems[1].at[k],
                                         device_id=to, device_id_type=MESH).wait_recv()
        for cp in sends(ins, outs, sems):
            cp.wait_send()

    return _Side([v], [SDS((N_DEVICES,) + v.shape, v.dtype)],
                 [pltpu.SemaphoreType.DMA((N_DEVICES - 1,)), pltpu.SemaphoreType.DMA((N_DEVICES - 1,))], start, finish)


def _sum_slots(v, r, me, name, side=None):
    n, rows, lanes = r.shape
    tr = _tile(rows, 512)

    def body(me_ref, v_ref, r_ref, o_ref):
        j = pl.program_id(1)
        term = jnp.where(j == me_ref[0], v_ref[...], r_ref[0])

        @pl.when(j == 0)
        def _():
            o_ref[...] = term

        @pl.when(j != 0)
        def _():
            o_ref[...] += term

    other = lambda j, k: jnp.where(j == k, (k + 1) % n, j)
    return _pcall(
        body, (v, r), name=name, grid=(rows // tr, n), prefetch=(me,),
        in_specs=[pl.BlockSpec((tr, lanes), lambda i, j, me_ref: (i, 0)),
                  pl.BlockSpec((1, tr, lanes), lambda i, j, me_ref: (other(j, me_ref[0]), i, 0))],
        out_specs=pl.BlockSpec((tr, lanes), lambda i, j, me_ref: (i, 0)),
        out_shape=SDS((rows, lanes), F32), sem=("parallel", "arbitrary"), side=side)


def _pair_sum(g, r1, cq, name, side=None):
    nq, xr, yc = g.shape
    h = xr // 2
    tr = _tile(h, 512, 16)
    nt = h // tr

    def body(cq_ref, g_ref, r_ref, o_ref):
        o_ref[...] = (g_ref[...] + r_ref[...]).astype(BF)

    return _pcall(
        body, (g, r1), name=name, grid=(nq, nt), prefetch=(cq,),
        in_specs=[pl.BlockSpec((1, tr, yc), lambda j, i, cq_ref: (j, cq_ref[0] * nt + i, 0)),
                  pl.BlockSpec((1, tr, yc), lambda j, i, cq_ref: (j, i, 0))],
        out_specs=pl.BlockSpec((1, tr, yc), lambda j, i, cq_ref: (j, i, 0)),
        out_shape=SDS((nq, h, yc), BF), sem=("parallel", "parallel"), side=side)


def _chip_sum(p, r2, cq, name, side=None):
    nq, h, yc = r2.shape
    tr = _tile(h, 512, 16)

    def body(cq_ref, p_ref, r_ref, o_ref):
        j = pl.program_id(1)
        term = jnp.where(j == cq_ref[1], p_ref[0], r_ref[0]).astype(F32)

        @pl.when(j == 0)
        def _():
            o_ref[...] = term

        @pl.when(j != 0)
        def _():
            o_ref[...] += term

    other = lambda j, q: jnp.where(j == q, (q + 1) % nq, j)
    return _pcall(
        body, (p, r2), name=name, grid=(h // tr, nq), prefetch=(cq,),
        in_specs=[pl.BlockSpec((1, tr, yc), lambda i, j, cq_ref: (cq_ref[1], i, 0)),
                  pl.BlockSpec((1, tr, yc), lambda i, j, cq_ref: (other(j, cq_ref[1]), i, 0))],
        out_specs=pl.BlockSpec((tr, yc), lambda i, j, cq_ref: (i, 0)),
        out_shape=SDS((h, yc), F32), sem=("parallel", "arbitrary"), side=side)


def _ffn_up_fwd(x, wup, name, side=None, emit_xt=False):
    t, d = x.shape
    w = wup.shape[2]
    tm = _tile(t, 512)

    def body(x_ref, wg_ref, wu_ref, h_ref, a_ref, at_ref, *xt_ref):
        xb = x_ref[...].astype(BF)
        g = _dot(xb, wg_ref[0])
        u = _dot(xb, wu_ref[0])
        h_ref[0] = g.astype(BF)
        h_ref[1] = u.astype(BF)
        ab = (g * _sigmoid(g) * u).astype(BF)
        a_ref[...] = ab
        at_ref[...] = ab.T
        if emit_xt:
            @pl.when(pl.program_id(0) == 0)
            def _():
                xt_ref[0][...] = xb.T

    nt = t // tm
    out_specs = [pl.BlockSpec((2, tm, w), lambda j, i: (0, i, j)), pl.BlockSpec((tm, w), lambda j, i: (i, j)),
                 pl.BlockSpec((w, tm), lambda j, i: (j, i))]
    out_shape = [SDS((2, t, 2 * w), BF), SDS((t, 2 * w), BF), SDS((2 * w, t), BF)]
    if emit_xt:
        out_specs.append(pl.BlockSpec((d, tm), lambda j, i: (0, jnp.where(j == 0, i, nt - 1))))
        out_shape.append(SDS((d, t), BF))
    return _pcall(
        body, (x, wup, wup), name=name, grid=(2, nt),
        in_specs=[pl.BlockSpec((tm, d), lambda j, i: (i, 0)),
                  pl.BlockSpec((1, d, w), lambda j, i: (j, 0, 0)),
                  pl.BlockSpec((1, d, w), lambda j, i: (j + 2, 0, 0))],
        out_specs=out_specs, out_shape=out_shape,
        sem=("arbitrary", "arbitrary") if emit_xt else ("parallel", "parallel"), side=side)


def _res_ln_fwd(parts, w, x, gamma, beta, alpha, res_scale, name, side=None, parts_t=False, target=None):
    t, d = x.shape
    n = len(parts)
    offs = [0]
    for p in parts:
        offs.append(offs[-1] + p.shape[1])
    tm = _tile(t, 512)
    n_in = n + 4 + (target is not None)

    def body(*refs):
        p_refs = refs[:n]
        w_ref, x_ref, g_ref, b_ref = refs[n:n + 4]
        out = refs[n_in:]
        f = _dot(p_refs[0][...], w_ref[offs[0]:offs[1], :])
        for k in range(1, n):
            f = f + _dot(p_refs[k][...], w_ref[offs[k]:offs[k + 1], :])
        z = alpha * x_ref[...] + res_scale * f
        out[0][...] = z
        xhat, _ = _ln_stats(z)
        y = xhat * g_ref[...] + b_ref[...]
        if target is not None:
            err = y - refs[n + 4][...]
            out[1][...] = err * (1.0 / d)
            part = 0.5 * jnp.sum(jnp.sum(err * err, axis=-1, keepdims=True) * (1.0 / d), axis=0, keepdims=True)
            _acc_rows(out[2], jnp.broadcast_to(part, (8, 128)), pl.program_id(0) == 0)
            return
        out[1][...] = y
        out[2][...] = y.astype(BF).T
        for k in range(len(out) - 3):
            out[3 + k][...] = p_refs[k][...].T

    row = lambda i: (i, 0)
    col = lambda i: (0, i)
    fixed = lambda i: (0, 0)
    operands = [*parts, w, x, gamma, beta]
    in_specs = [pl.BlockSpec((tm, p.shape[1]), row) for p in parts] + [
        pl.BlockSpec(w.shape, fixed), pl.BlockSpec((tm, d), row), pl.BlockSpec((1, d), fixed), pl.BlockSpec((1, d), fixed)]
    if target is not None:
        operands.append(target)
        in_specs.append(pl.BlockSpec((tm, d), row))
        out_specs = [pl.BlockSpec((tm, d), row), pl.BlockSpec((tm, d), row), pl.BlockSpec((8, 128), fixed)]
        out_shape = [SDS((t, d), F32), SDS((t, d), F32), SDS((8, 128), F32)]
    else:
        out_specs = [pl.BlockSpec((tm, d), row), pl.BlockSpec((tm, d), row), pl.BlockSpec((d, tm), col)]
        out_shape = [SDS((t, d), F32), SDS((t, d), F32), SDS((d, t), BF)]
        if parts_t:
            out_specs += [pl.BlockSpec((p.shape[1], tm), col) for p in parts]
            out_shape += [SDS((p.shape[1], t), BF) for p in parts]
    return _pcall(
        body, operands, name=name, grid=(t // tm,), in_specs=in_specs, out_specs=out_specs, out_shape=out_shape,
        sem=("arbitrary",) if target is not None else ("parallel",), side=side)


def _mix_proj_fwd(x, w_main, widths, name, side=None):
    t, d = x.shape
    dc, dq, ds = widths
    tm = _tile(t, 512)

    def body(x_ref, w_ref, pc_ref, pq_ref, ps_ref):
        xb = x_ref[...].astype(BF)
        pc_ref[...] = _dot(xb, w_ref[:, 0:dc])
        pq_ref[...] = _dot(xb, w_ref[:, dc:dc + dq]).astype(BF)
        ps_ref[...] = _dot(xb, w_ref[:, dc + dq:dc + dq + ds])

    row = lambda i: (i, 0)
    return _pcall(
        body, (x, w_main), name=name, grid=(t // tm,),
        in_specs=[pl.BlockSpec((tm, d), row), pl.BlockSpec(w_main.shape, lambda i: (0, 0))],
        out_specs=[pl.BlockSpec((tm, dc), row), pl.BlockSpec((tm, dq), row), pl.BlockSpec((tm, ds), row)],
        out_shape=[SDS((t, dc), F32), SDS((t, dq), BF), SDS((t, ds), F32)],
        sem=("parallel",), side=side)


def _prefix_sum_lanes(v, reverse):
    n = v.shape[-1]
    lane = lax.broadcasted_iota(jnp.int32, v.shape, v.ndim - 1)
    sh = 1
    while sh < n:
        if reverse:
            v = v + jnp.where(lane < n - sh, pltpu.roll(v, n - sh, axis=v.ndim - 1), 0.0)
        else:
            v = v + jnp.where(lane >= sh, pltpu.roll(v, sh, axis=v.ndim - 1), 0.0)
        sh *= 2
    return v


def _cum_fwd(x3, wft, bf, name, side=None):
    b, s, d = x3.shape
    h = bf.shape[0]

    def body(x_ref, w_ref, b_ref, fl_ref, cum_ref):
        fl = _dotg(w_ref[...], x_ref[0].astype(BF), NT)[0:h] + b_ref[...]
        fl_ref[0] = fl
        lf = jnp.minimum(fl, 0.0) - jnp.log(1.0 + jnp.exp(-jnp.abs(fl)))
        cum_ref[0] = _prefix_sum_lanes(lf, reverse=False)

    return _pcall(
        body, (x3, wft, bf), name=name, grid=(b,),
        in_specs=[pl.BlockSpec((1, s, d), lambda i: (i, 0, 0)),
                  pl.BlockSpec(wft.shape, lambda i: (0, 0)), pl.BlockSpec((h, 1), lambda i: (0, 0))],
        out_specs=[pl.BlockSpec((1, h, s), lambda i: (i, 0, 0)), pl.BlockSpec((1, h, s), lambda i: (i, 0, 0))],
        out_shape=[SDS((b, h, s), F32), SDS((b, h, s), F32)],
        sem=("parallel",), side=side)


def _shift_rows(z, k, down):
    n = z.shape[0]
    row = lax.broadcasted_iota(jnp.int32, z.shape, 0)
    if down:
        return jnp.where(row >= k, pltpu.roll(z, k, axis=0), 0.0)
    return jnp.where(row < n - k, pltpu.roll(z, n - k, axis=0), 0.0)


def _conv_fwd(pc3, cw, name, side=None):
    b, s, c3 = pc3.shape
    c = c3 // 3

    def body(p_ref, w_ref, y_ref):
        z = p_ref[0, :, c:2 * c] * p_ref[0, :, 2 * c:3 * c]
        conv = w_ref[0:1, :] * _shift_rows(z, 2, True) + w_ref[1:2, :] * _shift_rows(z, 1, True) + w_ref[2:3, :] * z
        y_ref[0] = (p_ref[0, :, 0:c] * conv).astype(BF)

    return _pcall(
        body, (pc3, cw), name=name, grid=(b,),
        in_specs=[pl.BlockSpec((1, s, c3), lambda i: (i, 0, 0)), pl.BlockSpec((3, c), lambda i: (0, 0))],
        out_specs=pl.BlockSpec((1, s, c), lambda i: (i, 0, 0)),
        out_shape=SDS((b, s, c), BF), sem=("parallel",), side=side)


def _head_masks(width):
    lane = lax.broadcasted_iota(jnp.int32, (1, width), 1)
    return [lane < FOX_HEAD_DIM, lane >= FOX_HEAD_DIM]


def _fox_scores(q, k, cum_row, lo, head_mask):
    tq = q.shape[0]
    qm = jnp.where(head_mask, q * (FOX_HEAD_DIM ** -0.5), 0)
    s = _dotg(qm, k, NT) - cum_row
    tri = lax.broadcasted_iota(jnp.int32, (tq, tq), 1) <= lax.broadcasted_iota(jnp.int32, (tq, tq), 0)
    parts = [(jnp.where(tri, s[:, lo:], NEG_BIG), lo, lo + tq)]
    if lo:
        parts.insert(0, (s[:, :lo], 0, lo))
    return parts, qm


def _fox_fwd(pq3, cum4, name, side=None):
    b, s, d3 = pq3.shape
    df = d3 // 3
    hp = df // 128
    tq = _tile(s, FOX_Q_BLOCK)

    def body(q_ref, k_ref, v_ref, c_ref, o_ref, lse_ref):
        masks = _head_masks(128)
        for i in range(s // tq):
            lo, hi = i * tq, (i + 1) * tq
            q = q_ref[0, lo:hi, :]
            k = k_ref[0, 0:hi, :]
            v = v_ref[0, 0:hi, :]
            o = jnp.zeros((tq, 128), F32)
            lse = jnp.zeros((tq, 128), F32)
            for e in range(2):
                parts, _ = _fox_scores(q, k, c_ref[0, 0, e:e + 1, 0:hi], lo, masks[e])
                m = functools.reduce(jnp.maximum, [jnp.max(sc, axis=-1, keepdims=True) for sc, _, _ in parts])
                l, pv = 0.0, 0.0
                for sc, c0, c1 in parts:
                    p = jnp.exp(sc - m)
                    l = l + jnp.sum(p, axis=-1, keepdims=True)
                    pv = pv + _dot(p.astype(BF), v[c0:c1])
                o = jnp.where(masks[e], pv * (1.0 / l), o)
                lse = jnp.where(masks[e], m + jnp.log(l), lse)
            o_ref[0, lo:hi, :] = o.astype(BF)
            lse_ref[0, 0, lo:hi, :] = lse

    blk = lambda off: pl.BlockSpec((1, s, 128), lambda i, j: (i, 0, off + j))
    return _pcall(
        body, (pq3, pq3, pq3, cum4), name=name, grid=(b, hp),
        in_specs=[blk(0), blk(hp), blk(2 * hp), pl.BlockSpec((1, 1, 2, s), lambda i, j: (i, j, 0, 0))],
        out_specs=[blk(0), pl.BlockSpec((1, 1, s, 128), lambda i, j: (i, j, 0, 0))],
        out_shape=[SDS((b, s, df), BF), SDS((b, hp, s, 128), F32)],
        sem=("parallel", "parallel"), side=side)


def _sgu_mix(wm, vnb, bias, gmasks):
    out = bias
    for g in range(len(wm)):
        out = out + jnp.where(gmasks[g], _dot(wm[g], vnb), 0.0)
    return out


def _sgu_consts(ws_ref, bs_ref, ds):
    ng, c, _ = ws_ref.shape
    gd = ds // ng
    tri = lax.broadcasted_iota(jnp.int32, (c, c), 0) >= lax.broadcasted_iota(jnp.int32, (c, c), 1)
    lane = lax.broadcasted_iota(jnp.int32, (1, ds), 1)
    gmasks = [(lane >= g * gd) & (lane < (g + 1) * gd) for g in range(ng)]
    wm = [jnp.where(tri, ws_ref[g], 0.0).astype(BF) for g in range(ng)]
    bias = jnp.zeros((c, ds), F32)
    for g in range(ng):
        bias = jnp.where(gmasks[g], bs_ref[g], bias)
    return tri, gmasks, wm, bias


def _sgu_fwd(ps, lng, lnb, ws, bs, name, side=None):
    t, ds2 = ps.shape
    ds = ds2 // 2
    c = ws.shape[1]
    tm = _tile(t, 512, c)

    def body(p_ref, g_ref, b_ref, ws_ref, bs_ref, y_ref):
        _, gmasks, wm, bias = _sgu_consts(ws_ref, bs_ref, ds)
        up = _gelu(p_ref[:, 0:ds])
        xhat, _ = _ln_stats(_gelu(p_ref[:, ds:ds2]))
        vnb = (xhat * g_ref[...] + b_ref[...]).astype(BF)
        for n in range(tm // c):
            r0, r1 = n * c, (n + 1) * c
            y_ref[r0:r1, :] = (up[r0:r1] * _sgu_mix(wm, vnb[r0:r1], bias, gmasks)).astype(BF)

    fixed2 = lambda i: (0, 0)
    fixed3 = lambda i: (0, 0, 0)
    return _pcall(
        body, (ps, lng, lnb, ws, bs), name=name, grid=(t // tm,),
        in_specs=[pl.BlockSpec((tm, ds2), lambda i: (i, 0)), pl.BlockSpec((1, ds), fixed2),
                  pl.BlockSpec((1, ds), fixed2), pl.BlockSpec(ws.shape, fixed3), pl.BlockSpec(bs.shape, fixed3)],
        out_specs=pl.BlockSpec((tm, ds), lambda i: (i, 0)),
        out_shape=SDS((t, ds), BF), sem=("parallel",), side=side)


def _acc_rows(ref, val, first):
    @pl.when(first)
    def _():
        ref[...] = val

    @pl.when(jnp.logical_not(first))
    def _():
        ref[...] += val


def _ffn_bwd_mid(dy, z, gamma, wd, h, name, side=None):
    t, d = dy.shape
    dff = wd.shape[0]
    half = dff // 2
    tm = _tile(t, 512)

    def body(dy_ref, z_ref, g_ref, wd_ref, h_ref, dz_ref, df_ref, dh_ref, dg_ref, db_ref):
        dz, dgam, dbet = _ln_bwd(dy_ref[...], z_ref[...], g_ref[...])
        first = pl.program_id(0) == 0
        _acc_rows(dg_ref, dgam, first)
        _acc_rows(db_ref, dbet, first)
        dz_ref[...] = dz
        dfb = (0.5 * dz).astype(BF)
        df_ref[...] = dfb
        for j in range(2):
            c0, c1 = j * half, (j + 1) * half
            da = _dotg(dfb, wd_ref[c0:c1, :], NT).astype(BF)
            g = h_ref[0, :, c0:c1]
            u = h_ref[1, :, c0:c1]
            sg = _sigmoid(g)
            dh_ref[0, :, c0:c1] = da * u * sg * (1.0 + g * (1.0 - sg))
            dh_ref[1, :, c0:c1] = da * g * sg

    row = lambda i: (i, 0)
    fixed = lambda i: (0, 0)
    return _pcall(
        body, (dy, z, gamma, wd, h), name=name, grid=(t // tm,),
        in_specs=[pl.BlockSpec((tm, d), row), pl.BlockSpec((tm, d), row), pl.BlockSpec((1, d), fixed),
                  pl.BlockSpec(wd.shape, fixed, pipeline_mode=pl.Buffered(1)),
                  pl.BlockSpec((2, tm, dff), lambda i: (0, i, 0))],
        out_specs=[pl.BlockSpec((tm, d), row), pl.BlockSpec((tm, d), row),
                   pl.BlockSpec((2, tm, dff), lambda i: (0, i, 0)),
                   pl.BlockSpec((1, d), fixed), pl.BlockSpec((1, d), fixed)],
        out_shape=[SDS((t, d), F32), SDS((t, d), BF), SDS((2, t, dff), BF), SDS((1, d), F32), SDS((1, d), F32)],
        sem=("arbitrary",), side=side)


def _ffn_bwd_dx(dh, wup, dz, alpha, name, side=None):
    _, t, dff = dh.shape
    nq, d, w = wup.shape
    per = dff // w
    tm = _tile(t, 512)

    def body(dh_ref, w_ref, dz_ref, dx_ref):
        acc = alpha * dz_ref[...]
        for q in range(nq):
            c0 = (q % per) * w
            acc = acc + _dotg(dh_ref[q // per, :, c0:c0 + w], w_ref[q], NT)
        dx_ref[...] = acc

    row = lambda i: (i, 0)
    return _pcall(
        body, (dh, wup, dz), name=name, grid=(t // tm,),
        in_specs=[pl.BlockSpec((2, tm, dff), lambda i: (0, i, 0)),
                  pl.BlockSpec(wup.shape, lambda i: (0, 0, 0), pipeline_mode=pl.Buffered(1)),
                  pl.BlockSpec((tm, d), row)],
        out_specs=pl.BlockSpec((tm, d), row), out_shape=SDS((t, d), F32),
        sem=("parallel",), side=side)


def _dw(at, b3, ka, nb, name, side=None):
    ka_tot, t = at.shape
    gb, _, nb_tot = b3.shape
    na, ncb = ka_tot // ka, nb_tot // nb
    tm = _tile(t, DW_TOKENS, 128)

    def body(a_ref, b_ref, o_ref):
        part = _dot(a_ref[...], b_ref[0])

        @pl.when(pl.program_id(2) == 0)
        def _():
            o_ref[0, 0] = part

        @pl.when(pl.program_id(2) != 0)
        def _():
            o_ref[0, 0] += part

    return _pcall(
        body, (at, b3), name=name, grid=(na, gb * ncb, t // tm),
        in_specs=[pl.BlockSpec((ka, tm), lambda ja, jb, i: (ja, i)),
                  pl.BlockSpec((1, tm, nb), lambda ja, jb, i: (jb // ncb, i, jb % ncb))],
        out_specs=pl.BlockSpec((1, 1, ka, nb), lambda ja, jb, i: (ja, jb, 0, 0)),
        out_shape=SDS((na, gb * ncb, ka, nb), F32),
        sem=("parallel", "parallel", "arbitrary"), side=side)


def _dw_groups(ats, bs, tokens, name, side=None):
    t = bs[0].shape[0]
    na, nb = len(ats), len(bs)
    roff, coff = [0], [0]
    for a in ats:
        roff.append(roff[-1] + a.shape[0])
    for b in bs:
        coff.append(coff[-1] + b.shape[1])
    tm = _tile(t, tokens, 128)

    def body(*refs):
        a_refs, b_refs, o_ref = refs[:na], refs[na:na + nb], refs[na + nb]
        first = pl.program_id(0) == 0
        for i in range(na):
            for j in range(nb):
                _acc_rows(o_ref.at[roff[i]:roff[i + 1], coff[j]:coff[j + 1]], _dot(a_refs[i][...], b_refs[j][...]), first)

    return _pcall(
        body, (*ats, *bs), name=name, grid=(t // tm,),
        in_specs=[pl.BlockSpec((a.shape[0], tm), lambda i: (0, i)) for a in ats]
        + [pl.BlockSpec((tm, b.shape[1]), lambda i: (i, 0)) for b in bs],
        out_specs=pl.BlockSpec((roff[-1], coff[-1]), lambda i: (0, 0)),
        out_shape=SDS((roff[-1], coff[-1]), F32), sem=("arbitrary",), side=side)


def _out_bwd(dy, z, gamma, wout, widths, name, side=None):
    t, d = dy.shape
    wa, wb, wc = widths
    tm = _tile(t, 512)

    def body(dy_ref, z_ref, g_ref, w_ref, dz_ref, dzb_ref, da_ref, dbb_ref, dc_ref, dg_ref, db_ref):
        dz, dgam, dbet = _ln_bwd(dy_ref[...], z_ref[...], g_ref[...])
        first = pl.program_id(0) == 0
        _acc_rows(dg_ref, dgam, first)
        _acc_rows(db_ref, dbet, first)
        dz_ref[...] = dz
        dzb = dz.astype(BF)
        dzb_ref[...] = dzb
        da_ref[...] = _dotg(dzb, w_ref[0:wa, :], NT).astype(BF)
        dbb_ref[...] = _dotg(dzb, w_ref[wa:wa + wb, :], NT).astype(BF)
        dc_ref[...] = _dotg(dzb, w_ref[wa + wb:wa + wb + wc, :], NT).astype(BF)

    row = lambda i: (i, 0)
    fixed = lambda i: (0, 0)
    return _pcall(
        body, (dy, z, gamma, wout), name=name, grid=(t // tm,),
        in_specs=[pl.BlockSpec((tm, d), row), pl.BlockSpec((tm, d), row), pl.BlockSpec((1, d), fixed),
                  pl.BlockSpec(wout.shape, fixed)],
        out_specs=[pl.BlockSpec((tm, d), row), pl.BlockSpec((tm, d), row), pl.BlockSpec((tm, wa), row),
                   pl.BlockSpec((tm, wb), row), pl.BlockSpec((tm, wc), row),
                   pl.BlockSpec((1, d), fixed), pl.BlockSpec((1, d), fixed)],
        out_shape=[SDS((t, d), F32), SDS((t, d), BF), SDS((t, wa), BF), SDS((t, wb), BF), SDS((t, wc), BF),
                   SDS((1, d), F32), SDS((1, d), F32)],
        sem=("arbitrary",), side=side)


def _conv_bwd(pc3, dya3, cw, name, side=None):
    b, s, c3 = pc3.shape
    c = c3 // 3

    def body(p_ref, dy_ref, w_ref, dp_ref, dw_ref):
        cb = p_ref[0, :, 0:c]
        cc = p_ref[0, :, c:2 * c]
        ch = p_ref[0, :, 2 * c:3 * c]
        z = cc * ch
        z1 = _shift_rows(z, 1, True)
        z2 = _shift_rows(z, 2, True)
        w0, w1, w2 = w_ref[0:1, :], w_ref[1:2, :], w_ref[2:3, :]
        dy = dy_ref[0].astype(F32)
        dconv = dy * cb
        dz = w2 * dconv + w1 * _shift_rows(dconv, 1, False) + w0 * _shift_rows(dconv, 2, False)
        dp_ref[0, :, 0:c] = (dy * (w0 * z2 + w1 * z1 + w2 * z)).astype(BF)
        dp_ref[0, :, c:2 * c] = (dz * ch).astype(BF)
        dp_ref[0, :, 2 * c:3 * c] = (dz * cc).astype(BF)
        first = pl.program_id(0) == 0
        for r, zs in enumerate((z2, z1, z)):
            _acc_rows(dw_ref.at[r:r + 1], jnp.sum(dconv * zs, axis=0, keepdims=True), first)

    blk = lambda i: (i, 0, 0)
    return _pcall(
        body, (pc3, dya3, cw), name=name, grid=(b,),
        in_specs=[pl.BlockSpec((1, s, c3), blk), pl.BlockSpec((1, s, c), blk), pl.BlockSpec((3, c), lambda i: (0, 0))],
        out_specs=[pl.BlockSpec((1, s, c3), blk), pl.BlockSpec((3, c), lambda i: (0, 0))],
        out_shape=[SDS((b, s, c3), BF), SDS((3, c), F32)],
        sem=("arbitrary",), side=side)


def _fox_bwd(pq3, cum4, lse4, dyb3, name, side=None):
    b, s, d3 = pq3.shape
    df = d3 // 3
    hp = df // 128
    tq = _tile(s, FOX_Q_BLOCK)
    scale = FOX_HEAD_DIM ** -0.5

    def body(q_ref, k_ref, v_ref, c_ref, lse_ref, do_ref, dq_ref, dk_ref, dv_ref, dc_ref, dk_acc, dv_acc):
        masks = _head_masks(128)
        dk_acc[...] = jnp.zeros_like(dk_acc)
        dv_acc[...] = jnp.zeros_like(dv_acc)
        dc_ref[...] = jnp.zeros_like(dc_ref)
        for i in range(s // tq):
            lo, hi = i * tq, (i + 1) * tq
            q = q_ref[0, lo:hi, :]
            do = do_ref[0, lo:hi, :]
            k = k_ref[0, 0:hi, :]
            v = v_ref[0, 0:hi, :]
            lse = lse_ref[0, 0, lo:hi, :]
            dq = jnp.zeros((tq, 128), F32)
            for e in range(2):
                dom = jnp.where(masks[e], do, 0)
                parts, qm = _fox_scores(q, k, c_ref[0, 0, e:e + 1, 0:hi], lo, masks[e])
                lse_e = lse[:, FOX_HEAD_DIM * e:FOX_HEAD_DIM * e + 1]
                probs = [jnp.exp(sc - lse_e) for sc, _, _ in parts]
                dps = [_dotg(dom, v[c0:c1], NT) for _, c0, c1 in parts]
                row = functools.reduce(lambda a, c: a + c,
                                       [jnp.sum(p * dp, axis=-1, keepdims=True) for p, dp in zip(probs, dps)])
                dq_e = 0.0
                for p, dp, (_, c0, c1) in zip(probs, dps, parts):
                    ds = p * (dp - row)
                    dsb = ds.astype(BF)
                    dq_e = dq_e + _dot(dsb, k[c0:c1])
                    dk_acc[c0:c1, :] += _dotg(dsb, qm, TN)
                    dv_acc[c0:c1, :] += _dotg(p.astype(BF), dom, TN)
                    dc_ref[0, 0, e:e + 1, c0:c1] -= jnp.sum(ds, axis=0, keepdims=True)
                dq = jnp.where(masks[e], dq_e * scale, dq)
            dq_ref[0, lo:hi, :] = dq.astype(BF)
        dk_ref[0] = dk_acc[...].astype(BF)
        dv_ref[0] = dv_acc[...].astype(BF)

    blk = lambda off: pl.BlockSpec((1, s, 128), lambda i, j: (i, 0, off + j))
    cblk = pl.BlockSpec((1, 1, 2, s), lambda i, j: (i, j, 0, 0))
    return _pcall(
        body, (pq3, pq3, pq3, cum4, lse4, dyb3), name=name, grid=(b, hp),
        in_specs=[blk(0), blk(hp), blk(2 * hp), cblk,
                  pl.BlockSpec((1, 1, s, 128), lambda i, j: (i, j, 0, 0)), blk(0)],
        out_specs=[blk(0), blk(0), blk(0), cblk],
        out_shape=[SDS((b, s, df), BF), SDS((b, s, df), BF), SDS((b, s, df), BF), SDS(cum4.shape, F32)],
        scratch_shapes=[pltpu.VMEM((s, 128), F32), pltpu.VMEM((s, 128), F32)],
        sem=("parallel", "parallel"), side=side)


def _cum_bwd(dcum, flog, xt, name, side=None):
    b, h, s = dcum.shape
    d = xt.shape[0]

    def body(dc_ref, fl_ref, xt_ref, dfl_ref, dbf_ref, dwf_ref):
        dfl = _prefix_sum_lanes(dc_ref[0], reverse=True) * _sigmoid(-fl_ref[0])
        dfl_ref[0] = dfl
        first = pl.program_id(0) == 0
        _acc_rows(dbf_ref, jnp.broadcast_to(jnp.sum(dfl, axis=-1, keepdims=True), (h, 128)), first)
        dflp = jnp.concatenate([dfl, jnp.zeros((HEAD_ROWS - h, s), F32)], axis=0).astype(BF)
        _acc_rows(dwf_ref, _dotg(dflp, xt_ref[...], NT)[0:h], first)

    blk = lambda i: (i, 0, 0)
    return _pcall(
        body, (dcum, flog, xt), name=name, grid=(b,),
        in_specs=[pl.BlockSpec((1, h, s), blk), pl.BlockSpec((1, h, s), blk), pl.BlockSpec((d, s), lambda i: (0, i))],
        out_specs=[pl.BlockSpec((1, h, s), blk), pl.BlockSpec((h, 128), lambda i: (0, 0)),
                   pl.BlockSpec((h, d), lambda i: (0, 0))],
        out_shape=[SDS((b, h, s), F32), SDS((h, 128), F32), SDS((h, d), F32)],
        sem=("arbitrary",), side=side)


def _sgu_bwd(ps, dyc, lng, lnb, ws, bs, name, side=None):
    t, ds2 = ps.shape
    ds = ds2 // 2
    ng, c, _ = ws.shape
    tm = _tile(t, 512, c)

    def body(p_ref, dy_ref, g_ref, b_ref, ws_ref, bs_ref, dp_ref, dws_ref, dbs_ref, dg_ref, db_ref, dvn_acc):
        tri, gmasks, wm, bias = _sgu_consts(ws_ref, bs_ref, ds)
        su = p_ref[:, 0:ds]
        sv = p_ref[:, ds:ds2]
        up = _gelu(su)
        gv = _gelu(sv)
        xhat, rstd = _ln_stats(gv)
        vnb = (xhat * g_ref[...] + b_ref[...]).astype(BF)
        dy = dy_ref[...].astype(F32)
        dws = [jnp.zeros((c, c), F32) for _ in range(ng)]
        dbs = [jnp.zeros((c, 1), F32) for _ in range(ng)]
        for n in range(tm // c):
            r0, r1 = n * c, (n + 1) * c
            mixed = _sgu_mix(wm, vnb[r0:r1], bias, gmasks)
            dp_ref[r0:r1, 0:ds] = (dy[r0:r1] * mixed * _gelu_grad(su[r0:r1])).astype(BF)
            dmix = dy[r0:r1] * up[r0:r1]
            dvn = jnp.zeros((c, ds), F32)
            for g in range(ng):
                dmg = jnp.where(gmasks[g], dmix, 0.0)
                dmb = dmg.astype(BF)
                dws[g] = dws[g] + _dotg(dmb, vnb[r0:r1], NT)
                dbs[g] = dbs[g] + jnp.sum(dmg, axis=-1, keepdims=True)
                dvn = dvn + _dotg(wm[g], dmb, TN)
            dvn_acc[r0:r1, :] = dvn
        dvn_all = dvn_acc[...]
        gdv = dvn_all * g_ref[...]
        m1 = jnp.mean(gdv, axis=-1, keepdims=True)
        m2 = jnp.mean(gdv * xhat, axis=-1, keepdims=True)
        dgv = rstd * (gdv - m1 - xhat * m2)
        dp_ref[:, ds:ds2] = (dgv * _gelu_grad(sv)).astype(BF)
        first = pl.program_id(0) == 0
        _acc_rows(dg_ref, jnp.sum(dvn_all * xhat, axis=0, keepdims=True), first)
        _acc_rows(db_ref, jnp.sum(dvn_all, axis=0, keepdims=True), first)
        for g in range(ng):
            _acc_rows(dws_ref.at[g], jnp.where(tri, dws[g], 0.0), first)
            _acc_rows(dbs_ref.at[g], dbs[g], first)

    row = lambda i: (i, 0)
    fixed2 = lambda i: (0, 0)
    fixed3 = lambda i: (0, 0, 0)
    return _pcall(
        body, (ps, dyc, lng, lnb, ws, bs), name=name, grid=(t // tm,),
        in_specs=[pl.BlockSpec((tm, ds2), row), pl.BlockSpec((tm, ds), row), pl.BlockSpec((1, ds), fixed2),
                  pl.BlockSpec((1, ds), fixed2), pl.BlockSpec(ws.shape, fixed3), pl.BlockSpec(bs.shape, fixed3)],
        out_specs=[pl.BlockSpec((tm, ds2), row), pl.BlockSpec(ws.shape, fixed3), pl.BlockSpec(bs.shape, fixed3),
                   pl.BlockSpec((1, ds), fixed2), pl.BlockSpec((1, ds), fixed2)],
        out_shape=[SDS((t, ds2), BF), SDS(ws.shape, F32), SDS(bs.shape, F32), SDS((1, ds), F32), SDS((1, ds), F32)],
        scratch_shapes=[pltpu.VMEM((tm, ds), F32)],
        sem=("arbitrary",), side=side)


def _mix_bwd_dx(dz, dconv, dq, dk, dv, dsgu, dflog, w_main, wft, seq, alpha, name, side=None):
    t, d = dz.shape
    groups = [dconv, dq, dk, dv, dsgu]
    offs = [0]
    for g in groups:
        offs.append(offs[-1] + g.shape[1])
    h = dflog.shape[1]
    tm = _tile(seq, 512)
    per_seq = seq // tm

    def body(dz_ref, a0, a1, a2, a3, a4, dfl_ref, w_ref, wf_ref, dx_ref):
        dflp = jnp.concatenate([dfl_ref[0], jnp.zeros((HEAD_ROWS - h, tm), F32)], axis=0).astype(BF)
        acc = alpha * dz_ref[...] + _dotg(dflp, wf_ref[...], TN)
        for k, a_ref in enumerate((a0, a1, a2, a3, a4)):
            acc = acc + _dotg(a_ref[...], w_ref[:, offs[k]:offs[k + 1]], NT)
        dx_ref[...] = acc

    row = lambda i: (i, 0)
    return _pcall(
        body, (dz, *groups, dflog, w_main, wft), name=name, grid=(t // tm,),
        in_specs=[pl.BlockSpec((tm, d), row)] + [pl.BlockSpec((tm, g.shape[1]), row) for g in groups]
        + [pl.BlockSpec((1, h, tm), lambda i: (i // per_seq, 0, i % per_seq)),
           pl.BlockSpec(w_main.shape, lambda i: (0, 0)), pl.BlockSpec(wft.shape, lambda i: (0, 0))],
        out_specs=pl.BlockSpec((tm, d), row), out_shape=SDS((t, d), F32),
        sem=("parallel",), side=side)


def _adam_math(w, g, m, v):
    c1 = 1.0 / (1.0 - ADAM_B1 ** ADAM_STEP)
    c2 = 1.0 / (1.0 - ADAM_B2 ** ADAM_STEP)
    nm = ADAM_B1 * m + (1.0 - ADAM_B1) * g
    nv = ADAM_B2 * v + (1.0 - ADAM_B2) * (g * g)
    delta = -ADAM_LR * ((nm * c1) / (jnp.sqrt(nv * c2) + ADAM_EPS) + ADAM_WD * w)
    return delta, nm, nv


def _adamw_small(w, g, m, v, name):
    r, c = w.shape
    tr = _tile(r, 512)

    def body(w_ref, g_ref, m_ref, v_ref, d_ref, nm_ref, nv_ref):
        d_ref[...], nm_ref[...], nv_ref[...] = _adam_math(w_ref[...], g_ref[...], m_ref[...], v_ref[...])

    blk = pl.BlockSpec((tr, c), lambda i: (i, 0))
    return _pcall(body, (w, g, m, v), name=name, grid=(r // tr,), in_specs=[blk] * 4, out_specs=[blk] * 3,
                  out_shape=[SDS((r, c), F32)] * 3, sem=("parallel",))


def _adamw_shard(w, m, v, tot, recv, cq, layer, prev, name, side=None):
    nl, xr, yc = w.shape
    h = xr // 2
    tr = _tile(h, 256)
    nt = h // tr

    def body(cq_ref, w_ref, m_ref, v_ref, t_ref, r_ref, *rest):
        g_ref, d_ref, nm_ref, nv_ref = rest[-4:]
        g = jnp.where(pl.program_id(0) == cq_ref[0], t_ref[...], r_ref[...])
        g_ref[0] = g
        d_ref[0], nm_ref[0], nv_ref[0] = _adam_math(w_ref[0], g, m_ref[0], v_ref[0])

    slab = pl.BlockSpec((1, tr, yc), lambda hf, i, cq_ref: (layer, hf * nt + i, 0))
    mine = pl.BlockSpec((tr, yc), lambda hf, i, cq_ref: (jnp.where(hf == cq_ref[0], i, 0), 0))
    theirs = pl.BlockSpec((tr, yc), lambda hf, i, cq_ref: (jnp.where(hf == cq_ref[0], 0, i), 0))
    operands = [w, m, v, tot, recv]
    in_specs = [slab, slab, slab, mine, theirs]
    aliases = None
    if prev is not None:
        operands += list(prev)
        in_specs += [HBM] * 4
        aliases = {6 + k: k for k in range(4)}
    return _pcall(body, operands, name=name, grid=(2, nt), prefetch=(cq,), in_specs=in_specs,
                  out_specs=[slab] * 4, out_shape=[SDS(w.shape, F32)] * 4, aliases=aliases,
                  sem=("parallel", "parallel"), side=side)


BIG = ("ffn1_w_up", "ffn1_w_down", "mix_w_in", "mix_w_out", "ffn2_w_up", "ffn2_w_down")
SMALL = ("ln1_g", "ln1_b", "fox_b_f", "sgu_ln_g", "sgu_ln_b", "sgu_w_s", "sgu_b_s", "ln2_g", "ln2_b", "ln3_g", "ln3_b")
ORDER = ("ln1_g", "ln1_b", "ffn1_w_up", "ffn1_w_down", "mix_w_in", "fox_b_f", "conv_w", "sgu_ln_g", "sgu_ln_b",
         "sgu_w_s", "sgu_b_s", "mix_w_out", "ln2_g", "ln2_b", "ffn2_w_up", "ffn2_w_down", "ln3_g", "ln3_b")


def _row(v):
    return v.reshape(1, -1)


class _Pipe:
    def __init__(self, stages):
        self.stages = list(stages)
        self.pos = 0
        self.last = None

    def kind(self):
        return self.stages[self.pos][0] if self.pos < len(self.stages) else None


class _Sched:
    def __init__(self):
        self.pipes = []
        self.n_alone = 0

    def add(self, stages):
        self.pipes.append(_Pipe(stages))

    def _take_comms(self, skip=None):
        jobs = []
        for p in self.pipes:
            if p is not skip and p.kind() == "comm":
                jobs.append((p, p.stages[p.pos][1]()))
        return jobs

    @staticmethod
    def _landed(jobs):
        for p, side in jobs:
            p.last = side.results
            p.pos += 1

    def carry(self, builder, *args, **kw):
        jobs = self._take_comms()
        res = builder(*args, side=_join([s for _, s in jobs]), **kw)
        self._landed(jobs)
        self._computes(ride=False)
        return res

    def _computes(self, ride):
        again = True
        while again:
            again = False
            for p in self.pipes:
                if p.kind() == "compute":
                    jobs = self._take_comms(skip=p) if ride else []
                    p.stages[p.pos][1](p.last, _join([s for _, s in jobs]))
                    p.pos += 1
                    self._landed(jobs)
                    again = True

    def drain(self):
        while any(p.kind() is not None for p in self.pipes):
            self._computes(ride=True)
            jobs = self._take_comms()
            if jobs:
                _run_side(_join([s for _, s in jobs]), "exchange_tail_%d" % self.n_alone)
                self.n_alone += 1
                self._landed(jobs)


def _forward_layer(x, xt, p, dims, alpha, l, ride, target=None):
    b, s = dims["b"], dims["s"]
    t, d = x.shape
    tag = "l%d_" % l

    def run(stage, builder, *args, **kw):
        side, on_done = ride.get(stage, (None, None))
        res = builder(*args, tag + stage, side=side, **kw)
        if on_done is not None:
            on_done()
        return res

    if xt is None:
        h1, a1, a1t, xt = run("ffn1_up", _ffn_up_fwd, x, p["wup1"], emit_xt=True)
    else:
        h1, a1, a1t = run("ffn1_up", _ffn_up_fwd, x, p["wup1"])
    z1, x1, x1t = run("ffn1_down", _res_ln_fwd, [a1], p["wd1"], x, p["ln1_g"], p["ln1_b"], alpha, 0.5)
    pc, pq, ps = run("mix_proj", _mix_proj_fwd, x1, p["win"], dims["proj_widths"])
    x1_3 = x1.reshape(b, s, d)
    flog, cum = run("fox_gate", _cum_fwd, x1_3, p["wft"], p["bf"])
    nh = flog.shape[1]
    cum4 = cum.reshape(b, nh // 2, 2, s)
    pc3 = pc.reshape(b, s, -1)
    pq3 = pq.reshape(b, s, -1)
    ya = run("conv", _conv_fwd, pc3, p["cw"]).reshape(t, -1)
    yb3, lse4 = run("fox", _fox_fwd, pq3, cum4)
    yb = yb3.reshape(t, -1)
    yc = run("sgu", _sgu_fwd, ps, p["sgu_g"], p["sgu_b"], p["ws"], p["bs"])
    z2, x2, x2t, yat, ybt, yct = run("mix_out", _res_ln_fwd, [ya, yb, yc], p["wout"], x1, p["ln2_g"], p["ln2_b"],
                                      alpha, 1.0, parts_t=True)
    h2, a2, a2t = run("ffn2_up", _ffn_up_fwd, x2, p["wup2"])
    z3, x3, x3t = run("ffn2_down", _res_ln_fwd, [a2], p["wd2"], x2, p["ln3_g"], p["ln3_b"], alpha, 0.5, target=target)
    saved = dict(xt=xt, h1=h1, a1t=a1t, z1=z1, x1=x1, x1t=x1t, pc3=pc3, pq3=pq3, ps=ps, flog=flog, cum4=cum4,
                 lse4=lse4, yat=yat, ybt=ybt, yct=yct, z2=z2, x2t=x2t, h2=h2, a2t=a2t, z3=z3)
    return x3, x3t, saved


def _ffn_backward(sched, emit, which, dy, z, gamma, wd, wup, h, a_t, x_in_t, alpha, tag, after_mid=None):
    dz, df, dh, dgam, dbet = sched.carry(_ffn_bwd_mid, dy, z, gamma, wd, h, tag + "_bwd_mid")
    if after_mid is not None:
        after_mid(dgam, dbet)
    nq, d, w = wup.shape
    emit(which + "_w_up", sched.carry(_dw, x_in_t, dh, d, w, tag + "_dw_up")[0])
    half = wd.shape[0] // 2
    emit(which + "_w_down", sched.carry(_dw, a_t, df[None], half, d, tag + "_dw_down").reshape(nq, -1, d))
    dx = sched.carry(_ffn_bwd_dx, dh, wup, dz, alpha, tag + "_bwd_dx")
    return dx, dgam, dbet


def _backward_layer(sched, emit, emit_small, dy, sv, p, dims, alpha, l):
    b, s = dims["b"], dims["s"]
    tag = "l%d_" % l
    t, d = dy.shape
    g = {}
    dx2, g["ln3_g"], g["ln3_b"] = _ffn_backward(sched, emit, "ffn2", dy, sv["z3"], p["ln3_g"], p["wd2"], p["wup2"],
                                                sv["h2"], sv["a2t"], sv["x2t"], alpha, tag + "ffn2")
    wa, wb, wc = sv["yat"].shape[0], sv["ybt"].shape[0], sv["yct"].shape[0]
    dz2, dz2b, dya, dyb, dyc, g["ln2_g"], g["ln2_b"] = sched.carry(
        _out_bwd, dx2, sv["z2"], p["ln2_g"], p["wout"], (wa, wb, wc), tag + "mix_out_bwd")
    emit("mix_w_out", sched.carry(_dw_groups, [sv["yat"], sv["ybt"], sv["yct"]], [dz2b], DW_TOKENS,
                                  tag + "dw_out").reshape(N_SHARDS, -1, d))
    dpc3, g["conv_w"] = sched.carry(_conv_bwd, sv["pc3"], dya.reshape(b, s, -1), p["cw"], tag + "conv_bwd")
    dq3, dk3, dv3, dcum4 = sched.carry(_fox_bwd, sv["pq3"], sv["cum4"], sv["lse4"], dyb.reshape(b, s, -1),
                                       tag + "fox_bwd")
    nh = sv["flog"].shape[1]
    dflog, dbf, dwft = sched.carry(_cum_bwd, dcum4.reshape(b, nh, s), sv["flog"], sv["x1t"], tag + "fox_gate_bwd")
    g["fox_b_f"] = dbf[:, 0]
    dps, g["sgu_w_s"], dbs, g["sgu_ln_g"], g["sgu_ln_b"] = sched.carry(
        _sgu_bwd, sv["ps"], dyc, p["sgu_g"], p["sgu_b"], p["ws"], p["bs"], tag + "sgu_bwd")
    g["sgu_b_s"] = dbs[:, :, 0]
    dpc = dpc3.reshape(t, -1)
    dq, dk, dv = dq3.reshape(t, -1), dk3.reshape(t, -1), dv3.reshape(t, -1)
    main = sched.carry(_dw_groups, [sv["x1t"]], [dpc, dq, dk, dv, dps], DW_TOKENS // 2, tag + "dw_in")
    n_main = main.shape[1] - dps.shape[1]
    w_in_grad = jnp.concatenate([main[:, :n_main], dwft.T, main[:, n_main:]], axis=1)
    emit("mix_w_in", jnp.moveaxis(w_in_grad.reshape(d, N_SHARDS, -1), 1, 0))
    dx1 = sched.carry(_mix_bwd_dx, dz2, dpc, dq, dk, dv, dps, dflog, p["win"], p["wft"], s, alpha, tag + "mix_bwd_dx")

    def small_ready(dgam, dbet):
        g["ln1_g"], g["ln1_b"] = dgam, dbet
        emit_small(g)

    dx0, _, _ = _ffn_backward(sched, emit, "ffn1", dx1, sv["z1"], p["ln1_g"], p["wd1"], p["wup1"],
                              sv["h1"], sv["a1t"], sv["xt"], alpha, tag + "ffn1", after_mid=small_ready)
    return dx0


def _span(n):
    return -(-n // 1024) * 1024


def _pack_rows(flat_list):
    return jnp.concatenate([jnp.pad(v, (0, _span(v.shape[0]) - v.shape[0])) for v in flat_list]).reshape(-1, 128)


def kernel(x, ln1_g, ln1_b, ffn1_w_up, ffn1_w_down, mix_w_in, fox_b_f, conv_w, sgu_ln_g, sgu_ln_b, sgu_w_s, sgu_b_s, mix_w_out, ln2_g, ln2_b, ffn2_w_up, ffn2_w_down, ln3_g, ln3_b, loss_target, m_ln1_g, m_ln1_b, m_ffn1_w_up, m_ffn1_w_down, m_mix_w_in, m_fox_b_f, m_conv_w, m_sgu_ln_g, m_sgu_ln_b, m_sgu_w_s, m_sgu_b_s, m_mix_w_out, m_ln2_g, m_ln2_b, m_ffn2_w_up, m_ffn2_w_down, m_ln3_g, m_ln3_b, v_ln1_g, v_ln1_b, v_ffn1_w_up, v_ffn1_w_down, v_mix_w_in, v_fox_b_f, v_conv_w, v_sgu_ln_g, v_sgu_ln_b, v_sgu_w_s, v_sgu_b_s, v_mix_w_out, v_ln2_g, v_ln2_b, v_ffn2_w_up, v_ffn2_w_down, v_ln3_g, v_ln3_b):
    wts = dict(ln1_g=ln1_g, ln1_b=ln1_b, ffn1_w_up=ffn1_w_up, ffn1_w_down=ffn1_w_down, mix_w_in=mix_w_in,
               fox_b_f=fox_b_f, conv_w=conv_w, sgu_ln_g=sgu_ln_g, sgu_ln_b=sgu_ln_b, sgu_w_s=sgu_w_s,
               sgu_b_s=sgu_b_s, mix_w_out=mix_w_out, ln2_g=ln2_g, ln2_b=ln2_b, ffn2_w_up=ffn2_w_up,
               ffn2_w_down=ffn2_w_down, ln3_g=ln3_g, ln3_b=ln3_b)
    mom = dict(ln1_g=m_ln1_g, ln1_b=m_ln1_b, ffn1_w_up=m_ffn1_w_up, ffn1_w_down=m_ffn1_w_down, mix_w_in=m_mix_w_in,
               fox_b_f=m_fox_b_f, conv_w=m_conv_w, sgu_ln_g=m_sgu_ln_g, sgu_ln_b=m_sgu_ln_b, sgu_w_s=m_sgu_w_s,
               sgu_b_s=m_sgu_b_s, mix_w_out=m_mix_w_out, ln2_g=m_ln2_g, ln2_b=m_ln2_b, ffn2_w_up=m_ffn2_w_up,
               ffn2_w_down=m_ffn2_w_down, ln3_g=m_ln3_g, ln3_b=m_ln3_b)
    var = dict(ln1_g=v_ln1_g, ln1_b=v_ln1_b, ffn1_w_up=v_ffn1_w_up, ffn1_w_down=v_ffn1_w_down, mix_w_in=v_mix_w_in,
               fox_b_f=v_fox_b_f, conv_w=v_conv_w, sgu_ln_g=v_sgu_ln_g, sgu_ln_b=v_sgu_ln_b, sgu_w_s=v_sgu_w_s,
               sgu_b_s=v_sgu_b_s, mix_w_out=v_mix_w_out, ln2_g=v_ln2_g, ln2_b=v_ln2_b, ffn2_w_up=v_ffn2_w_up,
               ffn2_w_down=v_ffn2_w_down, ln3_g=v_ln3_g, ln3_b=v_ln3_b)

    nl = ln1_g.shape[0]
    b, s, d = x.shape
    t = b * s
    alpha = (2 * nl) ** 0.25
    cw_sh = conv_w.shape[2]
    d_conv = cw_sh * N_SHARDS
    d_sgu = sgu_ln_g.shape[1]
    nh = fox_b_f.shape[1]
    d_fox = nh * FOX_HEAD_DIM
    n_main = 3 * d_conv + 3 * d_fox
    dims = dict(b=b, s=s, proj_widths=(3 * d_conv, 3 * d_fox, 2 * d_sgu))
    cpos = lax.axis_index("c").astype(jnp.int32)
    qpos = (2 * lax.axis_index("x") + lax.axis_index("y")).astype(jnp.int32)
    cq = jnp.stack([cpos, qpos])

    me = (2 * qpos + cpos).reshape(1)
    assert nl == 2, "the gather schedule below names the carriers of a two-layer step"

    conv_tile = jnp.pad(conv_w, ((0, 0), (0, 8 - conv_w.shape[1]), (0, 128 - cw_sh)))
    params = [dict(bf=fox_b_f[l].reshape(nh, 1), sgu_g=_row(sgu_ln_g[l]), sgu_b=_row(sgu_ln_b[l]), ws=sgu_w_s[l],
                   bs=sgu_b_s[l][:, :, None], ln1_g=_row(ln1_g[l]), ln1_b=_row(ln1_b[l]), ln2_g=_row(ln2_g[l]),
                   ln2_b=_row(ln2_b[l]), ln3_g=_row(ln3_g[l]), ln3_b=_row(ln3_b[l])) for l in range(nl)]

    def operands_of(k, arr):
        if k == "mix_w_in":
            w_in = jnp.moveaxis(arr, 0, 1).reshape(d, -1)
            return dict(win=jnp.concatenate([w_in[:, :n_main], w_in[:, n_main + nh:]], axis=1),
                        wft=jnp.pad(w_in[:, n_main:n_main + nh].T, ((0, HEAD_ROWS - nh), (0, 0))))
        if k == "conv_w":
            return dict(cw=jnp.moveaxis(arr[:, :3, :cw_sh], 0, 1).reshape(3, d_conv))
        if k in ("ffn1_w_up", "ffn2_w_up"):
            return {"wup" + k[3]: arr}
        return {dict(ffn1_w_down="wd1", ffn2_w_down="wd2", mix_w_out="wout")[k]: arr.reshape(-1, d)}

    def gather(l, keys):
        side = _side_gather([conv_tile[l] if k == "conv_w" else wts[k][l].astype(BF) for k in keys],
                            [k != "conv_w" for k in keys])

        def install():
            for k, arr in zip(keys, side.results):
                params[l].update(operands_of(k, arr))
        return side, install

    first, install_first = gather(0, ["ffn1_w_up"])
    _run_side(first, "gather_first")
    install_first()
    rides = [{"ffn1_up": gather(0, ["ffn1_w_down", "mix_w_in", "mix_w_out", "conv_w"]),
              "ffn1_down": gather(0, ["ffn2_w_up"]),
              "mix_proj": gather(0, ["ffn2_w_down"]),
              "fox": gather(1, ["ffn1_w_up", "ffn1_w_down", "mix_w_in", "mix_w_out", "conv_w"]),
              "ffn2_up": gather(1, ["ffn2_w_up", "ffn2_w_down"])}, {}]

    act, act_t = x.reshape(t, d), None
    saved = []
    for l in range(nl):
        act, act_t, sv = _forward_layer(act, act_t, params[l], dims, alpha, l, rides[l],
                                        target=loss_target.reshape(t, d) if l == nl - 1 else None)
        saved.append(sv)
    dy, loss_blk = act, act_t

    sched = _Sched()
    prev = {k: None for k in BIG}
    red = {}

    def emit_for(l):
        def emit(key, g):
            st = {}
            name = "l%d_%s" % (l, key)

            def pair_sum(res, side):
                st["p"] = _pair_sum(g, res[0], cq, "rs_pair_sum_" + name, side=side)

            def chip_sum(res, side):
                st["t"] = _chip_sum(st["p"], res[0], cq, "rs_chip_sum_" + name, side=side)

            def adamw(res, side):
                prev[key] = _adamw_shard(wts[key], mom[key], var[key], st["t"], res[0], cq, l, prev[key],
                                         "adamw_" + name, side=side)

            sched.add([("comm", lambda: _side_pair_send([g])), ("compute", pair_sum),
                       ("comm", lambda: _side_scatter([st["p"]])), ("compute", chip_sum),
                       ("comm", lambda: _side_pair_share([st["t"]])), ("compute", adamw)])
        return emit

    def emit_small_for(l):
        def emit_small(g):
            flat = [g[k].reshape(-1) for k in SMALL] + [g["conv_w"].reshape(-1)]
            if l == nl - 1:
                flat.append(loss_blk[0, 0:1])
            vec = _pack_rows(flat)

            def slot_sum(res, side):
                red[l] = _sum_slots(vec, res[0], me, "small_sum_l%d" % l, side=side)

            sched.add([("comm", lambda: _side_bcast(vec)), ("compute", slot_sum)])
        return emit_small

    for l in reversed(range(nl)):
        dy = _backward_layer(sched, emit_for(l), emit_small_for(l), dy, saved[l], params[l], dims, alpha, l)
    sched.drain()
    grad_x = dy.reshape(b, s, d)
    gfin, delta, new_m, new_v = {}, {}, {}, {}
    for k in BIG:
        gfin[k], delta[k], new_m[k], new_v[k] = prev[k]

    gsm = {k: [] for k in SMALL + ("conv_w",)}
    for l in range(nl):
        flat_l = red[l].reshape(-1)
        off = 0
        for k in SMALL:
            n = wts[k][l].size
            gsm[k].append(flat_l[off:off + n].reshape(wts[k][l].shape))
            off += _span(n)
        n = 3 * d_conv
        gsm["conv_w"].append(lax.dynamic_slice_in_dim(flat_l[off:off + n].reshape(3, d_conv), qpos * cw_sh, cw_sh,
                                                      axis=1))
        off += _span(n)
        if l == nl - 1:
            loss = flat_l[off]
    for k in gsm:
        gfin[k] = jnp.stack(gsm[k])
    small_keys = SMALL + ("conv_w",)
    sizes = [wts[k].size for k in small_keys]
    pk = lambda src: _pack_rows([src[k].reshape(-1) for k in small_keys])
    dl, nm, nv = _adamw_small(pk(wts), pk(gfin), pk(mom), pk(var), "adamw_small")
    off = 0
    for k, n in zip(small_keys, sizes):
        shp = wts[k].shape
        delta[k] = dl.reshape(-1)[off:off + n].reshape(shp)
        new_m[k] = nm.reshape(-1)[off:off + n].reshape(shp)
        new_v[k] = nv.reshape(-1)[off:off + n].reshape(shp)
        off += _span(n)

    return (loss, grad_x, *[gfin[k] for k in ORDER], *[delta[k] for k in ORDER],
            *[new_m[k] for k in ORDER], *[new_v[k] for k in ORDER])
```

```python
import functools

import jax
import jax.numpy as jnp
from jax import lax
from jax.experimental import pallas as pl
from jax.experimental.pallas import tpu as pltpu

F32 = jnp.float32
BF = jnp.bfloat16
SDS = jax.ShapeDtypeStruct
MESH = pl.DeviceIdType.MESH

LN_EPS = 1e-5
FOX_HEAD_DIM = 64
FOX_Q_BLOCK = 512
DW_TOKENS = 2048
HEAD_ROWS = 128
GELU_K = 0.7978845608028654
GELU_C = 0.044715
NEG_BIG = -1e30
N_SHARDS = 4

ADAM_LR = 0.001
ADAM_B1 = 0.9
ADAM_B2 = 0.999
ADAM_EPS = 1e-08
ADAM_WD = 0.01
ADAM_STEP = 10

VMEM_LIMIT_BYTES = 56 * 1024 * 1024
NT = (((1,), (1,)), ((), ()))
TN = (((0,), (0,)), ((), ()))
HBM = pl.BlockSpec(memory_space=pl.ANY)


def _tile(n, pref, mult=8):
    t = min(n, pref)
    while n % t or t % mult:
        t -= mult
    return t


def _dot(a, b):
    return jnp.dot(a, b, preferred_element_type=F32)


def _dotg(a, b, dims):
    return lax.dot_general(a, b, dims, preferred_element_type=F32)


def _sigmoid(x):
    return 1.0 / (1.0 + jnp.exp(-x))


def _gelu(x):
    return 0.5 * x * (1.0 + jnp.tanh(GELU_K * (x + GELU_C * x * x * x)))


def _gelu_grad(x):
    t = jnp.tanh(GELU_K * (x + GELU_C * x * x * x))
    return 0.5 * (1.0 + t) + 0.5 * x * (1.0 - t * t) * GELU_K * (1.0 + 3.0 * GELU_C * x * x)


def _ln_stats(z):
    mu = jnp.mean(z, axis=-1, keepdims=True)
    zc = z - mu
    var = jnp.mean(zc * zc, axis=-1, keepdims=True)
    rstd = lax.rsqrt(var + LN_EPS)
    return zc * rstd, rstd


def _ln_bwd(dy, z, g):
    xhat, rstd = _ln_stats(z)
    gdy = dy * g
    m1 = jnp.mean(gdy, axis=-1, keepdims=True)
    m2 = jnp.mean(gdy * xhat, axis=-1, keepdims=True)
    dz = rstd * (gdy - m1 - xhat * m2)
    return dz, jnp.sum(dy * xhat, axis=0, keepdims=True), jnp.sum(dy, axis=0, keepdims=True)


class _Side:
    def __init__(self, ins, out_shapes, sems, start, finish):
        self.ins, self.out_shapes, self.sems = list(ins), list(out_shapes), list(sems)
        self.start, self.finish = start, finish
        self.results = None


def _join(sides):
    sides = [s for s in sides if s is not None]
    if not sides:
        return None
    ins = [a for s in sides for a in s.ins]
    outs = [a for s in sides for a in s.out_shapes]
    sems = [a for s in sides for a in s.sems]

    def parts(seq, field):
        out, o = [], 0
        for s in sides:
            n = len(getattr(s, field))
            out.append(seq[o:o + n])
            o += n
        return out

    def run(which):
        def fn(i, o, m):
            for s, a, b, c in zip(sides, parts(i, "ins"), parts(o, "out_shapes"), parts(m, "sems")):
                getattr(s, which)(a, b, c)
        return fn

    joined = _Side(ins, outs, sems, run("start"), run("finish"))
    joined.members = sides
    return joined


def _deliver(side, results):
    members = getattr(side, "members", None)
    side.results = list(results)
    if members:
        o = 0
        for s in members:
            n = len(s.out_shapes)
            _deliver(s, results[o:o + n])
            o += n


def _pcall(body, operands, *, name, grid, in_specs, out_specs, out_shape, sem, scratch_shapes=(),
           prefetch=(), aliases=None, side=None):
    single = not isinstance(out_shape, (list, tuple))
    out_shape = [out_shape] if single else list(out_shape)
    out_specs = [out_specs] if single else list(out_specs)
    in_specs, scratch_shapes = list(in_specs), list(scratch_shapes)
    n_pre, n_in, n_out, n_sc = len(prefetch), len(in_specs), len(out_shape), len(scratch_shapes)
    fn = body
    extra = []
    if side is not None:
        s_in, s_out = len(side.ins), len(side.out_shapes)

        def fn(*refs):
            pre, rest = refs[:n_pre], refs[n_pre:]
            m_in, c_in = rest[:n_in], rest[n_in:n_in + s_in]
            rest = rest[n_in + s_in:]
            m_out, c_out = rest[:n_out], rest[n_out:n_out + s_out]
            rest = rest[n_out + s_out:]
            m_sc, c_sc = rest[:n_sc], rest[n_sc:]
            first = pl.program_id(0) == 0
            last = pl.program_id(0) == grid[0] - 1
            for a in range(1, len(grid)):
                first = jnp.logical_and(first, pl.program_id(a) == 0)
                last = jnp.logical_and(last, pl.program_id(a) == grid[a] - 1)

            @pl.when(first)
            def _():
                side.start(c_in, c_out, c_sc)

            body(*pre, *m_in, *m_out, *m_sc)

            @pl.when(last)
            def _():
                side.finish(c_in, c_out, c_sc)

        in_specs = in_specs + [HBM] * s_in
        out_specs = out_specs + [HBM] * s_out
        out_shape = out_shape + side.out_shapes
        scratch_shapes = scratch_shapes + side.sems
        extra = side.ins
        sem = ("arbitrary",) * len(grid)
    params = pltpu.CompilerParams(dimension_semantics=tuple(sem), vmem_limit_bytes=VMEM_LIMIT_BYTES)
    kw = dict(input_output_aliases=aliases) if aliases else {}
    if n_pre:
        spec = pltpu.PrefetchScalarGridSpec(num_scalar_prefetch=n_pre, grid=grid, in_specs=in_specs,
                                            out_specs=out_specs, scratch_shapes=scratch_shapes)
        call = pl.pallas_call(fn, name=name, grid_spec=spec, out_shape=out_shape, compiler_params=params, **kw)
    else:
        call = pl.pallas_call(fn, name=name, grid=grid, in_specs=in_specs, out_specs=out_specs,
                              out_shape=out_shape, scratch_shapes=scratch_shapes, compiler_params=params, **kw)
    res = call(*prefetch, *operands, *extra)
    if side is not None:
        _deliver(side, res[n_out:])
        res = res[:n_out]
    return res[0] if single else res


def _run_side(side, name):
    def body(*refs):
        n_in, n_out = len(side.ins), len(side.out_shapes)
        i, o, m = refs[:n_in], refs[n_in:n_in + n_out], refs[n_in + n_out:]
        side.start(i, o, m)
        side.finish(i, o, m)

    res = pl.pallas_call(body, name=name, in_specs=[HBM] * len(side.ins), out_specs=[HBM] * len(side.out_shapes),
                         out_shape=side.out_shapes, scratch_shapes=side.sems)(*side.ins)
    _deliver(side, res)


def _mesh_pos():
    x, y, c = lax.axis_index("x"), lax.axis_index("y"), lax.axis_index("c")
    chips = [(1 - x, y), (x, 1 - y), (1 - x, 1 - y)]
    return x, y, c, chips


def _rows(ref, lead, half, n_rows):
    return ref.at[tuple(lead) + (pl.ds(half * n_rows, n_rows),)]


def _side_gather(shards, split):
    n = len(shards)
    hs = [w.shape[0] // 2 for w in shards]

    def plan(ins, outs, sems):
        ssem, rsem = sems
        x, y, c, chips = _mesh_pos()
        q = 2 * x + y
        sib = (x, y, 1 - c)

        def rc(p, k, src, dst, to):
            return pltpu.make_async_remote_copy(src_ref=src, dst_ref=dst, send_sem=ssem.at[p, k],
                                                recv_sem=rsem.at[p, k], device_id=to, device_id_type=MESH)

        def blk(ref, p, qi, half):
            return _rows(ref, (qi,), half, hs[p]) if split[p] else ref.at[qi]

        return x, y, c, chips, q, sib, rc, blk

    def first_sends(ins, outs, sems):
        x, y, c, chips, q, sib, rc, blk = plan(ins, outs, sems)
        cps = [rc(p, 0, ins[p], outs[p].at[q], sib) for p in range(n)]
        for j, (cx, cy) in enumerate(chips):
            for p in range(n):
                src = _rows(ins[p], (), c, hs[p]) if split[p] else ins[p]
                cps.append(rc(p, 1 + j, src, blk(outs[p], p, q, c), (cx, cy, c)))
        return cps

    def start(ins, outs, sems):
        for cp in first_sends(ins, outs, sems):
            cp.start()

    def finish(ins, outs, sems):
        x, y, c, chips, q, sib, rc, blk = plan(ins, outs, sems)
        sent = first_sends(ins, outs, sems)
        for j, (cx, cy) in enumerate(chips):
            qj = 2 * cx + cy
            for p in range(n):
                got = blk(outs[p], p, qj, c)
                rc(p, 1 + j, got, got, (cx, cy, c)).wait_recv()
                if split[p]:
                    fwd = rc(p, 4 + j, got, got, sib)
                    fwd.start()
                    sent.append(fwd)
        for j, (cx, cy) in enumerate(chips):
            qj = 2 * cx + cy
            for p in range(n):
                if split[p]:
                    got = blk(outs[p], p, qj, 1 - c)
                    rc(p, 4 + j, got, got, sib).wait_recv()
        for p in range(n):
            rc(p, 0, outs[p].at[q], outs[p].at[q], sib).wait_recv()
        for cp in sent:
            cp.wait_send()

    return _Side(shards, [SDS((N_SHARDS,) + w.shape, w.dtype) for w in shards],
                 [pltpu.SemaphoreType.DMA((n, 7)), pltpu.SemaphoreType.DMA((n, 7))], start, finish)


def _side_pair_send(gs):
    n = len(gs)

    def copies(ins, outs, sems):
        x, y, c, _ = _mesh_pos()
        return [pltpu.make_async_remote_copy(
            src_ref=ins[p].at[:, pl.ds((1 - c) * (gs[p].shape[1] // 2), gs[p].shape[1] // 2)], dst_ref=outs[p],
            send_sem=sems[0].at[p], recv_sem=sems[1].at[p], device_id=(x, y, 1 - c), device_id_type=MESH)
            for p in range(n)]

    def start(ins, outs, sems):
        for cp in copies(ins, outs, sems):
            cp.start()

    def finish(ins, outs, sems):
        for cp in copies(ins, outs, sems):
            cp.wait()

    return _Side(gs, [SDS((g.shape[0], g.shape[1] // 2, g.shape[2]), g.dtype) for g in gs],
                 [pltpu.SemaphoreType.DMA((n,)), pltpu.SemaphoreType.DMA((n,))], start, finish)


def _side_scatter(ps):
    n = len(ps)

    def sends(ins, outs, sems):
        x, y, c, chips = _mesh_pos()
        q = 2 * x + y
        return [pltpu.make_async_remote_copy(src_ref=ins[p].at[2 * cx + cy], dst_ref=outs[p].at[q],
                                             send_sem=sems[0].at[p, j], recv_sem=sems[1].at[p, j],
                                             device_id=(cx, cy, c), device_id_type=MESH)
                for j, (cx, cy) in enumerate(chips) for p in range(n)]

    def start(ins, outs, sems):
        for cp in sends(ins, outs, sems):
            cp.start()

    def finish(ins, outs, sems):
        x, y, c, chips = _mesh_pos()
        for j, (cx, cy) in enumerate(chips):
            for p in range(n):
                got = outs[p].at[2 * cx + cy]
                pltpu.make_async_remote_copy(src_ref=got, dst_ref=got, send_sem=sems[0].at[p, j],
                                             recv_sem=sems[1].at[p, j], device_id=(cx, cy, c),
                                             device_id_type=MESH).wait_recv()
        for cp in sends(ins, outs, sems):
            cp.wait_send()

    return _Side(ps, [SDS(p.shape, p.dtype) for p in ps],
                 [pltpu.SemaphoreType.DMA((n, 3)), pltpu.SemaphoreType.DMA((n, 3))], start, finish)


def _side_pair_share(tots):
    n = len(tots)

    def copies(ins, outs, sems):
        x, y, c, _ = _mesh_pos()
        return [pltpu.make_async_remote_copy(src_ref=ins[p], dst_ref=outs[p], send_sem=sems[0].at[p],
                                             recv_sem=sems[1].at[p], device_id=(x, y, 1 - c), device_id_type=MESH)
                for p in range(n)]

    def start(ins, outs, sems):
        for cp in copies(ins, outs, sems):
            cp.start()

    def finish(ins, outs, sems):
        for cp in copies(ins, outs, sems):
            cp.wait()

    return _Side(tots, [SDS(t_.shape, t_.dtype) for t_ in tots],
                 [pltpu.SemaphoreType.DMA((n,)), pltpu.SemaphoreType.DMA((n,))], start, finish)


N_DEVICES = 8


def _side_bcast(v):
    def peers():
        x, y, c, _ = _mesh_pos()
        out = []
        for k in range(1, N_DEVICES):
            px, py, pc = x ^ ((k >> 2) & 1), y ^ ((k >> 1) & 1), c ^ (k & 1)
            out.append((k - 1, (px, py, pc), 4 * px + 2 * py + pc))
        return 4 * x + 2 * y + c, out

    def sends(ins, outs, sems):
        me, ps = peers()
        return [pltpu.make_async_remote_copy(src_ref=ins[0], dst_ref=outs[0].at[me], send_sem=sems[0].at[k],
                                             recv_sem=sems[1].at[k], device_id=to, device_id_type=MESH)
                for k, to, _ in ps]

    def start(ins, outs, sems):
        for cp in sends(ins, outs, sems):
            cp.start()

    def finish(ins, outs, sems):
        _, ps = peers()
        for k, to, slot in ps:
            got = outs[0].at[slot]
            pltpu.make_async_remote_copy(src_ref=got, dst_ref=got, send_sem=sems[0].at[k], recv_sem=sems[1].at[k],
                                         device_id=to, device_id_type=MESH).wait_recv()
        for cp in sends(ins, outs, sems):
            cp.wait_send()

    return _Side([v], [SDS((N_DEVICES,) + v.shape, v.dtype)],
                 [pltpu.SemaphoreType.DMA((N_DEVICES - 1,)), pltpu.SemaphoreType.DMA((N_DEVICES - 1,))], start, finish)


def _sum_slots(v, r, me, name, side=None):
    n, rows, lanes = r.shape
    tr = _tile(rows, 512)

    def body(me_ref, v_ref, r_ref, o_ref):
        j = pl.program_id(1)
        term = jnp.where(j == me_ref[0], v_ref[...], r_ref[0])

        @pl.when(j == 0)
        def _():
            o_ref[...] = term

        @pl.when(j != 0)
        def _():
            o_ref[...] += term

    other = lambda j, k: jnp.where(j == k, (k + 1) % n, j)
    return _pcall(
        body, (v, r), name=name, grid=(rows // tr, n), prefetch=(me,),
        in_specs=[pl.BlockSpec((tr, lanes), lambda i, j, me_ref: (i, 0)),
                  pl.BlockSpec((1, tr, lanes), lambda i, j, me_ref: (other(j, me_ref[0]), i, 0))],
        out_specs=pl.BlockSpec((tr, lanes), lambda i, j, me_ref: (i, 0)),
        out_shape=SDS((rows, lanes), F32), sem=("parallel", "arbitrary"), side=side)


def _pair_sum(g, r1, cq, name, side=None):
    nq, xr, yc = g.shape
    h = xr // 2
    tr = _tile(h, 512, 16)
    nt = h // tr

    def body(cq_ref, g_ref, r_ref, o_ref):
        o_ref[...] = (g_ref[...] + r_ref[...]).astype(BF)

    return _pcall(
        body, (g, r1), name=name, grid=(nq, nt), prefetch=(cq,),
        in_specs=[pl.BlockSpec((1, tr, yc), lambda j, i, cq_ref: (j, cq_ref[0] * nt + i, 0)),
                  pl.BlockSpec((1, tr, yc), lambda j, i, cq_ref: (j, i, 0))],
        out_specs=pl.BlockSpec((1, tr, yc), lambda j, i, cq_ref: (j, i, 0)),
        out_shape=SDS((nq, h, yc), BF), sem=("parallel", "parallel"), side=side)


def _chip_sum(p, r2, cq, name, side=None):
    nq, h, yc = r2.shape
    tr = _tile(h, 512, 16)

    def body(cq_ref, p_ref, r_ref, o_ref):
        j = pl.program_id(1)
        term = jnp.where(j == cq_ref[1], p_ref[0], r_ref[0]).astype(F32)

        @pl.when(j == 0)
        def _():
            o_ref[...] = term

        @pl.when(j != 0)
        def _():
            o_ref[...] += term

    other = lambda j, q: jnp.where(j == q, (q + 1) % nq, j)
    return _pcall(
        body, (p, r2), name=name, grid=(h // tr, nq), prefetch=(cq,),
        in_specs=[pl.BlockSpec((1, tr, yc), lambda i, j, cq_ref: (cq_ref[1], i, 0)),
                  pl.BlockSpec((1, tr, yc), lambda i, j, cq_ref: (other(j, cq_ref[1]), i, 0))],
        out_specs=pl.BlockSpec((tr, yc), lambda i, j, cq_ref: (i, 0)),
        out_shape=SDS((h, yc), F32), sem=("parallel", "arbitrary"), side=side)


def _ffn_up_fwd(x, wup, name, side=None, emit_xt=False):
    t, d = x.shape
    w = wup.shape[2]
    tm = _tile(t, 512)

    def body(x_ref, wg_ref, wu_ref, h_ref, a_ref, at_ref, *xt_ref):
        xb = x_ref[...].astype(BF)
        g = _dot(xb, wg_ref[0])
        u = _dot(xb, wu_ref[0])
        h_ref[0] = g.astype(BF)
        h_ref[1] = u.astype(BF)
        ab = (g * _sigmoid(g) * u).astype(BF)
        a_ref[...] = ab
        at_ref[...] = ab.T
        if emit_xt:
            @pl.when(pl.program_id(0) == 0)
            def _():
                xt_ref[0][...] = xb.T

    nt = t // tm
    out_specs = [pl.BlockSpec((2, tm, w), lambda j, i: (0, i, j)), pl.BlockSpec((tm, w), lambda j, i: (i, j)),
                 pl.BlockSpec((w, tm), lambda j, i: (j, i))]
    out_shape = [SDS((2, t, 2 * w), BF), SDS((t, 2 * w), BF), SDS((2 * w, t), BF)]
    if emit_xt:
        out_specs.append(pl.BlockSpec((d, tm), lambda j, i: (0, jnp.where(j == 0, i, nt - 1))))
        out_shape.append(SDS((d, t), BF))
    return _pcall(
        body, (x, wup, wup), name=name, grid=(2, nt),
        in_specs=[pl.BlockSpec((tm, d), lambda j, i: (i, 0)),
                  pl.BlockSpec((1, d, w), lambda j, i: (j, 0, 0)),
                  pl.BlockSpec((1, d, w), lambda j, i: (j + 2, 0, 0))],
        out_specs=out_specs, out_shape=out_shape,
        sem=("arbitrary", "arbitrary") if emit_xt else ("parallel", "parallel"), side=side)


def _res_ln_fwd(parts, w, x, gamma, beta, alpha, res_scale, name, side=None, parts_t=False, target=None):
    t, d = x.shape
    n = len(parts)
    offs = [0]
    for p in parts:
        offs.append(offs[-1] + p.shape[1])
    tm = _tile(t, 512)
    n_in = n + 4 + (target is not None)

    def body(*refs):
        p_refs = refs[:n]
        w_ref, x_ref, g_ref, b_ref = refs[n:n + 4]
        out = refs[n_in:]
        f = _dot(p_refs[0][...], w_ref[offs[0]:offs[1], :])
        for k in range(1, n):
            f = f + _dot(p_refs[k][...], w_ref[offs[k]:offs[k + 1], :])
        z = alpha * x_ref[...] + res_scale * f
        out[0][...] = z
        xhat, _ = _ln_stats(z)
        y = xhat * g_ref[...] + b_ref[...]
        if target is not None:
            err = y - refs[n + 4][...]
            out[1][...] = err * (1.0 / d)
            part = 0.5 * jnp.sum(jnp.sum(err * err, axis=-1, keepdims=True) * (1.0 / d), axis=0, keepdims=True)
            _acc_rows(out[2], jnp.broadcast_to(part, (8, 128)), pl.program_id(0) == 0)
            return
        out[1][...] = y
        out[2][...] = y.astype(BF).T
        for k in range(len(out) - 3):
            out[3 + k][...] = p_refs[k][...].T

    row = lambda i: (i, 0)
    col = lambda i: (0, i)
    fixed = lambda i: (0, 0)
    operands = [*parts, w, x, gamma, beta]
    in_specs = [pl.BlockSpec((tm, p.shape[1]), row) for p in parts] + [
        pl.BlockSpec(w.shape, fixed), pl.BlockSpec((tm, d), row), pl.BlockSpec((1, d), fixed), pl.BlockSpec((1, d), fixed)]
    if target is not None:
        operands.append(target)
        in_specs.append(pl.BlockSpec((tm, d), row))
        out_specs = [pl.BlockSpec((tm, d), row), pl.BlockSpec((tm, d), row), pl.BlockSpec((8, 128), fixed)]
        out_shape = [SDS((t, d), F32), SDS((t, d), F32), SDS((8, 128), F32)]
    else:
        out_specs = [pl.BlockSpec((tm, d), row), pl.BlockSpec((tm, d), row), pl.BlockSpec((d, tm), col)]
        out_shape = [SDS((t, d), F32), SDS((t, d), F32), SDS((d, t), BF)]
        if parts_t:
            out_specs += [pl.BlockSpec((p.shape[1], tm), col) for p in parts]
            out_shape += [SDS((p.shape[1], t), BF) for p in parts]
    return _pcall(
        body, operands, name=name, grid=(t // tm,), in_specs=in_specs, out_specs=out_specs, out_shape=out_shape,
        sem=("arbitrary",) if target is not None else ("parallel",), side=side)


def _mix_proj_fwd(x, w_main, widths, name, side=None):
    t, d = x.shape
    dc, dq, ds = widths
    tm = _tile(t, 512)

    def body(x_ref, w_ref, pc_ref, pq_ref, ps_ref):
        xb = x_ref[...].astype(BF)
        pc_ref[...] = _dot(xb, w_ref[:, 0:dc])
        pq_ref[...] = _dot(xb, w_ref[:, dc:dc + dq]).astype(BF)
        ps_ref[...] = _dot(xb, w_ref[:, dc + dq:dc + dq + ds])

    row = lambda i: (i, 0)
    return _pcall(
        body, (x, w_main), name=name, grid=(t // tm,),
        in_specs=[pl.BlockSpec((tm, d), row), pl.BlockSpec(w_main.shape, lambda i: (0, 0))],
        out_specs=[pl.BlockSpec((tm, dc), row), pl.BlockSpec((tm, dq), row), pl.BlockSpec((tm, ds), row)],
        out_shape=[SDS((t, dc), F32), SDS((t, dq), BF), SDS((t, ds), F32)],
        sem=("parallel",), side=side)


def _prefix_sum_lanes(v, reverse):
    n = v.shape[-1]
    lane = lax.broadcasted_iota(jnp.int32, v.shape, v.ndim - 1)
    sh = 1
    while sh < n:
        if reverse:
            v = v + jnp.where(lane < n - sh, pltpu.roll(v, n - sh, axis=v.ndim - 1), 0.0)
        else:
            v = v + jnp.where(lane >= sh, pltpu.roll(v, sh, axis=v.ndim - 1), 0.0)
        sh *= 2
    return v


def _cum_fwd(x3, wft, bf, name, side=None):
    b, s, d = x3.shape
    h = bf.shape[0]

    def body(x_ref, w_ref, b_ref, fl_ref, cum_ref):
        fl = _dotg(w_ref[...], x_ref[0].astype(BF), NT)[0:h] + b_ref[...]
        fl_ref[0] = fl
        lf = jnp.minimum(fl, 0.0) - jnp.log(1.0 + jnp.exp(-jnp.abs(fl)))
        cum_ref[0] = _prefix_sum_lanes(lf, reverse=False)

    return _pcall(
        body, (x3, wft, bf), name=name, grid=(b,),
        in_specs=[pl.BlockSpec((1, s, d), lambda i: (i, 0, 0)),
                  pl.BlockSpec(wft.shape, lambda i: (0, 0)), pl.BlockSpec((h, 1), lambda i: (0, 0))],
        out_specs=[pl.BlockSpec((1, h, s), lambda i: (i, 0, 0)), pl.BlockSpec((1, h, s), lambda i: (i, 0, 0))],
        out_shape=[SDS((b, h, s), F32), SDS((b, h, s), F32)],
        sem=("parallel",), side=side)


def _shift_rows(z, k, down):
    n = z.shape[0]
    row = lax.broadcasted_iota(jnp.int32, z.shape, 0)
    if down:
        return jnp.where(row >= k, pltpu.roll(z, k, axis=0), 0.0)
    return jnp.where(row < n - k, pltpu.roll(z, n - k, axis=0), 0.0)


def _conv_fwd(pc3, cw, name, side=None):
    b, s, c3 = pc3.shape
    c = c3 // 3

    def body(p_ref, w_ref, y_ref):
        z = p_ref[0, :, c:2 * c] * p_ref[0, :, 2 * c:3 * c]
        conv = w_ref[0:1, :] * _shift_rows(z, 2, True) + w_ref[1:2, :] * _shift_rows(z, 1, True) + w_ref[2:3, :] * z
        y_ref[0] = (p_ref[0, :, 0:c] * conv).astype(BF)

    return _pcall(
        body, (pc3, cw), name=name, grid=(b,),
        in_specs=[pl.BlockSpec((1, s, c3), lambda i: (i, 0, 0)), pl.BlockSpec((3, c), lambda i: (0, 0))],
        out_specs=pl.BlockSpec((1, s, c), lambda i: (i, 0, 0)),
        out_shape=SDS((b, s, c), BF), sem=("parallel",), side=side)


def _head_masks(width):
    lane = lax.broadcasted_iota(jnp.int32, (1, width), 1)
    return [lane < FOX_HEAD_DIM, lane >= FOX_HEAD_DIM]


def _fox_scores(q, k, cum_row, lo, head_mask):
    tq = q.shape[0]
    qm = jnp.where(head_mask, q * (FOX_HEAD_DIM ** -0.5), 0)
    s = _dotg(qm, k, NT) - cum_row
    tri = lax.broadcasted_iota(jnp.int32, (tq, tq), 1) <= lax.broadcasted_iota(jnp.int32, (tq, tq), 0)
    parts = [(jnp.where(tri, s[:, lo:], NEG_BIG), lo, lo + tq)]
    if lo:
        parts.insert(0, (s[:, :lo], 0, lo))
    return parts, qm


def _fox_fwd(pq3, cum4, name, side=None):
    b, s, d3 = pq3.shape
    df = d3 // 3
    hp = df // 128
    tq = _tile(s, FOX_Q_BLOCK)

    def body(q_ref, k_ref, v_ref, c_ref, o_ref, lse_ref):
        masks = _head_masks(128)
        for i in range(s // tq):
            lo, hi = i * tq, (i + 1) * tq
            q = q_ref[0, lo:hi, :]
            k = k_ref[0, 0:hi, :]
            v = v_ref[0, 0:hi, :]
            o = jnp.zeros((tq, 128), F32)
            lse = jnp.zeros((tq, 128), F32)
            for e in range(2):
                parts, _ = _fox_scores(q, k, c_ref[0, 0, e:e + 1, 0:hi], lo, masks[e])
                m = functools.reduce(jnp.maximum, [jnp.max(sc, axis=-1, keepdims=True) for sc, _, _ in parts])
                l, pv = 0.0, 0.0
                for sc, c0, c1 in parts:
                    p = jnp.exp(sc - m)
                    l = l + jnp.sum(p, axis=-1, keepdims=True)
                    pv = pv + _dot(p.astype(BF), v[c0:c1])
                o = jnp.where(masks[e], pv * (1.0 / l), o)
                lse = jnp.where(masks[e], m + jnp.log(l), lse)
            o_ref[0, lo:hi, :] = o.astype(BF)
            lse_ref[0, 0, lo:hi, :] = lse

    blk = lambda off: pl.BlockSpec((1, s, 128), lambda i, j: (i, 0, off + j))
    return _pcall(
        body, (pq3, pq3, pq3, cum4), name=name, grid=(b, hp),
        in_specs=[blk(0), blk(hp), blk(2 * hp), pl.BlockSpec((1, 1, 2, s), lambda i, j: (i, j, 0, 0))],
        out_specs=[blk(0), pl.BlockSpec((1, 1, s, 128), lambda i, j: (i, j, 0, 0))],
        out_shape=[SDS((b, s, df), BF), SDS((b, hp, s, 128), F32)],
        sem=("parallel", "parallel"), side=side)


def _sgu_mix(wm, vnb, bias, gmasks):
    out = bias
    for g in range(len(wm)):
        out = out + jnp.where(gmasks[g], _dot(wm[g], vnb), 0.0)
    return out


def _sgu_consts(ws_ref, bs_ref, ds):
    ng, c, _ = ws_ref.shape
    gd = ds // ng
    tri = lax.broadcasted_iota(jnp.int32, (c, c), 0) >= lax.broadcasted_iota(jnp.int32, (c, c), 1)
    lane = lax.broadcasted_iota(jnp.int32, (1, ds), 1)
    gmasks = [(lane >= g * gd) & (lane < (g + 1) * gd) for g in range(ng)]
    wm = [jnp.where(tri, ws_ref[g], 0.0).astype(BF) for g in range(ng)]
    bias = jnp.zeros((c, ds), F32)
    for g in range(ng):
        bias = jnp.where(gmasks[g], bs_ref[g], bias)
    return tri, gmasks, wm, bias


def _sgu_fwd(ps, lng, lnb, ws, bs, name, side=None):
    t, ds2 = ps.shape
    ds = ds2 // 2
    c = ws.shape[1]
    tm = _tile(t, 512, c)

    def body(p_ref, g_ref, b_ref, ws_ref, bs_ref, y_ref):
        _, gmasks, wm, bias = _sgu_consts(ws_ref, bs_ref, ds)
        up = _gelu(p_ref[:, 0:ds])
        xhat, _ = _ln_stats(_gelu(p_ref[:, ds:ds2]))
        vnb = (xhat * g_ref[...] + b_ref[...]).astype(BF)
        for n in range(tm // c):
            r0, r1 = n * c, (n + 1) * c
            y_ref[r0:r1, :] = (up[r0:r1] * _sgu_mix(wm, vnb[r0:r1], bias, gmasks)).astype(BF)

    fixed2 = lambda i: (0, 0)
    fixed3 = lambda i: (0, 0, 0)
    return _pcall(
        body, (ps, lng, lnb, ws, bs), name=name, grid=(t // tm,),
        in_specs=[pl.BlockSpec((tm, ds2), lambda i: (i, 0)), pl.BlockSpec((1, ds), fixed2),
                  pl.BlockSpec((1, ds), fixed2), pl.BlockSpec(ws.shape, fixed3), pl.BlockSpec(bs.shape, fixed3)],
        out_specs=pl.BlockSpec((tm, ds), lambda i: (i, 0)),
        out_shape=SDS((t, ds), BF), sem=("parallel",), side=side)


def _acc_rows(ref, val, first):
    @pl.when(first)
    def _():
        ref[...] = val

    @pl.when(jnp.logical_not(first))
    def _():
        ref[...] += val


def _ffn_bwd_mid(dy, z, gamma, wd, h, name, side=None):
    t, d = dy.shape
    dff = wd.shape[0]
    half = dff // 2
    tm = _tile(t, 512)

    def body(dy_ref, z_ref, g_ref, wd_ref, h_ref, dz_ref, df_ref, dh_ref, dg_ref, db_ref):
        dz, dgam, dbet = _ln_bwd(dy_ref[...], z_ref[...], g_ref[...])
        first = pl.program_id(0) == 0
        _acc_rows(dg_ref, dgam, first)
        _acc_rows(db_ref, dbet, first)
        dz_ref[...] = dz
        dfb = (0.5 * dz).astype(BF)
        df_ref[...] = dfb
        for j in range(2):
            c0, c1 = j * half, (j + 1) * half
            da = _dotg(dfb, wd_ref[c0:c1, :], NT).astype(BF)
            g = h_ref[0, :, c0:c1]
            u = h_ref[1, :, c0:c1]
            sg = _sigmoid(g)
            dh_ref[0, :, c0:c1] = da * u * sg * (1.0 + g * (1.0 - sg))
            dh_ref[1, :, c0:c1] = da * g * sg

    row = lambda i: (i, 0)
    fixed = lambda i: (0, 0)
    return _pcall(
        body, (dy, z, gamma, wd, h), name=name, grid=(t // tm,),
        in_specs=[pl.BlockSpec((tm, d), row), pl.BlockSpec((tm, d), row), pl.BlockSpec((1, d), fixed),
                  pl.BlockSpec(wd.shape, fixed, pipeline_mode=pl.Buffered(1)),
                  pl.BlockSpec((2, tm, dff), lambda i: (0, i, 0))],
        out_specs=[pl.BlockSpec((tm, d), row), pl.BlockSpec((tm, d), row),
                   pl.BlockSpec((2, tm, dff), lambda i: (0, i, 0)),
                   pl.BlockSpec((1, d), fixed), pl.BlockSpec((1, d), fixed)],
        out_shape=[SDS((t, d), F32), SDS((t, d), BF), SDS((2, t, dff), BF), SDS((1, d), F32), SDS((1, d), F32)],
        sem=("arbitrary",), side=side)


def _ffn_bwd_dx(dh, wup, dz, alpha, name, side=None):
    _, t, dff = dh.shape
    nq, d, w = wup.shape
    per = dff // w
    tm = _tile(t, 512)

    def body(dh_ref, w_ref, dz_ref, dx_ref):
        acc = alpha * dz_ref[...]
        for q in range(nq):
            c0 = (q % per) * w
            acc = acc + _dotg(dh_ref[q // per, :, c0:c0 + w], w_ref[q], NT)
        dx_ref[...] = acc

    row = lambda i: (i, 0)
    return _pcall(
        body, (dh, wup, dz), name=name, grid=(t // tm,),
        in_specs=[pl.BlockSpec((2, tm, dff), lambda i: (0, i, 0)),
                  pl.BlockSpec(wup.shape, lambda i: (0, 0, 0), pipeline_mode=pl.Buffered(1)),
                  pl.BlockSpec((tm, d), row)],
        out_specs=pl.BlockSpec((tm, d), row), out_shape=SDS((t, d), F32),
        sem=("parallel",), side=side)


def _dw(at, b3, ka, nb, name, side=None):
    ka_tot, t = at.shape
    gb, _, nb_tot = b3.shape
    na, ncb = ka_tot // ka, nb_tot // nb
    tm = _tile(t, DW_TOKENS, 128)

    def body(a_ref, b_ref, o_ref):
        part = _dot(a_ref[...], b_ref[0])

        @pl.when(pl.program_id(2) == 0)
        def _():
            o_ref[0, 0] = part

        @pl.when(pl.program_id(2) != 0)
        def _():
            o_ref[0, 0] += part

    return _pcall(
        body, (at, b3), name=name, grid=(na, gb * ncb, t // tm),
        in_specs=[pl.BlockSpec((ka, tm), lambda ja, jb, i: (ja, i)),
                  pl.BlockSpec((1, tm, nb), lambda ja, jb, i: (jb // ncb, i, jb % ncb))],
        out_specs=pl.BlockSpec((1, 1, ka, nb), lambda ja, jb, i: (ja, jb, 0, 0)),
        out_shape=SDS((na, gb * ncb, ka, nb), F32),
        sem=("parallel", "parallel", "arbitrary"), side=side)


def _dw_groups(ats, bs, tokens, name, side=None):
    t = bs[0].shape[0]
    na, nb = len(ats), len(bs)
    roff, coff = [0], [0]
    for a in ats:
        roff.append(roff[-1] + a.shape[0])
    for b in bs:
        coff.append(coff[-1] + b.shape[1])
    tm = _tile(t, tokens, 128)

    def body(*refs):
        a_refs, b_refs, o_ref = refs[:na], refs[na:na + nb], refs[na + nb]
        first = pl.program_id(0) == 0
        for i in range(na):
            for j in range(nb):
                _acc_rows(o_ref.at[roff[i]:roff[i + 1], coff[j]:coff[j + 1]], _dot(a_refs[i][...], b_refs[j][...]), first)

    return _pcall(
        body, (*ats, *bs), name=name, grid=(t // tm,),
        in_specs=[pl.BlockSpec((a.shape[0], tm), lambda i: (0, i)) for a in ats]
        + [pl.BlockSpec((tm, b.shape[1]), lambda i: (i, 0)) for b in bs],
        out_specs=pl.BlockSpec((roff[-1], coff[-1]), lambda i: (0, 0)),
        out_shape=SDS((roff[-1], coff[-1]), F32), sem=("arbitrary",), side=side)


def _out_bwd(dy, z, gamma, wout, widths, name, side=None):
    t, d = dy.shape
    wa, wb, wc = widths
    tm = _tile(t, 512)

    def body(dy_ref, z_ref, g_ref, w_ref, dz_ref, dzb_ref, da_ref, dbb_ref, dc_ref, dg_ref, db_ref):
        dz, dgam, dbet = _ln_bwd(dy_ref[...], z_ref[...], g_ref[...])
        first = pl.program_id(0) == 0
        _acc_rows(dg_ref, dgam, first)
        _acc_rows(db_ref, dbet, first)
        dz_ref[...] = dz
        dzb = dz.astype(BF)
        dzb_ref[...] = dzb
        da_ref[...] = _dotg(dzb, w_ref[0:wa, :], NT).astype(BF)
        dbb_ref[...] = _dotg(dzb, w_ref[wa:wa + wb, :], NT).astype(BF)
        dc_ref[...] = _dotg(dzb, w_ref[wa + wb:wa + wb + wc, :], NT).astype(BF)

    row = lambda i: (i, 0)
    fixed = lambda i: (0, 0)
    return _pcall(
        body, (dy, z, gamma, wout), name=name, grid=(t // tm,),
        in_specs=[pl.BlockSpec((tm, d), row), pl.BlockSpec((tm, d), row), pl.BlockSpec((1, d), fixed),
                  pl.BlockSpec(wout.shape, fixed)],
        out_specs=[pl.BlockSpec((tm, d), row), pl.BlockSpec((tm, d), row), pl.BlockSpec((tm, wa), row),
                   pl.BlockSpec((tm, wb), row), pl.BlockSpec((tm, wc), row),
                   pl.BlockSpec((1, d), fixed), pl.BlockSpec((1, d), fixed)],
        out_shape=[SDS((t, d), F32), SDS((t, d), BF), SDS((t, wa), BF), SDS((t, wb), BF), SDS((t, wc), BF),
                   SDS((1, d), F32), SDS((1, d), F32)],
        sem=("arbitrary",), side=side)


def _conv_bwd(pc3, dya3, cw, name, side=None):
    b, s, c3 = pc3.shape
    c = c3 // 3

    def body(p_ref, dy_ref, w_ref, dp_ref, dw_ref):
        cb = p_ref[0, :, 0:c]
        cc = p_ref[0, :, c:2 * c]
        ch = p_ref[0, :, 2 * c:3 * c]
        z = cc * ch
        z1 = _shift_rows(z, 1, True)
        z2 = _shift_rows(z, 2, True)
        w0, w1, w2 = w_ref[0:1, :], w_ref[1:2, :], w_ref[2:3, :]
        dy = dy_ref[0].astype(F32)
        dconv = dy * cb
        dz = w2 * dconv + w1 * _shift_rows(dconv, 1, False) + w0 * _shift_rows(dconv, 2, False)
        dp_ref[0, :, 0:c] = (dy * (w0 * z2 + w1 * z1 + w2 * z)).astype(BF)
        dp_ref[0, :, c:2 * c] = (dz * ch).astype(BF)
        dp_ref[0, :, 2 * c:3 * c] = (dz * cc).astype(BF)
        first = pl.program_id(0) == 0
        for r, zs in enumerate((z2, z1, z)):
            _acc_rows(dw_ref.at[r:r + 1], jnp.sum(dconv * zs, axis=0, keepdims=True), first)

    blk = lambda i: (i, 0, 0)
    return _pcall(
        body, (pc3, dya3, cw), name=name, grid=(b,),
        in_specs=[pl.BlockSpec((1, s, c3), blk), pl.BlockSpec((1, s, c), blk), pl.BlockSpec((3, c), lambda i: (0, 0))],
        out_specs=[pl.BlockSpec((1, s, c3), blk), pl.BlockSpec((3, c), lambda i: (0, 0))],
        out_shape=[SDS((b, s, c3), BF), SDS((3, c), F32)],
        sem=("arbitrary",), side=side)


def _fox_bwd(pq3, cum4, lse4, dyb3, name, side=None):
    b, s, d3 = pq3.shape
    df = d3 // 3
    hp = df // 128
    tq = _tile(s, FOX_Q_BLOCK)
    scale = FOX_HEAD_DIM ** -0.5

    def body(q_ref, k_ref, v_ref, c_ref, lse_ref, do_ref, dq_ref, dk_ref, dv_ref, dc_ref, dk_acc, dv_acc):
        masks = _head_masks(128)
        dk_acc[...] = jnp.zeros_like(dk_acc)
        dv_acc[...] = jnp.zeros_like(dv_acc)
        dc_ref[...] = jnp.zeros_like(dc_ref)
        for i in range(s // tq):
            lo, hi = i * tq, (i + 1) * tq
            q = q_ref[0, lo:hi, :]
            do = do_ref[0, lo:hi, :]
            k = k_ref[0, 0:hi, :]
            v = v_ref[0, 0:hi, :]
            lse = lse_ref[0, 0, lo:hi, :]
            dq = jnp.zeros((tq, 128), F32)
            for e in range(2):
                dom = jnp.where(masks[e], do, 0)
                parts, qm = _fox_scores(q, k, c_ref[0, 0, e:e + 1, 0:hi], lo, masks[e])
                lse_e = lse[:, FOX_HEAD_DIM * e:FOX_HEAD_DIM * e + 1]
                probs = [jnp.exp(sc - lse_e) for sc, _, _ in parts]
                dps = [_dotg(dom, v[c0:c1], NT) for _, c0, c1 in parts]
                row = functools.reduce(lambda a, c: a + c,
                                       [jnp.sum(p * dp, axis=-1, keepdims=True) for p, dp in zip(probs, dps)])
                dq_e = 0.0
                for p, dp, (_, c0, c1) in zip(probs, dps, parts):
                    ds = p * (dp - row)
                    dsb = ds.astype(BF)
                    dq_e = dq_e + _dot(dsb, k[c0:c1])
                    dk_acc[c0:c1, :] += _dotg(dsb, qm, TN)
                    dv_acc[c0:c1, :] += _dotg(p.astype(BF), dom, TN)
                    dc_ref[0, 0, e:e + 1, c0:c1] -= jnp.sum(ds, axis=0, keepdims=True)
                dq = jnp.where(masks[e], dq_e * scale, dq)
            dq_ref[0, lo:hi, :] = dq.astype(BF)
        dk_ref[0] = dk_acc[...].astype(BF)
        dv_ref[0] = dv_acc[...].astype(BF)

    blk = lambda off: pl.BlockSpec((1, s, 128), lambda i, j: (i, 0, off + j))
    cblk = pl.BlockSpec((1, 1, 2, s), lambda i, j: (i, j, 0, 0))
    return _pcall(
        body, (pq3, pq3, pq3, cum4, lse4, dyb3), name=name, grid=(b, hp),
        in_specs=[blk(0), blk(hp), blk(2 * hp), cblk,
                  pl.BlockSpec((1, 1, s, 128), lambda i, j: (i, j, 0, 0)), blk(0)],
        out_specs=[blk(0), blk(0), blk(0), cblk],
        out_shape=[SDS((b, s, df), BF), SDS((b, s, df), BF), SDS((b, s, df), BF), SDS(cum4.shape, F32)],
        scratch_shapes=[pltpu.VMEM((s, 128), F32), pltpu.VMEM((s, 128), F32)],
        sem=("parallel", "parallel"), side=side)


def _cum_bwd(dcum, flog, xt, name, side=None):
    b, h, s = dcum.shape
    d = xt.shape[0]

    def body(dc_ref, fl_ref, xt_ref, dfl_ref, dbf_ref, dwf_ref):
        dfl = _prefix_sum_lanes(dc_ref[0], reverse=True) * _sigmoid(-fl_ref[0])
        dfl_ref[0] = dfl
        first = pl.program_id(0) == 0
        _acc_rows(dbf_ref, jnp.broadcast_to(jnp.sum(dfl, axis=-1, keepdims=True), (h, 128)), first)
        dflp = jnp.concatenate([dfl, jnp.zeros((HEAD_ROWS - h, s), F32)], axis=0).astype(BF)
        _acc_rows(dwf_ref, _dotg(dflp, xt_ref[...], NT)[0:h], first)

    blk = lambda i: (i, 0, 0)
    return _pcall(
        body, (dcum, flog, xt), name=name, grid=(b,),
        in_specs=[pl.BlockSpec((1, h, s), blk), pl.BlockSpec((1, h, s), blk), pl.BlockSpec((d, s), lambda i: (0, i))],
        out_specs=[pl.BlockSpec((1, h, s), blk), pl.BlockSpec((h, 128), lambda i: (0, 0)),
                   pl.BlockSpec((h, d), lambda i: (0, 0))],
        out_shape=[SDS((b, h, s), F32), SDS((h, 128), F32), SDS((h, d), F32)],
        sem=("arbitrary",), side=side)


def _sgu_bwd(ps, dyc, lng, lnb, ws, bs, name, side=None):
    t, ds2 = ps.shape
    ds = ds2 // 2
    ng, c, _ = ws.shape
    tm = _tile(t, 512, c)

    def body(p_ref, dy_ref, g_ref, b_ref, ws_ref, bs_ref, dp_ref, dws_ref, dbs_ref, dg_ref, db_ref, dvn_acc):
        tri, gmasks, wm, bias = _sgu_consts(ws_ref, bs_ref, ds)
        su = p_ref[:, 0:ds]
        sv = p_ref[:, ds:ds2]
        up = _gelu(su)
        gv = _gelu(sv)
        xhat, rstd = _ln_stats(gv)
        vnb = (xhat * g_ref[...] + b_ref[...]).astype(BF)
        dy = dy_ref[...].astype(F32)
        dws = [jnp.zeros((c, c), F32) for _ in range(ng)]
        dbs = [jnp.zeros((c, 1), F32) for _ in range(ng)]
        for n in range(tm // c):
            r0, r1 = n * c, (n + 1) * c
            mixed = _sgu_mix(wm, vnb[r0:r1], bias, gmasks)
            dp_ref[r0:r1, 0:ds] = (dy[r0:r1] * mixed * _gelu_grad(su[r0:r1])).astype(BF)
            dmix = dy[r0:r1] * up[r0:r1]
            dvn = jnp.zeros((c, ds), F32)
            for g in range(ng):
                dmg = jnp.where(gmasks[g], dmix, 0.0)
                dmb = dmg.astype(BF)
                dws[g] = dws[g] + _dotg(dmb, vnb[r0:r1], NT)
                dbs[g] = dbs[g] + jnp.sum(dmg, axis=-1, keepdims=True)
                dvn = dvn + _dotg(wm[g], dmb, TN)
            dvn_acc[r0:r1, :] = dvn
        dvn_all = dvn_acc[...]
        gdv = dvn_all * g_ref[...]
        m1 = jnp.mean(gdv, axis=-1, keepdims=True)
        m2 = jnp.mean(gdv * xhat, axis=-1, keepdims=True)
        dgv = rstd * (gdv - m1 - xhat * m2)
        dp_ref[:, ds:ds2] = (dgv * _gelu_grad(sv)).astype(BF)
        first = pl.program_id(0) == 0
        _acc_rows(dg_ref, jnp.sum(dvn_all * xhat, axis=0, keepdims=True), first)
        _acc_rows(db_ref, jnp.sum(dvn_all, axis=0, keepdims=True), first)
        for g in range(ng):
            _acc_rows(dws_ref.at[g], jnp.where(tri, dws[g], 0.0), first)
            _acc_rows(dbs_ref.at[g], dbs[g], first)

    row = lambda i: (i, 0)
    fixed2 = lambda i: (0, 0)
    fixed3 = lambda i: (0, 0, 0)
    return _pcall(
        body, (ps, dyc, lng, lnb, ws, bs), name=name, grid=(t // tm,),
        in_specs=[pl.BlockSpec((tm, ds2), row), pl.BlockSpec((tm, ds), row), pl.BlockSpec((1, ds), fixed2),
                  pl.BlockSpec((1, ds), fixed2), pl.BlockSpec(ws.shape, fixed3), pl.BlockSpec(bs.shape, fixed3)],
        out_specs=[pl.BlockSpec((tm, ds2), row), pl.BlockSpec(ws.shape, fixed3), pl.BlockSpec(bs.shape, fixed3),
                   pl.BlockSpec((1, ds), fixed2), pl.BlockSpec((1, ds), fixed2)],
        out_shape=[SDS((t, ds2), BF), SDS(ws.shape, F32), SDS(bs.shape, F32), SDS((1, ds), F32), SDS((1, ds), F32)],
        scratch_shapes=[pltpu.VMEM((tm, ds), F32)],
        sem=("arbitrary",), side=side)


def _mix_bwd_dx(dz, dconv, dq, dk, dv, dsgu, dflog, w_main, wft, seq, alpha, name, side=None):
    t, d = dz.shape
    groups = [dconv, dq, dk, dv, dsgu]
    offs = [0]
    for g in groups:
        offs.append(offs[-1] + g.shape[1])
    h = dflog.shape[1]
    tm = _tile(seq, 512)
    per_seq = seq // tm

    def body(dz_ref, a0, a1, a2, a3, a4, dfl_ref, w_ref, wf_ref, dx_ref):
        dflp = jnp.concatenate([dfl_ref[0], jnp.zeros((HEAD_ROWS - h, tm), F32)], axis=0).astype(BF)
        acc = alpha * dz_ref[...] + _dotg(dflp, wf_ref[...], TN)
        for k, a_ref in enumerate((a0, a1, a2, a3, a4)):
            acc = acc + _dotg(a_ref[...], w_ref[:, offs[k]:offs[k + 1]], NT)
        dx_ref[...] = acc

    row = lambda i: (i, 0)
    return _pcall(
        body, (dz, *groups, dflog, w_main, wft), name=name, grid=(t // tm,),
        in_specs=[pl.BlockSpec((tm, d), row)] + [pl.BlockSpec((tm, g.shape[1]), row) for g in groups]
        + [pl.BlockSpec((1, h, tm), lambda i: (i // per_seq, 0, i % per_seq)),
           pl.BlockSpec(w_main.shape, lambda i: (0, 0)), pl.BlockSpec(wft.shape, lambda i: (0, 0))],
        out_specs=pl.BlockSpec((tm, d), row), out_shape=SDS((t, d), F32),
        sem=("parallel",), side=side)


def _adam_math(w, g, m, v):
    c1 = 1.0 / (1.0 - ADAM_B1 ** ADAM_STEP)
    c2 = 1.0 / (1.0 - ADAM_B2 ** ADAM_STEP)
    nm = ADAM_B1 * m + (1.0 - ADAM_B1) * g
    nv = ADAM_B2 * v + (1.0 - ADAM_B2) * (g * g)
    delta = -ADAM_LR * ((nm * c1) / (jnp.sqrt(nv * c2) + ADAM_EPS) + ADAM_WD * w)
    return delta, nm, nv


def _adamw_small(w, g, m, v, name):
    r, c = w.shape
    tr = _tile(r, 512)

    def body(w_ref, g_ref, m_ref, v_ref, d_ref, nm_ref, nv_ref):
        d_ref[...], nm_ref[...], nv_ref[...] = _adam_math(w_ref[...], g_ref[...], m_ref[...], v_ref[...])

    blk = pl.BlockSpec((tr, c), lambda i: (i, 0))
    return _pcall(body, (w, g, m, v), name=name, grid=(r // tr,), in_specs=[blk] * 4, out_specs=[blk] * 3,
                  out_shape=[SDS((r, c), F32)] * 3, sem=("parallel",))


def _adamw_shard(w, m, v, tot, recv, cq, layer, prev, name, side=None):
    nl, xr, yc = w.shape
    h = xr // 2
    tr = _tile(h, 256)
    nt = h // tr

    def body(cq_ref, w_ref, m_ref, v_ref, t_ref, r_ref, *rest):
        g_ref, d_ref, nm_ref, nv_ref = rest[-4:]
        g = jnp.where(pl.program_id(0) == cq_ref[0], t_ref[...], r_ref[...])
        g_ref[0] = g
        d_ref[0], nm_ref[0], nv_ref[0] = _adam_math(w_ref[0], g, m_ref[0], v_ref[0])

    slab = pl.BlockSpec((1, tr, yc), lambda hf, i, cq_ref: (layer, hf * nt + i, 0))
    mine = pl.BlockSpec((tr, yc), lambda hf, i, cq_ref: (jnp.where(hf == cq_ref[0], i, 0), 0))
    theirs = pl.BlockSpec((tr, yc), lambda hf, i, cq_ref: (jnp.where(hf == cq_ref[0], 0, i), 0))
    operands = [w, m, v, tot, recv]
    in_specs = [slab, slab, slab, mine, theirs]
    aliases = None
    if prev is not None:
        operands += list(prev)
        in_specs += [HBM] * 4
        aliases = {6 + k: k for k in range(4)}
    return _pcall(body, operands, name=name, grid=(2, nt), prefetch=(cq,), in_specs=in_specs,
                  out_specs=[slab] * 4, out_shape=[SDS(w.shape, F32)] * 4, aliases=aliases,
                  sem=("parallel", "parallel"), side=side)


BIG = ("ffn1_w_up", "ffn1_w_down", "mix_w_in", "mix_w_out", "ffn2_w_up", "ffn2_w_down")
SMALL = ("ln1_g", "ln1_b", "fox_b_f", "sgu_ln_g", "sgu_ln_b", "sgu_w_s", "sgu_b_s", "ln2_g", "ln2_b", "ln3_g", "ln3_b")
ORDER = ("ln1_g", "ln1_b", "ffn1_w_up", "ffn1_w_down", "mix_w_in", "fox_b_f", "conv_w", "sgu_ln_g", "sgu_ln_b",
         "sgu_w_s", "sgu_b_s", "mix_w_out", "ln2_g", "ln2_b", "ffn2_w_up", "ffn2_w_down", "ln3_g", "ln3_b")


def _row(v):
    return v.reshape(1, -1)


class _Pipe:
    def __init__(self, stages):
        self.stages = list(stages)
        self.pos = 0
        self.last = None

    def kind(self):
        return self.stages[self.pos][0] if self.pos < len(self.stages) else None


class _Sched:
    def __init__(self):
        self.pipes = []
        self.n_alone = 0

    def add(self, stages):
        self.pipes.append(_Pipe(stages))

    def _take_comms(self, skip=None):
        jobs = []
        for p in self.pipes:
            if p is not skip and p.kind() == "comm":
                jobs.append((p, p.stages[p.pos][1]()))
        return jobs

    @staticmethod
    def _landed(jobs):
        for p, side in jobs:
            p.last = side.results
            p.pos += 1

    def carry(self, builder, *args, **kw):
        jobs = self._take_comms()
        res = builder(*args, side=_join([s for _, s in jobs]), **kw)
        self._landed(jobs)
        self._computes(ride=False)
        return res

    def _computes(self, ride):
        again = True
        while again:
            again = False
            for p in self.pipes:
                if p.kind() == "compute":
                    jobs = self._take_comms(skip=p) if ride else []
                    p.stages[p.pos][1](p.last, _join([s for _, s in jobs]))
                    p.pos += 1
                    self._landed(jobs)
                    again = True

    def drain(self):
        while any(p.kind() is not None for p in self.pipes):
            self._computes(ride=True)
            jobs = self._take_comms()
            if jobs:
                _run_side(_join([s for _, s in jobs]), "exchange_tail_%d" % self.n_alone)
                self.n_alone += 1
                self._landed(jobs)


def _forward_layer(x, xt, p, dims, alpha, l, ride, target=None):
    b, s = dims["b"], dims["s"]
    t, d = x.shape
    tag = "l%d_" % l

    def run(stage, builder, *args, **kw):
        side, on_done = ride.get(stage, (None, None))
        res = builder(*args, tag + stage, side=side, **kw)
        if on_done is not None:
            on_done()
        return res

    if xt is None:
        h1, a1, a1t, xt = run("ffn1_up", _ffn_up_fwd, x, p["wup1"], emit_xt=True)
    else:
        h1, a1, a1t = run("ffn1_up", _ffn_up_fwd, x, p["wup1"])
    z1, x1, x1t = run("ffn1_down", _res_ln_fwd, [a1], p["wd1"], x, p["ln1_g"], p["ln1_b"], alpha, 0.5)
    pc, pq, ps = run("mix_proj", _mix_proj_fwd, x1, p["win"], dims["proj_widths"])
    x1_3 = x1.reshape(b, s, d)
    flog, cum = run("fox_gate", _cum_fwd, x1_3, p["wft"], p["bf"])
    nh = flog.shape[1]
    cum4 = cum.reshape(b, nh // 2, 2, s)
    pc3 = pc.reshape(b, s, -1)
    pq3 = pq.reshape(b, s, -1)
    ya = run("conv", _conv_fwd, pc3, p["cw"]).reshape(t, -1)
    yb3, lse4 = run("fox", _fox_fwd, pq3, cum4)
    yb = yb3.reshape(t, -1)
    yc = run("sgu", _sgu_fwd, ps, p["sgu_g"], p["sgu_b"], p["ws"], p["bs"])
    z2, x2, x2t, yat, ybt, yct = run("mix_out", _res_ln_fwd, [ya, yb, yc], p["wout"], x1, p["ln2_g"], p["ln2_b"],
                                      alpha, 1.0, parts_t=True)
    h2, a2, a2t = run("ffn2_up", _ffn_up_fwd, x2, p["wup2"])
    z3, x3, x3t = run("ffn2_down", _res_ln_fwd, [a2], p["wd2"], x2, p["ln3_g"], p["ln3_b"], alpha, 0.5, target=target)
    saved = dict(xt=xt, h1=h1, a1t=a1t, z1=z1, x1=x1, x1t=x1t, pc3=pc3, pq3=pq3, ps=ps, flog=flog, cum4=cum4,
                 lse4=lse4, yat=yat, ybt=ybt, yct=yct, z2=z2, x2t=x2t, h2=h2, a2t=a2t, z3=z3)
    return x3, x3t, saved


def _ffn_backward(sched, emit, which, dy, z, gamma, wd, wup, h, a_t, x_in_t, alpha, tag, after_mid=None):
    dz, df, dh, dgam, dbet = sched.carry(_ffn_bwd_mid, dy, z, gamma, wd, h, tag + "_bwd_mid")
    if after_mid is not None:
        after_mid(dgam, dbet)
    nq, d, w = wup.shape
    emit(which + "_w_up", sched.carry(_dw, x_in_t, dh, d, w, tag + "_dw_up")[0])
    half = wd.shape[0] // 2
    emit(which + "_w_down", sched.carry(_dw, a_t, df[None], half, d, tag + "_dw_down").reshape(nq, -1, d))
    dx = sched.carry(_ffn_bwd_dx, dh, wup, dz, alpha, tag + "_bwd_dx")
    return dx, dgam, dbet


def _backward_layer(sched, emit, emit_small, dy, sv, p, dims, alpha, l):
    b, s = dims["b"], dims["s"]
    tag = "l%d_" % l
    t, d = dy.shape
    g = {}
    dx2, g["ln3_g"], g["ln3_b"] = _ffn_backward(sched, emit, "ffn2", dy, sv["z3"], p["ln3_g"], p["wd2"], p["wup2"],
                                                sv["h2"], sv["a2t"], sv["x2t"], alpha, tag + "ffn2")
    wa, wb, wc = sv["yat"].shape[0], sv["ybt"].shape[0], sv["yct"].shape[0]
    dz2, dz2b, dya, dyb, dyc, g["ln2_g"], g["ln2_b"] = sched.carry(
        _out_bwd, dx2, sv["z2"], p["ln2_g"], p["wout"], (wa, wb, wc), tag + "mix_out_bwd")
    emit("mix_w_out", sched.carry(_dw_groups, [sv["yat"], sv["ybt"], sv["yct"]], [dz2b], DW_TOKENS,
                                  tag + "dw_out").reshape(N_SHARDS, -1, d))
    dpc3, g["conv_w"] = sched.carry(_conv_bwd, sv["pc3"], dya.reshape(b, s, -1), p["cw"], tag + "conv_bwd")
    dq3, dk3, dv3, dcum4 = sched.carry(_fox_bwd, sv["pq3"], sv["cum4"], sv["lse4"], dyb.reshape(b, s, -1),
                                       tag + "fox_bwd")
    nh = sv["flog"].shape[1]
    dflog, dbf, dwft = sched.carry(_cum_bwd, dcum4.reshape(b, nh, s), sv["flog"], sv["x1t"], tag + "fox_gate_bwd")
    g["fox_b_f"] = dbf[:, 0]
    dps, g["sgu_w_s"], dbs, g["sgu_ln_g"], g["sgu_ln_b"] = sched.carry(
        _sgu_bwd, sv["ps"], dyc, p["sgu_g"], p["sgu_b"], p["ws"], p["bs"], tag + "sgu_bwd")
    g["sgu_b_s"] = dbs[:, :, 0]
    dpc = dpc3.reshape(t, -1)
    dq, dk, dv = dq3.reshape(t, -1), dk3.reshape(t, -1), dv3.reshape(t, -1)
    main = sched.carry(_dw_groups, [sv["x1t"]], [dpc, dq, dk, dv, dps], DW_TOKENS // 2, tag + "dw_in")
    n_main = main.shape[1] - dps.shape[1]
    w_in_grad = jnp.concatenate([main[:, :n_main], dwft.T, main[:, n_main:]], axis=1)
    emit("mix_w_in", jnp.moveaxis(w_in_grad.reshape(d, N_SHARDS, -1), 1, 0))
    dx1 = sched.carry(_mix_bwd_dx, dz2, dpc, dq, dk, dv, dps, dflog, p["win"], p["wft"], s, alpha, tag + "mix_bwd_dx")

    def small_ready(dgam, dbet):
        g["ln1_g"], g["ln1_b"] = dgam, dbet
        emit_small(g)

    dx0, _, _ = _ffn_backward(sched, emit, "ffn1", dx1, sv["z1"], p["ln1_g"], p["wd1"], p["wup1"],
                              sv["h1"], sv["a1t"], sv["xt"], alpha, tag + "ffn1", after_mid=small_ready)
    return dx0


def _span(n):
    return -(-n // 1024) * 1024


def _pack_rows(flat_list):
    return jnp.concatenate([jnp.pad(v, (0, _span(v.shape[0]) - v.shape[0])) for v in flat_list]).reshape(-1, 128)


def kernel(x, ln1_g, ln1_b, ffn1_w_up, ffn1_w_down, mix_w_in, fox_b_f, conv_w, sgu_ln_g, sgu_ln_b, sgu_w_s, sgu_b_s, mix_w_out, ln2_g, ln2_b, ffn2_w_up, ffn2_w_down, ln3_g, ln3_b, loss_target, m_ln1_g, m_ln1_b, m_ffn1_w_up, m_ffn1_w_down, m_mix_w_in, m_fox_b_f, m_conv_w, m_sgu_ln_g, m_sgu_ln_b, m_sgu_w_s, m_sgu_b_s, m_mix_w_out, m_ln2_g, m_ln2_b, m_ffn2_w_up, m_ffn2_w_down, m_ln3_g, m_ln3_b, v_ln1_g, v_ln1_b, v_ffn1_w_up, v_ffn1_w_down, v_mix_w_in, v_fox_b_f, v_conv_w, v_sgu_ln_g, v_sgu_ln_b, v_sgu_w_s, v_sgu_b_s, v_mix_w_out, v_ln2_g, v_ln2_b, v_ffn2_w_up, v_ffn2_w_down, v_ln3_g, v_ln3_b):
    wts = dict(ln1_g=ln1_g, ln1_b=ln1_b, ffn1_w_up=ffn1_w_up, ffn1_w_down=ffn1_w_down, mix_w_in=mix_w_in,
               fox_b_f=fox_b_f, conv_w=conv_w, sgu_ln_g=sgu_ln_g, sgu_ln_b=sgu_ln_b, sgu_w_s=sgu_w_s,
               sgu_b_s=sgu_b_s, mix_w_out=mix_w_out, ln2_g=ln2_g, ln2_b=ln2_b, ffn2_w_up=ffn2_w_up,
               ffn2_w_down=ffn2_w_down, ln3_g=ln3_g, ln3_b=ln3_b)
    mom = dict(ln1_g=m_ln1_g, ln1_b=m_ln1_b, ffn1_w_up=m_ffn1_w_up, ffn1_w_down=m_ffn1_w_down, mix_w_in=m_mix_w_in,
               fox_b_f=m_fox_b_f, conv_w=m_conv_w, sgu_ln_g=m_sgu_ln_g, sgu_ln_b=m_sgu_ln_b, sgu_w_s=m_sgu_w_s,
               sgu_b_s=m_sgu_b_s, mix_w_out=m_mix_w_out, ln2_g=m_ln2_g, ln2_b=m_ln2_b, ffn2_w_up=m_ffn2_w_up,
               ffn2_w_down=m_ffn2_w_down, ln3_g=m_ln3_g, ln3_b=m_ln3_b)
    var = dict(ln1_g=v_ln1_g, ln1_b=v_ln1_b, ffn1_w_up=v_ffn1_w_up, ffn1_w_down=v_ffn1_w_down, mix_w_in=v_mix_w_in,
               fox_b_f=v_fox_b_f, conv_w=v_conv_w, sgu_ln_g=v_sgu_ln_g, sgu_ln_b=v_sgu_ln_b, sgu_w_s=v_sgu_w_s,
               sgu_b_s=v_sgu_b_s, mix_w_out=v_mix_w_out, ln2_g=v_ln2_g, ln2_b=v_ln2_b, ffn2_w_up=v_ffn2_w_up,
               ffn2_w_down=v_ffn2_w_down, ln3_g=v_ln3_g, ln3_b=v_ln3_b)

    nl = ln1_g.shape[0]
    b, s, d = x.shape
    t = b * s
    alpha = (2 * nl) ** 0.25
    cw_sh = conv_w.shape[2]
    d_conv = cw_sh * N_SHARDS
    d_sgu = sgu_ln_g.shape[1]
    nh = fox_b_f.shape[1]
    d_fox = nh * FOX_HEAD_DIM
    n_main = 3 * d_conv + 3 * d_fox
    dims = dict(b=b, s=s, proj_widths=(3 * d_conv, 3 * d_fox, 2 * d_sgu))
    cpos = lax.axis_index("c").astype(jnp.int32)
    qpos = (2 * lax.axis_index("x") + lax.axis_index("y")).astype(jnp.int32)
    cq = jnp.stack([cpos, qpos])

    me = (2 * qpos + cpos).reshape(1)
    assert nl == 2, "the gather schedule below names the carriers of a two-layer step"

    conv_tile = jnp.pad(conv_w, ((0, 0), (0, 8 - conv_w.shape[1]), (0, 128 - cw_sh)))
    params = [dict(bf=fox_b_f[l].reshape(nh, 1), sgu_g=_row(sgu_ln_g[l]), sgu_b=_row(sgu_ln_b[l]), ws=sgu_w_s[l],
                   bs=sgu_b_s[l][:, :, None], ln1_g=_row(ln1_g[l]), ln1_b=_row(ln1_b[l]), ln2_g=_row(ln2_g[l]),
                   ln2_b=_row(ln2_b[l]), ln3_g=_row(ln3_g[l]), ln3_b=_row(ln3_b[l])) for l in range(nl)]

    def operands_of(k, arr):
        if k == "mix_w_in":
            w_in = jnp.moveaxis(arr, 0, 1).reshape(d, -1)
            return dict(win=jnp.concatenate([w_in[:, :n_main], w_in[:, n_main + nh:]], axis=1),
                        wft=jnp.pad(w_in[:, n_main:n_main + nh].T, ((0, HEAD_ROWS - nh), (0, 0))))
        if k == "conv_w":
            return dict(cw=jnp.moveaxis(arr[:, :3, :cw_sh], 0, 1).reshape(3, d_conv))
        if k in ("ffn1_w_up", "ffn2_w_up"):
            return {"wup" + k[3]: arr}
        return {dict(ffn1_w_down="wd1", ffn2_w_down="wd2", mix_w_out="wout")[k]: arr.reshape(-1, d)}

    def gather(l, keys):
        side = _side_gather([conv_tile[l] if k == "conv_w" else wts[k][l].astype(BF) for k in keys],
                            [k != "conv_w" for k in keys])

        def install():
            for k, arr in zip(keys, side.results):
                params[l].update(operands_of(k, arr))
        return side, install

    first, install_first = gather(0, ["ffn1_w_up"])
    _run_side(first, "gather_first")
    install_first()
    def both(a, b):
        return _join([a[0], b[0]]), lambda: (a[1](), b[1]())

    rides = [{"ffn1_up": gather(0, ["ffn1_w_down", "mix_w_in", "mix_w_out", "conv_w"]),
              "ffn1_down": gather(0, ["ffn2_w_down"]),
              "fox": both(gather(0, ["ffn2_w_up"]), gather(1, ["ffn1_w_up"])),
              "mix_out": gather(1, ["ffn1_w_down"]),
              "ffn2_up": gather(1, ["mix_w_in", "mix_w_out", "conv_w", "ffn2_w_up"]),
              "ffn2_down": gather(1, ["ffn2_w_down"])}, {}]

    act, act_t = x.reshape(t, d), None
    saved = []
    for l in range(nl):
        act, act_t, sv = _forward_layer(act, act_t, params[l], dims, alpha, l, rides[l],
                                        target=loss_target.reshape(t, d) if l == nl - 1 else None)
        saved.append(sv)
    dy, loss_blk = act, act_t

    sched = _Sched()
    prev = {k: None for k in BIG}
    red = {}

    def emit_for(l):
        def emit(key, g):
            st = {}
            name = "l%d_%s" % (l, key)

            def pair_sum(res, side):
                st["p"] = _pair_sum(g, res[0], cq, "rs_pair_sum_" + name, side=side)

            def chip_sum(res, side):
                st["t"] = _chip_sum(st["p"], res[0], cq, "rs_chip_sum_" + name, side=side)

            def adamw(res, side):
                prev[key] = _adamw_shard(wts[key], mom[key], var[key], st["t"], res[0], cq, l, prev[key],
                                         "adamw_" + name, side=side)

            sched.add([("comm", lambda: _side_pair_send([g])), ("compute", pair_sum),
                       ("comm", lambda: _side_scatter([st["p"]])), ("compute", chip_sum),
                       ("comm", lambda: _side_pair_share([st["t"]])), ("compute", adamw)])
        return emit

    def emit_small_for(l):
        def emit_small(g):
            flat = [g[k].reshape(-1) for k in SMALL] + [g["conv_w"].reshape(-1)]
            if l == nl - 1:
                flat.append(loss_blk[0, 0:1])
            vec = _pack_rows(flat)

            def slot_sum(res, side):
                red[l] = _sum_slots(vec, res[0], me, "small_sum_l%d" % l, side=side)

            sched.add([("comm", lambda: _side_bcast(vec)), ("compute", slot_sum)])
        return emit_small

    for l in reversed(range(nl)):
        dy = _backward_layer(sched, emit_for(l), emit_small_for(l), dy, saved[l], params[l], dims, alpha, l)
    sched.drain()
    grad_x = dy.reshape(b, s, d)
    gfin, delta, new_m, new_v = {}, {}, {}, {}
    for k in BIG:
        gfin[k], delta[k], new_m[k], new_v[k] = prev[k]

    gsm = {k: [] for k in SMALL + ("conv_w",)}
    for l in range(nl):
        flat_l = red[l].reshape(-1)
        off = 0
        for k in SMALL:
            n = wts[k][l].size
            gsm[k].append(flat_l[off:off + n].reshape(wts[k][l].shape))
            off += _span(n)
        n = 3 * d_conv
        gsm["conv_w"].append(lax.dynamic_slice_in_dim(flat_l[off:off + n].reshape(3, d_conv), qpos * cw_sh, cw_sh,
                                                      axis=1))
        off += _span(n)
        if l == nl - 1:
            loss = flat_l[off]
    for k in gsm:
        gfin[k] = jnp.stack(gsm[k])
    small_keys = SMALL + ("conv_w",)
    sizes = [wts[k].size for k in small_keys]
    pk = lambda src: _pack_rows([src[k].reshape(-1) for k in small_keys])
    dl, nm, nv = _adamw_small(pk(wts), pk(gfin), pk(mom), pk(var), "adamw_small")
    off = 0
    for k, n in zip(small_keys, sizes):
        shp = wts[k].shape
        delta[k] = dl.reshape(-1)[off:off + n].reshape(shp)
        new_m[k] = nm.reshape(-1)[off:off + n].reshape(shp)
        new_v[k] = nv.reshape(-1)[off:off + n].reshape(shp)
        off += _span(n)

    return (loss, grad_x, *[gfin[k] for k in ORDER], *[delta[k] for k in ORDER],
            *[new_m[k] for k in ORDER], *[new_v[k] for k in ORDER])
```
